```python
import math
import jax, jax.numpy as jnp
from jax import lax
import numpy as np

D_MODEL = 1024
BATCH = 4
SEQ = 4096
DEPTH = 2
DEC_BATCH = 32
DEC_SEQ = 4
PAST_LEN = 8192
PAGE_SIZE = 128

HEAD_DIM = 64
N_ATT_HEADS = 8
ATT_DIM = N_ATT_HEADS * HEAD_DIM
CONV_DIM = D_MODEL - ATT_DIM
CONV_WIDTH = 3
MIX_IN = 3 * ATT_DIM + 3 * CONV_DIM
SPLITS = (ATT_DIM, 2 * ATT_DIM, 3 * ATT_DIM, 3 * ATT_DIM + CONV_DIM, 3 * ATT_DIM + 2 * CONV_DIM)
DILATED_PATTERNS = ((128, 1), (512, 4), (2048, 16))
ATT_WINDOW_MAX = 2048
Q_BLOCK = 128
N_BUCKETS = 32
MAX_DISTANCE = 2048
N_GROUPS = 4
EXPERTS_PER_GROUP = 8
N_EXPERTS = N_GROUPS * EXPERTS_PER_GROUP
TOP_K = 2
D_EXPERT = 256
D_PLE = 256
EPS = 1e-6
NEG = -1e30

kernel_name = "hybrid_dilated_attn_shortconv_hmoe_step"


def rms_norm(x, gain):
    xf = x.astype(jnp.float32)
    y = xf * lax.rsqrt(jnp.mean(xf * xf, axis=-1, keepdims=True) + EPS)
    return (y * gain.astype(jnp.float32)).astype(x.dtype)


def t5_bucket(dist):
    max_exact = N_BUCKETS // 2
    d_f = jnp.maximum(dist, 1).astype(jnp.float32)
    large = max_exact + (jnp.log(d_f / max_exact) / math.log(MAX_DISTANCE / max_exact)
                         * (N_BUCKETS - max_exact)).astype(jnp.int32)
    large = jnp.minimum(large, N_BUCKETS - 1)
    return jnp.where(dist < max_exact, dist, large)


def pattern_bias(rel_bias, dilation, n_k):
    dist = dilation * jnp.arange(n_k + 1, dtype=jnp.int32)
    return rel_bias[t5_bucket(dist)]


def band_pattern_stats(q, k, v, rel_bias, window, dilation):
    B, S, H, Dh = q.shape
    n_k = window // dilation
    M = S // dilation
    nb = -(-M // Q_BLOCK)
    Mp = nb * Q_BLOCK

    def split(t):
        return jnp.swapaxes(t.reshape(B, M, dilation, H, Dh), 1, 2)

    qs, ks, vs = split(q), split(k), split(v)
    qb = jnp.pad(qs, ((0, 0), (0, 0), (0, Mp - M), (0, 0), (0, 0))).reshape(B, dilation, nb, Q_BLOCK, H, Dh)

    def band(t):
        tp = jnp.pad(t, ((0, 0), (0, 0), (Q_BLOCK, Mp - M), (0, 0), (0, 0)))
        prev = tp[:, :, :Mp].reshape(B, dilation, nb, Q_BLOCK, H, Dh)
        cur = tp[:, :, Q_BLOCK:].reshape(B, dilation, nb, Q_BLOCK, H, Dh)
        return jnp.concatenate([prev, cur], axis=3)

    kb, vb = band(ks), band(vs)
    i = jnp.arange(Q_BLOCK)[:, None]
    j = jnp.arange(2 * Q_BLOCK)[None, :]
    off = i - j + Q_BLOCK
    key_idx = jnp.arange(nb)[:, None, None] * Q_BLOCK + j[None] - Q_BLOCK
    valid = (off >= 0) & (off <= n_k) & (key_idx >= 0)
    bias = pattern_bias(rel_bias, dilation, n_k)[jnp.clip(off, 0, n_k)].transpose(2, 0, 1)

    s = jnp.einsum('bdnqhc,bdnkhc->bdnhqk', qb, kb, preferred_element_type=jnp.float32)
    s = s * (HEAD_DIM ** -0.5) + bias.astype(jnp.float32)
    s = jnp.where(valid[None, None, :, None], s, NEG)
    mx = jnp.max(s, axis=-1, keepdims=True)
    e = jnp.exp(s - mx)
    den = jnp.sum(e, axis=-1)
    num = jnp.einsum('bdnhqk,bdnkhc->bdnqhc', e, vb.astype(jnp.float32))

    def merge(t):
        rest = t.shape[4:]
        t = t.reshape((B, dilation, Mp) + rest)[:, :, :M]
        return jnp.swapaxes(t, 1, 2).reshape((B, S) + rest)

    mx = jnp.swapaxes(mx[..., 0], 3, 4)
    den = jnp.swapaxes(den, 3, 4)
    return merge(num), merge(mx), merge(den)


def dilated_attention_prompt(q, k, v, rel_bias):
    stats = [band_pattern_stats(q, k, v, rel_bias, w, d) for (w, d) in DILATED_PATTERNS]
    nums = jnp.stack([st[0] for st in stats])
    mxs = jnp.stack([st[1] for st in stats])
    dens = jnp.stack([st[2] for st in stats])
    wts = jnp.exp(mxs - jnp.max(mxs, axis=0))
    num = jnp.einsum('pbsh,pbshc->bshc', wts, nums)
    den = jnp.sum(wts * dens, axis=0)
    return num / den[..., None]


def dilated_attention_sample(q, k_all, v_all, rel_bias):
    T = q.shape[1]
    L = k_all.shape[1]
    q_idx = L - T + jnp.arange(T, dtype=jnp.int32)
    idxs, biases = [], []
    for (w, d) in DILATED_PATTERNS:
        n_k = w // d
        dist = d * jnp.arange(n_k + 1, dtype=jnp.int32)
        idxs.append(q_idx[:, None] - dist[None, :])
        biases.append(pattern_bias(rel_bias, d, n_k))
    idx = jnp.concatenate(idxs, axis=1)
    bias = jnp.concatenate(biases, axis=0)
    valid = idx >= 0
    idx = jnp.clip(idx, 0, L - 1)
    kg = k_all[:, idx]
    vg = v_all[:, idx]
    s = jnp.einsum('bthc,btkhc->bthk', q, kg, preferred_element_type=jnp.float32)
    s = s * (HEAD_DIM ** -0.5) + bias.T.astype(jnp.float32)[None, None]
    s = jnp.where(valid[None, :, None, :], s, NEG)
    p = jax.nn.softmax(s, axis=-1)
    return jnp.einsum('bthk,btkhc->bthc', p, vg.astype(jnp.float32))


def token_mixers(a, rel_bias, w_in, q_gain, k_gain, conv_w, g_out_att, g_out_conv, w_out,
                 k_past, v_past, conv_past):
    B, T, _ = a.shape
    proj = jnp.einsum('btd,dm->btm', a, w_in)
    q, k, v, hc, gb, gc = jnp.split(proj, SPLITS, axis=-1)
    q = rms_norm(q.reshape(B, T, N_ATT_HEADS, HEAD_DIM), q_gain)
    k = rms_norm(k.reshape(B, T, N_ATT_HEADS, HEAD_DIM), k_gain)
    v = v.reshape(B, T, N_ATT_HEADS, HEAD_DIM)
    if k_past is None:
        att = dilated_attention_prompt(q, k, v, rel_bias)
        keep = min(ATT_WINDOW_MAX, T)
        new_k, new_v = k[:, T - keep:], v[:, T - keep:]
        u_prev = jnp.zeros((B, CONV_WIDTH - 1, CONV_DIM), a.dtype)
    else:
        att = dilated_attention_sample(q, jnp.concatenate([k_past.astype(k.dtype), k], axis=1),
                                       jnp.concatenate([v_past.astype(v.dtype), v], axis=1), rel_bias)
        new_k, new_v = k, v
        u_prev = conv_past.astype(a.dtype)
    u = gc * hc
    u_ext = jnp.concatenate([u_prev, u], axis=1)
    conv = conv_w[0] * u_ext[:, 0:T]
    for j in range(1, CONV_WIDTH):
        conv = conv + conv_w[j] * u_ext[:, j:j + T]
    y_conv = gb * conv
    new_conv = u_ext[:, -(CONV_WIDTH - 1):]
    merged = jnp.concatenate([rms_norm(att.astype(a.dtype).reshape(B, T, ATT_DIM), g_out_att),
                              rms_norm(y_conv, g_out_conv)], axis=-1)
    return jnp.einsum('btm,md->btd', merged, w_out), new_k, new_v, new_conv


def hier_moe(m, w_router_group, w_router_expert, w_gate, w_up, w_down):
    B, T, _ = m.shape
    g_logits = jnp.einsum('btd,dg->btg', m, w_router_group, preferred_element_type=jnp.float32)
    g_prob = jax.nn.softmax(g_logits, axis=-1)
    g_w, g_sel = lax.top_k(g_prob, 1)
    e_logits = jnp.einsum('btd,de->bte', m, w_router_expert, preferred_element_type=jnp.float32)
    e_logits = e_logits.reshape(B, T, N_GROUPS, EXPERTS_PER_GROUP)
    e_logits = jnp.take_along_axis(e_logits, g_sel[..., None], axis=2)[:, :, 0]
    top_v, top_i = lax.top_k(e_logits, TOP_K)
    w_k = jax.nn.softmax(top_v, axis=-1) * g_w
    expert_id = g_sel * EXPERTS_PER_GROUP + top_i
    gates = jnp.einsum('btk,btke->bte', w_k, jax.nn.one_hot(expert_id, N_EXPERTS, dtype=jnp.float32))
    hg = jnp.einsum('btd,edf->btef', m, w_gate)
    hu = jnp.einsum('btd,edf->btef', m, w_up)
    hidden = jax.nn.silu(hg) * hu * gates[..., None].astype(m.dtype)
    return jnp.einsum('btef,efd->btd', hidden, w_down)


def decoder_layer(h, p_i, k_past, v_past, conv_past, g_mix, g_ffn, mix_w, ffn_w, ple_w):
    mixed, new_k, new_v, new_conv = token_mixers(rms_norm(h, g_mix), *mix_w, k_past, v_past, conv_past)
    h = h + mixed
    h = h + hier_moe(rms_norm(h, g_ffn), *ffn_w)
    g_ple, w_ple_gate, w_ple_proj = ple_w
    gate = jax.nn.sigmoid(jnp.einsum('btd,de->bte', rms_norm(h, g_ple), w_ple_gate))
    h = h + jnp.einsum('btp,pd->btd', p_i, w_ple_proj) * gate
    return h, new_k, new_v, new_conv


def setup_inputs(seed: int = 0) -> dict:
    key = jax.random.key(seed)
    ks = jax.random.split(key, 26)
    f32 = jnp.float32
    w_buf = min(ATT_WINDOW_MAX, PAST_LEN)

    def nrm(k, shape, scale):
        return scale * jax.random.normal(k, shape, f32)

    def gain(k, shape):
        return 1.0 + 0.1 * jax.random.normal(k, shape, f32)

    return {
        "x_prompt": nrm(ks[0], (BATCH, SEQ, D_MODEL), 1.0),
        "x_sample": nrm(ks[1], (DEC_BATCH, DEC_SEQ, D_MODEL), 1.0),
        "cache_k": nrm(ks[2], (DEPTH, DEC_BATCH, w_buf, N_ATT_HEADS, HEAD_DIM), 1.0),
        "cache_v": nrm(ks[3], (DEPTH, DEC_BATCH, w_buf, N_ATT_HEADS, HEAD_DIM), 1.0),
        "state_conv": nrm(ks[4], (DEPTH, DEC_BATCH, CONV_WIDTH - 1, CONV_DIM), 1.0),
        "p_prompt": nrm(ks[5], (DEPTH, BATCH, SEQ, D_PLE), 1.0),
        "p_sample": nrm(ks[6], (DEPTH, DEC_BATCH, DEC_SEQ, D_PLE), 1.0),
        "rel_bias": nrm(ks[7], (N_BUCKETS, N_ATT_HEADS), 0.5),
        "g_mix": gain(ks[8], (DEPTH, D_MODEL)),
        "w_in": nrm(ks[9], (DEPTH, D_MODEL, MIX_IN), D_MODEL ** -0.5),
        "q_gain": gain(ks[10], (DEPTH, HEAD_DIM)),
        "k_gain": gain(ks[11], (DEPTH, HEAD_DIM)),
        "conv_w": nrm(ks[12], (DEPTH, CONV_WIDTH, CONV_DIM), CONV_WIDTH ** -0.5),
        "g_out_att": gain(ks[13], (DEPTH, ATT_DIM)),
        "g_out_conv": gain(ks[14], (DEPTH, CONV_DIM)),
        "w_out": nrm(ks[15], (DEPTH, D_MODEL, D_MODEL), D_MODEL ** -0.5),
        "g_ffn": gain(ks[16], (DEPTH, D_MODEL)),
        "w_router_group": nrm(ks[17], (DEPTH, D_MODEL, N_GROUPS), D_MODEL ** -0.5),
        "w_router_expert": nrm(ks[18], (DEPTH, D_MODEL, N_EXPERTS), D_MODEL ** -0.5),
        "w_gate": nrm(ks[19], (DEPTH, N_EXPERTS, D_MODEL, D_EXPERT), D_MODEL ** -0.5),
        "w_up": nrm(ks[20], (DEPTH, N_EXPERTS, D_MODEL, D_EXPERT), D_MODEL ** -0.5),
        "w_down": nrm(ks[21], (DEPTH, N_EXPERTS, D_EXPERT, D_MODEL), D_EXPERT ** -0.5),
        "g_ple": gain(ks[22], (DEPTH, D_MODEL)),
        "w_ple_gate": nrm(ks[23], (DEPTH, D_MODEL, D_MODEL), D_MODEL ** -0.5),
        "w_ple_proj": nrm(ks[24], (DEPTH, D_PLE, D_MODEL), D_PLE ** -0.5),
    }


def reference(x_prompt, x_sample, cache_k, cache_v, state_conv, p_prompt, p_sample,
              rel_bias, g_mix, w_in, q_gain, k_gain, conv_w, g_out_att, g_out_conv, w_out,
              g_ffn, w_router_group, w_router_expert, w_gate, w_up, w_down,
              g_ple, w_ple_gate, w_ple_proj):
    hp, hs = x_prompt, x_sample
    kp_l, vp_l, cp_l, ks_l, vs_l, cs_l = [], [], [], [], [], []
    for i in range(DEPTH):
        mix_w = (rel_bias, w_in[i], q_gain[i], k_gain[i], conv_w[i], g_out_att[i], g_out_conv[i], w_out[i])
        ffn_w = (w_router_group[i], w_router_expert[i], w_gate[i], w_up[i], w_down[i])
        ple_w = (g_ple[i], w_ple_gate[i], w_ple_proj[i])
        hp, kp, vp, cp = decoder_layer(hp, p_prompt[i], None, None, None, g_mix[i], g_ffn[i],
                                       mix_w, ffn_w, ple_w)
        hs, ksn, vsn, csn = decoder_layer(hs, p_sample[i], cache_k[i], cache_v[i], state_conv[i],
                                          g_mix[i], g_ffn[i], mix_w, ffn_w, ple_w)
        kp_l.append(kp); vp_l.append(vp); cp_l.append(cp)
        ks_l.append(ksn); vs_l.append(vsn); cs_l.append(csn)
    new_k_prompt = jnp.stack(kp_l)
    new_v_prompt = jnp.stack(vp_l)
    new_conv_prompt = jnp.stack(cp_l)
    new_k_sample = jnp.stack(ks_l)
    new_v_sample = jnp.stack(vs_l)
    new_conv_sample = jnp.stack(cs_l)
    return (hp, hs, new_k_prompt, new_v_prompt, new_conv_prompt, new_k_sample, new_v_sample, new_conv_sample)
```

```python
import functools

import jax
import jax.numpy as jnp
import numpy as np
from jax import lax
from jax.experimental import pallas as pl
from jax.experimental.pallas import tpu as pltpu

F32 = jnp.float32
BF16 = jnp.bfloat16

D_MODEL = 1024
HEAD_DIM = 64
N_HEADS = 8
ATT_DIM = N_HEADS * HEAD_DIM
CONV_DIM = D_MODEL - ATT_DIM
MIX_IN = 3 * ATT_DIM + 3 * CONV_DIM
PATTERNS = ((128, 1), (512, 4), (2048, 16))
N_KEYS = 128
Q_BLOCK = 128
N_BUCKETS = 32
MAX_DISTANCE = 2048
N_GROUPS = 4
EXPERTS_PER_GROUP = 8
N_EXPERTS = N_GROUPS * EXPERTS_PER_GROUP
D_EXPERT = 256
D_PLE = 256
EPS = 1e-6
NEG = -1e30

LANES = 128
SUBLANES = 8
LSE_LANES_PER_HEAD = LANES // N_HEADS
VMEM_LIMIT = 48 * 1024 * 1024


def _cparams(n_axes):
    return pltpu.CompilerParams(dimension_semantics=("arbitrary",) * n_axes,
                                vmem_limit_bytes=VMEM_LIMIT)


def _full(shape):
    n = len(shape)
    return pl.BlockSpec(shape, lambda *_: (0,) * n)


def _rms(x, gain):
    ms = jnp.mean(x * x, axis=-1, keepdims=True)
    return x * lax.rsqrt(ms + EPS) * gain


def _inproj_kernel(h_ref, gmix_ref, w_ref, qg_ref, kg_ref, bd_ref, cw_ref, gconv_ref, hist_ref,
                   q_ref, k_ref, v_ref, yn_ref, nconv_ref, carry_ref, *, shift, tiles_per_seq):
    i = pl.program_id(0)
    a = _rms(h_ref[...], gmix_ref[...])
    proj = jnp.dot(a.astype(BF16), w_ref[...], preferred_element_type=F32)
    tm = proj.shape[0]

    def head_norm(t, g):
        ms = jnp.dot((t * t).astype(BF16), bd_ref[...], preferred_element_type=F32)
        return t * lax.rsqrt(ms + EPS) * g

    q_ref[...] = head_norm(proj[:, 0:ATT_DIM], qg_ref[...])
    k_ref[...] = head_norm(proj[:, ATT_DIM:2 * ATT_DIM], kg_ref[...])
    v_ref[...] = proj[:, 2 * ATT_DIM:3 * ATT_DIM]
    c0 = 3 * ATT_DIM
    hc = proj[:, c0:c0 + CONV_DIM]
    gb = proj[:, c0 + CONV_DIM:c0 + 2 * CONV_DIM]
    gc = proj[:, c0 + 2 * CONV_DIM:c0 + 3 * CONV_DIM]
    u = gc * hc

    if shift == 1:
        @pl.when(i % tiles_per_seq == 0)
        def _():
            carry_ref[...] = hist_ref[0]
        h0 = carry_ref[SUBLANES - 2:SUBLANES - 1, :]
        h1 = carry_ref[SUBLANES - 1:SUBLANES, :]
        row = lax.broadcasted_iota(jnp.int32, (tm, 1), 0)
        u1 = jnp.where(row == 0, h1, pltpu.roll(u, 1, 0))
        u2 = jnp.where(row == 0, h0, jnp.where(row == 1, h1, pltpu.roll(u, 2, 0)))
        carry_ref[...] = u[tm - SUBLANES:tm, :]
        nconv_ref[0] = u[tm - SUBLANES:tm, :]
    else:
        hist = hist_ref[0]
        u1 = jnp.concatenate([hist[shift:2 * shift], u[0:tm - shift]], axis=0)
        u2 = jnp.concatenate([hist, u[0:tm - 2 * shift]], axis=0)
        nconv_ref[0] = u[tm - 2 * shift:tm, :]
    conv = cw_ref[0:1, :] * u2 + cw_ref[1:2, :] * u1 + cw_ref[2:3, :] * u
    yn_ref[...] = _rms(gb * conv, gconv_ref[...])


def _inproj(h, hist, g_mix, w_in_bf, q_gain, k_gain, bd, conv_w, g_out_conv, *, tm, shift, tiles_per_seq):
    n = h.shape[0]
    hist_rows = hist.shape[1]
    nseq = hist.shape[0]
    tok = lambda w: pl.BlockSpec((tm, w), lambda i: (i, 0))
    seq3 = lambda r: pl.BlockSpec((1, r, CONV_DIM), lambda i: (i // tiles_per_seq, 0, 0))
    nconv_rows = SUBLANES if shift == 1 else 2 * shift
    out_shape = [jax.ShapeDtypeStruct((n, ATT_DIM), F32)] * 3 + [
        jax.ShapeDtypeStruct((n, CONV_DIM), F32),
        jax.ShapeDtypeStruct((nseq, nconv_rows, CONV_DIM), F32)]
    return pl.pallas_call(
        functools.partial(_inproj_kernel, shift=shift, tiles_per_seq=tiles_per_seq),
        grid=(n // tm,),
        in_specs=[tok(D_MODEL), _full((1, D_MODEL)), _full((D_MODEL, MIX_IN)), _full((1, ATT_DIM)),
                  _full((1, ATT_DIM)), _full((ATT_DIM, ATT_DIM)), _full((3, CONV_DIM)),
                  _full((1, CONV_DIM)), seq3(hist_rows)],
        out_specs=[tok(ATT_DIM), tok(ATT_DIM), tok(ATT_DIM), tok(CONV_DIM), seq3(nconv_rows)],
        out_shape=out_shape,
        scratch_shapes=[pltpu.VMEM((SUBLANES, CONV_DIM), F32)],
        compiler_params=_cparams(1),
        name="inproj",
    )(h, g_mix, w_in_bf, q_gain, k_gain, bd, conv_w, g_out_conv, hist)


def _head_blocks(q, kl, kr, vl, vr, bias_l, bias_r, pen):
    nq = q.shape[0]
    lane = lax.broadcasted_iota(jnp.int32, (nq, LANES), 1)
    upper = lane >= HEAD_DIM
    lse_grp = lane // LSE_LANES_PER_HEAD
    lse_tile = jnp.zeros((nq, LANES), F32)
    outs = []
    nt = (((1,), (1,)), ((), ()))
    for j in range(N_HEADS // 2):
        sl = slice(j * LANES, (j + 1) * LANES)
        qp, klp, krp, vlp, vrp = q[:, sl], kl[:, sl], kr[:, sl], vl[:, sl], vr[:, sl]
        o_pair = None
        for e in range(2):
            h = 2 * j + e
            qm = (jnp.where(upper, qp, 0.0) if e else jnp.where(upper, 0.0, qp)).astype(BF16)
            s_l = lax.dot_general(qm, klp, nt, preferred_element_type=F32) + (bias_l[h] + pen)
            s_r = lax.dot_general(qm, krp, nt, preferred_element_type=F32) + bias_r[h]
            m = jnp.maximum(jnp.max(s_l, axis=-1, keepdims=True), jnp.max(s_r, axis=-1, keepdims=True))
            p_l = jnp.exp(s_l - m)
            p_r = jnp.exp(s_r - m)
            den = jnp.sum(p_l, axis=-1, keepdims=True) + jnp.sum(p_r, axis=-1, keepdims=True)
            o = (jnp.dot(p_l.astype(BF16), vlp, preferred_element_type=F32)
                 + jnp.dot(p_r.astype(BF16), vrp, preferred_element_type=F32)) / den
            o_pair = o if e == 0 else jnp.where(upper, o, o_pair)
            lse_tile = jnp.where(lse_grp == h, m + jnp.log(den), lse_tile)
        outs.append(o_pair)
    return jnp.concatenate(outs, axis=-1), lse_tile


def _attn_kernel(q_ref, kp_ref, kc_ref, vp_ref, vc_ref, bl_ref, br_ref, o_ref, lse_ref, kbuf, vbuf, *, sub):
    n = pl.program_id(2)
    kbuf[0:Q_BLOCK, :] = kp_ref[0].astype(BF16)
    kbuf[Q_BLOCK:, :] = kc_ref[0].astype(BF16)
    vbuf[0:Q_BLOCK, :] = vp_ref[0].astype(BF16)
    vbuf[Q_BLOCK:, :] = vc_ref[0].astype(BF16)

    def body(j, carry):
        r0 = pl.multiple_of(j * Q_BLOCK, Q_BLOCK)
        r1 = pl.multiple_of(j * Q_BLOCK + Q_BLOCK, Q_BLOCK)
        q = q_ref[0, pl.ds(r0, Q_BLOCK), :] * (HEAD_DIM ** -0.5)
        pen = jnp.where(jnp.logical_and(n == 0, j == 0), NEG, 0.0).astype(F32)
        o, lse = _head_blocks(q, kbuf[pl.ds(r0, Q_BLOCK), :], kbuf[pl.ds(r1, Q_BLOCK), :],
                              vbuf[pl.ds(r0, Q_BLOCK), :], vbuf[pl.ds(r1, Q_BLOCK), :],
                              bl_ref, br_ref, pen)
        o_ref[0, pl.ds(r0, Q_BLOCK), :] = o
        lse_ref[0, pl.ds(r0, Q_BLOCK), :] = lse
        return carry

    lax.fori_loop(0, sub, body, 0)


def _attn_pattern(q, k, v, bias_l, bias_r, *, batch, seq, dil, sub):
    m = seq // dil
    qb = sub * Q_BLOCK
    view = lambda t, c: t.reshape(batch, m, dil * c)
    cur = lambda c, rows: pl.BlockSpec((1, rows, c), lambda b, r, n: (b, n, r))
    prev = pl.BlockSpec((1, Q_BLOCK, ATT_DIM), lambda b, r, n: (b, jnp.maximum(n * sub - 1, 0), r))
    o, lse = pl.pallas_call(
        functools.partial(_attn_kernel, sub=sub),
        grid=(batch, dil, m // qb),
        in_specs=[cur(ATT_DIM, qb), prev, cur(ATT_DIM, qb), prev, cur(ATT_DIM, qb),
                  _full((N_HEADS, Q_BLOCK, Q_BLOCK)), _full((N_HEADS, Q_BLOCK, Q_BLOCK))],
        out_specs=[cur(ATT_DIM, qb), cur(LANES, qb)],
        out_shape=[jax.ShapeDtypeStruct((batch, m, dil * ATT_DIM), F32),
                   jax.ShapeDtypeStruct((batch, m, dil * LANES), F32)],
        scratch_shapes=[pltpu.VMEM((Q_BLOCK + qb, ATT_DIM), BF16), pltpu.VMEM((Q_BLOCK + qb, ATT_DIM), BF16)],
        compiler_params=_cparams(3),
        name=f"attn_d{dil}",
    )(view(q, ATT_DIM), view(k, ATT_DIM), view(k, ATT_DIM), view(v, ATT_DIM), view(v, ATT_DIM), bias_l, bias_r)
    return o.reshape(batch * seq, ATT_DIM), lse.reshape(batch * seq, LANES)


SAMPLE_T = 4
SAMPLE_RES = 4
TAIL = 512
KEY_COLS = SAMPLE_RES * N_KEYS + TAIL + LANES


def _attn_sample_kernel(q_ref, ka_ref, kb_ref, kn_ref, va_ref, vb_ref, vn_ref, bias_ref, mult_ref, o_ref):
    rows = SAMPLE_T * N_HEADS
    q4 = q_ref[0] * (HEAD_DIM ** -0.5)
    qt = jnp.concatenate([jnp.broadcast_to(q4[t:t + 1, :], (N_HEADS, ATT_DIM)) for t in range(SAMPLE_T)], axis=0)
    lane_head = lax.broadcasted_iota(jnp.int32, (rows, ATT_DIM), 1) // HEAD_DIM
    row_head = lax.broadcasted_iota(jnp.int32, (rows, ATT_DIM), 0) % N_HEADS
    own = lane_head == row_head
    qbd = jnp.where(own, qt, 0.0).astype(BF16)
    nt = (((1,), (1,)), ((), ()))
    pad = jnp.zeros((LANES - SUBLANES, ATT_DIM), F32)
    new_rows = lambda ref: jnp.concatenate([ref[0], pad], axis=0).astype(BF16)
    keys = [ka_ref[0, 0, :, r * ATT_DIM:(r + 1) * ATT_DIM].astype(BF16) for r in range(SAMPLE_RES)]
    keys += [kb_ref[0, 0, 0].astype(BF16), new_rows(kn_ref)]
    vals = [va_ref[0, 0, :, r * ATT_DIM:(r + 1) * ATT_DIM].astype(BF16) for r in range(SAMPLE_RES)]
    vals += [vb_ref[0, 0, 0].astype(BF16), new_rows(vn_ref)]
    s = jnp.concatenate([lax.dot_general(qbd, kk, nt, preferred_element_type=F32) for kk in keys], axis=-1)
    s = s + bias_ref[...]
    m = jnp.max(s, axis=-1, keepdims=True)
    p = jnp.exp(s - m) * mult_ref[...]
    den = jnp.sum(p, axis=-1, keepdims=True)
    pb = p.astype(BF16)
    acc = jnp.zeros((rows, ATT_DIM), F32)
    c = 0
    for vv in vals:
        w = vv.shape[0]
        acc = acc + jnp.dot(pb[:, c:c + w], vv, preferred_element_type=F32)
        c += w
    acc = jnp.where(own, acc / den, 0.0)
    for t in range(SAMPLE_T):
        o_ref[0, t:t + 1, :] = jnp.sum(acc[t * N_HEADS:(t + 1) * N_HEADS, :], axis=0, keepdims=True)


def _attn_sample(q, k, v, cache_k, cache_v, layer, bias, mult):
    nb = q.shape[0]
    w_buf = cache_k.shape[2]
    groups = w_buf // 16
    res_view = lambda c: c.reshape(c.shape[0], nb, groups, 16 * ATT_DIM)
    tail_view = lambda c: c.reshape(c.shape[0], nb, w_buf // TAIL, TAIL, ATT_DIM)
    new = pl.BlockSpec((1, SAMPLE_T, ATT_DIM), lambda b: (b, 0, 0))
    new8 = pl.BlockSpec((1, SUBLANES, ATT_DIM), lambda b: (b, 0, 0))
    res = pl.BlockSpec((1, 1, groups, SAMPLE_RES * ATT_DIM), lambda b: (layer, b, 0, 0))
    tail = pl.BlockSpec((1, 1, 1, TAIL, ATT_DIM), lambda b: (layer, b, w_buf // TAIL - 1, 0, 0))
    tbl = _full((SAMPLE_T * N_HEADS, KEY_COLS))
    return pl.pallas_call(
        _attn_sample_kernel,
        grid=(nb,),
        in_specs=[new, res, tail, new8, res, tail, new8, tbl, tbl],
        out_specs=new,
        out_shape=jax.ShapeDtypeStruct((nb, SAMPLE_T, ATT_DIM), F32),
        compiler_params=_cparams(1),
        name="attn_sample",
    )(q, res_view(cache_k), tail_view(cache_k), k, res_view(cache_v), tail_view(cache_v), v, bias, mult)


def _split_dot(x, e_ref):
    hi = x.astype(BF16)
    lo = (x - hi.astype(F32)).astype(BF16)
    return (jnp.dot(hi, e_ref[...], preferred_element_type=F32)
            + jnp.dot(lo, e_ref[...], preferred_element_type=F32))


def _outproj_kernel(*refs, n_pat):
    n_lse = n_pat if n_pat > 1 else 0
    o_refs = refs[0:n_pat]
    l_refs = refs[n_pat:n_pat + n_lse]
    yn_ref, h_ref, gatt_ref, exp_ref, wa_ref, wc_ref, out_ref = refs[n_pat + n_lse:]
    if n_pat == 1:
        att = o_refs[0][...]
    else:
        lses = [r[...] for r in l_refs]
        top = functools.reduce(jnp.maximum, lses)
        ws = [jnp.exp(l - top) for l in lses]
        tot = functools.reduce(lambda a, b: a + b, ws)
        att = None
        for w, o_ref in zip(ws, o_refs):
            term = _split_dot(w / tot, exp_ref) * o_ref[...]
            att = term if att is None else att + term
    att_n = _rms(att, gatt_ref[...])
    y = (jnp.dot(att_n.astype(BF16), wa_ref[...], preferred_element_type=F32)
         + jnp.dot(yn_ref[...].astype(BF16), wc_ref[...], preferred_element_type=F32))
    out_ref[...] = h_ref[...] + y


def _outproj(os, lses, yn, h, g_att, expand, wa_bf, wc_bf, *, tm):
    n = h.shape[0]
    n_pat = len(os)
    tok = lambda w: pl.BlockSpec((tm, w), lambda i: (i, 0))
    return pl.pallas_call(
        functools.partial(_outproj_kernel, n_pat=n_pat),
        grid=(n // tm,),
        in_specs=[tok(ATT_DIM)] * n_pat + [tok(LANES)] * len(lses) + [
            tok(CONV_DIM), tok(D_MODEL), _full((1, ATT_DIM)), _full((LANES, ATT_DIM)),
            _full((ATT_DIM, D_MODEL)), _full((CONV_DIM, D_MODEL))],
        out_specs=tok(D_MODEL),
        out_shape=jax.ShapeDtypeStruct((n, D_MODEL), F32),
        compiler_params=_cparams(1),
        name="outproj",
    )(*os, *lses, yn, h, g_att, expand, wa_bf, wc_bf)


ROUTE_I1, ROUTE_I2, ROUTE_R1, ROUTE_R2, ROUTE_W1, ROUTE_W2 = range(6)
GROUP_LANE0 = N_EXPERTS


def _route_kernel(h_ref, g_ref, wr_ref, route_ref, cnt_ref, carry_ref):
    i = pl.program_id(0)

    @pl.when(i == 0)
    def _():
        carry_ref[...] = jnp.zeros_like(carry_ref)

    m = _rms(h_ref[...], g_ref[...])
    logits = jnp.dot(m, wr_ref[...], preferred_element_type=F32, precision=lax.Precision.HIGHEST)
    tm = logits.shape[0]
    lane_i = lax.broadcasted_iota(jnp.int32, (tm, LANES), 1)
    lane = lane_i.astype(F32)
    big = jnp.float32(4 * LANES)

    is_g = jnp.logical_and(lane_i >= GROUP_LANE0, lane_i < GROUP_LANE0 + N_GROUPS)
    gl = jnp.where(is_g, logits, NEG)
    gmax = jnp.max(gl, axis=-1, keepdims=True)
    g_w = 1.0 / jnp.sum(jnp.where(is_g, jnp.exp(gl - gmax), 0.0), axis=-1, keepdims=True)
    g_sel = jnp.min(jnp.where(gl == gmax, lane - GROUP_LANE0, big), axis=-1, keepdims=True)

    grp_of_lane = (lane_i // EXPERTS_PER_GROUP).astype(F32)
    in_grp = jnp.logical_and(lane_i < N_EXPERTS, grp_of_lane == g_sel)
    el = jnp.where(in_grp, logits, NEG)
    t1 = jnp.max(el, axis=-1, keepdims=True)
    i1 = jnp.min(jnp.where(el == t1, lane, big), axis=-1, keepdims=True)
    el2 = jnp.where(lane == i1, NEG, el)
    t2 = jnp.max(el2, axis=-1, keepdims=True)
    i2 = jnp.min(jnp.where(el2 == t2, lane, big), axis=-1, keepdims=True)
    e2 = jnp.exp(t2 - t1)
    w1 = g_w / (1.0 + e2)
    w2 = g_w * e2 / (1.0 + e2)

    hit1 = lane == i1
    hit2 = lane == i2
    c = jnp.where(jnp.logical_or(hit1, hit2), 1.0, 0.0)
    rr = lax.broadcasted_iota(jnp.int32, (tm, tm), 0)
    cc = lax.broadcasted_iota(jnp.int32, (tm, tm), 1)
    lower = jnp.where(rr > cc, 1.0, 0.0).astype(BF16)
    before = jnp.dot(lower, c.astype(BF16), preferred_element_type=F32) + carry_ref[0:1, :]
    r1 = jnp.sum(jnp.where(hit1, before, 0.0), axis=-1, keepdims=True)
    r2 = jnp.sum(jnp.where(hit2, before, 0.0), axis=-1, keepdims=True)
    total = carry_ref[0:1, :] + jnp.sum(c, axis=0, keepdims=True)
    carry_ref[...] = jnp.broadcast_to(total, carry_ref.shape)
    cnt_ref[...] = jnp.broadcast_to(total, cnt_ref.shape)

    rec = jnp.zeros((tm, LANES), F32)
    for idx, val in ((ROUTE_I1, i1), (ROUTE_I2, i2), (ROUTE_R1, r1), (ROUTE_R2, r2), (ROUTE_W1, w1), (ROUTE_W2, w2)):
        rec = jnp.where(lane_i == idx, val, rec)
    route_ref[...] = rec


def _route(h, g_ffn, w_router, *, tm):
    n = h.shape[0]
    tok = lambda w: pl.BlockSpec((tm, w), lambda i: (i, 0))
    return pl.pallas_call(
        _route_kernel,
        grid=(n // tm,),
        in_specs=[tok(D_MODEL), _full((1, D_MODEL)), _full((D_MODEL, LANES))],
        out_specs=[tok(LANES), _full((SUBLANES, LANES))],
        out_shape=[jax.ShapeDtypeStruct((n, LANES), F32), jax.ShapeDtypeStruct((SUBLANES, LANES), F32)],
        scratch_shapes=[pltpu.VMEM((SUBLANES, LANES), F32)],
        compiler_params=_cparams(1),
        name="route",
    )(h, g_ffn, w_router)


def _dispatch_kernel(pos_ref, h_ref, g_ref, xs_ref, mbuf, sem):
    tm = mbuf.shape[0]
    mbuf[...] = _rms(h_ref[...], g_ref[...])

    def row_copy(t, p):
        return pltpu.make_async_copy(mbuf.at[pl.ds(t, 1), :], xs_ref.at[pl.ds(p, 1), :], sem)

    def issue(t, carry):
        row_copy(t, pos_ref[0, 0, 2 * t]).start()
        row_copy(t, pos_ref[0, 0, 2 * t + 1]).start()
        return carry

    lax.fori_loop(0, tm, issue, 0)

    def drain(t, carry):
        row_copy(0, 0).wait()
        row_copy(0, 0).wait()
        return carry

    lax.fori_loop(0, tm, drain, 0)


def _dispatch(h, g_ffn, pos, n_rows, *, tm):
    n = h.shape[0]
    return pl.pallas_call(
        _dispatch_kernel,
        grid=(n // tm,),
        in_specs=[pl.BlockSpec((1, 1, 2 * tm), lambda i: (i, 0, 0), memory_space=pltpu.SMEM),
                  pl.BlockSpec((tm, D_MODEL), lambda i: (i, 0)), _full((1, D_MODEL))],
        out_specs=pl.BlockSpec(memory_space=pl.ANY),
        out_shape=jax.ShapeDtypeStruct((n_rows, D_MODEL), F32),
        scratch_shapes=[pltpu.VMEM((tm, D_MODEL), F32), pltpu.SemaphoreType.DMA(())],
        compiler_params=pltpu.CompilerParams(dimension_semantics=("arbitrary",), vmem_limit_bytes=VMEM_LIMIT,
                                             has_side_effects=True),
        name="dispatch",
    )(pos, h, g_ffn)


def _expert_kernel(te_ref, tv_ref, x_ref, wg_ref, wu_ref, wd_ref, y_ref, wg_bf, wu_bf, wd_bf):
    i = pl.program_id(0)
    valid = tv_ref[i]
    changed = jnp.logical_or(i == 0, te_ref[i] != te_ref[jnp.maximum(i - 1, 0)])

    @pl.when(jnp.logical_and(changed, valid > 0))
    def _():
        wg_bf[...] = wg_ref[0].astype(BF16)
        wu_bf[...] = wu_ref[0].astype(BF16)
        wd_bf[...] = wd_ref[0].astype(BF16)

    @pl.when(valid > 0)
    def _():
        te = x_ref.shape[0]
        row = lax.broadcasted_iota(jnp.int32, (te, 1), 0)
        x = jnp.where(row < valid, x_ref[...], 0.0).astype(BF16)
        hg = jnp.dot(x, wg_bf[...], preferred_element_type=F32)
        hu = jnp.dot(x, wu_bf[...], preferred_element_type=F32)
        hid = (hg * jax.nn.sigmoid(hg)) * hu
        y_ref[...] = jnp.dot(hid.astype(BF16), wd_bf[...], preferred_element_type=F32)

    @pl.when(valid <= 0)
    def _():
        y_ref[...] = jnp.zeros_like(y_ref)


def _experts(xs, tile_expert, tile_valid, w_gate, w_up, w_down, *, te):
    n_rows = xs.shape[0]
    wspec = lambda a, b: pl.BlockSpec((1, a, b), lambda i, e, v: (e[i], 0, 0))
    grid_spec = pltpu.PrefetchScalarGridSpec(
        num_scalar_prefetch=2,
        grid=(n_rows // te,),
        in_specs=[pl.BlockSpec((te, D_MODEL), lambda i, e, v: (i, 0)),
                  wspec(D_MODEL, D_EXPERT), wspec(D_MODEL, D_EXPERT), wspec(D_EXPERT, D_MODEL)],
        out_specs=pl.BlockSpec((te, D_MODEL), lambda i, e, v: (i, 0)),
        scratch_shapes=[pltpu.VMEM((D_MODEL, D_EXPERT), BF16), pltpu.VMEM((D_MODEL, D_EXPERT), BF16),
                        pltpu.VMEM((D_EXPERT, D_MODEL), BF16)])
    return pl.pallas_call(
        _expert_kernel,
        grid_spec=grid_spec,
        out_shape=jax.ShapeDtypeStruct((n_rows, D_MODEL), F32),
        compiler_params=_cparams(1),
        name="experts",
    )(tile_expert, tile_valid, xs, w_gate, w_up, w_down)


def _combine_kernel(pos_ref, h_ref, route_ref, ys_ref, p_ref, gple_ref, wgate_ref, wproj_ref, out_ref,
                    y0, y1, sem):
    tm = y0.shape[0]

    def row_copy(p, dst, t):
        return pltpu.make_async_copy(ys_ref.at[pl.ds(p, 1), :], dst.at[pl.ds(t, 1), :], sem)

    def issue(t, carry):
        row_copy(pos_ref[0, 0, 2 * t], y0, t).start()
        row_copy(pos_ref[0, 0, 2 * t + 1], y1, t).start()
        return carry

    lax.fori_loop(0, tm, issue, 0)

    def drain(t, carry):
        row_copy(0, y0, 0).wait()
        row_copy(0, y1, 0).wait()
        return carry

    lax.fori_loop(0, tm, drain, 0)

    rec = route_ref[...]
    w1 = rec[:, ROUTE_W1:ROUTE_W1 + 1]
    w2 = rec[:, ROUTE_W2:ROUTE_W2 + 1]
    h2 = h_ref[...] + w1 * y0[...] + w2 * y1[...]
    gate = jax.nn.sigmoid(jnp.dot(_rms(h2, gple_ref[...]).astype(BF16), wgate_ref[...], preferred_element_type=F32))
    ple = jnp.dot(p_ref[...].astype(BF16), wproj_ref[...], preferred_element_type=F32)
    out_ref[...] = h2 + ple * gate


def _combine(h, route, pos, ys, p, g_ple, wgate_bf, wproj_bf, *, tm):
    n = h.shape[0]
    tok = lambda w: pl.BlockSpec((tm, w), lambda i: (i, 0))
    return pl.pallas_call(
        _combine_kernel,
        grid=(n // tm,),
        in_specs=[pl.BlockSpec((1, 1, 2 * tm), lambda i: (i, 0, 0), memory_space=pltpu.SMEM),
                  tok(D_MODEL), tok(LANES), pl.BlockSpec(memory_space=pl.ANY), tok(D_PLE),
                  _full((1, D_MODEL)), _full((D_MODEL, D_MODEL)), _full((D_PLE, D_MODEL))],
        out_specs=tok(D_MODEL),
        out_shape=jax.ShapeDtypeStruct((n, D_MODEL), F32),
        scratch_shapes=[pltpu.VMEM((tm, D_MODEL), F32), pltpu.VMEM((tm, D_MODEL), F32),
                        pltpu.SemaphoreType.DMA(())],
        compiler_params=_cparams(1),
        name="combine",
    )(pos, h, route, ys, p, g_ple, wgate_bf, wproj_bf)


def _moe_ple(h, p, g_ffn, w_router, w_gate, w_up, w_down, g_ple, wgate_bf, wproj_bf, *, tm, te):
    n = h.shape[0]
    route, counts = _route(h, g_ffn, w_router, tm=tm)
    counts = counts[0, :N_EXPERTS].astype(jnp.int32)
    padded = ((counts + te - 1) // te) * te
    ends = jnp.cumsum(padded)
    offs = ends - padded
    ids = route[:, ROUTE_I1:ROUTE_I2 + 1].astype(jnp.int32)
    ranks = route[:, ROUTE_R1:ROUTE_R2 + 1].astype(jnp.int32)
    pos = (offs[ids] + ranks).reshape(n // tm, 1, 2 * tm)
    n_tiles = (2 * n) // te + N_EXPERTS
    starts = jnp.arange(n_tiles, dtype=jnp.int32) * te
    tile_expert = jnp.minimum(jnp.searchsorted(ends, starts, side="right"), N_EXPERTS - 1).astype(jnp.int32)
    tile_valid = jnp.clip(counts[tile_expert] - (starts - offs[tile_expert]), 0, te).astype(jnp.int32)
    xs = _dispatch(h, g_ffn, pos, n_tiles * te, tm=tm)
    ys = _experts(xs, tile_expert, tile_valid, w_gate, w_up, w_down, te=te)
    return _combine(h, route, pos, ys, p, g_ple, wgate_bf, wproj_bf, tm=tm)


def _bucket(dist):
    max_exact = N_BUCKETS // 2
    d_f = jnp.maximum(dist, 1).astype(F32)
    large = max_exact + (jnp.log(d_f / max_exact) / np.log(MAX_DISTANCE / max_exact)
                         * (N_BUCKETS - max_exact)).astype(jnp.int32)
    large = jnp.minimum(large, N_BUCKETS - 1)
    return jnp.where(dist < max_exact, dist, large)


def _band_bias(rel_bias, dil):
    i = jnp.arange(Q_BLOCK)[:, None]
    j = jnp.arange(2 * Q_BLOCK)[None, :]
    off = i - j + Q_BLOCK
    valid = (off >= 0) & (off <= N_KEYS)
    tbl = rel_bias[_bucket(dil * jnp.arange(N_KEYS + 1, dtype=jnp.int32))]
    bias = tbl[jnp.clip(off, 0, N_KEYS)].transpose(2, 0, 1).astype(F32)
    bias = jnp.where(valid[None], bias, NEG)
    return bias[:, :, :Q_BLOCK], bias[:, :, Q_BLOCK:]


def _sample_tables(rel_bias, w_buf):
    t = jnp.arange(SAMPLE_T, dtype=jnp.int32)[:, None]
    qpos = w_buf + t
    g = jnp.arange(w_buf // 16, dtype=jnp.int32)
    cols_pos, cols_kind = [], []
    for r in range(SAMPLE_RES):
        cols_pos.append(16 * g + r)
        cols_kind.append(jnp.full(g.shape, 0, jnp.int32))
    cols_pos.append(w_buf - TAIL + jnp.arange(TAIL, dtype=jnp.int32))
    cols_kind.append(jnp.full((TAIL,), 1, jnp.int32))
    cols_pos.append(w_buf + jnp.arange(LANES, dtype=jnp.int32))
    cols_kind.append(jnp.full((LANES,), 2, jnp.int32))
    pos = jnp.concatenate(cols_pos)[None, :]
    kind = jnp.concatenate(cols_kind)[None, :]
    dist = qpos - pos
    in_seq = (dist >= 0) & (pos < w_buf + SAMPLE_T)
    hits = []
    for (w, d) in PATTERNS:
        hit = in_seq & (dist % d == 0) & (dist <= w)
        if d == 16:
            hit = hit & (kind != 1)
        else:
            hit = hit & (kind != 0)
        hits.append(hit)
    mult = sum(h.astype(F32) for h in hits)
    bias_h = rel_bias[_bucket(jnp.maximum(dist, 0))].astype(F32)
    bias = jnp.where((mult > 0)[:, :, None], bias_h, NEG).transpose(0, 2, 1)
    mult = jnp.broadcast_to(mult[:, None, :], bias.shape)
    rows = SAMPLE_T * N_HEADS
    return bias.reshape(rows, KEY_COLS), mult.reshape(rows, KEY_COLS)


PROMPT_TM = 256
PROMPT_TE = 256
ATTN_SUB = 2


def kernel(x_prompt, x_sample, cache_k, cache_v, state_conv, p_prompt, p_sample, rel_bias, g_mix, w_in, q_gain,
           k_gain, conv_w, g_out_att, g_out_conv, w_out, g_ffn, w_router_group, w_router_expert, w_gate, w_up,
           w_down, g_ple, w_ple_gate, w_ple_proj):
    depth = w_in.shape[0]
    batch, seq, _ = x_prompt.shape
    dec_b, dec_t, _ = x_sample.shape
    w_buf = cache_k.shape[2]
    n_s = dec_b * dec_t
    assert dec_t == SAMPLE_T and w_buf % TAIL == 0 and seq % (PROMPT_TM) == 0

    row = lambda a: a.reshape(1, -1)
    head_avg = jnp.kron(jnp.eye(N_HEADS, dtype=F32), jnp.full((HEAD_DIM, HEAD_DIM), 1.0 / HEAD_DIM, F32)).astype(BF16)
    lane_head = jnp.arange(LANES) // LSE_LANES_PER_HEAD
    expand = ((lane_head[:, None] == (jnp.arange(ATT_DIM) // HEAD_DIM)[None, :])
              & ((jnp.arange(LANES) % LSE_LANES_PER_HEAD)[:, None] == 0)).astype(BF16)
    band = [_band_bias(rel_bias, d) for (_, d) in PATTERNS]
    s_bias, s_mult = _sample_tables(rel_bias, w_buf)

    hp = x_prompt.reshape(batch * seq, D_MODEL)
    hs = jnp.swapaxes(x_sample, 0, 1).reshape(n_s, D_MODEL)
    new = {k: [] for k in ("kp", "vp", "cp", "ks", "vs", "cs")}
    hist_p = jnp.zeros((batch, SUBLANES, CONV_DIM), F32)

    for l in range(depth):
        w_in_bf = w_in[l].astype(BF16)
        wa_bf = w_out[l, :ATT_DIM].astype(BF16)
        wc_bf = w_out[l, ATT_DIM:].astype(BF16)
        wgate_bf = w_ple_gate[l].astype(BF16)
        wproj_bf = w_ple_proj[l].astype(BF16)
        w_router = jnp.concatenate(
            [w_router_expert[l], w_router_group[l],
             jnp.zeros((D_MODEL, LANES - N_EXPERTS - N_GROUPS), F32)], axis=1)
        qg, kg = row(jnp.tile(q_gain[l], N_HEADS)), row(jnp.tile(k_gain[l], N_HEADS))
        mix = (row(g_mix[l]), w_in_bf, qg, kg, head_avg, conv_w[l], row(g_out_conv[l]))
        moe = (row(g_ffn[l]), w_router, w_gate[l], w_up[l], w_down[l], row(g_ple[l]), wgate_bf, wproj_bf)

        q, k, v, yn, nconv = _inproj(hp, hist_p, *mix, tm=PROMPT_TM, shift=1, tiles_per_seq=seq // PROMPT_TM)
        os, lses = [], []
        for (bl, br), (_, d) in zip(band, PATTERNS):
            o, lse = _attn_pattern(q, k, v, bl, br, batch=batch, seq=seq, dil=d, sub=ATTN_SUB)
            os.append(o)
            lses.append(lse)
        hp = _outproj(os, lses, yn, hp, row(g_out_att[l]), expand, wa_bf, wc_bf, tm=PROMPT_TM)
        hp = _moe_ple(hp, p_prompt[l].reshape(batch * seq, D_PLE), *moe, tm=PROMPT_TM, te=PROMPT_TE)
        keep = min(w_buf, seq)
        new["kp"].append(k.reshape(batch, seq, N_HEADS, HEAD_DIM)[:, seq - keep:])
        new["vp"].append(v.reshape(batch, seq, N_HEADS, HEAD_DIM)[:, seq - keep:])
        new["cp"].append(nconv[:, SUBLANES - 2:])

        hist_s = jnp.swapaxes(state_conv[l], 0, 1).reshape(1, 2 * dec_b, CONV_DIM)
        q, k, v, yn, nconv = _inproj(hs, hist_s, *mix, tm=n_s, shift=dec_b, tiles_per_seq=1)
        bmaj = lambda a: jnp.swapaxes(a.reshape(dec_t, dec_b, ATT_DIM), 0, 1)
        qb, kb, vb = bmaj(q), bmaj(k), bmaj(v)
        pad8 = lambda a: jnp.pad(a, ((0, 0), (0, SUBLANES - dec_t), (0, 0)))
        att = _attn_sample(qb, pad8(kb), pad8(vb), cache_k, cache_v, l, s_bias, s_mult)
        att_tm = jnp.swapaxes(att, 0, 1).reshape(n_s, ATT_DIM)
        hs = _outproj([att_tm], [], yn, hs, row(g_out_att[l]), expand, wa_bf, wc_bf, tm=n_s)
        p_s = jnp.swapaxes(p_sample[l], 0, 1).reshape(n_s, D_PLE)
        hs = _moe_ple(hs, p_s, *moe, tm=n_s, te=LANES)
        new["ks"].append(kb.reshape(dec_b, dec_t, N_HEADS, HEAD_DIM))
        new["vs"].append(vb.reshape(dec_b, dec_t, N_HEADS, HEAD_DIM))
        new["cs"].append(jnp.swapaxes(nconv.reshape(2, dec_b, CONV_DIM), 0, 1))

    y_prompt = hp.reshape(batch, seq, D_MODEL)
    y_sample = jnp.swapaxes(hs.reshape(dec_t, dec_b, D_MODEL), 0, 1)
    st = lambda key: jnp.stack(new[key])
    return (y_prompt, y_sample, st("kp"), st("vp"), st("cp"), st("ks"), st("vs"), st("cs"))
```

```python
import functools

import jax
import jax.numpy as jnp
import numpy as np
from jax import lax
from jax.experimental import pallas as pl
from jax.experimental.pallas import tpu as pltpu

F32 = jnp.float32
BF16 = jnp.bfloat16
HIGHEST = lax.Precision.HIGHEST

D_MODEL = 1024
HEAD_DIM = 64
N_HEADS = 8
ATT_DIM = N_HEADS * HEAD_DIM
CONV_DIM = D_MODEL - ATT_DIM
MIX_IN = 3 * ATT_DIM + 3 * CONV_DIM
PATTERNS = ((128, 1), (512, 4), (2048, 16))
N_KEYS = 128
Q_BLOCK = 128
N_BUCKETS = 32
MAX_DISTANCE = 2048
N_GROUPS = 4
EXPERTS_PER_GROUP = 8
N_EXPERTS = N_GROUPS * EXPERTS_PER_GROUP
D_EXPERT = 256
D_PLE = 256
EPS = 1e-6
NEG = -1e30

LANES = 128
SUBLANES = 8
SLABS = Q_BLOCK // SUBLANES
LSE_LANES_PER_HEAD = LANES // N_HEADS
VMEM_LIMIT = 48 * 1024 * 1024
NT = (((1,), (1,)), ((), ()))


def _cparams(n_axes):
    return pltpu.CompilerParams(dimension_semantics=("arbitrary",) * n_axes,
                                vmem_limit_bytes=VMEM_LIMIT)


def _full(shape):
    n = len(shape)
    return pl.BlockSpec(shape, lambda *_: (0,) * n)


def _rms(x, gain):
    ms = jnp.mean(x * x, axis=-1, keepdims=True)
    return x * lax.rsqrt(ms + EPS) * gain


def _inproj_kernel(*refs, shift, tiles_per_seq, permute):
    (h_ref, gmix_ref, w_ref, qg_ref, kg_ref, bd_ref, cw_ref, gconv_ref, hist_ref) = refs[:9]
    if permute:
        perm_ref, q_ref, k_ref, v_ref, kn_ref, vn_ref, yn_ref, nconv_ref, carry_ref = refs[9:]
    else:
        q_ref, k_ref, v_ref, yn_ref, nconv_ref, carry_ref = refs[9:]
    i = pl.program_id(0)
    a = _rms(h_ref[...], gmix_ref[...])
    proj = jnp.dot(a.astype(BF16), w_ref[...], preferred_element_type=F32)
    tm = proj.shape[0]

    def head_norm(t, g):
        ms = jnp.dot((t * t).astype(BF16), bd_ref[...], preferred_element_type=F32)
        return t * lax.rsqrt(ms + EPS) * g

    q = head_norm(proj[:, 0:ATT_DIM], qg_ref[...])
    k = head_norm(proj[:, ATT_DIM:2 * ATT_DIM], kg_ref[...])
    v = proj[:, 2 * ATT_DIM:3 * ATT_DIM]
    if permute:
        kn_ref[...] = k
        vn_ref[...] = v
        qkv = jnp.concatenate([q * (HEAD_DIM ** -0.5), k, v], axis=-1).astype(BF16)
        moved = jnp.dot(perm_ref[...], qkv, preferred_element_type=F32)
        q_ref[...] = moved[:, 0:ATT_DIM]
        k_ref[...] = moved[:, ATT_DIM:2 * ATT_DIM]
        v_ref[...] = moved[:, 2 * ATT_DIM:3 * ATT_DIM]
    else:
        q_ref[...] = q
        k_ref[...] = k
        v_ref[...] = v
    c0 = 3 * ATT_DIM
    hc = proj[:, c0:c0 + CONV_DIM]
    gb = proj[:, c0 + CONV_DIM:c0 + 2 * CONV_DIM]
    gc = proj[:, c0 + 2 * CONV_DIM:c0 + 3 * CONV_DIM]
    u = gc * hc

    if shift == 1:
        @pl.when(i % tiles_per_seq == 0)
        def _():
            carry_ref[...] = hist_ref[0]
        h0 = carry_ref[SUBLANES - 2:SUBLANES - 1, :]
        h1 = carry_ref[SUBLANES - 1:SUBLANES, :]
        row = lax.broadcasted_iota(jnp.int32, (tm, 1), 0)
        u1 = jnp.where(row == 0, h1, pltpu.roll(u, 1, 0))
        u2 = jnp.where(row == 0, h0, jnp.where(row == 1, h1, pltpu.roll(u, 2, 0)))
        carry_ref[...] = u[tm - SUBLANES:tm, :]
        nconv_ref[0] = u[tm - SUBLANES:tm, :]
    else:
        hist = hist_ref[0]
        u1 = jnp.concatenate([hist[shift:2 * shift], u[0:tm - shift]], axis=0)
        u2 = jnp.concatenate([hist, u[0:tm - 2 * shift]], axis=0)
        nconv_ref[0] = u[tm - 2 * shift:tm, :]
    conv = cw_ref[0:1, :] * u2 + cw_ref[1:2, :] * u1 + cw_ref[2:3, :] * u
    yn_ref[...] = _rms(gb * conv, gconv_ref[...])


def _inproj(h, hist, g_mix, w_in_bf, q_gain, k_gain, bd, conv_w, g_out_conv, perm=None, *,
            tm, shift, tiles_per_seq, keep_tiles=0):
    n = h.shape[0]
    hist_rows = hist.shape[1]
    nseq = hist.shape[0]
    tok = lambda w: pl.BlockSpec((tm, w), lambda i: (i, 0))
    seq3 = lambda r: pl.BlockSpec((1, r, CONV_DIM), lambda i: (i // tiles_per_seq, 0, 0))
    nconv_rows = SUBLANES if shift == 1 else 2 * shift
    att = jax.ShapeDtypeStruct((n, ATT_DIM), F32)
    in_specs = [tok(D_MODEL), _full((1, D_MODEL)), _full((D_MODEL, MIX_IN)), _full((1, ATT_DIM)),
                _full((1, ATT_DIM)), _full((ATT_DIM, ATT_DIM)), _full((3, CONV_DIM)),
                _full((1, CONV_DIM)), seq3(hist_rows)]
    args = [h, g_mix, w_in_bf, q_gain, k_gain, bd, conv_w, g_out_conv, hist]
    out_specs = [tok(ATT_DIM)] * 3
    out_shape = [att] * 3
    if perm is not None:
        in_specs.append(_full((tm, tm)))
        args.append(perm)
        first = tiles_per_seq - keep_tiles
        kept = pl.BlockSpec((tm, ATT_DIM), lambda i: ((i // tiles_per_seq) * keep_tiles
                                                      + jnp.maximum(i % tiles_per_seq - first, 0), 0))
        out_specs += [kept, kept]
        out_shape += [jax.ShapeDtypeStruct((nseq * keep_tiles * tm, ATT_DIM), F32)] * 2
    out_specs += [tok(CONV_DIM), seq3(nconv_rows)]
    out_shape += [jax.ShapeDtypeStruct((n, CONV_DIM), F32),
                  jax.ShapeDtypeStruct((nseq, nconv_rows, CONV_DIM), F32)]
    return pl.pallas_call(
        functools.partial(_inproj_kernel, shift=shift, tiles_per_seq=tiles_per_seq, permute=perm is not None),
        grid=(n // tm,),
        in_specs=in_specs, out_specs=out_specs, out_shape=out_shape,
        scratch_shapes=[pltpu.VMEM((SUBLANES, CONV_DIM), F32)],
        compiler_params=_cparams(1),
        name="inproj",
    )(*args)


def _attn_block(q, kk, vv, bias):
    lane = lax.broadcasted_iota(jnp.int32, (Q_BLOCK, LANES), 1)
    upper = lane >= HEAD_DIM
    scores = []
    for h in range(N_HEADS):
        j, e = divmod(h, 2)
        qp = q[:, j * LANES:(j + 1) * LANES]
        qm = (jnp.where(upper, qp, 0.0) if e else jnp.where(upper, 0.0, qp)).astype(BF16)
        scores.append(lax.dot_general(qm, kk[:, j * LANES:(j + 1) * LANES], NT, preferred_element_type=F32))
    s = jnp.concatenate(scores, axis=0) + bias
    m = jnp.max(s, axis=-1, keepdims=True)
    p = jnp.exp(s - m)
    den = jnp.sum(p, axis=-1, keepdims=True)
    pb = p.astype(BF16)
    inv = 1.0 / den
    lse = m + jnp.log(den)
    lse_grp = lane // LSE_LANES_PER_HEAD
    lse_tile = jnp.zeros((Q_BLOCK, LANES), F32)
    outs = []
    for j in range(N_HEADS // 2):
        pair = None
        for e in range(2):
            h = 2 * j + e
            rows = slice(h * Q_BLOCK, (h + 1) * Q_BLOCK)
            o = jnp.dot(pb[rows], vv[:, j * LANES:(j + 1) * LANES], preferred_element_type=F32) * inv[rows]
            pair = o if e == 0 else jnp.where(upper, o, pair)
            lse_tile = jnp.where(lse_grp == h, lse[rows], lse_tile)
        outs.append(pair)
    return jnp.concatenate(outs, axis=-1), lse_tile


def _attn_kernel(q_ref, kp_ref, kc_ref, vp_ref, vc_ref, bias_ref, o_ref, lse_ref, kbuf, vbuf, *, sub):
    n = pl.program_id(2)
    rows = sub * Q_BLOCK
    kbuf[0:Q_BLOCK, :] = kp_ref[...].reshape(Q_BLOCK, ATT_DIM).astype(BF16)
    kbuf[Q_BLOCK:, :] = kc_ref[...].reshape(rows, ATT_DIM).astype(BF16)
    vbuf[0:Q_BLOCK, :] = vp_ref[...].reshape(Q_BLOCK, ATT_DIM).astype(BF16)
    vbuf[Q_BLOCK:, :] = vc_ref[...].reshape(rows, ATT_DIM).astype(BF16)

    for j in range(sub):
        q = q_ref[j].reshape(Q_BLOCK, ATT_DIM)
        r0 = j * Q_BLOCK
        first = jnp.logical_and(n == 0, j == 0).astype(jnp.int32) if j == 0 else 0
        o, lse = _attn_block(q, kbuf[r0:r0 + 2 * Q_BLOCK, :], vbuf[r0:r0 + 2 * Q_BLOCK, :], bias_ref[first])
        o_ref[j] = o.reshape(SLABS, SUBLANES, ATT_DIM)
        lse_ref[j] = lse.reshape(SLABS, SUBLANES, LANES)


def _attn_pattern(q, k, v, bias, *, batch, seq, dil, sub):
    nblk = seq // (Q_BLOCK * dil)
    view = lambda t: t.reshape(batch, nblk, SLABS, dil, SUBLANES, t.shape[-1])
    cur = lambda c: pl.BlockSpec((None, sub, SLABS, None, SUBLANES, c), lambda b, r, n: (b, n, 0, r, 0, 0))
    prev = pl.BlockSpec((None, None, SLABS, None, SUBLANES, ATT_DIM),
                        lambda b, r, n: (b, jnp.maximum(n * sub - 1, 0), 0, r, 0, 0))
    o, lse = pl.pallas_call(
        functools.partial(_attn_kernel, sub=sub),
        grid=(batch, dil, nblk // sub),
        in_specs=[cur(ATT_DIM), prev, cur(ATT_DIM), prev, cur(ATT_DIM),
                  _full((2, N_HEADS * Q_BLOCK, 2 * Q_BLOCK))],
        out_specs=[cur(ATT_DIM), cur(LANES)],
        out_shape=[jax.ShapeDtypeStruct((batch, nblk, SLABS, dil, SUBLANES, ATT_DIM), F32),
                   jax.ShapeDtypeStruct((batch, nblk, SLABS, dil, SUBLANES, LANES), F32)],
        scratch_shapes=[pltpu.VMEM(((sub + 1) * Q_BLOCK, ATT_DIM), BF16),
                        pltpu.VMEM(((sub + 1) * Q_BLOCK, ATT_DIM), BF16)],
        compiler_params=_cparams(3),
        name=f"attn_d{dil}",
    )(view(q), view(k), view(k), view(v), view(v), bias)
    return o.reshape(batch * seq, ATT_DIM), lse.reshape(batch * seq, LANES)


SAMPLE_T = 4
SAMPLE_RES = 4
TAIL = 512
RES_ROWS = N_KEYS * SAMPLE_RES * N_HEADS
TAIL_ROWS = TAIL * N_HEADS
NEW_ROWS = LANES
KEY_COLS = RES_ROWS + TAIL_ROWS + NEW_ROWS


def _attn_sample_kernel(q_ref, ka_ref, kb_ref, kn_ref, va_ref, vb_ref, vn_ref, bias_ref, mult_ref, o_ref):
    rows = SAMPLE_T * N_HEADS
    q = (q_ref[...].reshape(rows, HEAD_DIM) * (HEAD_DIM ** -0.5)).astype(BF16)
    pad = jnp.zeros((NEW_ROWS - rows, HEAD_DIM), F32)

    def flat(ref, n_rows):
        return ref[...].reshape(n_rows, HEAD_DIM).astype(BF16)

    def flat_new(ref):
        return jnp.concatenate([ref[...].reshape(rows, HEAD_DIM), pad], axis=0).astype(BF16)

    keys = [flat(ka_ref, RES_ROWS), flat(kb_ref, TAIL_ROWS), flat_new(kn_ref)]
    vals = [flat(va_ref, RES_ROWS), flat(vb_ref, TAIL_ROWS), flat_new(vn_ref)]
    s = jnp.concatenate([lax.dot_general(q, kk, NT, preferred_element_type=F32) for kk in keys], axis=-1)
    s = s + bias_ref[...]
    m = jnp.max(s, axis=-1, keepdims=True)
    p = jnp.exp(s - m) * mult_ref[...]
    den = jnp.sum(p, axis=-1, keepdims=True)
    pb = p.astype(BF16)
    acc = jnp.zeros((rows, HEAD_DIM), F32)
    c = 0
    for vv in vals:
        w = vv.shape[0]
        acc = acc + jnp.dot(pb[:, c:c + w], vv, preferred_element_type=F32)
        c += w
    o_ref[...] = (acc / den).reshape(SAMPLE_T, N_HEADS, HEAD_DIM)


def _attn_sample(q, k, v, cache_k, cache_v, layer, bias, mult):
    nb = q.shape[0]
    n_layers, _, w_buf = cache_k.shape[:3]
    res_view = lambda c: c.reshape(n_layers, nb, w_buf // 16, 16, N_HEADS, HEAD_DIM)
    new = pl.BlockSpec((None, SAMPLE_T, N_HEADS, HEAD_DIM), lambda b: (b, 0, 0, 0))
    res = pl.BlockSpec((None, None, w_buf // 16, SAMPLE_RES, N_HEADS, HEAD_DIM), lambda b: (layer, b, 0, 0, 0, 0))
    tail = pl.BlockSpec((None, None, TAIL, N_HEADS, HEAD_DIM), lambda b: (layer, b, w_buf // TAIL - 1, 0, 0))
    tbl = _full((SAMPLE_T * N_HEADS, KEY_COLS))
    return pl.pallas_call(
        _attn_sample_kernel,
        grid=(nb,),
        in_specs=[new, res, tail, new, res, tail, new, tbl, tbl],
        out_specs=new,
        out_shape=jax.ShapeDtypeStruct((nb, SAMPLE_T, N_HEADS, HEAD_DIM), F32),
        compiler_params=_cparams(1),
        name="attn_sample",
    )(q, res_view(cache_k), cache_k, k, res_view(cache_v), cache_v, v, bias, mult)


def _split_dot(x, e_ref):
    hi = x.astype(BF16)
    lo = (x - hi.astype(F32)).astype(BF16)
    return (jnp.dot(hi, e_ref[...], preferred_element_type=F32)
            + jnp.dot(lo, e_ref[...], preferred_element_type=F32))


def _outproj_kernel(*refs, n_pat):
    mix = n_pat > 1
    n_lse = n_pat if mix else 0
    o_refs = refs[0:n_pat]
    l_refs = refs[n_pat:n_pat + n_lse]
    rest = refs[n_pat + n_lse:]
    if mix:
        yn_ref, h_ref, gatt_ref, exp_ref, unperm_ref, wa_ref, wc_ref, out_ref = rest
        lses = [r[...] for r in l_refs]
        top = functools.reduce(jnp.maximum, lses)
        ws = [jnp.exp(l - top) for l in lses]
        tot = functools.reduce(lambda a, b: a + b, ws)
        att = None
        for w, o_ref in zip(ws, o_refs):
            term = _split_dot(w / tot, exp_ref) * o_ref[...]
            att = term if att is None else att + term
        att_bf = jnp.dot(unperm_ref[...], _rms(att, gatt_ref[...]).astype(BF16),
                         preferred_element_type=F32).astype(BF16)
    else:
        yn_ref, h_ref, gatt_ref, wa_ref, wc_ref, out_ref = rest
        att_bf = _rms(o_refs[0][...], gatt_ref[...]).astype(BF16)
    y = (jnp.dot(att_bf, wa_ref[...], preferred_element_type=F32)
         + jnp.dot(yn_ref[...].astype(BF16), wc_ref[...], preferred_element_type=F32))
    out_ref[...] = h_ref[...] + y


def _outproj(os, lses, yn, h, g_att, wa_bf, wc_bf, expand=None, unperm=None, *, tm):
    n = h.shape[0]
    n_pat = len(os)
    tok = lambda w: pl.BlockSpec((tm, w), lambda i: (i, 0))
    in_specs = [tok(ATT_DIM)] * n_pat + [tok(LANES)] * len(lses) + [tok(CONV_DIM), tok(D_MODEL), _full((1, ATT_DIM))]
    args = [*os, *lses, yn, h, g_att]
    if n_pat > 1:
        in_specs += [_full((LANES, ATT_DIM)), _full((tm, tm))]
        args += [expand, unperm]
    in_specs += [_full((ATT_DIM, D_MODEL)), _full((CONV_DIM, D_MODEL))]
    args += [wa_bf, wc_bf]
    return pl.pallas_call(
        functools.partial(_outproj_kernel, n_pat=n_pat),
        grid=(n // tm,),
        in_specs=in_specs,
        out_specs=tok(D_MODEL),
        out_shape=jax.ShapeDtypeStruct((n, D_MODEL), F32),
        compiler_params=_cparams(1),
        name="outproj",
    )(*args)


ROUTE_I1, ROUTE_I2, ROUTE_R1, ROUTE_R2, ROUTE_W1, ROUTE_W2 = range(6)
GROUP_LANE0 = N_EXPERTS


def _route_kernel(h_ref, g_ref, wr_ref, route_ref, cnt_ref, carry_ref):
    i = pl.program_id(0)

    @pl.when(i == 0)
    def _():
        carry_ref[...] = jnp.zeros_like(carry_ref)

    m = _rms(h_ref[...], g_ref[...])
    logits = jnp.dot(m, wr_ref[...], preferred_element_type=F32, precision=HIGHEST)
    tm = logits.shape[0]
    lane_i = lax.broadcasted_iota(jnp.int32, (tm, LANES), 1)
    lane = lane_i.astype(F32)
    big = jnp.float32(4 * LANES)

    is_g = jnp.logical_and(lane_i >= GROUP_LANE0, lane_i < GROUP_LANE0 + N_GROUPS)
    gl = jnp.where(is_g, logits, NEG)
    gmax = jnp.max(gl, axis=-1, keepdims=True)
    g_w = 1.0 / jnp.sum(jnp.where(is_g, jnp.exp(gl - gmax), 0.0), axis=-1, keepdims=True)
    g_sel = jnp.min(jnp.where(gl == gmax, lane - GROUP_LANE0, big), axis=-1, keepdims=True)

    grp_of_lane = (lane_i // EXPERTS_PER_GROUP).astype(F32)
    in_grp = jnp.logical_and(lane_i < N_EXPERTS, grp_of_lane == g_sel)
    el = jnp.where(in_grp, logits, NEG)
    t1 = jnp.max(el, axis=-1, keepdims=True)
    i1 = jnp.min(jnp.where(el == t1, lane, big), axis=-1, keepdims=True)
    el2 = jnp.where(lane == i1, NEG, el)
    t2 = jnp.max(el2, axis=-1, keepdims=True)
    i2 = jnp.min(jnp.where(el2 == t2, lane, big), axis=-1, keepdims=True)
    e2 = jnp.exp(t2 - t1)
    w1 = g_w / (1.0 + e2)
    w2 = g_w * e2 / (1.0 + e2)

    hit1 = lane == i1
    hit2 = lane == i2
    c = jnp.where(jnp.logical_or(hit1, hit2), 1.0, 0.0)
    rr = lax.broadcasted_iota(jnp.int32, (tm, tm), 0)
    cc = lax.broadcasted_iota(jnp.int32, (tm, tm), 1)
    lower = jnp.where(rr > cc, 1.0, 0.0).astype(BF16)
    before = jnp.dot(lower, c.astype(BF16), preferred_element_type=F32) + carry_ref[0:1, :]
    r1 = jnp.sum(jnp.where(hit1, before, 0.0), axis=-1, keepdims=True)
    r2 = jnp.sum(jnp.where(hit2, before, 0.0), axis=-1, keepdims=True)
    total = carry_ref[0:1, :] + jnp.sum(c, axis=0, keepdims=True)
    carry_ref[...] = jnp.broadcast_to(total, carry_ref.shape)
    cnt_ref[...] = jnp.broadcast_to(total, cnt_ref.shape)

    rec = jnp.zeros((tm, LANES), F32)
    for idx, val in ((ROUTE_I1, i1), (ROUTE_I2, i2), (ROUTE_R1, r1), (ROUTE_R2, r2), (ROUTE_W1, w1), (ROUTE_W2, w2)):
        rec = jnp.where(lane_i == idx, val, rec)
    route_ref[...] = rec


def _route(h, g_ffn, w_router, *, tm):
    n = h.shape[0]
    tok = lambda w: pl.BlockSpec((tm, w), lambda i: (i, 0))
    return pl.pallas_call(
        _route_kernel,
        grid=(n // tm,),
        in_specs=[tok(D_MODEL), _full((1, D_MODEL)), _full((D_MODEL, LANES))],
        out_specs=[tok(LANES), _full((SUBLANES, LANES))],
        out_shape=[jax.ShapeDtypeStruct((n, LANES), F32), jax.ShapeDtypeStruct((SUBLANES, LANES), F32)],
        scratch_shapes=[pltpu.VMEM((SUBLANES, LANES), F32)],
        compiler_params=_cparams(1),
        name="route",
    )(h, g_ffn, w_router)


def _dispatch_kernel(pos_ref, h_ref, g_ref, xs_ref, mbuf, sem):
    tm = mbuf.shape[0]
    mbuf[...] = _rms(h_ref[...], g_ref[...])

    def row_copy(t, p):
        return pltpu.make_async_copy(mbuf.at[pl.ds(t, 1), :], xs_ref.at[pl.ds(p, 1), :], sem)

    def issue(t, carry):
        row_copy(t, pos_ref[0, 0, 2 * t]).start()
        row_copy(t, pos_ref[0, 0, 2 * t + 1]).start()
        return carry

    lax.fori_loop(0, tm, issue, 0)

    def drain(t, carry):
        row_copy(0, 0).wait()
        row_copy(0, 0).wait()
        return carry

    lax.fori_loop(0, tm, drain, 0)


def _dispatch(h, g_ffn, pos, n_rows, *, tm):
    n = h.shape[0]
    return pl.pallas_call(
        _dispatch_kernel,
        grid=(n // tm,),
        in_specs=[pl.BlockSpec((1, 1, 2 * tm), lambda i: (i, 0, 0), memory_space=pltpu.SMEM),
                  pl.BlockSpec((tm, D_MODEL), lambda i: (i, 0)), _full((1, D_MODEL))],
        out_specs=pl.BlockSpec(memory_space=pl.ANY),
        out_shape=jax.ShapeDtypeStruct((n_rows, D_MODEL), F32),
        scratch_shapes=[pltpu.VMEM((tm, D_MODEL), F32), pltpu.SemaphoreType.DMA(())],
        compiler_params=pltpu.CompilerParams(dimension_semantics=("arbitrary",), vmem_limit_bytes=VMEM_LIMIT,
                                             has_side_effects=True),
        name="dispatch",
    )(pos, h, g_ffn)


def _expert_kernel(te_ref, tv_ref, x_ref, wg_ref, wu_ref, wd_ref, y_ref, wg_bf, wu_bf, wd_bf):
    i = pl.program_id(0)
    valid = tv_ref[i]
    changed = jnp.logical_or(i == 0, te_ref[i] != te_ref[jnp.maximum(i - 1, 0)])

    @pl.when(jnp.logical_and(changed, valid > 0))
    def _():
        wg_bf[...] = wg_ref[0].astype(BF16)
        wu_bf[...] = wu_ref[0].astype(BF16)
        wd_bf[...] = wd_ref[0].astype(BF16)

    @pl.when(valid > 0)
    def _():
        te = x_ref.shape[0]
        row = lax.broadcasted_iota(jnp.int32, (te, 1), 0)
        x = jnp.where(row < valid, x_ref[...], 0.0).astype(BF16)
        hg = jnp.dot(x, wg_bf[...], preferred_element_type=F32)
        hu = jnp.dot(x, wu_bf[...], preferred_element_type=F32)
        hid = (hg * jax.nn.sigmoid(hg)) * hu
        y_ref[...] = jnp.dot(hid.astype(BF16), wd_bf[...], preferred_element_type=F32)

    @pl.when(valid <= 0)
    def _():
        y_ref[...] = jnp.zeros_like(y_ref)


def _experts(xs, tile_expert, tile_valid, w_gate, w_up, w_down, *, te):
    n_rows = xs.shape[0]
    wspec = lambda a, b: pl.BlockSpec((1, a, b), lambda i, e, v: (e[i], 0, 0))
    grid_spec = pltpu.PrefetchScalarGridSpec(
        num_scalar_prefetch=2,
        grid=(n_rows // te,),
        in_specs=[pl.BlockSpec((te, D_MODEL), lambda i, e, v: (i, 0)),
                  wspec(D_MODEL, D_EXPERT), wspec(D_MODEL, D_EXPERT), wspec(D_EXPERT, D_MODEL)],
        out_specs=pl.BlockSpec((te, D_MODEL), lambda i, e, v: (i, 0)),
        scratch_shapes=[pltpu.VMEM((D_MODEL, D_EXPERT), BF16), pltpu.VMEM((D_MODEL, D_EXPERT), BF16),
                        pltpu.VMEM((D_EXPERT, D_MODEL), BF16)])
    return pl.pallas_call(
        _expert_kernel,
        grid_spec=grid_spec,
        out_shape=jax.ShapeDtypeStruct((n_rows, D_MODEL), F32),
        compiler_params=_cparams(1),
        name="experts",
    )(tile_expert, tile_valid, xs, w_gate, w_up, w_down)


def _combine_kernel(pos_ref, h_ref, route_ref, ys_ref, p_ref, gple_ref, wgate_ref, wproj_ref, out_ref,
                    y0, y1, sem):
    tm = y0.shape[0]

    def row_copy(p, dst, t):
        return pltpu.make_async_copy(ys_ref.at[pl.ds(p, 1), :], dst.at[pl.ds(t, 1), :], sem)

    def issue(t, carry):
        row_copy(pos_ref[0, 0, 2 * t], y0, t).start()
        row_copy(pos_ref[0, 0, 2 * t + 1], y1, t).start()
        return carry

    lax.fori_loop(0, tm, issue, 0)

    def drain(t, carry):
        row_copy(0, y0, 0).wait()
        row_copy(0, y1, 0).wait()
        return carry

    lax.fori_loop(0, tm, drain, 0)

    rec = route_ref[...]
    w1 = rec[:, ROUTE_W1:ROUTE_W1 + 1]
    w2 = rec[:, ROUTE_W2:ROUTE_W2 + 1]
    h2 = h_ref[...] + w1 * y0[...] + w2 * y1[...]
    gate = jax.nn.sigmoid(jnp.dot(_rms(h2, gple_ref[...]).astype(BF16), wgate_ref[...], preferred_element_type=F32))
    ple = jnp.dot(p_ref[...].astype(BF16), wproj_ref[...], preferred_element_type=F32)
    out_ref[...] = h2 + ple * gate


def _combine(h, route, pos, ys, p, g_ple, wgate_bf, wproj_bf, *, tm):
    n = h.shape[0]
    tok = lambda w: pl.BlockSpec((tm, w), lambda i: (i, 0))
    return pl.pallas_call(
        _combine_kernel,
        grid=(n // tm,),
        in_specs=[pl.BlockSpec((1, 1, 2 * tm), lambda i: (i, 0, 0), memory_space=pltpu.SMEM),
                  tok(D_MODEL), tok(LANES), pl.BlockSpec(memory_space=pl.ANY), tok(D_PLE),
                  _full((1, D_MODEL)), _full((D_MODEL, D_MODEL)), _full((D_PLE, D_MODEL))],
        out_specs=tok(D_MODEL),
        out_shape=jax.ShapeDtypeStruct((n, D_MODEL), F32),
        scratch_shapes=[pltpu.VMEM((tm, D_MODEL), F32), pltpu.VMEM((tm, D_MODEL), F32),
                        pltpu.SemaphoreType.DMA(())],
        compiler_params=_cparams(1),
        name="combine",
    )(pos, h, route, ys, p, g_ple, wgate_bf, wproj_bf)


def _lookup(table, idx, size):
    hit = idx[..., None] == jnp.arange(size, dtype=jnp.int32)
    return jnp.sum(jnp.where(hit, table, 0), axis=-1)


def _moe_ple(h, p, g_ffn, w_router, w_gate, w_up, w_down, g_ple, wgate_bf, wproj_bf, *, tm, te):
    n = h.shape[0]
    route, counts = _route(h, g_ffn, w_router, tm=tm)
    counts = counts[0, :N_EXPERTS].astype(jnp.int32)
    padded = ((counts + te - 1) // te) * te
    ends = jnp.cumsum(padded)
    offs = ends - padded
    ids = route[:, ROUTE_I1:ROUTE_I2 + 1].astype(jnp.int32)
    ranks = route[:, ROUTE_R1:ROUTE_R2 + 1].astype(jnp.int32)
    pos = (_lookup(offs, ids, N_EXPERTS) + ranks).reshape(n // tm, 1, 2 * tm)
    n_tiles = (2 * n) // te + N_EXPERTS
    starts = jnp.arange(n_tiles, dtype=jnp.int32) * te
    tile_expert = jnp.minimum(jnp.sum((starts[:, None] >= ends[None, :]).astype(jnp.int32), axis=-1), N_EXPERTS - 1)
    tile_valid = jnp.clip(_lookup(counts, tile_expert, N_EXPERTS)
                          - (starts - _lookup(offs, tile_expert, N_EXPERTS)), 0, te).astype(jnp.int32)
    xs = _dispatch(h, g_ffn, pos, n_tiles * te, tm=tm)
    ys = _experts(xs, tile_expert, tile_valid, w_gate, w_up, w_down, te=te)
    return _combine(h, route, pos, ys, p, g_ple, wgate_bf, wproj_bf, tm=tm)


def _bucket_np(dist):
    max_exact = N_BUCKETS // 2
    d_f = np.maximum(dist, 1).astype(np.float32)
    large = max_exact + (np.log(d_f / np.float32(max_exact)) / np.float32(np.log(MAX_DISTANCE / max_exact))
                         * np.float32(N_BUCKETS - max_exact)).astype(np.int32)
    large = np.minimum(large, N_BUCKETS - 1)
    return np.where(dist < max_exact, dist, large).astype(np.int32)


def _bias_from_buckets(rel_bias, bucket, valid):
    onehot = (jnp.asarray(bucket)[..., None] == jnp.arange(N_BUCKETS, dtype=jnp.int32)).astype(F32)
    bias = jnp.einsum("...k,kh->h...", onehot, rel_bias.astype(F32), precision=HIGHEST)
    return jnp.where(jnp.asarray(valid)[None], bias, NEG)


def _block_order(dil):
    g = np.arange(Q_BLOCK) // SUBLANES
    j = np.arange(Q_BLOCK) % SUBLANES
    if dil == 1:
        return 16 * j + g
    if dil == 4:
        return 32 * (g // 4) + 4 * j + g % 4
    return SUBLANES * g + j


def _band_bias(rel_bias, dil):
    mu = _block_order(dil)
    qi = mu[:, None] + Q_BLOCK
    ki = np.concatenate([mu, mu + Q_BLOCK])[None, :]
    off = qi - ki
    valid = (off >= 0) & (off <= N_KEYS)
    bucket = _bucket_np(dil * np.clip(off, 0, N_KEYS))
    first = valid & (np.arange(2 * Q_BLOCK)[None, :] >= Q_BLOCK)
    tables = [_bias_from_buckets(rel_bias, bucket, v).reshape(N_HEADS * Q_BLOCK, 2 * Q_BLOCK) for v in (valid, first)]
    return jnp.stack(tables)


def _sample_tables(rel_bias, w_buf):
    qpos = w_buf + np.arange(SAMPLE_T)[:, None]
    g = np.arange(w_buf // 16)
    res_pos = (16 * g[:, None] + np.arange(SAMPLE_RES)[None, :]).reshape(-1)
    tail_pos = w_buf - TAIL + np.arange(TAIL)
    new_pos = w_buf + np.arange(NEW_ROWS // N_HEADS)
    pos = np.concatenate([res_pos, tail_pos, new_pos])[None, :]
    kind = np.concatenate([np.zeros_like(res_pos), np.ones_like(tail_pos), np.full_like(new_pos, 2)])[None, :]
    dist = qpos - pos
    in_seq = (dist >= 0) & (pos < w_buf + SAMPLE_T)
    mult = np.zeros(dist.shape, np.float32)
    for (w, d) in PATTERNS:
        hit = in_seq & (dist % d == 0) & (dist <= w)
        hit &= (kind != 1) if d == 16 else (kind != 0)
        mult += hit
    same = np.eye(N_HEADS, dtype=bool)[None, :, None, :]
    mult4 = np.where(same, mult[:, None, :, None], 0.0)
    bucket = _bucket_np(np.maximum(dist, 0))
    bias_h = _bias_from_buckets(rel_bias, bucket, mult > 0)
    bias4 = jnp.where(jnp.asarray(same), jnp.transpose(bias_h, (1, 0, 2))[:, :, :, None], NEG)
    rows = SAMPLE_T * N_HEADS
    return bias4.reshape(rows, KEY_COLS), jnp.asarray(mult4.reshape(rows, KEY_COLS))


PROMPT_TM = 256
PROMPT_TE = 256
ATTN_SUB = 2


def _row_perm(tm):
    a = np.arange(tm)
    src = (a // Q_BLOCK) * Q_BLOCK + 16 * (a % SUBLANES) + (a % Q_BLOCK) // SUBLANES
    perm = np.zeros((tm, tm), np.float32)
    perm[a, src] = 1.0
    return perm


def kernel(x_prompt, x_sample, cache_k, cache_v, state_conv, p_prompt, p_sample, rel_bias, g_mix, w_in, q_gain,
           k_gain, conv_w, g_out_att, g_out_conv, w_out, g_ffn, w_router_group, w_router_expert, w_gate, w_up,
           w_down, g_ple, w_ple_gate, w_ple_proj):
    depth = w_in.shape[0]
    batch, seq, _ = x_prompt.shape
    dec_b, dec_t, _ = x_sample.shape
    w_buf = cache_k.shape[2]
    n_s = dec_b * dec_t
    keep = min(w_buf, seq)
    assert dec_t == SAMPLE_T and w_buf % TAIL == 0 and w_buf // 16 == N_KEYS
    assert seq % (Q_BLOCK * 16 * ATTN_SUB) == 0 and keep % PROMPT_TM == 0

    row = lambda a: a.reshape(1, -1)
    head_avg = jnp.asarray(np.kron(np.eye(N_HEADS), np.full((HEAD_DIM, HEAD_DIM), 1.0 / HEAD_DIM)), BF16)
    src_lane = np.arange(LANES)
    expand = jnp.asarray((src_lane[:, None] // LSE_LANES_PER_HEAD == np.arange(ATT_DIM)[None, :] // HEAD_DIM)
                         & (src_lane[:, None] % LSE_LANES_PER_HEAD == 0), BF16)
    perm_np = _row_perm(PROMPT_TM)
    perm, unperm = jnp.asarray(perm_np, BF16), jnp.asarray(perm_np.T, BF16)
    band = [_band_bias(rel_bias, d) for (_, d) in PATTERNS]
    s_bias, s_mult = _sample_tables(rel_bias, w_buf)

    hp = x_prompt.reshape(batch * seq, D_MODEL)
    hs = jnp.swapaxes(x_sample, 0, 1).reshape(n_s, D_MODEL)
    new = {k: [] for k in ("kp", "vp", "cp", "ks", "vs", "cs")}
    hist_p = jnp.zeros((batch, SUBLANES, CONV_DIM), F32)

    for l in range(depth):
        w_in_bf = w_in[l].astype(BF16)
        wa_bf = w_out[l, :ATT_DIM].astype(BF16)
        wc_bf = w_out[l, ATT_DIM:].astype(BF16)
        wgate_bf = w_ple_gate[l].astype(BF16)
        wproj_bf = w_ple_proj[l].astype(BF16)
        w_router = jnp.concatenate(
            [w_router_expert[l], w_router_group[l],
             jnp.zeros((D_MODEL, LANES - N_EXPERTS - N_GROUPS), F32)], axis=1)
        qg, kg = row(jnp.tile(q_gain[l], N_HEADS)), row(jnp.tile(k_gain[l], N_HEADS))
        mix = (row(g_mix[l]), w_in_bf, qg, kg, head_avg, conv_w[l], row(g_out_conv[l]))
        moe = (row(g_ffn[l]), w_router, w_gate[l], w_up[l], w_down[l], row(g_ple[l]), wgate_bf, wproj_bf)

        q, k, v, k_nat, v_nat, yn, nconv = _inproj(
            hp, hist_p, *mix, perm, tm=PROMPT_TM, shift=1, tiles_per_seq=seq // PROMPT_TM,
            keep_tiles=keep // PROMPT_TM)
        os, lses = [], []
        for bias, (_, d) in zip(band, PATTERNS):
            o, lse = _attn_pattern(q, k, v, bias, batch=batch, seq=seq, dil=d, sub=ATTN_SUB)
            os.append(o)
            lses.append(lse)
        hp = _outproj(os, lses, yn, hp, row(g_out_att[l]), wa_bf, wc_bf, expand, unperm, tm=PROMPT_TM)
        hp = _moe_ple(hp, p_prompt[l].reshape(batch * seq, D_PLE), *moe, tm=PROMPT_TM, te=PROMPT_TE)
        new["kp"].append(k_nat.reshape(batch, keep, N_HEADS, HEAD_DIM))
        new["vp"].append(v_nat.reshape(batch, keep, N_HEADS, HEAD_DIM))
        new["cp"].append(nconv[:, SUBLANES - 2:])

        hist_s = jnp.swapaxes(state_conv[l], 0, 1).reshape(1, 2 * dec_b, CONV_DIM)
        q, k, v, yn, nconv = _inproj(hs, hist_s, *mix, tm=n_s, shift=dec_b, tiles_per_seq=1)
        bmaj = lambda a: jnp.swapaxes(a.reshape(dec_t, dec_b, N_HEADS, HEAD_DIM), 0, 1)
        qb, kb, vb = bmaj(q), bmaj(k), bmaj(v)
        att = _attn_sample(qb, kb, vb, cache_k, cache_v, l, s_bias, s_mult)
        att_tm = jnp.swapaxes(att, 0, 1).reshape(n_s, ATT_DIM)
        hs = _outproj([att_tm], [], yn, hs, row(g_out_att[l]), wa_bf, wc_bf, tm=n_s)
        p_s = jnp.swapaxes(p_sample[l], 0, 1).reshape(n_s, D_PLE)
        hs = _moe_ple(hs, p_s, *moe, tm=n_s, te=LANES)
        new["ks"].append(kb)
        new["vs"].append(vb)
        new["cs"].append(jnp.swapaxes(nconv.reshape(2, dec_b, CONV_DIM), 0, 1))

    y_prompt = hp.reshape(batch, seq, D_MODEL)
    y_sample = jnp.swapaxes(hs.reshape(dec_t, dec_b, D_MODEL), 0, 1)
    st = lambda key: jnp.stack(new[key])
    return (y_prompt, y_sample, st("kp"), st("vp"), st("cp"), st("ks"), st("vs"), st("cs"))
```

```python
import functools

import jax
import jax.numpy as jnp
import numpy as np
from jax import lax
from jax.experimental import pallas as pl
from jax.experimental.pallas import tpu as pltpu

F32 = jnp.float32
BF16 = jnp.bfloat16
HIGHEST = lax.Precision.HIGHEST

D_MODEL = 1024
HEAD_DIM = 64
N_HEADS = 8
ATT_DIM = N_HEADS * HEAD_DIM
CONV_DIM = D_MODEL - ATT_DIM
MIX_IN = 3 * ATT_DIM + 3 * CONV_DIM
PATTERNS = ((128, 1), (512, 4), (2048, 16))
N_KEYS = 128
Q_BLOCK = 128
N_BUCKETS = 32
MAX_DISTANCE = 2048
N_GROUPS = 4
EXPERTS_PER_GROUP = 8
N_EXPERTS = N_GROUPS * EXPERTS_PER_GROUP
D_EXPERT = 256
D_PLE = 256
EPS = 1e-6
NEG = -1e30

LANES = 128
SUBLANES = 8
SLABS = Q_BLOCK // SUBLANES
LSE_LANES_PER_HEAD = LANES // N_HEADS
VMEM_LIMIT = 48 * 1024 * 1024
NT = (((1,), (1,)), ((), ()))


def _cparams(n_axes):
    return pltpu.CompilerParams(dimension_semantics=("arbitrary",) * n_axes,
                                vmem_limit_bytes=VMEM_LIMIT)


def _full(shape):
    n = len(shape)
    return pl.BlockSpec(shape, lambda *_: (0,) * n)


def _rms(x, gain):
    ms = jnp.mean(x * x, axis=-1, keepdims=True)
    return x * lax.rsqrt(ms + EPS) * gain


def _exact_dot(x, e_ref):
    hi = x.astype(BF16)
    r1 = x - hi.astype(F32)
    mid = r1.astype(BF16)
    lo = (r1 - mid.astype(F32)).astype(BF16)
    e = e_ref[...]
    return (jnp.dot(hi, e, preferred_element_type=F32) + jnp.dot(mid, e, preferred_element_type=F32)
            + jnp.dot(lo, e, preferred_element_type=F32))


def _inproj_kernel(*refs, shift, tiles_per_seq, permute):
    (h_ref, gmix_ref, w_ref, qg_ref, kg_ref, hmean_ref, hexp_ref, cw_ref, gconv_ref, hist_ref) = refs[:10]
    if permute:
        perm_ref, q_ref, k_ref, v_ref, kn_ref, vn_ref, yn_ref, nconv_ref, carry_ref = refs[10:]
    else:
        q_ref, k_ref, v_ref, yn_ref, nconv_ref, carry_ref = refs[10:]
    i = pl.program_id(0)
    a = _rms(h_ref[...], gmix_ref[...])
    proj = jnp.dot(a.astype(BF16), w_ref[...], preferred_element_type=F32)
    tm = proj.shape[0]

    def head_norm(t, g):
        ms = _exact_dot(t * t, hmean_ref)
        return t * _exact_dot(lax.rsqrt(ms + EPS), hexp_ref) * g

    q = head_norm(proj[:, 0:ATT_DIM], qg_ref[...])
    k = head_norm(proj[:, ATT_DIM:2 * ATT_DIM], kg_ref[...])
    v = proj[:, 2 * ATT_DIM:3 * ATT_DIM]
    if permute:
        kn_ref[...] = k.T.reshape(N_HEADS, HEAD_DIM, tm)
        vn_ref[...] = v.T.reshape(N_HEADS, HEAD_DIM, tm)
        qkv = jnp.concatenate([q * (HEAD_DIM ** -0.5), k, v], axis=-1).astype(BF16)
        moved = jnp.dot(perm_ref[...], qkv, preferred_element_type=F32)
        q_ref[...] = moved[:, 0:ATT_DIM]
        k_ref[...] = moved[:, ATT_DIM:2 * ATT_DIM]
        v_ref[...] = moved[:, 2 * ATT_DIM:3 * ATT_DIM]
    else:
        q_ref[...] = q
        k_ref[...] = k
        v_ref[...] = v
    c0 = 3 * ATT_DIM
    hc = proj[:, c0:c0 + CONV_DIM]
    gb = proj[:, c0 + CONV_DIM:c0 + 2 * CONV_DIM]
    gc = proj[:, c0 + 2 * CONV_DIM:c0 + 3 * CONV_DIM]
    u = gc * hc

    if shift == 1:
        @pl.when(i % tiles_per_seq == 0)
        def _():
            carry_ref[...] = hist_ref[0]
        h0 = carry_ref[SUBLANES - 2:SUBLANES - 1, :]
        h1 = carry_ref[SUBLANES - 1:SUBLANES, :]
        row = lax.broadcasted_iota(jnp.int32, (tm, 1), 0)
        u1 = jnp.where(row == 0, h1, pltpu.roll(u, 1, 0))
        u2 = jnp.where(row == 0, h0, jnp.where(row == 1, h1, pltpu.roll(u, 2, 0)))
        carry_ref[...] = u[tm - SUBLANES:tm, :]
        nconv_ref[0] = u[tm - SUBLANES:tm, :]
    else:
        hist = hist_ref[0]
        u1 = jnp.concatenate([hist[shift:2 * shift], u[0:tm - shift]], axis=0)
        u2 = jnp.concatenate([hist, u[0:tm - 2 * shift]], axis=0)
        nconv_ref[0] = u[tm - 2 * shift:tm, :]
    conv = cw_ref[0:1, :] * u2 + cw_ref[1:2, :] * u1 + cw_ref[2:3, :] * u
    yn_ref[...] = _rms(gb * conv, gconv_ref[...])


def _inproj(h, hist, g_mix, w_in_bf, q_gain, k_gain, hmean, hexp, conv_w, g_out_conv, perm=None, *,
            tm, shift, tiles_per_seq, keep_tiles=0):
    n = h.shape[0]
    hist_rows = hist.shape[1]
    nseq = hist.shape[0]
    tok = lambda w: pl.BlockSpec((tm, w), lambda i: (i, 0))
    seq3 = lambda r: pl.BlockSpec((1, r, CONV_DIM), lambda i: (i // tiles_per_seq, 0, 0))
    nconv_rows = SUBLANES if shift == 1 else 2 * shift
    att = jax.ShapeDtypeStruct((n, ATT_DIM), F32)
    in_specs = [tok(D_MODEL), _full((1, D_MODEL)), _full((D_MODEL, MIX_IN)), _full((1, ATT_DIM)),
                _full((1, ATT_DIM)), _full((ATT_DIM, LANES)), _full((LANES, ATT_DIM)), _full((3, CONV_DIM)),
                _full((1, CONV_DIM)), seq3(hist_rows)]
    args = [h, g_mix, w_in_bf, q_gain, k_gain, hmean, hexp, conv_w, g_out_conv, hist]
    out_specs = [tok(ATT_DIM)] * 3
    out_shape = [att] * 3
    if perm is not None:
        in_specs.append(_full((tm, tm)))
        args.append(perm)
        first = tiles_per_seq - keep_tiles
        kept = pl.BlockSpec((None, N_HEADS, HEAD_DIM, tm),
                            lambda i: (i // tiles_per_seq, 0, 0, jnp.maximum(i % tiles_per_seq - first, 0)))
        out_specs += [kept, kept]
        out_shape += [jax.ShapeDtypeStruct((nseq, N_HEADS, HEAD_DIM, keep_tiles * tm), F32)] * 2
    out_specs += [tok(CONV_DIM), seq3(nconv_rows)]
    out_shape += [jax.ShapeDtypeStruct((n, CONV_DIM), F32),
                  jax.ShapeDtypeStruct((nseq, nconv_rows, CONV_DIM), F32)]
    return pl.pallas_call(
        functools.partial(_inproj_kernel, shift=shift, tiles_per_seq=tiles_per_seq, permute=perm is not None),
        grid=(n // tm,),
        in_specs=in_specs, out_specs=out_specs, out_shape=out_shape,
        scratch_shapes=[pltpu.VMEM((SUBLANES, CONV_DIM), F32)],
        compiler_params=_cparams(1),
        name="inproj",
    )(*args)


def _attn_block(q, kk, vv, bias):
    lane = lax.broadcasted_iota(jnp.int32, (Q_BLOCK, LANES), 1)
    upper = lane >= HEAD_DIM
    scores = []
    for h in range(N_HEADS):
        j, e = divmod(h, 2)
        qp = q[:, j * LANES:(j + 1) * LANES]
        qm = (jnp.where(upper, qp, 0.0) if e else jnp.where(upper, 0.0, qp)).astype(BF16)
        scores.append(lax.dot_general(qm, kk[:, j * LANES:(j + 1) * LANES], NT, preferred_element_type=F32))
    s = jnp.concatenate(scores, axis=0) + bias
    m = jnp.max(s, axis=-1, keepdims=True)
    p = jnp.exp(s - m)
    den = jnp.sum(p, axis=-1, keepdims=True)
    pb = p.astype(BF16)
    inv = 1.0 / den
    lse = m + jnp.log(den)
    lse_grp = lane // LSE_LANES_PER_HEAD
    lse_tile = jnp.zeros((Q_BLOCK, LANES), F32)
    outs = []
    for j in range(N_HEADS // 2):
        pair = None
        for e in range(2):
            h = 2 * j + e
            rows = slice(h * Q_BLOCK, (h + 1) * Q_BLOCK)
            o = jnp.dot(pb[rows], vv[:, j * LANES:(j + 1) * LANES], preferred_element_type=F32) * inv[rows]
            pair = o if e == 0 else jnp.where(upper, o, pair)
            lse_tile = jnp.where(lse_grp == h, lse[rows], lse_tile)
        outs.append(pair)
    return jnp.concatenate(outs, axis=-1), lse_tile


def _attn_kernel(q_ref, kp_ref, kc_ref, vp_ref, vc_ref, bias_ref, o_ref, lse_ref, kbuf, vbuf, *, sub):
    n = pl.program_id(2)
    rows = sub * Q_BLOCK
    kbuf[0:Q_BLOCK, :] = kp_ref[...].reshape(Q_BLOCK, ATT_DIM).astype(BF16)
    kbuf[Q_BLOCK:, :] = kc_ref[...].reshape(rows, ATT_DIM).astype(BF16)
    vbuf[0:Q_BLOCK, :] = vp_ref[...].reshape(Q_BLOCK, ATT_DIM).astype(BF16)
    vbuf[Q_BLOCK:, :] = vc_ref[...].reshape(rows, ATT_DIM).astype(BF16)

    for j in range(sub):
        q = q_ref[j].reshape(Q_BLOCK, ATT_DIM)
        r0 = j * Q_BLOCK
        first = jnp.logical_and(n == 0, j == 0).astype(jnp.int32) if j == 0 else 0
        o, lse = _attn_block(q, kbuf[r0:r0 + 2 * Q_BLOCK, :], vbuf[r0:r0 + 2 * Q_BLOCK, :], bias_ref[first])
        o_ref[j] = o.reshape(SLABS, SUBLANES, ATT_DIM)
        lse_ref[j] = lse.reshape(SLABS, SUBLANES, LANES)


def _attn_pattern(q, k, v, bias, *, batch, seq, dil, sub):
    nblk = seq // (Q_BLOCK * dil)
    view = lambda t: t.reshape(batch, nblk, SLABS, dil, SUBLANES, t.shape[-1])
    cur = lambda c: pl.BlockSpec((None, sub, SLABS, None, SUBLANES, c), lambda b, r, n: (b, n, 0, r, 0, 0))
    prev = pl.BlockSpec((None, None, SLABS, None, SUBLANES, ATT_DIM),
                        lambda b, r, n: (b, jnp.maximum(n * sub - 1, 0), 0, r, 0, 0))
    o, lse = pl.pallas_call(
        functools.partial(_attn_kernel, sub=sub),
        grid=(batch, dil, nblk // sub),
        in_specs=[cur(ATT_DIM), prev, cur(ATT_DIM), prev, cur(ATT_DIM),
                  _full((2, N_HEADS * Q_BLOCK, 2 * Q_BLOCK))],
        out_specs=[cur(ATT_DIM), cur(LANES)],
        out_shape=[jax.ShapeDtypeStruct((batch, nblk, SLABS, dil, SUBLANES, ATT_DIM), F32),
                   jax.ShapeDtypeStruct((batch, nblk, SLABS, dil, SUBLANES, LANES), F32)],
        scratch_shapes=[pltpu.VMEM(((sub + 1) * Q_BLOCK, ATT_DIM), BF16),
                        pltpu.VMEM(((sub + 1) * Q_BLOCK, ATT_DIM), BF16)],
        compiler_params=_cparams(3),
        name=f"attn_d{dil}",
    )(view(q), view(k), view(k), view(v), view(v), bias)
    return o.reshape(batch * seq, ATT_DIM), lse.reshape(batch * seq, LANES)


SAMPLE_T = 4
NEW_COLS = LANES


def _attn_sample_kernel(q_ref, kt_ref, ktn_ref, vt_ref, vtn_ref, bias_ref, mult_ref, o_ref):
    rows = SAMPLE_T * N_HEADS
    q4 = q_ref[...] * (HEAD_DIM ** -0.5)
    qt = jnp.concatenate([jnp.broadcast_to(q4[t:t + 1, :], (N_HEADS, ATT_DIM)) for t in range(SAMPLE_T)], axis=0)
    lane_head = lax.broadcasted_iota(jnp.int32, (rows, ATT_DIM), 1) // HEAD_DIM
    row_head = lax.broadcasted_iota(jnp.int32, (rows, ATT_DIM), 0) % N_HEADS
    own = lane_head == row_head
    qbd = jnp.where(own, qt, 0.0).astype(BF16)
    flat = lambda ref: ref[...].reshape(ATT_DIM, ref.shape[-1]).astype(BF16)
    s = jnp.concatenate([jnp.dot(qbd, flat(kt_ref), preferred_element_type=F32),
                         jnp.dot(qbd, flat(ktn_ref), preferred_element_type=F32)], axis=-1) + bias_ref[...]
    m = jnp.max(s, axis=-1, keepdims=True)
    p = jnp.exp(s - m) * mult_ref[...]
    den = jnp.sum(p, axis=-1, keepdims=True)
    pb = p.astype(BF16)
    w_buf = kt_ref.shape[-1]
    acc = (lax.dot_general(pb[:, :w_buf], flat(vt_ref), NT, preferred_element_type=F32)
           + lax.dot_general(pb[:, w_buf:], flat(vtn_ref), NT, preferred_element_type=F32))
    acc = jnp.where(own, acc / den, 0.0)
    for t in range(SAMPLE_T):
        o_ref[t:t + 1, :] = jnp.sum(acc[t * N_HEADS:(t + 1) * N_HEADS, :], axis=0, keepdims=True)


def _attn_sample(q, kt_new, vt_new, cache_kt, cache_vt, layer, bias, mult):
    nb = q.shape[0]
    w_buf = cache_kt.shape[-1]
    tok = pl.BlockSpec((None, SAMPLE_T, ATT_DIM), lambda b: (b, 0, 0))
    new = pl.BlockSpec((None, N_HEADS, HEAD_DIM, NEW_COLS), lambda b: (b, 0, 0, 0))
    old = pl.BlockSpec((None, None, N_HEADS, HEAD_DIM, w_buf), lambda b: (layer, b, 0, 0, 0))
    tbl = _full((SAMPLE_T * N_HEADS, w_buf + NEW_COLS))
    return pl.pallas_call(
        _attn_sample_kernel,
        grid=(nb,),
        in_specs=[tok, old, new, old, new, tbl, tbl],
        out_specs=tok,
        out_shape=jax.ShapeDtypeStruct((nb, SAMPLE_T, ATT_DIM), F32),
        compiler_params=_cparams(1),
        name="attn_sample",
    )(q, cache_kt, kt_new, cache_vt, vt_new, bias, mult)


def _split_dot(x, e_ref):
    hi = x.astype(BF16)
    lo = (x - hi.astype(F32)).astype(BF16)
    return (jnp.dot(hi, e_ref[...], preferred_element_type=F32)
            + jnp.dot(lo, e_ref[...], preferred_element_type=F32))


def _outproj_kernel(*refs, n_pat):
    mix = n_pat > 1
    n_lse = n_pat if mix else 0
    o_refs = refs[0:n_pat]
    l_refs = refs[n_pat:n_pat + n_lse]
    rest = refs[n_pat + n_lse:]
    if mix:
        yn_ref, h_ref, gatt_ref, exp_ref, unperm_ref, wa_ref, wc_ref, out_ref = rest
        lses = [r[...] for r in l_refs]
        top = functools.reduce(jnp.maximum, lses)
        ws = [jnp.exp(l - top) for l in lses]
        tot = functools.reduce(lambda a, b: a + b, ws)
        att = None
        for w, o_ref in zip(ws, o_refs):
            term = _split_dot(w / tot, exp_ref) * o_ref[...]
            att = term if att is None else att + term
        att_bf = jnp.dot(unperm_ref[...], _rms(att, gatt_ref[...]).astype(BF16),
                         preferred_element_type=F32).astype(BF16)
    else:
        yn_ref, h_ref, gatt_ref, wa_ref, wc_ref, out_ref = rest
        att_bf = _rms(o_refs[0][...], gatt_ref[...]).astype(BF16)
    y = (jnp.dot(att_bf, wa_ref[...], preferred_element_type=F32)
         + jnp.dot(yn_ref[...].astype(BF16), wc_ref[...], preferred_element_type=F32))
    out_ref[...] = h_ref[...] + y


def _outproj(os, lses, yn, h, g_att, wa_bf, wc_bf, expand=None, unperm=None, *, tm):
    n = h.shape[0]
    n_pat = len(os)
    tok = lambda w: pl.BlockSpec((tm, w), lambda i: (i, 0))
    in_specs = [tok(ATT_DIM)] * n_pat + [tok(LANES)] * len(lses) + [tok(CONV_DIM), tok(D_MODEL), _full((1, ATT_DIM))]
    args = [*os, *lses, yn, h, g_att]
    if n_pat > 1:
        in_specs += [_full((LANES, ATT_DIM)), _full((tm, tm))]
        args += [expand, unperm]
    in_specs += [_full((ATT_DIM, D_MODEL)), _full((CONV_DIM, D_MODEL))]
    args += [wa_bf, wc_bf]
    return pl.pallas_call(
        functools.partial(_outproj_kernel, n_pat=n_pat),
        grid=(n // tm,),
        in_specs=in_specs,
        out_specs=tok(D_MODEL),
        out_shape=jax.ShapeDtypeStruct((n, D_MODEL), F32),
        compiler_params=_cparams(1),
        name="outproj",
    )(*args)


ROUTE_I1, ROUTE_I2, ROUTE_R1, ROUTE_R2, ROUTE_W1, ROUTE_W2 = range(6)
GROUP_LANE0 = N_EXPERTS


def _route_kernel(h_ref, g_ref, wr_ref, route_ref, cnt_ref, carry_ref):
    i = pl.program_id(0)

    @pl.when(i == 0)
    def _():
        carry_ref[...] = jnp.zeros_like(carry_ref)

    m = _rms(h_ref[...], g_ref[...])
    logits = jnp.dot(m.astype(BF16), wr_ref[...], preferred_element_type=F32)
    tm = logits.shape[0]
    lane_i = lax.broadcasted_iota(jnp.int32, (tm, LANES), 1)
    lane = lane_i.astype(F32)
    big = jnp.float32(4 * LANES)

    is_g = jnp.logical_and(lane_i >= GROUP_LANE0, lane_i < GROUP_LANE0 + N_GROUPS)
    gl = jnp.where(is_g, logits, NEG)
    gmax = jnp.max(gl, axis=-1, keepdims=True)
    g_w = 1.0 / jnp.sum(jnp.where(is_g, jnp.exp(gl - gmax), 0.0), axis=-1, keepdims=True)
    g_sel = jnp.min(jnp.where(gl == gmax, lane - GROUP_LANE0, big), axis=-1, keepdims=True)

    grp_of_lane = (lane_i // EXPERTS_PER_GROUP).astype(F32)
    in_grp = jnp.logical_and(lane_i < N_EXPERTS, grp_of_lane == g_sel)
    el = jnp.where(in_grp, logits, NEG)
    t1 = jnp.max(el, axis=-1, keepdims=True)
    i1 = jnp.min(jnp.where(el == t1, lane, big), axis=-1, keepdims=True)
    el2 = jnp.where(lane == i1, NEG, el)
    t2 = jnp.max(el2, axis=-1, keepdims=True)
    i2 = jnp.min(jnp.where(el2 == t2, lane, big), axis=-1, keepdims=True)
    e2 = jnp.exp(t2 - t1)
    w1 = g_w / (1.0 + e2)
    w2 = g_w * e2 / (1.0 + e2)

    hit1 = lane == i1
    hit2 = lane == i2
    c = jnp.where(jnp.logical_or(hit1, hit2), 1.0, 0.0)
    rr = lax.broadcasted_iota(jnp.int32, (tm, tm), 0)
    cc = lax.broadcasted_iota(jnp.int32, (tm, tm), 1)
    lower = jnp.where(rr > cc, 1.0, 0.0).astype(BF16)
    before = jnp.dot(lower, c.astype(BF16), preferred_element_type=F32) + carry_ref[0:1, :]
    r1 = jnp.sum(jnp.where(hit1, before, 0.0), axis=-1, keepdims=True)
    r2 = jnp.sum(jnp.where(hit2, before, 0.0), axis=-1, keepdims=True)
    total = carry_ref[0:1, :] + jnp.sum(c, axis=0, keepdims=True)
    carry_ref[...] = jnp.broadcast_to(total, carry_ref.shape)
    cnt_ref[...] = jnp.broadcast_to(total, cnt_ref.shape)

    rec = jnp.zeros((tm, LANES), F32)
    for idx, val in ((ROUTE_I1, i1), (ROUTE_I2, i2), (ROUTE_R1, r1), (ROUTE_R2, r2), (ROUTE_W1, w1), (ROUTE_W2, w2)):
        rec = jnp.where(lane_i == idx, val, rec)
    route_ref[...] = rec


def _route(h, g_ffn, w_router, *, tm):
    n = h.shape[0]
    tok = lambda w: pl.BlockSpec((tm, w), lambda i: (i, 0))
    return pl.pallas_call(
        _route_kernel,
        grid=(n // tm,),
        in_specs=[tok(D_MODEL), _full((1, D_MODEL)), _full((D_MODEL, LANES))],
        out_specs=[tok(LANES), _full((SUBLANES, LANES))],
        out_shape=[jax.ShapeDtypeStruct((n, LANES), F32), jax.ShapeDtypeStruct((SUBLANES, LANES), F32)],
        scratch_shapes=[pltpu.VMEM((SUBLANES, LANES), F32)],
        compiler_params=_cparams(1),
        name="route",
    )(h, g_ffn, w_router)


def _dispatch_kernel(pos_ref, h_ref, g_ref, xs_ref, mbuf, sem):
    tm = mbuf.shape[0]
    mbuf[...] = _rms(h_ref[...], g_ref[...])

    def row_copy(t, p):
        return pltpu.make_async_copy(mbuf.at[pl.ds(t, 1), :], xs_ref.at[pl.ds(p, 1), :], sem)

    def issue(t, carry):
        row_copy(t, pos_ref[0, 0, 2 * t]).start()
        row_copy(t, pos_ref[0, 0, 2 * t + 1]).start()
        return carry

    lax.fori_loop(0, tm, issue, 0)

    def drain(t, carry):
        row_copy(0, 0).wait()
        row_copy(0, 0).wait()
        return carry

    lax.fori_loop(0, tm, drain, 0)


def _dispatch(h, g_ffn, pos, n_rows, *, tm):
    n = h.shape[0]
    return pl.pallas_call(
        _dispatch_kernel,
        grid=(n // tm,),
        in_specs=[pl.BlockSpec((1, 1, 2 * tm), lambda i: (i, 0, 0), memory_space=pltpu.SMEM),
                  pl.BlockSpec((tm, D_MODEL), lambda i: (i, 0)), _full((1, D_MODEL))],
        out_specs=pl.BlockSpec(memory_space=pl.ANY),
        out_shape=jax.ShapeDtypeStruct((n_rows, D_MODEL), F32),
        scratch_shapes=[pltpu.VMEM((tm, D_MODEL), F32), pltpu.SemaphoreType.DMA(())],
        compiler_params=pltpu.CompilerParams(dimension_semantics=("arbitrary",), vmem_limit_bytes=VMEM_LIMIT,
                                             has_side_effects=True),
        name="dispatch",
    )(pos, h, g_ffn)


def _expert_kernel(te_ref, tv_ref, x_ref, wg_ref, wu_ref, wd_ref, y_ref, wg_bf, wu_bf, wd_bf):
    i = pl.program_id(0)
    valid = tv_ref[i]
    changed = jnp.logical_or(i == 0, te_ref[i] != te_ref[jnp.maximum(i - 1, 0)])

    @pl.when(jnp.logical_and(changed, valid > 0))
    def _():
        wg_bf[...] = wg_ref[0].astype(BF16)
        wu_bf[...] = wu_ref[0].astype(BF16)
        wd_bf[...] = wd_ref[0].astype(BF16)

    @pl.when(valid > 0)
    def _():
        te = x_ref.shape[0]
        row = lax.broadcasted_iota(jnp.int32, (te, 1), 0)
        x = jnp.where(row < valid, x_ref[...], 0.0).astype(BF16)
        hg = jnp.dot(x, wg_bf[...], preferred_element_type=F32)
        hu = jnp.dot(x, wu_bf[...], preferred_element_type=F32)
        hid = (hg * jax.nn.sigmoid(hg)) * hu
        y_ref[...] = jnp.dot(hid.astype(BF16), wd_bf[...], preferred_element_type=F32)

    @pl.when(valid <= 0)
    def _():
        y_ref[...] = jnp.zeros_like(y_ref)


def _experts(xs, tile_expert, tile_valid, w_gate, w_up, w_down, *, te):
    n_rows = xs.shape[0]
    wspec = lambda a, b: pl.BlockSpec((1, a, b), lambda i, e, v: (e[i], 0, 0))
    grid_spec = pltpu.PrefetchScalarGridSpec(
        num_scalar_prefetch=2,
        grid=(n_rows // te,),
        in_specs=[pl.BlockSpec((te, D_MODEL), lambda i, e, v: (i, 0)),
                  wspec(D_MODEL, D_EXPERT), wspec(D_MODEL, D_EXPERT), wspec(D_EXPERT, D_MODEL)],
        out_specs=pl.BlockSpec((te, D_MODEL), lambda i, e, v: (i, 0)),
        scratch_shapes=[pltpu.VMEM((D_MODEL, D_EXPERT), BF16), pltpu.VMEM((D_MODEL, D_EXPERT), BF16),
                        pltpu.VMEM((D_EXPERT, D_MODEL), BF16)])
    return pl.pallas_call(
        _expert_kernel,
        grid_spec=grid_spec,
        out_shape=jax.ShapeDtypeStruct((n_rows, D_MODEL), F32),
        compiler_params=_cparams(1),
        name="experts",
    )(tile_expert, tile_valid, xs, w_gate, w_up, w_down)


def _combine_kernel(pos_ref, h_ref, route_ref, ys_ref, p_ref, gple_ref, wgate_ref, wproj_ref, out_ref,
                    y0, y1, sem):
    tm = y0.shape[0]

    def row_copy(p, dst, t):
        return pltpu.make_async_copy(ys_ref.at[pl.ds(p, 1), :], dst.at[pl.ds(t, 1), :], sem)

    def issue(t, carry):
        row_copy(pos_ref[0, 0, 2 * t], y0, t).start()
        row_copy(pos_ref[0, 0, 2 * t + 1], y1, t).start()
        return carry

    lax.fori_loop(0, tm, issue, 0)

    def drain(t, carry):
        row_copy(0, y0, 0).wait()
        row_copy(0, y1, 0).wait()
        return carry

    lax.fori_loop(0, tm, drain, 0)

    rec = route_ref[...]
    w1 = rec[:, ROUTE_W1:ROUTE_W1 + 1]
    w2 = rec[:, ROUTE_W2:ROUTE_W2 + 1]
    h2 = h_ref[...] + w1 * y0[...] + w2 * y1[...]
    gate = jax.nn.sigmoid(jnp.dot(_rms(h2, gple_ref[...]).astype(BF16), wgate_ref[...], preferred_element_type=F32))
    ple = jnp.dot(p_ref[...].astype(BF16), wproj_ref[...], preferred_element_type=F32)
    out_ref[...] = h2 + ple * gate


def _combine(h, route, pos, ys, p, g_ple, wgate_bf, wproj_bf, *, tm):
    n = h.shape[0]
    tok = lambda w: pl.BlockSpec((tm, w), lambda i: (i, 0))
    return pl.pallas_call(
        _combine_kernel,
        grid=(n // tm,),
        in_specs=[pl.BlockSpec((1, 1, 2 * tm), lambda i: (i, 0, 0), memory_space=pltpu.SMEM),
                  tok(D_MODEL), tok(LANES), pl.BlockSpec(memory_space=pl.ANY), tok(D_PLE),
                  _full((1, D_MODEL)), _full((D_MODEL, D_MODEL)), _full((D_PLE, D_MODEL))],
        out_specs=tok(D_MODEL),
        out_shape=jax.ShapeDtypeStruct((n, D_MODEL), F32),
        scratch_shapes=[pltpu.VMEM((tm, D_MODEL), F32), pltpu.VMEM((tm, D_MODEL), F32),
                        pltpu.SemaphoreType.DMA(())],
        compiler_params=_cparams(1),
        name="combine",
    )(pos, h, route, ys, p, g_ple, wgate_bf, wproj_bf)


def _lookup(table, idx, size):
    hit = idx[..., None] == jnp.arange(size, dtype=jnp.int32)
    return jnp.sum(jnp.where(hit, table, 0), axis=-1)


def _moe_ple(h, p, g_ffn, w_router, w_gate, w_up, w_down, g_ple, wgate_bf, wproj_bf, *, tm, te):
    n = h.shape[0]
    route, counts = _route(h, g_ffn, w_router, tm=tm)
    counts = counts[0, :N_EXPERTS].astype(jnp.int32)
    padded = ((counts + te - 1) // te) * te
    ends = jnp.cumsum(padded)
    offs = ends - padded
    ids = route[:, ROUTE_I1:ROUTE_I2 + 1].astype(jnp.int32)
    ranks = route[:, ROUTE_R1:ROUTE_R2 + 1].astype(jnp.int32)
    pos = (_lookup(offs, ids, N_EXPERTS) + ranks).reshape(n // tm, 1, 2 * tm)
    n_tiles = (2 * n) // te + N_EXPERTS
    starts = jnp.arange(n_tiles, dtype=jnp.int32) * te
    tile_expert = jnp.minimum(jnp.sum((starts[:, None] >= ends[None, :]).astype(jnp.int32), axis=-1), N_EXPERTS - 1)
    tile_valid = jnp.clip(_lookup(counts, tile_expert, N_EXPERTS)
                          - (starts - _lookup(offs, tile_expert, N_EXPERTS)), 0, te).astype(jnp.int32)
    xs = _dispatch(h, g_ffn, pos, n_tiles * te, tm=tm)
    ys = _experts(xs, tile_expert, tile_valid, w_gate, w_up, w_down, te=te)
    return _combine(h, route, pos, ys, p, g_ple, wgate_bf, wproj_bf, tm=tm)


def _bucket_np(dist):
    max_exact = N_BUCKETS // 2
    d_f = np.maximum(dist, 1).astype(np.float32)
    large = max_exact + (np.log(d_f / np.float32(max_exact)) / np.float32(np.log(MAX_DISTANCE / max_exact))
                         * np.float32(N_BUCKETS - max_exact)).astype(np.int32)
    large = np.minimum(large, N_BUCKETS - 1)
    return np.where(dist < max_exact, dist, large).astype(np.int32)


def _bias_from_buckets(rel_bias, bucket, valid):
    onehot = (jnp.asarray(bucket)[..., None] == jnp.arange(N_BUCKETS, dtype=jnp.int32)).astype(F32)
    bias = jnp.einsum("...k,kh->h...", onehot, rel_bias.astype(F32), precision=HIGHEST)
    return jnp.where(jnp.asarray(valid)[None], bias, NEG)


def _block_order(dil):
    g = np.arange(Q_BLOCK) // SUBLANES
    j = np.arange(Q_BLOCK) % SUBLANES
    if dil == 1:
        return 16 * j + g
    if dil == 4:
        return 32 * (g // 4) + 4 * j + g % 4
    return SUBLANES * g + j


def _band_bias(rel_bias, dil):
    mu = _block_order(dil)
    qi = mu[:, None] + Q_BLOCK
    ki = np.concatenate([mu, mu + Q_BLOCK])[None, :]
    off = qi - ki
    valid = (off >= 0) & (off <= N_KEYS)
    bucket = _bucket_np(dil * np.clip(off, 0, N_KEYS))
    first = valid & (np.arange(2 * Q_BLOCK)[None, :] >= Q_BLOCK)
    tables = [_bias_from_buckets(rel_bias, bucket, v).reshape(N_HEADS * Q_BLOCK, 2 * Q_BLOCK) for v in (valid, first)]
    return jnp.stack(tables)


def _sample_tables(rel_bias, w_buf):
    qpos = w_buf + np.arange(SAMPLE_T)[:, None]
    pos = np.arange(w_buf + NEW_COLS)[None, :]
    dist = qpos - pos
    in_seq = (dist >= 0) & (pos < w_buf + SAMPLE_T)
    mult = np.zeros(dist.shape, np.float32)
    for (w, d) in PATTERNS:
        mult += in_seq & (dist % d == 0) & (dist <= w)
    bucket = _bucket_np(np.maximum(dist, 0))
    bias = jnp.transpose(_bias_from_buckets(rel_bias, bucket, mult > 0), (1, 0, 2))
    rows = SAMPLE_T * N_HEADS
    mult_rows = np.broadcast_to(mult[:, None, :], (SAMPLE_T, N_HEADS, mult.shape[-1]))
    return bias.reshape(rows, -1), jnp.asarray(mult_rows.reshape(rows, -1))


PROMPT_TM = 256
PROMPT_TE = 256
ATTN_SUB = 2


def _row_perm(tm):
    a = np.arange(tm)
    src = (a // Q_BLOCK) * Q_BLOCK + 16 * (a % SUBLANES) + (a % Q_BLOCK) // SUBLANES
    perm = np.zeros((tm, tm), np.float32)
    perm[a, src] = 1.0
    return perm


def kernel(x_prompt, x_sample, cache_k, cache_v, state_conv, p_prompt, p_sample, rel_bias, g_mix, w_in, q_gain,
           k_gain, conv_w, g_out_att, g_out_conv, w_out, g_ffn, w_router_group, w_router_expert, w_gate, w_up,
           w_down, g_ple, w_ple_gate, w_ple_proj):
    depth = w_in.shape[0]
    batch, seq, _ = x_prompt.shape
    dec_b, dec_t, _ = x_sample.shape
    w_buf = cache_k.shape[2]
    n_s = dec_b * dec_t
    keep = min(w_buf, seq)
    assert dec_t == SAMPLE_T and w_buf % LANES == 0
    assert seq % (Q_BLOCK * 16 * ATTN_SUB) == 0 and keep % PROMPT_TM == 0
    cache_kt = jnp.transpose(cache_k, (0, 1, 3, 4, 2))
    cache_vt = jnp.transpose(cache_v, (0, 1, 3, 4, 2))

    row = lambda a: a.reshape(1, -1)
    member = (np.arange(ATT_DIM)[:, None] // HEAD_DIM == np.arange(LANES)[None, :])
    head_mean = jnp.asarray(member / HEAD_DIM, BF16)
    head_exp = jnp.asarray(member.T, BF16)
    src_lane = np.arange(LANES)
    expand = jnp.asarray((src_lane[:, None] // LSE_LANES_PER_HEAD == np.arange(ATT_DIM)[None, :] // HEAD_DIM)
                         & (src_lane[:, None] % LSE_LANES_PER_HEAD == 0), BF16)
    perm_np = _row_perm(PROMPT_TM)
    perm, unperm = jnp.asarray(perm_np, BF16), jnp.asarray(perm_np.T, BF16)
    band = [_band_bias(rel_bias, d) for (_, d) in PATTERNS]
    s_bias, s_mult = _sample_tables(rel_bias, w_buf)

    hp = x_prompt.reshape(batch * seq, D_MODEL)
    hs = jnp.swapaxes(x_sample, 0, 1).reshape(n_s, D_MODEL)
    new = {k: [] for k in ("kp", "vp", "cp", "ks", "vs", "cs")}
    hist_p = jnp.zeros((batch, SUBLANES, CONV_DIM), F32)

    for l in range(depth):
        w_in_bf = w_in[l].astype(BF16)
        wa_bf = w_out[l, :ATT_DIM].astype(BF16)
        wc_bf = w_out[l, ATT_DIM:].astype(BF16)
        wgate_bf = w_ple_gate[l].astype(BF16)
        wproj_bf = w_ple_proj[l].astype(BF16)
        w_router = jnp.concatenate(
            [w_router_expert[l], w_router_group[l],
             jnp.zeros((D_MODEL, LANES - N_EXPERTS - N_GROUPS), F32)], axis=1).astype(BF16)
        qg, kg = row(jnp.tile(q_gain[l], N_HEADS)), row(jnp.tile(k_gain[l], N_HEADS))
        mix = (row(g_mix[l]), w_in_bf, qg, kg, head_mean, head_exp, conv_w[l], row(g_out_conv[l]))
        moe = (row(g_ffn[l]), w_router, w_gate[l], w_up[l], w_down[l], row(g_ple[l]), wgate_bf, wproj_bf)

        q, k, v, k_nat, v_nat, yn, nconv = _inproj(
            hp, hist_p, *mix, perm, tm=PROMPT_TM, shift=1, tiles_per_seq=seq // PROMPT_TM,
            keep_tiles=keep // PROMPT_TM)
        os, lses = [], []
        for bias, (_, d) in zip(band, PATTERNS):
            o, lse = _attn_pattern(q, k, v, bias, batch=batch, seq=seq, dil=d, sub=ATTN_SUB)
            os.append(o)
            lses.append(lse)
        hp = _outproj(os, lses, yn, hp, row(g_out_att[l]), wa_bf, wc_bf, expand, unperm, tm=PROMPT_TM)
        hp = _moe_ple(hp, p_prompt[l].reshape(batch * seq, D_PLE), *moe, tm=PROMPT_TM, te=PROMPT_TE)
        new["kp"].append(jnp.transpose(k_nat, (0, 3, 1, 2)))
        new["vp"].append(jnp.transpose(v_nat, (0, 3, 1, 2)))
        new["cp"].append(nconv[:, SUBLANES - 2:])

        hist_s = jnp.swapaxes(state_conv[l], 0, 1).reshape(1, 2 * dec_b, CONV_DIM)
        q, k, v, yn, nconv = _inproj(hs, hist_s, *mix, tm=n_s, shift=dec_b, tiles_per_seq=1)
        bmaj = lambda a: jnp.swapaxes(a.reshape(dec_t, dec_b, N_HEADS, HEAD_DIM), 0, 1)
        qb, kb, vb = bmaj(q), bmaj(k), bmaj(v)
        feat = lambda a: jnp.pad(jnp.transpose(a, (0, 2, 3, 1)), ((0, 0), (0, 0), (0, 0), (0, NEW_COLS - dec_t)))
        att = _attn_sample(qb.reshape(dec_b, dec_t, ATT_DIM), feat(kb), feat(vb), cache_kt, cache_vt, l,
                           s_bias, s_mult)
        att_tm = jnp.swapaxes(att, 0, 1).reshape(n_s, ATT_DIM)
        hs = _outproj([att_tm], [], yn, hs, row(g_out_att[l]), wa_bf, wc_bf, tm=n_s)
        p_s = jnp.swapaxes(p_sample[l], 0, 1).reshape(n_s, D_PLE)
        hs = _moe_ple(hs, p_s, *moe, tm=n_s, te=LANES)
        new["ks"].append(kb)
        new["vs"].append(vb)
        new["cs"].append(jnp.swapaxes(nconv.reshape(2, dec_b, CONV_DIM), 0, 1))

    y_prompt = hp.reshape(batch, seq, D_MODEL)
    y_sample = jnp.swapaxes(hs.reshape(dec_t, dec_b, D_MODEL), 0, 1)
    st = lambda key: jnp.stack(new[key])
    return (y_prompt, y_sample, st("kp"), st("vp"), st("cp"), st("ks"), st("vs"), st("cs"))
```

```python
import functools

import jax
import jax.numpy as jnp
import numpy as np
from jax import lax
from jax.experimental import pallas as pl
from jax.experimental.pallas import tpu as pltpu

F32 = jnp.float32
BF16 = jnp.bfloat16
HIGHEST = lax.Precision.HIGHEST

D_MODEL = 1024
HEAD_DIM = 64
N_HEADS = 8
ATT_DIM = N_HEADS * HEAD_DIM
CONV_DIM = D_MODEL - ATT_DIM
MIX_IN = 3 * ATT_DIM + 3 * CONV_DIM
PATTERNS = ((128, 1), (512, 4), (2048, 16))
N_KEYS = 128
Q_BLOCK = 128
N_BUCKETS = 32
MAX_DISTANCE = 2048
N_GROUPS = 4
EXPERTS_PER_GROUP = 8
N_EXPERTS = N_GROUPS * EXPERTS_PER_GROUP
D_EXPERT = 256
D_PLE = 256
EPS = 1e-6
NEG = -1e30

LANES = 128
SUBLANES = 8
SLABS = Q_BLOCK // SUBLANES
LSE_LANES_PER_HEAD = LANES // N_HEADS
VMEM_LIMIT = 48 * 1024 * 1024
NT = (((1,), (1,)), ((), ()))


def _cparams(n_axes):
    return pltpu.CompilerParams(dimension_semantics=("arbitrary",) * n_axes,
                                vmem_limit_bytes=VMEM_LIMIT)


def _full(shape):
    n = len(shape)
    return pl.BlockSpec(shape, lambda *_: (0,) * n)


def _rms(x, gain):
    ms = jnp.mean(x * x, axis=-1, keepdims=True)
    return x * lax.rsqrt(ms + EPS) * gain


def _exact_dot(x, e_ref):
    hi = x.astype(BF16)
    r1 = x - hi.astype(F32)
    mid = r1.astype(BF16)
    lo = (r1 - mid.astype(F32)).astype(BF16)
    e = e_ref[...]
    return (jnp.dot(hi, e, preferred_element_type=F32) + jnp.dot(mid, e, preferred_element_type=F32)
            + jnp.dot(lo, e, preferred_element_type=F32))


def _inproj_kernel(*refs, shift, tiles_per_seq, permute):
    (h_ref, gmix_ref, w_ref, qg_ref, kg_ref, hmean_ref, hexp_ref, cw_ref, gconv_ref, hist_ref) = refs[:10]
    if permute:
        perm_ref, q_ref, k_ref, v_ref, kn_ref, vn_ref, yn_ref, nconv_ref, carry_ref = refs[10:]
    else:
        q_ref, k_ref, v_ref, yn_ref, nconv_ref, carry_ref = refs[10:]
    i = pl.program_id(0)
    a = _rms(h_ref[...], gmix_ref[...])
    proj = jnp.dot(a.astype(BF16), w_ref[...], preferred_element_type=F32)
    tm = proj.shape[0]

    def head_norm(t, g):
        ms = _exact_dot(t * t, hmean_ref)
        return t * _exact_dot(lax.rsqrt(ms + EPS), hexp_ref) * g

    q = head_norm(proj[:, 0:ATT_DIM], qg_ref[...])
    k = head_norm(proj[:, ATT_DIM:2 * ATT_DIM], kg_ref[...])
    v = proj[:, 2 * ATT_DIM:3 * ATT_DIM]
    if permute:
        kn_ref[...] = k.T.reshape(N_HEADS, HEAD_DIM, tm)
        vn_ref[...] = v.T.reshape(N_HEADS, HEAD_DIM, tm)
        qkv = jnp.concatenate([q * (HEAD_DIM ** -0.5), k, v], axis=-1).astype(BF16)
        moved = jnp.dot(perm_ref[...], qkv, preferred_element_type=F32)
        q_ref[...] = moved[:, 0:ATT_DIM]
        k_ref[...] = moved[:, ATT_DIM:2 * ATT_DIM]
        v_ref[...] = moved[:, 2 * ATT_DIM:3 * ATT_DIM]
    else:
        q_ref[...] = q
        k_ref[...] = k
        v_ref[...] = v
    c0 = 3 * ATT_DIM
    hc = proj[:, c0:c0 + CONV_DIM]
    gb = proj[:, c0 + CONV_DIM:c0 + 2 * CONV_DIM]
    gc = proj[:, c0 + 2 * CONV_DIM:c0 + 3 * CONV_DIM]
    u = gc * hc

    if shift == 1:
        @pl.when(i % tiles_per_seq == 0)
        def _():
            carry_ref[...] = hist_ref[0]
        h0 = carry_ref[SUBLANES - 2:SUBLANES - 1, :]
        h1 = carry_ref[SUBLANES - 1:SUBLANES, :]
        row = lax.broadcasted_iota(jnp.int32, (tm, 1), 0)
        u1 = jnp.where(row == 0, h1, pltpu.roll(u, 1, 0))
        u2 = jnp.where(row == 0, h0, jnp.where(row == 1, h1, pltpu.roll(u, 2, 0)))
        carry_ref[...] = u[tm - SUBLANES:tm, :]
        nconv_ref[0] = u[tm - SUBLANES:tm, :]
    else:
        hist = hist_ref[0]
        u1 = jnp.concatenate([hist[shift:2 * shift], u[0:tm - shift]], axis=0)
        u2 = jnp.concatenate([hist, u[0:tm - 2 * shift]], axis=0)
        nconv_ref[0] = u[tm - 2 * shift:tm, :]
    conv = cw_ref[0:1, :] * u2 + cw_ref[1:2, :] * u1 + cw_ref[2:3, :] * u
    yn_ref[...] = _rms(gb * conv, gconv_ref[...])


def _inproj(h, hist, g_mix, w_in_bf, q_gain, k_gain, hmean, hexp, conv_w, g_out_conv, perm=None, *,
            tm, shift, tiles_per_seq, keep_tiles=0):
    n = h.shape[0]
    hist_rows = hist.shape[1]
    nseq = hist.shape[0]
    tok = lambda w: pl.BlockSpec((tm, w), lambda i: (i, 0))
    seq3 = lambda r: pl.BlockSpec((1, r, CONV_DIM), lambda i: (i // tiles_per_seq, 0, 0))
    nconv_rows = SUBLANES if shift == 1 else 2 * shift
    att = jax.ShapeDtypeStruct((n, ATT_DIM), F32)
    in_specs = [tok(D_MODEL), _full((1, D_MODEL)), _full((D_MODEL, MIX_IN)), _full((1, ATT_DIM)),
                _full((1, ATT_DIM)), _full((ATT_DIM, LANES)), _full((LANES, ATT_DIM)), _full((3, CONV_DIM)),
                _full((1, CONV_DIM)), seq3(hist_rows)]
    args = [h, g_mix, w_in_bf, q_gain, k_gain, hmean, hexp, conv_w, g_out_conv, hist]
    out_specs = [tok(ATT_DIM)] * 3
    out_shape = [att] * 3
    if perm is not None:
        in_specs.append(_full((tm, tm)))
        args.append(perm)
        first = tiles_per_seq - keep_tiles
        kept = pl.BlockSpec((None, N_HEADS, HEAD_DIM, tm),
                            lambda i: (i // tiles_per_seq, 0, 0, jnp.maximum(i % tiles_per_seq - first, 0)))
        out_specs += [kept, kept]
        out_shape += [jax.ShapeDtypeStruct((nseq, N_HEADS, HEAD_DIM, keep_tiles * tm), F32)] * 2
    out_specs += [tok(CONV_DIM), seq3(nconv_rows)]
    out_shape += [jax.ShapeDtypeStruct((n, CONV_DIM), F32),
                  jax.ShapeDtypeStruct((nseq, nconv_rows, CONV_DIM), F32)]
    return pl.pallas_call(
        functools.partial(_inproj_kernel, shift=shift, tiles_per_seq=tiles_per_seq, permute=perm is not None),
        grid=(n // tm,),
        in_specs=in_specs, out_specs=out_specs, out_shape=out_shape,
        scratch_shapes=[pltpu.VMEM((SUBLANES, CONV_DIM), F32)],
        compiler_params=_cparams(1),
        name="inproj",
    )(*args)


def _attn_block(q, kk, vv, bias):
    lane = lax.broadcasted_iota(jnp.int32, (Q_BLOCK, LANES), 1)
    upper = lane >= HEAD_DIM
    scores = []
    for h in range(N_HEADS):
        j, e = divmod(h, 2)
        qp = q[:, j * LANES:(j + 1) * LANES]
        qm = (jnp.where(upper, qp, 0.0) if e else jnp.where(upper, 0.0, qp)).astype(BF16)
        scores.append(lax.dot_general(qm, kk[:, j * LANES:(j + 1) * LANES], NT, preferred_element_type=F32))
    s = jnp.concatenate(scores, axis=0) + bias
    m = jnp.max(s, axis=-1, keepdims=True)
    p = jnp.exp(s - m)
    den = jnp.sum(p, axis=-1, keepdims=True)
    pb = p.astype(BF16)
    inv = 1.0 / den
    lse = m + jnp.log(den)
    lse_grp = lane // LSE_LANES_PER_HEAD
    lse_tile = jnp.zeros((Q_BLOCK, LANES), F32)
    outs = []
    for j in range(N_HEADS // 2):
        pair = None
        for e in range(2):
            h = 2 * j + e
            rows = slice(h * Q_BLOCK, (h + 1) * Q_BLOCK)
            o = jnp.dot(pb[rows], vv[:, j * LANES:(j + 1) * LANES], preferred_element_type=F32) * inv[rows]
            pair = o if e == 0 else jnp.where(upper, o, pair)
            lse_tile = jnp.where(lse_grp == h, lse[rows], lse_tile)
        outs.append(pair)
    return jnp.concatenate(outs, axis=-1), lse_tile


def _attn_kernel(q_ref, kp_ref, kc_ref, vp_ref, vc_ref, bias_ref, o_ref, lse_ref, kbuf, vbuf, *, sub):
    n = pl.program_id(2)
    rows = sub * Q_BLOCK
    kbuf[0:Q_BLOCK, :] = kp_ref[...].reshape(Q_BLOCK, ATT_DIM).astype(BF16)
    kbuf[Q_BLOCK:, :] = kc_ref[...].reshape(rows, ATT_DIM).astype(BF16)
    vbuf[0:Q_BLOCK, :] = vp_ref[...].reshape(Q_BLOCK, ATT_DIM).astype(BF16)
    vbuf[Q_BLOCK:, :] = vc_ref[...].reshape(rows, ATT_DIM).astype(BF16)

    for j in range(sub):
        q = q_ref[j].reshape(Q_BLOCK, ATT_DIM)
        r0 = j * Q_BLOCK
        first = jnp.logical_and(n == 0, j == 0).astype(jnp.int32) if j == 0 else 0
        o, lse = _attn_block(q, kbuf[r0:r0 + 2 * Q_BLOCK, :], vbuf[r0:r0 + 2 * Q_BLOCK, :], bias_ref[first])
        o_ref[j] = o.reshape(SLABS, SUBLANES, ATT_DIM)
        lse_ref[j] = lse.reshape(SLABS, SUBLANES, LANES)


def _attn_pattern(q, k, v, bias, *, batch, seq, dil, sub):
    nblk = seq // (Q_BLOCK * dil)
    view = lambda t: t.reshape(batch, nblk, SLABS, dil, SUBLANES, t.shape[-1])
    cur = lambda c: pl.BlockSpec((None, sub, SLABS, None, SUBLANES, c), lambda b, r, n: (b, n, 0, r, 0, 0))
    prev = pl.BlockSpec((None, None, SLABS, None, SUBLANES, ATT_DIM),
                        lambda b, r, n: (b, jnp.maximum(n * sub - 1, 0), 0, r, 0, 0))
    o, lse = pl.pallas_call(
        functools.partial(_attn_kernel, sub=sub),
        grid=(batch, dil, nblk // sub),
        in_specs=[cur(ATT_DIM), prev, cur(ATT_DIM), prev, cur(ATT_DIM),
                  _full((2, N_HEADS * Q_BLOCK, 2 * Q_BLOCK))],
        out_specs=[cur(ATT_DIM), cur(LANES)],
        out_shape=[jax.ShapeDtypeStruct((batch, nblk, SLABS, dil, SUBLANES, ATT_DIM), F32),
                   jax.ShapeDtypeStruct((batch, nblk, SLABS, dil, SUBLANES, LANES), F32)],
        scratch_shapes=[pltpu.VMEM(((sub + 1) * Q_BLOCK, ATT_DIM), BF16),
                        pltpu.VMEM(((sub + 1) * Q_BLOCK, ATT_DIM), BF16)],
        compiler_params=_cparams(3),
        name=f"attn_d{dil}",
    )(view(q), view(k), view(k), view(v), view(v), bias)
    return o.reshape(batch * seq, ATT_DIM), lse.reshape(batch * seq, LANES)


SAMPLE_T = 4
NEW_COLS = LANES


def _attn_sample_kernel(q_ref, kt_ref, ktn_ref, vt_ref, vtn_ref, bias_ref, mult_ref, o_ref):
    rows = SAMPLE_T * N_HEADS
    q4 = q_ref[...] * (HEAD_DIM ** -0.5)
    qt = jnp.concatenate([jnp.broadcast_to(q4[t:t + 1, :], (N_HEADS, ATT_DIM)) for t in range(SAMPLE_T)], axis=0)
    lane_head = lax.broadcasted_iota(jnp.int32, (rows, ATT_DIM), 1) // HEAD_DIM
    row_head = lax.broadcasted_iota(jnp.int32, (rows, ATT_DIM), 0) % N_HEADS
    own = lane_head == row_head
    qbd = jnp.where(own, qt, 0.0).astype(BF16)
    flat = lambda ref: ref[...].reshape(ATT_DIM, ref.shape[-1]).astype(BF16)
    s = jnp.concatenate([jnp.dot(qbd, flat(kt_ref), preferred_element_type=F32),
                         jnp.dot(qbd, flat(ktn_ref), preferred_element_type=F32)], axis=-1) + bias_ref[...]
    m = jnp.max(s, axis=-1, keepdims=True)
    p = jnp.exp(s - m) * mult_ref[...]
    den = jnp.sum(p, axis=-1, keepdims=True)
    pb = p.astype(BF16)
    w_buf = kt_ref.shape[-1]
    acc = (lax.dot_general(pb[:, :w_buf], flat(vt_ref), NT, preferred_element_type=F32)
           + lax.dot_general(pb[:, w_buf:], flat(vtn_ref), NT, preferred_element_type=F32))
    acc = jnp.where(own, acc / den, 0.0)
    for t in range(SAMPLE_T):
        o_ref[t:t + 1, :] = jnp.sum(acc[t * N_HEADS:(t + 1) * N_HEADS, :], axis=0, keepdims=True)


def _attn_sample(q, kt_new, vt_new, cache_kt, cache_vt, layer, bias, mult):
    nb = q.shape[0]
    w_buf = cache_kt.shape[-1]
    tok = pl.BlockSpec((None, SAMPLE_T, ATT_DIM), lambda b: (b, 0, 0))
    new = pl.BlockSpec((None, N_HEADS, HEAD_DIM, NEW_COLS), lambda b: (b, 0, 0, 0))
    old = pl.BlockSpec((None, None, N_HEADS, HEAD_DIM, w_buf), lambda b: (layer, b, 0, 0, 0))
    tbl = _full((SAMPLE_T * N_HEADS, w_buf + NEW_COLS))
    return pl.pallas_call(
        _attn_sample_kernel,
        grid=(nb,),
        in_specs=[tok, old, new, old, new, tbl, tbl],
        out_specs=tok,
        out_shape=jax.ShapeDtypeStruct((nb, SAMPLE_T, ATT_DIM), F32),
        compiler_params=_cparams(1),
        name="attn_sample",
    )(q, cache_kt, kt_new, cache_vt, vt_new, bias, mult)


def _split_dot(x, e_ref):
    hi = x.astype(BF16)
    lo = (x - hi.astype(F32)).astype(BF16)
    return (jnp.dot(hi, e_ref[...], preferred_element_type=F32)
            + jnp.dot(lo, e_ref[...], preferred_element_type=F32))


def _outproj_kernel(*refs, n_pat):
    mix = n_pat > 1
    n_lse = n_pat if mix else 0
    o_refs = refs[0:n_pat]
    l_refs = refs[n_pat:n_pat + n_lse]
    rest = refs[n_pat + n_lse:]
    if mix:
        yn_ref, h_ref, gatt_ref, exp_ref, unperm_ref, wa_ref, wc_ref, out_ref = rest
        lses = [r[...] for r in l_refs]
        top = functools.reduce(jnp.maximum, lses)
        ws = [jnp.exp(l - top) for l in lses]
        tot = functools.reduce(lambda a, b: a + b, ws)
        att = None
        for w, o_ref in zip(ws, o_refs):
            term = _split_dot(w / tot, exp_ref) * o_ref[...]
            att = term if att is None else att + term
        att_bf = jnp.dot(unperm_ref[...], _rms(att, gatt_ref[...]).astype(BF16),
                         preferred_element_type=F32).astype(BF16)
    else:
        yn_ref, h_ref, gatt_ref, wa_ref, wc_ref, out_ref = rest
        att_bf = _rms(o_refs[0][...], gatt_ref[...]).astype(BF16)
    y = (jnp.dot(att_bf, wa_ref[...], preferred_element_type=F32)
         + jnp.dot(yn_ref[...].astype(BF16), wc_ref[...], preferred_element_type=F32))
    out_ref[...] = h_ref[...] + y


def _outproj(os, lses, yn, h, g_att, wa_bf, wc_bf, expand=None, unperm=None, *, tm):
    n = h.shape[0]
    n_pat = len(os)
    tok = lambda w: pl.BlockSpec((tm, w), lambda i: (i, 0))
    in_specs = [tok(ATT_DIM)] * n_pat + [tok(LANES)] * len(lses) + [tok(CONV_DIM), tok(D_MODEL), _full((1, ATT_DIM))]
    args = [*os, *lses, yn, h, g_att]
    if n_pat > 1:
        in_specs += [_full((LANES, ATT_DIM)), _full((tm, tm))]
        args += [expand, unperm]
    in_specs += [_full((ATT_DIM, D_MODEL)), _full((CONV_DIM, D_MODEL))]
    args += [wa_bf, wc_bf]
    return pl.pallas_call(
        functools.partial(_outproj_kernel, n_pat=n_pat),
        grid=(n // tm,),
        in_specs=in_specs,
        out_specs=tok(D_MODEL),
        out_shape=jax.ShapeDtypeStruct((n, D_MODEL), F32),
        compiler_params=_cparams(1),
        name="outproj",
    )(*args)


ROUTE_I1, ROUTE_I2, ROUTE_R1, ROUTE_R2, ROUTE_W1, ROUTE_W2 = range(6)
GROUP_LANE0 = N_EXPERTS


def _route_kernel(h_ref, g_ref, wr_ref, route_ref, cnt_ref, carry_ref):
    i = pl.program_id(0)

    @pl.when(i == 0)
    def _():
        carry_ref[...] = jnp.zeros_like(carry_ref)

    m = _rms(h_ref[...], g_ref[...])
    logits = jnp.dot(m.astype(BF16), wr_ref[...], preferred_element_type=F32)
    tm = logits.shape[0]
    lane_i = lax.broadcasted_iota(jnp.int32, (tm, LANES), 1)
    lane = lane_i.astype(F32)
    big = jnp.float32(4 * LANES)

    is_g = jnp.logical_and(lane_i >= GROUP_LANE0, lane_i < GROUP_LANE0 + N_GROUPS)
    gl = jnp.where(is_g, logits, NEG)
    gmax = jnp.max(gl, axis=-1, keepdims=True)
    g_w = 1.0 / jnp.sum(jnp.where(is_g, jnp.exp(gl - gmax), 0.0), axis=-1, keepdims=True)
    g_sel = jnp.min(jnp.where(gl == gmax, lane - GROUP_LANE0, big), axis=-1, keepdims=True)

    grp_of_lane = (lane_i // EXPERTS_PER_GROUP).astype(F32)
    in_grp = jnp.logical_and(lane_i < N_EXPERTS, grp_of_lane == g_sel)
    el = jnp.where(in_grp, logits, NEG)
    t1 = jnp.max(el, axis=-1, keepdims=True)
    i1 = jnp.min(jnp.where(el == t1, lane, big), axis=-1, keepdims=True)
    el2 = jnp.where(lane == i1, NEG, el)
    t2 = jnp.max(el2, axis=-1, keepdims=True)
    i2 = jnp.min(jnp.where(el2 == t2, lane, big), axis=-1, keepdims=True)
    e2 = jnp.exp(t2 - t1)
    w1 = g_w / (1.0 + e2)
    w2 = g_w * e2 / (1.0 + e2)

    hit1 = lane == i1
    hit2 = lane == i2
    c = jnp.where(jnp.logical_or(hit1, hit2), 1.0, 0.0)
    rr = lax.broadcasted_iota(jnp.int32, (tm, tm), 0)
    cc = lax.broadcasted_iota(jnp.int32, (tm, tm), 1)
    lower = jnp.where(rr > cc, 1.0, 0.0).astype(BF16)
    before = jnp.dot(lower, c.astype(BF16), preferred_element_type=F32) + carry_ref[0:1, :]
    r1 = jnp.sum(jnp.where(hit1, before, 0.0), axis=-1, keepdims=True)
    r2 = jnp.sum(jnp.where(hit2, before, 0.0), axis=-1, keepdims=True)
    total = carry_ref[0:1, :] + jnp.sum(c, axis=0, keepdims=True)
    carry_ref[...] = jnp.broadcast_to(total, carry_ref.shape)
    cnt_ref[...] = jnp.broadcast_to(total, cnt_ref.shape)

    rec = jnp.zeros((tm, LANES), F32)
    for idx, val in ((ROUTE_I1, i1), (ROUTE_I2, i2), (ROUTE_R1, r1), (ROUTE_R2, r2), (ROUTE_W1, w1), (ROUTE_W2, w2)):
        rec = jnp.where(lane_i == idx, val, rec)
    route_ref[...] = rec


def _route(h, g_ffn, w_router, *, tm):
    n = h.shape[0]
    tok = lambda w: pl.BlockSpec((tm, w), lambda i: (i, 0))
    return pl.pallas_call(
        _route_kernel,
        grid=(n // tm,),
        in_specs=[tok(D_MODEL), _full((1, D_MODEL)), _full((D_MODEL, LANES))],
        out_specs=[tok(LANES), _full((SUBLANES, LANES))],
        out_shape=[jax.ShapeDtypeStruct((n, LANES), F32), jax.ShapeDtypeStruct((SUBLANES, LANES), F32)],
        scratch_shapes=[pltpu.VMEM((SUBLANES, LANES), F32)],
        compiler_params=_cparams(1),
        name="route",
    )(h, g_ffn, w_router)


def _dispatch_kernel(pos_ref, h_ref, g_ref, xs_ref, mbuf, sem):
    tm = mbuf.shape[0]
    mbuf[...] = _rms(h_ref[...], g_ref[...])

    def row_copy(t, p):
        return pltpu.make_async_copy(mbuf.at[pl.ds(t, 1), :], xs_ref.at[pl.ds(p, 1), :], sem)

    def issue(t, carry):
        row_copy(t, pos_ref[0, 0, 2 * t]).start()
        row_copy(t, pos_ref[0, 0, 2 * t + 1]).start()
        return carry

    lax.fori_loop(0, tm, issue, 0)

    def drain(t, carry):
        row_copy(0, 0).wait()
        row_copy(0, 0).wait()
        return carry

    lax.fori_loop(0, tm, drain, 0)


def _dispatch(h, g_ffn, pos, n_rows, *, tm):
    n = h.shape[0]
    return pl.pallas_call(
        _dispatch_kernel,
        grid=(n // tm,),
        in_specs=[pl.BlockSpec((1, 1, 2 * tm), lambda i: (i, 0, 0), memory_space=pltpu.SMEM),
                  pl.BlockSpec((tm, D_MODEL), lambda i: (i, 0)), _full((1, D_MODEL))],
        out_specs=pl.BlockSpec(memory_space=pl.ANY),
        out_shape=jax.ShapeDtypeStruct((n_rows, D_MODEL), F32),
        scratch_shapes=[pltpu.VMEM((tm, D_MODEL), F32), pltpu.SemaphoreType.DMA(())],
        compiler_params=pltpu.CompilerParams(dimension_semantics=("arbitrary",), vmem_limit_bytes=VMEM_LIMIT,
                                             has_side_effects=True),
        name="dispatch",
    )(pos, h, g_ffn)


def _expert_kernel(te_ref, tv_ref, x_ref, wg_ref, wu_ref, wd_ref, y_ref, wg_bf, wu_bf, wd_bf):
    i = pl.program_id(0)
    valid = tv_ref[i]
    changed = jnp.logical_or(i == 0, te_ref[i] != te_ref[jnp.maximum(i - 1, 0)])

    @pl.when(jnp.logical_and(changed, valid > 0))
    def _():
        wg_bf[...] = wg_ref[0].astype(BF16)
        wu_bf[...] = wu_ref[0].astype(BF16)
        wd_bf[...] = wd_ref[0].astype(BF16)

    @pl.when(valid > 0)
    def _():
        te = x_ref.shape[0]
        row = lax.broadcasted_iota(jnp.int32, (te, 1), 0)
        x = jnp.where(row < valid, x_ref[...], 0.0).astype(BF16)
        hg = jnp.dot(x, wg_bf[...], preferred_element_type=F32)
        hu = jnp.dot(x, wu_bf[...], preferred_element_type=F32)
        hid = (hg * jax.nn.sigmoid(hg)) * hu
        y_ref[...] = jnp.dot(hid.astype(BF16), wd_bf[...], preferred_element_type=F32)

    @pl.when(valid <= 0)
    def _():
        y_ref[...] = jnp.zeros_like(y_ref)


def _experts(xs, tile_expert, tile_valid, w_gate, w_up, w_down, *, te):
    n_rows = xs.shape[0]
    wspec = lambda a, b: pl.BlockSpec((1, a, b), lambda i, e, v: (e[i], 0, 0))
    grid_spec = pltpu.PrefetchScalarGridSpec(
        num_scalar_prefetch=2,
        grid=(n_rows // te,),
        in_specs=[pl.BlockSpec((te, D_MODEL), lambda i, e, v: (i, 0)),
                  wspec(D_MODEL, D_EXPERT), wspec(D_MODEL, D_EXPERT), wspec(D_EXPERT, D_MODEL)],
        out_specs=pl.BlockSpec((te, D_MODEL), lambda i, e, v: (i, 0)),
        scratch_shapes=[pltpu.VMEM((D_MODEL, D_EXPERT), BF16), pltpu.VMEM((D_MODEL, D_EXPERT), BF16),
                        pltpu.VMEM((D_EXPERT, D_MODEL), BF16)])
    return pl.pallas_call(
        _expert_kernel,
        grid_spec=grid_spec,
        out_shape=jax.ShapeDtypeStruct((n_rows, D_MODEL), F32),
        compiler_params=_cparams(1),
        name="experts",
    )(tile_expert, tile_valid, xs, w_gate, w_up, w_down)


def _combine_kernel(pos_ref, h_ref, route_ref, ys_ref, p_ref, gple_ref, wgate_ref, wproj_ref, out_ref,
                    y0, y1, sem):
    tm = y0.shape[0]

    def row_copy(p, dst, t):
        return pltpu.make_async_copy(ys_ref.at[pl.ds(p, 1), :], dst.at[pl.ds(t, 1), :], sem)

    def issue(t, carry):
        row_copy(pos_ref[0, 0, 2 * t], y0, t).start()
        row_copy(pos_ref[0, 0, 2 * t + 1], y1, t).start()
        return carry

    lax.fori_loop(0, tm, issue, 0)

    def drain(t, carry):
        row_copy(0, y0, 0).wait()
        row_copy(0, y1, 0).wait()
        return carry

    lax.fori_loop(0, tm, drain, 0)

    rec = route_ref[...]
    w1 = rec[:, ROUTE_W1:ROUTE_W1 + 1]
    w2 = rec[:, ROUTE_W2:ROUTE_W2 + 1]
    h2 = h_ref[...] + w1 * y0[...] + w2 * y1[...]
    gate = jax.nn.sigmoid(jnp.dot(_rms(h2, gple_ref[...]).astype(BF16), wgate_ref[...], preferred_element_type=F32))
    ple = jnp.dot(p_ref[...].astype(BF16), wproj_ref[...], preferred_element_type=F32)
    out_ref[...] = h2 + ple * gate


def _combine(h, route, pos, ys, p, g_ple, wgate_bf, wproj_bf, *, tm):
    n = h.shape[0]
    tok = lambda w: pl.BlockSpec((tm, w), lambda i: (i, 0))
    return pl.pallas_call(
        _combine_kernel,
        grid=(n // tm,),
        in_specs=[pl.BlockSpec((1, 1, 2 * tm), lambda i: (i, 0, 0), memory_space=pltpu.SMEM),
                  tok(D_MODEL), tok(LANES), pl.BlockSpec(memory_space=pl.ANY), tok(D_PLE),
                  _full((1, D_MODEL)), _full((D_MODEL, D_MODEL)), _full((D_PLE, D_MODEL))],
        out_specs=tok(D_MODEL),
        out_shape=jax.ShapeDtypeStruct((n, D_MODEL), F32),
        scratch_shapes=[pltpu.VMEM((tm, D_MODEL), F32), pltpu.VMEM((tm, D_MODEL), F32),
                        pltpu.SemaphoreType.DMA(())],
        compiler_params=_cparams(1),
        name="combine",
    )(pos, h, route, ys, p, g_ple, wgate_bf, wproj_bf)


def _lookup(table, idx, size):
    hit = idx[..., None] == jnp.arange(size, dtype=jnp.int32)
    return jnp.sum(jnp.where(hit, table, 0), axis=-1)


def _moe_ple(h, p, g_ffn, w_router, w_gate, w_up, w_down, g_ple, wgate_bf, wproj_bf, *, tm, te):
    n = h.shape[0]
    route, counts = _route(h, g_ffn, w_router, tm=tm)
    counts = counts[0, :N_EXPERTS].astype(jnp.int32)
    padded = ((counts + te - 1) // te) * te
    ends = jnp.cumsum(padded)
    offs = ends - padded
    ids = route[:, ROUTE_I1:ROUTE_I2 + 1].astype(jnp.int32)
    ranks = route[:, ROUTE_R1:ROUTE_R2 + 1].astype(jnp.int32)
    pos = (_lookup(offs, ids, N_EXPERTS) + ranks).reshape(n // tm, 1, 2 * tm)
    n_tiles = (2 * n) // te + N_EXPERTS
    starts = jnp.arange(n_tiles, dtype=jnp.int32) * te
    tile_expert = jnp.minimum(jnp.sum((starts[:, None] >= ends[None, :]).astype(jnp.int32), axis=-1), N_EXPERTS - 1)
    tile_valid = jnp.clip(_lookup(counts, tile_expert, N_EXPERTS)
                          - (starts - _lookup(offs, tile_expert, N_EXPERTS)), 0, te).astype(jnp.int32)
    xs = _dispatch(h, g_ffn, pos, n_tiles * te, tm=tm)
    ys = _experts(xs, tile_expert, tile_valid, w_gate, w_up, w_down, te=te)
    return _combine(h, route, pos, ys, p, g_ple, wgate_bf, wproj_bf, tm=tm)


REC_E1, REC_E2, REC_S1, REC_S2, REC_W1, REC_W2 = range(6)
ROW_W = D_MODEL + LANES
LIST_LANES = LANES
LIST_COUNT = LIST_LANES - 1


def _slots(tm):
    need = 2 * tm + (SUBLANES - 1) * N_EXPERTS + SUBLANES
    return -(-need // LANES) * LANES


def _pieces(x):
    hi = x.astype(BF16)
    r1 = x - hi.astype(F32)
    mid = r1.astype(BF16)
    return hi, mid, (r1 - mid.astype(F32)).astype(BF16)


def _sort_kernel(h_ref, g_ref, wr_ref, before_ref, sel_ref, xs_ref, rec_ref, cnt_ref):
    m = _rms(h_ref[...], g_ref[...])
    logits = jnp.dot(m.astype(BF16), wr_ref[...], preferred_element_type=F32)
    tm = logits.shape[0]
    slots = xs_ref.shape[0]
    lane_i = lax.broadcasted_iota(jnp.int32, (tm, LANES), 1)
    lane = lane_i.astype(F32)
    big = jnp.float32(4 * LANES)

    is_g = jnp.logical_and(lane_i >= GROUP_LANE0, lane_i < GROUP_LANE0 + N_GROUPS)
    gl = jnp.where(is_g, logits, NEG)
    gmax = jnp.max(gl, axis=-1, keepdims=True)
    g_w = 1.0 / jnp.sum(jnp.where(is_g, jnp.exp(gl - gmax), 0.0), axis=-1, keepdims=True)
    g_sel = jnp.min(jnp.where(gl == gmax, lane - GROUP_LANE0, big), axis=-1, keepdims=True)

    grp_of_lane = (lane_i // EXPERTS_PER_GROUP).astype(F32)
    in_grp = jnp.logical_and(lane_i < N_EXPERTS, grp_of_lane == g_sel)
    el = jnp.where(in_grp, logits, NEG)
    t1 = jnp.max(el, axis=-1, keepdims=True)
    e1 = jnp.min(jnp.where(el == t1, lane, big), axis=-1, keepdims=True)
    el2 = jnp.where(lane == e1, NEG, el)
    t2 = jnp.max(el2, axis=-1, keepdims=True)
    e2 = jnp.min(jnp.where(el2 == t2, lane, big), axis=-1, keepdims=True)
    ex = jnp.exp(t2 - t1)
    w1 = g_w / (1.0 + ex)
    w2 = g_w * ex / (1.0 + ex)

    hit1 = lane == e1
    hit2 = lane == e2
    c = jnp.where(jnp.logical_or(hit1, hit2), 1.0, 0.0)
    rr = lax.broadcasted_iota(jnp.int32, (tm, tm), 0)
    cc = lax.broadcasted_iota(jnp.int32, (tm, tm), 1)
    lower = jnp.where(rr > cc, 1.0, 0.0).astype(BF16)
    rank = jnp.dot(lower, c.astype(BF16), preferred_element_type=F32)
    cnt = jnp.sum(c, axis=0, keepdims=True)
    chunks = jnp.floor((cnt + (SUBLANES - 1)) * (1.0 / SUBLANES))
    start = SUBLANES * jnp.dot(jnp.broadcast_to(chunks, (SUBLANES, LANES)).astype(BF16), before_ref[...],
                               preferred_element_type=F32)[0:1, :]
    slot_of = rank + start
    s1 = jnp.sum(jnp.where(hit1, slot_of, 0.0), axis=-1, keepdims=True)
    s2 = jnp.sum(jnp.where(hit2, slot_of, 0.0), axis=-1, keepdims=True)

    rec = jnp.zeros((tm, LANES), F32)
    for idx, val in ((REC_E1, e1), (REC_E2, e2), (REC_S1, s1), (REC_S2, s2), (REC_W1, w1), (REC_W2, w2)):
        rec = jnp.where(lane_i == idx, val, rec)
    rec_ref[...] = rec
    cnt_ref[0] = jnp.broadcast_to(cnt, (SUBLANES, LANES))

    rec_parts = _pieces(rec)
    srow = sum(lax.dot_general(sel_ref[...], part, NT, preferred_element_type=F32) for part in rec_parts)
    slot_id = lax.broadcasted_iota(jnp.int32, (slots, tm), 0).astype(F32)
    place = jnp.where(jnp.logical_or(slot_id == srow[0:1, :], slot_id == srow[1:2, :]), 1.0, 0.0).astype(BF16)
    payload = jnp.concatenate([m.astype(BF16), *rec_parts], axis=-1)
    moved = jnp.dot(place, payload, preferred_element_type=F32)
    info = (moved[:, D_MODEL:D_MODEL + LANES] + moved[:, D_MODEL + LANES:D_MODEL + 2 * LANES]
            + moved[:, D_MODEL + 2 * LANES:])
    xs_ref[...] = jnp.concatenate([moved[:, :D_MODEL], info], axis=-1)


def _sort(h, g_ffn, w_router, before, sel, *, tm):
    n = h.shape[0]
    slots = _slots(tm)
    tok = lambda w: pl.BlockSpec((tm, w), lambda i: (i, 0))
    return pl.pallas_call(
        _sort_kernel,
        grid=(n // tm,),
        in_specs=[tok(D_MODEL), _full((1, D_MODEL)), _full((D_MODEL, LANES)), _full((LANES, LANES)),
                  _full((SUBLANES, LANES))],
        out_specs=[pl.BlockSpec((slots, ROW_W), lambda i: (i, 0)), tok(LANES),
                   pl.BlockSpec((1, SUBLANES, LANES), lambda i: (i, 0, 0))],
        out_shape=[jax.ShapeDtypeStruct((n // tm * slots, ROW_W), F32), jax.ShapeDtypeStruct((n, LANES), F32),
                   jax.ShapeDtypeStruct((n // tm, SUBLANES, LANES), F32)],
        compiler_params=_cparams(1),
        name="moe_sort",
    )(h, g_ffn, w_router, before, sel)


def _chunk_copy(src_hbm, row, dst, c, sem):
    if not isinstance(row, int):
        row = pl.multiple_of(row, SUBLANES)
    return pltpu.make_async_copy(src_hbm.at[pl.ds(row, SUBLANES), :],
                                 dst.at[pl.ds(c * SUBLANES, SUBLANES), :], sem)


def _expert_kernel2(te_ref, tv_ref, src_ref, nxt_ref, xs_ref, wg_ref, wu_ref, wd_ref, y_ref,
                    xbuf, sem, wg_bf, wu_bf, wd_bf):
    i = pl.program_id(0)
    n = pl.num_programs(0)
    te = xbuf.shape[1]
    slot = i % 2

    def fetch(list_ref, to_slot):
        for c in range(te // SUBLANES):
            _chunk_copy(xs_ref, list_ref[0, 0, c], xbuf.at[to_slot], c, sem.at[to_slot]).start()

    @pl.when(jnp.logical_and(i == 0, tv_ref[0] > 0))
    def _():
        fetch(src_ref, 0)

    nxt = jnp.minimum(i + 1, n - 1)

    @pl.when(jnp.logical_and(i + 1 < n, tv_ref[nxt] > 0))
    def _():
        fetch(nxt_ref, 1 - slot)

    valid = tv_ref[i]
    changed = jnp.logical_or(i == 0, te_ref[i] != te_ref[jnp.maximum(i - 1, 0)])

    @pl.when(jnp.logical_and(changed, valid > 0))
    def _():
        wg_bf[...] = wg_ref[0].astype(BF16)
        wu_bf[...] = wu_ref[0].astype(BF16)
        wd_bf[...] = wd_ref[0].astype(BF16)

    @pl.when(valid > 0)
    def _():
        pltpu.make_async_copy(xbuf.at[slot], xbuf.at[slot], sem.at[slot]).wait()
        rows = xbuf[slot]
        x = rows[:, :D_MODEL].astype(BF16)
        info = rows[:, D_MODEL:]
        mine = info[:, REC_E1:REC_E1 + 1] == te_ref[i].astype(F32)
        gate = jnp.where(mine, info[:, REC_W1:REC_W1 + 1], info[:, REC_W2:REC_W2 + 1])
        hg = jnp.dot(x, wg_bf[...], preferred_element_type=F32)
        hu = jnp.dot(x, wu_bf[...], preferred_element_type=F32)
        hid = (hg * jax.nn.sigmoid(hg)) * hu * gate
        y_ref[...] = jnp.dot(hid.astype(BF16), wd_bf[...], preferred_element_type=F32)

    @pl.when(valid <= 0)
    def _():
        y_ref[...] = jnp.zeros_like(y_ref)


def _experts2(xs, tile_expert, tile_valid, src, w_gate, w_up, w_down, *, te):
    n_tiles = tile_expert.shape[0]
    wspec = lambda a, b: pl.BlockSpec((1, a, b), lambda i, e, v: (e[i], 0, 0))
    lst = lambda shift: pl.BlockSpec((1, 1, LIST_LANES),
                                     lambda i, e, v: (jnp.minimum(i + shift, n_tiles - 1), 0, 0),
                                     memory_space=pltpu.SMEM)
    grid_spec = pltpu.PrefetchScalarGridSpec(
        num_scalar_prefetch=2,
        grid=(n_tiles,),
        in_specs=[lst(0), lst(1), pl.BlockSpec(memory_space=pl.ANY),
                  wspec(D_MODEL, D_EXPERT), wspec(D_MODEL, D_EXPERT), wspec(D_EXPERT, D_MODEL)],
        out_specs=pl.BlockSpec((te, D_MODEL), lambda i, e, v: (i, 0)),
        scratch_shapes=[pltpu.VMEM((2, te, ROW_W), F32), pltpu.SemaphoreType.DMA((2,)),
                        pltpu.VMEM((D_MODEL, D_EXPERT), BF16), pltpu.VMEM((D_MODEL, D_EXPERT), BF16),
                        pltpu.VMEM((D_EXPERT, D_MODEL), BF16)])
    return pl.pallas_call(
        _expert_kernel2,
        grid_spec=grid_spec,
        out_shape=jax.ShapeDtypeStruct((n_tiles * te, D_MODEL), F32),
        compiler_params=_cparams(1),
        name="experts",
    )(tile_expert, tile_valid, src, src, xs, w_gate, w_up, w_down)


def _combine_kernel2(dst_ref, nxt_ref, h_ref, rec_ref, ys_ref, p_ref, gple_ref, wgate_ref, wproj_ref, out_ref,
                     ybuf, sem):
    i = pl.program_id(0)
    n = pl.num_programs(0)
    slot = i % 2
    tm = h_ref.shape[0]
    slots = ybuf.shape[1]

    def fetch(list_ref, to_slot):
        def body(c, carry):
            _chunk_copy(ys_ref, list_ref[0, 0, c], ybuf.at[to_slot], c, sem.at[to_slot]).start()
            return carry
        lax.fori_loop(0, list_ref[0, 0, LIST_COUNT], body, 0)

    @pl.when(i == 0)
    def _():
        ybuf[...] = jnp.zeros_like(ybuf)
        fetch(dst_ref, 0)

    @pl.when(i + 1 < n)
    def _():
        fetch(nxt_ref, 1 - slot)

    def drain(c, carry):
        _chunk_copy(ys_ref, 0, ybuf.at[slot], 0, sem.at[slot]).wait()
        return carry

    lax.fori_loop(0, dst_ref[0, 0, LIST_COUNT], drain, 0)

    rec = rec_ref[...]
    slot_id = lax.broadcasted_iota(jnp.int32, (tm, slots), 1).astype(F32)
    back = jnp.where(jnp.logical_or(slot_id == rec[:, REC_S1:REC_S1 + 1], slot_id == rec[:, REC_S2:REC_S2 + 1]),
                     1.0, 0.0).astype(BF16)
    ys = ybuf[slot]
    y_hi = ys.astype(BF16)
    y_lo = (ys - y_hi.astype(F32)).astype(BF16)
    h2 = h_ref[...] + (jnp.dot(back, y_hi, preferred_element_type=F32)
                       + jnp.dot(back, y_lo, preferred_element_type=F32))
    gate = jax.nn.sigmoid(jnp.dot(_rms(h2, gple_ref[...]).astype(BF16), wgate_ref[...], preferred_element_type=F32))
    ple = jnp.dot(p_ref[...].astype(BF16), wproj_ref[...], preferred_element_type=F32)
    out_ref[...] = h2 + ple * gate


def _combine2(h, rec, dst, ys, p, g_ple, wgate_bf, wproj_bf, *, tm):
    n = h.shape[0]
    n_tok = n // tm
    tok = lambda w: pl.BlockSpec((tm, w), lambda i: (i, 0))
    lst = lambda shift: pl.BlockSpec((1, 1, LIST_LANES), lambda i: (jnp.minimum(i + shift, n_tok - 1), 0, 0),
                                     memory_space=pltpu.SMEM)
    return pl.pallas_call(
        _combine_kernel2,
        grid=(n_tok,),
        in_specs=[lst(0), lst(1), tok(D_MODEL), tok(LANES), pl.BlockSpec(memory_space=pl.ANY), tok(D_PLE),
                  _full((1, D_MODEL)), _full((D_MODEL, D_MODEL)), _full((D_PLE, D_MODEL))],
        out_specs=tok(D_MODEL),
        out_shape=jax.ShapeDtypeStruct((n, D_MODEL), F32),
        scratch_shapes=[pltpu.VMEM((2, _slots(tm), D_MODEL), F32), pltpu.SemaphoreType.DMA((2,))],
        compiler_params=_cparams(1),
        name="combine",
    )(dst, dst, h, rec, ys, p, g_ple, wgate_bf, wproj_bf)


def _excl_cumsum(x, axis):
    return jnp.cumsum(x, axis=axis) - x


def _chunk_plan(cnt, *, tm, te):
    n_tok = cnt.shape[0]
    slots = _slots(tm)
    per_tile = te // SUBLANES
    chunks = (cnt + SUBLANES - 1) // SUBLANES
    run0 = _excl_cumsum(chunks, 1)
    n_run = jnp.sum(chunks, axis=1)
    seg0 = _excl_cumsum(chunks, 0)
    total = jnp.sum(chunks, axis=0)
    region = ((total + per_tile - 1) // per_tile) * per_tile
    reg_end = jnp.cumsum(region)
    reg0 = reg_end - region

    n_tiles = -(-(2 * n_tok * tm + (SUBLANES - 1) * N_EXPERTS * n_tok) // te) + N_EXPERTS
    t0 = jnp.arange(n_tiles, dtype=jnp.int32) * per_tile
    tile_expert = jnp.minimum(jnp.sum((t0[:, None] >= reg_end[None, :]).astype(jnp.int32), axis=1), N_EXPERTS - 1)
    pick = tile_expert[:, None] == jnp.arange(N_EXPERTS, dtype=jnp.int32)[None, :]
    of_tile = lambda v: jnp.sum(jnp.where(pick, v[None, :], 0), axis=1)
    tile_valid = jnp.clip(of_tile(total) - (t0 - of_tile(reg0)), 0, per_tile).astype(jnp.int32)

    q = (t0 - of_tile(reg0))[:, None] + jnp.arange(per_tile, dtype=jnp.int32)[None, :]
    col_of_tile = lambda v: jnp.sum(jnp.where(pick[:, None, :], v[None, :, :], 0), axis=2)
    seg0_t, seg1_t, run0_t = col_of_tile(seg0), col_of_tile(seg0 + chunks), col_of_tile(run0)
    holds = (q[:, :, None] >= seg0_t[:, None, :]) & (q[:, :, None] < seg1_t[:, None, :])
    local = run0_t[:, None, :] + q[:, :, None] - seg0_t[:, None, :]
    row = jnp.arange(n_tok, dtype=jnp.int32)[None, None, :] * slots + SUBLANES * local
    src = jnp.sum(jnp.where(holds, row, 0), axis=2)
    src = jnp.where(jnp.any(holds, axis=2), src, slots - SUBLANES)
    src = jnp.pad(src, ((0, 0), (0, LIST_LANES - per_tile))).reshape(n_tiles, 1, LIST_LANES).astype(jnp.int32)

    j = jnp.arange(LIST_LANES, dtype=jnp.int32)[None, :, None]
    inside = (j >= run0[:, None, :]) & (j < (run0 + chunks)[:, None, :])
    base = (reg0[None, :] + seg0 - run0)[:, None, :]
    dst = SUBLANES * jnp.sum(jnp.where(inside, base + j, 0), axis=2)
    dst = dst.at[:, LIST_COUNT].set(n_run).reshape(n_tok, 1, LIST_LANES).astype(jnp.int32)
    return tile_expert.astype(jnp.int32), tile_valid, src, dst


def _moe_ple2(h, p, g_ffn, w_router, before, sel, w_gate, w_up, w_down, g_ple, wgate_bf, wproj_bf, *, tm, te):
    xs, rec, cnt = _sort(h, g_ffn, w_router, before, sel, tm=tm)
    tile_expert, tile_valid, src, dst = _chunk_plan(cnt[:, 0, :N_EXPERTS].astype(jnp.int32), tm=tm, te=te)
    ys = _experts2(xs, tile_expert, tile_valid, src, w_gate, w_up, w_down, te=te)
    return _combine2(h, rec, dst, ys, p, g_ple, wgate_bf, wproj_bf, tm=tm)


def _bucket_np(dist):
    max_exact = N_BUCKETS // 2
    d_f = np.maximum(dist, 1).astype(np.float32)
    large = max_exact + (np.log(d_f / np.float32(max_exact)) / np.float32(np.log(MAX_DISTANCE / max_exact))
                         * np.float32(N_BUCKETS - max_exact)).astype(np.int32)
    large = np.minimum(large, N_BUCKETS - 1)
    return np.where(dist < max_exact, dist, large).astype(np.int32)


def _bias_from_buckets(rel_bias, bucket, valid):
    onehot = (jnp.asarray(bucket)[..., None] == jnp.arange(N_BUCKETS, dtype=jnp.int32)).astype(F32)
    bias = jnp.einsum("...k,kh->h...", onehot, rel_bias.astype(F32), precision=HIGHEST)
    return jnp.where(jnp.asarray(valid)[None], bias, NEG)


def _block_order(dil):
    g = np.arange(Q_BLOCK) // SUBLANES
    j = np.arange(Q_BLOCK) % SUBLANES
    if dil == 1:
        return 16 * j + g
    if dil == 4:
        return 32 * (g // 4) + 4 * j + g % 4
    return SUBLANES * g + j


def _band_bias(rel_bias, dil):
    mu = _block_order(dil)
    qi = mu[:, None] + Q_BLOCK
    ki = np.concatenate([mu, mu + Q_BLOCK])[None, :]
    off = qi - ki
    valid = (off >= 0) & (off <= N_KEYS)
    bucket = _bucket_np(dil * np.clip(off, 0, N_KEYS))
    first = valid & (np.arange(2 * Q_BLOCK)[None, :] >= Q_BLOCK)
    tables = [_bias_from_buckets(rel_bias, bucket, v).reshape(N_HEADS * Q_BLOCK, 2 * Q_BLOCK) for v in (valid, first)]
    return jnp.stack(tables)


def _sample_tables(rel_bias, w_buf):
    qpos = w_buf + np.arange(SAMPLE_T)[:, None]
    pos = np.arange(w_buf + NEW_COLS)[None, :]
    dist = qpos - pos
    in_seq = (dist >= 0) & (pos < w_buf + SAMPLE_T)
    mult = np.zeros(dist.shape, np.float32)
    for (w, d) in PATTERNS:
        mult += in_seq & (dist % d == 0) & (dist <= w)
    bucket = _bucket_np(np.maximum(dist, 0))
    bias = jnp.transpose(_bias_from_buckets(rel_bias, bucket, mult > 0), (1, 0, 2))
    rows = SAMPLE_T * N_HEADS
    mult_rows = np.broadcast_to(mult[:, None, :], (SAMPLE_T, N_HEADS, mult.shape[-1]))
    return bias.reshape(rows, -1), jnp.asarray(mult_rows.reshape(rows, -1))


PROMPT_TM = 256
PROMPT_TE = 256
ATTN_SUB = 2


def _row_perm(tm):
    a = np.arange(tm)
    src = (a // Q_BLOCK) * Q_BLOCK + 16 * (a % SUBLANES) + (a % Q_BLOCK) // SUBLANES
    perm = np.zeros((tm, tm), np.float32)
    perm[a, src] = 1.0
    return perm


def kernel(x_prompt, x_sample, cache_k, cache_v, state_conv, p_prompt, p_sample, rel_bias, g_mix, w_in, q_gain,
           k_gain, conv_w, g_out_att, g_out_conv, w_out, g_ffn, w_router_group, w_router_expert, w_gate, w_up,
           w_down, g_ple, w_ple_gate, w_ple_proj):
    depth = w_in.shape[0]
    batch, seq, _ = x_prompt.shape
    dec_b, dec_t, _ = x_sample.shape
    w_buf = cache_k.shape[2]
    n_s = dec_b * dec_t
    keep = min(w_buf, seq)
    assert dec_t == SAMPLE_T and w_buf % LANES == 0
    assert seq % (Q_BLOCK * 16 * ATTN_SUB) == 0 and keep % PROMPT_TM == 0
    cache_kt = jnp.transpose(cache_k, (0, 1, 3, 4, 2))
    cache_vt = jnp.transpose(cache_v, (0, 1, 3, 4, 2))

    row = lambda a: a.reshape(1, -1)
    member = (np.arange(ATT_DIM)[:, None] // HEAD_DIM == np.arange(LANES)[None, :])
    head_mean = jnp.asarray(member / HEAD_DIM, BF16)
    head_exp = jnp.asarray(member.T, BF16)
    src_lane = np.arange(LANES)
    expand = jnp.asarray((src_lane[:, None] // LSE_LANES_PER_HEAD == np.arange(ATT_DIM)[None, :] // HEAD_DIM)
                         & (src_lane[:, None] % LSE_LANES_PER_HEAD == 0), BF16)
    before = jnp.asarray(np.arange(LANES)[:, None] < np.arange(LANES)[None, :], BF16)
    sel_np = np.zeros((SUBLANES, LANES), np.float32)
    sel_np[0, REC_S1] = sel_np[1, REC_S2] = 1.0
    sel = jnp.asarray(sel_np, BF16)
    perm_np = _row_perm(PROMPT_TM)
    perm, unperm = jnp.asarray(perm_np, BF16), jnp.asarray(perm_np.T, BF16)
    band = [_band_bias(rel_bias, d) for (_, d) in PATTERNS]
    s_bias, s_mult = _sample_tables(rel_bias, w_buf)

    hp = x_prompt.reshape(batch * seq, D_MODEL)
    hs = jnp.swapaxes(x_sample, 0, 1).reshape(n_s, D_MODEL)
    new = {k: [] for k in ("kp", "vp", "cp", "ks", "vs", "cs")}
    hist_p = jnp.zeros((batch, SUBLANES, CONV_DIM), F32)

    for l in range(depth):
        w_in_bf = w_in[l].astype(BF16)
        wa_bf = w_out[l, :ATT_DIM].astype(BF16)
        wc_bf = w_out[l, ATT_DIM:].astype(BF16)
        wgate_bf = w_ple_gate[l].astype(BF16)
        wproj_bf = w_ple_proj[l].astype(BF16)
        w_router = jnp.concatenate(
            [w_router_expert[l], w_router_group[l],
             jnp.zeros((D_MODEL, LANES - N_EXPERTS - N_GROUPS), F32)], axis=1).astype(BF16)
        qg, kg = row(jnp.tile(q_gain[l], N_HEADS)), row(jnp.tile(k_gain[l], N_HEADS))
        mix = (row(g_mix[l]), w_in_bf, qg, kg, head_mean, head_exp, conv_w[l], row(g_out_conv[l]))
        moe = (row(g_ffn[l]), w_router, before, sel, w_gate[l], w_up[l], w_down[l], row(g_ple[l]), wgate_bf,
               wproj_bf)

        q, k, v, k_nat, v_nat, yn, nconv = _inproj(
            hp, hist_p, *mix, perm, tm=PROMPT_TM, shift=1, tiles_per_seq=seq // PROMPT_TM,
            keep_tiles=keep // PROMPT_TM)
        os, lses = [], []
        for bias, (_, d) in zip(band, PATTERNS):
            o, lse = _attn_pattern(q, k, v, bias, batch=batch, seq=seq, dil=d, sub=ATTN_SUB)
            os.append(o)
            lses.append(lse)
        hp = _outproj(os, lses, yn, hp, row(g_out_att[l]), wa_bf, wc_bf, expand, unperm, tm=PROMPT_TM)
        hp = _moe_ple2(hp, p_prompt[l].reshape(batch * seq, D_PLE), *moe, tm=PROMPT_TM, te=PROMPT_TE)
        new["kp"].append(jnp.transpose(k_nat, (0, 3, 1, 2)))
        new["vp"].append(jnp.transpose(v_nat, (0, 3, 1, 2)))
        new["cp"].append(nconv[:, SUBLANES - 2:])

        hist_s = jnp.swapaxes(state_conv[l], 0, 1).reshape(1, 2 * dec_b, CONV_DIM)
        q, k, v, yn, nconv = _inproj(hs, hist_s, *mix, tm=n_s, shift=dec_b, tiles_per_seq=1)
        bmaj = lambda a: jnp.swapaxes(a.reshape(dec_t, dec_b, N_HEADS, HEAD_DIM), 0, 1)
        qb, kb, vb = bmaj(q), bmaj(k), bmaj(v)
        feat = lambda a: jnp.pad(jnp.transpose(a, (0, 2, 3, 1)), ((0, 0), (0, 0), (0, 0), (0, NEW_COLS - dec_t)))
        att = _attn_sample(qb.reshape(dec_b, dec_t, ATT_DIM), feat(kb), feat(vb), cache_kt, cache_vt, l,
                           s_bias, s_mult)
        att_tm = jnp.swapaxes(att, 0, 1).reshape(n_s, ATT_DIM)
        hs = _outproj([att_tm], [], yn, hs, row(g_out_att[l]), wa_bf, wc_bf, tm=n_s)
        p_s = jnp.swapaxes(p_sample[l], 0, 1).reshape(n_s, D_PLE)
        hs = _moe_ple2(hs, p_s, *moe, tm=n_s, te=LANES)
        new["ks"].append(kb)
        new["vs"].append(vb)
        new["cs"].append(jnp.swapaxes(nconv.reshape(2, dec_b, CONV_DIM), 0, 1))

    y_prompt = hp.reshape(batch, seq, D_MODEL)
    y_sample = jnp.swapaxes(hs.reshape(dec_t, dec_b, D_MODEL), 0, 1)
    st = lambda key: jnp.stack(new[key])
    return (y_prompt, y_sample, st("kp"), st("vp"), st("cp"), st("ks"), st("vs"), st("cs"))
```

```python
import functools

import jax
import jax.numpy as jnp
import numpy as np
from jax import lax
from jax.experimental import pallas as pl
from jax.experimental.pallas import tpu as pltpu

F32 = jnp.float32
BF16 = jnp.bfloat16
HIGHEST = lax.Precision.HIGHEST

D_MODEL = 1024
HEAD_DIM = 64
N_HEADS = 8
ATT_DIM = N_HEADS * HEAD_DIM
CONV_DIM = D_MODEL - ATT_DIM
MIX_IN = 3 * ATT_DIM + 3 * CONV_DIM
PATTERNS = ((128, 1), (512, 4), (2048, 16))
N_KEYS = 128
Q_BLOCK = 128
N_BUCKETS = 32
MAX_DISTANCE = 2048
N_GROUPS = 4
EXPERTS_PER_GROUP = 8
N_EXPERTS = N_GROUPS * EXPERTS_PER_GROUP
D_EXPERT = 256
D_PLE = 256
EPS = 1e-6
NEG = -1e30

LANES = 128
SUBLANES = 8
SLABS = Q_BLOCK // SUBLANES
LSE_LANES_PER_HEAD = LANES // N_HEADS
VMEM_LIMIT = 56 * 1024 * 1024
NT = (((1,), (1,)), ((), ()))


def _cparams(n_axes):
    return pltpu.CompilerParams(dimension_semantics=("arbitrary",) * n_axes,
                                vmem_limit_bytes=VMEM_LIMIT)


def _full(shape):
    n = len(shape)
    return pl.BlockSpec(shape, lambda *_: (0,) * n)


def _rms(x, gain):
    ms = jnp.mean(x * x, axis=-1, keepdims=True)
    return x * lax.rsqrt(ms + EPS) * gain


def _exact_dot(x, e_ref):
    hi = x.astype(BF16)
    r1 = x - hi.astype(F32)
    mid = r1.astype(BF16)
    lo = (r1 - mid.astype(F32)).astype(BF16)
    e = e_ref[...]
    return (jnp.dot(hi, e, preferred_element_type=F32) + jnp.dot(mid, e, preferred_element_type=F32)
            + jnp.dot(lo, e, preferred_element_type=F32))


def _inproj_kernel(*refs, shift, tiles_per_seq, permute):
    (h_ref, gmix_ref, w_ref, qg_ref, kg_ref, hmean_ref, hexp_ref, cw_ref, gconv_ref, hist_ref) = refs[:10]
    if permute:
        perm_ref, q_ref, k_ref, v_ref, kn_ref, vn_ref, yn_ref, nconv_ref, carry_ref = refs[10:]
    else:
        q_ref, k_ref, v_ref, yn_ref, nconv_ref, carry_ref = refs[10:]
    i = pl.program_id(0)
    a = _rms(h_ref[...], gmix_ref[...])
    proj = jnp.dot(a.astype(BF16), w_ref[...], preferred_element_type=F32)
    tm = proj.shape[0]

    def head_norm(t, g):
        ms = _exact_dot(t * t, hmean_ref)
        return t * _exact_dot(lax.rsqrt(ms + EPS), hexp_ref) * g

    q = head_norm(proj[:, 0:ATT_DIM], qg_ref[...])
    k = head_norm(proj[:, ATT_DIM:2 * ATT_DIM], kg_ref[...])
    v = proj[:, 2 * ATT_DIM:3 * ATT_DIM]
    if permute:
        kn_ref[...] = k.T.reshape(N_HEADS, HEAD_DIM, tm)
        vn_ref[...] = v.T.reshape(N_HEADS, HEAD_DIM, tm)
        qkv = jnp.concatenate([q * (HEAD_DIM ** -0.5), k, v], axis=-1).astype(BF16)
        moved = jnp.dot(perm_ref[...], qkv, preferred_element_type=F32)
        q_ref[...] = moved[:, 0:ATT_DIM]
        k_ref[...] = moved[:, ATT_DIM:2 * ATT_DIM]
        v_ref[...] = moved[:, 2 * ATT_DIM:3 * ATT_DIM]
    else:
        q_ref[...] = q
        k_ref[...] = k
        v_ref[...] = v
    c0 = 3 * ATT_DIM
    hc = proj[:, c0:c0 + CONV_DIM]
    gb = proj[:, c0 + CONV_DIM:c0 + 2 * CONV_DIM]
    gc = proj[:, c0 + 2 * CONV_DIM:c0 + 3 * CONV_DIM]
    u = gc * hc

    if shift == 1:
        @pl.when(i % tiles_per_seq == 0)
        def _():
            carry_ref[...] = hist_ref[0]
        h0 = carry_ref[SUBLANES - 2:SUBLANES - 1, :]
        h1 = carry_ref[SUBLANES - 1:SUBLANES, :]
        row = lax.broadcasted_iota(jnp.int32, (tm, 1), 0)
        u1 = jnp.where(row == 0, h1, pltpu.roll(u, 1, 0))
        u2 = jnp.where(row == 0, h0, jnp.where(row == 1, h1, pltpu.roll(u, 2, 0)))
        carry_ref[...] = u[tm - SUBLANES:tm, :]
        nconv_ref[0] = u[tm - SUBLANES:tm, :]
    else:
        hist = hist_ref[0]
        u1 = jnp.concatenate([hist[shift:2 * shift], u[0:tm - shift]], axis=0)
        u2 = jnp.concatenate([hist, u[0:tm - 2 * shift]], axis=0)
        nconv_ref[0] = u[tm - 2 * shift:tm, :]
    conv = cw_ref[0:1, :] * u2 + cw_ref[1:2, :] * u1 + cw_ref[2:3, :] * u
    yn_ref[...] = _rms(gb * conv, gconv_ref[...])


def _inproj(h, hist, g_mix, w_in_bf, q_gain, k_gain, hmean, hexp, conv_w, g_out_conv, perm=None, *,
            tm, shift, tiles_per_seq, keep_tiles=0):
    n = h.shape[0]
    hist_rows = hist.shape[1]
    nseq = hist.shape[0]
    tok = lambda w: pl.BlockSpec((tm, w), lambda i: (i, 0))
    seq3 = lambda r: pl.BlockSpec((1, r, CONV_DIM), lambda i: (i // tiles_per_seq, 0, 0))
    nconv_rows = SUBLANES if shift == 1 else 2 * shift
    att = jax.ShapeDtypeStruct((n, ATT_DIM), F32)
    in_specs = [tok(D_MODEL), _full((1, D_MODEL)), _full((D_MODEL, MIX_IN)), _full((1, ATT_DIM)),
                _full((1, ATT_DIM)), _full((ATT_DIM, LANES)), _full((LANES, ATT_DIM)), _full((3, CONV_DIM)),
                _full((1, CONV_DIM)), seq3(hist_rows)]
    args = [h, g_mix, w_in_bf, q_gain, k_gain, hmean, hexp, conv_w, g_out_conv, hist]
    out_specs = [tok(ATT_DIM)] * 3
    out_shape = [att] * 3
    if perm is not None:
        in_specs.append(_full((tm, tm)))
        args.append(perm)
        first = tiles_per_seq - keep_tiles
        kept = pl.BlockSpec((None, N_HEADS, HEAD_DIM, tm),
                            lambda i: (i // tiles_per_seq, 0, 0, jnp.maximum(i % tiles_per_seq - first, 0)))
        out_specs += [kept, kept]
        out_shape += [jax.ShapeDtypeStruct((nseq, N_HEADS, HEAD_DIM, keep_tiles * tm), F32)] * 2
    out_specs += [tok(CONV_DIM), seq3(nconv_rows)]
    out_shape += [jax.ShapeDtypeStruct((n, CONV_DIM), F32),
                  jax.ShapeDtypeStruct((nseq, nconv_rows, CONV_DIM), F32)]
    return pl.pallas_call(
        functools.partial(_inproj_kernel, shift=shift, tiles_per_seq=tiles_per_seq, permute=perm is not None),
        grid=(n // tm,),
        in_specs=in_specs, out_specs=out_specs, out_shape=out_shape,
        scratch_shapes=[pltpu.VMEM((SUBLANES, CONV_DIM), F32)],
        compiler_params=_cparams(1),
        name="inproj",
    )(*args)


def _attn_block(q, kk, vv, bias):
    lane = lax.broadcasted_iota(jnp.int32, (Q_BLOCK, LANES), 1)
    upper = lane >= HEAD_DIM
    scores = []
    for h in range(N_HEADS):
        j, e = divmod(h, 2)
        qp = q[:, j * LANES:(j + 1) * LANES]
        qm = (jnp.where(upper, qp, 0.0) if e else jnp.where(upper, 0.0, qp)).astype(BF16)
        scores.append(lax.dot_general(qm, kk[:, j * LANES:(j + 1) * LANES], NT, preferred_element_type=F32))
    s = jnp.concatenate(scores, axis=0) + bias
    m = jnp.max(s, axis=-1, keepdims=True)
    p = jnp.exp(s - m)
    den = jnp.sum(p, axis=-1, keepdims=True)
    pb = p.astype(BF16)
    inv = 1.0 / den
    lse = m + jnp.log(den)
    lse_grp = lane // LSE_LANES_PER_HEAD
    lse_tile = jnp.zeros((Q_BLOCK, LANES), F32)
    outs = []
    for j in range(N_HEADS // 2):
        pair = None
        for e in range(2):
            h = 2 * j + e
            rows = slice(h * Q_BLOCK, (h + 1) * Q_BLOCK)
            o = jnp.dot(pb[rows], vv[:, j * LANES:(j + 1) * LANES], preferred_element_type=F32) * inv[rows]
            pair = o if e == 0 else jnp.where(upper, o, pair)
            lse_tile = jnp.where(lse_grp == h, lse[rows], lse_tile)
        outs.append(pair)
    return jnp.concatenate(outs, axis=-1), lse_tile


def _attn_kernel(q_ref, kp_ref, kc_ref, vp_ref, vc_ref, bias_ref, o_ref, lse_ref, kbuf, vbuf, *, sub):
    n = pl.program_id(2)
    rows = sub * Q_BLOCK
    kbuf[0:Q_BLOCK, :] = kp_ref[...].reshape(Q_BLOCK, ATT_DIM).astype(BF16)
    kbuf[Q_BLOCK:, :] = kc_ref[...].reshape(rows, ATT_DIM).astype(BF16)
    vbuf[0:Q_BLOCK, :] = vp_ref[...].reshape(Q_BLOCK, ATT_DIM).astype(BF16)
    vbuf[Q_BLOCK:, :] = vc_ref[...].reshape(rows, ATT_DIM).astype(BF16)

    for j in range(sub):
        q = q_ref[j].reshape(Q_BLOCK, ATT_DIM)
        r0 = j * Q_BLOCK
        first = jnp.logical_and(n == 0, j == 0).astype(jnp.int32) if j == 0 else 0
        o, lse = _attn_block(q, kbuf[r0:r0 + 2 * Q_BLOCK, :], vbuf[r0:r0 + 2 * Q_BLOCK, :], bias_ref[first])
        o_ref[j] = o.reshape(SLABS, SUBLANES, ATT_DIM)
        lse_ref[j] = lse.reshape(SLABS, SUBLANES, LANES)


def _attn_pattern(q, k, v, bias, *, batch, seq, dil, sub):
    nblk = seq // (Q_BLOCK * dil)
    view = lambda t: t.reshape(batch, nblk, SLABS, dil, SUBLANES, t.shape[-1])
    cur = lambda c: pl.BlockSpec((None, sub, SLABS, None, SUBLANES, c), lambda b, r, n: (b, n, 0, r, 0, 0))
    prev = pl.BlockSpec((None, None, SLABS, None, SUBLANES, ATT_DIM),
                        lambda b, r, n: (b, jnp.maximum(n * sub - 1, 0), 0, r, 0, 0))
    o, lse = pl.pallas_call(
        functools.partial(_attn_kernel, sub=sub),
        grid=(batch, dil, nblk // sub),
        in_specs=[cur(ATT_DIM), prev, cur(ATT_DIM), prev, cur(ATT_DIM),
                  _full((2, N_HEADS * Q_BLOCK, 2 * Q_BLOCK))],
        out_specs=[cur(ATT_DIM), cur(LANES)],
        out_shape=[jax.ShapeDtypeStruct((batch, nblk, SLABS, dil, SUBLANES, ATT_DIM), F32),
                   jax.ShapeDtypeStruct((batch, nblk, SLABS, dil, SUBLANES, LANES), F32)],
        scratch_shapes=[pltpu.VMEM(((sub + 1) * Q_BLOCK, ATT_DIM), BF16),
                        pltpu.VMEM(((sub + 1) * Q_BLOCK, ATT_DIM), BF16)],
        compiler_params=_cparams(3),
        name=f"attn_d{dil}",
    )(view(q), view(k), view(k), view(v), view(v), bias)
    return o.reshape(batch * seq, ATT_DIM), lse.reshape(batch * seq, LANES)


SAMPLE_T = 4
NEW_COLS = LANES


def _attn_sample_kernel(q_ref, kt_ref, ktn_ref, vt_ref, vtn_ref, bias_ref, mult_ref, o_ref):
    rows = SAMPLE_T * N_HEADS
    q4 = q_ref[...] * (HEAD_DIM ** -0.5)
    qt = jnp.concatenate([jnp.broadcast_to(q4[t:t + 1, :], (N_HEADS, ATT_DIM)) for t in range(SAMPLE_T)], axis=0)
    lane_head = lax.broadcasted_iota(jnp.int32, (rows, ATT_DIM), 1) // HEAD_DIM
    row_head = lax.broadcasted_iota(jnp.int32, (rows, ATT_DIM), 0) % N_HEADS
    own = lane_head == row_head
    qbd = jnp.where(own, qt, 0.0).astype(BF16)
    flat = lambda ref: ref[...].reshape(ATT_DIM, ref.shape[-1]).astype(BF16)
    s = jnp.concatenate([jnp.dot(qbd, flat(kt_ref), preferred_element_type=F32),
                         jnp.dot(qbd, flat(ktn_ref), preferred_element_type=F32)], axis=-1) + bias_ref[...]
    m = jnp.max(s, axis=-1, keepdims=True)
    p = jnp.exp(s - m) * mult_ref[...]
    den = jnp.sum(p, axis=-1, keepdims=True)
    pb = p.astype(BF16)
    w_buf = kt_ref.shape[-1]
    acc = (lax.dot_general(pb[:, :w_buf], flat(vt_ref), NT, preferred_element_type=F32)
           + lax.dot_general(pb[:, w_buf:], flat(vtn_ref), NT, preferred_element_type=F32))
    acc = jnp.where(own, acc / den, 0.0)
    for t in range(SAMPLE_T):
        o_ref[t:t + 1, :] = jnp.sum(acc[t * N_HEADS:(t + 1) * N_HEADS, :], axis=0, keepdims=True)


def _attn_sample(q, kt_new, vt_new, cache_kt, cache_vt, layer, bias, mult):
    nb = q.shape[0]
    w_buf = cache_kt.shape[-1]
    tok = pl.BlockSpec((None, SAMPLE_T, ATT_DIM), lambda b: (b, 0, 0))
    new = pl.BlockSpec((None, N_HEADS, HEAD_DIM, NEW_COLS), lambda b: (b, 0, 0, 0))
    old = pl.BlockSpec((None, None, N_HEADS, HEAD_DIM, w_buf), lambda b: (layer, b, 0, 0, 0))
    tbl = _full((SAMPLE_T * N_HEADS, w_buf + NEW_COLS))
    return pl.pallas_call(
        _attn_sample_kernel,
        grid=(nb,),
        in_specs=[tok, old, new, old, new, tbl, tbl],
        out_specs=tok,
        out_shape=jax.ShapeDtypeStruct((nb, SAMPLE_T, ATT_DIM), F32),
        compiler_params=_cparams(1),
        name="attn_sample",
    )(q, cache_kt, kt_new, cache_vt, vt_new, bias, mult)


def _split_dot(x, e_ref):
    hi = x.astype(BF16)
    lo = (x - hi.astype(F32)).astype(BF16)
    return (jnp.dot(hi, e_ref[...], preferred_element_type=F32)
            + jnp.dot(lo, e_ref[...], preferred_element_type=F32))


def _outproj_kernel(*refs, n_pat):
    mix = n_pat > 1
    n_lse = n_pat if mix else 0
    o_refs = refs[0:n_pat]
    l_refs = refs[n_pat:n_pat + n_lse]
    rest = refs[n_pat + n_lse:]
    if mix:
        yn_ref, h_ref, gatt_ref, exp_ref, unperm_ref, wa_ref, wc_ref, out_ref = rest
        lses = [r[...] for r in l_refs]
        top = functools.reduce(jnp.maximum, lses)
        ws = [jnp.exp(l - top) for l in lses]
        tot = functools.reduce(lambda a, b: a + b, ws)
        att = None
        for w, o_ref in zip(ws, o_refs):
            term = _split_dot(w / tot, exp_ref) * o_ref[...]
            att = term if att is None else att + term
        att_bf = jnp.dot(unperm_ref[...], _rms(att, gatt_ref[...]).astype(BF16),
                         preferred_element_type=F32).astype(BF16)
    else:
        yn_ref, h_ref, gatt_ref, wa_ref, wc_ref, out_ref = rest
        att_bf = _rms(o_refs[0][...], gatt_ref[...]).astype(BF16)
    y = (jnp.dot(att_bf, wa_ref[...], preferred_element_type=F32)
         + jnp.dot(yn_ref[...].astype(BF16), wc_ref[...], preferred_element_type=F32))
    out_ref[...] = h_ref[...] + y


def _outproj(os, lses, yn, h, g_att, wa_bf, wc_bf, expand=None, unperm=None, *, tm):
    n = h.shape[0]
    n_pat = len(os)
    tok = lambda w: pl.BlockSpec((tm, w), lambda i: (i, 0))
    in_specs = [tok(ATT_DIM)] * n_pat + [tok(LANES)] * len(lses) + [tok(CONV_DIM), tok(D_MODEL), _full((1, ATT_DIM))]
    args = [*os, *lses, yn, h, g_att]
    if n_pat > 1:
        in_specs += [_full((LANES, ATT_DIM)), _full((tm, tm))]
        args += [expand, unperm]
    in_specs += [_full((ATT_DIM, D_MODEL)), _full((CONV_DIM, D_MODEL))]
    args += [wa_bf, wc_bf]
    return pl.pallas_call(
        functools.partial(_outproj_kernel, n_pat=n_pat),
        grid=(n // tm,),
        in_specs=in_specs,
        out_specs=tok(D_MODEL),
        out_shape=jax.ShapeDtypeStruct((n, D_MODEL), F32),
        compiler_params=_cparams(1),
        name="outproj",
    )(*args)


ROUTE_I1, ROUTE_I2, ROUTE_R1, ROUTE_R2, ROUTE_W1, ROUTE_W2 = range(6)
GROUP_LANE0 = N_EXPERTS


def _route_kernel(h_ref, g_ref, wr_ref, route_ref, cnt_ref, carry_ref):
    i = pl.program_id(0)

    @pl.when(i == 0)
    def _():
        carry_ref[...] = jnp.zeros_like(carry_ref)

    m = _rms(h_ref[...], g_ref[...])
    logits = jnp.dot(m.astype(BF16), wr_ref[...], preferred_element_type=F32)
    tm = logits.shape[0]
    lane_i = lax.broadcasted_iota(jnp.int32, (tm, LANES), 1)
    lane = lane_i.astype(F32)
    big = jnp.float32(4 * LANES)

    is_g = jnp.logical_and(lane_i >= GROUP_LANE0, lane_i < GROUP_LANE0 + N_GROUPS)
    gl = jnp.where(is_g, logits, NEG)
    gmax = jnp.max(gl, axis=-1, keepdims=True)
    g_w = 1.0 / jnp.sum(jnp.where(is_g, jnp.exp(gl - gmax), 0.0), axis=-1, keepdims=True)
    g_sel = jnp.min(jnp.where(gl == gmax, lane - GROUP_LANE0, big), axis=-1, keepdims=True)

    grp_of_lane = (lane_i // EXPERTS_PER_GROUP).astype(F32)
    in_grp = jnp.logical_and(lane_i < N_EXPERTS, grp_of_lane == g_sel)
    el = jnp.where(in_grp, logits, NEG)
    t1 = jnp.max(el, axis=-1, keepdims=True)
    i1 = jnp.min(jnp.where(el == t1, lane, big), axis=-1, keepdims=True)
    el2 = jnp.where(lane == i1, NEG, el)
    t2 = jnp.max(el2, axis=-1, keepdims=True)
    i2 = jnp.min(jnp.where(el2 == t2, lane, big), axis=-1, keepdims=True)
    e2 = jnp.exp(t2 - t1)
    w1 = g_w / (1.0 + e2)
    w2 = g_w * e2 / (1.0 + e2)

    hit1 = lane == i1
    hit2 = lane == i2
    c = jnp.where(jnp.logical_or(hit1, hit2), 1.0, 0.0)
    rr = lax.broadcasted_iota(jnp.int32, (tm, tm), 0)
    cc = lax.broadcasted_iota(jnp.int32, (tm, tm), 1)
    lower = jnp.where(rr > cc, 1.0, 0.0).astype(BF16)
    before = jnp.dot(lower, c.astype(BF16), preferred_element_type=F32) + carry_ref[0:1, :]
    r1 = jnp.sum(jnp.where(hit1, before, 0.0), axis=-1, keepdims=True)
    r2 = jnp.sum(jnp.where(hit2, before, 0.0), axis=-1, keepdims=True)
    total = carry_ref[0:1, :] + jnp.sum(c, axis=0, keepdims=True)
    carry_ref[...] = jnp.broadcast_to(total, carry_ref.shape)
    cnt_ref[...] = jnp.broadcast_to(total, cnt_ref.shape)

    rec = jnp.zeros((tm, LANES), F32)
    for idx, val in ((ROUTE_I1, i1), (ROUTE_I2, i2), (ROUTE_R1, r1), (ROUTE_R2, r2), (ROUTE_W1, w1), (ROUTE_W2, w2)):
        rec = jnp.where(lane_i == idx, val, rec)
    route_ref[...] = rec


def _route(h, g_ffn, w_router, *, tm):
    n = h.shape[0]
    tok = lambda w: pl.BlockSpec((tm, w), lambda i: (i, 0))
    return pl.pallas_call(
        _route_kernel,
        grid=(n // tm,),
        in_specs=[tok(D_MODEL), _full((1, D_MODEL)), _full((D_MODEL, LANES))],
        out_specs=[tok(LANES), _full((SUBLANES, LANES))],
        out_shape=[jax.ShapeDtypeStruct((n, LANES), F32), jax.ShapeDtypeStruct((SUBLANES, LANES), F32)],
        scratch_shapes=[pltpu.VMEM((SUBLANES, LANES), F32)],
        compiler_params=_cparams(1),
        name="route",
    )(h, g_ffn, w_router)


def _dispatch_kernel(pos_ref, h_ref, g_ref, xs_ref, mbuf, sem):
    tm = mbuf.shape[0]
    mbuf[...] = _rms(h_ref[...], g_ref[...])

    def row_copy(t, p):
        return pltpu.make_async_copy(mbuf.at[pl.ds(t, 1), :], xs_ref.at[pl.ds(p, 1), :], sem)

    def issue(t, carry):
        row_copy(t, pos_ref[0, 0, 2 * t]).start()
        row_copy(t, pos_ref[0, 0, 2 * t + 1]).start()
        return carry

    lax.fori_loop(0, tm, issue, 0)

    def drain(t, carry):
        row_copy(0, 0).wait()
        row_copy(0, 0).wait()
        return carry

    lax.fori_loop(0, tm, drain, 0)


def _dispatch(h, g_ffn, pos, n_rows, *, tm):
    n = h.shape[0]
    return pl.pallas_call(
        _dispatch_kernel,
        grid=(n // tm,),
        in_specs=[pl.BlockSpec((1, 1, 2 * tm), lambda i: (i, 0, 0), memory_space=pltpu.SMEM),
                  pl.BlockSpec((tm, D_MODEL), lambda i: (i, 0)), _full((1, D_MODEL))],
        out_specs=pl.BlockSpec(memory_space=pl.ANY),
        out_shape=jax.ShapeDtypeStruct((n_rows, D_MODEL), F32),
        scratch_shapes=[pltpu.VMEM((tm, D_MODEL), F32), pltpu.SemaphoreType.DMA(())],
        compiler_params=pltpu.CompilerParams(dimension_semantics=("arbitrary",), vmem_limit_bytes=VMEM_LIMIT,
                                             has_side_effects=True),
        name="dispatch",
    )(pos, h, g_ffn)


def _expert_kernel(te_ref, tv_ref, x_ref, wg_ref, wu_ref, wd_ref, y_ref, wg_bf, wu_bf, wd_bf):
    i = pl.program_id(0)
    valid = tv_ref[i]
    changed = jnp.logical_or(i == 0, te_ref[i] != te_ref[jnp.maximum(i - 1, 0)])

    @pl.when(jnp.logical_and(changed, valid > 0))
    def _():
        wg_bf[...] = wg_ref[0].astype(BF16)
        wu_bf[...] = wu_ref[0].astype(BF16)
        wd_bf[...] = wd_ref[0].astype(BF16)

    @pl.when(valid > 0)
    def _():
        te = x_ref.shape[0]
        row = lax.broadcasted_iota(jnp.int32, (te, 1), 0)
        x = jnp.where(row < valid, x_ref[...], 0.0).astype(BF16)
        hg = jnp.dot(x, wg_bf[...], preferred_element_type=F32)
        hu = jnp.dot(x, wu_bf[...], preferred_element_type=F32)
        hid = (hg * jax.nn.sigmoid(hg)) * hu
        y_ref[...] = jnp.dot(hid.astype(BF16), wd_bf[...], preferred_element_type=F32)

    @pl.when(valid <= 0)
    def _():
        y_ref[...] = jnp.zeros_like(y_ref)


def _experts(xs, tile_expert, tile_valid, w_gate, w_up, w_down, *, te):
    n_rows = xs.shape[0]
    wspec = lambda a, b: pl.BlockSpec((1, a, b), lambda i, e, v: (e[i], 0, 0))
    grid_spec = pltpu.PrefetchScalarGridSpec(
        num_scalar_prefetch=2,
        grid=(n_rows // te,),
        in_specs=[pl.BlockSpec((te, D_MODEL), lambda i, e, v: (i, 0)),
                  wspec(D_MODEL, D_EXPERT), wspec(D_MODEL, D_EXPERT), wspec(D_EXPERT, D_MODEL)],
        out_specs=pl.BlockSpec((te, D_MODEL), lambda i, e, v: (i, 0)),
        scratch_shapes=[pltpu.VMEM((D_MODEL, D_EXPERT), BF16), pltpu.VMEM((D_MODEL, D_EXPERT), BF16),
                        pltpu.VMEM((D_EXPERT, D_MODEL), BF16)])
    return pl.pallas_call(
        _expert_kernel,
        grid_spec=grid_spec,
        out_shape=jax.ShapeDtypeStruct((n_rows, D_MODEL), F32),
        compiler_params=_cparams(1),
        name="experts",
    )(tile_expert, tile_valid, xs, w_gate, w_up, w_down)


def _combine_kernel(pos_ref, h_ref, route_ref, ys_ref, p_ref, gple_ref, wgate_ref, wproj_ref, out_ref,
                    y0, y1, sem):
    tm = y0.shape[0]

    def row_copy(p, dst, t):
        return pltpu.make_async_copy(ys_ref.at[pl.ds(p, 1), :], dst.at[pl.ds(t, 1), :], sem)

    def issue(t, carry):
        row_copy(pos_ref[0, 0, 2 * t], y0, t).start()
        row_copy(pos_ref[0, 0, 2 * t + 1], y1, t).start()
        return carry

    lax.fori_loop(0, tm, issue, 0)

    def drain(t, carry):
        row_copy(0, y0, 0).wait()
        row_copy(0, y1, 0).wait()
        return carry

    lax.fori_loop(0, tm, drain, 0)

    rec = route_ref[...]
    w1 = rec[:, ROUTE_W1:ROUTE_W1 + 1]
    w2 = rec[:, ROUTE_W2:ROUTE_W2 + 1]
    h2 = h_ref[...] + w1 * y0[...] + w2 * y1[...]
    gate = jax.nn.sigmoid(jnp.dot(_rms(h2, gple_ref[...]).astype(BF16), wgate_ref[...], preferred_element_type=F32))
    ple = jnp.dot(p_ref[...].astype(BF16), wproj_ref[...], preferred_element_type=F32)
    out_ref[...] = h2 + ple * gate


def _combine(h, route, pos, ys, p, g_ple, wgate_bf, wproj_bf, *, tm):
    n = h.shape[0]
    tok = lambda w: pl.BlockSpec((tm, w), lambda i: (i, 0))
    return pl.pallas_call(
        _combine_kernel,
        grid=(n // tm,),
        in_specs=[pl.BlockSpec((1, 1, 2 * tm), lambda i: (i, 0, 0), memory_space=pltpu.SMEM),
                  tok(D_MODEL), tok(LANES), pl.BlockSpec(memory_space=pl.ANY), tok(D_PLE),
                  _full((1, D_MODEL)), _full((D_MODEL, D_MODEL)), _full((D_PLE, D_MODEL))],
        out_specs=tok(D_MODEL),
        out_shape=jax.ShapeDtypeStruct((n, D_MODEL), F32),
        scratch_shapes=[pltpu.VMEM((tm, D_MODEL), F32), pltpu.VMEM((tm, D_MODEL), F32),
                        pltpu.SemaphoreType.DMA(())],
        compiler_params=_cparams(1),
        name="combine",
    )(pos, h, route, ys, p, g_ple, wgate_bf, wproj_bf)


def _lookup(table, idx, size):
    hit = idx[..., None] == jnp.arange(size, dtype=jnp.int32)
    return jnp.sum(jnp.where(hit, table, 0), axis=-1)


def _moe_ple(h, p, g_ffn, w_router, w_gate, w_up, w_down, g_ple, wgate_bf, wproj_bf, *, tm, te):
    n = h.shape[0]
    route, counts = _route(h, g_ffn, w_router, tm=tm)
    counts = counts[0, :N_EXPERTS].astype(jnp.int32)
    padded = ((counts + te - 1) // te) * te
    ends = jnp.cumsum(padded)
    offs = ends - padded
    ids = route[:, ROUTE_I1:ROUTE_I2 + 1].astype(jnp.int32)
    ranks = route[:, ROUTE_R1:ROUTE_R2 + 1].astype(jnp.int32)
    pos = (_lookup(offs, ids, N_EXPERTS) + ranks).reshape(n // tm, 1, 2 * tm)
    n_tiles = (2 * n) // te + N_EXPERTS
    starts = jnp.arange(n_tiles, dtype=jnp.int32) * te
    tile_expert = jnp.minimum(jnp.sum((starts[:, None] >= ends[None, :]).astype(jnp.int32), axis=-1), N_EXPERTS - 1)
    tile_valid = jnp.clip(_lookup(counts, tile_expert, N_EXPERTS)
                          - (starts - _lookup(offs, tile_expert, N_EXPERTS)), 0, te).astype(jnp.int32)
    xs = _dispatch(h, g_ffn, pos, n_tiles * te, tm=tm)
    ys = _experts(xs, tile_expert, tile_valid, w_gate, w_up, w_down, te=te)
    return _combine(h, route, pos, ys, p, g_ple, wgate_bf, wproj_bf, tm=tm)


REC_E1, REC_E2, REC_S1, REC_S2, REC_W1, REC_W2 = range(6)
ROW_W = D_MODEL + LANES
LIST_LANES = LANES
LIST_COUNT = LIST_LANES - 1


def _slots(tm):
    need = 2 * tm + (SUBLANES - 1) * N_EXPERTS + SUBLANES
    return -(-need // LANES) * LANES


def _pieces(x):
    hi = x.astype(BF16)
    r1 = x - hi.astype(F32)
    mid = r1.astype(BF16)
    return hi, mid, (r1 - mid.astype(F32)).astype(BF16)


def _sort_kernel(h_ref, g_ref, wr_ref, before_ref, sel_ref, xs_ref, rec_ref, cnt_ref):
    m = _rms(h_ref[...], g_ref[...])
    logits = jnp.dot(m.astype(BF16), wr_ref[...], preferred_element_type=F32)
    tm = logits.shape[0]
    slots = xs_ref.shape[0]
    lane_i = lax.broadcasted_iota(jnp.int32, (tm, LANES), 1)
    lane = lane_i.astype(F32)
    big = jnp.float32(4 * LANES)

    is_g = jnp.logical_and(lane_i >= GROUP_LANE0, lane_i < GROUP_LANE0 + N_GROUPS)
    gl = jnp.where(is_g, logits, NEG)
    gmax = jnp.max(gl, axis=-1, keepdims=True)
    g_w = 1.0 / jnp.sum(jnp.where(is_g, jnp.exp(gl - gmax), 0.0), axis=-1, keepdims=True)
    g_sel = jnp.min(jnp.where(gl == gmax, lane - GROUP_LANE0, big), axis=-1, keepdims=True)

    grp_of_lane = (lane_i // EXPERTS_PER_GROUP).astype(F32)
    in_grp = jnp.logical_and(lane_i < N_EXPERTS, grp_of_lane == g_sel)
    el = jnp.where(in_grp, logits, NEG)
    t1 = jnp.max(el, axis=-1, keepdims=True)
    e1 = jnp.min(jnp.where(el == t1, lane, big), axis=-1, keepdims=True)
    el2 = jnp.where(lane == e1, NEG, el)
    t2 = jnp.max(el2, axis=-1, keepdims=True)
    e2 = jnp.min(jnp.where(el2 == t2, lane, big), axis=-1, keepdims=True)
    ex = jnp.exp(t2 - t1)
    w1 = g_w / (1.0 + ex)
    w2 = g_w * ex / (1.0 + ex)

    hit1 = lane == e1
    hit2 = lane == e2
    c = jnp.where(jnp.logical_or(hit1, hit2), 1.0, 0.0)
    rr = lax.broadcasted_iota(jnp.int32, (tm, tm), 0)
    cc = lax.broadcasted_iota(jnp.int32, (tm, tm), 1)
    lower = jnp.where(rr > cc, 1.0, 0.0).astype(BF16)
    rank = jnp.dot(lower, c.astype(BF16), preferred_element_type=F32)
    cnt = jnp.sum(c, axis=0, keepdims=True)
    chunks = jnp.floor((cnt + (SUBLANES - 1)) * (1.0 / SUBLANES))
    start = SUBLANES * jnp.dot(jnp.broadcast_to(chunks, (SUBLANES, LANES)).astype(BF16), before_ref[...],
                               preferred_element_type=F32)[0:1, :]
    slot_of = rank + start
    s1 = jnp.sum(jnp.where(hit1, slot_of, 0.0), axis=-1, keepdims=True)
    s2 = jnp.sum(jnp.where(hit2, slot_of, 0.0), axis=-1, keepdims=True)

    rec = jnp.zeros((tm, LANES), F32)
    for idx, val in ((REC_E1, e1), (REC_E2, e2), (REC_S1, s1), (REC_S2, s2), (REC_W1, w1), (REC_W2, w2)):
        rec = jnp.where(lane_i == idx, val, rec)
    rec_ref[...] = rec
    cnt_ref[0] = jnp.broadcast_to(cnt, (SUBLANES, LANES))

    rec_parts = _pieces(rec)
    srow = sum(lax.dot_general(sel_ref[...], part, NT, preferred_element_type=F32) for part in rec_parts)
    slot_id = lax.broadcasted_iota(jnp.int32, (slots, tm), 0).astype(F32)
    place = jnp.where(jnp.logical_or(slot_id == srow[0:1, :], slot_id == srow[1:2, :]), 1.0, 0.0).astype(BF16)
    payload = jnp.concatenate([m.astype(BF16), *rec_parts], axis=-1)
    moved = jnp.dot(place, payload, preferred_element_type=F32)
    info = (moved[:, D_MODEL:D_MODEL + LANES] + moved[:, D_MODEL + LANES:D_MODEL + 2 * LANES]
            + moved[:, D_MODEL + 2 * LANES:])
    xs_ref[...] = jnp.concatenate([moved[:, :D_MODEL], info], axis=-1)


def _sort(h, g_ffn, w_router, before, sel, *, tm):
    n = h.shape[0]
    slots = _slots(tm)
    tok = lambda w: pl.BlockSpec((tm, w), lambda i: (i, 0))
    return pl.pallas_call(
        _sort_kernel,
        grid=(n // tm,),
        in_specs=[tok(D_MODEL), _full((1, D_MODEL)), _full((D_MODEL, LANES)), _full((LANES, LANES)),
                  _full((SUBLANES, LANES))],
        out_specs=[pl.BlockSpec((slots, ROW_W), lambda i: (i, 0)), tok(LANES),
                   pl.BlockSpec((1, SUBLANES, LANES), lambda i: (i, 0, 0))],
        out_shape=[jax.ShapeDtypeStruct((n // tm * slots, ROW_W), F32), jax.ShapeDtypeStruct((n, LANES), F32),
                   jax.ShapeDtypeStruct((n // tm, SUBLANES, LANES), F32)],
        compiler_params=_cparams(1),
        name="moe_sort",
    )(h, g_ffn, w_router, before, sel)


def _chunk_copy(src_hbm, row, dst, c, sem):
    if not isinstance(row, int):
        row = pl.multiple_of(row, SUBLANES)
    return pltpu.make_async_copy(src_hbm.at[pl.ds(row, SUBLANES), :],
                                 dst.at[pl.ds(c * SUBLANES, SUBLANES), :], sem)


def _expert_kernel2(te_ref, tv_ref, src_ref, nxt_ref, xs_ref, wg_ref, wu_ref, wd_ref, y_ref,
                    xbuf, sem, wg_bf, wu_bf, wd_bf):
    i = pl.program_id(0)
    n = pl.num_programs(0)
    te = xbuf.shape[1]
    slot = i % 2

    def fetch(list_ref, to_slot):
        for c in range(te // SUBLANES):
            _chunk_copy(xs_ref, list_ref[0, 0, c], xbuf.at[to_slot], c, sem.at[to_slot]).start()

    @pl.when(jnp.logical_and(i == 0, tv_ref[0] > 0))
    def _():
        fetch(src_ref, 0)

    nxt = jnp.minimum(i + 1, n - 1)

    @pl.when(jnp.logical_and(i + 1 < n, tv_ref[nxt] > 0))
    def _():
        fetch(nxt_ref, 1 - slot)

    valid = tv_ref[i]
    changed = jnp.logical_or(i == 0, te_ref[i] != te_ref[jnp.maximum(i - 1, 0)])

    @pl.when(jnp.logical_and(changed, valid > 0))
    def _():
        wg_bf[...] = wg_ref[0].astype(BF16)
        wu_bf[...] = wu_ref[0].astype(BF16)
        wd_bf[...] = wd_ref[0].astype(BF16)

    @pl.when(valid > 0)
    def _():
        pltpu.make_async_copy(xbuf.at[slot], xbuf.at[slot], sem.at[slot]).wait()
        rows = xbuf[slot]
        x = rows[:, :D_MODEL].astype(BF16)
        info = rows[:, D_MODEL:]
        mine = info[:, REC_E1:REC_E1 + 1] == te_ref[i].astype(F32)
        gate = jnp.where(mine, info[:, REC_W1:REC_W1 + 1], info[:, REC_W2:REC_W2 + 1])
        hg = jnp.dot(x, wg_bf[...], preferred_element_type=F32)
        hu = jnp.dot(x, wu_bf[...], preferred_element_type=F32)
        hid = (hg * jax.nn.sigmoid(hg)) * hu * gate
        y_ref[...] = jnp.dot(hid.astype(BF16), wd_bf[...], preferred_element_type=F32)

    @pl.when(valid <= 0)
    def _():
        y_ref[...] = jnp.zeros_like(y_ref)


def _experts2(xs, tile_expert, tile_valid, src, w_gate, w_up, w_down, *, te, layer):
    n_tiles = tile_expert.shape[0]
    wspec = lambda a, b: pl.BlockSpec((None, 1, a, b), lambda i, e, v: (layer, e[i], 0, 0))
    lst = lambda shift: pl.BlockSpec((1, 1, LIST_LANES),
                                     lambda i, e, v: (jnp.minimum(i + shift, n_tiles - 1), 0, 0),
                                     memory_space=pltpu.SMEM)
    grid_spec = pltpu.PrefetchScalarGridSpec(
        num_scalar_prefetch=2,
        grid=(n_tiles,),
        in_specs=[lst(0), lst(1), pl.BlockSpec(memory_space=pl.ANY),
                  wspec(D_MODEL, D_EXPERT), wspec(D_MODEL, D_EXPERT), wspec(D_EXPERT, D_MODEL)],
        out_specs=pl.BlockSpec((te, D_MODEL), lambda i, e, v: (i, 0)),
        scratch_shapes=[pltpu.VMEM((2, te, ROW_W), F32), pltpu.SemaphoreType.DMA((2,)),
                        pltpu.VMEM((D_MODEL, D_EXPERT), BF16), pltpu.VMEM((D_MODEL, D_EXPERT), BF16),
                        pltpu.VMEM((D_EXPERT, D_MODEL), BF16)])
    return pl.pallas_call(
        _expert_kernel2,
        grid_spec=grid_spec,
        out_shape=jax.ShapeDtypeStruct((n_tiles * te, D_MODEL), F32),
        compiler_params=_cparams(1),
        name="experts",
    )(tile_expert, tile_valid, src, src, xs, w_gate, w_up, w_down)


def _combine_kernel2(dst_ref, nxt_ref, h_ref, rec_ref, ys_ref, p_ref, gple_ref, wgate_ref, wproj_ref, out_ref,
                     ybuf, sem):
    i = pl.program_id(0)
    n = pl.num_programs(0)
    slot = i % 2
    tm = h_ref.shape[0]
    slots = ybuf.shape[1]

    def fetch(list_ref, to_slot):
        def body(c, carry):
            _chunk_copy(ys_ref, list_ref[0, 0, c], ybuf.at[to_slot], c, sem.at[to_slot]).start()
            return carry
        lax.fori_loop(0, list_ref[0, 0, LIST_COUNT], body, 0)

    @pl.when(i == 0)
    def _():
        ybuf[...] = jnp.zeros_like(ybuf)
        fetch(dst_ref, 0)

    @pl.when(i + 1 < n)
    def _():
        fetch(nxt_ref, 1 - slot)

    def drain(c, carry):
        _chunk_copy(ys_ref, 0, ybuf.at[slot], 0, sem.at[slot]).wait()
        return carry

    lax.fori_loop(0, dst_ref[0, 0, LIST_COUNT], drain, 0)

    rec = rec_ref[...]
    slot_id = lax.broadcasted_iota(jnp.int32, (tm, slots), 1).astype(F32)
    back = jnp.where(jnp.logical_or(slot_id == rec[:, REC_S1:REC_S1 + 1], slot_id == rec[:, REC_S2:REC_S2 + 1]),
                     1.0, 0.0).astype(BF16)
    ys = ybuf[slot]
    y_hi = ys.astype(BF16)
    y_lo = (ys - y_hi.astype(F32)).astype(BF16)
    h2 = h_ref[...] + (jnp.dot(back, y_hi, preferred_element_type=F32)
                       + jnp.dot(back, y_lo, preferred_element_type=F32))
    gate = jax.nn.sigmoid(jnp.dot(_rms(h2, gple_ref[...]).astype(BF16), wgate_ref[...], preferred_element_type=F32))
    ple = jnp.dot(p_ref[...].astype(BF16), wproj_ref[...], preferred_element_type=F32)
    out_ref[...] = h2 + ple * gate


def _combine2(h, rec, dst, ys, p, g_ple, wgate_bf, wproj_bf, *, tm, layer):
    n = h.shape[0]
    n_tok = n // tm
    tok = lambda w: pl.BlockSpec((tm, w), lambda i: (i, 0))
    p_spec = pl.BlockSpec((None, tm, D_PLE), lambda i: (layer, i, 0))
    lst = lambda shift: pl.BlockSpec((1, 1, LIST_LANES), lambda i: (jnp.minimum(i + shift, n_tok - 1), 0, 0),
                                     memory_space=pltpu.SMEM)
    return pl.pallas_call(
        _combine_kernel2,
        grid=(n_tok,),
        in_specs=[lst(0), lst(1), tok(D_MODEL), tok(LANES), pl.BlockSpec(memory_space=pl.ANY), p_spec,
                  _full((1, D_MODEL)), _full((D_MODEL, D_MODEL)), _full((D_PLE, D_MODEL))],
        out_specs=tok(D_MODEL),
        out_shape=jax.ShapeDtypeStruct((n, D_MODEL), F32),
        scratch_shapes=[pltpu.VMEM((2, _slots(tm), D_MODEL), F32), pltpu.SemaphoreType.DMA((2,))],
        compiler_params=_cparams(1),
        name="combine",
    )(dst, dst, h, rec, ys, p, g_ple, wgate_bf, wproj_bf)


def _excl_cumsum(x, axis):
    return jnp.cumsum(x, axis=axis) - x


def _chunk_plan(cnt, *, tm, te):
    n_tok = cnt.shape[0]
    slots = _slots(tm)
    per_tile = te // SUBLANES
    chunks = (cnt + SUBLANES - 1) // SUBLANES
    run0 = _excl_cumsum(chunks, 1)
    n_run = jnp.sum(chunks, axis=1)
    seg0 = _excl_cumsum(chunks, 0)
    total = jnp.sum(chunks, axis=0)
    region = ((total + per_tile - 1) // per_tile) * per_tile
    reg_end = jnp.cumsum(region)
    reg0 = reg_end - region

    n_tiles = -(-(2 * n_tok * tm + (SUBLANES - 1) * N_EXPERTS * n_tok) // te) + N_EXPERTS
    t0 = jnp.arange(n_tiles, dtype=jnp.int32) * per_tile
    tile_expert = jnp.minimum(jnp.sum((t0[:, None] >= reg_end[None, :]).astype(jnp.int32), axis=1), N_EXPERTS - 1)
    pick = tile_expert[:, None] == jnp.arange(N_EXPERTS, dtype=jnp.int32)[None, :]
    of_tile = lambda v: jnp.sum(jnp.where(pick, v[None, :], 0), axis=1)
    tile_valid = jnp.clip(of_tile(total) - (t0 - of_tile(reg0)), 0, per_tile).astype(jnp.int32)

    q = (t0 - of_tile(reg0))[:, None] + jnp.arange(per_tile, dtype=jnp.int32)[None, :]
    col_of_tile = lambda v: jnp.sum(jnp.where(pick[:, None, :], v[None, :, :], 0), axis=2)
    seg0_t, seg1_t, run0_t = col_of_tile(seg0), col_of_tile(seg0 + chunks), col_of_tile(run0)
    holds = (q[:, :, None] >= seg0_t[:, None, :]) & (q[:, :, None] < seg1_t[:, None, :])
    local = run0_t[:, None, :] + q[:, :, None] - seg0_t[:, None, :]
    row = jnp.arange(n_tok, dtype=jnp.int32)[None, None, :] * slots + SUBLANES * local
    src = jnp.sum(jnp.where(holds, row, 0), axis=2)
    src = jnp.where(jnp.any(holds, axis=2), src, slots - SUBLANES)
    src = jnp.pad(src, ((0, 0), (0, LIST_LANES - per_tile))).reshape(n_tiles, 1, LIST_LANES).astype(jnp.int32)

    j = jnp.arange(LIST_LANES, dtype=jnp.int32)[None, :, None]
    inside = (j >= run0[:, None, :]) & (j < (run0 + chunks)[:, None, :])
    base = (reg0[None, :] + seg0 - run0)[:, None, :]
    dst = SUBLANES * jnp.sum(jnp.where(inside, base + j, 0), axis=2)
    dst = dst.at[:, LIST_COUNT].set(n_run).reshape(n_tok, 1, LIST_LANES).astype(jnp.int32)
    return tile_expert.astype(jnp.int32), tile_valid, src, dst


def _moe_ple2(h, p, g_ffn, w_router, before, sel, w_gate, w_up, w_down, g_ple, wgate_bf, wproj_bf, *,
              tm, te, layer):
    xs, rec, cnt = _sort(h, g_ffn, w_router, before, sel, tm=tm)
    tile_expert, tile_valid, src, dst = _chunk_plan(cnt[:, 0, :N_EXPERTS].astype(jnp.int32), tm=tm, te=te)
    ys = _experts2(xs, tile_expert, tile_valid, src, w_gate, w_up, w_down, te=te, layer=layer)
    return _combine2(h, rec, dst, ys, p, g_ple, wgate_bf, wproj_bf, tm=tm, layer=layer)


def _bucket_np(dist):
    max_exact = N_BUCKETS // 2
    d_f = np.maximum(dist, 1).astype(np.float32)
    large = max_exact + (np.log(d_f / np.float32(max_exact)) / np.float32(np.log(MAX_DISTANCE / max_exact))
                         * np.float32(N_BUCKETS - max_exact)).astype(np.int32)
    large = np.minimum(large, N_BUCKETS - 1)
    return np.where(dist < max_exact, dist, large).astype(np.int32)


def _bias_from_buckets(rel_bias, bucket, valid):
    onehot = (jnp.asarray(bucket)[..., None] == jnp.arange(N_BUCKETS, dtype=jnp.int32)).astype(F32)
    bias = jnp.einsum("...k,kh->h...", onehot, rel_bias.astype(F32), precision=HIGHEST)
    return jnp.where(jnp.asarray(valid)[None], bias, NEG)


def _block_order(dil):
    g = np.arange(Q_BLOCK) // SUBLANES
    j = np.arange(Q_BLOCK) % SUBLANES
    if dil == 1:
        return 16 * j + g
    if dil == 4:
        return 32 * (g // 4) + 4 * j + g % 4
    return SUBLANES * g + j


def _band_bias(rel_bias, dil):
    mu = _block_order(dil)
    qi = mu[:, None] + Q_BLOCK
    ki = np.concatenate([mu, mu + Q_BLOCK])[None, :]
    off = qi - ki
    valid = (off >= 0) & (off <= N_KEYS)
    bucket = _bucket_np(dil * np.clip(off, 0, N_KEYS))
    first = valid & (np.arange(2 * Q_BLOCK)[None, :] >= Q_BLOCK)
    tables = [_bias_from_buckets(rel_bias, bucket, v).reshape(N_HEADS * Q_BLOCK, 2 * Q_BLOCK) for v in (valid, first)]
    return jnp.stack(tables)


def _sample_tables(rel_bias, w_buf):
    qpos = w_buf + np.arange(SAMPLE_T)[:, None]
    pos = np.arange(w_buf + NEW_COLS)[None, :]
    dist = qpos - pos
    in_seq = (dist >= 0) & (pos < w_buf + SAMPLE_T)
    mult = np.zeros(dist.shape, np.float32)
    for (w, d) in PATTERNS:
        mult += in_seq & (dist % d == 0) & (dist <= w)
    bucket = _bucket_np(np.maximum(dist, 0))
    bias = jnp.transpose(_bias_from_buckets(rel_bias, bucket, mult > 0), (1, 0, 2))
    rows = SAMPLE_T * N_HEADS
    mult_rows = np.broadcast_to(mult[:, None, :], (SAMPLE_T, N_HEADS, mult.shape[-1]))
    return bias.reshape(rows, -1), jnp.asarray(mult_rows.reshape(rows, -1))


PROJ_TM = 256
OUT_TM = 512
PROMPT_TM = 256
PROMPT_TE = 256
ATTN_SUB = 4


def _row_perm(tm):
    a = np.arange(tm)
    src = (a // Q_BLOCK) * Q_BLOCK + 16 * (a % SUBLANES) + (a % Q_BLOCK) // SUBLANES
    perm = np.zeros((tm, tm), np.float32)
    perm[a, src] = 1.0
    return perm


def kernel(x_prompt, x_sample, cache_k, cache_v, state_conv, p_prompt, p_sample, rel_bias, g_mix, w_in, q_gain,
           k_gain, conv_w, g_out_att, g_out_conv, w_out, g_ffn, w_router_group, w_router_expert, w_gate, w_up,
           w_down, g_ple, w_ple_gate, w_ple_proj):
    depth = w_in.shape[0]
    batch, seq, _ = x_prompt.shape
    dec_b, dec_t, _ = x_sample.shape
    w_buf = cache_k.shape[2]
    n_s = dec_b * dec_t
    keep = min(w_buf, seq)
    assert dec_t == SAMPLE_T and w_buf % LANES == 0
    assert seq % (Q_BLOCK * 16 * 2) == 0 and keep % PROJ_TM == 0 and seq % PROMPT_TM == 0
    cache_kt = jnp.transpose(cache_k, (0, 1, 3, 4, 2))
    cache_vt = jnp.transpose(cache_v, (0, 1, 3, 4, 2))

    row = lambda a: a.reshape(1, -1)
    member = (np.arange(ATT_DIM)[:, None] // HEAD_DIM == np.arange(LANES)[None, :])
    head_mean = jnp.asarray(member / HEAD_DIM, BF16)
    head_exp = jnp.asarray(member.T, BF16)
    src_lane = np.arange(LANES)
    expand = jnp.asarray((src_lane[:, None] // LSE_LANES_PER_HEAD == np.arange(ATT_DIM)[None, :] // HEAD_DIM)
                         & (src_lane[:, None] % LSE_LANES_PER_HEAD == 0), BF16)
    before = jnp.asarray(np.arange(LANES)[:, None] < np.arange(LANES)[None, :], BF16)
    sel_np = np.zeros((SUBLANES, LANES), np.float32)
    sel_np[0, REC_S1] = sel_np[1, REC_S2] = 1.0
    sel = jnp.asarray(sel_np, BF16)
    perm = jnp.asarray(_row_perm(PROJ_TM), BF16)
    unperm = jnp.asarray(_row_perm(OUT_TM).T, BF16)
    band = [_band_bias(rel_bias, d) for (_, d) in PATTERNS]
    s_bias, s_mult = _sample_tables(rel_bias, w_buf)

    hp = x_prompt.reshape(batch * seq, D_MODEL)
    hs = jnp.swapaxes(x_sample, 0, 1).reshape(n_s, D_MODEL)
    pp_all = p_prompt.reshape(depth, batch * seq, D_PLE)
    ps_all = jnp.swapaxes(p_sample, 1, 2).reshape(depth, n_s, D_PLE)
    new = {k: [] for k in ("kp", "vp", "cp", "ks", "vs", "cs")}
    hist_p = jnp.zeros((batch, SUBLANES, CONV_DIM), F32)

    for l in range(depth):
        w_in_bf = w_in[l].astype(BF16)
        wa_bf = w_out[l, :ATT_DIM].astype(BF16)
        wc_bf = w_out[l, ATT_DIM:].astype(BF16)
        wgate_bf = w_ple_gate[l].astype(BF16)
        wproj_bf = w_ple_proj[l].astype(BF16)
        w_router = jnp.concatenate(
            [w_router_expert[l], w_router_group[l],
             jnp.zeros((D_MODEL, LANES - N_EXPERTS - N_GROUPS), F32)], axis=1).astype(BF16)
        qg, kg = row(jnp.tile(q_gain[l], N_HEADS)), row(jnp.tile(k_gain[l], N_HEADS))
        mix = (row(g_mix[l]), w_in_bf, qg, kg, head_mean, head_exp, conv_w[l], row(g_out_conv[l]))
        moe = (row(g_ffn[l]), w_router, before, sel, w_gate, w_up, w_down, row(g_ple[l]), wgate_bf, wproj_bf)

        q, k, v, k_nat, v_nat, yn, nconv = _inproj(
            hp, hist_p, *mix, perm, tm=PROJ_TM, shift=1, tiles_per_seq=seq // PROJ_TM,
            keep_tiles=keep // PROJ_TM)
        os, lses = [], []
        for bias, (_, d) in zip(band, PATTERNS):
            o, lse = _attn_pattern(q, k, v, bias, batch=batch, seq=seq, dil=d,
                                   sub=min(ATTN_SUB, seq // (Q_BLOCK * d)))
            os.append(o)
            lses.append(lse)
        hp = _outproj(os, lses, yn, hp, row(g_out_att[l]), wa_bf, wc_bf, expand, unperm, tm=OUT_TM)
        hp = _moe_ple2(hp, pp_all, *moe, tm=PROMPT_TM, te=PROMPT_TE, layer=l)
        new["kp"].append(jnp.transpose(k_nat, (0, 3, 1, 2)))
        new["vp"].append(jnp.transpose(v_nat, (0, 3, 1, 2)))
        new["cp"].append(nconv[:, SUBLANES - 2:])

        hist_s = jnp.swapaxes(state_conv[l], 0, 1).reshape(1, 2 * dec_b, CONV_DIM)
        q, k, v, yn, nconv = _inproj(hs, hist_s, *mix, tm=n_s, shift=dec_b, tiles_per_seq=1)
        bmaj = lambda a: jnp.swapaxes(a.reshape(dec_t, dec_b, N_HEADS, HEAD_DIM), 0, 1)
        qb, kb, vb = bmaj(q), bmaj(k), bmaj(v)
        feat = lambda a: jnp.pad(jnp.transpose(a, (0, 2, 3, 1)), ((0, 0), (0, 0), (0, 0), (0, NEW_COLS - dec_t)))
        att = _attn_sample(qb.reshape(dec_b, dec_t, ATT_DIM), feat(kb), feat(vb), cache_kt, cache_vt, l,
                           s_bias, s_mult)
        att_tm = jnp.swapaxes(att, 0, 1).reshape(n_s, ATT_DIM)
        hs = _outproj([att_tm], [], yn, hs, row(g_out_att[l]), wa_bf, wc_bf, tm=n_s)
        hs = _moe_ple2(hs, ps_all, *moe, tm=n_s, te=LANES, layer=l)
        new["ks"].append(kb)
        new["vs"].append(vb)
        new["cs"].append(jnp.swapaxes(nconv.reshape(2, dec_b, CONV_DIM), 0, 1))

    y_prompt = hp.reshape(batch, seq, D_MODEL)
    y_sample = jnp.swapaxes(hs.reshape(dec_t, dec_b, D_MODEL), 0, 1)
    st = lambda key: jnp.stack(new[key])
    return (y_prompt, y_sample, st("kp"), st("vp"), st("cp"), st("ks"), st("vs"), st("cs"))
```

```python
import functools

import jax
import jax.numpy as jnp
import numpy as np
from jax import lax
from jax.experimental import pallas as pl
from jax.experimental.pallas import tpu as pltpu

F32 = jnp.float32
BF16 = jnp.bfloat16
HIGHEST = lax.Precision.HIGHEST

D_MODEL = 1024
HEAD_DIM = 64
N_HEADS = 8
ATT_DIM = N_HEADS * HEAD_DIM
CONV_DIM = D_MODEL - ATT_DIM
MIX_IN = 3 * ATT_DIM + 3 * CONV_DIM
PATTERNS = ((128, 1), (512, 4), (2048, 16))
N_KEYS = 128
Q_BLOCK = 128
N_BUCKETS = 32
MAX_DISTANCE = 2048
N_GROUPS = 4
EXPERTS_PER_GROUP = 8
N_EXPERTS = N_GROUPS * EXPERTS_PER_GROUP
D_EXPERT = 256
D_PLE = 256
EPS = 1e-6
NEG = -1e30

LANES = 128
SUBLANES = 8
SLABS = Q_BLOCK // SUBLANES
LSE_LANES_PER_HEAD = LANES // N_HEADS
VMEM_LIMIT = 56 * 1024 * 1024
NT = (((1,), (1,)), ((), ()))


def _cparams(n_axes):
    return pltpu.CompilerParams(dimension_semantics=("arbitrary",) * n_axes,
                                vmem_limit_bytes=VMEM_LIMIT)


def _full(shape):
    n = len(shape)
    return pl.BlockSpec(shape, lambda *_: (0,) * n)


def _rms(x, gain):
    ms = jnp.mean(x * x, axis=-1, keepdims=True)
    return x * lax.rsqrt(ms + EPS) * gain


def _exact_dot(x, e_ref):
    hi = x.astype(BF16)
    r1 = x - hi.astype(F32)
    mid = r1.astype(BF16)
    lo = (r1 - mid.astype(F32)).astype(BF16)
    e = e_ref[...]
    return (jnp.dot(hi, e, preferred_element_type=F32) + jnp.dot(mid, e, preferred_element_type=F32)
            + jnp.dot(lo, e, preferred_element_type=F32))


def _inproj_kernel(*refs, shift, tiles_per_seq, permute):
    (h_ref, gmix_ref, w_ref, qg_ref, kg_ref, hmean_ref, hexp_ref, cw_ref, gconv_ref, hist_ref) = refs[:10]
    if permute:
        perm_ref, q_ref, k_ref, v_ref, kn_ref, vn_ref, yn_ref, nconv_ref, carry_ref = refs[10:]
    else:
        q_ref, k_ref, v_ref, yn_ref, nconv_ref, carry_ref = refs[10:]
    i = pl.program_id(0)
    a = _rms(h_ref[...], gmix_ref[...])
    proj = jnp.dot(a.astype(BF16), w_ref[...], preferred_element_type=F32)
    tm = proj.shape[0]

    def head_norm(t, g):
        ms = _exact_dot(t * t, hmean_ref)
        return t * _exact_dot(lax.rsqrt(ms + EPS), hexp_ref) * g

    q = head_norm(proj[:, 0:ATT_DIM], qg_ref[...])
    k = head_norm(proj[:, ATT_DIM:2 * ATT_DIM], kg_ref[...])
    v = proj[:, 2 * ATT_DIM:3 * ATT_DIM]
    if permute:
        kn_ref[...] = k.T.reshape(N_HEADS, HEAD_DIM, tm)
        vn_ref[...] = v.T.reshape(N_HEADS, HEAD_DIM, tm)
        qkv = jnp.concatenate([q * (HEAD_DIM ** -0.5), k, v], axis=-1).astype(BF16)
        moved = jnp.dot(perm_ref[...], qkv, preferred_element_type=F32)
        q_ref[...] = moved[:, 0:ATT_DIM]
        k_ref[...] = moved[:, ATT_DIM:2 * ATT_DIM]
        v_ref[...] = moved[:, 2 * ATT_DIM:3 * ATT_DIM]
    else:
        q_ref[...] = q
        k_ref[...] = k
        v_ref[...] = v
    c0 = 3 * ATT_DIM
    hc = proj[:, c0:c0 + CONV_DIM]
    gb = proj[:, c0 + CONV_DIM:c0 + 2 * CONV_DIM]
    gc = proj[:, c0 + 2 * CONV_DIM:c0 + 3 * CONV_DIM]
    u = gc * hc

    if shift == 1:
        @pl.when(i % tiles_per_seq == 0)
        def _():
            carry_ref[...] = hist_ref[0]
        h0 = carry_ref[SUBLANES - 2:SUBLANES - 1, :]
        h1 = carry_ref[SUBLANES - 1:SUBLANES, :]
        row = lax.broadcasted_iota(jnp.int32, (tm, 1), 0)
        u1 = jnp.where(row == 0, h1, pltpu.roll(u, 1, 0))
        u2 = jnp.where(row == 0, h0, jnp.where(row == 1, h1, pltpu.roll(u, 2, 0)))
        carry_ref[...] = u[tm - SUBLANES:tm, :]
        nconv_ref[0] = u[tm - SUBLANES:tm, :]
    else:
        hist = hist_ref[0]
        u1 = jnp.concatenate([hist[shift:2 * shift], u[0:tm - shift]], axis=0)
        u2 = jnp.concatenate([hist, u[0:tm - 2 * shift]], axis=0)
        nconv_ref[0] = u[tm - 2 * shift:tm, :]
    conv = cw_ref[0:1, :] * u2 + cw_ref[1:2, :] * u1 + cw_ref[2:3, :] * u
    yn_ref[...] = _rms(gb * conv, gconv_ref[...])


def _inproj(h, hist, g_mix, w_in_bf, q_gain, k_gain, hmean, hexp, conv_w, g_out_conv, perm=None, *,
            tm, shift, tiles_per_seq, keep_tiles=0):
    n = h.shape[0]
    hist_rows = hist.shape[1]
    nseq = hist.shape[0]
    tok = lambda w: pl.BlockSpec((tm, w), lambda i: (i, 0))
    seq3 = lambda r: pl.BlockSpec((1, r, CONV_DIM), lambda i: (i // tiles_per_seq, 0, 0))
    nconv_rows = SUBLANES if shift == 1 else 2 * shift
    att = jax.ShapeDtypeStruct((n, ATT_DIM), F32)
    in_specs = [tok(D_MODEL), _full((1, D_MODEL)), _full((D_MODEL, MIX_IN)), _full((1, ATT_DIM)),
                _full((1, ATT_DIM)), _full((ATT_DIM, LANES)), _full((LANES, ATT_DIM)), _full((3, CONV_DIM)),
                _full((1, CONV_DIM)), seq3(hist_rows)]
    args = [h, g_mix, w_in_bf, q_gain, k_gain, hmean, hexp, conv_w, g_out_conv, hist]
    out_specs = [tok(ATT_DIM)] * 3
    out_shape = [att] * 3
    if perm is not None:
        in_specs.append(_full((tm, tm)))
        args.append(perm)
        first = tiles_per_seq - keep_tiles
        kept = pl.BlockSpec((None, N_HEADS, HEAD_DIM, tm),
                            lambda i: (i // tiles_per_seq, 0, 0, jnp.maximum(i % tiles_per_seq - first, 0)))
        out_specs += [kept, kept]
        out_shape += [jax.ShapeDtypeStruct((nseq, N_HEADS, HEAD_DIM, keep_tiles * tm), F32)] * 2
    out_specs += [tok(CONV_DIM), seq3(nconv_rows)]
    out_shape += [jax.ShapeDtypeStruct((n, CONV_DIM), F32),
                  jax.ShapeDtypeStruct((nseq, nconv_rows, CONV_DIM), F32)]
    return pl.pallas_call(
        functools.partial(_inproj_kernel, shift=shift, tiles_per_seq=tiles_per_seq, permute=perm is not None),
        grid=(n // tm,),
        in_specs=in_specs, out_specs=out_specs, out_shape=out_shape,
        scratch_shapes=[pltpu.VMEM((SUBLANES, CONV_DIM), F32)],
        compiler_params=_cparams(1),
        name="inproj",
    )(*args)


def _attn_block(q, kk, vv, bias):
    lane = lax.broadcasted_iota(jnp.int32, (Q_BLOCK, LANES), 1)
    upper = lane >= HEAD_DIM
    scores = []
    for h in range(N_HEADS):
        j, e = divmod(h, 2)
        qp = q[:, j * LANES:(j + 1) * LANES]
        qm = (jnp.where(upper, qp, 0.0) if e else jnp.where(upper, 0.0, qp)).astype(BF16)
        scores.append(lax.dot_general(qm, kk[:, j * LANES:(j + 1) * LANES], NT, preferred_element_type=F32))
    s = jnp.concatenate(scores, axis=0) + bias
    m = jnp.max(s, axis=-1, keepdims=True)
    p = jnp.exp(s - m)
    den = jnp.sum(p, axis=-1, keepdims=True)
    pb = p.astype(BF16)
    inv = 1.0 / den
    lse = m + jnp.log(den)
    lse_grp = lane // LSE_LANES_PER_HEAD
    lse_tile = jnp.zeros((Q_BLOCK, LANES), F32)
    outs = []
    for j in range(N_HEADS // 2):
        pair = None
        for e in range(2):
            h = 2 * j + e
            rows = slice(h * Q_BLOCK, (h + 1) * Q_BLOCK)
            o = jnp.dot(pb[rows], vv[:, j * LANES:(j + 1) * LANES], preferred_element_type=F32) * inv[rows]
            pair = o if e == 0 else jnp.where(upper, o, pair)
            lse_tile = jnp.where(lse_grp == h, lse[rows], lse_tile)
        outs.append(pair)
    return jnp.concatenate(outs, axis=-1), lse_tile


def _attn_kernel(q_ref, kp_ref, kc_ref, vp_ref, vc_ref, bias_ref, o_ref, lse_ref, kbuf, vbuf, *, sub, res):
    n = pl.program_id(2)
    rows = sub * Q_BLOCK
    for r in range(res):
        kbuf[0:Q_BLOCK, :] = kp_ref[:, r].reshape(Q_BLOCK, ATT_DIM).astype(BF16)
        kbuf[Q_BLOCK:, :] = kc_ref[:, :, r].reshape(rows, ATT_DIM).astype(BF16)
        vbuf[0:Q_BLOCK, :] = vp_ref[:, r].reshape(Q_BLOCK, ATT_DIM).astype(BF16)
        vbuf[Q_BLOCK:, :] = vc_ref[:, :, r].reshape(rows, ATT_DIM).astype(BF16)
        for j in range(sub):
            q = q_ref[j, :, r].reshape(Q_BLOCK, ATT_DIM)
            r0 = j * Q_BLOCK
            first = (n == 0).astype(jnp.int32) if j == 0 else 0
            o, lse = _attn_block(q, kbuf[r0:r0 + 2 * Q_BLOCK, :], vbuf[r0:r0 + 2 * Q_BLOCK, :], bias_ref[first])
            o_ref[j, :, r] = o.reshape(SLABS, SUBLANES, ATT_DIM)
            lse_ref[j, :, r] = lse.reshape(SLABS, SUBLANES, LANES)


def _attn_pattern(q, k, v, bias, *, batch, seq, dil, sub, res):
    nblk = seq // (Q_BLOCK * dil)
    view = lambda t: t.reshape(batch, nblk, SLABS, dil, SUBLANES, t.shape[-1])
    cur = lambda c: pl.BlockSpec((None, sub, SLABS, res, SUBLANES, c), lambda b, r, n: (b, n, 0, r, 0, 0))
    prev = pl.BlockSpec((None, None, SLABS, res, SUBLANES, ATT_DIM),
                        lambda b, r, n: (b, jnp.maximum(n * sub - 1, 0), 0, r, 0, 0))
    o, lse = pl.pallas_call(
        functools.partial(_attn_kernel, sub=sub, res=res),
        grid=(batch, dil // res, nblk // sub),
        in_specs=[cur(ATT_DIM), prev, cur(ATT_DIM), prev, cur(ATT_DIM),
                  _full((2, N_HEADS * Q_BLOCK, 2 * Q_BLOCK))],
        out_specs=[cur(ATT_DIM), cur(LANES)],
        out_shape=[jax.ShapeDtypeStruct((batch, nblk, SLABS, dil, SUBLANES, ATT_DIM), F32),
                   jax.ShapeDtypeStruct((batch, nblk, SLABS, dil, SUBLANES, LANES), F32)],
        scratch_shapes=[pltpu.VMEM(((sub + 1) * Q_BLOCK, ATT_DIM), BF16),
                        pltpu.VMEM(((sub + 1) * Q_BLOCK, ATT_DIM), BF16)],
        compiler_params=_cparams(3),
        name=f"attn_d{dil}",
    )(view(q), view(k), view(k), view(v), view(v), bias)
    return o.reshape(batch * seq, ATT_DIM), lse.reshape(batch * seq, LANES)


SAMPLE_T = 4
NEW_COLS = LANES


def _attn_sample_kernel(q_ref, kt_ref, ktn_ref, vt_ref, vtn_ref, bias_ref, mult_ref, o_ref):
    rows = SAMPLE_T * N_HEADS
    q4 = q_ref[...] * (HEAD_DIM ** -0.5)
    qt = jnp.concatenate([jnp.broadcast_to(q4[t:t + 1, :], (N_HEADS, ATT_DIM)) for t in range(SAMPLE_T)], axis=0)
    lane_head = lax.broadcasted_iota(jnp.int32, (rows, ATT_DIM), 1) // HEAD_DIM
    row_head = lax.broadcasted_iota(jnp.int32, (rows, ATT_DIM), 0) % N_HEADS
    own = lane_head == row_head
    qbd = jnp.where(own, qt, 0.0).astype(BF16)
    flat = lambda ref: ref[...].reshape(ATT_DIM, ref.shape[-1]).astype(BF16)
    s = jnp.concatenate([jnp.dot(qbd, flat(kt_ref), preferred_element_type=F32),
                         jnp.dot(qbd, flat(ktn_ref), preferred_element_type=F32)], axis=-1) + bias_ref[...]
    m = jnp.max(s, axis=-1, keepdims=True)
    p = jnp.exp(s - m) * mult_ref[...]
    den = jnp.sum(p, axis=-1, keepdims=True)
    pb = p.astype(BF16)
    w_buf = kt_ref.shape[-1]
    acc = (lax.dot_general(pb[:, :w_buf], flat(vt_ref), NT, preferred_element_type=F32)
           + lax.dot_general(pb[:, w_buf:], flat(vtn_ref), NT, preferred_element_type=F32))
    acc = jnp.where(own, acc / den, 0.0)
    for t in range(SAMPLE_T):
        o_ref[t:t + 1, :] = jnp.sum(acc[t * N_HEADS:(t + 1) * N_HEADS, :], axis=0, keepdims=True)


def _attn_sample(q, kt_new, vt_new, cache_kt, cache_vt, layer, bias, mult):
    nb = q.shape[0]
    w_buf = cache_kt.shape[-1]
    tok = pl.BlockSpec((None, SAMPLE_T, ATT_DIM), lambda b: (b, 0, 0))
    new = pl.BlockSpec((None, N_HEADS, HEAD_DIM, NEW_COLS), lambda b: (b, 0, 0, 0))
    old = pl.BlockSpec((None, None, N_HEADS, HEAD_DIM, w_buf), lambda b: (layer, b, 0, 0, 0))
    tbl = _full((SAMPLE_T * N_HEADS, w_buf + NEW_COLS))
    return pl.pallas_call(
        _attn_sample_kernel,
        grid=(nb,),
        in_specs=[tok, old, new, old, new, tbl, tbl],
        out_specs=tok,
        out_shape=jax.ShapeDtypeStruct((nb, SAMPLE_T, ATT_DIM), F32),
        compiler_params=_cparams(1),
        name="attn_sample",
    )(q, cache_kt, kt_new, cache_vt, vt_new, bias, mult)


def _split_dot(x, e_ref):
    hi = x.astype(BF16)
    lo = (x - hi.astype(F32)).astype(BF16)
    return (jnp.dot(hi, e_ref[...], preferred_element_type=F32)
            + jnp.dot(lo, e_ref[...], preferred_element_type=F32))


def _outproj_kernel(*refs, n_pat):
    mix = n_pat > 1
    n_lse = n_pat if mix else 0
    o_refs = refs[0:n_pat]
    l_refs = refs[n_pat:n_pat + n_lse]
    rest = refs[n_pat + n_lse:]
    if mix:
        yn_ref, h_ref, gatt_ref, exp_ref, unperm_ref, wa_ref, wc_ref, out_ref = rest
        lses = [r[...] for r in l_refs]
        top = functools.reduce(jnp.maximum, lses)
        ws = [jnp.exp(l - top) for l in lses]
        tot = functools.reduce(lambda a, b: a + b, ws)
        att = None
        for w, o_ref in zip(ws, o_refs):
            term = _split_dot(w / tot, exp_ref) * o_ref[...]
            att = term if att is None else att + term
        att_bf = jnp.dot(unperm_ref[...], _rms(att, gatt_ref[...]).astype(BF16),
                         preferred_element_type=F32).astype(BF16)
    else:
        yn_ref, h_ref, gatt_ref, wa_ref, wc_ref, out_ref = rest
        att_bf = _rms(o_refs[0][...], gatt_ref[...]).astype(BF16)
    y = (jnp.dot(att_bf, wa_ref[...], preferred_element_type=F32)
         + jnp.dot(yn_ref[...].astype(BF16), wc_ref[...], preferred_element_type=F32))
    out_ref[...] = h_ref[...] + y


def _outproj(os, lses, yn, h, g_att, wa_bf, wc_bf, expand=None, unperm=None, *, tm):
    n = h.shape[0]
    n_pat = len(os)
    tok = lambda w: pl.BlockSpec((tm, w), lambda i: (i, 0))
    in_specs = [tok(ATT_DIM)] * n_pat + [tok(LANES)] * len(lses) + [tok(CONV_DIM), tok(D_MODEL), _full((1, ATT_DIM))]
    args = [*os, *lses, yn, h, g_att]
    if n_pat > 1:
        in_specs += [_full((LANES, ATT_DIM)), _full((tm, tm))]
        args += [expand, unperm]
    in_specs += [_full((ATT_DIM, D_MODEL)), _full((CONV_DIM, D_MODEL))]
    args += [wa_bf, wc_bf]
    return pl.pallas_call(
        functools.partial(_outproj_kernel, n_pat=n_pat),
        grid=(n // tm,),
        in_specs=in_specs,
        out_specs=tok(D_MODEL),
        out_shape=jax.ShapeDtypeStruct((n, D_MODEL), F32),
        compiler_params=_cparams(1),
        name="outproj",
    )(*args)


ROUTE_I1, ROUTE_I2, ROUTE_R1, ROUTE_R2, ROUTE_W1, ROUTE_W2 = range(6)
GROUP_LANE0 = N_EXPERTS


def _route_kernel(h_ref, g_ref, wr_ref, route_ref, cnt_ref, carry_ref):
    i = pl.program_id(0)

    @pl.when(i == 0)
    def _():
        carry_ref[...] = jnp.zeros_like(carry_ref)

    m = _rms(h_ref[...], g_ref[...])
    logits = jnp.dot(m.astype(BF16), wr_ref[...], preferred_element_type=F32)
    tm = logits.shape[0]
    lane_i = lax.broadcasted_iota(jnp.int32, (tm, LANES), 1)
    lane = lane_i.astype(F32)
    big = jnp.float32(4 * LANES)

    is_g = jnp.logical_and(lane_i >= GROUP_LANE0, lane_i < GROUP_LANE0 + N_GROUPS)
    gl = jnp.where(is_g, logits, NEG)
    gmax = jnp.max(gl, axis=-1, keepdims=True)
    g_w = 1.0 / jnp.sum(jnp.where(is_g, jnp.exp(gl - gmax), 0.0), axis=-1, keepdims=True)
    g_sel = jnp.min(jnp.where(gl == gmax, lane - GROUP_LANE0, big), axis=-1, keepdims=True)

    grp_of_lane = (lane_i // EXPERTS_PER_GROUP).astype(F32)
    in_grp = jnp.logical_and(lane_i < N_EXPERTS, grp_of_lane == g_sel)
    el = jnp.where(in_grp, logits, NEG)
    t1 = jnp.max(el, axis=-1, keepdims=True)
    i1 = jnp.min(jnp.where(el == t1, lane, big), axis=-1, keepdims=True)
    el2 = jnp.where(lane == i1, NEG, el)
    t2 = jnp.max(el2, axis=-1, keepdims=True)
    i2 = jnp.min(jnp.where(el2 == t2, lane, big), axis=-1, keepdims=True)
    e2 = jnp.exp(t2 - t1)
    w1 = g_w / (1.0 + e2)
    w2 = g_w * e2 / (1.0 + e2)

    hit1 = lane == i1
    hit2 = lane == i2
    c = jnp.where(jnp.logical_or(hit1, hit2), 1.0, 0.0)
    rr = lax.broadcasted_iota(jnp.int32, (tm, tm), 0)
    cc = lax.broadcasted_iota(jnp.int32, (tm, tm), 1)
    lower = jnp.where(rr > cc, 1.0, 0.0).astype(BF16)
    before = jnp.dot(lower, c.astype(BF16), preferred_element_type=F32) + carry_ref[0:1, :]
    r1 = jnp.sum(jnp.where(hit1, before, 0.0), axis=-1, keepdims=True)
    r2 = jnp.sum(jnp.where(hit2, before, 0.0), axis=-1, keepdims=True)
    total = carry_ref[0:1, :] + jnp.sum(c, axis=0, keepdims=True)
    carry_ref[...] = jnp.broadcast_to(total, carry_ref.shape)
    cnt_ref[...] = jnp.broadcast_to(total, cnt_ref.shape)

    rec = jnp.zeros((tm, LANES), F32)
    for idx, val in ((ROUTE_I1, i1), (ROUTE_I2, i2), (ROUTE_R1, r1), (ROUTE_R2, r2), (ROUTE_W1, w1), (ROUTE_W2, w2)):
        rec = jnp.where(lane_i == idx, val, rec)
    route_ref[...] = rec


def _route(h, g_ffn, w_router, *, tm):
    n = h.shape[0]
    tok = lambda w: pl.BlockSpec((tm, w), lambda i: (i, 0))
    return pl.pallas_call(
        _route_kernel,
        grid=(n // tm,),
        in_specs=[tok(D_MODEL), _full((1, D_MODEL)), _full((D_MODEL, LANES))],
        out_specs=[tok(LANES), _full((SUBLANES, LANES))],
        out_shape=[jax.ShapeDtypeStruct((n, LANES), F32), jax.ShapeDtypeStruct((SUBLANES, LANES), F32)],
        scratch_shapes=[pltpu.VMEM((SUBLANES, LANES), F32)],
        compiler_params=_cparams(1),
        name="route",
    )(h, g_ffn, w_router)


def _dispatch_kernel(pos_ref, h_ref, g_ref, xs_ref, mbuf, sem):
    tm = mbuf.shape[0]
    mbuf[...] = _rms(h_ref[...], g_ref[...])

    def row_copy(t, p):
        return pltpu.make_async_copy(mbuf.at[pl.ds(t, 1), :], xs_ref.at[pl.ds(p, 1), :], sem)

    def issue(t, carry):
        row_copy(t, pos_ref[0, 0, 2 * t]).start()
        row_copy(t, pos_ref[0, 0, 2 * t + 1]).start()
        return carry

    lax.fori_loop(0, tm, issue, 0)

    def drain(t, carry):
        row_copy(0, 0).wait()
        row_copy(0, 0).wait()
        return carry

    lax.fori_loop(0, tm, drain, 0)


def _dispatch(h, g_ffn, pos, n_rows, *, tm):
    n = h.shape[0]
    return pl.pallas_call(
        _dispatch_kernel,
        grid=(n // tm,),
        in_specs=[pl.BlockSpec((1, 1, 2 * tm), lambda i: (i, 0, 0), memory_space=pltpu.SMEM),
                  pl.BlockSpec((tm, D_MODEL), lambda i: (i, 0)), _full((1, D_MODEL))],
        out_specs=pl.BlockSpec(memory_space=pl.ANY),
        out_shape=jax.ShapeDtypeStruct((n_rows, D_MODEL), F32),
        scratch_shapes=[pltpu.VMEM((tm, D_MODEL), F32), pltpu.SemaphoreType.DMA(())],
        compiler_params=pltpu.CompilerParams(dimension_semantics=("arbitrary",), vmem_limit_bytes=VMEM_LIMIT,
                                             has_side_effects=True),
        name="dispatch",
    )(pos, h, g_ffn)


def _expert_kernel(te_ref, tv_ref, x_ref, wg_ref, wu_ref, wd_ref, y_ref, wg_bf, wu_bf, wd_bf):
    i = pl.program_id(0)
    valid = tv_ref[i]
    changed = jnp.logical_or(i == 0, te_ref[i] != te_ref[jnp.maximum(i - 1, 0)])

    @pl.when(jnp.logical_and(changed, valid > 0))
    def _():
        wg_bf[...] = wg_ref[0].astype(BF16)
        wu_bf[...] = wu_ref[0].astype(BF16)
        wd_bf[...] = wd_ref[0].astype(BF16)

    @pl.when(valid > 0)
    def _():
        te = x_ref.shape[0]
        row = lax.broadcasted_iota(jnp.int32, (te, 1), 0)
        x = jnp.where(row < valid, x_ref[...], 0.0).astype(BF16)
        hg = jnp.dot(x, wg_bf[...], preferred_element_type=F32)
        hu = jnp.dot(x, wu_bf[...], preferred_element_type=F32)
        hid = (hg * jax.nn.sigmoid(hg)) * hu
        y_ref[...] = jnp.dot(hid.astype(BF16), wd_bf[...], preferred_element_type=F32)

    @pl.when(valid <= 0)
    def _():
        y_ref[...] = jnp.zeros_like(y_ref)


def _experts(xs, tile_expert, tile_valid, w_gate, w_up, w_down, *, te):
    n_rows = xs.shape[0]
    wspec = lambda a, b: pl.BlockSpec((1, a, b), lambda i, e, v: (e[i], 0, 0))
    grid_spec = pltpu.PrefetchScalarGridSpec(
        num_scalar_prefetch=2,
        grid=(n_rows // te,),
        in_specs=[pl.BlockSpec((te, D_MODEL), lambda i, e, v: (i, 0)),
                  wspec(D_MODEL, D_EXPERT), wspec(D_MODEL, D_EXPERT), wspec(D_EXPERT, D_MODEL)],
        out_specs=pl.BlockSpec((te, D_MODEL), lambda i, e, v: (i, 0)),
        scratch_shapes=[pltpu.VMEM((D_MODEL, D_EXPERT), BF16), pltpu.VMEM((D_MODEL, D_EXPERT), BF16),
                        pltpu.VMEM((D_EXPERT, D_MODEL), BF16)])
    return pl.pallas_call(
        _expert_kernel,
        grid_spec=grid_spec,
        out_shape=jax.ShapeDtypeStruct((n_rows, D_MODEL), F32),
        compiler_params=_cparams(1),
        name="experts",
    )(tile_expert, tile_valid, xs, w_gate, w_up, w_down)


def _combine_kernel(pos_ref, h_ref, route_ref, ys_ref, p_ref, gple_ref, wgate_ref, wproj_ref, out_ref,
                    y0, y1, sem):
    tm = y0.shape[0]

    def row_copy(p, dst, t):
        return pltpu.make_async_copy(ys_ref.at[pl.ds(p, 1), :], dst.at[pl.ds(t, 1), :], sem)

    def issue(t, carry):
        row_copy(pos_ref[0, 0, 2 * t], y0, t).start()
        row_copy(pos_ref[0, 0, 2 * t + 1], y1, t).start()
        return carry

    lax.fori_loop(0, tm, issue, 0)

    def drain(t, carry):
        row_copy(0, y0, 0).wait()
        row_copy(0, y1, 0).wait()
        return carry

    lax.fori_loop(0, tm, drain, 0)

    rec = route_ref[...]
    w1 = rec[:, ROUTE_W1:ROUTE_W1 + 1]
    w2 = rec[:, ROUTE_W2:ROUTE_W2 + 1]
    h2 = h_ref[...] + w1 * y0[...] + w2 * y1[...]
    gate = jax.nn.sigmoid(jnp.dot(_rms(h2, gple_ref[...]).astype(BF16), wgate_ref[...], preferred_element_type=F32))
    ple = jnp.dot(p_ref[...].astype(BF16), wproj_ref[...], preferred_element_type=F32)
    out_ref[...] = h2 + ple * gate


def _combine(h, route, pos, ys, p, g_ple, wgate_bf, wproj_bf, *, tm):
    n = h.shape[0]
    tok = lambda w: pl.BlockSpec((tm, w), lambda i: (i, 0))
    return pl.pallas_call(
        _combine_kernel,
        grid=(n // tm,),
        in_specs=[pl.BlockSpec((1, 1, 2 * tm), lambda i: (i, 0, 0), memory_space=pltpu.SMEM),
                  tok(D_MODEL), tok(LANES), pl.BlockSpec(memory_space=pl.ANY), tok(D_PLE),
                  _full((1, D_MODEL)), _full((D_MODEL, D_MODEL)), _full((D_PLE, D_MODEL))],
        out_specs=tok(D_MODEL),
        out_shape=jax.ShapeDtypeStruct((n, D_MODEL), F32),
        scratch_shapes=[pltpu.VMEM((tm, D_MODEL), F32), pltpu.VMEM((tm, D_MODEL), F32),
                        pltpu.SemaphoreType.DMA(())],
        compiler_params=_cparams(1),
        name="combine",
    )(pos, h, route, ys, p, g_ple, wgate_bf, wproj_bf)


def _lookup(table, idx, size):
    hit = idx[..., None] == jnp.arange(size, dtype=jnp.int32)
    return jnp.sum(jnp.where(hit, table, 0), axis=-1)


def _moe_ple(h, p, g_ffn, w_router, w_gate, w_up, w_down, g_ple, wgate_bf, wproj_bf, *, tm, te):
    n = h.shape[0]
    route, counts = _route(h, g_ffn, w_router, tm=tm)
    counts = counts[0, :N_EXPERTS].astype(jnp.int32)
    padded = ((counts + te - 1) // te) * te
    ends = jnp.cumsum(padded)
    offs = ends - padded
    ids = route[:, ROUTE_I1:ROUTE_I2 + 1].astype(jnp.int32)
    ranks = route[:, ROUTE_R1:ROUTE_R2 + 1].astype(jnp.int32)
    pos = (_lookup(offs, ids, N_EXPERTS) + ranks).reshape(n // tm, 1, 2 * tm)
    n_tiles = (2 * n) // te + N_EXPERTS
    starts = jnp.arange(n_tiles, dtype=jnp.int32) * te
    tile_expert = jnp.minimum(jnp.sum((starts[:, None] >= ends[None, :]).astype(jnp.int32), axis=-1), N_EXPERTS - 1)
    tile_valid = jnp.clip(_lookup(counts, tile_expert, N_EXPERTS)
                          - (starts - _lookup(offs, tile_expert, N_EXPERTS)), 0, te).astype(jnp.int32)
    xs = _dispatch(h, g_ffn, pos, n_tiles * te, tm=tm)
    ys = _experts(xs, tile_expert, tile_valid, w_gate, w_up, w_down, te=te)
    return _combine(h, route, pos, ys, p, g_ple, wgate_bf, wproj_bf, tm=tm)


REC_E1, REC_E2, REC_S1, REC_S2, REC_W1, REC_W2 = range(6)
PACKED = D_MODEL // 2
ROW_W = PACKED + LANES
LIST_LANES = LANES
LIST_COUNT = LIST_LANES - 1


def _slots(tm):
    need = 2 * tm + (SUBLANES - 1) * N_EXPERTS + SUBLANES
    return -(-need // LANES) * LANES


def _pieces(x):
    hi = x.astype(BF16)
    r1 = x - hi.astype(F32)
    mid = r1.astype(BF16)
    return hi, mid, (r1 - mid.astype(F32)).astype(BF16)


def _sort_kernel(h_ref, g_ref, wr_ref, before_ref, sel_ref, xs_ref, rec_ref, cnt_ref):
    m = _rms(h_ref[...], g_ref[...])
    logits = jnp.dot(m.astype(BF16), wr_ref[...], preferred_element_type=F32)
    tm = logits.shape[0]
    slots = xs_ref.shape[0]
    lane_i = lax.broadcasted_iota(jnp.int32, (tm, LANES), 1)
    lane = lane_i.astype(F32)
    big = jnp.float32(4 * LANES)

    is_g = jnp.logical_and(lane_i >= GROUP_LANE0, lane_i < GROUP_LANE0 + N_GROUPS)
    gl = jnp.where(is_g, logits, NEG)
    gmax = jnp.max(gl, axis=-1, keepdims=True)
    g_w = 1.0 / jnp.sum(jnp.where(is_g, jnp.exp(gl - gmax), 0.0), axis=-1, keepdims=True)
    g_sel = jnp.min(jnp.where(gl == gmax, lane - GROUP_LANE0, big), axis=-1, keepdims=True)

    grp_of_lane = (lane_i // EXPERTS_PER_GROUP).astype(F32)
    in_grp = jnp.logical_and(lane_i < N_EXPERTS, grp_of_lane == g_sel)
    el = jnp.where(in_grp, logits, NEG)
    t1 = jnp.max(el, axis=-1, keepdims=True)
    e1 = jnp.min(jnp.where(el == t1, lane, big), axis=-1, keepdims=True)
    el2 = jnp.where(lane == e1, NEG, el)
    t2 = jnp.max(el2, axis=-1, keepdims=True)
    e2 = jnp.min(jnp.where(el2 == t2, lane, big), axis=-1, keepdims=True)
    ex = jnp.exp(t2 - t1)
    w1 = g_w / (1.0 + ex)
    w2 = g_w * ex / (1.0 + ex)

    hit1 = lane == e1
    hit2 = lane == e2
    c = jnp.where(jnp.logical_or(hit1, hit2), 1.0, 0.0)
    rr = lax.broadcasted_iota(jnp.int32, (tm, tm), 0)
    cc = lax.broadcasted_iota(jnp.int32, (tm, tm), 1)
    lower = jnp.where(rr > cc, 1.0, 0.0).astype(BF16)
    rank = jnp.dot(lower, c.astype(BF16), preferred_element_type=F32)
    cnt = jnp.sum(c, axis=0, keepdims=True)
    chunks = jnp.floor((cnt + (SUBLANES - 1)) * (1.0 / SUBLANES))
    start = SUBLANES * jnp.dot(jnp.broadcast_to(chunks, (SUBLANES, LANES)).astype(BF16), before_ref[...],
                               preferred_element_type=F32)[0:1, :]
    slot_of = rank + start
    s1 = jnp.sum(jnp.where(hit1, slot_of, 0.0), axis=-1, keepdims=True)
    s2 = jnp.sum(jnp.where(hit2, slot_of, 0.0), axis=-1, keepdims=True)

    rec = jnp.zeros((tm, LANES), F32)
    for idx, val in ((REC_E1, e1), (REC_E2, e2), (REC_S1, s1), (REC_S2, s2), (REC_W1, w1), (REC_W2, w2)):
        rec = jnp.where(lane_i == idx, val, rec)
    rec_ref[...] = rec
    cnt_ref[0] = jnp.broadcast_to(cnt, (SUBLANES, LANES))

    rec_parts = _pieces(rec)
    srow = sum(lax.dot_general(sel_ref[...], part, NT, preferred_element_type=F32) for part in rec_parts)
    slot_id = lax.broadcasted_iota(jnp.int32, (slots, tm), 0).astype(F32)
    place = jnp.where(jnp.logical_or(slot_id == srow[0:1, :], slot_id == srow[1:2, :]), 1.0, 0.0).astype(BF16)
    payload = jnp.concatenate([m.astype(BF16), *rec_parts], axis=-1)
    moved = jnp.dot(place, payload, preferred_element_type=F32)
    info = (moved[:, D_MODEL:D_MODEL + LANES] + moved[:, D_MODEL + LANES:D_MODEL + 2 * LANES]
            + moved[:, D_MODEL + 2 * LANES:])
    bits = lambda a: pltpu.bitcast(a, jnp.uint32)
    low = lax.shift_right_logical(bits(moved[:, :PACKED]), jnp.uint32(16))
    high = jnp.bitwise_and(bits(moved[:, PACKED:D_MODEL]), jnp.uint32(0xFFFF0000))
    xs_ref[...] = jnp.concatenate([jnp.bitwise_or(low, high), bits(info)], axis=-1)


def _sort(h, g_ffn, w_router, before, sel, *, tm):
    n = h.shape[0]
    slots = _slots(tm)
    tok = lambda w: pl.BlockSpec((tm, w), lambda i: (i, 0))
    return pl.pallas_call(
        _sort_kernel,
        grid=(n // tm,),
        in_specs=[tok(D_MODEL), _full((1, D_MODEL)), _full((D_MODEL, LANES)), _full((LANES, LANES)),
                  _full((SUBLANES, LANES))],
        out_specs=[pl.BlockSpec((slots, ROW_W), lambda i: (i, 0)), tok(LANES),
                   pl.BlockSpec((1, SUBLANES, LANES), lambda i: (i, 0, 0))],
        out_shape=[jax.ShapeDtypeStruct((n // tm * slots, ROW_W), jnp.uint32), jax.ShapeDtypeStruct((n, LANES), F32),
                   jax.ShapeDtypeStruct((n // tm, SUBLANES, LANES), F32)],
        compiler_params=_cparams(1),
        name="moe_sort",
    )(h, g_ffn, w_router, before, sel)


def _chunk_copy(src_hbm, row, dst, c, sem):
    if not isinstance(row, int):
        row = pl.multiple_of(row, SUBLANES)
    return pltpu.make_async_copy(src_hbm.at[pl.ds(row, SUBLANES), :],
                                 dst.at[pl.ds(c * SUBLANES, SUBLANES), :], sem)


def _expert_kernel2(te_ref, tv_ref, src_ref, nxt_ref, xs_ref, wg_ref, wu_ref, wd_ref, y_ref,
                    xbuf, sem, wg_bf, wu_bf, wd_bf):
    i = pl.program_id(0)
    n = pl.num_programs(0)
    te = xbuf.shape[1]
    slot = i % 2

    def fetch(list_ref, to_slot):
        for c in range(te // SUBLANES):
            _chunk_copy(xs_ref, list_ref[0, 0, c], xbuf.at[to_slot], c, sem.at[to_slot]).start()

    @pl.when(jnp.logical_and(i == 0, tv_ref[0] > 0))
    def _():
        fetch(src_ref, 0)

    nxt = jnp.minimum(i + 1, n - 1)

    @pl.when(jnp.logical_and(i + 1 < n, tv_ref[nxt] > 0))
    def _():
        fetch(nxt_ref, 1 - slot)

    valid = tv_ref[i]
    changed = jnp.logical_or(i == 0, te_ref[i] != te_ref[jnp.maximum(i - 1, 0)])

    @pl.when(jnp.logical_and(changed, valid > 0))
    def _():
        wg_bf[...] = wg_ref[0].astype(BF16)
        wu_bf[...] = wu_ref[0].astype(BF16)
        wd_bf[...] = wd_ref[0].astype(BF16)

    @pl.when(valid > 0)
    def _():
        pltpu.make_async_copy(xbuf.at[slot], xbuf.at[slot], sem.at[slot]).wait()
        rows = xbuf[slot]
        words = rows[:, :PACKED]
        as_f32 = lambda a: pltpu.bitcast(a, F32)
        x = jnp.concatenate([as_f32(lax.shift_left(words, jnp.uint32(16))),
                             as_f32(jnp.bitwise_and(words, jnp.uint32(0xFFFF0000)))], axis=-1).astype(BF16)
        info = as_f32(rows[:, PACKED:])
        mine = info[:, REC_E1:REC_E1 + 1] == te_ref[i].astype(F32)
        gate = jnp.where(mine, info[:, REC_W1:REC_W1 + 1], info[:, REC_W2:REC_W2 + 1])
        hg = jnp.dot(x, wg_bf[...], preferred_element_type=F32)
        hu = jnp.dot(x, wu_bf[...], preferred_element_type=F32)
        hid = (hg * jax.nn.sigmoid(hg)) * hu * gate
        y_ref[...] = jnp.dot(hid.astype(BF16), wd_bf[...], preferred_element_type=F32)

    @pl.when(valid <= 0)
    def _():
        y_ref[...] = jnp.zeros_like(y_ref)


def _experts2(xs, tile_expert, tile_valid, src, w_gate, w_up, w_down, *, te, layer):
    n_tiles = tile_expert.shape[0]
    wspec = lambda a, b: pl.BlockSpec((None, 1, a, b), lambda i, e, v: (layer, e[i], 0, 0))
    lst = lambda shift: pl.BlockSpec((1, 1, LIST_LANES),
                                     lambda i, e, v: (jnp.minimum(i + shift, n_tiles - 1), 0, 0),
                                     memory_space=pltpu.SMEM)
    grid_spec = pltpu.PrefetchScalarGridSpec(
        num_scalar_prefetch=2,
        grid=(n_tiles,),
        in_specs=[lst(0), lst(1), pl.BlockSpec(memory_space=pl.ANY),
                  wspec(D_MODEL, D_EXPERT), wspec(D_MODEL, D_EXPERT), wspec(D_EXPERT, D_MODEL)],
        out_specs=pl.BlockSpec((te, D_MODEL), lambda i, e, v: (i, 0)),
        scratch_shapes=[pltpu.VMEM((2, te, ROW_W), jnp.uint32), pltpu.SemaphoreType.DMA((2,)),
                        pltpu.VMEM((D_MODEL, D_EXPERT), BF16), pltpu.VMEM((D_MODEL, D_EXPERT), BF16),
                        pltpu.VMEM((D_EXPERT, D_MODEL), BF16)])
    return pl.pallas_call(
        _expert_kernel2,
        grid_spec=grid_spec,
        out_shape=jax.ShapeDtypeStruct((n_tiles * te, D_MODEL), F32),
        compiler_params=_cparams(1),
        name="experts",
    )(tile_expert, tile_valid, src, src, xs, w_gate, w_up, w_down)


def _combine_kernel2(dst_ref, nxt_ref, h_ref, rec_ref, ys_ref, p_ref, gple_ref, wgate_ref, wproj_ref, out_ref,
                     ybuf, sem):
    i = pl.program_id(0)
    n = pl.num_programs(0)
    slot = i % 2
    tm = h_ref.shape[0]
    slots = ybuf.shape[1]

    def fetch(list_ref, to_slot):
        def body(c, carry):
            _chunk_copy(ys_ref, list_ref[0, 0, c], ybuf.at[to_slot], c, sem.at[to_slot]).start()
            return carry
        lax.fori_loop(0, list_ref[0, 0, LIST_COUNT], body, 0)

    @pl.when(i == 0)
    def _():
        ybuf[...] = jnp.zeros_like(ybuf)
        fetch(dst_ref, 0)

    @pl.when(i + 1 < n)
    def _():
        fetch(nxt_ref, 1 - slot)

    def drain(c, carry):
        _chunk_copy(ys_ref, 0, ybuf.at[slot], 0, sem.at[slot]).wait()
        return carry

    lax.fori_loop(0, dst_ref[0, 0, LIST_COUNT], drain, 0)

    rec = rec_ref[...]
    slot_id = lax.broadcasted_iota(jnp.int32, (tm, slots), 1).astype(F32)
    back = jnp.where(jnp.logical_or(slot_id == rec[:, REC_S1:REC_S1 + 1], slot_id == rec[:, REC_S2:REC_S2 + 1]),
                     1.0, 0.0).astype(BF16)
    ys = ybuf[slot]
    y_hi = ys.astype(BF16)
    y_lo = (ys - y_hi.astype(F32)).astype(BF16)
    h2 = h_ref[...] + (jnp.dot(back, y_hi, preferred_element_type=F32)
                       + jnp.dot(back, y_lo, preferred_element_type=F32))
    gate = jax.nn.sigmoid(jnp.dot(_rms(h2, gple_ref[...]).astype(BF16), wgate_ref[...], preferred_element_type=F32))
    ple = jnp.dot(p_ref[...].astype(BF16), wproj_ref[...], preferred_element_type=F32)
    out_ref[...] = h2 + ple * gate


def _combine2(h, rec, dst, ys, p, g_ple, wgate_bf, wproj_bf, *, tm, layer):
    n = h.shape[0]
    n_tok = n // tm
    tok = lambda w: pl.BlockSpec((tm, w), lambda i: (i, 0))
    p_spec = pl.BlockSpec((None, tm, D_PLE), lambda i: (layer, i, 0))
    lst = lambda shift: pl.BlockSpec((1, 1, LIST_LANES), lambda i: (jnp.minimum(i + shift, n_tok - 1), 0, 0),
                                     memory_space=pltpu.SMEM)
    return pl.pallas_call(
        _combine_kernel2,
        grid=(n_tok,),
        in_specs=[lst(0), lst(1), tok(D_MODEL), tok(LANES), pl.BlockSpec(memory_space=pl.ANY), p_spec,
                  _full((1, D_MODEL)), _full((D_MODEL, D_MODEL)), _full((D_PLE, D_MODEL))],
        out_specs=tok(D_MODEL),
        out_shape=jax.ShapeDtypeStruct((n, D_MODEL), F32),
        scratch_shapes=[pltpu.VMEM((2, _slots(tm), D_MODEL), F32), pltpu.SemaphoreType.DMA((2,))],
        compiler_params=_cparams(1),
        name="combine",
    )(dst, dst, h, rec, ys, p, g_ple, wgate_bf, wproj_bf)


def _excl_cumsum(x, axis):
    return jnp.cumsum(x, axis=axis) - x


def _chunk_plan(cnt, *, tm, te):
    n_tok = cnt.shape[0]
    slots = _slots(tm)
    per_tile = te // SUBLANES
    chunks = (cnt + SUBLANES - 1) // SUBLANES
    run0 = _excl_cumsum(chunks, 1)
    n_run = jnp.sum(chunks, axis=1)
    seg0 = _excl_cumsum(chunks, 0)
    total = jnp.sum(chunks, axis=0)
    region = ((total + per_tile - 1) // per_tile) * per_tile
    reg_end = jnp.cumsum(region)
    reg0 = reg_end - region

    n_tiles = -(-(2 * n_tok * tm + (SUBLANES - 1) * N_EXPERTS * n_tok) // te) + N_EXPERTS
    t0 = jnp.arange(n_tiles, dtype=jnp.int32) * per_tile
    tile_expert = jnp.minimum(jnp.sum((t0[:, None] >= reg_end[None, :]).astype(jnp.int32), axis=1), N_EXPERTS - 1)
    pick = tile_expert[:, None] == jnp.arange(N_EXPERTS, dtype=jnp.int32)[None, :]
    of_tile = lambda v: jnp.sum(jnp.where(pick, v[None, :], 0), axis=1)
    tile_valid = jnp.clip(of_tile(total) - (t0 - of_tile(reg0)), 0, per_tile).astype(jnp.int32)

    q = (t0 - of_tile(reg0))[:, None] + jnp.arange(per_tile, dtype=jnp.int32)[None, :]
    col_of_tile = lambda v: jnp.sum(jnp.where(pick[:, None, :], v[None, :, :], 0), axis=2)
    seg0_t, seg1_t, run0_t = col_of_tile(seg0), col_of_tile(seg0 + chunks), col_of_tile(run0)
    holds = (q[:, :, None] >= seg0_t[:, None, :]) & (q[:, :, None] < seg1_t[:, None, :])
    local = run0_t[:, None, :] + q[:, :, None] - seg0_t[:, None, :]
    row = jnp.arange(n_tok, dtype=jnp.int32)[None, None, :] * slots + SUBLANES * local
    src = jnp.sum(jnp.where(holds, row, 0), axis=2)
    src = jnp.where(jnp.any(holds, axis=2), src, slots - SUBLANES)
    src = jnp.pad(src, ((0, 0), (0, LIST_LANES - per_tile))).reshape(n_tiles, 1, LIST_LANES).astype(jnp.int32)

    j = jnp.arange(LIST_LANES, dtype=jnp.int32)[None, :, None]
    inside = (j >= run0[:, None, :]) & (j < (run0 + chunks)[:, None, :])
    base = (reg0[None, :] + seg0 - run0)[:, None, :]
    dst = SUBLANES * jnp.sum(jnp.where(inside, base + j, 0), axis=2)
    dst = dst.at[:, LIST_COUNT].set(n_run).reshape(n_tok, 1, LIST_LANES).astype(jnp.int32)
    return tile_expert.astype(jnp.int32), tile_valid, src, dst


def _moe_ple2(h, p, g_ffn, w_router, before, sel, w_gate, w_up, w_down, g_ple, wgate_bf, wproj_bf, *,
              tm, te, layer):
    xs, rec, cnt = _sort(h, g_ffn, w_router, before, sel, tm=tm)
    tile_expert, tile_valid, src, dst = _chunk_plan(cnt[:, 0, :N_EXPERTS].astype(jnp.int32), tm=tm, te=te)
    ys = _experts2(xs, tile_expert, tile_valid, src, w_gate, w_up, w_down, te=te, layer=layer)
    return _combine2(h, rec, dst, ys, p, g_ple, wgate_bf, wproj_bf, tm=tm, layer=layer)


def _bucket_np(dist):
    max_exact = N_BUCKETS // 2
    d_f = np.maximum(dist, 1).astype(np.float32)
    large = max_exact + (np.log(d_f / np.float32(max_exact)) / np.float32(np.log(MAX_DISTANCE / max_exact))
                         * np.float32(N_BUCKETS - max_exact)).astype(np.int32)
    large = np.minimum(large, N_BUCKETS - 1)
    return np.where(dist < max_exact, dist, large).astype(np.int32)


def _bias_from_buckets(rel_bias, bucket, valid):
    onehot = (jnp.asarray(bucket)[..., None] == jnp.arange(N_BUCKETS, dtype=jnp.int32)).astype(F32)
    bias = jnp.einsum("...k,kh->h...", onehot, rel_bias.astype(F32), precision=HIGHEST)
    return jnp.where(jnp.asarray(valid)[None], bias, NEG)


def _block_order(dil):
    g = np.arange(Q_BLOCK) // SUBLANES
    j = np.arange(Q_BLOCK) % SUBLANES
    if dil == 1:
        return 16 * j + g
    if dil == 4:
        return 32 * (g // 4) + 4 * j + g % 4
    return SUBLANES * g + j


def _band_bias(rel_bias, dil):
    mu = _block_order(dil)
    qi = mu[:, None] + Q_BLOCK
    ki = np.concatenate([mu, mu + Q_BLOCK])[None, :]
    off = qi - ki
    valid = (off >= 0) & (off <= N_KEYS)
    bucket = _bucket_np(dil * np.clip(off, 0, N_KEYS))
    first = valid & (np.arange(2 * Q_BLOCK)[None, :] >= Q_BLOCK)
    tables = [_bias_from_buckets(rel_bias, bucket, v).reshape(N_HEADS * Q_BLOCK, 2 * Q_BLOCK) for v in (valid, first)]
    return jnp.stack(tables)


def _sample_tables(rel_bias, w_buf):
    qpos = w_buf + np.arange(SAMPLE_T)[:, None]
    pos = np.arange(w_buf + NEW_COLS)[None, :]
    dist = qpos - pos
    in_seq = (dist >= 0) & (pos < w_buf + SAMPLE_T)
    mult = np.zeros(dist.shape, np.float32)
    for (w, d) in PATTERNS:
        mult += in_seq & (dist % d == 0) & (dist <= w)
    bucket = _bucket_np(np.maximum(dist, 0))
    bias = jnp.transpose(_bias_from_buckets(rel_bias, bucket, mult > 0), (1, 0, 2))
    rows = SAMPLE_T * N_HEADS
    mult_rows = np.broadcast_to(mult[:, None, :], (SAMPLE_T, N_HEADS, mult.shape[-1]))
    return bias.reshape(rows, -1), jnp.asarray(mult_rows.reshape(rows, -1))


PROJ_TM = 256
OUT_TM = 512
PROMPT_TM = 256
PROMPT_TE = 256
ATTN_SUB = 4


def _row_perm(tm):
    a = np.arange(tm)
    src = (a // Q_BLOCK) * Q_BLOCK + 16 * (a % SUBLANES) + (a % Q_BLOCK) // SUBLANES
    perm = np.zeros((tm, tm), np.float32)
    perm[a, src] = 1.0
    return perm


def kernel(x_prompt, x_sample, cache_k, cache_v, state_conv, p_prompt, p_sample, rel_bias, g_mix, w_in, q_gain,
           k_gain, conv_w, g_out_att, g_out_conv, w_out, g_ffn, w_router_group, w_router_expert, w_gate, w_up,
           w_down, g_ple, w_ple_gate, w_ple_proj):
    depth = w_in.shape[0]
    batch, seq, _ = x_prompt.shape
    dec_b, dec_t, _ = x_sample.shape
    w_buf = cache_k.shape[2]
    n_s = dec_b * dec_t
    keep = min(w_buf, seq)
    assert dec_t == SAMPLE_T and w_buf % LANES == 0
    assert seq % (Q_BLOCK * 16 * 2) == 0 and keep % PROJ_TM == 0 and seq % PROMPT_TM == 0
    cache_kt = jnp.transpose(cache_k, (0, 1, 3, 4, 2))
    cache_vt = jnp.transpose(cache_v, (0, 1, 3, 4, 2))

    row = lambda a: a.reshape(1, -1)
    member = (np.arange(ATT_DIM)[:, None] // HEAD_DIM == np.arange(LANES)[None, :])
    head_mean = jnp.asarray(member / HEAD_DIM, BF16)
    head_exp = jnp.asarray(member.T, BF16)
    src_lane = np.arange(LANES)
    expand = jnp.asarray((src_lane[:, None] // LSE_LANES_PER_HEAD == np.arange(ATT_DIM)[None, :] // HEAD_DIM)
                         & (src_lane[:, None] % LSE_LANES_PER_HEAD == 0), BF16)
    before = jnp.asarray(np.arange(LANES)[:, None] < np.arange(LANES)[None, :], BF16)
    sel_np = np.zeros((SUBLANES, LANES), np.float32)
    sel_np[0, REC_S1] = sel_np[1, REC_S2] = 1.0
    sel = jnp.asarray(sel_np, BF16)
    perm = jnp.asarray(_row_perm(PROJ_TM), BF16)
    unperm = jnp.asarray(_row_perm(OUT_TM).T, BF16)
    band = [_band_bias(rel_bias, d) for (_, d) in PATTERNS]
    s_bias, s_mult = _sample_tables(rel_bias, w_buf)

    hp = x_prompt.reshape(batch * seq, D_MODEL)
    hs = jnp.swapaxes(x_sample, 0, 1).reshape(n_s, D_MODEL)
    pp_all = p_prompt.reshape(depth, batch * seq, D_PLE)
    ps_all = jnp.swapaxes(p_sample, 1, 2).reshape(depth, n_s, D_PLE)
    new = {k: [] for k in ("kp", "vp", "cp", "ks", "vs", "cs")}
    hist_p = jnp.zeros((batch, SUBLANES, CONV_DIM), F32)

    for l in range(depth):
        w_in_bf = w_in[l].astype(BF16)
        wa_bf = w_out[l, :ATT_DIM].astype(BF16)
        wc_bf = w_out[l, ATT_DIM:].astype(BF16)
        wgate_bf = w_ple_gate[l].astype(BF16)
        wproj_bf = w_ple_proj[l].astype(BF16)
        w_router = jnp.concatenate(
            [w_router_expert[l], w_router_group[l],
             jnp.zeros((D_MODEL, LANES - N_EXPERTS - N_GROUPS), F32)], axis=1).astype(BF16)
        qg, kg = row(jnp.tile(q_gain[l], N_HEADS)), row(jnp.tile(k_gain[l], N_HEADS))
        mix = (row(g_mix[l]), w_in_bf, qg, kg, head_mean, head_exp, conv_w[l], row(g_out_conv[l]))
        moe = (row(g_ffn[l]), w_router, before, sel, w_gate, w_up, w_down, row(g_ple[l]), wgate_bf, wproj_bf)

        q, k, v, k_nat, v_nat, yn, nconv = _inproj(
            hp, hist_p, *mix, perm, tm=PROJ_TM, shift=1, tiles_per_seq=seq // PROJ_TM,
            keep_tiles=keep // PROJ_TM)
        os, lses = [], []
        for bias, (_, d) in zip(band, PATTERNS):
            sub = min(ATTN_SUB, seq // (Q_BLOCK * d))
            o, lse = _attn_pattern(q, k, v, bias, batch=batch, seq=seq, dil=d, sub=sub,
                                   res=min(d, ATTN_SUB // sub))
            os.append(o)
            lses.append(lse)
        hp = _outproj(os, lses, yn, hp, row(g_out_att[l]), wa_bf, wc_bf, expand, unperm, tm=OUT_TM)
        hp = _moe_ple2(hp, pp_all, *moe, tm=PROMPT_TM, te=PROMPT_TE, layer=l)
        new["kp"].append(jnp.transpose(k_nat, (0, 3, 1, 2)))
        new["vp"].append(jnp.transpose(v_nat, (0, 3, 1, 2)))
        new["cp"].append(nconv[:, SUBLANES - 2:])

        hist_s = jnp.swapaxes(state_conv[l], 0, 1).reshape(1, 2 * dec_b, CONV_DIM)
        q, k, v, yn, nconv = _inproj(hs, hist_s, *mix, tm=n_s, shift=dec_b, tiles_per_seq=1)
        bmaj = lambda a: jnp.swapaxes(a.reshape(dec_t, dec_b, N_HEADS, HEAD_DIM), 0, 1)
        qb, kb, vb = bmaj(q), bmaj(k), bmaj(v)
        feat = lambda a: jnp.pad(jnp.transpose(a, (0, 2, 3, 1)), ((0, 0), (0, 0), (0, 0), (0, NEW_COLS - dec_t)))
        att = _attn_sample(qb.reshape(dec_b, dec_t, ATT_DIM), feat(kb), feat(vb), cache_kt, cache_vt, l,
                           s_bias, s_mult)
        att_tm = jnp.swapaxes(att, 0, 1).reshape(n_s, ATT_DIM)
        hs = _outproj([att_tm], [], yn, hs, row(g_out_att[l]), wa_bf, wc_bf, tm=n_s)
        hs = _moe_ple2(hs, ps_all, *moe, tm=n_s, te=LANES, layer=l)
        new["ks"].append(kb)
        new["vs"].append(vb)
        new["cs"].append(jnp.swapaxes(nconv.reshape(2, dec_b, CONV_DIM), 0, 1))

    y_prompt = hp.reshape(batch, seq, D_MODEL)
    y_sample = jnp.swapaxes(hs.reshape(dec_t, dec_b, D_MODEL), 0, 1)
    st = lambda key: jnp.stack(new[key])
    return (y_prompt, y_sample, st("kp"), st("vp"), st("cp"), st("ks"), st("vs"), st("cs"))
```

```python
import functools

import jax
import jax.numpy as jnp
import numpy as np
from jax import lax
from jax.experimental import pallas as pl
from jax.experimental.pallas import tpu as pltpu

F32 = jnp.float32
BF16 = jnp.bfloat16
HIGHEST = lax.Precision.HIGHEST

D_MODEL = 1024
HEAD_DIM = 64
N_HEADS = 8
ATT_DIM = N_HEADS * HEAD_DIM
CONV_DIM = D_MODEL - ATT_DIM
MIX_IN = 3 * ATT_DIM + 3 * CONV_DIM
PATTERNS = ((128, 1), (512, 4), (2048, 16))
N_KEYS = 128
Q_BLOCK = 128
N_BUCKETS = 32
MAX_DISTANCE = 2048
N_GROUPS = 4
EXPERTS_PER_GROUP = 8
N_EXPERTS = N_GROUPS * EXPERTS_PER_GROUP
D_EXPERT = 256
D_PLE = 256
EPS = 1e-6
NEG = -1e30

LANES = 128
SUBLANES = 8
SLABS = Q_BLOCK // SUBLANES
LSE_LANES_PER_HEAD = LANES // N_HEADS
VMEM_LIMIT = 56 * 1024 * 1024
NT = (((1,), (1,)), ((), ()))


def _cparams(n_axes):
    return pltpu.CompilerParams(dimension_semantics=("arbitrary",) * n_axes,
                                vmem_limit_bytes=VMEM_LIMIT)


def _full(shape):
    n = len(shape)
    return pl.BlockSpec(shape, lambda *_: (0,) * n)


def _rms(x, gain):
    ms = jnp.mean(x * x, axis=-1, keepdims=True)
    return x * lax.rsqrt(ms + EPS) * gain


def _exact_dot(x, e_ref):
    hi = x.astype(BF16)
    r1 = x - hi.astype(F32)
    mid = r1.astype(BF16)
    lo = (r1 - mid.astype(F32)).astype(BF16)
    e = e_ref[...]
    return (jnp.dot(hi, e, preferred_element_type=F32) + jnp.dot(mid, e, preferred_element_type=F32)
            + jnp.dot(lo, e, preferred_element_type=F32))


def _inproj_kernel(*refs, shift, tiles_per_seq, permute):
    (h_ref, gmix_ref, w_ref, qg_ref, kg_ref, hmean_ref, hexp_ref, cw_ref, gconv_ref, hist_ref) = refs[:10]
    if permute:
        perm_ref, q_ref, k_ref, v_ref, kn_ref, vn_ref, yn_ref, nconv_ref, carry_ref = refs[10:]
    else:
        q_ref, k_ref, v_ref, yn_ref, nconv_ref, carry_ref = refs[10:]
    i = pl.program_id(0)
    a = _rms(h_ref[...], gmix_ref[...])
    proj = jnp.dot(a.astype(BF16), w_ref[...], preferred_element_type=F32)
    tm = proj.shape[0]

    def head_norm(t, g):
        ms = _exact_dot(t * t, hmean_ref)
        return t * _exact_dot(lax.rsqrt(ms + EPS), hexp_ref) * g

    q = head_norm(proj[:, 0:ATT_DIM], qg_ref[...])
    k = head_norm(proj[:, ATT_DIM:2 * ATT_DIM], kg_ref[...])
    v = proj[:, 2 * ATT_DIM:3 * ATT_DIM]
    if permute:
        kn_ref[...] = k.T.reshape(N_HEADS, HEAD_DIM, tm)
        vn_ref[...] = v.T.reshape(N_HEADS, HEAD_DIM, tm)
        qkv = jnp.concatenate([q * (HEAD_DIM ** -0.5), k, v], axis=-1).astype(BF16)
        moved = jnp.dot(perm_ref[...], qkv, preferred_element_type=F32)
        q_ref[...] = moved[:, 0:ATT_DIM]
        k_ref[...] = moved[:, ATT_DIM:2 * ATT_DIM]
        v_ref[...] = moved[:, 2 * ATT_DIM:3 * ATT_DIM]
    else:
        q_ref[...] = q
        k_ref[...] = k
        v_ref[...] = v
    c0 = 3 * ATT_DIM
    hc = proj[:, c0:c0 + CONV_DIM]
    gb = proj[:, c0 + CONV_DIM:c0 + 2 * CONV_DIM]
    gc = proj[:, c0 + 2 * CONV_DIM:c0 + 3 * CONV_DIM]
    u = gc * hc

    if shift == 1:
        @pl.when(i % tiles_per_seq == 0)
        def _():
            carry_ref[...] = hist_ref[0]
        h0 = carry_ref[SUBLANES - 2:SUBLANES - 1, :]
        h1 = carry_ref[SUBLANES - 1:SUBLANES, :]
        row = lax.broadcasted_iota(jnp.int32, (tm, 1), 0)
        u1 = jnp.where(row == 0, h1, pltpu.roll(u, 1, 0))
        u2 = jnp.where(row == 0, h0, jnp.where(row == 1, h1, pltpu.roll(u, 2, 0)))
        carry_ref[...] = u[tm - SUBLANES:tm, :]
        nconv_ref[0] = u[tm - SUBLANES:tm, :]
    else:
        hist = hist_ref[0]
        u1 = jnp.concatenate([hist[shift:2 * shift], u[0:tm - shift]], axis=0)
        u2 = jnp.concatenate([hist, u[0:tm - 2 * shift]], axis=0)
        nconv_ref[0] = u[tm - 2 * shift:tm, :]
    conv = cw_ref[0:1, :] * u2 + cw_ref[1:2, :] * u1 + cw_ref[2:3, :] * u
    yn_ref[...] = _rms(gb * conv, gconv_ref[...])


def _inproj(h, hist, g_mix, w_in_bf, q_gain, k_gain, hmean, hexp, conv_w, g_out_conv, perm=None, *,
            tm, shift, tiles_per_seq, keep_tiles=0):
    n = h.shape[0]
    hist_rows = hist.shape[1]
    nseq = hist.shape[0]
    tok = lambda w: pl.BlockSpec((tm, w), lambda i: (i, 0))
    seq3 = lambda r: pl.BlockSpec((1, r, CONV_DIM), lambda i: (i // tiles_per_seq, 0, 0))
    nconv_rows = SUBLANES if shift == 1 else 2 * shift
    att = jax.ShapeDtypeStruct((n, ATT_DIM), F32)
    in_specs = [tok(D_MODEL), _full((1, D_MODEL)), _full((D_MODEL, MIX_IN)), _full((1, ATT_DIM)),
                _full((1, ATT_DIM)), _full((ATT_DIM, LANES)), _full((LANES, ATT_DIM)), _full((3, CONV_DIM)),
                _full((1, CONV_DIM)), seq3(hist_rows)]
    args = [h, g_mix, w_in_bf, q_gain, k_gain, hmean, hexp, conv_w, g_out_conv, hist]
    out_specs = [tok(ATT_DIM)] * 3
    out_shape = [att] * 3
    if perm is not None:
        in_specs.append(_full((tm, tm)))
        args.append(perm)
        first = tiles_per_seq - keep_tiles
        kept = pl.BlockSpec((None, N_HEADS, HEAD_DIM, tm),
                            lambda i: (i // tiles_per_seq, 0, 0, jnp.maximum(i % tiles_per_seq - first, 0)))
        out_specs += [kept, kept]
        out_shape += [jax.ShapeDtypeStruct((nseq, N_HEADS, HEAD_DIM, keep_tiles * tm), F32)] * 2
    out_specs += [tok(CONV_DIM), seq3(nconv_rows)]
    out_shape += [jax.ShapeDtypeStruct((n, CONV_DIM), F32),
                  jax.ShapeDtypeStruct((nseq, nconv_rows, CONV_DIM), F32)]
    return pl.pallas_call(
        functools.partial(_inproj_kernel, shift=shift, tiles_per_seq=tiles_per_seq, permute=perm is not None),
        grid=(n // tm,),
        in_specs=in_specs, out_specs=out_specs, out_shape=out_shape,
        scratch_shapes=[pltpu.VMEM((SUBLANES, CONV_DIM), F32)],
        compiler_params=_cparams(1),
        name="inproj",
    )(*args)


def _attn_block(q, kk, vv, bias):
    lane = lax.broadcasted_iota(jnp.int32, (Q_BLOCK, LANES), 1)
    upper = lane >= HEAD_DIM
    scores = []
    for h in range(N_HEADS):
        j, e = divmod(h, 2)
        qp = q[:, j * LANES:(j + 1) * LANES]
        qm = (jnp.where(upper, qp, 0.0) if e else jnp.where(upper, 0.0, qp)).astype(BF16)
        scores.append(lax.dot_general(qm, kk[:, j * LANES:(j + 1) * LANES], NT, preferred_element_type=F32))
    s = jnp.concatenate(scores, axis=0) + bias
    m = jnp.max(s, axis=-1, keepdims=True)
    p = jnp.exp(s - m)
    den = jnp.sum(p, axis=-1, keepdims=True)
    pb = p.astype(BF16)
    inv = 1.0 / den
    lse = m + jnp.log(den)
    lse_grp = lane // LSE_LANES_PER_HEAD
    lse_tile = jnp.zeros((Q_BLOCK, LANES), F32)
    outs = []
    for j in range(N_HEADS // 2):
        pair = None
        for e in range(2):
            h = 2 * j + e
            rows = slice(h * Q_BLOCK, (h + 1) * Q_BLOCK)
            o = jnp.dot(pb[rows], vv[:, j * LANES:(j + 1) * LANES], preferred_element_type=F32) * inv[rows]
            pair = o if e == 0 else jnp.where(upper, o, pair)
            lse_tile = jnp.where(lse_grp == h, lse[rows], lse_tile)
        outs.append(pair)
    return jnp.concatenate(outs, axis=-1), lse_tile


def _attn_kernel(q_ref, kp_ref, kc_ref, vp_ref, vc_ref, bias_ref, o_ref, lse_ref, kbuf, vbuf, *, sub, res):
    n = pl.program_id(2)
    rows = sub * Q_BLOCK
    for r in range(res):
        kbuf[0:Q_BLOCK, :] = kp_ref[:, r].reshape(Q_BLOCK, ATT_DIM).astype(BF16)
        kbuf[Q_BLOCK:, :] = kc_ref[:, :, r].reshape(rows, ATT_DIM).astype(BF16)
        vbuf[0:Q_BLOCK, :] = vp_ref[:, r].reshape(Q_BLOCK, ATT_DIM).astype(BF16)
        vbuf[Q_BLOCK:, :] = vc_ref[:, :, r].reshape(rows, ATT_DIM).astype(BF16)
        for j in range(sub):
            q = q_ref[j, :, r].reshape(Q_BLOCK, ATT_DIM)
            r0 = j * Q_BLOCK
            first = (n == 0).astype(jnp.int32) if j == 0 else 0
            o, lse = _attn_block(q, kbuf[r0:r0 + 2 * Q_BLOCK, :], vbuf[r0:r0 + 2 * Q_BLOCK, :], bias_ref[first])
            o_ref[j, :, r] = o.reshape(SLABS, SUBLANES, ATT_DIM)
            lse_ref[j, :, r] = lse.reshape(SLABS, SUBLANES, LANES)


def _attn_pattern(q, k, v, bias, *, batch, seq, dil, sub, res):
    nblk = seq // (Q_BLOCK * dil)
    view = lambda t: t.reshape(batch, nblk, SLABS, dil, SUBLANES, t.shape[-1])
    cur = lambda c: pl.BlockSpec((None, sub, SLABS, res, SUBLANES, c), lambda b, r, n: (b, n, 0, r, 0, 0))
    prev = pl.BlockSpec((None, None, SLABS, res, SUBLANES, ATT_DIM),
                        lambda b, r, n: (b, jnp.maximum(n * sub - 1, 0), 0, r, 0, 0))
    o, lse = pl.pallas_call(
        functools.partial(_attn_kernel, sub=sub, res=res),
        grid=(batch, dil // res, nblk // sub),
        in_specs=[cur(ATT_DIM), prev, cur(ATT_DIM), prev, cur(ATT_DIM),
                  _full((2, N_HEADS * Q_BLOCK, 2 * Q_BLOCK))],
        out_specs=[cur(ATT_DIM), cur(LANES)],
        out_shape=[jax.ShapeDtypeStruct((batch, nblk, SLABS, dil, SUBLANES, ATT_DIM), F32),
                   jax.ShapeDtypeStruct((batch, nblk, SLABS, dil, SUBLANES, LANES), F32)],
        scratch_shapes=[pltpu.VMEM(((sub + 1) * Q_BLOCK, ATT_DIM), BF16),
                        pltpu.VMEM(((sub + 1) * Q_BLOCK, ATT_DIM), BF16)],
        compiler_params=_cparams(3),
        name=f"attn_d{dil}",
    )(view(q), view(k), view(k), view(v), view(v), bias)
    return o.reshape(batch * seq, ATT_DIM), lse.reshape(batch * seq, LANES)


SAMPLE_T = 4
NEW_COLS = LANES


def _attn_sample_kernel(q_ref, kt_ref, kn_ref, vt_ref, vn_ref, bias_ref, mult_ref, o_ref):
    rows = SAMPLE_T * N_HEADS
    q4 = q_ref[...] * (HEAD_DIM ** -0.5)
    qt = jnp.concatenate([jnp.broadcast_to(q4[t:t + 1, :], (N_HEADS, ATT_DIM)) for t in range(SAMPLE_T)], axis=0)
    lane_head = lax.broadcasted_iota(jnp.int32, (rows, ATT_DIM), 1) // HEAD_DIM
    row_head = lax.broadcasted_iota(jnp.int32, (rows, ATT_DIM), 0) % N_HEADS
    own = lane_head == row_head
    qbd = jnp.where(own, qt, 0.0).astype(BF16)
    flat = lambda ref: ref[...].reshape(ATT_DIM, ref.shape[-1]).astype(BF16)
    pad = jnp.zeros((NEW_COLS - SUBLANES, ATT_DIM), F32)
    new_rows = lambda ref: jnp.concatenate([ref[...], pad], axis=0).astype(BF16)
    s = jnp.concatenate([jnp.dot(qbd, flat(kt_ref), preferred_element_type=F32),
                         lax.dot_general(qbd, new_rows(kn_ref), NT, preferred_element_type=F32)],
                        axis=-1) + bias_ref[...]
    m = jnp.max(s, axis=-1, keepdims=True)
    p = jnp.exp(s - m) * mult_ref[...]
    den = jnp.sum(p, axis=-1, keepdims=True)
    pb = p.astype(BF16)
    w_buf = kt_ref.shape[-1]
    acc = (lax.dot_general(pb[:, :w_buf], flat(vt_ref), NT, preferred_element_type=F32)
           + jnp.dot(pb[:, w_buf:], new_rows(vn_ref), preferred_element_type=F32))
    acc = jnp.where(own, acc / den, 0.0)
    for t in range(SAMPLE_T):
        o_ref[t:t + 1, :] = jnp.sum(acc[t * N_HEADS:(t + 1) * N_HEADS, :], axis=0, keepdims=True)


def _attn_sample(q, k_new, v_new, cache_kt, cache_vt, layer, bias, mult):
    nb = q.shape[0]
    w_buf = cache_kt.shape[-1]
    tok = pl.BlockSpec((None, SAMPLE_T, ATT_DIM), lambda b: (b, 0, 0))
    new = pl.BlockSpec((None, SUBLANES, ATT_DIM), lambda b: (b, 0, 0))
    old = pl.BlockSpec((None, None, N_HEADS, HEAD_DIM, w_buf), lambda b: (layer, b, 0, 0, 0))
    tbl = _full((SAMPLE_T * N_HEADS, w_buf + NEW_COLS))
    return pl.pallas_call(
        _attn_sample_kernel,
        grid=(nb,),
        in_specs=[tok, old, new, old, new, tbl, tbl],
        out_specs=tok,
        out_shape=jax.ShapeDtypeStruct((nb, SAMPLE_T, ATT_DIM), F32),
        compiler_params=_cparams(1),
        name="attn_sample",
    )(q, cache_kt, k_new, cache_vt, v_new, bias, mult)


def _split_dot(x, e_ref):
    hi = x.astype(BF16)
    lo = (x - hi.astype(F32)).astype(BF16)
    return (jnp.dot(hi, e_ref[...], preferred_element_type=F32)
            + jnp.dot(lo, e_ref[...], preferred_element_type=F32))


def _outproj_kernel(*refs, n_pat):
    mix = n_pat > 1
    n_lse = n_pat if mix else 0
    o_refs = refs[0:n_pat]
    l_refs = refs[n_pat:n_pat + n_lse]
    rest = refs[n_pat + n_lse:]
    if mix:
        yn_ref, h_ref, gatt_ref, exp_ref, unperm_ref, wa_ref, wc_ref, out_ref = rest
        lses = [r[...] for r in l_refs]
        top = functools.reduce(jnp.maximum, lses)
        ws = [jnp.exp(l - top) for l in lses]
        tot = functools.reduce(lambda a, b: a + b, ws)
        att = None
        for w, o_ref in zip(ws, o_refs):
            term = _split_dot(w / tot, exp_ref) * o_ref[...]
            att = term if att is None else att + term
        att_bf = jnp.dot(unperm_ref[...], _rms(att, gatt_ref[...]).astype(BF16),
                         preferred_element_type=F32).astype(BF16)
    else:
        yn_ref, h_ref, gatt_ref, wa_ref, wc_ref, out_ref = rest
        att_bf = _rms(o_refs[0][...], gatt_ref[...]).astype(BF16)
    y = (jnp.dot(att_bf, wa_ref[...], preferred_element_type=F32)
         + jnp.dot(yn_ref[...].astype(BF16), wc_ref[...], preferred_element_type=F32))
    out_ref[...] = h_ref[...] + y


def _outproj(os, lses, yn, h, g_att, wa_bf, wc_bf, expand=None, unperm=None, *, tm):
    n = h.shape[0]
    n_pat = len(os)
    tok = lambda w: pl.BlockSpec((tm, w), lambda i: (i, 0))
    in_specs = [tok(ATT_DIM)] * n_pat + [tok(LANES)] * len(lses) + [tok(CONV_DIM), tok(D_MODEL), _full((1, ATT_DIM))]
    args = [*os, *lses, yn, h, g_att]
    if n_pat > 1:
        in_specs += [_full((LANES, ATT_DIM)), _full((tm, tm))]
        args += [expand, unperm]
    in_specs += [_full((ATT_DIM, D_MODEL)), _full((CONV_DIM, D_MODEL))]
    args += [wa_bf, wc_bf]
    return pl.pallas_call(
        functools.partial(_outproj_kernel, n_pat=n_pat),
        grid=(n // tm,),
        in_specs=in_specs,
        out_specs=tok(D_MODEL),
        out_shape=jax.ShapeDtypeStruct((n, D_MODEL), F32),
        compiler_params=_cparams(1),
        name="outproj",
    )(*args)


ROUTE_I1, ROUTE_I2, ROUTE_R1, ROUTE_R2, ROUTE_W1, ROUTE_W2 = range(6)
GROUP_LANE0 = N_EXPERTS


def _route_kernel(h_ref, g_ref, wr_ref, route_ref, cnt_ref, carry_ref):
    i = pl.program_id(0)

    @pl.when(i == 0)
    def _():
        carry_ref[...] = jnp.zeros_like(carry_ref)

    m = _rms(h_ref[...], g_ref[...])
    logits = jnp.dot(m.astype(BF16), wr_ref[...], preferred_element_type=F32)
    tm = logits.shape[0]
    lane_i = lax.broadcasted_iota(jnp.int32, (tm, LANES), 1)
    lane = lane_i.astype(F32)
    big = jnp.float32(4 * LANES)

    is_g = jnp.logical_and(lane_i >= GROUP_LANE0, lane_i < GROUP_LANE0 + N_GROUPS)
    gl = jnp.where(is_g, logits, NEG)
    gmax = jnp.max(gl, axis=-1, keepdims=True)
    g_w = 1.0 / jnp.sum(jnp.where(is_g, jnp.exp(gl - gmax), 0.0), axis=-1, keepdims=True)
    g_sel = jnp.min(jnp.where(gl == gmax, lane - GROUP_LANE0, big), axis=-1, keepdims=True)

    grp_of_lane = (lane_i // EXPERTS_PER_GROUP).astype(F32)
    in_grp = jnp.logical_and(lane_i < N_EXPERTS, grp_of_lane == g_sel)
    el = jnp.where(in_grp, logits, NEG)
    t1 = jnp.max(el, axis=-1, keepdims=True)
    i1 = jnp.min(jnp.where(el == t1, lane, big), axis=-1, keepdims=True)
    el2 = jnp.where(lane == i1, NEG, el)
    t2 = jnp.max(el2, axis=-1, keepdims=True)
    i2 = jnp.min(jnp.where(el2 == t2, lane, big), axis=-1, keepdims=True)
    e2 = jnp.exp(t2 - t1)
    w1 = g_w / (1.0 + e2)
    w2 = g_w * e2 / (1.0 + e2)

    hit1 = lane == i1
    hit2 = lane == i2
    c = jnp.where(jnp.logical_or(hit1, hit2), 1.0, 0.0)
    rr = lax.broadcasted_iota(jnp.int32, (tm, tm), 0)
    cc = lax.broadcasted_iota(jnp.int32, (tm, tm), 1)
    lower = jnp.where(rr > cc, 1.0, 0.0).astype(BF16)
    before = jnp.dot(lower, c.astype(BF16), preferred_element_type=F32) + carry_ref[0:1, :]
    r1 = jnp.sum(jnp.where(hit1, before, 0.0), axis=-1, keepdims=True)
    r2 = jnp.sum(jnp.where(hit2, before, 0.0), axis=-1, keepdims=True)
    total = carry_ref[0:1, :] + jnp.sum(c, axis=0, keepdims=True)
    carry_ref[...] = jnp.broadcast_to(total, carry_ref.shape)
    cnt_ref[...] = jnp.broadcast_to(total, cnt_ref.shape)

    rec = jnp.zeros((tm, LANES), F32)
    for idx, val in ((ROUTE_I1, i1), (ROUTE_I2, i2), (ROUTE_R1, r1), (ROUTE_R2, r2), (ROUTE_W1, w1), (ROUTE_W2, w2)):
        rec = jnp.where(lane_i == idx, val, rec)
    route_ref[...] = rec


def _route(h, g_ffn, w_router, *, tm):
    n = h.shape[0]
    tok = lambda w: pl.BlockSpec((tm, w), lambda i: (i, 0))
    return pl.pallas_call(
        _route_kernel,
        grid=(n // tm,),
        in_specs=[tok(D_MODEL), _full((1, D_MODEL)), _full((D_MODEL, LANES))],
        out_specs=[tok(LANES), _full((SUBLANES, LANES))],
        out_shape=[jax.ShapeDtypeStruct((n, LANES), F32), jax.ShapeDtypeStruct((SUBLANES, LANES), F32)],
        scratch_shapes=[pltpu.VMEM((SUBLANES, LANES), F32)],
        compiler_params=_cparams(1),
        name="route",
    )(h, g_ffn, w_router)


def _dispatch_kernel(pos_ref, h_ref, g_ref, xs_ref, mbuf, sem):
    tm = mbuf.shape[0]
    mbuf[...] = _rms(h_ref[...], g_ref[...])

    def row_copy(t, p):
        return pltpu.make_async_copy(mbuf.at[pl.ds(t, 1), :], xs_ref.at[pl.ds(p, 1), :], sem)

    def issue(t, carry):
        row_copy(t, pos_ref[0, 0, 2 * t]).start()
        row_copy(t, pos_ref[0, 0, 2 * t + 1]).start()
        return carry

    lax.fori_loop(0, tm, issue, 0)

    def drain(t, carry):
        row_copy(0, 0).wait()
        row_copy(0, 0).wait()
        return carry

    lax.fori_loop(0, tm, drain, 0)


def _dispatch(h, g_ffn, pos, n_rows, *, tm):
    n = h.shape[0]
    return pl.pallas_call(
        _dispatch_kernel,
        grid=(n // tm,),
        in_specs=[pl.BlockSpec((1, 1, 2 * tm), lambda i: (i, 0, 0), memory_space=pltpu.SMEM),
                  pl.BlockSpec((tm, D_MODEL), lambda i: (i, 0)), _full((1, D_MODEL))],
        out_specs=pl.BlockSpec(memory_space=pl.ANY),
        out_shape=jax.ShapeDtypeStruct((n_rows, D_MODEL), F32),
        scratch_shapes=[pltpu.VMEM((tm, D_MODEL), F32), pltpu.SemaphoreType.DMA(())],
        compiler_params=pltpu.CompilerParams(dimension_semantics=("arbitrary",), vmem_limit_bytes=VMEM_LIMIT,
                                             has_side_effects=True),
        name="dispatch",
    )(pos, h, g_ffn)


def _expert_kernel(te_ref, tv_ref, x_ref, wg_ref, wu_ref, wd_ref, y_ref, wg_bf, wu_bf, wd_bf):
    i = pl.program_id(0)
    valid = tv_ref[i]
    changed = jnp.logical_or(i == 0, te_ref[i] != te_ref[jnp.maximum(i - 1, 0)])

    @pl.when(jnp.logical_and(changed, valid > 0))
    def _():
        wg_bf[...] = wg_ref[0].astype(BF16)
        wu_bf[...] = wu_ref[0].astype(BF16)
        wd_bf[...] = wd_ref[0].astype(BF16)

    @pl.when(valid > 0)
    def _():
        te = x_ref.shape[0]
        row = lax.broadcasted_iota(jnp.int32, (te, 1), 0)
        x = jnp.where(row < valid, x_ref[...], 0.0).astype(BF16)
        hg = jnp.dot(x, wg_bf[...], preferred_element_type=F32)
        hu = jnp.dot(x, wu_bf[...], preferred_element_type=F32)
        hid = (hg * jax.nn.sigmoid(hg)) * hu
        y_ref[...] = jnp.dot(hid.astype(BF16), wd_bf[...], preferred_element_type=F32)

    @pl.when(valid <= 0)
    def _():
        y_ref[...] = jnp.zeros_like(y_ref)


def _experts(xs, tile_expert, tile_valid, w_gate, w_up, w_down, *, te):
    n_rows = xs.shape[0]
    wspec = lambda a, b: pl.BlockSpec((1, a, b), lambda i, e, v: (e[i], 0, 0))
    grid_spec = pltpu.PrefetchScalarGridSpec(
        num_scalar_prefetch=2,
        grid=(n_rows // te,),
        in_specs=[pl.BlockSpec((te, D_MODEL), lambda i, e, v: (i, 0)),
                  wspec(D_MODEL, D_EXPERT), wspec(D_MODEL, D_EXPERT), wspec(D_EXPERT, D_MODEL)],
        out_specs=pl.BlockSpec((te, D_MODEL), lambda i, e, v: (i, 0)),
        scratch_shapes=[pltpu.VMEM((D_MODEL, D_EXPERT), BF16), pltpu.VMEM((D_MODEL, D_EXPERT), BF16),
                        pltpu.VMEM((D_EXPERT, D_MODEL), BF16)])
    return pl.pallas_call(
        _expert_kernel,
        grid_spec=grid_spec,
        out_shape=jax.ShapeDtypeStruct((n_rows, D_MODEL), F32),
        compiler_params=_cparams(1),
        name="experts",
    )(tile_expert, tile_valid, xs, w_gate, w_up, w_down)


def _combine_kernel(pos_ref, h_ref, route_ref, ys_ref, p_ref, gple_ref, wgate_ref, wproj_ref, out_ref,
                    y0, y1, sem):
    tm = y0.shape[0]

    def row_copy(p, dst, t):
        return pltpu.make_async_copy(ys_ref.at[pl.ds(p, 1), :], dst.at[pl.ds(t, 1), :], sem)

    def issue(t, carry):
        row_copy(pos_ref[0, 0, 2 * t], y0, t).start()
        row_copy(pos_ref[0, 0, 2 * t + 1], y1, t).start()
        return carry

    lax.fori_loop(0, tm, issue, 0)

    def drain(t, carry):
        row_copy(0, y0, 0).wait()
        row_copy(0, y1, 0).wait()
        return carry

    lax.fori_loop(0, tm, drain, 0)

    rec = route_ref[...]
    w1 = rec[:, ROUTE_W1:ROUTE_W1 + 1]
    w2 = rec[:, ROUTE_W2:ROUTE_W2 + 1]
    h2 = h_ref[...] + w1 * y0[...] + w2 * y1[...]
    gate = jax.nn.sigmoid(jnp.dot(_rms(h2, gple_ref[...]).astype(BF16), wgate_ref[...], preferred_element_type=F32))
    ple = jnp.dot(p_ref[...].astype(BF16), wproj_ref[...], preferred_element_type=F32)
    out_ref[...] = h2 + ple * gate


def _combine(h, route, pos, ys, p, g_ple, wgate_bf, wproj_bf, *, tm):
    n = h.shape[0]
    tok = lambda w: pl.BlockSpec((tm, w), lambda i: (i, 0))
    return pl.pallas_call(
        _combine_kernel,
        grid=(n // tm,),
        in_specs=[pl.BlockSpec((1, 1, 2 * tm), lambda i: (i, 0, 0), memory_space=pltpu.SMEM),
                  tok(D_MODEL), tok(LANES), pl.BlockSpec(memory_space=pl.ANY), tok(D_PLE),
                  _full((1, D_MODEL)), _full((D_MODEL, D_MODEL)), _full((D_PLE, D_MODEL))],
        out_specs=tok(D_MODEL),
        out_shape=jax.ShapeDtypeStruct((n, D_MODEL), F32),
        scratch_shapes=[pltpu.VMEM((tm, D_MODEL), F32), pltpu.VMEM((tm, D_MODEL), F32),
                        pltpu.SemaphoreType.DMA(())],
        compiler_params=_cparams(1),
        name="combine",
    )(pos, h, route, ys, p, g_ple, wgate_bf, wproj_bf)


def _lookup(table, idx, size):
    hit = idx[..., None] == jnp.arange(size, dtype=jnp.int32)
    return jnp.sum(jnp.where(hit, table, 0), axis=-1)


def _moe_ple(h, p, g_ffn, w_router, w_gate, w_up, w_down, g_ple, wgate_bf, wproj_bf, *, tm, te):
    n = h.shape[0]
    route, counts = _route(h, g_ffn, w_router, tm=tm)
    counts = counts[0, :N_EXPERTS].astype(jnp.int32)
    padded = ((counts + te - 1) // te) * te
    ends = jnp.cumsum(padded)
    offs = ends - padded
    ids = route[:, ROUTE_I1:ROUTE_I2 + 1].astype(jnp.int32)
    ranks = route[:, ROUTE_R1:ROUTE_R2 + 1].astype(jnp.int32)
    pos = (_lookup(offs, ids, N_EXPERTS) + ranks).reshape(n // tm, 1, 2 * tm)
    n_tiles = (2 * n) // te + N_EXPERTS
    starts = jnp.arange(n_tiles, dtype=jnp.int32) * te
    tile_expert = jnp.minimum(jnp.sum((starts[:, None] >= ends[None, :]).astype(jnp.int32), axis=-1), N_EXPERTS - 1)
    tile_valid = jnp.clip(_lookup(counts, tile_expert, N_EXPERTS)
                          - (starts - _lookup(offs, tile_expert, N_EXPERTS)), 0, te).astype(jnp.int32)
    xs = _dispatch(h, g_ffn, pos, n_tiles * te, tm=tm)
    ys = _experts(xs, tile_expert, tile_valid, w_gate, w_up, w_down, te=te)
    return _combine(h, route, pos, ys, p, g_ple, wgate_bf, wproj_bf, tm=tm)


REC_E1, REC_E2, REC_S1, REC_S2, REC_W1, REC_W2 = range(6)
PACKED = D_MODEL // 2
ROW_W = PACKED + LANES
LIST_LANES = LANES
LIST_COUNT = LIST_LANES - 1


def _slots(tm):
    need = 2 * tm + (SUBLANES - 1) * N_EXPERTS + SUBLANES
    return -(-need // LANES) * LANES


def _pieces(x):
    hi = x.astype(BF16)
    r1 = x - hi.astype(F32)
    mid = r1.astype(BF16)
    return hi, mid, (r1 - mid.astype(F32)).astype(BF16)


def _sort_kernel(h_ref, g_ref, wr_ref, before_ref, sel_ref, xs_ref, rec_ref, cnt_ref):
    m = _rms(h_ref[...], g_ref[...])
    logits = jnp.dot(m.astype(BF16), wr_ref[...], preferred_element_type=F32)
    tm = logits.shape[0]
    slots = xs_ref.shape[0]
    lane_i = lax.broadcasted_iota(jnp.int32, (tm, LANES), 1)
    lane = lane_i.astype(F32)
    big = jnp.float32(4 * LANES)

    is_g = jnp.logical_and(lane_i >= GROUP_LANE0, lane_i < GROUP_LANE0 + N_GROUPS)
    gl = jnp.where(is_g, logits, NEG)
    gmax = jnp.max(gl, axis=-1, keepdims=True)
    g_w = 1.0 / jnp.sum(jnp.where(is_g, jnp.exp(gl - gmax), 0.0), axis=-1, keepdims=True)
    g_sel = jnp.min(jnp.where(gl == gmax, lane - GROUP_LANE0, big), axis=-1, keepdims=True)

    grp_of_lane = (lane_i // EXPERTS_PER_GROUP).astype(F32)
    in_grp = jnp.logical_and(lane_i < N_EXPERTS, grp_of_lane == g_sel)
    el = jnp.where(in_grp, logits, NEG)
    t1 = jnp.max(el, axis=-1, keepdims=True)
    e1 = jnp.min(jnp.where(el == t1, lane, big), axis=-1, keepdims=True)
    el2 = jnp.where(lane == e1, NEG, el)
    t2 = jnp.max(el2, axis=-1, keepdims=True)
    e2 = jnp.min(jnp.where(el2 == t2, lane, big), axis=-1, keepdims=True)
    ex = jnp.exp(t2 - t1)
    w1 = g_w / (1.0 + ex)
    w2 = g_w * ex / (1.0 + ex)

    hit1 = lane == e1
    hit2 = lane == e2
    c = jnp.where(hit1, 1.0, jnp.where(hit2, 1.0, 0.0))
    rr = lax.broadcasted_iota(jnp.int32, (tm, tm), 0)
    cc = lax.broadcasted_iota(jnp.int32, (tm, tm), 1)
    lower = jnp.where(rr > cc, 1.0, 0.0).astype(BF16)
    rank = jnp.dot(lower, c.astype(BF16), preferred_element_type=F32)
    cnt = jnp.sum(c, axis=0, keepdims=True)
    chunks = jnp.floor((cnt + (SUBLANES - 1)) * (1.0 / SUBLANES))
    start = SUBLANES * jnp.dot(jnp.broadcast_to(chunks, (SUBLANES, LANES)).astype(BF16), before_ref[...],
                               preferred_element_type=F32)[0:1, :]
    slot_of = rank + start
    s1 = jnp.sum(jnp.where(hit1, slot_of, 0.0), axis=-1, keepdims=True)
    s2 = jnp.sum(jnp.where(hit2, slot_of, 0.0), axis=-1, keepdims=True)

    rec = jnp.zeros((tm, LANES), F32)
    for idx, val in ((REC_E1, e1), (REC_E2, e2), (REC_S1, s1), (REC_S2, s2), (REC_W1, w1), (REC_W2, w2)):
        rec = jnp.where(lane_i == idx, val, rec)
    rec_ref[...] = rec
    cnt_ref[0] = jnp.broadcast_to(cnt, (SUBLANES, LANES))

    rec_parts = _pieces(rec)
    srow = sum(lax.dot_general(sel_ref[...], part, NT, preferred_element_type=F32) for part in rec_parts)
    slot_id = lax.broadcasted_iota(jnp.int32, (slots, tm), 0).astype(F32)
    place = jnp.where(slot_id == srow[0:1, :], 1.0, jnp.where(slot_id == srow[1:2, :], 1.0, 0.0)).astype(BF16)
    payload = jnp.concatenate([m.astype(BF16), *rec_parts], axis=-1)
    moved = jnp.dot(place, payload, preferred_element_type=F32)
    info = (moved[:, D_MODEL:D_MODEL + LANES] + moved[:, D_MODEL + LANES:D_MODEL + 2 * LANES]
            + moved[:, D_MODEL + 2 * LANES:])
    bits = lambda a: pltpu.bitcast(a, jnp.uint32)
    low = lax.shift_right_logical(bits(moved[:, :PACKED]), jnp.uint32(16))
    high = jnp.bitwise_and(bits(moved[:, PACKED:D_MODEL]), jnp.uint32(0xFFFF0000))
    xs_ref[...] = jnp.concatenate([jnp.bitwise_or(low, high), bits(info)], axis=-1)


def _sort(h, g_ffn, w_router, before, sel, *, tm):
    n = h.shape[0]
    slots = _slots(tm)
    tok = lambda w: pl.BlockSpec((tm, w), lambda i: (i, 0))
    return pl.pallas_call(
        _sort_kernel,
        grid=(n // tm,),
        in_specs=[tok(D_MODEL), _full((1, D_MODEL)), _full((D_MODEL, LANES)), _full((LANES, LANES)),
                  _full((SUBLANES, LANES))],
        out_specs=[pl.BlockSpec((slots, ROW_W), lambda i: (i, 0)), tok(LANES),
                   pl.BlockSpec((1, SUBLANES, LANES), lambda i: (i, 0, 0))],
        out_shape=[jax.ShapeDtypeStruct((n // tm * slots, ROW_W), jnp.uint32), jax.ShapeDtypeStruct((n, LANES), F32),
                   jax.ShapeDtypeStruct((n // tm, SUBLANES, LANES), F32)],
        compiler_params=_cparams(1),
        name="moe_sort",
    )(h, g_ffn, w_router, before, sel)


def _chunk_copy(src_hbm, row, dst, c, sem):
    if not isinstance(row, int):
        row = pl.multiple_of(row, SUBLANES)
    return pltpu.make_async_copy(src_hbm.at[pl.ds(row, SUBLANES), :],
                                 dst.at[pl.ds(c * SUBLANES, SUBLANES), :], sem)


def _expert_kernel2(te_ref, tv_ref, src_ref, nxt_ref, xs_ref, wg_ref, wu_ref, wd_ref, y_ref,
                    xbuf, sem, wg_bf, wu_bf, wd_bf):
    i = pl.program_id(0)
    n = pl.num_programs(0)
    te = xbuf.shape[1]
    slot = i % 2

    def fetch(list_ref, to_slot):
        for c in range(te // SUBLANES):
            _chunk_copy(xs_ref, list_ref[0, 0, c], xbuf.at[to_slot], c, sem.at[to_slot]).start()

    @pl.when(jnp.logical_and(i == 0, tv_ref[0] > 0))
    def _():
        fetch(src_ref, 0)

    nxt = jnp.minimum(i + 1, n - 1)

    @pl.when(jnp.logical_and(i + 1 < n, tv_ref[nxt] > 0))
    def _():
        fetch(nxt_ref, 1 - slot)

    valid = tv_ref[i]
    changed = jnp.logical_or(i == 0, te_ref[i] != te_ref[jnp.maximum(i - 1, 0)])

    @pl.when(jnp.logical_and(changed, valid > 0))
    def _():
        wg_bf[...] = wg_ref[0].astype(BF16)
        wu_bf[...] = wu_ref[0].astype(BF16)
        wd_bf[...] = wd_ref[0].astype(BF16)

    @pl.when(valid > 0)
    def _():
        pltpu.make_async_copy(xbuf.at[slot], xbuf.at[slot], sem.at[slot]).wait()
        rows = xbuf[slot]
        words = rows[:, :PACKED]
        as_f32 = lambda a: pltpu.bitcast(a, F32)
        x = jnp.concatenate([as_f32(lax.shift_left(words, jnp.uint32(16))),
                             as_f32(jnp.bitwise_and(words, jnp.uint32(0xFFFF0000)))], axis=-1).astype(BF16)
        info = as_f32(rows[:, PACKED:])
        mine = info[:, REC_E1:REC_E1 + 1] == te_ref[i].astype(F32)
        gate = jnp.where(mine, info[:, REC_W1:REC_W1 + 1], info[:, REC_W2:REC_W2 + 1])
        hg = jnp.dot(x, wg_bf[...], preferred_element_type=F32)
        hu = jnp.dot(x, wu_bf[...], preferred_element_type=F32)
        hid = (hg * jax.nn.sigmoid(hg)) * hu * gate
        y_ref[...] = jnp.dot(hid.astype(BF16), wd_bf[...], preferred_element_type=F32)

    @pl.when(valid <= 0)
    def _():
        y_ref[...] = jnp.zeros_like(y_ref)


def _experts2(xs, tile_expert, tile_valid, src, w_gate, w_up, w_down, *, te, layer):
    n_tiles = tile_expert.shape[0]
    wspec = lambda a, b: pl.BlockSpec((None, 1, a, b), lambda i, e, v: (layer, e[i], 0, 0))
    lst = lambda shift: pl.BlockSpec((1, 1, LIST_LANES),
                                     lambda i, e, v: (jnp.minimum(i + shift, n_tiles - 1), 0, 0),
                                     memory_space=pltpu.SMEM)
    grid_spec = pltpu.PrefetchScalarGridSpec(
        num_scalar_prefetch=2,
        grid=(n_tiles,),
        in_specs=[lst(0), lst(1), pl.BlockSpec(memory_space=pl.ANY),
                  wspec(D_MODEL, D_EXPERT), wspec(D_MODEL, D_EXPERT), wspec(D_EXPERT, D_MODEL)],
        out_specs=pl.BlockSpec((te, D_MODEL), lambda i, e, v: (i, 0)),
        scratch_shapes=[pltpu.VMEM((2, te, ROW_W), jnp.uint32), pltpu.SemaphoreType.DMA((2,)),
                        pltpu.VMEM((D_MODEL, D_EXPERT), BF16), pltpu.VMEM((D_MODEL, D_EXPERT), BF16),
                        pltpu.VMEM((D_EXPERT, D_MODEL), BF16)])
    return pl.pallas_call(
        _expert_kernel2,
        grid_spec=grid_spec,
        out_shape=jax.ShapeDtypeStruct((n_tiles * te, D_MODEL), F32),
        compiler_params=_cparams(1),
        name="experts",
    )(tile_expert, tile_valid, src, src, xs, w_gate, w_up, w_down)


def _combine_kernel2(dst_ref, nxt_ref, h_ref, rec_ref, ys_ref, p_ref, gple_ref, wgate_ref, wproj_ref, out_ref,
                     ybuf, sem):
    i = pl.program_id(0)
    n = pl.num_programs(0)
    slot = i % 2
    tm = h_ref.shape[0]
    slots = ybuf.shape[1]

    def fetch(list_ref, to_slot):
        def body(c, carry):
            _chunk_copy(ys_ref, list_ref[0, 0, c], ybuf.at[to_slot], c, sem.at[to_slot]).start()
            return carry
        lax.fori_loop(0, list_ref[0, 0, LIST_COUNT], body, 0)

    @pl.when(i == 0)
    def _():
        ybuf[...] = jnp.zeros_like(ybuf)
        fetch(dst_ref, 0)

    @pl.when(i + 1 < n)
    def _():
        fetch(nxt_ref, 1 - slot)

    def drain(c, carry):
        _chunk_copy(ys_ref, 0, ybuf.at[slot], 0, sem.at[slot]).wait()
        return carry

    lax.fori_loop(0, dst_ref[0, 0, LIST_COUNT], drain, 0)

    rec = rec_ref[...]
    slot_id = lax.broadcasted_iota(jnp.int32, (tm, slots), 1).astype(F32)
    back = jnp.where(slot_id == rec[:, REC_S1:REC_S1 + 1], 1.0,
                     jnp.where(slot_id == rec[:, REC_S2:REC_S2 + 1], 1.0, 0.0)).astype(BF16)
    ys = ybuf[slot]
    y_hi = ys.astype(BF16)
    y_lo = (ys - y_hi.astype(F32)).astype(BF16)
    h2 = h_ref[...] + (jnp.dot(back, y_hi, preferred_element_type=F32)
                       + jnp.dot(back, y_lo, preferred_element_type=F32))
    gate = jax.nn.sigmoid(jnp.dot(_rms(h2, gple_ref[...]).astype(BF16), wgate_ref[...], preferred_element_type=F32))
    ple = jnp.dot(p_ref[...].astype(BF16), wproj_ref[...], preferred_element_type=F32)
    out_ref[...] = h2 + ple * gate


def _combine2(h, rec, dst, ys, p, g_ple, wgate_bf, wproj_bf, *, tm, layer):
    n = h.shape[0]
    n_tok = n // tm
    tok = lambda w: pl.BlockSpec((tm, w), lambda i: (i, 0))
    p_spec = pl.BlockSpec((None, tm, D_PLE), lambda i: (layer, i, 0))
    lst = lambda shift: pl.BlockSpec((1, 1, LIST_LANES), lambda i: (jnp.minimum(i + shift, n_tok - 1), 0, 0),
                                     memory_space=pltpu.SMEM)
    return pl.pallas_call(
        _combine_kernel2,
        grid=(n_tok,),
        in_specs=[lst(0), lst(1), tok(D_MODEL), tok(LANES), pl.BlockSpec(memory_space=pl.ANY), p_spec,
                  _full((1, D_MODEL)), _full((D_MODEL, D_MODEL)), _full((D_PLE, D_MODEL))],
        out_specs=tok(D_MODEL),
        out_shape=jax.ShapeDtypeStruct((n, D_MODEL), F32),
        scratch_shapes=[pltpu.VMEM((2, _slots(tm), D_MODEL), F32), pltpu.SemaphoreType.DMA((2,))],
        compiler_params=_cparams(1),
        name="combine",
    )(dst, dst, h, rec, ys, p, g_ple, wgate_bf, wproj_bf)


def _excl_cumsum(x, axis):
    return jnp.cumsum(x, axis=axis) - x


def _chunk_plan(cnt, *, tm, te):
    n_tok = cnt.shape[0]
    slots = _slots(tm)
    per_tile = te // SUBLANES
    chunks = (cnt + SUBLANES - 1) // SUBLANES
    run0 = _excl_cumsum(chunks, 1)
    n_run = jnp.sum(chunks, axis=1)
    seg0 = _excl_cumsum(chunks, 0)
    total = jnp.sum(chunks, axis=0)
    region = ((total + per_tile - 1) // per_tile) * per_tile
    reg_end = jnp.cumsum(region)
    reg0 = reg_end - region

    n_tiles = -(-(2 * n_tok * tm + (SUBLANES - 1) * N_EXPERTS * n_tok) // te) + N_EXPERTS
    t0 = jnp.arange(n_tiles, dtype=jnp.int32) * per_tile
    tile_expert = jnp.minimum(jnp.sum((t0[:, None] >= reg_end[None, :]).astype(jnp.int32), axis=1), N_EXPERTS - 1)
    pick = tile_expert[:, None] == jnp.arange(N_EXPERTS, dtype=jnp.int32)[None, :]
    of_tile = lambda v: jnp.sum(jnp.where(pick, v[None, :], 0), axis=1)
    tile_valid = jnp.clip(of_tile(total) - (t0 - of_tile(reg0)), 0, per_tile).astype(jnp.int32)

    q = (t0 - of_tile(reg0))[:, None] + jnp.arange(per_tile, dtype=jnp.int32)[None, :]
    col_of_tile = lambda v: jnp.sum(jnp.where(pick[:, None, :], v[None, :, :], 0), axis=2)
    seg0_t, seg1_t, run0_t = col_of_tile(seg0), col_of_tile(seg0 + chunks), col_of_tile(run0)
    holds = (q[:, :, None] >= seg0_t[:, None, :]) & (q[:, :, None] < seg1_t[:, None, :])
    local = run0_t[:, None, :] + q[:, :, None] - seg0_t[:, None, :]
    row = jnp.arange(n_tok, dtype=jnp.int32)[None, None, :] * slots + SUBLANES * local
    src = jnp.sum(jnp.where(holds, row, 0), axis=2)
    src = jnp.where(jnp.any(holds, axis=2), src, slots - SUBLANES)
    src = jnp.pad(src, ((0, 0), (0, LIST_LANES - per_tile))).reshape(n_tiles, 1, LIST_LANES).astype(jnp.int32)

    j = jnp.arange(LIST_LANES, dtype=jnp.int32)[None, :, None]
    inside = (j >= run0[:, None, :]) & (j < (run0 + chunks)[:, None, :])
    base = (reg0[None, :] + seg0 - run0)[:, None, :]
    dst = SUBLANES * jnp.sum(jnp.where(inside, base + j, 0), axis=2)
    dst = dst.at[:, LIST_COUNT].set(n_run).reshape(n_tok, 1, LIST_LANES).astype(jnp.int32)
    return tile_expert.astype(jnp.int32), tile_valid, src, dst


def _moe_ple2(h, p, g_ffn, w_router, before, sel, w_gate, w_up, w_down, g_ple, wgate_bf, wproj_bf, *,
              tm, te, layer):
    xs, rec, cnt = _sort(h, g_ffn, w_router, before, sel, tm=tm)
    tile_expert, tile_valid, src, dst = _chunk_plan(cnt[:, 0, :N_EXPERTS].astype(jnp.int32), tm=tm, te=te)
    ys = _experts2(xs, tile_expert, tile_valid, src, w_gate, w_up, w_down, te=te, layer=layer)
    return _combine2(h, rec, dst, ys, p, g_ple, wgate_bf, wproj_bf, tm=tm, layer=layer)


def _bucket_np(dist):
    max_exact = N_BUCKETS // 2
    d_f = np.maximum(dist, 1).astype(np.float32)
    large = max_exact + (np.log(d_f / np.float32(max_exact)) / np.float32(np.log(MAX_DISTANCE / max_exact))
                         * np.float32(N_BUCKETS - max_exact)).astype(np.int32)
    large = np.minimum(large, N_BUCKETS - 1)
    return np.where(dist < max_exact, dist, large).astype(np.int32)


def _bias_from_buckets(rel_bias, bucket, valid):
    onehot = (jnp.asarray(bucket)[..., None] == jnp.arange(N_BUCKETS, dtype=jnp.int32)).astype(F32)
    bias = jnp.einsum("...k,kh->h...", onehot, rel_bias.astype(F32), precision=HIGHEST)
    return jnp.where(jnp.asarray(valid)[None], bias, NEG)


def _block_order(dil):
    g = np.arange(Q_BLOCK) // SUBLANES
    j = np.arange(Q_BLOCK) % SUBLANES
    if dil == 1:
        return 16 * j + g
    if dil == 4:
        return 32 * (g // 4) + 4 * j + g % 4
    return SUBLANES * g + j


def _band_bias(rel_bias, dil):
    mu = _block_order(dil)
    qi = mu[:, None] + Q_BLOCK
    ki = np.concatenate([mu, mu + Q_BLOCK])[None, :]
    off = qi - ki
    valid = (off >= 0) & (off <= N_KEYS)
    bucket = _bucket_np(dil * np.clip(off, 0, N_KEYS))
    first = valid & (np.arange(2 * Q_BLOCK)[None, :] >= Q_BLOCK)
    tables = [_bias_from_buckets(rel_bias, bucket, v).reshape(N_HEADS * Q_BLOCK, 2 * Q_BLOCK) for v in (valid, first)]
    return jnp.stack(tables)


def _sample_tables(rel_bias, w_buf):
    qpos = w_buf + np.arange(SAMPLE_T)[:, None]
    pos = np.arange(w_buf + NEW_COLS)[None, :]
    dist = qpos - pos
    in_seq = (dist >= 0) & (pos < w_buf + SAMPLE_T)
    mult = np.zeros(dist.shape, np.float32)
    for (w, d) in PATTERNS:
        mult += in_seq & (dist % d == 0) & (dist <= w)
    bucket = _bucket_np(np.maximum(dist, 0))
    bias = jnp.transpose(_bias_from_buckets(rel_bias, bucket, mult > 0), (1, 0, 2))
    rows = SAMPLE_T * N_HEADS
    mult_rows = np.broadcast_to(mult[:, None, :], (SAMPLE_T, N_HEADS, mult.shape[-1]))
    return bias.reshape(rows, -1), jnp.asarray(mult_rows.reshape(rows, -1))


PROJ_TM = 256
OUT_TM = 512
PROMPT_TM = 256
PROMPT_TE = 256
ATTN_SUB = 4


def _row_perm(tm):
    a = np.arange(tm)
    src = (a // Q_BLOCK) * Q_BLOCK + 16 * (a % SUBLANES) + (a % Q_BLOCK) // SUBLANES
    perm = np.zeros((tm, tm), np.float32)
    perm[a, src] = 1.0
    return perm


def kernel(x_prompt, x_sample, cache_k, cache_v, state_conv, p_prompt, p_sample, rel_bias, g_mix, w_in, q_gain,
           k_gain, conv_w, g_out_att, g_out_conv, w_out, g_ffn, w_router_group, w_router_expert, w_gate, w_up,
           w_down, g_ple, w_ple_gate, w_ple_proj):
    depth = w_in.shape[0]
    batch, seq, _ = x_prompt.shape
    dec_b, dec_t, _ = x_sample.shape
    w_buf = cache_k.shape[2]
    n_s = dec_b * dec_t
    keep = min(w_buf, seq)
    assert dec_t == SAMPLE_T and w_buf % LANES == 0
    assert seq % (Q_BLOCK * 16 * 2) == 0 and keep % PROJ_TM == 0 and seq % PROMPT_TM == 0
    cache_kt = jnp.transpose(cache_k, (0, 1, 3, 4, 2))
    cache_vt = jnp.transpose(cache_v, (0, 1, 3, 4, 2))

    row = lambda a: a.reshape(1, -1)
    member = (np.arange(ATT_DIM)[:, None] // HEAD_DIM == np.arange(LANES)[None, :])
    head_mean = jnp.asarray(member / HEAD_DIM, BF16)
    head_exp = jnp.asarray(member.T, BF16)
    src_lane = np.arange(LANES)
    expand = jnp.asarray((src_lane[:, None] // LSE_LANES_PER_HEAD == np.arange(ATT_DIM)[None, :] // HEAD_DIM)
                         & (src_lane[:, None] % LSE_LANES_PER_HEAD == 0), BF16)
    before = jnp.asarray(np.arange(LANES)[:, None] < np.arange(LANES)[None, :], BF16)
    sel_np = np.zeros((SUBLANES, LANES), np.float32)
    sel_np[0, REC_S1] = sel_np[1, REC_S2] = 1.0
    sel = jnp.asarray(sel_np, BF16)
    perm = jnp.asarray(_row_perm(PROJ_TM), BF16)
    unperm = jnp.asarray(_row_perm(OUT_TM).T, BF16)
    band = [_band_bias(rel_bias, d) for (_, d) in PATTERNS]
    s_bias, s_mult = _sample_tables(rel_bias, w_buf)

    hp = x_prompt.reshape(batch * seq, D_MODEL)
    hs = jnp.swapaxes(x_sample, 0, 1).reshape(n_s, D_MODEL)
    pp_all = p_prompt.reshape(depth, batch * seq, D_PLE)
    ps_all = jnp.swapaxes(p_sample, 1, 2).reshape(depth, n_s, D_PLE)
    new = {k: [] for k in ("kp", "vp", "cp", "ks", "vs", "cs")}
    hist_p = jnp.zeros((batch, SUBLANES, CONV_DIM), F32)

    for l in range(depth):
        w_in_bf = w_in[l].astype(BF16)
        wa_bf = w_out[l, :ATT_DIM].astype(BF16)
        wc_bf = w_out[l, ATT_DIM:].astype(BF16)
        wgate_bf = w_ple_gate[l].astype(BF16)
        wproj_bf = w_ple_proj[l].astype(BF16)
        w_router = jnp.concatenate(
            [w_router_expert[l], w_router_group[l],
             jnp.zeros((D_MODEL, LANES - N_EXPERTS - N_GROUPS), F32)], axis=1).astype(BF16)
        qg, kg = row(jnp.tile(q_gain[l], N_HEADS)), row(jnp.tile(k_gain[l], N_HEADS))
        mix = (row(g_mix[l]), w_in_bf, qg, kg, head_mean, head_exp, conv_w[l], row(g_out_conv[l]))
        moe = (row(g_ffn[l]), w_router, before, sel, w_gate, w_up, w_down, row(g_ple[l]), wgate_bf, wproj_bf)

        q, k, v, k_nat, v_nat, yn, nconv = _inproj(
            hp, hist_p, *mix, perm, tm=PROJ_TM, shift=1, tiles_per_seq=seq // PROJ_TM,
            keep_tiles=keep // PROJ_TM)
        os, lses = [], []
        for bias, (_, d) in zip(band, PATTERNS):
            sub = min(ATTN_SUB, seq // (Q_BLOCK * d))
            o, lse = _attn_pattern(q, k, v, bias, batch=batch, seq=seq, dil=d, sub=sub,
                                   res=min(d, ATTN_SUB // sub))
            os.append(o)
            lses.append(lse)
        hp = _outproj(os, lses, yn, hp, row(g_out_att[l]), wa_bf, wc_bf, expand, unperm, tm=OUT_TM)
        hp = _moe_ple2(hp, pp_all, *moe, tm=PROMPT_TM, te=PROMPT_TE, layer=l)
        new["kp"].append(jnp.transpose(k_nat, (0, 3, 1, 2)))
        new["vp"].append(jnp.transpose(v_nat, (0, 3, 1, 2)))
        new["cp"].append(nconv[:, SUBLANES - 2:])

        hist_s = jnp.swapaxes(state_conv[l], 0, 1).reshape(1, 2 * dec_b, CONV_DIM)
        q, k, v, yn, nconv = _inproj(hs, hist_s, *mix, tm=n_s, shift=dec_b, tiles_per_seq=1)
        bmaj = lambda a: jnp.swapaxes(a.reshape(dec_t, dec_b, N_HEADS, HEAD_DIM), 0, 1)
        qb, kb, vb = bmaj(q), bmaj(k), bmaj(v)
        rows8 = lambda a: jnp.pad(a.reshape(dec_b, dec_t, ATT_DIM), ((0, 0), (0, SUBLANES - dec_t), (0, 0)))
        att = _attn_sample(qb.reshape(dec_b, dec_t, ATT_DIM), rows8(kb), rows8(vb), cache_kt, cache_vt, l,
                           s_bias, s_mult)
        att_tm = jnp.swapaxes(att, 0, 1).reshape(n_s, ATT_DIM)
        hs = _outproj([att_tm], [], yn, hs, row(g_out_att[l]), wa_bf, wc_bf, tm=n_s)
        hs = _moe_ple2(hs, ps_all, *moe, tm=n_s, te=LANES, layer=l)
        new["ks"].append(kb)
        new["vs"].append(vb)
        new["cs"].append(jnp.swapaxes(nconv.reshape(2, dec_b, CONV_DIM), 0, 1))

    y_prompt = hp.reshape(batch, seq, D_MODEL)
    y_sample = jnp.swapaxes(hs.reshape(dec_t, dec_b, D_MODEL), 0, 1)
    st = lambda key: jnp.stack(new[key])
    return (y_prompt, y_sample, st("kp"), st("vp"), st("cp"), st("ks"), st("vs"), st("cs"))
```

```python
import functools

import jax
import jax.numpy as jnp
import numpy as np
from jax import lax
from jax.experimental import pallas as pl
from jax.experimental.pallas import tpu as pltpu

F32 = jnp.float32
BF16 = jnp.bfloat16
HIGHEST = lax.Precision.HIGHEST

D_MODEL = 1024
HEAD_DIM = 64
N_HEADS = 8
ATT_DIM = N_HEADS * HEAD_DIM
CONV_DIM = D_MODEL - ATT_DIM
MIX_IN = 3 * ATT_DIM + 3 * CONV_DIM
PATTERNS = ((128, 1), (512, 4), (2048, 16))
N_KEYS = 128
Q_BLOCK = 128
N_BUCKETS = 32
MAX_DISTANCE = 2048
N_GROUPS = 4
EXPERTS_PER_GROUP = 8
N_EXPERTS = N_GROUPS * EXPERTS_PER_GROUP
D_EXPERT = 256
D_PLE = 256
EPS = 1e-6
NEG = -1e30

LANES = 128
SUBLANES = 8
SLABS = Q_BLOCK // SUBLANES
LSE_LANES_PER_HEAD = LANES // N_HEADS
VMEM_LIMIT = 56 * 1024 * 1024
NT = (((1,), (1,)), ((), ()))


def _cparams(n_axes):
    return pltpu.CompilerParams(dimension_semantics=("arbitrary",) * n_axes,
                                vmem_limit_bytes=VMEM_LIMIT)


def _full(shape):
    n = len(shape)
    return pl.BlockSpec(shape, lambda *_: (0,) * n)


def _rms(x, gain):
    ms = jnp.mean(x * x, axis=-1, keepdims=True)
    return x * lax.rsqrt(ms + EPS) * gain


def _exact_dot(x, e_ref):
    hi = x.astype(BF16)
    r1 = x - hi.astype(F32)
    mid = r1.astype(BF16)
    lo = (r1 - mid.astype(F32)).astype(BF16)
    e = e_ref[...]
    return (jnp.dot(hi, e, preferred_element_type=F32) + jnp.dot(mid, e, preferred_element_type=F32)
            + jnp.dot(lo, e, preferred_element_type=F32))


def _inproj_kernel(*refs, shift, tiles_per_seq, permute):
    (h_ref, gmix_ref, w_ref, qg_ref, kg_ref, hmean_ref, hexp_ref, cw_ref, gconv_ref, hist_ref) = refs[:10]
    if permute:
        perm_ref, q_ref, k_ref, v_ref, kn_ref, vn_ref, yn_ref, nconv_ref, carry_ref = refs[10:]
    else:
        q_ref, k_ref, v_ref, yn_ref, nconv_ref, carry_ref = refs[10:]
    i = pl.program_id(0)
    a = _rms(h_ref[...], gmix_ref[...])
    proj = jnp.dot(a.astype(BF16), w_ref[...], preferred_element_type=F32)
    tm = proj.shape[0]

    def head_norm(t, g):
        ms = _exact_dot(t * t, hmean_ref)
        return t * _exact_dot(lax.rsqrt(ms + EPS), hexp_ref) * g

    q = head_norm(proj[:, 0:ATT_DIM], qg_ref[...])
    k = head_norm(proj[:, ATT_DIM:2 * ATT_DIM], kg_ref[...])
    v = proj[:, 2 * ATT_DIM:3 * ATT_DIM]
    if permute:
        kn_ref[...] = k.T.reshape(N_HEADS, HEAD_DIM, tm)
        vn_ref[...] = v.T.reshape(N_HEADS, HEAD_DIM, tm)
        qkv = jnp.concatenate([q * (HEAD_DIM ** -0.5), k, v], axis=-1).astype(BF16)
        moved = jnp.dot(perm_ref[...], qkv, preferred_element_type=F32)
        q_ref[...] = moved[:, 0:ATT_DIM]
        k_ref[...] = moved[:, ATT_DIM:2 * ATT_DIM]
        v_ref[...] = moved[:, 2 * ATT_DIM:3 * ATT_DIM]
    else:
        q_ref[...] = q
        k_ref[...] = k
        v_ref[...] = v
    c0 = 3 * ATT_DIM
    hc = proj[:, c0:c0 + CONV_DIM]
    gb = proj[:, c0 + CONV_DIM:c0 + 2 * CONV_DIM]
    gc = proj[:, c0 + 2 * CONV_DIM:c0 + 3 * CONV_DIM]
    u = gc * hc

    if shift == 1:
        @pl.when(i % tiles_per_seq == 0)
        def _():
            carry_ref[...] = hist_ref[0]
        h0 = carry_ref[SUBLANES - 2:SUBLANES - 1, :]
        h1 = carry_ref[SUBLANES - 1:SUBLANES, :]
        row = lax.broadcasted_iota(jnp.int32, (tm, 1), 0)
        u1 = jnp.where(row == 0, h1, pltpu.roll(u, 1, 0))
        u2 = jnp.where(row == 0, h0, jnp.where(row == 1, h1, pltpu.roll(u, 2, 0)))
        carry_ref[...] = u[tm - SUBLANES:tm, :]
        nconv_ref[0] = u[tm - SUBLANES:tm, :]
    else:
        hist = hist_ref[0]
        u1 = jnp.concatenate([hist[shift:2 * shift], u[0:tm - shift]], axis=0)
        u2 = jnp.concatenate([hist, u[0:tm - 2 * shift]], axis=0)
        nconv_ref[0] = u[tm - 2 * shift:tm, :]
    conv = cw_ref[0:1, :] * u2 + cw_ref[1:2, :] * u1 + cw_ref[2:3, :] * u
    yn_ref[...] = _rms(gb * conv, gconv_ref[...])


def _inproj(h, hist, g_mix, w_in_bf, q_gain, k_gain, hmean, hexp, conv_w, g_out_conv, perm=None, *,
            tm, shift, tiles_per_seq, keep_tiles=0):
    n = h.shape[0]
    hist_rows = hist.shape[1]
    nseq = hist.shape[0]
    tok = lambda w: pl.BlockSpec((tm, w), lambda i: (i, 0))
    seq3 = lambda r: pl.BlockSpec((1, r, CONV_DIM), lambda i: (i // tiles_per_seq, 0, 0))
    nconv_rows = SUBLANES if shift == 1 else 2 * shift
    att = jax.ShapeDtypeStruct((n, ATT_DIM), F32)
    in_specs = [tok(D_MODEL), _full((1, D_MODEL)), _full((D_MODEL, MIX_IN)), _full((1, ATT_DIM)),
                _full((1, ATT_DIM)), _full((ATT_DIM, LANES)), _full((LANES, ATT_DIM)), _full((3, CONV_DIM)),
                _full((1, CONV_DIM)), seq3(hist_rows)]
    args = [h, g_mix, w_in_bf, q_gain, k_gain, hmean, hexp, conv_w, g_out_conv, hist]
    out_specs = [tok(ATT_DIM)] * 3
    out_shape = [att] * 3
    if perm is not None:
        in_specs.append(_full((tm, tm)))
        args.append(perm)
        first = tiles_per_seq - keep_tiles
        kept = pl.BlockSpec((None, N_HEADS, HEAD_DIM, tm),
                            lambda i: (i // tiles_per_seq, 0, 0, jnp.maximum(i % tiles_per_seq - first, 0)))
        out_specs += [kept, kept]
        out_shape += [jax.ShapeDtypeStruct((nseq, N_HEADS, HEAD_DIM, keep_tiles * tm), F32)] * 2
    out_specs += [tok(CONV_DIM), seq3(nconv_rows)]
    out_shape += [jax.ShapeDtypeStruct((n, CONV_DIM), F32),
                  jax.ShapeDtypeStruct((nseq, nconv_rows, CONV_DIM), F32)]
    return pl.pallas_call(
        functools.partial(_inproj_kernel, shift=shift, tiles_per_seq=tiles_per_seq, permute=perm is not None),
        grid=(n // tm,),
        in_specs=in_specs, out_specs=out_specs, out_shape=out_shape,
        scratch_shapes=[pltpu.VMEM((SUBLANES, CONV_DIM), F32)],
        compiler_params=_cparams(1),
        name="inproj",
    )(*args)


def _attn_block(q, kk, vv, bias):
    lane = lax.broadcasted_iota(jnp.int32, (Q_BLOCK, LANES), 1)
    upper = lane >= HEAD_DIM
    scores = []
    for h in range(N_HEADS):
        j, e = divmod(h, 2)
        qp = q[:, j * LANES:(j + 1) * LANES]
        qm = (jnp.where(upper, qp, 0.0) if e else jnp.where(upper, 0.0, qp)).astype(BF16)
        scores.append(lax.dot_general(qm, kk[:, j * LANES:(j + 1) * LANES], NT, preferred_element_type=F32))
    s = jnp.concatenate(scores, axis=0) + bias
    m = jnp.max(s, axis=-1, keepdims=True)
    p = jnp.exp(s - m)
    den = jnp.sum(p, axis=-1, keepdims=True)
    pb = p.astype(BF16)
    inv = 1.0 / den
    lse = m + jnp.log(den)
    lse_grp = lane // LSE_LANES_PER_HEAD
    lse_tile = jnp.zeros((Q_BLOCK, LANES), F32)
    outs = []
    for j in range(N_HEADS // 2):
        pair = None
        for e in range(2):
            h = 2 * j + e
            rows = slice(h * Q_BLOCK, (h + 1) * Q_BLOCK)
            o = jnp.dot(pb[rows], vv[:, j * LANES:(j + 1) * LANES], preferred_element_type=F32) * inv[rows]
            pair = o if e == 0 else jnp.where(upper, o, pair)
            lse_tile = jnp.where(lse_grp == h, lse[rows], lse_tile)
        outs.append(pair)
    return jnp.concatenate(outs, axis=-1), lse_tile


def _attn_kernel(q_ref, kp_ref, kc_ref, vp_ref, vc_ref, bias_ref, o_ref, lse_ref, kbuf, vbuf, *, sub, res):
    n = pl.program_id(2)
    rows = sub * Q_BLOCK
    for r in range(res):
        kbuf[0:Q_BLOCK, :] = kp_ref[:, r].reshape(Q_BLOCK, ATT_DIM).astype(BF16)
        kbuf[Q_BLOCK:, :] = kc_ref[:, :, r].reshape(rows, ATT_DIM).astype(BF16)
        vbuf[0:Q_BLOCK, :] = vp_ref[:, r].reshape(Q_BLOCK, ATT_DIM).astype(BF16)
        vbuf[Q_BLOCK:, :] = vc_ref[:, :, r].reshape(rows, ATT_DIM).astype(BF16)
        for j in range(sub):
            q = q_ref[j, :, r].reshape(Q_BLOCK, ATT_DIM)
            r0 = j * Q_BLOCK
            first = (n == 0).astype(jnp.int32) if j == 0 else 0
            o, lse = _attn_block(q, kbuf[r0:r0 + 2 * Q_BLOCK, :], vbuf[r0:r0 + 2 * Q_BLOCK, :], bias_ref[first])
            o_ref[j, :, r] = o.reshape(SLABS, SUBLANES, ATT_DIM)
            lse_ref[j, :, r] = lse.reshape(SLABS, SUBLANES, LANES)


def _attn_pattern(q, k, v, bias, *, batch, seq, dil, sub, res):
    nblk = seq // (Q_BLOCK * dil)
    view = lambda t: t.reshape(batch, nblk, SLABS, dil, SUBLANES, t.shape[-1])
    cur = lambda c: pl.BlockSpec((None, sub, SLABS, res, SUBLANES, c), lambda b, r, n: (b, n, 0, r, 0, 0))
    prev = pl.BlockSpec((None, None, SLABS, res, SUBLANES, ATT_DIM),
                        lambda b, r, n: (b, jnp.maximum(n * sub - 1, 0), 0, r, 0, 0))
    o, lse = pl.pallas_call(
        functools.partial(_attn_kernel, sub=sub, res=res),
        grid=(batch, dil // res, nblk // sub),
        in_specs=[cur(ATT_DIM), prev, cur(ATT_DIM), prev, cur(ATT_DIM),
                  _full((2, N_HEADS * Q_BLOCK, 2 * Q_BLOCK))],
        out_specs=[cur(ATT_DIM), cur(LANES)],
        out_shape=[jax.ShapeDtypeStruct((batch, nblk, SLABS, dil, SUBLANES, ATT_DIM), F32),
                   jax.ShapeDtypeStruct((batch, nblk, SLABS, dil, SUBLANES, LANES), F32)],
        scratch_shapes=[pltpu.VMEM(((sub + 1) * Q_BLOCK, ATT_DIM), BF16),
                        pltpu.VMEM(((sub + 1) * Q_BLOCK, ATT_DIM), BF16)],
        compiler_params=_cparams(3),
        name=f"attn_d{dil}",
    )(view(q), view(k), view(k), view(v), view(v), bias)
    return o.reshape(batch * seq, ATT_DIM), lse.reshape(batch * seq, LANES)


SAMPLE_T = 4
NEW_COLS = LANES


def _attn_sample_kernel(q_ref, kt_ref, kn_ref, vt_ref, vn_ref, bias_ref, mult_ref, o_ref):
    rows = SAMPLE_T * N_HEADS
    q4 = q_ref[...] * (HEAD_DIM ** -0.5)
    qt = jnp.concatenate([jnp.broadcast_to(q4[t:t + 1, :], (N_HEADS, ATT_DIM)) for t in range(SAMPLE_T)], axis=0)
    lane_head = lax.broadcasted_iota(jnp.int32, (rows, ATT_DIM), 1) // HEAD_DIM
    row_head = lax.broadcasted_iota(jnp.int32, (rows, ATT_DIM), 0) % N_HEADS
    own = lane_head == row_head
    qbd = jnp.where(own, qt, 0.0).astype(BF16)
    flat = lambda ref: ref[...].reshape(ATT_DIM, ref.shape[-1]).astype(BF16)
    pad = jnp.zeros((NEW_COLS - SUBLANES, ATT_DIM), F32)
    new_rows = lambda ref: jnp.concatenate([ref[...], pad], axis=0).astype(BF16)
    s = jnp.concatenate([jnp.dot(qbd, flat(kt_ref), preferred_element_type=F32),
                         lax.dot_general(qbd, new_rows(kn_ref), NT, preferred_element_type=F32)],
                        axis=-1) + bias_ref[...]
    m = jnp.max(s, axis=-1, keepdims=True)
    p = jnp.exp(s - m) * mult_ref[...]
    den = jnp.sum(p, axis=-1, keepdims=True)
    pb = p.astype(BF16)
    w_buf = kt_ref.shape[-1]
    acc = (lax.dot_general(pb[:, :w_buf], flat(vt_ref), NT, preferred_element_type=F32)
           + jnp.dot(pb[:, w_buf:], new_rows(vn_ref), preferred_element_type=F32))
    acc = jnp.where(own, acc / den, 0.0)
    for t in range(SAMPLE_T):
        o_ref[t:t + 1, :] = jnp.sum(acc[t * N_HEADS:(t + 1) * N_HEADS, :], axis=0, keepdims=True)


def _attn_sample(q, k_new, v_new, cache_kt, cache_vt, layer, bias, mult):
    nb = q.shape[0]
    w_buf = cache_kt.shape[-1]
    tok = pl.BlockSpec((None, SAMPLE_T, ATT_DIM), lambda b: (b, 0, 0))
    new = pl.BlockSpec((None, SUBLANES, ATT_DIM), lambda b: (b, 0, 0))
    old = pl.BlockSpec((None, None, N_HEADS, HEAD_DIM, w_buf), lambda b: (layer, b, 0, 0, 0))
    tbl = _full((SAMPLE_T * N_HEADS, w_buf + NEW_COLS))
    return pl.pallas_call(
        _attn_sample_kernel,
        grid=(nb,),
        in_specs=[tok, old, new, old, new, tbl, tbl],
        out_specs=tok,
        out_shape=jax.ShapeDtypeStruct((nb, SAMPLE_T, ATT_DIM), F32),
        compiler_params=_cparams(1),
        name="attn_sample",
    )(q, cache_kt, k_new, cache_vt, v_new, bias, mult)


def _split_dot(x, e_ref):
    hi = x.astype(BF16)
    lo = (x - hi.astype(F32)).astype(BF16)
    return (jnp.dot(hi, e_ref[...], preferred_element_type=F32)
            + jnp.dot(lo, e_ref[...], preferred_element_type=F32))


def _outproj_kernel(*refs, n_pat):
    mix = n_pat > 1
    n_lse = n_pat if mix else 0
    o_refs = refs[0:n_pat]
    l_refs = refs[n_pat:n_pat + n_lse]
    rest = refs[n_pat + n_lse:]
    if mix:
        yn_ref, h_ref, gatt_ref, exp_ref, unperm_ref, wa_ref, wc_ref, out_ref = rest
        lses = [r[...] for r in l_refs]
        top = functools.reduce(jnp.maximum, lses)
        ws = [jnp.exp(l - top) for l in lses]
        tot = functools.reduce(lambda a, b: a + b, ws)
        att = None
        for w, o_ref in zip(ws, o_refs):
            term = _split_dot(w / tot, exp_ref) * o_ref[...]
            att = term if att is None else att + term
        att_bf = jnp.dot(unperm_ref[...], _rms(att, gatt_ref[...]).astype(BF16),
                         preferred_element_type=F32).astype(BF16)
    else:
        yn_ref, h_ref, gatt_ref, wa_ref, wc_ref, out_ref = rest
        att_bf = _rms(o_refs[0][...], gatt_ref[...]).astype(BF16)
    y = (jnp.dot(att_bf, wa_ref[...], preferred_element_type=F32)
         + jnp.dot(yn_ref[...].astype(BF16), wc_ref[...], preferred_element_type=F32))
    out_ref[...] = h_ref[...] + y


def _outproj(os, lses, yn, h, g_att, wa_bf, wc_bf, expand=None, unperm=None, *, tm):
    n = h.shape[0]
    n_pat = len(os)
    tok = lambda w: pl.BlockSpec((tm, w), lambda i: (i, 0))
    in_specs = [tok(ATT_DIM)] * n_pat + [tok(LANES)] * len(lses) + [tok(CONV_DIM), tok(D_MODEL), _full((1, ATT_DIM))]
    args = [*os, *lses, yn, h, g_att]
    if n_pat > 1:
        in_specs += [_full((LANES, ATT_DIM)), _full((tm, tm))]
        args += [expand, unperm]
    in_specs += [_full((ATT_DIM, D_MODEL)), _full((CONV_DIM, D_MODEL))]
    args += [wa_bf, wc_bf]
    return pl.pallas_call(
        functools.partial(_outproj_kernel, n_pat=n_pat),
        grid=(n // tm,),
        in_specs=in_specs,
        out_specs=tok(D_MODEL),
        out_shape=jax.ShapeDtypeStruct((n, D_MODEL), F32),
        compiler_params=_cparams(1),
        name="outproj",
    )(*args)


ROUTE_I1, ROUTE_I2, ROUTE_R1, ROUTE_R2, ROUTE_W1, ROUTE_W2 = range(6)
GROUP_LANE0 = N_EXPERTS


def _route_kernel(h_ref, g_ref, wr_ref, route_ref, cnt_ref, carry_ref):
    i = pl.program_id(0)

    @pl.when(i == 0)
    def _():
        carry_ref[...] = jnp.zeros_like(carry_ref)

    m = _rms(h_ref[...], g_ref[...])
    logits = jnp.dot(m.astype(BF16), wr_ref[...], preferred_element_type=F32)
    tm = logits.shape[0]
    lane_i = lax.broadcasted_iota(jnp.int32, (tm, LANES), 1)
    lane = lane_i.astype(F32)
    big = jnp.float32(4 * LANES)

    is_g = jnp.logical_and(lane_i >= GROUP_LANE0, lane_i < GROUP_LANE0 + N_GROUPS)
    gl = jnp.where(is_g, logits, NEG)
    gmax = jnp.max(gl, axis=-1, keepdims=True)
    g_w = 1.0 / jnp.sum(jnp.where(is_g, jnp.exp(gl - gmax), 0.0), axis=-1, keepdims=True)
    g_sel = jnp.min(jnp.where(gl == gmax, lane - GROUP_LANE0, big), axis=-1, keepdims=True)

    grp_of_lane = (lane_i // EXPERTS_PER_GROUP).astype(F32)
    in_grp = jnp.logical_and(lane_i < N_EXPERTS, grp_of_lane == g_sel)
    el = jnp.where(in_grp, logits, NEG)
    t1 = jnp.max(el, axis=-1, keepdims=True)
    i1 = jnp.min(jnp.where(el == t1, lane, big), axis=-1, keepdims=True)
    el2 = jnp.where(lane == i1, NEG, el)
    t2 = jnp.max(el2, axis=-1, keepdims=True)
    i2 = jnp.min(jnp.where(el2 == t2, lane, big), axis=-1, keepdims=True)
    e2 = jnp.exp(t2 - t1)
    w1 = g_w / (1.0 + e2)
    w2 = g_w * e2 / (1.0 + e2)

    hit1 = lane == i1
    hit2 = lane == i2
    c = jnp.where(jnp.logical_or(hit1, hit2), 1.0, 0.0)
    rr = lax.broadcasted_iota(jnp.int32, (tm, tm), 0)
    cc = lax.broadcasted_iota(jnp.int32, (tm, tm), 1)
    lower = jnp.where(rr > cc, 1.0, 0.0).astype(BF16)
    before = jnp.dot(lower, c.astype(BF16), preferred_element_type=F32) + carry_ref[0:1, :]
    r1 = jnp.sum(jnp.where(hit1, before, 0.0), axis=-1, keepdims=True)
    r2 = jnp.sum(jnp.where(hit2, before, 0.0), axis=-1, keepdims=True)
    total = carry_ref[0:1, :] + jnp.sum(c, axis=0, keepdims=True)
    carry_ref[...] = jnp.broadcast_to(total, carry_ref.shape)
    cnt_ref[...] = jnp.broadcast_to(total, cnt_ref.shape)

    rec = jnp.zeros((tm, LANES), F32)
    for idx, val in ((ROUTE_I1, i1), (ROUTE_I2, i2), (ROUTE_R1, r1), (ROUTE_R2, r2), (ROUTE_W1, w1), (ROUTE_W2, w2)):
        rec = jnp.where(lane_i == idx, val, rec)
    route_ref[...] = rec


def _route(h, g_ffn, w_router, *, tm):
    n = h.shape[0]
    tok = lambda w: pl.BlockSpec((tm, w), lambda i: (i, 0))
    return pl.pallas_call(
        _route_kernel,
        grid=(n // tm,),
        in_specs=[tok(D_MODEL), _full((1, D_MODEL)), _full((D_MODEL, LANES))],
        out_specs=[tok(LANES), _full((SUBLANES, LANES))],
        out_shape=[jax.ShapeDtypeStruct((n, LANES), F32), jax.ShapeDtypeStruct((SUBLANES, LANES), F32)],
        scratch_shapes=[pltpu.VMEM((SUBLANES, LANES), F32)],
        compiler_params=_cparams(1),
        name="route",
    )(h, g_ffn, w_router)


def _dispatch_kernel(pos_ref, h_ref, g_ref, xs_ref, mbuf, sem):
    tm = mbuf.shape[0]
    mbuf[...] = _rms(h_ref[...], g_ref[...])

    def row_copy(t, p):
        return pltpu.make_async_copy(mbuf.at[pl.ds(t, 1), :], xs_ref.at[pl.ds(p, 1), :], sem)

    def issue(t, carry):
        row_copy(t, pos_ref[0, 0, 2 * t]).start()
        row_copy(t, pos_ref[0, 0, 2 * t + 1]).start()
        return carry

    lax.fori_loop(0, tm, issue, 0)

    def drain(t, carry):
        row_copy(0, 0).wait()
        row_copy(0, 0).wait()
        return carry

    lax.fori_loop(0, tm, drain, 0)


def _dispatch(h, g_ffn, pos, n_rows, *, tm):
    n = h.shape[0]
    return pl.pallas_call(
        _dispatch_kernel,
        grid=(n // tm,),
        in_specs=[pl.BlockSpec((1, 1, 2 * tm), lambda i: (i, 0, 0), memory_space=pltpu.SMEM),
                  pl.BlockSpec((tm, D_MODEL), lambda i: (i, 0)), _full((1, D_MODEL))],
        out_specs=pl.BlockSpec(memory_space=pl.ANY),
        out_shape=jax.ShapeDtypeStruct((n_rows, D_MODEL), F32),
        scratch_shapes=[pltpu.VMEM((tm, D_MODEL), F32), pltpu.SemaphoreType.DMA(())],
        compiler_params=pltpu.CompilerParams(dimension_semantics=("arbitrary",), vmem_limit_bytes=VMEM_LIMIT,
                                             has_side_effects=True),
        name="dispatch",
    )(pos, h, g_ffn)


def _expert_kernel(te_ref, tv_ref, x_ref, wg_ref, wu_ref, wd_ref, y_ref, wg_bf, wu_bf, wd_bf):
    i = pl.program_id(0)
    valid = tv_ref[i]
    changed = jnp.logical_or(i == 0, te_ref[i] != te_ref[jnp.maximum(i - 1, 0)])

    @pl.when(jnp.logical_and(changed, valid > 0))
    def _():
        wg_bf[...] = wg_ref[0].astype(BF16)
        wu_bf[...] = wu_ref[0].astype(BF16)
        wd_bf[...] = wd_ref[0].astype(BF16)

    @pl.when(valid > 0)
    def _():
        te = x_ref.shape[0]
        row = lax.broadcasted_iota(jnp.int32, (te, 1), 0)
        x = jnp.where(row < valid, x_ref[...], 0.0).astype(BF16)
        hg = jnp.dot(x, wg_bf[...], preferred_element_type=F32)
        hu = jnp.dot(x, wu_bf[...], preferred_element_type=F32)
        hid = (hg * jax.nn.sigmoid(hg)) * hu
        y_ref[...] = jnp.dot(hid.astype(BF16), wd_bf[...], preferred_element_type=F32)

    @pl.when(valid <= 0)
    def _():
        y_ref[...] = jnp.zeros_like(y_ref)


def _experts(xs, tile_expert, tile_valid, w_gate, w_up, w_down, *, te):
    n_rows = xs.shape[0]
    wspec = lambda a, b: pl.BlockSpec((1, a, b), lambda i, e, v: (e[i], 0, 0))
    grid_spec = pltpu.PrefetchScalarGridSpec(
        num_scalar_prefetch=2,
        grid=(n_rows // te,),
        in_specs=[pl.BlockSpec((te, D_MODEL), lambda i, e, v: (i, 0)),
                  wspec(D_MODEL, D_EXPERT), wspec(D_MODEL, D_EXPERT), wspec(D_EXPERT, D_MODEL)],
        out_specs=pl.BlockSpec((te, D_MODEL), lambda i, e, v: (i, 0)),
        scratch_shapes=[pltpu.VMEM((D_MODEL, D_EXPERT), BF16), pltpu.VMEM((D_MODEL, D_EXPERT), BF16),
                        pltpu.VMEM((D_EXPERT, D_MODEL), BF16)])
    return pl.pallas_call(
        _expert_kernel,
        grid_spec=grid_spec,
        out_shape=jax.ShapeDtypeStruct((n_rows, D_MODEL), F32),
        compiler_params=_cparams(1),
        name="experts",
    )(tile_expert, tile_valid, xs, w_gate, w_up, w_down)


def _combine_kernel(pos_ref, h_ref, route_ref, ys_ref, p_ref, gple_ref, wgate_ref, wproj_ref, out_ref,
                    y0, y1, sem):
    tm = y0.shape[0]

    def row_copy(p, dst, t):
        return pltpu.make_async_copy(ys_ref.at[pl.ds(p, 1), :], dst.at[pl.ds(t, 1), :], sem)

    def issue(t, carry):
        row_copy(pos_ref[0, 0, 2 * t], y0, t).start()
        row_copy(pos_ref[0, 0, 2 * t + 1], y1, t).start()
        return carry

    lax.fori_loop(0, tm, issue, 0)

    def drain(t, carry):
        row_copy(0, y0, 0).wait()
        row_copy(0, y1, 0).wait()
        return carry

    lax.fori_loop(0, tm, drain, 0)

    rec = route_ref[...]
    w1 = rec[:, ROUTE_W1:ROUTE_W1 + 1]
    w2 = rec[:, ROUTE_W2:ROUTE_W2 + 1]
    h2 = h_ref[...] + w1 * y0[...] + w2 * y1[...]
    gate = jax.nn.sigmoid(jnp.dot(_rms(h2, gple_ref[...]).astype(BF16), wgate_ref[...], preferred_element_type=F32))
    ple = jnp.dot(p_ref[...].astype(BF16), wproj_ref[...], preferred_element_type=F32)
    out_ref[...] = h2 + ple * gate


def _combine(h, route, pos, ys, p, g_ple, wgate_bf, wproj_bf, *, tm):
    n = h.shape[0]
    tok = lambda w: pl.BlockSpec((tm, w), lambda i: (i, 0))
    return pl.pallas_call(
        _combine_kernel,
        grid=(n // tm,),
        in_specs=[pl.BlockSpec((1, 1, 2 * tm), lambda i: (i, 0, 0), memory_space=pltpu.SMEM),
                  tok(D_MODEL), tok(LANES), pl.BlockSpec(memory_space=pl.ANY), tok(D_PLE),
                  _full((1, D_MODEL)), _full((D_MODEL, D_MODEL)), _full((D_PLE, D_MODEL))],
        out_specs=tok(D_MODEL),
        out_shape=jax.ShapeDtypeStruct((n, D_MODEL), F32),
        scratch_shapes=[pltpu.VMEM((tm, D_MODEL), F32), pltpu.VMEM((tm, D_MODEL), F32),
                        pltpu.SemaphoreType.DMA(())],
        compiler_params=_cparams(1),
        name="combine",
    )(pos, h, route, ys, p, g_ple, wgate_bf, wproj_bf)


def _lookup(table, idx, size):
    hit = idx[..., None] == jnp.arange(size, dtype=jnp.int32)
    return jnp.sum(jnp.where(hit, table, 0), axis=-1)


def _moe_ple(h, p, g_ffn, w_router, w_gate, w_up, w_down, g_ple, wgate_bf, wproj_bf, *, tm, te):
    n = h.shape[0]
    route, counts = _route(h, g_ffn, w_router, tm=tm)
    counts = counts[0, :N_EXPERTS].astype(jnp.int32)
    padded = ((counts + te - 1) // te) * te
    ends = jnp.cumsum(padded)
    offs = ends - padded
    ids = route[:, ROUTE_I1:ROUTE_I2 + 1].astype(jnp.int32)
    ranks = route[:, ROUTE_R1:ROUTE_R2 + 1].astype(jnp.int32)
    pos = (_lookup(offs, ids, N_EXPERTS) + ranks).reshape(n // tm, 1, 2 * tm)
    n_tiles = (2 * n) // te + N_EXPERTS
    starts = jnp.arange(n_tiles, dtype=jnp.int32) * te
    tile_expert = jnp.minimum(jnp.sum((starts[:, None] >= ends[None, :]).astype(jnp.int32), axis=-1), N_EXPERTS - 1)
    tile_valid = jnp.clip(_lookup(counts, tile_expert, N_EXPERTS)
                          - (starts - _lookup(offs, tile_expert, N_EXPERTS)), 0, te).astype(jnp.int32)
    xs = _dispatch(h, g_ffn, pos, n_tiles * te, tm=tm)
    ys = _experts(xs, tile_expert, tile_valid, w_gate, w_up, w_down, te=te)
    return _combine(h, route, pos, ys, p, g_ple, wgate_bf, wproj_bf, tm=tm)


REC_E1, REC_E2, REC_S1, REC_S2, REC_W1, REC_W2 = range(6)
PACKED = D_MODEL // 2
ROW_W = PACKED + LANES
LIST_LANES = LANES
LIST_COUNT = LIST_LANES - 1


def _slots(tm):
    need = 2 * tm + (SUBLANES - 1) * N_EXPERTS + SUBLANES
    return -(-need // LANES) * LANES


def _pieces(x):
    hi = x.astype(BF16)
    r1 = x - hi.astype(F32)
    mid = r1.astype(BF16)
    return hi, mid, (r1 - mid.astype(F32)).astype(BF16)


def _sort_kernel(h_ref, g_ref, wr_ref, before_ref, sel_ref, xs_ref, rec_ref, cnt_ref):
    m = _rms(h_ref[...], g_ref[...])
    logits = jnp.dot(m.astype(BF16), wr_ref[...], preferred_element_type=F32)
    tm = logits.shape[0]
    slots = xs_ref.shape[0]
    lane_i = lax.broadcasted_iota(jnp.int32, (tm, LANES), 1)
    lane = lane_i.astype(F32)
    big = jnp.float32(4 * LANES)

    is_g = jnp.logical_and(lane_i >= GROUP_LANE0, lane_i < GROUP_LANE0 + N_GROUPS)
    gl = jnp.where(is_g, logits, NEG)
    gmax = jnp.max(gl, axis=-1, keepdims=True)
    g_w = 1.0 / jnp.sum(jnp.where(is_g, jnp.exp(gl - gmax), 0.0), axis=-1, keepdims=True)
    g_sel = jnp.min(jnp.where(gl == gmax, lane - GROUP_LANE0, big), axis=-1, keepdims=True)

    grp_of_lane = (lane_i // EXPERTS_PER_GROUP).astype(F32)
    in_grp = jnp.logical_and(lane_i < N_EXPERTS, grp_of_lane == g_sel)
    el = jnp.where(in_grp, logits, NEG)
    t1 = jnp.max(el, axis=-1, keepdims=True)
    e1 = jnp.min(jnp.where(el == t1, lane, big), axis=-1, keepdims=True)
    el2 = jnp.where(lane == e1, NEG, el)
    t2 = jnp.max(el2, axis=-1, keepdims=True)
    e2 = jnp.min(jnp.where(el2 == t2, lane, big), axis=-1, keepdims=True)
    ex = jnp.exp(t2 - t1)
    w1 = g_w / (1.0 + ex)
    w2 = g_w * ex / (1.0 + ex)

    hit1 = lane == e1
    hit2 = lane == e2
    c = jnp.where(hit1, 1.0, jnp.where(hit2, 1.0, 0.0))
    rr = lax.broadcasted_iota(jnp.int32, (tm, tm), 0)
    cc = lax.broadcasted_iota(jnp.int32, (tm, tm), 1)
    lower = jnp.where(rr > cc, 1.0, 0.0).astype(BF16)
    rank = jnp.dot(lower, c.astype(BF16), preferred_element_type=F32)
    cnt = jnp.sum(c, axis=0, keepdims=True)
    chunks = jnp.floor((cnt + (SUBLANES - 1)) * (1.0 / SUBLANES))
    start = SUBLANES * jnp.dot(jnp.broadcast_to(chunks, (SUBLANES, LANES)).astype(BF16), before_ref[...],
                               preferred_element_type=F32)[0:1, :]
    slot_of = rank + start
    s1 = jnp.sum(jnp.where(hit1, slot_of, 0.0), axis=-1, keepdims=True)
    s2 = jnp.sum(jnp.where(hit2, slot_of, 0.0), axis=-1, keepdims=True)

    rec = jnp.zeros((tm, LANES), F32)
    for idx, val in ((REC_E1, e1), (REC_E2, e2), (REC_S1, s1), (REC_S2, s2), (REC_W1, w1), (REC_W2, w2)):
        rec = jnp.where(lane_i == idx, val, rec)
    rec_ref[...] = rec
    cnt_ref[0] = jnp.broadcast_to(cnt, (SUBLANES, LANES))

    rec_parts = _pieces(rec)
    srow = sum(lax.dot_general(sel_ref[...], part, NT, preferred_element_type=F32) for part in rec_parts)
    slot_id = lax.broadcasted_iota(jnp.int32, (slots, tm), 0).astype(F32)
    place = jnp.where(slot_id == srow[0:1, :], 1.0, jnp.where(slot_id == srow[1:2, :], 1.0, 0.0)).astype(BF16)
    payload = jnp.concatenate([m.astype(BF16), *rec_parts], axis=-1)
    moved = jnp.dot(place, payload, preferred_element_type=F32)
    info = (moved[:, D_MODEL:D_MODEL + LANES] + moved[:, D_MODEL + LANES:D_MODEL + 2 * LANES]
            + moved[:, D_MODEL + 2 * LANES:])
    xs_ref[...] = jnp.concatenate([_pack_bf16(moved[:, :D_MODEL]), pltpu.bitcast(info, jnp.uint32)], axis=-1)


def _sort(h, g_ffn, w_router, before, sel, *, tm):
    n = h.shape[0]
    slots = _slots(tm)
    tok = lambda w: pl.BlockSpec((tm, w), lambda i: (i, 0))
    return pl.pallas_call(
        _sort_kernel,
        grid=(n // tm,),
        in_specs=[tok(D_MODEL), _full((1, D_MODEL)), _full((D_MODEL, LANES)), _full((LANES, LANES)),
                  _full((SUBLANES, LANES))],
        out_specs=[pl.BlockSpec((slots, ROW_W), lambda i: (i, 0)), tok(LANES),
                   pl.BlockSpec((1, SUBLANES, LANES), lambda i: (i, 0, 0))],
        out_shape=[jax.ShapeDtypeStruct((n // tm * slots, ROW_W), jnp.uint32), jax.ShapeDtypeStruct((n, LANES), F32),
                   jax.ShapeDtypeStruct((n // tm, SUBLANES, LANES), F32)],
        compiler_params=_cparams(1),
        name="moe_sort",
    )(h, g_ffn, w_router, before, sel)


def _chunk_copy(src_hbm, row, dst, c, sem):
    if not isinstance(row, int):
        row = pl.multiple_of(row, SUBLANES)
    return pltpu.make_async_copy(src_hbm.at[pl.ds(row, SUBLANES), :],
                                 dst.at[pl.ds(c * SUBLANES, SUBLANES), :], sem)


def _expert_kernel2(te_ref, tv_ref, src_ref, nxt_ref, xs_ref, wg_ref, wu_ref, wd_ref, y_ref,
                    xbuf, sem, wg_bf, wu_bf, wd_bf):
    i = pl.program_id(0)
    n = pl.num_programs(0)
    te = xbuf.shape[1]
    slot = i % 2

    def fetch(list_ref, to_slot):
        for c in range(te // SUBLANES):
            _chunk_copy(xs_ref, list_ref[0, 0, c], xbuf.at[to_slot], c, sem.at[to_slot]).start()

    @pl.when(jnp.logical_and(i == 0, tv_ref[0] > 0))
    def _():
        fetch(src_ref, 0)

    nxt = jnp.minimum(i + 1, n - 1)

    @pl.when(jnp.logical_and(i + 1 < n, tv_ref[nxt] > 0))
    def _():
        fetch(nxt_ref, 1 - slot)

    valid = tv_ref[i]
    changed = jnp.logical_or(i == 0, te_ref[i] != te_ref[jnp.maximum(i - 1, 0)])

    @pl.when(jnp.logical_and(changed, valid > 0))
    def _():
        wg_bf[...] = wg_ref[0].astype(BF16)
        wu_bf[...] = wu_ref[0].astype(BF16)
        wd_bf[...] = wd_ref[0].astype(BF16)

    @pl.when(valid > 0)
    def _():
        pltpu.make_async_copy(xbuf.at[slot], xbuf.at[slot], sem.at[slot]).wait()
        rows = xbuf[slot]
        x = _unpack_bf16(rows[:, :PACKED])
        info = pltpu.bitcast(rows[:, PACKED:], F32)
        mine = info[:, REC_E1:REC_E1 + 1] == te_ref[i].astype(F32)
        gate = jnp.where(mine, info[:, REC_W1:REC_W1 + 1], info[:, REC_W2:REC_W2 + 1])
        hg = jnp.dot(x, wg_bf[...], preferred_element_type=F32)
        hu = jnp.dot(x, wu_bf[...], preferred_element_type=F32)
        hid = (hg * jax.nn.sigmoid(hg)) * hu * gate
        y = jnp.dot(hid.astype(BF16), wd_bf[...], preferred_element_type=F32)
        y_ref[...] = _pack_bf16(y)

    @pl.when(valid <= 0)
    def _():
        y_ref[...] = jnp.zeros_like(y_ref)


def _pack_bf16(x):
    w = x.shape[-1] // 2
    bits = lambda a: pltpu.bitcast(a.astype(BF16).astype(F32), jnp.uint32)
    return jnp.bitwise_or(lax.shift_right_logical(bits(x[:, :w]), jnp.uint32(16)),
                          jnp.bitwise_and(bits(x[:, w:]), jnp.uint32(0xFFFF0000)))


def _unpack_bf16(words):
    as_f32 = lambda a: pltpu.bitcast(a, F32)
    return jnp.concatenate([as_f32(lax.shift_left(words, jnp.uint32(16))),
                            as_f32(jnp.bitwise_and(words, jnp.uint32(0xFFFF0000)))], axis=-1).astype(BF16)


def _experts2(xs, tile_expert, tile_valid, src, w_gate, w_up, w_down, *, te, layer):
    n_tiles = tile_expert.shape[0]
    wspec = lambda a, b: pl.BlockSpec((None, 1, a, b), lambda i, e, v: (layer, e[i], 0, 0))
    lst = lambda shift: pl.BlockSpec((1, 1, LIST_LANES),
                                     lambda i, e, v: (jnp.minimum(i + shift, n_tiles - 1), 0, 0),
                                     memory_space=pltpu.SMEM)
    grid_spec = pltpu.PrefetchScalarGridSpec(
        num_scalar_prefetch=2,
        grid=(n_tiles,),
        in_specs=[lst(0), lst(1), pl.BlockSpec(memory_space=pl.ANY),
                  wspec(D_MODEL, D_EXPERT), wspec(D_MODEL, D_EXPERT), wspec(D_EXPERT, D_MODEL)],
        out_specs=pl.BlockSpec((te, PACKED), lambda i, e, v: (i, 0)),
        scratch_shapes=[pltpu.VMEM((2, te, ROW_W), jnp.uint32), pltpu.SemaphoreType.DMA((2,)),
                        pltpu.VMEM((D_MODEL, D_EXPERT), BF16), pltpu.VMEM((D_MODEL, D_EXPERT), BF16),
                        pltpu.VMEM((D_EXPERT, D_MODEL), BF16)])
    return pl.pallas_call(
        _expert_kernel2,
        grid_spec=grid_spec,
        out_shape=jax.ShapeDtypeStruct((n_tiles * te, PACKED), jnp.uint32),
        compiler_params=_cparams(1),
        name="experts",
    )(tile_expert, tile_valid, src, src, xs, w_gate, w_up, w_down)


def _combine_kernel2(dst_ref, nxt_ref, h_ref, rec_ref, ys_ref, p_ref, gple_ref, wgate_ref, wproj_ref, out_ref,
                     ybuf, sem):
    i = pl.program_id(0)
    n = pl.num_programs(0)
    slot = i % 2
    tm = h_ref.shape[0]
    slots = ybuf.shape[1]

    def fetch(list_ref, to_slot):
        def body(c, carry):
            _chunk_copy(ys_ref, list_ref[0, 0, c], ybuf.at[to_slot], c, sem.at[to_slot]).start()
            return carry
        lax.fori_loop(0, list_ref[0, 0, LIST_COUNT], body, 0)

    @pl.when(i == 0)
    def _():
        ybuf[...] = jnp.zeros_like(ybuf)
        fetch(dst_ref, 0)

    @pl.when(i + 1 < n)
    def _():
        fetch(nxt_ref, 1 - slot)

    def drain(c, carry):
        _chunk_copy(ys_ref, 0, ybuf.at[slot], 0, sem.at[slot]).wait()
        return carry

    lax.fori_loop(0, dst_ref[0, 0, LIST_COUNT], drain, 0)

    rec = rec_ref[...]
    slot_id = lax.broadcasted_iota(jnp.int32, (tm, slots), 1).astype(F32)
    back = jnp.where(slot_id == rec[:, REC_S1:REC_S1 + 1], 1.0,
                     jnp.where(slot_id == rec[:, REC_S2:REC_S2 + 1], 1.0, 0.0)).astype(BF16)
    h2 = h_ref[...] + jnp.dot(back, _unpack_bf16(ybuf[slot]), preferred_element_type=F32)
    gate = jax.nn.sigmoid(jnp.dot(_rms(h2, gple_ref[...]).astype(BF16), wgate_ref[...], preferred_element_type=F32))
    ple = jnp.dot(p_ref[...].astype(BF16), wproj_ref[...], preferred_element_type=F32)
    out_ref[...] = h2 + ple * gate


def _combine2(h, rec, dst, ys, p, g_ple, wgate_bf, wproj_bf, *, tm, layer):
    n = h.shape[0]
    n_tok = n // tm
    tok = lambda w: pl.BlockSpec((tm, w), lambda i: (i, 0))
    p_spec = pl.BlockSpec((None, tm, D_PLE), lambda i: (layer, i, 0))
    lst = lambda shift: pl.BlockSpec((1, 1, LIST_LANES), lambda i: (jnp.minimum(i + shift, n_tok - 1), 0, 0),
                                     memory_space=pltpu.SMEM)
    return pl.pallas_call(
        _combine_kernel2,
        grid=(n_tok,),
        in_specs=[lst(0), lst(1), tok(D_MODEL), tok(LANES), pl.BlockSpec(memory_space=pl.ANY), p_spec,
                  _full((1, D_MODEL)), _full((D_MODEL, D_MODEL)), _full((D_PLE, D_MODEL))],
        out_specs=tok(D_MODEL),
        out_shape=jax.ShapeDtypeStruct((n, D_MODEL), F32),
        scratch_shapes=[pltpu.VMEM((2, _slots(tm), PACKED), jnp.uint32), pltpu.SemaphoreType.DMA((2,))],
        compiler_params=_cparams(1),
        name="combine",
    )(dst, dst, h, rec, ys, p, g_ple, wgate_bf, wproj_bf)


def _excl_cumsum(x, axis):
    return jnp.cumsum(x, axis=axis) - x


def _chunk_plan(cnt, *, tm, te):
    n_tok = cnt.shape[0]
    slots = _slots(tm)
    per_tile = te // SUBLANES
    chunks = (cnt + SUBLANES - 1) // SUBLANES
    run0 = _excl_cumsum(chunks, 1)
    n_run = jnp.sum(chunks, axis=1)
    seg0 = _excl_cumsum(chunks, 0)
    total = jnp.sum(chunks, axis=0)
    region = ((total + per_tile - 1) // per_tile) * per_tile
    reg_end = jnp.cumsum(region)
    reg0 = reg_end - region

    n_tiles = -(-(2 * n_tok * tm + (SUBLANES - 1) * N_EXPERTS * n_tok) // te) + N_EXPERTS
    t0 = jnp.arange(n_tiles, dtype=jnp.int32) * per_tile
    tile_expert = jnp.minimum(jnp.sum((t0[:, None] >= reg_end[None, :]).astype(jnp.int32), axis=1), N_EXPERTS - 1)
    pick = tile_expert[:, None] == jnp.arange(N_EXPERTS, dtype=jnp.int32)[None, :]
    of_tile = lambda v: jnp.sum(jnp.where(pick, v[None, :], 0), axis=1)
    tile_valid = jnp.clip(of_tile(total) - (t0 - of_tile(reg0)), 0, per_tile).astype(jnp.int32)

    q = (t0 - of_tile(reg0))[:, None] + jnp.arange(per_tile, dtype=jnp.int32)[None, :]
    col_of_tile = lambda v: jnp.sum(jnp.where(pick[:, None, :], v[None, :, :], 0), axis=2)
    seg0_t, seg1_t, run0_t = col_of_tile(seg0), col_of_tile(seg0 + chunks), col_of_tile(run0)
    holds = (q[:, :, None] >= seg0_t[:, None, :]) & (q[:, :, None] < seg1_t[:, None, :])
    local = run0_t[:, None, :] + q[:, :, None] - seg0_t[:, None, :]
    row = jnp.arange(n_tok, dtype=jnp.int32)[None, None, :] * slots + SUBLANES * local
    src = jnp.sum(jnp.where(holds, row, 0), axis=2)
    src = jnp.where(jnp.any(holds, axis=2), src, slots - SUBLANES)
    src = jnp.pad(src, ((0, 0), (0, LIST_LANES - per_tile))).reshape(n_tiles, 1, LIST_LANES).astype(jnp.int32)

    j = jnp.arange(LIST_LANES, dtype=jnp.int32)[None, :, None]
    inside = (j >= run0[:, None, :]) & (j < (run0 + chunks)[:, None, :])
    base = (reg0[None, :] + seg0 - run0)[:, None, :]
    dst = SUBLANES * jnp.sum(jnp.where(inside, base + j, 0), axis=2)
    dst = dst.at[:, LIST_COUNT].set(n_run).reshape(n_tok, 1, LIST_LANES).astype(jnp.int32)
    return tile_expert.astype(jnp.int32), tile_valid, src, dst


def _moe_ple2(h, p, g_ffn, w_router, before, sel, w_gate, w_up, w_down, g_ple, wgate_bf, wproj_bf, *,
              tm, te, layer):
    xs, rec, cnt = _sort(h, g_ffn, w_router, before, sel, tm=tm)
    tile_expert, tile_valid, src, dst = _chunk_plan(cnt[:, 0, :N_EXPERTS].astype(jnp.int32), tm=tm, te=te)
    ys = _experts2(xs, tile_expert, tile_valid, src, w_gate, w_up, w_down, te=te, layer=layer)
    return _combine2(h, rec, dst, ys, p, g_ple, wgate_bf, wproj_bf, tm=tm, layer=layer)


def _bucket_np(dist):
    max_exact = N_BUCKETS // 2
    d_f = np.maximum(dist, 1).astype(np.float32)
    large = max_exact + (np.log(d_f / np.float32(max_exact)) / np.float32(np.log(MAX_DISTANCE / max_exact))
                         * np.float32(N_BUCKETS - max_exact)).astype(np.int32)
    large = np.minimum(large, N_BUCKETS - 1)
    return np.where(dist < max_exact, dist, large).astype(np.int32)


def _bias_from_buckets(rel_bias, bucket, valid):
    onehot = (jnp.asarray(bucket)[..., None] == jnp.arange(N_BUCKETS, dtype=jnp.int32)).astype(F32)
    bias = jnp.einsum("...k,kh->h...", onehot, rel_bias.astype(F32), precision=HIGHEST)
    return jnp.where(jnp.asarray(valid)[None], bias, NEG)


def _block_order(dil):
    g = np.arange(Q_BLOCK) // SUBLANES
    j = np.arange(Q_BLOCK) % SUBLANES
    if dil == 1:
        return 16 * j + g
    if dil == 4:
        return 32 * (g // 4) + 4 * j + g % 4
    return SUBLANES * g + j


def _band_bias(rel_bias, dil):
    mu = _block_order(dil)
    qi = mu[:, None] + Q_BLOCK
    ki = np.concatenate([mu, mu + Q_BLOCK])[None, :]
    off = qi - ki
    valid = (off >= 0) & (off <= N_KEYS)
    bucket = _bucket_np(dil * np.clip(off, 0, N_KEYS))
    first = valid & (np.arange(2 * Q_BLOCK)[None, :] >= Q_BLOCK)
    tables = [_bias_from_buckets(rel_bias, bucket, v).reshape(N_HEADS * Q_BLOCK, 2 * Q_BLOCK) for v in (valid, first)]
    return jnp.stack(tables)


def _sample_tables(rel_bias, w_buf):
    qpos = w_buf + np.arange(SAMPLE_T)[:, None]
    pos = np.arange(w_buf + NEW_COLS)[None, :]
    dist = qpos - pos
    in_seq = (dist >= 0) & (pos < w_buf + SAMPLE_T)
    mult = np.zeros(dist.shape, np.float32)
    for (w, d) in PATTERNS:
        mult += in_seq & (dist % d == 0) & (dist <= w)
    bucket = _bucket_np(np.maximum(dist, 0))
    bias = jnp.transpose(_bias_from_buckets(rel_bias, bucket, mult > 0), (1, 0, 2))
    rows = SAMPLE_T * N_HEADS
    mult_rows = np.broadcast_to(mult[:, None, :], (SAMPLE_T, N_HEADS, mult.shape[-1]))
    return bias.reshape(rows, -1), jnp.asarray(mult_rows.reshape(rows, -1))


PROJ_TM = 256
OUT_TM = 512
PROMPT_TM = 256
PROMPT_TE = 256
ATTN_SUB = 4


def _row_perm(tm):
    a = np.arange(tm)
    src = (a // Q_BLOCK) * Q_BLOCK + 16 * (a % SUBLANES) + (a % Q_BLOCK) // SUBLANES
    perm = np.zeros((tm, tm), np.float32)
    perm[a, src] = 1.0
    return perm


def kernel(x_prompt, x_sample, cache_k, cache_v, state_conv, p_prompt, p_sample, rel_bias, g_mix, w_in, q_gain,
           k_gain, conv_w, g_out_att, g_out_conv, w_out, g_ffn, w_router_group, w_router_expert, w_gate, w_up,
           w_down, g_ple, w_ple_gate, w_ple_proj):
    depth = w_in.shape[0]
    batch, seq, _ = x_prompt.shape
    dec_b, dec_t, _ = x_sample.shape
    w_buf = cache_k.shape[2]
    n_s = dec_b * dec_t
    keep = min(w_buf, seq)
    assert dec_t == SAMPLE_T and w_buf % LANES == 0
    assert seq % (Q_BLOCK * 16 * 2) == 0 and keep % PROJ_TM == 0 and seq % PROMPT_TM == 0
    cache_kt = jnp.transpose(cache_k, (0, 1, 3, 4, 2))
    cache_vt = jnp.transpose(cache_v, (0, 1, 3, 4, 2))

    row = lambda a: a.reshape(1, -1)
    member = (np.arange(ATT_DIM)[:, None] // HEAD_DIM == np.arange(LANES)[None, :])
    head_mean = jnp.asarray(member / HEAD_DIM, BF16)
    head_exp = jnp.asarray(member.T, BF16)
    src_lane = np.arange(LANES)
    expand = jnp.asarray((src_lane[:, None] // LSE_LANES_PER_HEAD == np.arange(ATT_DIM)[None, :] // HEAD_DIM)
                         & (src_lane[:, None] % LSE_LANES_PER_HEAD == 0), BF16)
    before = jnp.asarray(np.arange(LANES)[:, None] < np.arange(LANES)[None, :], BF16)
    sel_np = np.zeros((SUBLANES, LANES), np.float32)
    sel_np[0, REC_S1] = sel_np[1, REC_S2] = 1.0
    sel = jnp.asarray(sel_np, BF16)
    perm = jnp.asarray(_row_perm(PROJ_TM), BF16)
    unperm = jnp.asarray(_row_perm(OUT_TM).T, BF16)
    band = [_band_bias(rel_bias, d) for (_, d) in PATTERNS]
    s_bias, s_mult = _sample_tables(rel_bias, w_buf)

    hp = x_prompt.reshape(batch * seq, D_MODEL)
    hs = jnp.swapaxes(x_sample, 0, 1).reshape(n_s, D_MODEL)
    pp_all = p_prompt.reshape(depth, batch * seq, D_PLE)
    ps_all = jnp.swapaxes(p_sample, 1, 2).reshape(depth, n_s, D_PLE)
    new = {k: [] for k in ("kp", "vp", "cp", "ks", "vs", "cs")}
    hist_p = jnp.zeros((batch, SUBLANES, CONV_DIM), F32)

    for l in range(depth):
        w_in_bf = w_in[l].astype(BF16)
        wa_bf = w_out[l, :ATT_DIM].astype(BF16)
        wc_bf = w_out[l, ATT_DIM:].astype(BF16)
        wgate_bf = w_ple_gate[l].astype(BF16)
        wproj_bf = w_ple_proj[l].astype(BF16)
        w_router = jnp.concatenate(
            [w_router_expert[l], w_router_group[l],
             jnp.zeros((D_MODEL, LANES - N_EXPERTS - N_GROUPS), F32)], axis=1).astype(BF16)
        qg, kg = row(jnp.tile(q_gain[l], N_HEADS)), row(jnp.tile(k_gain[l], N_HEADS))
        mix = (row(g_mix[l]), w_in_bf, qg, kg, head_mean, head_exp, conv_w[l], row(g_out_conv[l]))
        moe = (row(g_ffn[l]), w_router, before, sel, w_gate, w_up, w_down, row(g_ple[l]), wgate_bf, wproj_bf)

        q, k, v, k_nat, v_nat, yn, nconv = _inproj(
            hp, hist_p, *mix, perm, tm=PROJ_TM, shift=1, tiles_per_seq=seq // PROJ_TM,
            keep_tiles=keep // PROJ_TM)
        os, lses = [], []
        for bias, (_, d) in zip(band, PATTERNS):
            sub = min(ATTN_SUB, seq // (Q_BLOCK * d))
            o, lse = _attn_pattern(q, k, v, bias, batch=batch, seq=seq, dil=d, sub=sub,
                                   res=min(d, ATTN_SUB // sub))
            os.append(o)
            lses.append(lse)
        hp = _outproj(os, lses, yn, hp, row(g_out_att[l]), wa_bf, wc_bf, expand, unperm, tm=OUT_TM)
        hp = _moe_ple2(hp, pp_all, *moe, tm=PROMPT_TM, te=PROMPT_TE, layer=l)
        new["kp"].append(jnp.transpose(k_nat, (0, 3, 1, 2)))
        new["vp"].append(jnp.transpose(v_nat, (0, 3, 1, 2)))
        new["cp"].append(nconv[:, SUBLANES - 2:])

        hist_s = jnp.swapaxes(state_conv[l], 0, 1).reshape(1, 2 * dec_b, CONV_DIM)
        q, k, v, yn, nconv = _inproj(hs, hist_s, *mix, tm=n_s, shift=dec_b, tiles_per_seq=1)
        bmaj = lambda a: jnp.swapaxes(a.reshape(dec_t, dec_b, N_HEADS, HEAD_DIM), 0, 1)
        qb, kb, vb = bmaj(q), bmaj(k), bmaj(v)
        rows8 = lambda a: jnp.pad(a.reshape(dec_b, dec_t, ATT_DIM), ((0, 0), (0, SUBLANES - dec_t), (0, 0)))
        att = _attn_sample(qb.reshape(dec_b, dec_t, ATT_DIM), rows8(kb), rows8(vb), cache_kt, cache_vt, l,
                           s_bias, s_mult)
        att_tm = jnp.swapaxes(att, 0, 1).reshape(n_s, ATT_DIM)
        hs = _outproj([att_tm], [], yn, hs, row(g_out_att[l]), wa_bf, wc_bf, tm=n_s)
        hs = _moe_ple2(hs, ps_all, *moe, tm=n_s, te=LANES, layer=l)
        new["ks"].append(kb)
        new["vs"].append(vb)
        new["cs"].append(jnp.swapaxes(nconv.reshape(2, dec_b, CONV_DIM), 0, 1))

    y_prompt = hp.reshape(batch, seq, D_MODEL)
    y_sample = jnp.swapaxes(hs.reshape(dec_t, dec_b, D_MODEL), 0, 1)
    st = lambda key: jnp.stack(new[key])
    return (y_prompt, y_sample, st("kp"), st("vp"), st("cp"), st("ks"), st("vs"), st("cs"))
```

```python
import functools

import jax
import jax.numpy as jnp
import numpy as np
from jax import lax
from jax.experimental import pallas as pl
from jax.experimental.pallas import tpu as pltpu

F32 = jnp.float32
BF16 = jnp.bfloat16
HIGHEST = lax.Precision.HIGHEST

D_MODEL = 1024
HEAD_DIM = 64
N_HEADS = 8
ATT_DIM = N_HEADS * HEAD_DIM
CONV_DIM = D_MODEL - ATT_DIM
MIX_IN = 3 * ATT_DIM + 3 * CONV_DIM
PATTERNS = ((128, 1), (512, 4), (2048, 16))
N_KEYS = 128
Q_BLOCK = 128
N_BUCKETS = 32
MAX_DISTANCE = 2048
N_GROUPS = 4
EXPERTS_PER_GROUP = 8
N_EXPERTS = N_GROUPS * EXPERTS_PER_GROUP
D_EXPERT = 256
D_PLE = 256
EPS = 1e-6
NEG = -1e30

LANES = 128
SUBLANES = 8
SLABS = Q_BLOCK // SUBLANES
LSE_LANES_PER_HEAD = LANES // N_HEADS
VMEM_LIMIT = 56 * 1024 * 1024
NT = (((1,), (1,)), ((), ()))


def _cparams(n_axes):
    return pltpu.CompilerParams(dimension_semantics=("arbitrary",) * n_axes,
                                vmem_limit_bytes=VMEM_LIMIT)


def _full(shape):
    n = len(shape)
    return pl.BlockSpec(shape, lambda *_: (0,) * n)


def _rms(x, gain):
    ms = jnp.mean(x * x, axis=-1, keepdims=True)
    return x * lax.rsqrt(ms + EPS) * gain


def _exact_dot(x, e_ref):
    hi = x.astype(BF16)
    r1 = x - hi.astype(F32)
    mid = r1.astype(BF16)
    lo = (r1 - mid.astype(F32)).astype(BF16)
    e = e_ref[...]
    return (jnp.dot(hi, e, preferred_element_type=F32) + jnp.dot(mid, e, preferred_element_type=F32)
            + jnp.dot(lo, e, preferred_element_type=F32))


def _inproj_kernel(*refs, shift, tiles_per_seq, permute):
    (h_ref, gmix_ref, w_ref, qg_ref, kg_ref, hmean_ref, hexp_ref, cw_ref, gconv_ref, hist_ref) = refs[:10]
    if permute:
        perm_ref, q_ref, k_ref, v_ref, kn_ref, vn_ref, yn_ref, nconv_ref, carry_ref = refs[10:]
    else:
        q_ref, k_ref, v_ref, yn_ref, nconv_ref, carry_ref = refs[10:]
    i = pl.program_id(0)
    a = _rms(h_ref[...], gmix_ref[...])
    proj = jnp.dot(a.astype(BF16), w_ref[...], preferred_element_type=F32)
    tm = proj.shape[0]

    def head_norm(t, g):
        ms = _exact_dot(t * t, hmean_ref)
        return t * _exact_dot(lax.rsqrt(ms + EPS), hexp_ref) * g

    q = head_norm(proj[:, 0:ATT_DIM], qg_ref[...])
    k = head_norm(proj[:, ATT_DIM:2 * ATT_DIM], kg_ref[...])
    v = proj[:, 2 * ATT_DIM:3 * ATT_DIM]
    if permute:
        kn_ref[...] = k.T.reshape(N_HEADS, HEAD_DIM, tm)
        vn_ref[...] = v.T.reshape(N_HEADS, HEAD_DIM, tm)
        qkv = jnp.concatenate([q * (HEAD_DIM ** -0.5), k, v], axis=-1).astype(BF16)
        moved = jnp.dot(perm_ref[...], qkv, preferred_element_type=F32)
        q_ref[...] = moved[:, 0:ATT_DIM]
        k_ref[...] = moved[:, ATT_DIM:2 * ATT_DIM]
        v_ref[...] = moved[:, 2 * ATT_DIM:3 * ATT_DIM]
    else:
        q_ref[...] = q
        k_ref[...] = k
        v_ref[...] = v
    c0 = 3 * ATT_DIM
    hc = proj[:, c0:c0 + CONV_DIM]
    gb = proj[:, c0 + CONV_DIM:c0 + 2 * CONV_DIM]
    gc = proj[:, c0 + 2 * CONV_DIM:c0 + 3 * CONV_DIM]
    u = gc * hc

    if shift == 1:
        @pl.when(i % tiles_per_seq == 0)
        def _():
            carry_ref[...] = hist_ref[0]
        h0 = carry_ref[SUBLANES - 2:SUBLANES - 1, :]
        h1 = carry_ref[SUBLANES - 1:SUBLANES, :]
        row = lax.broadcasted_iota(jnp.int32, (tm, 1), 0)
        u1 = jnp.where(row == 0, h1, pltpu.roll(u, 1, 0))
        u2 = jnp.where(row == 0, h0, jnp.where(row == 1, h1, pltpu.roll(u, 2, 0)))
        carry_ref[...] = u[tm - SUBLANES:tm, :]
        nconv_ref[0] = u[tm - SUBLANES:tm, :]
    else:
        hist = hist_ref[0]
        u1 = jnp.concatenate([hist[shift:2 * shift], u[0:tm - shift]], axis=0)
        u2 = jnp.concatenate([hist, u[0:tm - 2 * shift]], axis=0)
        nconv_ref[0] = u[tm - 2 * shift:tm, :]
    conv = cw_ref[0:1, :] * u2 + cw_ref[1:2, :] * u1 + cw_ref[2:3, :] * u
    yn_ref[...] = _rms(gb * conv, gconv_ref[...])


def _inproj(h, hist, g_mix, w_in_bf, q_gain, k_gain, hmean, hexp, conv_w, g_out_conv, perm=None, *,
            tm, shift, tiles_per_seq, keep_tiles=0):
    n = h.shape[0]
    hist_rows = hist.shape[1]
    nseq = hist.shape[0]
    tok = lambda w: pl.BlockSpec((tm, w), lambda i: (i, 0))
    seq3 = lambda r: pl.BlockSpec((1, r, CONV_DIM), lambda i: (i // tiles_per_seq, 0, 0))
    nconv_rows = SUBLANES if shift == 1 else 2 * shift
    att = jax.ShapeDtypeStruct((n, ATT_DIM), F32)
    in_specs = [tok(D_MODEL), _full((1, D_MODEL)), _full((D_MODEL, MIX_IN)), _full((1, ATT_DIM)),
                _full((1, ATT_DIM)), _full((ATT_DIM, LANES)), _full((LANES, ATT_DIM)), _full((3, CONV_DIM)),
                _full((1, CONV_DIM)), seq3(hist_rows)]
    args = [h, g_mix, w_in_bf, q_gain, k_gain, hmean, hexp, conv_w, g_out_conv, hist]
    out_specs = [tok(ATT_DIM)] * 3
    out_shape = [att] * 3
    if perm is not None:
        in_specs.append(_full((tm, tm)))
        args.append(perm)
        first = tiles_per_seq - keep_tiles
        kept = pl.BlockSpec((None, N_HEADS, HEAD_DIM, tm),
                            lambda i: (i // tiles_per_seq, 0, 0, jnp.maximum(i % tiles_per_seq - first, 0)))
        out_specs += [kept, kept]
        out_shape += [jax.ShapeDtypeStruct((nseq, N_HEADS, HEAD_DIM, keep_tiles * tm), F32)] * 2
    out_specs += [tok(CONV_DIM), seq3(nconv_rows)]
    out_shape += [jax.ShapeDtypeStruct((n, CONV_DIM), F32),
                  jax.ShapeDtypeStruct((nseq, nconv_rows, CONV_DIM), F32)]
    return pl.pallas_call(
        functools.partial(_inproj_kernel, shift=shift, tiles_per_seq=tiles_per_seq, permute=perm is not None),
        grid=(n // tm,),
        in_specs=in_specs, out_specs=out_specs, out_shape=out_shape,
        scratch_shapes=[pltpu.VMEM((SUBLANES, CONV_DIM), F32)],
        compiler_params=_cparams(1),
        name="inproj",
    )(*args)


def _attn_block(q, kk, vv, bias):
    lane = lax.broadcasted_iota(jnp.int32, (Q_BLOCK, LANES), 1)
    upper = lane >= HEAD_DIM
    scores = []
    for h in range(N_HEADS):
        j, e = divmod(h, 2)
        qp = q[:, j * LANES:(j + 1) * LANES]
        qm = (jnp.where(upper, qp, 0.0) if e else jnp.where(upper, 0.0, qp)).astype(BF16)
        scores.append(lax.dot_general(qm, kk[:, j * LANES:(j + 1) * LANES], NT, preferred_element_type=F32))
    s = jnp.concatenate(scores, axis=0) + bias
    m = jnp.max(s, axis=-1, keepdims=True)
    p = jnp.exp(s - m)
    den = jnp.sum(p, axis=-1, keepdims=True)
    pb = p.astype(BF16)
    inv = 1.0 / den
    lse = m + jnp.log(den)
    lse_grp = lane // LSE_LANES_PER_HEAD
    lse_tile = jnp.zeros((Q_BLOCK, LANES), F32)
    outs = []
    for j in range(N_HEADS // 2):
        pair = None
        for e in range(2):
            h = 2 * j + e
            rows = slice(h * Q_BLOCK, (h + 1) * Q_BLOCK)
            o = jnp.dot(pb[rows], vv[:, j * LANES:(j + 1) * LANES], preferred_element_type=F32) * inv[rows]
            pair = o if e == 0 else jnp.where(upper, o, pair)
            lse_tile = jnp.where(lse_grp == h, lse[rows], lse_tile)
        outs.append(pair)
    return jnp.concatenate(outs, axis=-1), lse_tile


def _attn_kernel(q_ref, kp_ref, kc_ref, vp_ref, vc_ref, bias_ref, o_ref, lse_ref, kbuf, vbuf, *, sub, res):
    n = pl.program_id(2)
    rows = sub * Q_BLOCK
    for r in range(res):
        kbuf[0:Q_BLOCK, :] = kp_ref[:, r].reshape(Q_BLOCK, ATT_DIM).astype(BF16)
        kbuf[Q_BLOCK:, :] = kc_ref[:, :, r].reshape(rows, ATT_DIM).astype(BF16)
        vbuf[0:Q_BLOCK, :] = vp_ref[:, r].reshape(Q_BLOCK, ATT_DIM).astype(BF16)
        vbuf[Q_BLOCK:, :] = vc_ref[:, :, r].reshape(rows, ATT_DIM).astype(BF16)
        for j in range(sub):
            q = q_ref[j, :, r].reshape(Q_BLOCK, ATT_DIM)
            r0 = j * Q_BLOCK
            first = (n == 0).astype(jnp.int32) if j == 0 else 0
            o, lse = _attn_block(q, kbuf[r0:r0 + 2 * Q_BLOCK, :], vbuf[r0:r0 + 2 * Q_BLOCK, :], bias_ref[first])
            o_ref[j, :, r] = o.reshape(SLABS, SUBLANES, ATT_DIM)
            lse_ref[j, :, r] = lse.reshape(SLABS, SUBLANES, LANES)


def _attn_pattern(q, k, v, bias, *, batch, seq, dil, sub, res):
    nblk = seq // (Q_BLOCK * dil)
    view = lambda t: t.reshape(batch, nblk, SLABS, dil, SUBLANES, t.shape[-1])
    cur = lambda c: pl.BlockSpec((None, sub, SLABS, res, SUBLANES, c), lambda b, r, n: (b, n, 0, r, 0, 0))
    prev = pl.BlockSpec((None, None, SLABS, res, SUBLANES, ATT_DIM),
                        lambda b, r, n: (b, jnp.maximum(n * sub - 1, 0), 0, r, 0, 0))
    o, lse = pl.pallas_call(
        functools.partial(_attn_kernel, sub=sub, res=res),
        grid=(batch, dil // res, nblk // sub),
        in_specs=[cur(ATT_DIM), prev, cur(ATT_DIM), prev, cur(ATT_DIM),
                  _full((2, N_HEADS * Q_BLOCK, 2 * Q_BLOCK))],
        out_specs=[cur(ATT_DIM), cur(LANES)],
        out_shape=[jax.ShapeDtypeStruct((batch, nblk, SLABS, dil, SUBLANES, ATT_DIM), F32),
                   jax.ShapeDtypeStruct((batch, nblk, SLABS, dil, SUBLANES, LANES), F32)],
        scratch_shapes=[pltpu.VMEM(((sub + 1) * Q_BLOCK, ATT_DIM), BF16),
                        pltpu.VMEM(((sub + 1) * Q_BLOCK, ATT_DIM), BF16)],
        compiler_params=_cparams(3),
        name=f"attn_d{dil}",
    )(view(q), view(k), view(k), view(v), view(v), bias)
    return o.reshape(batch * seq, ATT_DIM), lse.reshape(batch * seq, LANES)


SAMPLE_T = 4
NEW_COLS = LANES


def _attn_sample_kernel(q_ref, kt_ref, kn_ref, vt_ref, vn_ref, bias_ref, mult_ref, o_ref):
    rows = SAMPLE_T * N_HEADS
    q4 = q_ref[...] * (HEAD_DIM ** -0.5)
    qt = jnp.concatenate([jnp.broadcast_to(q4[t:t + 1, :], (N_HEADS, ATT_DIM)) for t in range(SAMPLE_T)], axis=0)
    lane_head = lax.broadcasted_iota(jnp.int32, (rows, ATT_DIM), 1) // HEAD_DIM
    row_head = lax.broadcasted_iota(jnp.int32, (rows, ATT_DIM), 0) % N_HEADS
    own = lane_head == row_head
    qbd = jnp.where(own, qt, 0.0).astype(BF16)
    flat = lambda ref: ref[...].reshape(ATT_DIM, ref.shape[-1]).astype(BF16)
    pad = jnp.zeros((NEW_COLS - SUBLANES, ATT_DIM), F32)
    new_rows = lambda ref: jnp.concatenate([ref[...], pad], axis=0).astype(BF16)
    s = jnp.concatenate([jnp.dot(qbd, flat(kt_ref), preferred_element_type=F32),
                         lax.dot_general(qbd, new_rows(kn_ref), NT, preferred_element_type=F32)],
                        axis=-1) + bias_ref[...]
    m = jnp.max(s, axis=-1, keepdims=True)
    p = jnp.exp(s - m) * mult_ref[...]
    den = jnp.sum(p, axis=-1, keepdims=True)
    pb = p.astype(BF16)
    w_buf = kt_ref.shape[-1]
    acc = (lax.dot_general(pb[:, :w_buf], flat(vt_ref), NT, preferred_element_type=F32)
           + jnp.dot(pb[:, w_buf:], new_rows(vn_ref), preferred_element_type=F32))
    acc = jnp.where(own, acc / den, 0.0)
    for t in range(SAMPLE_T):
        o_ref[t:t + 1, :] = jnp.sum(acc[t * N_HEADS:(t + 1) * N_HEADS, :], axis=0, keepdims=True)


def _attn_sample(q, k_new, v_new, cache_kt, cache_vt, layer, bias, mult):
    nb = q.shape[0]
    w_buf = cache_kt.shape[-1]
    tok = pl.BlockSpec((None, SAMPLE_T, ATT_DIM), lambda b: (b, 0, 0))
    new = pl.BlockSpec((None, SUBLANES, ATT_DIM), lambda b: (b, 0, 0))
    old = pl.BlockSpec((None, None, N_HEADS, HEAD_DIM, w_buf), lambda b: (layer, b, 0, 0, 0))
    tbl = _full((SAMPLE_T * N_HEADS, w_buf + NEW_COLS))
    return pl.pallas_call(
        _attn_sample_kernel,
        grid=(nb,),
        in_specs=[tok, old, new, old, new, tbl, tbl],
        out_specs=tok,
        out_shape=jax.ShapeDtypeStruct((nb, SAMPLE_T, ATT_DIM), F32),
        compiler_params=_cparams(1),
        name="attn_sample",
    )(q, cache_kt, k_new, cache_vt, v_new, bias, mult)


def _split_dot(x, e_ref):
    hi = x.astype(BF16)
    lo = (x - hi.astype(F32)).astype(BF16)
    return (jnp.dot(hi, e_ref[...], preferred_element_type=F32)
            + jnp.dot(lo, e_ref[...], preferred_element_type=F32))


def _outproj_kernel(*refs, n_pat):
    mix = n_pat > 1
    n_lse = n_pat if mix else 0
    o_refs = refs[0:n_pat]
    l_refs = refs[n_pat:n_pat + n_lse]
    rest = refs[n_pat + n_lse:]
    if mix:
        yn_ref, h_ref, gatt_ref, exp_ref, unperm_ref, wa_ref, wc_ref, out_ref = rest
        lses = [r[...] for r in l_refs]
        top = functools.reduce(jnp.maximum, lses)
        ws = [jnp.exp(l - top) for l in lses]
        tot = functools.reduce(lambda a, b: a + b, ws)
        att = None
        for w, o_ref in zip(ws, o_refs):
            term = _split_dot(w / tot, exp_ref) * o_ref[...]
            att = term if att is None else att + term
        att_bf = jnp.dot(unperm_ref[...], _rms(att, gatt_ref[...]).astype(BF16),
                         preferred_element_type=F32).astype(BF16)
    else:
        yn_ref, h_ref, gatt_ref, wa_ref, wc_ref, out_ref = rest
        att_bf = _rms(o_refs[0][...], gatt_ref[...]).astype(BF16)
    y = (jnp.dot(att_bf, wa_ref[...], preferred_element_type=F32)
         + jnp.dot(yn_ref[...].astype(BF16), wc_ref[...], preferred_element_type=F32))
    out_ref[...] = h_ref[...] + y


def _outproj(os, lses, yn, h, g_att, wa_bf, wc_bf, expand=None, unperm=None, *, tm):
    n = h.shape[0]
    n_pat = len(os)
    tok = lambda w: pl.BlockSpec((tm, w), lambda i: (i, 0))
    in_specs = [tok(ATT_DIM)] * n_pat + [tok(LANES)] * len(lses) + [tok(CONV_DIM), tok(D_MODEL), _full((1, ATT_DIM))]
    args = [*os, *lses, yn, h, g_att]
    if n_pat > 1:
        in_specs += [_full((LANES, ATT_DIM)), _full((tm, tm))]
        args += [expand, unperm]
    in_specs += [_full((ATT_DIM, D_MODEL)), _full((CONV_DIM, D_MODEL))]
    args += [wa_bf, wc_bf]
    return pl.pallas_call(
        functools.partial(_outproj_kernel, n_pat=n_pat),
        grid=(n // tm,),
        in_specs=in_specs,
        out_specs=tok(D_MODEL),
        out_shape=jax.ShapeDtypeStruct((n, D_MODEL), F32),
        compiler_params=_cparams(1),
        name="outproj",
    )(*args)


ROUTE_I1, ROUTE_I2, ROUTE_R1, ROUTE_R2, ROUTE_W1, ROUTE_W2 = range(6)
GROUP_LANE0 = N_EXPERTS


def _route_kernel(h_ref, g_ref, wr_ref, route_ref, cnt_ref, carry_ref):
    i = pl.program_id(0)

    @pl.when(i == 0)
    def _():
        carry_ref[...] = jnp.zeros_like(carry_ref)

    m = _rms(h_ref[...], g_ref[...])
    logits = jnp.dot(m.astype(BF16), wr_ref[...], preferred_element_type=F32)
    tm = logits.shape[0]
    lane_i = lax.broadcasted_iota(jnp.int32, (tm, LANES), 1)
    lane = lane_i.astype(F32)
    big = jnp.float32(4 * LANES)

    is_g = jnp.logical_and(lane_i >= GROUP_LANE0, lane_i < GROUP_LANE0 + N_GROUPS)
    gl = jnp.where(is_g, logits, NEG)
    gmax = jnp.max(gl, axis=-1, keepdims=True)
    g_w = 1.0 / jnp.sum(jnp.where(is_g, jnp.exp(gl - gmax), 0.0), axis=-1, keepdims=True)
    g_sel = jnp.min(jnp.where(gl == gmax, lane - GROUP_LANE0, big), axis=-1, keepdims=True)

    grp_of_lane = (lane_i // EXPERTS_PER_GROUP).astype(F32)
    in_grp = jnp.logical_and(lane_i < N_EXPERTS, grp_of_lane == g_sel)
    el = jnp.where(in_grp, logits, NEG)
    t1 = jnp.max(el, axis=-1, keepdims=True)
    i1 = jnp.min(jnp.where(el == t1, lane, big), axis=-1, keepdims=True)
    el2 = jnp.where(lane == i1, NEG, el)
    t2 = jnp.max(el2, axis=-1, keepdims=True)
    i2 = jnp.min(jnp.where(el2 == t2, lane, big), axis=-1, keepdims=True)
    e2 = jnp.exp(t2 - t1)
    w1 = g_w / (1.0 + e2)
    w2 = g_w * e2 / (1.0 + e2)

    hit1 = lane == i1
    hit2 = lane == i2
    c = jnp.where(jnp.logical_or(hit1, hit2), 1.0, 0.0)
    rr = lax.broadcasted_iota(jnp.int32, (tm, tm), 0)
    cc = lax.broadcasted_iota(jnp.int32, (tm, tm), 1)
    lower = jnp.where(rr > cc, 1.0, 0.0).astype(BF16)
    before = jnp.dot(lower, c.astype(BF16), preferred_element_type=F32) + carry_ref[0:1, :]
    r1 = jnp.sum(jnp.where(hit1, before, 0.0), axis=-1, keepdims=True)
    r2 = jnp.sum(jnp.where(hit2, before, 0.0), axis=-1, keepdims=True)
    total = carry_ref[0:1, :] + jnp.sum(c, axis=0, keepdims=True)
    carry_ref[...] = jnp.broadcast_to(total, carry_ref.shape)
    cnt_ref[...] = jnp.broadcast_to(total, cnt_ref.shape)

    rec = jnp.zeros((tm, LANES), F32)
    for idx, val in ((ROUTE_I1, i1), (ROUTE_I2, i2), (ROUTE_R1, r1), (ROUTE_R2, r2), (ROUTE_W1, w1), (ROUTE_W2, w2)):
        rec = jnp.where(lane_i == idx, val, rec)
    route_ref[...] = rec


def _route(h, g_ffn, w_router, *, tm):
    n = h.shape[0]
    tok = lambda w: pl.BlockSpec((tm, w), lambda i: (i, 0))
    return pl.pallas_call(
        _route_kernel,
        grid=(n // tm,),
        in_specs=[tok(D_MODEL), _full((1, D_MODEL)), _full((D_MODEL, LANES))],
        out_specs=[tok(LANES), _full((SUBLANES, LANES))],
        out_shape=[jax.ShapeDtypeStruct((n, LANES), F32), jax.ShapeDtypeStruct((SUBLANES, LANES), F32)],
        scratch_shapes=[pltpu.VMEM((SUBLANES, LANES), F32)],
        compiler_params=_cparams(1),
        name="route",
    )(h, g_ffn, w_router)


def _dispatch_kernel(pos_ref, h_ref, g_ref, xs_ref, mbuf, sem):
    tm = mbuf.shape[0]
    mbuf[...] = _rms(h_ref[...], g_ref[...])

    def row_copy(t, p):
        return pltpu.make_async_copy(mbuf.at[pl.ds(t, 1), :], xs_ref.at[pl.ds(p, 1), :], sem)

    def issue(t, carry):
        row_copy(t, pos_ref[0, 0, 2 * t]).start()
        row_copy(t, pos_ref[0, 0, 2 * t + 1]).start()
        return carry

    lax.fori_loop(0, tm, issue, 0)

    def drain(t, carry):
        row_copy(0, 0).wait()
        row_copy(0, 0).wait()
        return carry

    lax.fori_loop(0, tm, drain, 0)


def _dispatch(h, g_ffn, pos, n_rows, *, tm):
    n = h.shape[0]
    return pl.pallas_call(
        _dispatch_kernel,
        grid=(n // tm,),
        in_specs=[pl.BlockSpec((1, 1, 2 * tm), lambda i: (i, 0, 0), memory_space=pltpu.SMEM),
                  pl.BlockSpec((tm, D_MODEL), lambda i: (i, 0)), _full((1, D_MODEL))],
        out_specs=pl.BlockSpec(memory_space=pl.ANY),
        out_shape=jax.ShapeDtypeStruct((n_rows, D_MODEL), F32),
        scratch_shapes=[pltpu.VMEM((tm, D_MODEL), F32), pltpu.SemaphoreType.DMA(())],
        compiler_params=pltpu.CompilerParams(dimension_semantics=("arbitrary",), vmem_limit_bytes=VMEM_LIMIT,
                                             has_side_effects=True),
        name="dispatch",
    )(pos, h, g_ffn)


def _expert_kernel(te_ref, tv_ref, x_ref, wg_ref, wu_ref, wd_ref, y_ref, wg_bf, wu_bf, wd_bf):
    i = pl.program_id(0)
    valid = tv_ref[i]
    changed = jnp.logical_or(i == 0, te_ref[i] != te_ref[jnp.maximum(i - 1, 0)])

    @pl.when(jnp.logical_and(changed, valid > 0))
    def _():
        wg_bf[...] = wg_ref[0].astype(BF16)
        wu_bf[...] = wu_ref[0].astype(BF16)
        wd_bf[...] = wd_ref[0].astype(BF16)

    @pl.when(valid > 0)
    def _():
        te = x_ref.shape[0]
        row = lax.broadcasted_iota(jnp.int32, (te, 1), 0)
        x = jnp.where(row < valid, x_ref[...], 0.0).astype(BF16)
        hg = jnp.dot(x, wg_bf[...], preferred_element_type=F32)
        hu = jnp.dot(x, wu_bf[...], preferred_element_type=F32)
        hid = (hg * jax.nn.sigmoid(hg)) * hu
        y_ref[...] = jnp.dot(hid.astype(BF16), wd_bf[...], preferred_element_type=F32)

    @pl.when(valid <= 0)
    def _():
        y_ref[...] = jnp.zeros_like(y_ref)


def _experts(xs, tile_expert, tile_valid, w_gate, w_up, w_down, *, te):
    n_rows = xs.shape[0]
    wspec = lambda a, b: pl.BlockSpec((1, a, b), lambda i, e, v: (e[i], 0, 0))
    grid_spec = pltpu.PrefetchScalarGridSpec(
        num_scalar_prefetch=2,
        grid=(n_rows // te,),
        in_specs=[pl.BlockSpec((te, D_MODEL), lambda i, e, v: (i, 0)),
                  wspec(D_MODEL, D_EXPERT), wspec(D_MODEL, D_EXPERT), wspec(D_EXPERT, D_MODEL)],
        out_specs=pl.BlockSpec((te, D_MODEL), lambda i, e, v: (i, 0)),
        scratch_shapes=[pltpu.VMEM((D_MODEL, D_EXPERT), BF16), pltpu.VMEM((D_MODEL, D_EXPERT), BF16),
                        pltpu.VMEM((D_EXPERT, D_MODEL), BF16)])
    return pl.pallas_call(
        _expert_kernel,
        grid_spec=grid_spec,
        out_shape=jax.ShapeDtypeStruct((n_rows, D_MODEL), F32),
        compiler_params=_cparams(1),
        name="experts",
    )(tile_expert, tile_valid, xs, w_gate, w_up, w_down)


def _combine_kernel(pos_ref, h_ref, route_ref, ys_ref, p_ref, gple_ref, wgate_ref, wproj_ref, out_ref,
                    y0, y1, sem):
    tm = y0.shape[0]

    def row_copy(p, dst, t):
        return pltpu.make_async_copy(ys_ref.at[pl.ds(p, 1), :], dst.at[pl.ds(t, 1), :], sem)

    def issue(t, carry):
        row_copy(pos_ref[0, 0, 2 * t], y0, t).start()
        row_copy(pos_ref[0, 0, 2 * t + 1], y1, t).start()
        return carry

    lax.fori_loop(0, tm, issue, 0)

    def drain(t, carry):
        row_copy(0, y0, 0).wait()
        row_copy(0, y1, 0).wait()
        return carry

    lax.fori_loop(0, tm, drain, 0)

    rec = route_ref[...]
    w1 = rec[:, ROUTE_W1:ROUTE_W1 + 1]
    w2 = rec[:, ROUTE_W2:ROUTE_W2 + 1]
    h2 = h_ref[...] + w1 * y0[...] + w2 * y1[...]
    gate = jax.nn.sigmoid(jnp.dot(_rms(h2, gple_ref[...]).astype(BF16), wgate_ref[...], preferred_element_type=F32))
    ple = jnp.dot(p_ref[...].astype(BF16), wproj_ref[...], preferred_element_type=F32)
    out_ref[...] = h2 + ple * gate


def _combine(h, route, pos, ys, p, g_ple, wgate_bf, wproj_bf, *, tm):
    n = h.shape[0]
    tok = lambda w: pl.BlockSpec((tm, w), lambda i: (i, 0))
    return pl.pallas_call(
        _combine_kernel,
        grid=(n // tm,),
        in_specs=[pl.BlockSpec((1, 1, 2 * tm), lambda i: (i, 0, 0), memory_space=pltpu.SMEM),
                  tok(D_MODEL), tok(LANES), pl.BlockSpec(memory_space=pl.ANY), tok(D_PLE),
                  _full((1, D_MODEL)), _full((D_MODEL, D_MODEL)), _full((D_PLE, D_MODEL))],
        out_specs=tok(D_MODEL),
        out_shape=jax.ShapeDtypeStruct((n, D_MODEL), F32),
        scratch_shapes=[pltpu.VMEM((tm, D_MODEL), F32), pltpu.VMEM((tm, D_MODEL), F32),
                        pltpu.SemaphoreType.DMA(())],
        compiler_params=_cparams(1),
        name="combine",
    )(pos, h, route, ys, p, g_ple, wgate_bf, wproj_bf)


def _lookup(table, idx, size):
    hit = idx[..., None] == jnp.arange(size, dtype=jnp.int32)
    return jnp.sum(jnp.where(hit, table, 0), axis=-1)


def _moe_ple(h, p, g_ffn, w_router, w_gate, w_up, w_down, g_ple, wgate_bf, wproj_bf, *, tm, te):
    n = h.shape[0]
    route, counts = _route(h, g_ffn, w_router, tm=tm)
    counts = counts[0, :N_EXPERTS].astype(jnp.int32)
    padded = ((counts + te - 1) // te) * te
    ends = jnp.cumsum(padded)
    offs = ends - padded
    ids = route[:, ROUTE_I1:ROUTE_I2 + 1].astype(jnp.int32)
    ranks = route[:, ROUTE_R1:ROUTE_R2 + 1].astype(jnp.int32)
    pos = (_lookup(offs, ids, N_EXPERTS) + ranks).reshape(n // tm, 1, 2 * tm)
    n_tiles = (2 * n) // te + N_EXPERTS
    starts = jnp.arange(n_tiles, dtype=jnp.int32) * te
    tile_expert = jnp.minimum(jnp.sum((starts[:, None] >= ends[None, :]).astype(jnp.int32), axis=-1), N_EXPERTS - 1)
    tile_valid = jnp.clip(_lookup(counts, tile_expert, N_EXPERTS)
                          - (starts - _lookup(offs, tile_expert, N_EXPERTS)), 0, te).astype(jnp.int32)
    xs = _dispatch(h, g_ffn, pos, n_tiles * te, tm=tm)
    ys = _experts(xs, tile_expert, tile_valid, w_gate, w_up, w_down, te=te)
    return _combine(h, route, pos, ys, p, g_ple, wgate_bf, wproj_bf, tm=tm)


REC_E1, REC_E2, REC_S1, REC_S2, REC_W1, REC_W2 = range(6)
PACKED = D_MODEL // 2
ROW_W = PACKED + LANES
LIST_LANES = LANES
LIST_COUNT = LIST_LANES - 1


def _slots(tm):
    need = 2 * tm + (SUBLANES - 1) * N_EXPERTS + SUBLANES
    return -(-need // LANES) * LANES


def _pieces(x):
    hi = x.astype(BF16)
    r1 = x - hi.astype(F32)
    mid = r1.astype(BF16)
    return hi, mid, (r1 - mid.astype(F32)).astype(BF16)


def _sort_kernel(h_ref, g_ref, wr_ref, before_ref, sel_ref, xs_ref, rec_ref, cnt_ref):
    m = _rms(h_ref[...], g_ref[...])
    logits = jnp.dot(m.astype(BF16), wr_ref[...], preferred_element_type=F32)
    tm = logits.shape[0]
    slots = xs_ref.shape[0]
    lane_i = lax.broadcasted_iota(jnp.int32, (tm, LANES), 1)
    lane = lane_i.astype(F32)
    big = jnp.float32(4 * LANES)

    is_g = jnp.logical_and(lane_i >= GROUP_LANE0, lane_i < GROUP_LANE0 + N_GROUPS)
    gl = jnp.where(is_g, logits, NEG)
    gmax = jnp.max(gl, axis=-1, keepdims=True)
    g_w = 1.0 / jnp.sum(jnp.where(is_g, jnp.exp(gl - gmax), 0.0), axis=-1, keepdims=True)
    g_sel = jnp.min(jnp.where(gl == gmax, lane - GROUP_LANE0, big), axis=-1, keepdims=True)

    grp_of_lane = (lane_i // EXPERTS_PER_GROUP).astype(F32)
    in_grp = jnp.logical_and(lane_i < N_EXPERTS, grp_of_lane == g_sel)
    el = jnp.where(in_grp, logits, NEG)
    t1 = jnp.max(el, axis=-1, keepdims=True)
    e1 = jnp.min(jnp.where(el == t1, lane, big), axis=-1, keepdims=True)
    el2 = jnp.where(lane == e1, NEG, el)
    t2 = jnp.max(el2, axis=-1, keepdims=True)
    e2 = jnp.min(jnp.where(el2 == t2, lane, big), axis=-1, keepdims=True)
    ex = jnp.exp(t2 - t1)
    w1 = g_w / (1.0 + ex)
    w2 = g_w * ex / (1.0 + ex)

    hit1 = lane == e1
    hit2 = lane == e2
    c = jnp.where(hit1, 1.0, jnp.where(hit2, 1.0, 0.0))
    rr = lax.broadcasted_iota(jnp.int32, (tm, tm), 0)
    cc = lax.broadcasted_iota(jnp.int32, (tm, tm), 1)
    lower = jnp.where(rr > cc, 1.0, 0.0).astype(BF16)
    rank = jnp.dot(lower, c.astype(BF16), preferred_element_type=F32)
    cnt = jnp.sum(c, axis=0, keepdims=True)
    chunks = jnp.floor((cnt + (SUBLANES - 1)) * (1.0 / SUBLANES))
    start = SUBLANES * jnp.dot(jnp.broadcast_to(chunks, (SUBLANES, LANES)).astype(BF16), before_ref[...],
                               preferred_element_type=F32)[0:1, :]
    slot_of = rank + start
    s1 = jnp.sum(jnp.where(hit1, slot_of, 0.0), axis=-1, keepdims=True)
    s2 = jnp.sum(jnp.where(hit2, slot_of, 0.0), axis=-1, keepdims=True)

    rec = jnp.zeros((tm, LANES), F32)
    for idx, val in ((REC_E1, e1), (REC_E2, e2), (REC_S1, s1), (REC_S2, s2), (REC_W1, w1), (REC_W2, w2)):
        rec = jnp.where(lane_i == idx, val, rec)
    rec_ref[...] = rec
    cnt_ref[0] = jnp.broadcast_to(cnt, (SUBLANES, LANES))

    rec_parts = _pieces(rec)
    srow = sum(lax.dot_general(sel_ref[...], part, NT, preferred_element_type=F32) for part in rec_parts)
    slot_id = lax.broadcasted_iota(jnp.int32, (slots, tm), 0).astype(F32)
    place = jnp.where(slot_id == srow[0:1, :], 1.0, jnp.where(slot_id == srow[1:2, :], 1.0, 0.0)).astype(BF16)
    payload = jnp.concatenate([m.astype(BF16), *rec_parts], axis=-1)
    moved = jnp.dot(place, payload, preferred_element_type=F32)
    info = (moved[:, D_MODEL:D_MODEL + LANES] + moved[:, D_MODEL + LANES:D_MODEL + 2 * LANES]
            + moved[:, D_MODEL + 2 * LANES:])
    xs_ref[...] = jnp.concatenate([_pack_bf16(moved[:, :D_MODEL]), pltpu.bitcast(info, jnp.uint32)], axis=-1)


def _sort(h, g_ffn, w_router, before, sel, *, tm):
    n = h.shape[0]
    slots = _slots(tm)
    tok = lambda w: pl.BlockSpec((tm, w), lambda i: (i, 0))
    return pl.pallas_call(
        _sort_kernel,
        grid=(n // tm,),
        in_specs=[tok(D_MODEL), _full((1, D_MODEL)), _full((D_MODEL, LANES)), _full((LANES, LANES)),
                  _full((SUBLANES, LANES))],
        out_specs=[pl.BlockSpec((slots, ROW_W), lambda i: (i, 0)), tok(LANES),
                   pl.BlockSpec((1, SUBLANES, LANES), lambda i: (i, 0, 0))],
        out_shape=[jax.ShapeDtypeStruct((n // tm * slots, ROW_W), jnp.uint32), jax.ShapeDtypeStruct((n, LANES), F32),
                   jax.ShapeDtypeStruct((n // tm, SUBLANES, LANES), F32)],
        compiler_params=_cparams(1),
        name="moe_sort",
    )(h, g_ffn, w_router, before, sel)


def _chunk_copy(src_hbm, row, dst, c, sem):
    if not isinstance(row, int):
        row = pl.multiple_of(row, SUBLANES)
    to = c * SUBLANES
    if not isinstance(to, int):
        to = pl.multiple_of(to, SUBLANES)
    return pltpu.make_async_copy(src_hbm.at[pl.ds(row, SUBLANES), :], dst.at[pl.ds(to, SUBLANES), :], sem)


def _expert_kernel2(te_ref, tv_ref, src_ref, nxt_ref, xa_ref, xb_ref, wg_ref, wu_ref, wd_ref, y_ref,
                    xbuf, sem, wg_bf, wu_bf, wd_bf, *, rows_a):
    i = pl.program_id(0)
    n = pl.num_programs(0)
    te = xbuf.shape[1]
    slot = i % 2

    def fetch(list_ref, to_slot):
        n_a = list_ref[0, 0, LIST_COUNT]

        def from_a(c, carry):
            _chunk_copy(xa_ref, list_ref[0, 0, c], xbuf.at[to_slot], c, sem.at[to_slot]).start()
            return carry

        def from_b(c, carry):
            _chunk_copy(xb_ref, list_ref[0, 0, c] - rows_a, xbuf.at[to_slot], c, sem.at[to_slot]).start()
            return carry

        lax.fori_loop(0, n_a, from_a, 0)
        lax.fori_loop(n_a, te // SUBLANES, from_b, 0)

    @pl.when(jnp.logical_and(i == 0, tv_ref[0] > 0))
    def _():
        fetch(src_ref, 0)

    nxt = jnp.minimum(i + 1, n - 1)

    @pl.when(jnp.logical_and(i + 1 < n, tv_ref[nxt] > 0))
    def _():
        fetch(nxt_ref, 1 - slot)

    valid = tv_ref[i]
    changed = jnp.logical_or(i == 0, te_ref[i] != te_ref[jnp.maximum(i - 1, 0)])

    @pl.when(jnp.logical_and(changed, valid > 0))
    def _():
        wg_bf[...] = wg_ref[0].astype(BF16)
        wu_bf[...] = wu_ref[0].astype(BF16)
        wd_bf[...] = wd_ref[0].astype(BF16)

    @pl.when(valid > 0)
    def _():
        pltpu.make_async_copy(xbuf.at[slot], xbuf.at[slot], sem.at[slot]).wait()
        rows = xbuf[slot]
        x = _unpack_bf16(rows[:, :PACKED])
        info = pltpu.bitcast(rows[:, PACKED:], F32)
        mine = info[:, REC_E1:REC_E1 + 1] == te_ref[i].astype(F32)
        gate = jnp.where(mine, info[:, REC_W1:REC_W1 + 1], info[:, REC_W2:REC_W2 + 1])
        hg = jnp.dot(x, wg_bf[...], preferred_element_type=F32)
        hu = jnp.dot(x, wu_bf[...], preferred_element_type=F32)
        hid = (hg * jax.nn.sigmoid(hg)) * hu * gate
        y = jnp.dot(hid.astype(BF16), wd_bf[...], preferred_element_type=F32)
        y_ref[...] = _pack_bf16(y)

    @pl.when(valid <= 0)
    def _():
        y_ref[...] = jnp.zeros_like(y_ref)


def _pack_bf16(x):
    w = x.shape[-1] // 2
    bits = lambda a: pltpu.bitcast(a.astype(BF16).astype(F32), jnp.uint32)
    return jnp.bitwise_or(lax.shift_right_logical(bits(x[:, :w]), jnp.uint32(16)),
                          jnp.bitwise_and(bits(x[:, w:]), jnp.uint32(0xFFFF0000)))


def _unpack_bf16(words):
    as_f32 = lambda a: pltpu.bitcast(a, F32)
    return jnp.concatenate([as_f32(lax.shift_left(words, jnp.uint32(16))),
                            as_f32(jnp.bitwise_and(words, jnp.uint32(0xFFFF0000)))], axis=-1).astype(BF16)


def _experts2(xs_a, xs_b, tile_expert, tile_valid, src, w_gate, w_up, w_down, *, te, layer):
    n_tiles = tile_expert.shape[0]
    wspec = lambda a, b: pl.BlockSpec((None, 1, a, b), lambda i, e, v: (layer, e[i], 0, 0))
    lst = lambda shift: pl.BlockSpec((1, 1, LIST_LANES),
                                     lambda i, e, v: (jnp.minimum(i + shift, n_tiles - 1), 0, 0),
                                     memory_space=pltpu.SMEM)
    grid_spec = pltpu.PrefetchScalarGridSpec(
        num_scalar_prefetch=2,
        grid=(n_tiles,),
        in_specs=[lst(0), lst(1), pl.BlockSpec(memory_space=pl.ANY), pl.BlockSpec(memory_space=pl.ANY),
                  wspec(D_MODEL, D_EXPERT), wspec(D_MODEL, D_EXPERT), wspec(D_EXPERT, D_MODEL)],
        out_specs=pl.BlockSpec((te, PACKED), lambda i, e, v: (i, 0)),
        scratch_shapes=[pltpu.VMEM((2, te, ROW_W), jnp.uint32), pltpu.SemaphoreType.DMA((2,)),
                        pltpu.VMEM((D_MODEL, D_EXPERT), BF16), pltpu.VMEM((D_MODEL, D_EXPERT), BF16),
                        pltpu.VMEM((D_EXPERT, D_MODEL), BF16)])
    return pl.pallas_call(
        functools.partial(_expert_kernel2, rows_a=xs_a.shape[0]),
        grid_spec=grid_spec,
        out_shape=jax.ShapeDtypeStruct((n_tiles * te, PACKED), jnp.uint32),
        compiler_params=_cparams(1),
        name="experts",
    )(tile_expert, tile_valid, src, src, xs_a, xs_b, w_gate, w_up, w_down)


def _combine_kernel2(dst_ref, nxt_ref, h_ref, rec_ref, ys_ref, p_ref, gple_ref, wgate_ref, wproj_ref, out_ref,
                     ybuf, sem):
    i = pl.program_id(0)
    n = pl.num_programs(0)
    slot = i % 2
    tm = h_ref.shape[0]
    slots = ybuf.shape[1]

    def fetch(list_ref, to_slot):
        def body(c, carry):
            _chunk_copy(ys_ref, list_ref[0, 0, c], ybuf.at[to_slot], c, sem.at[to_slot]).start()
            return carry
        lax.fori_loop(0, list_ref[0, 0, LIST_COUNT], body, 0)

    @pl.when(i == 0)
    def _():
        ybuf[...] = jnp.zeros_like(ybuf)
        fetch(dst_ref, 0)

    @pl.when(i + 1 < n)
    def _():
        fetch(nxt_ref, 1 - slot)

    def drain(c, carry):
        _chunk_copy(ys_ref, 0, ybuf.at[slot], 0, sem.at[slot]).wait()
        return carry

    lax.fori_loop(0, dst_ref[0, 0, LIST_COUNT], drain, 0)

    rec = rec_ref[...]
    slot_id = lax.broadcasted_iota(jnp.int32, (tm, slots), 1).astype(F32)
    back = jnp.where(slot_id == rec[:, REC_S1:REC_S1 + 1], 1.0,
                     jnp.where(slot_id == rec[:, REC_S2:REC_S2 + 1], 1.0, 0.0)).astype(BF16)
    h2 = h_ref[...] + jnp.dot(back, _unpack_bf16(ybuf[slot]), preferred_element_type=F32)
    gate = jax.nn.sigmoid(jnp.dot(_rms(h2, gple_ref[...]).astype(BF16), wgate_ref[...], preferred_element_type=F32))
    ple = jnp.dot(p_ref[...].astype(BF16), wproj_ref[...], preferred_element_type=F32)
    out_ref[...] = h2 + ple * gate


def _combine2(h, rec, dst, ys, p, g_ple, wgate_bf, wproj_bf, *, tm, layer):
    n = h.shape[0]
    n_tok = n // tm
    tok = lambda w: pl.BlockSpec((tm, w), lambda i: (i, 0))
    p_spec = pl.BlockSpec((None, tm, D_PLE), lambda i: (layer, i, 0))
    lst = lambda shift: pl.BlockSpec((1, 1, LIST_LANES), lambda i: (jnp.minimum(i + shift, n_tok - 1), 0, 0),
                                     memory_space=pltpu.SMEM)
    return pl.pallas_call(
        _combine_kernel2,
        grid=(n_tok,),
        in_specs=[lst(0), lst(1), tok(D_MODEL), tok(LANES), pl.BlockSpec(memory_space=pl.ANY), p_spec,
                  _full((1, D_MODEL)), _full((D_MODEL, D_MODEL)), _full((D_PLE, D_MODEL))],
        out_specs=tok(D_MODEL),
        out_shape=jax.ShapeDtypeStruct((n, D_MODEL), F32),
        scratch_shapes=[pltpu.VMEM((2, _slots(tm), PACKED), jnp.uint32), pltpu.SemaphoreType.DMA((2,))],
        compiler_params=_cparams(1),
        name="combine",
    )(dst, dst, h, rec, ys, p, g_ple, wgate_bf, wproj_bf)


def _excl_cumsum(x, axis):
    return jnp.cumsum(x, axis=axis) - x


def _chunk_plan(cnt, tile_row0, *, n_pairs, rows_a, zero_row, te):
    n_tok = cnt.shape[0]
    per_tile = te // SUBLANES
    chunks = (cnt + SUBLANES - 1) // SUBLANES
    run0 = _excl_cumsum(chunks, 1)
    n_run = jnp.sum(chunks, axis=1)
    seg0 = _excl_cumsum(chunks, 0)
    total = jnp.sum(chunks, axis=0)
    region = ((total + per_tile - 1) // per_tile) * per_tile
    reg_end = jnp.cumsum(region)
    reg0 = reg_end - region

    n_tiles = -(-(n_pairs + (SUBLANES - 1) * N_EXPERTS * n_tok) // te) + N_EXPERTS
    t0 = jnp.arange(n_tiles, dtype=jnp.int32) * per_tile
    tile_expert = jnp.minimum(jnp.sum((t0[:, None] >= reg_end[None, :]).astype(jnp.int32), axis=1), N_EXPERTS - 1)
    pick = tile_expert[:, None] == jnp.arange(N_EXPERTS, dtype=jnp.int32)[None, :]
    of_tile = lambda v: jnp.sum(jnp.where(pick, v[None, :], 0), axis=1)
    tile_valid = jnp.clip(of_tile(total) - (t0 - of_tile(reg0)), 0, per_tile).astype(jnp.int32)

    q = (t0 - of_tile(reg0))[:, None] + jnp.arange(per_tile, dtype=jnp.int32)[None, :]
    col_of_tile = lambda v: jnp.sum(jnp.where(pick[:, None, :], v[None, :, :], 0), axis=2)
    seg0_t, seg1_t, run0_t = col_of_tile(seg0), col_of_tile(seg0 + chunks), col_of_tile(run0)
    holds = (q[:, :, None] >= seg0_t[:, None, :]) & (q[:, :, None] < seg1_t[:, None, :])
    local = run0_t[:, None, :] + q[:, :, None] - seg0_t[:, None, :]
    row = jnp.asarray(tile_row0, jnp.int32)[None, None, :] + SUBLANES * local
    src = jnp.sum(jnp.where(holds, row, 0), axis=2)
    src = jnp.where(jnp.any(holds, axis=2), src, zero_row)
    n_first = jnp.sum((src < rows_a).astype(jnp.int32), axis=1)
    src = jnp.pad(src, ((0, 0), (0, LIST_LANES - per_tile)))
    src = src.at[:, LIST_COUNT].set(n_first).reshape(n_tiles, 1, LIST_LANES).astype(jnp.int32)

    j = jnp.arange(LIST_LANES, dtype=jnp.int32)[None, :, None]
    inside = (j >= run0[:, None, :]) & (j < (run0 + chunks)[:, None, :])
    base = (reg0[None, :] + seg0 - run0)[:, None, :]
    dst = SUBLANES * jnp.sum(jnp.where(inside, base + j, 0), axis=2)
    dst = dst.at[:, LIST_COUNT].set(n_run).reshape(n_tok, 1, LIST_LANES).astype(jnp.int32)
    return tile_expert.astype(jnp.int32), tile_valid, src, dst


def _moe_ple2(h_a, h_b, p_a, p_b, g_ffn, w_router, before, sel, w_gate, w_up, w_down, g_ple, wgate_bf, wproj_bf,
              *, tm_a, tm_b, te, layer):
    xs_a, rec_a, cnt_a = _sort(h_a, g_ffn, w_router, before, sel, tm=tm_a)
    xs_b, rec_b, cnt_b = _sort(h_b, g_ffn, w_router, before, sel, tm=tm_b)
    t_a, t_b = cnt_a.shape[0], cnt_b.shape[0]
    cnt = jnp.concatenate([cnt_a[:, 0, :N_EXPERTS], cnt_b[:, 0, :N_EXPERTS]], axis=0).astype(jnp.int32)
    rows_a = xs_a.shape[0]
    row0 = np.concatenate([np.arange(t_a) * _slots(tm_a), rows_a + np.arange(t_b) * _slots(tm_b)])
    tile_expert, tile_valid, src, dst = _chunk_plan(
        cnt, row0, n_pairs=2 * (h_a.shape[0] + h_b.shape[0]), rows_a=rows_a,
        zero_row=rows_a + _slots(tm_b) - SUBLANES, te=te)
    ys = _experts2(xs_a, xs_b, tile_expert, tile_valid, src, w_gate, w_up, w_down, te=te, layer=layer)
    out_a = _combine2(h_a, rec_a, dst[:t_a], ys, p_a, g_ple, wgate_bf, wproj_bf, tm=tm_a, layer=layer)
    out_b = _combine2(h_b, rec_b, dst[t_a:], ys, p_b, g_ple, wgate_bf, wproj_bf, tm=tm_b, layer=layer)
    return out_a, out_b


def _bucket_np(dist):
    max_exact = N_BUCKETS // 2
    d_f = np.maximum(dist, 1).astype(np.float32)
    large = max_exact + (np.log(d_f / np.float32(max_exact)) / np.float32(np.log(MAX_DISTANCE / max_exact))
                         * np.float32(N_BUCKETS - max_exact)).astype(np.int32)
    large = np.minimum(large, N_BUCKETS - 1)
    return np.where(dist < max_exact, dist, large).astype(np.int32)


def _bias_from_buckets(rel_bias, bucket, valid):
    onehot = (jnp.asarray(bucket)[..., None] == jnp.arange(N_BUCKETS, dtype=jnp.int32)).astype(F32)
    bias = jnp.einsum("...k,kh->h...", onehot, rel_bias.astype(F32), precision=HIGHEST)
    return jnp.where(jnp.asarray(valid)[None], bias, NEG)


def _block_order(dil):
    g = np.arange(Q_BLOCK) // SUBLANES
    j = np.arange(Q_BLOCK) % SUBLANES
    if dil == 1:
        return 16 * j + g
    if dil == 4:
        return 32 * (g // 4) + 4 * j + g % 4
    return SUBLANES * g + j


def _band_bias(rel_bias, dil):
    mu = _block_order(dil)
    qi = mu[:, None] + Q_BLOCK
    ki = np.concatenate([mu, mu + Q_BLOCK])[None, :]
    off = qi - ki
    valid = (off >= 0) & (off <= N_KEYS)
    bucket = _bucket_np(dil * np.clip(off, 0, N_KEYS))
    first = valid & (np.arange(2 * Q_BLOCK)[None, :] >= Q_BLOCK)
    tables = [_bias_from_buckets(rel_bias, bucket, v).reshape(N_HEADS * Q_BLOCK, 2 * Q_BLOCK) for v in (valid, first)]
    return jnp.stack(tables)


def _sample_tables(rel_bias, w_buf):
    qpos = w_buf + np.arange(SAMPLE_T)[:, None]
    pos = np.arange(w_buf + NEW_COLS)[None, :]
    dist = qpos - pos
    in_seq = (dist >= 0) & (pos < w_buf + SAMPLE_T)
    mult = np.zeros(dist.shape, np.float32)
    for (w, d) in PATTERNS:
        mult += in_seq & (dist % d == 0) & (dist <= w)
    bucket = _bucket_np(np.maximum(dist, 0))
    bias = jnp.transpose(_bias_from_buckets(rel_bias, bucket, mult > 0), (1, 0, 2))
    rows = SAMPLE_T * N_HEADS
    mult_rows = np.broadcast_to(mult[:, None, :], (SAMPLE_T, N_HEADS, mult.shape[-1]))
    return bias.reshape(rows, -1), jnp.asarray(mult_rows.reshape(rows, -1))


PROJ_TM = 256
OUT_TM = 512
PROMPT_TM = 256
PROMPT_TE = 256
ATTN_SUB = 4


def _row_perm(tm):
    a = np.arange(tm)
    src = (a // Q_BLOCK) * Q_BLOCK + 16 * (a % SUBLANES) + (a % Q_BLOCK) // SUBLANES
    perm = np.zeros((tm, tm), np.float32)
    perm[a, src] = 1.0
    return perm


def kernel(x_prompt, x_sample, cache_k, cache_v, state_conv, p_prompt, p_sample, rel_bias, g_mix, w_in, q_gain,
           k_gain, conv_w, g_out_att, g_out_conv, w_out, g_ffn, w_router_group, w_router_expert, w_gate, w_up,
           w_down, g_ple, w_ple_gate, w_ple_proj):
    depth = w_in.shape[0]
    batch, seq, _ = x_prompt.shape
    dec_b, dec_t, _ = x_sample.shape
    w_buf = cache_k.shape[2]
    n_s = dec_b * dec_t
    keep = min(w_buf, seq)
    assert dec_t == SAMPLE_T and w_buf % LANES == 0
    assert seq % (Q_BLOCK * 16 * 2) == 0 and keep % PROJ_TM == 0 and seq % PROMPT_TM == 0
    cache_kt = jnp.transpose(cache_k, (0, 1, 3, 4, 2))
    cache_vt = jnp.transpose(cache_v, (0, 1, 3, 4, 2))

    row = lambda a: a.reshape(1, -1)
    member = (np.arange(ATT_DIM)[:, None] // HEAD_DIM == np.arange(LANES)[None, :])
    head_mean = jnp.asarray(member / HEAD_DIM, BF16)
    head_exp = jnp.asarray(member.T, BF16)
    src_lane = np.arange(LANES)
    expand = jnp.asarray((src_lane[:, None] // LSE_LANES_PER_HEAD == np.arange(ATT_DIM)[None, :] // HEAD_DIM)
                         & (src_lane[:, None] % LSE_LANES_PER_HEAD == 0), BF16)
    before = jnp.asarray(np.arange(LANES)[:, None] < np.arange(LANES)[None, :], BF16)
    sel_np = np.zeros((SUBLANES, LANES), np.float32)
    sel_np[0, REC_S1] = sel_np[1, REC_S2] = 1.0
    sel = jnp.asarray(sel_np, BF16)
    perm = jnp.asarray(_row_perm(PROJ_TM), BF16)
    unperm = jnp.asarray(_row_perm(OUT_TM).T, BF16)
    band = [_band_bias(rel_bias, d) for (_, d) in PATTERNS]
    s_bias, s_mult = _sample_tables(rel_bias, w_buf)

    hp = x_prompt.reshape(batch * seq, D_MODEL)
    hs = jnp.swapaxes(x_sample, 0, 1).reshape(n_s, D_MODEL)
    pp_all = p_prompt.reshape(depth, batch * seq, D_PLE)
    ps_all = jnp.swapaxes(p_sample, 1, 2).reshape(depth, n_s, D_PLE)
    new = {k: [] for k in ("kp", "vp", "cp", "ks", "vs", "cs")}
    hist_p = jnp.zeros((batch, SUBLANES, CONV_DIM), F32)

    for l in range(depth):
        w_in_bf = w_in[l].astype(BF16)
        wa_bf = w_out[l, :ATT_DIM].astype(BF16)
        wc_bf = w_out[l, ATT_DIM:].astype(BF16)
        wgate_bf = w_ple_gate[l].astype(BF16)
        wproj_bf = w_ple_proj[l].astype(BF16)
        w_router = jnp.concatenate(
            [w_router_expert[l], w_router_group[l],
             jnp.zeros((D_MODEL, LANES - N_EXPERTS - N_GROUPS), F32)], axis=1).astype(BF16)
        qg, kg = row(jnp.tile(q_gain[l], N_HEADS)), row(jnp.tile(k_gain[l], N_HEADS))
        mix = (row(g_mix[l]), w_in_bf, qg, kg, head_mean, head_exp, conv_w[l], row(g_out_conv[l]))
        moe = (row(g_ffn[l]), w_router, before, sel, w_gate, w_up, w_down, row(g_ple[l]), wgate_bf, wproj_bf)

        q, k, v, k_nat, v_nat, yn, nconv = _inproj(
            hp, hist_p, *mix, perm, tm=PROJ_TM, shift=1, tiles_per_seq=seq // PROJ_TM,
            keep_tiles=keep // PROJ_TM)
        os, lses = [], []
        for bias, (_, d) in zip(band, PATTERNS):
            sub = min(ATTN_SUB, seq // (Q_BLOCK * d))
            o, lse = _attn_pattern(q, k, v, bias, batch=batch, seq=seq, dil=d, sub=sub,
                                   res=min(d, ATTN_SUB // sub))
            os.append(o)
            lses.append(lse)
        hp = _outproj(os, lses, yn, hp, row(g_out_att[l]), wa_bf, wc_bf, expand, unperm, tm=OUT_TM)
        new["kp"].append(jnp.transpose(k_nat, (0, 3, 1, 2)))
        new["vp"].append(jnp.transpose(v_nat, (0, 3, 1, 2)))
        new["cp"].append(nconv[:, SUBLANES - 2:])

        hist_s = jnp.swapaxes(state_conv[l], 0, 1).reshape(1, 2 * dec_b, CONV_DIM)
        q, k, v, yn, nconv = _inproj(hs, hist_s, *mix, tm=n_s, shift=dec_b, tiles_per_seq=1)
        bmaj = lambda a: jnp.swapaxes(a.reshape(dec_t, dec_b, N_HEADS, HEAD_DIM), 0, 1)
        qb, kb, vb = bmaj(q), bmaj(k), bmaj(v)
        rows8 = lambda a: jnp.pad(a.reshape(dec_b, dec_t, ATT_DIM), ((0, 0), (0, SUBLANES - dec_t), (0, 0)))
        att = _attn_sample(qb.reshape(dec_b, dec_t, ATT_DIM), rows8(kb), rows8(vb), cache_kt, cache_vt, l,
                           s_bias, s_mult)
        att_tm = jnp.swapaxes(att, 0, 1).reshape(n_s, ATT_DIM)
        hs = _outproj([att_tm], [], yn, hs, row(g_out_att[l]), wa_bf, wc_bf, tm=n_s)
        hp, hs = _moe_ple2(hp, hs, pp_all, ps_all, *moe, tm_a=PROMPT_TM, tm_b=n_s, te=PROMPT_TE, layer=l)
        new["ks"].append(kb)
        new["vs"].append(vb)
        new["cs"].append(jnp.swapaxes(nconv.reshape(2, dec_b, CONV_DIM), 0, 1))

    y_prompt = hp.reshape(batch, seq, D_MODEL)
    y_sample = jnp.swapaxes(hs.reshape(dec_t, dec_b, D_MODEL), 0, 1)
    st = lambda key: jnp.stack(new[key])
    return (y_prompt, y_sample, st("kp"), st("vp"), st("cp"), st("ks"), st("vs"), st("cs"))
```

```python
import functools

import jax
import jax.numpy as jnp
import numpy as np
from jax import lax
from jax.experimental import pallas as pl
from jax.experimental.pallas import tpu as pltpu

F32 = jnp.float32
BF16 = jnp.bfloat16
HIGHEST = lax.Precision.HIGHEST

D_MODEL = 1024
HEAD_DIM = 64
N_HEADS = 8
ATT_DIM = N_HEADS * HEAD_DIM
CONV_DIM = D_MODEL - ATT_DIM
MIX_IN = 3 * ATT_DIM + 3 * CONV_DIM
PATTERNS = ((128, 1), (512, 4), (2048, 16))
N_KEYS = 128
Q_BLOCK = 128
N_BUCKETS = 32
MAX_DISTANCE = 2048
N_GROUPS = 4
EXPERTS_PER_GROUP = 8
N_EXPERTS = N_GROUPS * EXPERTS_PER_GROUP
D_EXPERT = 256
D_PLE = 256
EPS = 1e-6
NEG = -1e30

LANES = 128
SUBLANES = 8
SLABS = Q_BLOCK // SUBLANES
LSE_LANES_PER_HEAD = LANES // N_HEADS
VMEM_LIMIT = 56 * 1024 * 1024
NT = (((1,), (1,)), ((), ()))


def _cparams(n_axes):
    return pltpu.CompilerParams(dimension_semantics=("arbitrary",) * n_axes,
                                vmem_limit_bytes=VMEM_LIMIT)


def _full(shape):
    n = len(shape)
    return pl.BlockSpec(shape, lambda *_: (0,) * n)


def _rms(x, gain):
    ms = jnp.mean(x * x, axis=-1, keepdims=True)
    return x * lax.rsqrt(ms + EPS) * gain


def _exact_dot(x, e_ref):
    hi = x.astype(BF16)
    r1 = x - hi.astype(F32)
    mid = r1.astype(BF16)
    lo = (r1 - mid.astype(F32)).astype(BF16)
    e = e_ref[...]
    return (jnp.dot(hi, e, preferred_element_type=F32) + jnp.dot(mid, e, preferred_element_type=F32)
            + jnp.dot(lo, e, preferred_element_type=F32))


def _inproj_kernel(*refs, shift, tiles_per_seq, permute):
    (h_ref, gmix_ref, w_ref, qg_ref, kg_ref, cw_ref, gconv_ref, hist_ref) = refs[:8]
    if permute:
        perm_ref, q_ref, k_ref, v_ref, kn_ref, vn_ref, yn_ref, nconv_ref, carry_ref = refs[8:]
    else:
        q_ref, k_ref, v_ref, yn_ref, nconv_ref, carry_ref = refs[8:]
    i = pl.program_id(0)
    a = _rms(h_ref[...], gmix_ref[...])
    proj = jnp.dot(a.astype(BF16), w_ref[...], preferred_element_type=F32)
    tm = proj.shape[0]

    lower = lax.broadcasted_iota(jnp.int32, (tm, LANES), 1) < HEAD_DIM

    def head_norm(t, g):
        out = []
        for j in range(ATT_DIM // LANES):
            blk = t[:, j * LANES:(j + 1) * LANES]
            sq = blk * blk
            ms_lo = jnp.sum(jnp.where(lower, sq, 0.0), axis=-1, keepdims=True) * (1.0 / HEAD_DIM)
            ms_hi = jnp.sum(jnp.where(lower, 0.0, sq), axis=-1, keepdims=True) * (1.0 / HEAD_DIM)
            scale = jnp.where(lower, lax.rsqrt(ms_lo + EPS), lax.rsqrt(ms_hi + EPS))
            out.append(blk * scale)
        return jnp.concatenate(out, axis=-1) * g

    q = head_norm(proj[:, 0:ATT_DIM], qg_ref[...])
    k = head_norm(proj[:, ATT_DIM:2 * ATT_DIM], kg_ref[...])
    v = proj[:, 2 * ATT_DIM:3 * ATT_DIM]
    if permute:
        kn_ref[...] = k.T.reshape(N_HEADS, HEAD_DIM, tm)
        vn_ref[...] = v.T.reshape(N_HEADS, HEAD_DIM, tm)
        qkv = jnp.concatenate([q * (HEAD_DIM ** -0.5), k, v], axis=-1).astype(BF16)
        moved = jnp.dot(perm_ref[...], qkv, preferred_element_type=F32)
        q_ref[...] = moved[:, 0:ATT_DIM]
        k_ref[...] = moved[:, ATT_DIM:2 * ATT_DIM]
        v_ref[...] = moved[:, 2 * ATT_DIM:3 * ATT_DIM]
    else:
        q_ref[...] = q
        k_ref[...] = k
        v_ref[...] = v
    c0 = 3 * ATT_DIM
    hc = proj[:, c0:c0 + CONV_DIM]
    gb = proj[:, c0 + CONV_DIM:c0 + 2 * CONV_DIM]
    gc = proj[:, c0 + 2 * CONV_DIM:c0 + 3 * CONV_DIM]
    u = gc * hc

    if shift == 1:
        @pl.when(i % tiles_per_seq == 0)
        def _():
            carry_ref[...] = hist_ref[0]
        h0 = carry_ref[SUBLANES - 2:SUBLANES - 1, :]
        h1 = carry_ref[SUBLANES - 1:SUBLANES, :]
        row = lax.broadcasted_iota(jnp.int32, (tm, 1), 0)
        u1 = jnp.where(row == 0, h1, pltpu.roll(u, 1, 0))
        u2 = jnp.where(row == 0, h0, jnp.where(row == 1, h1, pltpu.roll(u, 2, 0)))
        carry_ref[...] = u[tm - SUBLANES:tm, :]
        nconv_ref[0] = u[tm - SUBLANES:tm, :]
    else:
        hist = hist_ref[0]
        u1 = jnp.concatenate([hist[shift:2 * shift], u[0:tm - shift]], axis=0)
        u2 = jnp.concatenate([hist, u[0:tm - 2 * shift]], axis=0)
        nconv_ref[0] = u[tm - 2 * shift:tm, :]
    conv = cw_ref[0:1, :] * u2 + cw_ref[1:2, :] * u1 + cw_ref[2:3, :] * u
    yn_ref[...] = _rms(gb * conv, gconv_ref[...])


def _inproj(h, hist, g_mix, w_in_bf, q_gain, k_gain, conv_w, g_out_conv, perm=None, *,
            tm, shift, tiles_per_seq, keep_tiles=0):
    n = h.shape[0]
    hist_rows = hist.shape[1]
    nseq = hist.shape[0]
    tok = lambda w: pl.BlockSpec((tm, w), lambda i: (i, 0))
    seq3 = lambda r: pl.BlockSpec((1, r, CONV_DIM), lambda i: (i // tiles_per_seq, 0, 0))
    nconv_rows = SUBLANES if shift == 1 else 2 * shift
    att = jax.ShapeDtypeStruct((n, ATT_DIM), F32)
    in_specs = [tok(D_MODEL), _full((1, D_MODEL)), _full((D_MODEL, MIX_IN)), _full((1, ATT_DIM)),
                _full((1, ATT_DIM)), _full((3, CONV_DIM)), _full((1, CONV_DIM)), seq3(hist_rows)]
    args = [h, g_mix, w_in_bf, q_gain, k_gain, conv_w, g_out_conv, hist]
    out_specs = [tok(ATT_DIM)] * 3
    out_shape = [att] * 3
    if perm is not None:
        in_specs.append(_full((tm, tm)))
        args.append(perm)
        first = tiles_per_seq - keep_tiles
        kept = pl.BlockSpec((None, N_HEADS, HEAD_DIM, tm),
                            lambda i: (i // tiles_per_seq, 0, 0, jnp.maximum(i % tiles_per_seq - first, 0)))
        out_specs += [kept, kept]
        out_shape += [jax.ShapeDtypeStruct((nseq, N_HEADS, HEAD_DIM, keep_tiles * tm), F32)] * 2
    out_specs += [tok(CONV_DIM), seq3(nconv_rows)]
    out_shape += [jax.ShapeDtypeStruct((n, CONV_DIM), F32),
                  jax.ShapeDtypeStruct((nseq, nconv_rows, CONV_DIM), F32)]
    return pl.pallas_call(
        functools.partial(_inproj_kernel, shift=shift, tiles_per_seq=tiles_per_seq, permute=perm is not None),
        grid=(n // tm,),
        in_specs=in_specs, out_specs=out_specs, out_shape=out_shape,
        scratch_shapes=[pltpu.VMEM((SUBLANES, CONV_DIM), F32)],
        compiler_params=_cparams(1),
        name="inproj",
    )(*args)


def _attn_block(q, kk, vv, bias):
    lane = lax.broadcasted_iota(jnp.int32, (Q_BLOCK, LANES), 1)
    upper = lane >= HEAD_DIM
    scores = []
    for h in range(N_HEADS):
        j, e = divmod(h, 2)
        qp = q[:, j * LANES:(j + 1) * LANES]
        qm = (jnp.where(upper, qp, 0.0) if e else jnp.where(upper, 0.0, qp)).astype(BF16)
        scores.append(lax.dot_general(qm, kk[:, j * LANES:(j + 1) * LANES], NT, preferred_element_type=F32))
    s = jnp.concatenate(scores, axis=0) + bias
    m = jnp.max(s, axis=-1, keepdims=True)
    p = jnp.exp(s - m)
    den = jnp.sum(p, axis=-1, keepdims=True)
    pb = p.astype(BF16)
    inv = 1.0 / den
    lse = m + jnp.log(den)
    lse_grp = lane // LSE_LANES_PER_HEAD
    lse_tile = jnp.zeros((Q_BLOCK, LANES), F32)
    outs = []
    for j in range(N_HEADS // 2):
        pair = None
        for e in range(2):
            h = 2 * j + e
            rows = slice(h * Q_BLOCK, (h + 1) * Q_BLOCK)
            o = jnp.dot(pb[rows], vv[:, j * LANES:(j + 1) * LANES], preferred_element_type=F32) * inv[rows]
            pair = o if e == 0 else jnp.where(upper, o, pair)
            lse_tile = jnp.where(lse_grp == h, lse[rows], lse_tile)
        outs.append(pair)
    return jnp.concatenate(outs, axis=-1), lse_tile


def _attn_kernel(q_ref, kp_ref, kc_ref, vp_ref, vc_ref, bias_ref, o_ref, lse_ref, kbuf, vbuf, *, sub, res):
    n = pl.program_id(2)
    rows = sub * Q_BLOCK
    for r in range(res):
        kbuf[0:Q_BLOCK, :] = kp_ref[:, r].reshape(Q_BLOCK, ATT_DIM).astype(BF16)
        kbuf[Q_BLOCK:, :] = kc_ref[:, :, r].reshape(rows, ATT_DIM).astype(BF16)
        vbuf[0:Q_BLOCK, :] = vp_ref[:, r].reshape(Q_BLOCK, ATT_DIM).astype(BF16)
        vbuf[Q_BLOCK:, :] = vc_ref[:, :, r].reshape(rows, ATT_DIM).astype(BF16)
        for j in range(sub):
            q = q_ref[j, :, r].reshape(Q_BLOCK, ATT_DIM)
            r0 = j * Q_BLOCK
            first = (n == 0).astype(jnp.int32) if j == 0 else 0
            o, lse = _attn_block(q, kbuf[r0:r0 + 2 * Q_BLOCK, :], vbuf[r0:r0 + 2 * Q_BLOCK, :], bias_ref[first])
            o_ref[j, :, r] = o.reshape(SLABS, SUBLANES, ATT_DIM)
            lse_ref[j, :, r] = lse.reshape(SLABS, SUBLANES, LANES)


def _attn_pattern(q, k, v, bias, *, batch, seq, dil, sub, res):
    nblk = seq // (Q_BLOCK * dil)
    view = lambda t: t.reshape(batch, nblk, SLABS, dil, SUBLANES, t.shape[-1])
    cur = lambda c: pl.BlockSpec((None, sub, SLABS, res, SUBLANES, c), lambda b, r, n: (b, n, 0, r, 0, 0))
    prev = pl.BlockSpec((None, None, SLABS, res, SUBLANES, ATT_DIM),
                        lambda b, r, n: (b, jnp.maximum(n * sub - 1, 0), 0, r, 0, 0))
    o, lse = pl.pallas_call(
        functools.partial(_attn_kernel, sub=sub, res=res),
        grid=(batch, dil // res, nblk // sub),
        in_specs=[cur(ATT_DIM), prev, cur(ATT_DIM), prev, cur(ATT_DIM),
                  _full((2, N_HEADS * Q_BLOCK, 2 * Q_BLOCK))],
        out_specs=[cur(ATT_DIM), cur(LANES)],
        out_shape=[jax.ShapeDtypeStruct((batch, nblk, SLABS, dil, SUBLANES, ATT_DIM), F32),
                   jax.ShapeDtypeStruct((batch, nblk, SLABS, dil, SUBLANES, LANES), F32)],
        scratch_shapes=[pltpu.VMEM(((sub + 1) * Q_BLOCK, ATT_DIM), BF16),
                        pltpu.VMEM(((sub + 1) * Q_BLOCK, ATT_DIM), BF16)],
        compiler_params=_cparams(3),
        name=f"attn_d{dil}",
    )(view(q), view(k), view(k), view(v), view(v), bias)
    return o.reshape(batch * seq, ATT_DIM), lse.reshape(batch * seq, LANES)


SAMPLE_T = 4
NEW_COLS = LANES


def _attn_sample_kernel(q_ref, kt_ref, kn_ref, vt_ref, vn_ref, bias_ref, mult_ref, o_ref):
    rows = SAMPLE_T * N_HEADS
    q4 = q_ref[...] * (HEAD_DIM ** -0.5)
    qt = jnp.concatenate([jnp.broadcast_to(q4[t:t + 1, :], (N_HEADS, ATT_DIM)) for t in range(SAMPLE_T)], axis=0)
    lane_head = lax.broadcasted_iota(jnp.int32, (rows, ATT_DIM), 1) // HEAD_DIM
    row_head = lax.broadcasted_iota(jnp.int32, (rows, ATT_DIM), 0) % N_HEADS
    own = lane_head == row_head
    qbd = jnp.where(own, qt, 0.0).astype(BF16)
    flat = lambda ref: ref[...].reshape(ATT_DIM, ref.shape[-1]).astype(BF16)
    pad = jnp.zeros((NEW_COLS - SUBLANES, ATT_DIM), F32)
    new_rows = lambda ref: jnp.concatenate([ref[...], pad], axis=0).astype(BF16)
    s = jnp.concatenate([jnp.dot(qbd, flat(kt_ref), preferred_element_type=F32),
                         lax.dot_general(qbd, new_rows(kn_ref), NT, preferred_element_type=F32)],
                        axis=-1) + bias_ref[...]
    m = jnp.max(s, axis=-1, keepdims=True)
    p = jnp.exp(s - m) * mult_ref[...]
    den = jnp.sum(p, axis=-1, keepdims=True)
    pb = p.astype(BF16)
    w_buf = kt_ref.shape[-1]
    acc = (lax.dot_general(pb[:, :w_buf], flat(vt_ref), NT, preferred_element_type=F32)
           + jnp.dot(pb[:, w_buf:], new_rows(vn_ref), preferred_element_type=F32))
    acc = jnp.where(own, acc / den, 0.0)
    for t in range(SAMPLE_T):
        o_ref[t:t + 1, :] = jnp.sum(acc[t * N_HEADS:(t + 1) * N_HEADS, :], axis=0, keepdims=True)


def _attn_sample(q, k_new, v_new, cache_kt, cache_vt, layer, bias, mult):
    nb = q.shape[0]
    w_buf = cache_kt.shape[-1]
    tok = pl.BlockSpec((None, SAMPLE_T, ATT_DIM), lambda b: (b, 0, 0))
    new = pl.BlockSpec((None, SUBLANES, ATT_DIM), lambda b: (b, 0, 0))
    old = pl.BlockSpec((None, None, N_HEADS, HEAD_DIM, w_buf), lambda b: (layer, b, 0, 0, 0))
    tbl = _full((SAMPLE_T * N_HEADS, w_buf + NEW_COLS))
    return pl.pallas_call(
        _attn_sample_kernel,
        grid=(nb,),
        in_specs=[tok, old, new, old, new, tbl, tbl],
        out_specs=tok,
        out_shape=jax.ShapeDtypeStruct((nb, SAMPLE_T, ATT_DIM), F32),
        compiler_params=_cparams(1),
        name="attn_sample",
    )(q, cache_kt, k_new, cache_vt, v_new, bias, mult)


def _split_dot(x, e_ref):
    hi = x.astype(BF16)
    lo = (x - hi.astype(F32)).astype(BF16)
    return (jnp.dot(hi, e_ref[...], preferred_element_type=F32)
            + jnp.dot(lo, e_ref[...], preferred_element_type=F32))


def _outproj_kernel(*refs, n_pat):
    mix = n_pat > 1
    n_lse = n_pat if mix else 0
    o_refs = refs[0:n_pat]
    l_refs = refs[n_pat:n_pat + n_lse]
    rest = refs[n_pat + n_lse:]
    if mix:
        yn_ref, h_ref, gatt_ref, exp_ref, unperm_ref, wa_ref, wc_ref, out_ref = rest
        lses = [r[...] for r in l_refs]
        top = functools.reduce(jnp.maximum, lses)
        ws = [jnp.exp(l - top) for l in lses]
        tot = functools.reduce(lambda a, b: a + b, ws)
        att = None
        for w, o_ref in zip(ws, o_refs):
            term = _split_dot(w / tot, exp_ref) * o_ref[...]
            att = term if att is None else att + term
        att_bf = jnp.dot(unperm_ref[...], _rms(att, gatt_ref[...]).astype(BF16),
                         preferred_element_type=F32).astype(BF16)
    else:
        yn_ref, h_ref, gatt_ref, wa_ref, wc_ref, out_ref = rest
        att_bf = _rms(o_refs[0][...], gatt_ref[...]).astype(BF16)
    y = (jnp.dot(att_bf, wa_ref[...], preferred_element_type=F32)
         + jnp.dot(yn_ref[...].astype(BF16), wc_ref[...], preferred_element_type=F32))
    out_ref[...] = h_ref[...] + y


def _outproj(os, lses, yn, h, g_att, wa_bf, wc_bf, expand=None, unperm=None, *, tm):
    n = h.shape[0]
    n_pat = len(os)
    tok = lambda w: pl.BlockSpec((tm, w), lambda i: (i, 0))
    in_specs = [tok(ATT_DIM)] * n_pat + [tok(LANES)] * len(lses) + [tok(CONV_DIM), tok(D_MODEL), _full((1, ATT_DIM))]
    args = [*os, *lses, yn, h, g_att]
    if n_pat > 1:
        in_specs += [_full((LANES, ATT_DIM)), _full((tm, tm))]
        args += [expand, unperm]
    in_specs += [_full((ATT_DIM, D_MODEL)), _full((CONV_DIM, D_MODEL))]
    args += [wa_bf, wc_bf]
    return pl.pallas_call(
        functools.partial(_outproj_kernel, n_pat=n_pat),
        grid=(n // tm,),
        in_specs=in_specs,
        out_specs=tok(D_MODEL),
        out_shape=jax.ShapeDtypeStruct((n, D_MODEL), F32),
        compiler_params=_cparams(1),
        name="outproj",
    )(*args)


ROUTE_I1, ROUTE_I2, ROUTE_R1, ROUTE_R2, ROUTE_W1, ROUTE_W2 = range(6)
GROUP_LANE0 = N_EXPERTS


def _route_kernel(h_ref, g_ref, wr_ref, route_ref, cnt_ref, carry_ref):
    i = pl.program_id(0)

    @pl.when(i == 0)
    def _():
        carry_ref[...] = jnp.zeros_like(carry_ref)

    m = _rms(h_ref[...], g_ref[...])
    logits = jnp.dot(m.astype(BF16), wr_ref[...], preferred_element_type=F32)
    tm = logits.shape[0]
    lane_i = lax.broadcasted_iota(jnp.int32, (tm, LANES), 1)
    lane = lane_i.astype(F32)
    big = jnp.float32(4 * LANES)

    is_g = jnp.logical_and(lane_i >= GROUP_LANE0, lane_i < GROUP_LANE0 + N_GROUPS)
    gl = jnp.where(is_g, logits, NEG)
    gmax = jnp.max(gl, axis=-1, keepdims=True)
    g_w = 1.0 / jnp.sum(jnp.where(is_g, jnp.exp(gl - gmax), 0.0), axis=-1, keepdims=True)
    g_sel = jnp.min(jnp.where(gl == gmax, lane - GROUP_LANE0, big), axis=-1, keepdims=True)

    grp_of_lane = (lane_i // EXPERTS_PER_GROUP).astype(F32)
    in_grp = jnp.logical_and(lane_i < N_EXPERTS, grp_of_lane == g_sel)
    el = jnp.where(in_grp, logits, NEG)
    t1 = jnp.max(el, axis=-1, keepdims=True)
    i1 = jnp.min(jnp.where(el == t1, lane, big), axis=-1, keepdims=True)
    el2 = jnp.where(lane == i1, NEG, el)
    t2 = jnp.max(el2, axis=-1, keepdims=True)
    i2 = jnp.min(jnp.where(el2 == t2, lane, big), axis=-1, keepdims=True)
    e2 = jnp.exp(t2 - t1)
    w1 = g_w / (1.0 + e2)
    w2 = g_w * e2 / (1.0 + e2)

    hit1 = lane == i1
    hit2 = lane == i2
    c = jnp.where(jnp.logical_or(hit1, hit2), 1.0, 0.0)
    rr = lax.broadcasted_iota(jnp.int32, (tm, tm), 0)
    cc = lax.broadcasted_iota(jnp.int32, (tm, tm), 1)
    lower = jnp.where(rr > cc, 1.0, 0.0).astype(BF16)
    before = jnp.dot(lower, c.astype(BF16), preferred_element_type=F32) + carry_ref[0:1, :]
    r1 = jnp.sum(jnp.where(hit1, before, 0.0), axis=-1, keepdims=True)
    r2 = jnp.sum(jnp.where(hit2, before, 0.0), axis=-1, keepdims=True)
    total = carry_ref[0:1, :] + jnp.sum(c, axis=0, keepdims=True)
    carry_ref[...] = jnp.broadcast_to(total, carry_ref.shape)
    cnt_ref[...] = jnp.broadcast_to(total, cnt_ref.shape)

    rec = jnp.zeros((tm, LANES), F32)
    for idx, val in ((ROUTE_I1, i1), (ROUTE_I2, i2), (ROUTE_R1, r1), (ROUTE_R2, r2), (ROUTE_W1, w1), (ROUTE_W2, w2)):
        rec = jnp.where(lane_i == idx, val, rec)
    route_ref[...] = rec


def _route(h, g_ffn, w_router, *, tm):
    n = h.shape[0]
    tok = lambda w: pl.BlockSpec((tm, w), lambda i: (i, 0))
    return pl.pallas_call(
        _route_kernel,
        grid=(n // tm,),
        in_specs=[tok(D_MODEL), _full((1, D_MODEL)), _full((D_MODEL, LANES))],
        out_specs=[tok(LANES), _full((SUBLANES, LANES))],
        out_shape=[jax.ShapeDtypeStruct((n, LANES), F32), jax.ShapeDtypeStruct((SUBLANES, LANES), F32)],
        scratch_shapes=[pltpu.VMEM((SUBLANES, LANES), F32)],
        compiler_params=_cparams(1),
        name="route",
    )(h, g_ffn, w_router)


def _dispatch_kernel(pos_ref, h_ref, g_ref, xs_ref, mbuf, sem):
    tm = mbuf.shape[0]
    mbuf[...] = _rms(h_ref[...], g_ref[...])

    def row_copy(t, p):
        return pltpu.make_async_copy(mbuf.at[pl.ds(t, 1), :], xs_ref.at[pl.ds(p, 1), :], sem)

    def issue(t, carry):
        row_copy(t, pos_ref[0, 0, 2 * t]).start()
        row_copy(t, pos_ref[0, 0, 2 * t + 1]).start()
        return carry

    lax.fori_loop(0, tm, issue, 0)

    def drain(t, carry):
        row_copy(0, 0).wait()
        row_copy(0, 0).wait()
        return carry

    lax.fori_loop(0, tm, drain, 0)


def _dispatch(h, g_ffn, pos, n_rows, *, tm):
    n = h.shape[0]
    return pl.pallas_call(
        _dispatch_kernel,
        grid=(n // tm,),
        in_specs=[pl.BlockSpec((1, 1, 2 * tm), lambda i: (i, 0, 0), memory_space=pltpu.SMEM),
                  pl.BlockSpec((tm, D_MODEL), lambda i: (i, 0)), _full((1, D_MODEL))],
        out_specs=pl.BlockSpec(memory_space=pl.ANY),
        out_shape=jax.ShapeDtypeStruct((n_rows, D_MODEL), F32),
        scratch_shapes=[pltpu.VMEM((tm, D_MODEL), F32), pltpu.SemaphoreType.DMA(())],
        compiler_params=pltpu.CompilerParams(dimension_semantics=("arbitrary",), vmem_limit_bytes=VMEM_LIMIT,
                                             has_side_effects=True),
        name="dispatch",
    )(pos, h, g_ffn)


def _expert_kernel(te_ref, tv_ref, x_ref, wg_ref, wu_ref, wd_ref, y_ref, wg_bf, wu_bf, wd_bf):
    i = pl.program_id(0)
    valid = tv_ref[i]
    changed = jnp.logical_or(i == 0, te_ref[i] != te_ref[jnp.maximum(i - 1, 0)])

    @pl.when(jnp.logical_and(changed, valid > 0))
    def _():
        wg_bf[...] = wg_ref[0].astype(BF16)
        wu_bf[...] = wu_ref[0].astype(BF16)
        wd_bf[...] = wd_ref[0].astype(BF16)

    @pl.when(valid > 0)
    def _():
        te = x_ref.shape[0]
        row = lax.broadcasted_iota(jnp.int32, (te, 1), 0)
        x = jnp.where(row < valid, x_ref[...], 0.0).astype(BF16)
        hg = jnp.dot(x, wg_bf[...], preferred_element_type=F32)
        hu = jnp.dot(x, wu_bf[...], preferred_element_type=F32)
        hid = (hg * jax.nn.sigmoid(hg)) * hu
        y_ref[...] = jnp.dot(hid.astype(BF16), wd_bf[...], preferred_element_type=F32)

    @pl.when(valid <= 0)
    def _():
        y_ref[...] = jnp.zeros_like(y_ref)


def _experts(xs, tile_expert, tile_valid, w_gate, w_up, w_down, *, te):
    n_rows = xs.shape[0]
    wspec = lambda a, b: pl.BlockSpec((1, a, b), lambda i, e, v: (e[i], 0, 0))
    grid_spec = pltpu.PrefetchScalarGridSpec(
        num_scalar_prefetch=2,
        grid=(n_rows // te,),
        in_specs=[pl.BlockSpec((te, D_MODEL), lambda i, e, v: (i, 0)),
                  wspec(D_MODEL, D_EXPERT), wspec(D_MODEL, D_EXPERT), wspec(D_EXPERT, D_MODEL)],
        out_specs=pl.BlockSpec((te, D_MODEL), lambda i, e, v: (i, 0)),
        scratch_shapes=[pltpu.VMEM((D_MODEL, D_EXPERT), BF16), pltpu.VMEM((D_MODEL, D_EXPERT), BF16),
                        pltpu.VMEM((D_EXPERT, D_MODEL), BF16)])
    return pl.pallas_call(
        _expert_kernel,
        grid_spec=grid_spec,
        out_shape=jax.ShapeDtypeStruct((n_rows, D_MODEL), F32),
        compiler_params=_cparams(1),
        name="experts",
    )(tile_expert, tile_valid, xs, w_gate, w_up, w_down)


def _combine_kernel(pos_ref, h_ref, route_ref, ys_ref, p_ref, gple_ref, wgate_ref, wproj_ref, out_ref,
                    y0, y1, sem):
    tm = y0.shape[0]

    def row_copy(p, dst, t):
        return pltpu.make_async_copy(ys_ref.at[pl.ds(p, 1), :], dst.at[pl.ds(t, 1), :], sem)

    def issue(t, carry):
        row_copy(pos_ref[0, 0, 2 * t], y0, t).start()
        row_copy(pos_ref[0, 0, 2 * t + 1], y1, t).start()
        return carry

    lax.fori_loop(0, tm, issue, 0)

    def drain(t, carry):
        row_copy(0, y0, 0).wait()
        row_copy(0, y1, 0).wait()
        return carry

    lax.fori_loop(0, tm, drain, 0)

    rec = route_ref[...]
    w1 = rec[:, ROUTE_W1:ROUTE_W1 + 1]
    w2 = rec[:, ROUTE_W2:ROUTE_W2 + 1]
    h2 = h_ref[...] + w1 * y0[...] + w2 * y1[...]
    gate = jax.nn.sigmoid(jnp.dot(_rms(h2, gple_ref[...]).astype(BF16), wgate_ref[...], preferred_element_type=F32))
    ple = jnp.dot(p_ref[...].astype(BF16), wproj_ref[...], preferred_element_type=F32)
    out_ref[...] = h2 + ple * gate


def _combine(h, route, pos, ys, p, g_ple, wgate_bf, wproj_bf, *, tm):
    n = h.shape[0]
    tok = lambda w: pl.BlockSpec((tm, w), lambda i: (i, 0))
    return pl.pallas_call(
        _combine_kernel,
        grid=(n // tm,),
        in_specs=[pl.BlockSpec((1, 1, 2 * tm), lambda i: (i, 0, 0), memory_space=pltpu.SMEM),
                  tok(D_MODEL), tok(LANES), pl.BlockSpec(memory_space=pl.ANY), tok(D_PLE),
                  _full((1, D_MODEL)), _full((D_MODEL, D_MODEL)), _full((D_PLE, D_MODEL))],
        out_specs=tok(D_MODEL),
        out_shape=jax.ShapeDtypeStruct((n, D_MODEL), F32),
        scratch_shapes=[pltpu.VMEM((tm, D_MODEL), F32), pltpu.VMEM((tm, D_MODEL), F32),
                        pltpu.SemaphoreType.DMA(())],
        compiler_params=_cparams(1),
        name="combine",
    )(pos, h, route, ys, p, g_ple, wgate_bf, wproj_bf)


def _lookup(table, idx, size):
    hit = idx[..., None] == jnp.arange(size, dtype=jnp.int32)
    return jnp.sum(jnp.where(hit, table, 0), axis=-1)


def _moe_ple(h, p, g_ffn, w_router, w_gate, w_up, w_down, g_ple, wgate_bf, wproj_bf, *, tm, te):
    n = h.shape[0]
    route, counts = _route(h, g_ffn, w_router, tm=tm)
    counts = counts[0, :N_EXPERTS].astype(jnp.int32)
    padded = ((counts + te - 1) // te) * te
    ends = jnp.cumsum(padded)
    offs = ends - padded
    ids = route[:, ROUTE_I1:ROUTE_I2 + 1].astype(jnp.int32)
    ranks = route[:, ROUTE_R1:ROUTE_R2 + 1].astype(jnp.int32)
    pos = (_lookup(offs, ids, N_EXPERTS) + ranks).reshape(n // tm, 1, 2 * tm)
    n_tiles = (2 * n) // te + N_EXPERTS
    starts = jnp.arange(n_tiles, dtype=jnp.int32) * te
    tile_expert = jnp.minimum(jnp.sum((starts[:, None] >= ends[None, :]).astype(jnp.int32), axis=-1), N_EXPERTS - 1)
    tile_valid = jnp.clip(_lookup(counts, tile_expert, N_EXPERTS)
                          - (starts - _lookup(offs, tile_expert, N_EXPERTS)), 0, te).astype(jnp.int32)
    xs = _dispatch(h, g_ffn, pos, n_tiles * te, tm=tm)
    ys = _experts(xs, tile_expert, tile_valid, w_gate, w_up, w_down, te=te)
    return _combine(h, route, pos, ys, p, g_ple, wgate_bf, wproj_bf, tm=tm)


REC_E1, REC_E2, REC_S1, REC_S2, REC_W1, REC_W2 = range(6)
PACKED = D_MODEL // 2
ROW_W = PACKED + LANES
LIST_LANES = LANES
LIST_COUNT = LIST_LANES - 1


def _slots(tm):
    need = 2 * tm + (SUBLANES - 1) * N_EXPERTS + SUBLANES
    return -(-need // LANES) * LANES


def _pieces(x):
    hi = x.astype(BF16)
    r1 = x - hi.astype(F32)
    mid = r1.astype(BF16)
    return hi, mid, (r1 - mid.astype(F32)).astype(BF16)


def _sort_kernel(h_ref, g_ref, wr_ref, before_ref, sel_ref, xs_ref, rec_ref, cnt_ref):
    m = _rms(h_ref[...], g_ref[...])
    logits = jnp.dot(m.astype(BF16), wr_ref[...], preferred_element_type=F32)
    tm = logits.shape[0]
    slots = xs_ref.shape[0]
    lane_i = lax.broadcasted_iota(jnp.int32, (tm, LANES), 1)
    lane = lane_i.astype(F32)
    big = jnp.float32(4 * LANES)

    is_g = jnp.logical_and(lane_i >= GROUP_LANE0, lane_i < GROUP_LANE0 + N_GROUPS)
    gl = jnp.where(is_g, logits, NEG)
    gmax = jnp.max(gl, axis=-1, keepdims=True)
    g_w = 1.0 / jnp.sum(jnp.where(is_g, jnp.exp(gl - gmax), 0.0), axis=-1, keepdims=True)
    g_sel = jnp.min(jnp.where(gl == gmax, lane - GROUP_LANE0, big), axis=-1, keepdims=True)

    grp_of_lane = (lane_i // EXPERTS_PER_GROUP).astype(F32)
    in_grp = jnp.logical_and(lane_i < N_EXPERTS, grp_of_lane == g_sel)
    el = jnp.where(in_grp, logits, NEG)
    t1 = jnp.max(el, axis=-1, keepdims=True)
    e1 = jnp.min(jnp.where(el == t1, lane, big), axis=-1, keepdims=True)
    el2 = jnp.where(lane == e1, NEG, el)
    t2 = jnp.max(el2, axis=-1, keepdims=True)
    e2 = jnp.min(jnp.where(el2 == t2, lane, big), axis=-1, keepdims=True)
    ex = jnp.exp(t2 - t1)
    w1 = g_w / (1.0 + ex)
    w2 = g_w * ex / (1.0 + ex)

    hit1 = lane == e1
    hit2 = lane == e2
    c = jnp.where(hit1, 1.0, jnp.where(hit2, 1.0, 0.0))
    rr = lax.broadcasted_iota(jnp.int32, (tm, tm), 0)
    cc = lax.broadcasted_iota(jnp.int32, (tm, tm), 1)
    lower = jnp.where(rr > cc, 1.0, 0.0).astype(BF16)
    rank = jnp.dot(lower, c.astype(BF16), preferred_element_type=F32)
    cnt = jnp.sum(c, axis=0, keepdims=True)
    chunks = jnp.floor((cnt + (SUBLANES - 1)) * (1.0 / SUBLANES))
    start = SUBLANES * jnp.dot(jnp.broadcast_to(chunks, (SUBLANES, LANES)).astype(BF16), before_ref[...],
                               preferred_element_type=F32)[0:1, :]
    slot_of = rank + start
    s1 = jnp.sum(jnp.where(hit1, slot_of, 0.0), axis=-1, keepdims=True)
    s2 = jnp.sum(jnp.where(hit2, slot_of, 0.0), axis=-1, keepdims=True)

    rec = jnp.zeros((tm, LANES), F32)
    for idx, val in ((REC_E1, e1), (REC_E2, e2), (REC_S1, s1), (REC_S2, s2), (REC_W1, w1), (REC_W2, w2)):
        rec = jnp.where(lane_i == idx, val, rec)
    rec_ref[...] = rec
    cnt_ref[0] = jnp.broadcast_to(cnt, (SUBLANES, LANES))

    rec_parts = _pieces(rec)
    srow = sum(lax.dot_general(sel_ref[...], part, NT, preferred_element_type=F32) for part in rec_parts)
    slot_id = lax.broadcasted_iota(jnp.int32, (slots, tm), 0).astype(F32)
    place = jnp.where(slot_id == srow[0:1, :], 1.0, jnp.where(slot_id == srow[1:2, :], 1.0, 0.0)).astype(BF16)
    payload = jnp.concatenate([m.astype(BF16), *rec_parts], axis=-1)
    moved = jnp.dot(place, payload, preferred_element_type=F32)
    info = (moved[:, D_MODEL:D_MODEL + LANES] + moved[:, D_MODEL + LANES:D_MODEL + 2 * LANES]
            + moved[:, D_MODEL + 2 * LANES:])
    xs_ref[...] = jnp.concatenate([_pack_bf16(moved[:, :D_MODEL]), pltpu.bitcast(info, jnp.uint32)], axis=-1)


def _sort(h, g_ffn, w_router, before, sel, *, tm):
    n = h.shape[0]
    slots = _slots(tm)
    tok = lambda w: pl.BlockSpec((tm, w), lambda i: (i, 0))
    return pl.pallas_call(
        _sort_kernel,
        grid=(n // tm,),
        in_specs=[tok(D_MODEL), _full((1, D_MODEL)), _full((D_MODEL, LANES)), _full((LANES, LANES)),
                  _full((SUBLANES, LANES))],
        out_specs=[pl.BlockSpec((slots, ROW_W), lambda i: (i, 0)), tok(LANES),
                   pl.BlockSpec((1, SUBLANES, LANES), lambda i: (i, 0, 0))],
        out_shape=[jax.ShapeDtypeStruct((n // tm * slots, ROW_W), jnp.uint32), jax.ShapeDtypeStruct((n, LANES), F32),
                   jax.ShapeDtypeStruct((n // tm, SUBLANES, LANES), F32)],
        compiler_params=_cparams(1),
        name="moe_sort",
    )(h, g_ffn, w_router, before, sel)


def _chunk_copy(src_hbm, row, dst, c, sem):
    if not isinstance(row, int):
        row = pl.multiple_of(row, SUBLANES)
    to = c * SUBLANES
    if not isinstance(to, int):
        to = pl.multiple_of(to, SUBLANES)
    return pltpu.make_async_copy(src_hbm.at[pl.ds(row, SUBLANES), :], dst.at[pl.ds(to, SUBLANES), :], sem)


def _expert_kernel2(te_ref, tv_ref, src_ref, nxt_ref, xa_ref, xb_ref, wg_ref, wu_ref, wd_ref, y_ref,
                    xbuf, sem, wg_bf, wu_bf, wd_bf, *, rows_a):
    i = pl.program_id(0)
    n = pl.num_programs(0)
    te = xbuf.shape[1]
    slot = i % 2

    def fetch(list_ref, to_slot):
        n_a = list_ref[0, 0, LIST_COUNT]

        def from_a(c, carry):
            _chunk_copy(xa_ref, list_ref[0, 0, c], xbuf.at[to_slot], c, sem.at[to_slot]).start()
            return carry

        def from_b(c, carry):
            _chunk_copy(xb_ref, list_ref[0, 0, c] - rows_a, xbuf.at[to_slot], c, sem.at[to_slot]).start()
            return carry

        per_tile = te // SUBLANES

        @pl.when(n_a == per_tile)
        def _():
            for c in range(per_tile):
                from_a(c, 0)

        @pl.when(n_a != per_tile)
        def _():
            lax.fori_loop(0, n_a, from_a, 0)
            lax.fori_loop(n_a, per_tile, from_b, 0)

    @pl.when(jnp.logical_and(i == 0, tv_ref[0] > 0))
    def _():
        fetch(src_ref, 0)

    nxt = jnp.minimum(i + 1, n - 1)

    @pl.when(jnp.logical_and(i + 1 < n, tv_ref[nxt] > 0))
    def _():
        fetch(nxt_ref, 1 - slot)

    valid = tv_ref[i]
    changed = jnp.logical_or(i == 0, te_ref[i] != te_ref[jnp.maximum(i - 1, 0)])

    @pl.when(jnp.logical_and(changed, valid > 0))
    def _():
        wg_bf[...] = wg_ref[0].astype(BF16)
        wu_bf[...] = wu_ref[0].astype(BF16)
        wd_bf[...] = wd_ref[0].astype(BF16)

    @pl.when(valid > 0)
    def _():
        pltpu.make_async_copy(xbuf.at[slot], xbuf.at[slot], sem.at[slot]).wait()
        rows = xbuf[slot]
        x = _unpack_bf16(rows[:, :PACKED])
        info = pltpu.bitcast(rows[:, PACKED:], F32)
        mine = info[:, REC_E1:REC_E1 + 1] == te_ref[i].astype(F32)
        gate = jnp.where(mine, info[:, REC_W1:REC_W1 + 1], info[:, REC_W2:REC_W2 + 1])
        hg = jnp.dot(x, wg_bf[...], preferred_element_type=F32)
        hu = jnp.dot(x, wu_bf[...], preferred_element_type=F32)
        hid = (hg * jax.nn.sigmoid(hg)) * hu * gate
        y = jnp.dot(hid.astype(BF16), wd_bf[...], preferred_element_type=F32)
        y_ref[...] = _pack_bf16(y)

    @pl.when(valid <= 0)
    def _():
        y_ref[...] = jnp.zeros_like(y_ref)


def _pack_bf16(x):
    w = x.shape[-1] // 2
    bits = lambda a: pltpu.bitcast(a.astype(BF16).astype(F32), jnp.uint32)
    return jnp.bitwise_or(lax.shift_right_logical(bits(x[:, :w]), jnp.uint32(16)),
                          jnp.bitwise_and(bits(x[:, w:]), jnp.uint32(0xFFFF0000)))


def _unpack_bf16(words):
    as_f32 = lambda a: pltpu.bitcast(a, F32)
    return jnp.concatenate([as_f32(lax.shift_left(words, jnp.uint32(16))),
                            as_f32(jnp.bitwise_and(words, jnp.uint32(0xFFFF0000)))], axis=-1).astype(BF16)


def _experts2(xs_a, xs_b, tile_expert, tile_valid, src, w_gate, w_up, w_down, *, te, layer):
    n_tiles = tile_expert.shape[0]
    wspec = lambda a, b: pl.BlockSpec((None, 1, a, b), lambda i, e, v: (layer, e[i], 0, 0))
    lst = lambda shift: pl.BlockSpec((1, 1, LIST_LANES),
                                     lambda i, e, v: (jnp.minimum(i + shift, n_tiles - 1), 0, 0),
                                     memory_space=pltpu.SMEM)
    grid_spec = pltpu.PrefetchScalarGridSpec(
        num_scalar_prefetch=2,
        grid=(n_tiles,),
        in_specs=[lst(0), lst(1), pl.BlockSpec(memory_space=pl.ANY), pl.BlockSpec(memory_space=pl.ANY),
                  wspec(D_MODEL, D_EXPERT), wspec(D_MODEL, D_EXPERT), wspec(D_EXPERT, D_MODEL)],
        out_specs=pl.BlockSpec((te, PACKED), lambda i, e, v: (i, 0)),
        scratch_shapes=[pltpu.VMEM((2, te, ROW_W), jnp.uint32), pltpu.SemaphoreType.DMA((2,)),
                        pltpu.VMEM((D_MODEL, D_EXPERT), BF16), pltpu.VMEM((D_MODEL, D_EXPERT), BF16),
                        pltpu.VMEM((D_EXPERT, D_MODEL), BF16)])
    return pl.pallas_call(
        functools.partial(_expert_kernel2, rows_a=xs_a.shape[0]),
        grid_spec=grid_spec,
        out_shape=jax.ShapeDtypeStruct((n_tiles * te, PACKED), jnp.uint32),
        compiler_params=_cparams(1),
        name="experts",
    )(tile_expert, tile_valid, src, src, xs_a, xs_b, w_gate, w_up, w_down)


def _combine_kernel2(dst_ref, nxt_ref, h_ref, rec_ref, ys_ref, p_ref, gple_ref, wgate_ref, wproj_ref, out_ref,
                     ybuf, sem):
    i = pl.program_id(0)
    n = pl.num_programs(0)
    slot = i % 2
    tm = h_ref.shape[0]
    slots = ybuf.shape[1]

    def fetch(list_ref, to_slot):
        for c in range(slots // SUBLANES):
            _chunk_copy(ys_ref, list_ref[0, 0, c], ybuf.at[to_slot], c, sem.at[to_slot]).start()

    @pl.when(i == 0)
    def _():
        fetch(dst_ref, 0)

    @pl.when(i + 1 < n)
    def _():
        fetch(nxt_ref, 1 - slot)

    pltpu.make_async_copy(ybuf.at[slot], ybuf.at[slot], sem.at[slot]).wait()

    rec = rec_ref[...]
    slot_id = lax.broadcasted_iota(jnp.int32, (tm, slots), 1).astype(F32)
    back = jnp.where(slot_id == rec[:, REC_S1:REC_S1 + 1], 1.0,
                     jnp.where(slot_id == rec[:, REC_S2:REC_S2 + 1], 1.0, 0.0)).astype(BF16)
    h2 = h_ref[...] + jnp.dot(back, _unpack_bf16(ybuf[slot]), preferred_element_type=F32)
    gate = jax.nn.sigmoid(jnp.dot(_rms(h2, gple_ref[...]).astype(BF16), wgate_ref[...], preferred_element_type=F32))
    ple = jnp.dot(p_ref[...].astype(BF16), wproj_ref[...], preferred_element_type=F32)
    out_ref[...] = h2 + ple * gate


def _combine2(h, rec, dst, ys, p, g_ple, wgate_bf, wproj_bf, *, tm, layer):
    n = h.shape[0]
    n_tok = n // tm
    tok = lambda w: pl.BlockSpec((tm, w), lambda i: (i, 0))
    p_spec = pl.BlockSpec((None, tm, D_PLE), lambda i: (layer, i, 0))
    lst = lambda shift: pl.BlockSpec((1, 1, LIST_LANES), lambda i: (jnp.minimum(i + shift, n_tok - 1), 0, 0),
                                     memory_space=pltpu.SMEM)
    return pl.pallas_call(
        _combine_kernel2,
        grid=(n_tok,),
        in_specs=[lst(0), lst(1), tok(D_MODEL), tok(LANES), pl.BlockSpec(memory_space=pl.ANY), p_spec,
                  _full((1, D_MODEL)), _full((D_MODEL, D_MODEL)), _full((D_PLE, D_MODEL))],
        out_specs=tok(D_MODEL),
        out_shape=jax.ShapeDtypeStruct((n, D_MODEL), F32),
        scratch_shapes=[pltpu.VMEM((2, _slots(tm), PACKED), jnp.uint32), pltpu.SemaphoreType.DMA((2,))],
        compiler_params=_cparams(1),
        name="combine",
    )(dst, dst, h, rec, ys, p, g_ple, wgate_bf, wproj_bf)


def _excl_cumsum(x, axis):
    return jnp.cumsum(x, axis=axis) - x


def _chunk_plan(cnt, tile_row0, *, n_pairs, rows_a, zero_row, te):
    n_tok = cnt.shape[0]
    per_tile = te // SUBLANES
    chunks = (cnt + SUBLANES - 1) // SUBLANES
    run0 = _excl_cumsum(chunks, 1)
    n_run = jnp.sum(chunks, axis=1)
    seg0 = _excl_cumsum(chunks, 0)
    total = jnp.sum(chunks, axis=0)
    region = ((total + per_tile - 1) // per_tile) * per_tile
    reg_end = jnp.cumsum(region)
    reg0 = reg_end - region

    n_tiles = -(-(n_pairs + (SUBLANES - 1) * N_EXPERTS * n_tok) // te) + N_EXPERTS
    t0 = jnp.arange(n_tiles, dtype=jnp.int32) * per_tile
    tile_expert = jnp.minimum(jnp.sum((t0[:, None] >= reg_end[None, :]).astype(jnp.int32), axis=1), N_EXPERTS - 1)
    pick = tile_expert[:, None] == jnp.arange(N_EXPERTS, dtype=jnp.int32)[None, :]
    of_tile = lambda v: jnp.sum(jnp.where(pick, v[None, :], 0), axis=1)
    tile_valid = jnp.clip(of_tile(total) - (t0 - of_tile(reg0)), 0, per_tile).astype(jnp.int32)

    q = (t0 - of_tile(reg0))[:, None] + jnp.arange(per_tile, dtype=jnp.int32)[None, :]
    col_of_tile = lambda v: jnp.sum(jnp.where(pick[:, None, :], v[None, :, :], 0), axis=2)
    seg0_t, seg1_t, run0_t = col_of_tile(seg0), col_of_tile(seg0 + chunks), col_of_tile(run0)
    holds = (q[:, :, None] >= seg0_t[:, None, :]) & (q[:, :, None] < seg1_t[:, None, :])
    local = run0_t[:, None, :] + q[:, :, None] - seg0_t[:, None, :]
    row = jnp.asarray(tile_row0, jnp.int32)[None, None, :] + SUBLANES * local
    src = jnp.sum(jnp.where(holds, row, 0), axis=2)
    src = jnp.where(jnp.any(holds, axis=2), src, zero_row)
    n_first = jnp.sum((src < rows_a).astype(jnp.int32), axis=1)
    lane = jnp.arange(LIST_LANES, dtype=jnp.int32)[None, :]
    src = jnp.where(lane == LIST_COUNT, n_first[:, None], jnp.pad(src, ((0, 0), (0, LIST_LANES - per_tile))))
    src = src.reshape(n_tiles, 1, LIST_LANES).astype(jnp.int32)

    j = jnp.arange(LIST_LANES, dtype=jnp.int32)[None, :, None]
    inside = (j >= run0[:, None, :]) & (j < (run0 + chunks)[:, None, :])
    base = (reg0[None, :] + seg0 - run0)[:, None, :]
    dst = SUBLANES * jnp.sum(jnp.where(inside, base + j, 0), axis=2)
    dst = dst.reshape(n_tok, 1, LIST_LANES).astype(jnp.int32)
    return tile_expert.astype(jnp.int32), tile_valid, src, dst


def _moe_ple2(h_a, h_b, p_a, p_b, g_ffn, w_router, before, sel, w_gate, w_up, w_down, g_ple, wgate_bf, wproj_bf,
              *, tm_a, tm_b, te, layer):
    xs_a, rec_a, cnt_a = _sort(h_a, g_ffn, w_router, before, sel, tm=tm_a)
    xs_b, rec_b, cnt_b = _sort(h_b, g_ffn, w_router, before, sel, tm=tm_b)
    t_a, t_b = cnt_a.shape[0], cnt_b.shape[0]
    cnt = jnp.concatenate([cnt_a[:, 0, :N_EXPERTS], cnt_b[:, 0, :N_EXPERTS]], axis=0).astype(jnp.int32)
    rows_a = xs_a.shape[0]
    row0 = np.concatenate([np.arange(t_a) * _slots(tm_a), rows_a + np.arange(t_b) * _slots(tm_b)])
    tile_expert, tile_valid, src, dst = _chunk_plan(
        cnt, row0, n_pairs=2 * (h_a.shape[0] + h_b.shape[0]), rows_a=rows_a,
        zero_row=rows_a + _slots(tm_b) - SUBLANES, te=te)
    ys = _experts2(xs_a, xs_b, tile_expert, tile_valid, src, w_gate, w_up, w_down, te=te, layer=layer)
    out_a = _combine2(h_a, rec_a, dst[:t_a], ys, p_a, g_ple, wgate_bf, wproj_bf, tm=tm_a, layer=layer)
    out_b = _combine2(h_b, rec_b, dst[t_a:], ys, p_b, g_ple, wgate_bf, wproj_bf, tm=tm_b, layer=layer)
    return out_a, out_b


def _bucket_np(dist):
    max_exact = N_BUCKETS // 2
    d_f = np.maximum(dist, 1).astype(np.float32)
    large = max_exact + (np.log(d_f / np.float32(max_exact)) / np.float32(np.log(MAX_DISTANCE / max_exact))
                         * np.float32(N_BUCKETS - max_exact)).astype(np.int32)
    large = np.minimum(large, N_BUCKETS - 1)
    return np.where(dist < max_exact, dist, large).astype(np.int32)


def _bias_from_buckets(rel_bias, bucket, valid):
    onehot = (jnp.asarray(bucket)[..., None] == jnp.arange(N_BUCKETS, dtype=jnp.int32)).astype(F32)
    bias = jnp.einsum("...k,kh->h...", onehot, rel_bias.astype(F32), precision=HIGHEST)
    return jnp.where(jnp.asarray(valid)[None], bias, NEG)


def _block_order(dil):
    g = np.arange(Q_BLOCK) // SUBLANES
    j = np.arange(Q_BLOCK) % SUBLANES
    if dil == 1:
        return 16 * j + g
    if dil == 4:
        return 32 * (g // 4) + 4 * j + g % 4
    return SUBLANES * g + j


def _band_bias(rel_bias, dil):
    mu = _block_order(dil)
    qi = mu[:, None] + Q_BLOCK
    ki = np.concatenate([mu, mu + Q_BLOCK])[None, :]
    off = qi - ki
    valid = (off >= 0) & (off <= N_KEYS)
    bucket = _bucket_np(dil * np.clip(off, 0, N_KEYS))
    first = valid & (np.arange(2 * Q_BLOCK)[None, :] >= Q_BLOCK)
    tables = [_bias_from_buckets(rel_bias, bucket, v).reshape(N_HEADS * Q_BLOCK, 2 * Q_BLOCK) for v in (valid, first)]
    return jnp.stack(tables)


def _sample_tables(rel_bias, w_buf):
    qpos = w_buf + np.arange(SAMPLE_T)[:, None]
    pos = np.arange(w_buf + NEW_COLS)[None, :]
    dist = qpos - pos
    in_seq = (dist >= 0) & (pos < w_buf + SAMPLE_T)
    mult = np.zeros(dist.shape, np.float32)
    for (w, d) in PATTERNS:
        mult += in_seq & (dist % d == 0) & (dist <= w)
    bucket = _bucket_np(np.maximum(dist, 0))
    bias = jnp.transpose(_bias_from_buckets(rel_bias, bucket, mult > 0), (1, 0, 2))
    rows = SAMPLE_T * N_HEADS
    mult_rows = np.broadcast_to(mult[:, None, :], (SAMPLE_T, N_HEADS, mult.shape[-1]))
    return bias.reshape(rows, -1), jnp.asarray(mult_rows.reshape(rows, -1))


PROJ_TM = 256
OUT_TM = 512
PROMPT_TM = 256
PROMPT_TE = 256
ATTN_SUB = 4


def _row_perm(tm):
    a = np.arange(tm)
    src = (a // Q_BLOCK) * Q_BLOCK + 16 * (a % SUBLANES) + (a % Q_BLOCK) // SUBLANES
    perm = np.zeros((tm, tm), np.float32)
    perm[a, src] = 1.0
    return perm


def kernel(x_prompt, x_sample, cache_k, cache_v, state_conv, p_prompt, p_sample, rel_bias, g_mix, w_in, q_gain,
           k_gain, conv_w, g_out_att, g_out_conv, w_out, g_ffn, w_router_group, w_router_expert, w_gate, w_up,
           w_down, g_ple, w_ple_gate, w_ple_proj):
    depth = w_in.shape[0]
    batch, seq, _ = x_prompt.shape
    dec_b, dec_t, _ = x_sample.shape
    w_buf = cache_k.shape[2]
    n_s = dec_b * dec_t
    keep = min(w_buf, seq)
    assert dec_t == SAMPLE_T and w_buf % LANES == 0
    assert seq % (Q_BLOCK * 16 * 2) == 0 and keep % PROJ_TM == 0 and seq % PROMPT_TM == 0
    cache_kt = jnp.transpose(cache_k, (0, 1, 3, 4, 2))
    cache_vt = jnp.transpose(cache_v, (0, 1, 3, 4, 2))

    row = lambda a: a.reshape(1, -1)
    src_lane = np.arange(LANES)
    expand = jnp.asarray((src_lane[:, None] // LSE_LANES_PER_HEAD == np.arange(ATT_DIM)[None, :] // HEAD_DIM)
                         & (src_lane[:, None] % LSE_LANES_PER_HEAD == 0), BF16)
    before = jnp.asarray(np.arange(LANES)[:, None] < np.arange(LANES)[None, :], BF16)
    sel_np = np.zeros((SUBLANES, LANES), np.float32)
    sel_np[0, REC_S1] = sel_np[1, REC_S2] = 1.0
    sel = jnp.asarray(sel_np, BF16)
    perm = jnp.asarray(_row_perm(PROJ_TM), BF16)
    unperm = jnp.asarray(_row_perm(OUT_TM).T, BF16)
    band = [_band_bias(rel_bias, d) for (_, d) in PATTERNS]
    s_bias, s_mult = _sample_tables(rel_bias, w_buf)

    hp = x_prompt.reshape(batch * seq, D_MODEL)
    hs = jnp.swapaxes(x_sample, 0, 1).reshape(n_s, D_MODEL)
    pp_all = p_prompt.reshape(depth, batch * seq, D_PLE)
    ps_all = jnp.swapaxes(p_sample, 1, 2).reshape(depth, n_s, D_PLE)
    new = {k: [] for k in ("kp", "vp", "cp", "ks", "vs", "cs")}
    hist_p = jnp.zeros((batch, SUBLANES, CONV_DIM), F32)

    for l in range(depth):
        w_in_bf = w_in[l].astype(BF16)
        wa_bf = w_out[l, :ATT_DIM].astype(BF16)
        wc_bf = w_out[l, ATT_DIM:].astype(BF16)
        wgate_bf = w_ple_gate[l].astype(BF16)
        wproj_bf = w_ple_proj[l].astype(BF16)
        w_router = jnp.concatenate(
            [w_router_expert[l], w_router_group[l],
             jnp.zeros((D_MODEL, LANES - N_EXPERTS - N_GROUPS), F32)], axis=1).astype(BF16)
        qg, kg = row(jnp.tile(q_gain[l], N_HEADS)), row(jnp.tile(k_gain[l], N_HEADS))
        mix = (row(g_mix[l]), w_in_bf, qg, kg, conv_w[l], row(g_out_conv[l]))
        moe = (row(g_ffn[l]), w_router, before, sel, w_gate, w_up, w_down, row(g_ple[l]), wgate_bf, wproj_bf)

        q, k, v, k_nat, v_nat, yn, nconv = _inproj(
            hp, hist_p, *mix, perm, tm=PROJ_TM, shift=1, tiles_per_seq=seq // PROJ_TM,
            keep_tiles=keep // PROJ_TM)
        os, lses = [], []
        for bias, (_, d) in zip(band, PATTERNS):
            sub = min(ATTN_SUB, seq // (Q_BLOCK * d))
            o, lse = _attn_pattern(q, k, v, bias, batch=batch, seq=seq, dil=d, sub=sub,
                                   res=min(d, ATTN_SUB // sub))
            os.append(o)
            lses.append(lse)
        hp = _outproj(os, lses, yn, hp, row(g_out_att[l]), wa_bf, wc_bf, expand, unperm, tm=OUT_TM)
        new["kp"].append(jnp.transpose(k_nat, (0, 3, 1, 2)))
        new["vp"].append(jnp.transpose(v_nat, (0, 3, 1, 2)))
        new["cp"].append(nconv[:, SUBLANES - 2:])

        hist_s = jnp.swapaxes(state_conv[l], 0, 1).reshape(1, 2 * dec_b, CONV_DIM)
        q, k, v, yn, nconv = _inproj(hs, hist_s, *mix, tm=n_s, shift=dec_b, tiles_per_seq=1)
        bmaj = lambda a: jnp.swapaxes(a.reshape(dec_t, dec_b, N_HEADS, HEAD_DIM), 0, 1)
        qb, kb, vb = bmaj(q), bmaj(k), bmaj(v)
        rows8 = lambda a: jnp.pad(a.reshape(dec_b, dec_t, ATT_DIM), ((0, 0), (0, SUBLANES - dec_t), (0, 0)))
        att = _attn_sample(qb.reshape(dec_b, dec_t, ATT_DIM), rows8(kb), rows8(vb), cache_kt, cache_vt, l,
                           s_bias, s_mult)
        att_tm = jnp.swapaxes(att, 0, 1).reshape(n_s, ATT_DIM)
        hs = _outproj([att_tm], [], yn, hs, row(g_out_att[l]), wa_bf, wc_bf, tm=n_s)
        hp, hs = _moe_ple2(hp, hs, pp_all, ps_all, *moe, tm_a=PROMPT_TM, tm_b=n_s, te=PROMPT_TE, layer=l)
        new["ks"].append(kb)
        new["vs"].append(vb)
        new["cs"].append(jnp.swapaxes(nconv.reshape(2, dec_b, CONV_DIM), 0, 1))

    y_prompt = hp.reshape(batch, seq, D_MODEL)
    y_sample = jnp.swapaxes(hs.reshape(dec_t, dec_b, D_MODEL), 0, 1)
    st = lambda key: jnp.stack(new[key])
    return (y_prompt, y_sample, st("kp"), st("vp"), st("cp"), st("ks"), st("vs"), st("cs"))
```

```python
import functools

import jax
import jax.numpy as jnp
import numpy as np
from jax import lax
from jax.experimental import pallas as pl
from jax.experimental.pallas import tpu as pltpu

F32 = jnp.float32
BF16 = jnp.bfloat16
HIGHEST = lax.Precision.HIGHEST

D_MODEL = 1024
HEAD_DIM = 64
N_HEADS = 8
ATT_DIM = N_HEADS * HEAD_DIM
CONV_DIM = D_MODEL - ATT_DIM
MIX_IN = 3 * ATT_DIM + 3 * CONV_DIM
PATTERNS = ((128, 1), (512, 4), (2048, 16))
N_KEYS = 128
Q_BLOCK = 128
N_BUCKETS = 32
MAX_DISTANCE = 2048
N_GROUPS = 4
EXPERTS_PER_GROUP = 8
N_EXPERTS = N_GROUPS * EXPERTS_PER_GROUP
D_EXPERT = 256
D_PLE = 256
EPS = 1e-6
NEG = -1e30

LANES = 128
SUBLANES = 8
SLABS = Q_BLOCK // SUBLANES
LSE_LANES_PER_HEAD = LANES // N_HEADS
VMEM_LIMIT = 56 * 1024 * 1024
NT = (((1,), (1,)), ((), ()))


def _cparams(n_axes):
    return pltpu.CompilerParams(dimension_semantics=("arbitrary",) * n_axes,
                                vmem_limit_bytes=VMEM_LIMIT)


def _full(shape):
    n = len(shape)
    return pl.BlockSpec(shape, lambda *_: (0,) * n)


def _rms(x, gain):
    ms = jnp.mean(x * x, axis=-1, keepdims=True)
    return x * lax.rsqrt(ms + EPS) * gain


def _exact_dot(x, e_ref):
    hi = x.astype(BF16)
    r1 = x - hi.astype(F32)
    mid = r1.astype(BF16)
    lo = (r1 - mid.astype(F32)).astype(BF16)
    e = e_ref[...]
    return (jnp.dot(hi, e, preferred_element_type=F32) + jnp.dot(mid, e, preferred_element_type=F32)
            + jnp.dot(lo, e, preferred_element_type=F32))


def _inproj_kernel(*refs, shift, tiles_per_seq, permute):
    (h_ref, gmix_ref, w_ref, qg_ref, kg_ref, cw_ref, gconv_ref, hist_ref) = refs[:8]
    if permute:
        perm_ref, q_ref, k_ref, v_ref, kn_ref, vn_ref, yn_ref, nconv_ref, carry_ref = refs[8:]
    else:
        q_ref, k_ref, v_ref, yn_ref, nconv_ref, carry_ref = refs[8:]
    i = pl.program_id(0)
    a = _rms(h_ref[...], gmix_ref[...])
    proj = jnp.dot(a.astype(BF16), w_ref[...], preferred_element_type=F32)
    tm = proj.shape[0]

    lower = lax.broadcasted_iota(jnp.int32, (tm, LANES), 1) < HEAD_DIM

    def head_norm(t, g):
        out = []
        for j in range(ATT_DIM // LANES):
            blk = t[:, j * LANES:(j + 1) * LANES]
            sq = blk * blk
            ms_lo = jnp.sum(jnp.where(lower, sq, 0.0), axis=-1, keepdims=True) * (1.0 / HEAD_DIM)
            ms_hi = jnp.sum(jnp.where(lower, 0.0, sq), axis=-1, keepdims=True) * (1.0 / HEAD_DIM)
            scale = jnp.where(lower, lax.rsqrt(ms_lo + EPS), lax.rsqrt(ms_hi + EPS))
            out.append(blk * scale)
        return jnp.concatenate(out, axis=-1) * g

    q = head_norm(proj[:, 0:ATT_DIM], qg_ref[...])
    k = head_norm(proj[:, ATT_DIM:2 * ATT_DIM], kg_ref[...])
    v = proj[:, 2 * ATT_DIM:3 * ATT_DIM]
    if permute:
        kn_ref[...] = k.T.reshape(N_HEADS, HEAD_DIM, tm)
        vn_ref[...] = v.T.reshape(N_HEADS, HEAD_DIM, tm)
        qkv = jnp.concatenate([q * (HEAD_DIM ** -0.5), k, v], axis=-1).astype(BF16)
        moved = jnp.dot(perm_ref[...], qkv, preferred_element_type=F32)
        q_ref[...] = moved[:, 0:ATT_DIM]
        k_ref[...] = moved[:, ATT_DIM:2 * ATT_DIM]
        v_ref[...] = moved[:, 2 * ATT_DIM:3 * ATT_DIM]
    else:
        q_ref[...] = q
        k_ref[...] = k
        v_ref[...] = v
    c0 = 3 * ATT_DIM
    hc = proj[:, c0:c0 + CONV_DIM]
    gb = proj[:, c0 + CONV_DIM:c0 + 2 * CONV_DIM]
    gc = proj[:, c0 + 2 * CONV_DIM:c0 + 3 * CONV_DIM]
    u = gc * hc

    if shift == 1:
        @pl.when(i % tiles_per_seq == 0)
        def _():
            carry_ref[...] = hist_ref[0]
        h0 = carry_ref[SUBLANES - 2:SUBLANES - 1, :]
        h1 = carry_ref[SUBLANES - 1:SUBLANES, :]
        row = lax.broadcasted_iota(jnp.int32, (tm, 1), 0)
        u1 = jnp.where(row == 0, h1, pltpu.roll(u, 1, 0))
        u2 = jnp.where(row == 0, h0, jnp.where(row == 1, h1, pltpu.roll(u, 2, 0)))
        carry_ref[...] = u[tm - SUBLANES:tm, :]
        nconv_ref[0] = u[tm - SUBLANES:tm, :]
    else:
        hist = hist_ref[0]
        u1 = jnp.concatenate([hist[shift:2 * shift], u[0:tm - shift]], axis=0)
        u2 = jnp.concatenate([hist, u[0:tm - 2 * shift]], axis=0)
        nconv_ref[0] = u[tm - 2 * shift:tm, :]
    conv = cw_ref[0:1, :] * u2 + cw_ref[1:2, :] * u1 + cw_ref[2:3, :] * u
    yn_ref[...] = _rms(gb * conv, gconv_ref[...])


def _inproj(h, hist, g_mix, w_in_bf, q_gain, k_gain, conv_w, g_out_conv, perm=None, *,
            tm, shift, tiles_per_seq, keep_tiles=0):
    n = h.shape[0]
    hist_rows = hist.shape[1]
    nseq = hist.shape[0]
    tok = lambda w: pl.BlockSpec((tm, w), lambda i: (i, 0))
    seq3 = lambda r: pl.BlockSpec((1, r, CONV_DIM), lambda i: (i // tiles_per_seq, 0, 0))
    nconv_rows = SUBLANES if shift == 1 else 2 * shift
    att = jax.ShapeDtypeStruct((n, ATT_DIM), F32)
    in_specs = [tok(D_MODEL), _full((1, D_MODEL)), _full((D_MODEL, MIX_IN)), _full((1, ATT_DIM)),
                _full((1, ATT_DIM)), _full((3, CONV_DIM)), _full((1, CONV_DIM)), seq3(hist_rows)]
    args = [h, g_mix, w_in_bf, q_gain, k_gain, conv_w, g_out_conv, hist]
    out_specs = [tok(ATT_DIM)] * 3
    out_shape = [att] * 3
    if perm is not None:
        in_specs.append(_full((tm, tm)))
        args.append(perm)
        first = tiles_per_seq - keep_tiles
        kept = pl.BlockSpec((None, N_HEADS, HEAD_DIM, tm),
                            lambda i: (i // tiles_per_seq, 0, 0, jnp.maximum(i % tiles_per_seq - first, 0)))
        out_specs += [kept, kept]
        out_shape += [jax.ShapeDtypeStruct((nseq, N_HEADS, HEAD_DIM, keep_tiles * tm), F32)] * 2
    out_specs += [tok(CONV_DIM), seq3(nconv_rows)]
    out_shape += [jax.ShapeDtypeStruct((n, CONV_DIM), F32),
                  jax.ShapeDtypeStruct((nseq, nconv_rows, CONV_DIM), F32)]
    return pl.pallas_call(
        functools.partial(_inproj_kernel, shift=shift, tiles_per_seq=tiles_per_seq, permute=perm is not None),
        grid=(n // tm,),
        in_specs=in_specs, out_specs=out_specs, out_shape=out_shape,
        scratch_shapes=[pltpu.VMEM((SUBLANES, CONV_DIM), F32)],
        compiler_params=_cparams(1),
        name="inproj",
    )(*args)


def _attn_block(q, kk, vv, bias):
    lane = lax.broadcasted_iota(jnp.int32, (Q_BLOCK, LANES), 1)
    upper = lane >= HEAD_DIM
    scores = []
    for h in range(N_HEADS):
        j, e = divmod(h, 2)
        qp = q[:, j * LANES:(j + 1) * LANES]
        qm = (jnp.where(upper, qp, 0.0) if e else jnp.where(upper, 0.0, qp)).astype(BF16)
        scores.append(lax.dot_general(qm, kk[:, j * LANES:(j + 1) * LANES], NT, preferred_element_type=F32))
    s = jnp.concatenate(scores, axis=0) + bias
    m = jnp.max(s, axis=-1, keepdims=True)
    p = jnp.exp(s - m)
    den = jnp.sum(p, axis=-1, keepdims=True)
    pb = p.astype(BF16)
    inv = 1.0 / den
    lse = m + jnp.log(den)
    lse_grp = lane // LSE_LANES_PER_HEAD
    lse_tile = jnp.zeros((Q_BLOCK, LANES), F32)
    outs = []
    for j in range(N_HEADS // 2):
        pair = None
        for e in range(2):
            h = 2 * j + e
            rows = slice(h * Q_BLOCK, (h + 1) * Q_BLOCK)
            o = jnp.dot(pb[rows], vv[:, j * LANES:(j + 1) * LANES], preferred_element_type=F32) * inv[rows]
            pair = o if e == 0 else jnp.where(upper, o, pair)
            lse_tile = jnp.where(lse_grp == h, lse[rows], lse_tile)
        outs.append(pair)
    return jnp.concatenate(outs, axis=-1), lse_tile


def _attn_kernel(q_ref, kp_ref, kc_ref, vp_ref, vc_ref, bias_ref, o_ref, lse_ref, kbuf, vbuf, *, sub, res):
    n = pl.program_id(2)
    rows = sub * Q_BLOCK
    for r in range(res):
        kbuf[0:Q_BLOCK, :] = kp_ref[:, r].reshape(Q_BLOCK, ATT_DIM).astype(BF16)
        kbuf[Q_BLOCK:, :] = kc_ref[:, :, r].reshape(rows, ATT_DIM).astype(BF16)
        vbuf[0:Q_BLOCK, :] = vp_ref[:, r].reshape(Q_BLOCK, ATT_DIM).astype(BF16)
        vbuf[Q_BLOCK:, :] = vc_ref[:, :, r].reshape(rows, ATT_DIM).astype(BF16)
        for j in range(sub):
            q = q_ref[j, :, r].reshape(Q_BLOCK, ATT_DIM)
            r0 = j * Q_BLOCK
            first = (n == 0).astype(jnp.int32) if j == 0 else 0
            o, lse = _attn_block(q, kbuf[r0:r0 + 2 * Q_BLOCK, :], vbuf[r0:r0 + 2 * Q_BLOCK, :], bias_ref[first])
            o_ref[j, :, r] = o.reshape(SLABS, SUBLANES, ATT_DIM)
            lse_ref[j, :, r] = lse.reshape(SLABS, SUBLANES, LANES)


def _attn_pattern(q, k, v, bias, *, batch, seq, dil, sub, res):
    nblk = seq // (Q_BLOCK * dil)
    view = lambda t: t.reshape(batch, nblk, SLABS, dil, SUBLANES, t.shape[-1])
    cur = lambda c: pl.BlockSpec((None, sub, SLABS, res, SUBLANES, c), lambda b, r, n: (b, n, 0, r, 0, 0))
    prev = pl.BlockSpec((None, None, SLABS, res, SUBLANES, ATT_DIM),
                        lambda b, r, n: (b, jnp.maximum(n * sub - 1, 0), 0, r, 0, 0))
    o, lse = pl.pallas_call(
        functools.partial(_attn_kernel, sub=sub, res=res),
        grid=(batch, dil // res, nblk // sub),
        in_specs=[cur(ATT_DIM), prev, cur(ATT_DIM), prev, cur(ATT_DIM),
                  _full((2, N_HEADS * Q_BLOCK, 2 * Q_BLOCK))],
        out_specs=[cur(ATT_DIM), cur(LANES)],
        out_shape=[jax.ShapeDtypeStruct((batch, nblk, SLABS, dil, SUBLANES, ATT_DIM), F32),
                   jax.ShapeDtypeStruct((batch, nblk, SLABS, dil, SUBLANES, LANES), F32)],
        scratch_shapes=[pltpu.VMEM(((sub + 1) * Q_BLOCK, ATT_DIM), BF16),
                        pltpu.VMEM(((sub + 1) * Q_BLOCK, ATT_DIM), BF16)],
        compiler_params=_cparams(3),
        name=f"attn_d{dil}",
    )(view(q), view(k), view(k), view(v), view(v), bias)
    return o.reshape(batch * seq, ATT_DIM), lse.reshape(batch * seq, LANES)


SAMPLE_T = 4
NEW_COLS = LANES


def _attn_sample_kernel(q_ref, kt_ref, kn_ref, vt_ref, vn_ref, bias_ref, mult_ref, o_ref):
    rows = SAMPLE_T * N_HEADS
    q4 = q_ref[...] * (HEAD_DIM ** -0.5)
    qt = jnp.concatenate([jnp.broadcast_to(q4[t:t + 1, :], (N_HEADS, ATT_DIM)) for t in range(SAMPLE_T)], axis=0)
    lane_head = lax.broadcasted_iota(jnp.int32, (rows, ATT_DIM), 1) // HEAD_DIM
    row_head = lax.broadcasted_iota(jnp.int32, (rows, ATT_DIM), 0) % N_HEADS
    own = lane_head == row_head
    qbd = jnp.where(own, qt, 0.0).astype(BF16)
    flat = lambda ref: ref[...].reshape(ATT_DIM, ref.shape[-1]).astype(BF16)
    pad = jnp.zeros((NEW_COLS - SUBLANES, ATT_DIM), F32)
    new_rows = lambda ref: jnp.concatenate([ref[...], pad], axis=0).astype(BF16)
    s = jnp.concatenate([jnp.dot(qbd, flat(kt_ref), preferred_element_type=F32),
                         lax.dot_general(qbd, new_rows(kn_ref), NT, preferred_element_type=F32)],
                        axis=-1) + bias_ref[...]
    m = jnp.max(s, axis=-1, keepdims=True)
    p = jnp.exp(s - m) * mult_ref[...]
    den = jnp.sum(p, axis=-1, keepdims=True)
    pb = p.astype(BF16)
    w_buf = kt_ref.shape[-1]
    acc = (lax.dot_general(pb[:, :w_buf], flat(vt_ref), NT, preferred_element_type=F32)
           + jnp.dot(pb[:, w_buf:], new_rows(vn_ref), preferred_element_type=F32))
    acc = jnp.where(own, acc / den, 0.0)
    for t in range(SAMPLE_T):
        o_ref[t:t + 1, :] = jnp.sum(acc[t * N_HEADS:(t + 1) * N_HEADS, :], axis=0, keepdims=True)


def _attn_sample(q, k_new, v_new, cache_kt, cache_vt, layer, bias, mult):
    nb = q.shape[0]
    w_buf = cache_kt.shape[-1]
    tok = pl.BlockSpec((None, SAMPLE_T, ATT_DIM), lambda b: (b, 0, 0))
    new = pl.BlockSpec((None, SUBLANES, ATT_DIM), lambda b: (b, 0, 0))
    old = pl.BlockSpec((None, None, N_HEADS, HEAD_DIM, w_buf), lambda b: (layer, b, 0, 0, 0))
    tbl = _full((SAMPLE_T * N_HEADS, w_buf + NEW_COLS))
    return pl.pallas_call(
        _attn_sample_kernel,
        grid=(nb,),
        in_specs=[tok, old, new, old, new, tbl, tbl],
        out_specs=tok,
        out_shape=jax.ShapeDtypeStruct((nb, SAMPLE_T, ATT_DIM), F32),
        compiler_params=_cparams(1),
        name="attn_sample",
    )(q, cache_kt, k_new, cache_vt, v_new, bias, mult)


def _split_dot(x, e_ref):
    hi = x.astype(BF16)
    lo = (x - hi.astype(F32)).astype(BF16)
    return (jnp.dot(hi, e_ref[...], preferred_element_type=F32)
            + jnp.dot(lo, e_ref[...], preferred_element_type=F32))


def _outproj_kernel(*refs, n_pat):
    mix = n_pat > 1
    n_lse = n_pat if mix else 0
    o_refs = refs[0:n_pat]
    l_refs = refs[n_pat:n_pat + n_lse]
    rest = refs[n_pat + n_lse:]
    if mix:
        yn_ref, h_ref, gatt_ref, exp_ref, unperm_ref, wa_ref, wc_ref, out_ref = rest
        lses = [r[...] for r in l_refs]
        top = functools.reduce(jnp.maximum, lses)
        ws = [jnp.exp(l - top) for l in lses]
        tot = functools.reduce(lambda a, b: a + b, ws)
        att = None
        for w, o_ref in zip(ws, o_refs):
            term = _split_dot(w / tot, exp_ref) * o_ref[...]
            att = term if att is None else att + term
        att_bf = jnp.dot(unperm_ref[...], _rms(att, gatt_ref[...]).astype(BF16),
                         preferred_element_type=F32).astype(BF16)
    else:
        yn_ref, h_ref, gatt_ref, wa_ref, wc_ref, out_ref = rest
        att_bf = _rms(o_refs[0][...], gatt_ref[...]).astype(BF16)
    y = (jnp.dot(att_bf, wa_ref[...], preferred_element_type=F32)
         + jnp.dot(yn_ref[...].astype(BF16), wc_ref[...], preferred_element_type=F32))
    out_ref[...] = h_ref[...] + y


def _outproj(os, lses, yn, h, g_att, wa_bf, wc_bf, expand=None, unperm=None, *, tm):
    n = h.shape[0]
    n_pat = len(os)
    tok = lambda w: pl.BlockSpec((tm, w), lambda i: (i, 0))
    in_specs = [tok(ATT_DIM)] * n_pat + [tok(LANES)] * len(lses) + [tok(CONV_DIM), tok(D_MODEL), _full((1, ATT_DIM))]
    args = [*os, *lses, yn, h, g_att]
    if n_pat > 1:
        in_specs += [_full((LANES, ATT_DIM)), _full((tm, tm))]
        args += [expand, unperm]
    in_specs += [_full((ATT_DIM, D_MODEL)), _full((CONV_DIM, D_MODEL))]
    args += [wa_bf, wc_bf]
    return pl.pallas_call(
        functools.partial(_outproj_kernel, n_pat=n_pat),
        grid=(n // tm,),
        in_specs=in_specs,
        out_specs=tok(D_MODEL),
        out_shape=jax.ShapeDtypeStruct((n, D_MODEL), F32),
        compiler_params=_cparams(1),
        name="outproj",
    )(*args)


ROUTE_I1, ROUTE_I2, ROUTE_R1, ROUTE_R2, ROUTE_W1, ROUTE_W2 = range(6)
GROUP_LANE0 = N_EXPERTS


def _route_kernel(h_ref, g_ref, wr_ref, route_ref, cnt_ref, carry_ref):
    i = pl.program_id(0)

    @pl.when(i == 0)
    def _():
        carry_ref[...] = jnp.zeros_like(carry_ref)

    m = _rms(h_ref[...], g_ref[...])
    logits = jnp.dot(m.astype(BF16), wr_ref[...], preferred_element_type=F32)
    tm = logits.shape[0]
    lane_i = lax.broadcasted_iota(jnp.int32, (tm, LANES), 1)
    lane = lane_i.astype(F32)
    big = jnp.float32(4 * LANES)

    is_g = jnp.logical_and(lane_i >= GROUP_LANE0, lane_i < GROUP_LANE0 + N_GROUPS)
    gl = jnp.where(is_g, logits, NEG)
    gmax = jnp.max(gl, axis=-1, keepdims=True)
    g_w = 1.0 / jnp.sum(jnp.where(is_g, jnp.exp(gl - gmax), 0.0), axis=-1, keepdims=True)
    g_sel = jnp.min(jnp.where(gl == gmax, lane - GROUP_LANE0, big), axis=-1, keepdims=True)

    grp_of_lane = (lane_i // EXPERTS_PER_GROUP).astype(F32)
    in_grp = jnp.logical_and(lane_i < N_EXPERTS, grp_of_lane == g_sel)
    el = jnp.where(in_grp, logits, NEG)
    t1 = jnp.max(el, axis=-1, keepdims=True)
    i1 = jnp.min(jnp.where(el == t1, lane, big), axis=-1, keepdims=True)
    el2 = jnp.where(lane == i1, NEG, el)
    t2 = jnp.max(el2, axis=-1, keepdims=True)
    i2 = jnp.min(jnp.where(el2 == t2, lane, big), axis=-1, keepdims=True)
    e2 = jnp.exp(t2 - t1)
    w1 = g_w / (1.0 + e2)
    w2 = g_w * e2 / (1.0 + e2)

    hit1 = lane == i1
    hit2 = lane == i2
    c = jnp.where(jnp.logical_or(hit1, hit2), 1.0, 0.0)
    rr = lax.broadcasted_iota(jnp.int32, (tm, tm), 0)
    cc = lax.broadcasted_iota(jnp.int32, (tm, tm), 1)
    lower = jnp.where(rr > cc, 1.0, 0.0).astype(BF16)
    before = jnp.dot(lower, c.astype(BF16), preferred_element_type=F32) + carry_ref[0:1, :]
    r1 = jnp.sum(jnp.where(hit1, before, 0.0), axis=-1, keepdims=True)
    r2 = jnp.sum(jnp.where(hit2, before, 0.0), axis=-1, keepdims=True)
    total = carry_ref[0:1, :] + jnp.sum(c, axis=0, keepdims=True)
    carry_ref[...] = jnp.broadcast_to(total, carry_ref.shape)
    cnt_ref[...] = jnp.broadcast_to(total, cnt_ref.shape)

    rec = jnp.zeros((tm, LANES), F32)
    for idx, val in ((ROUTE_I1, i1), (ROUTE_I2, i2), (ROUTE_R1, r1), (ROUTE_R2, r2), (ROUTE_W1, w1), (ROUTE_W2, w2)):
        rec = jnp.where(lane_i == idx, val, rec)
    route_ref[...] = rec


def _route(h, g_ffn, w_router, *, tm):
    n = h.shape[0]
    tok = lambda w: pl.BlockSpec((tm, w), lambda i: (i, 0))
    return pl.pallas_call(
        _route_kernel,
        grid=(n // tm,),
        in_specs=[tok(D_MODEL), _full((1, D_MODEL)), _full((D_MODEL, LANES))],
        out_specs=[tok(LANES), _full((SUBLANES, LANES))],
        out_shape=[jax.ShapeDtypeStruct((n, LANES), F32), jax.ShapeDtypeStruct((SUBLANES, LANES), F32)],
        scratch_shapes=[pltpu.VMEM((SUBLANES, LANES), F32)],
        compiler_params=_cparams(1),
        name="route",
    )(h, g_ffn, w_router)


def _dispatch_kernel(pos_ref, h_ref, g_ref, xs_ref, mbuf, sem):
    tm = mbuf.shape[0]
    mbuf[...] = _rms(h_ref[...], g_ref[...])

    def row_copy(t, p):
        return pltpu.make_async_copy(mbuf.at[pl.ds(t, 1), :], xs_ref.at[pl.ds(p, 1), :], sem)

    def issue(t, carry):
        row_copy(t, pos_ref[0, 0, 2 * t]).start()
        row_copy(t, pos_ref[0, 0, 2 * t + 1]).start()
        return carry

    lax.fori_loop(0, tm, issue, 0)

    def drain(t, carry):
        row_copy(0, 0).wait()
        row_copy(0, 0).wait()
        return carry

    lax.fori_loop(0, tm, drain, 0)


def _dispatch(h, g_ffn, pos, n_rows, *, tm):
    n = h.shape[0]
    return pl.pallas_call(
        _dispatch_kernel,
        grid=(n // tm,),
        in_specs=[pl.BlockSpec((1, 1, 2 * tm), lambda i: (i, 0, 0), memory_space=pltpu.SMEM),
                  pl.BlockSpec((tm, D_MODEL), lambda i: (i, 0)), _full((1, D_MODEL))],
        out_specs=pl.BlockSpec(memory_space=pl.ANY),
        out_shape=jax.ShapeDtypeStruct((n_rows, D_MODEL), F32),
        scratch_shapes=[pltpu.VMEM((tm, D_MODEL), F32), pltpu.SemaphoreType.DMA(())],
        compiler_params=pltpu.CompilerParams(dimension_semantics=("arbitrary",), vmem_limit_bytes=VMEM_LIMIT,
                                             has_side_effects=True),
        name="dispatch",
    )(pos, h, g_ffn)


def _expert_kernel(te_ref, tv_ref, x_ref, wg_ref, wu_ref, wd_ref, y_ref, wg_bf, wu_bf, wd_bf):
    i = pl.program_id(0)
    valid = tv_ref[i]
    changed = jnp.logical_or(i == 0, te_ref[i] != te_ref[jnp.maximum(i - 1, 0)])

    @pl.when(jnp.logical_and(changed, valid > 0))
    def _():
        wg_bf[...] = wg_ref[0].astype(BF16)
        wu_bf[...] = wu_ref[0].astype(BF16)
        wd_bf[...] = wd_ref[0].astype(BF16)

    @pl.when(valid > 0)
    def _():
        te = x_ref.shape[0]
        row = lax.broadcasted_iota(jnp.int32, (te, 1), 0)
        x = jnp.where(row < valid, x_ref[...], 0.0).astype(BF16)
        hg = jnp.dot(x, wg_bf[...], preferred_element_type=F32)
        hu = jnp.dot(x, wu_bf[...], preferred_element_type=F32)
        hid = (hg * jax.nn.sigmoid(hg)) * hu
        y_ref[...] = jnp.dot(hid.astype(BF16), wd_bf[...], preferred_element_type=F32)

    @pl.when(valid <= 0)
    def _():
        y_ref[...] = jnp.zeros_like(y_ref)


def _experts(xs, tile_expert, tile_valid, w_gate, w_up, w_down, *, te):
    n_rows = xs.shape[0]
    wspec = lambda a, b: pl.BlockSpec((1, a, b), lambda i, e, v: (e[i], 0, 0))
    grid_spec = pltpu.PrefetchScalarGridSpec(
        num_scalar_prefetch=2,
        grid=(n_rows // te,),
        in_specs=[pl.BlockSpec((te, D_MODEL), lambda i, e, v: (i, 0)),
                  wspec(D_MODEL, D_EXPERT), wspec(D_MODEL, D_EXPERT), wspec(D_EXPERT, D_MODEL)],
        out_specs=pl.BlockSpec((te, D_MODEL), lambda i, e, v: (i, 0)),
        scratch_shapes=[pltpu.VMEM((D_MODEL, D_EXPERT), BF16), pltpu.VMEM((D_MODEL, D_EXPERT), BF16),
                        pltpu.VMEM((D_EXPERT, D_MODEL), BF16)])
    return pl.pallas_call(
        _expert_kernel,
        grid_spec=grid_spec,
        out_shape=jax.ShapeDtypeStruct((n_rows, D_MODEL), F32),
        compiler_params=_cparams(1),
        name="experts",
    )(tile_expert, tile_valid, xs, w_gate, w_up, w_down)


def _combine_kernel(pos_ref, h_ref, route_ref, ys_ref, p_ref, gple_ref, wgate_ref, wproj_ref, out_ref,
                    y0, y1, sem):
    tm = y0.shape[0]

    def row_copy(p, dst, t):
        return pltpu.make_async_copy(ys_ref.at[pl.ds(p, 1), :], dst.at[pl.ds(t, 1), :], sem)

    def issue(t, carry):
        row_copy(pos_ref[0, 0, 2 * t], y0, t).start()
        row_copy(pos_ref[0, 0, 2 * t + 1], y1, t).start()
        return carry

    lax.fori_loop(0, tm, issue, 0)

    def drain(t, carry):
        row_copy(0, y0, 0).wait()
        row_copy(0, y1, 0).wait()
        return carry

    lax.fori_loop(0, tm, drain, 0)

    rec = route_ref[...]
    w1 = rec[:, ROUTE_W1:ROUTE_W1 + 1]
    w2 = rec[:, ROUTE_W2:ROUTE_W2 + 1]
    h2 = h_ref[...] + w1 * y0[...] + w2 * y1[...]
    gate = jax.nn.sigmoid(jnp.dot(_rms(h2, gple_ref[...]).astype(BF16), wgate_ref[...], preferred_element_type=F32))
    ple = jnp.dot(p_ref[...].astype(BF16), wproj_ref[...], preferred_element_type=F32)
    out_ref[...] = h2 + ple * gate


def _combine(h, route, pos, ys, p, g_ple, wgate_bf, wproj_bf, *, tm):
    n = h.shape[0]
    tok = lambda w: pl.BlockSpec((tm, w), lambda i: (i, 0))
    return pl.pallas_call(
        _combine_kernel,
        grid=(n // tm,),
        in_specs=[pl.BlockSpec((1, 1, 2 * tm), lambda i: (i, 0, 0), memory_space=pltpu.SMEM),
                  tok(D_MODEL), tok(LANES), pl.BlockSpec(memory_space=pl.ANY), tok(D_PLE),
                  _full((1, D_MODEL)), _full((D_MODEL, D_MODEL)), _full((D_PLE, D_MODEL))],
        out_specs=tok(D_MODEL),
        out_shape=jax.ShapeDtypeStruct((n, D_MODEL), F32),
        scratch_shapes=[pltpu.VMEM((tm, D_MODEL), F32), pltpu.VMEM((tm, D_MODEL), F32),
                        pltpu.SemaphoreType.DMA(())],
        compiler_params=_cparams(1),
        name="combine",
    )(pos, h, route, ys, p, g_ple, wgate_bf, wproj_bf)


def _lookup(table, idx, size):
    hit = idx[..., None] == jnp.arange(size, dtype=jnp.int32)
    return jnp.sum(jnp.where(hit, table, 0), axis=-1)


def _moe_ple(h, p, g_ffn, w_router, w_gate, w_up, w_down, g_ple, wgate_bf, wproj_bf, *, tm, te):
    n = h.shape[0]
    route, counts = _route(h, g_ffn, w_router, tm=tm)
    counts = counts[0, :N_EXPERTS].astype(jnp.int32)
    padded = ((counts + te - 1) // te) * te
    ends = jnp.cumsum(padded)
    offs = ends - padded
    ids = route[:, ROUTE_I1:ROUTE_I2 + 1].astype(jnp.int32)
    ranks = route[:, ROUTE_R1:ROUTE_R2 + 1].astype(jnp.int32)
    pos = (_lookup(offs, ids, N_EXPERTS) + ranks).reshape(n // tm, 1, 2 * tm)
    n_tiles = (2 * n) // te + N_EXPERTS
    starts = jnp.arange(n_tiles, dtype=jnp.int32) * te
    tile_expert = jnp.minimum(jnp.sum((starts[:, None] >= ends[None, :]).astype(jnp.int32), axis=-1), N_EXPERTS - 1)
    tile_valid = jnp.clip(_lookup(counts, tile_expert, N_EXPERTS)
                          - (starts - _lookup(offs, tile_expert, N_EXPERTS)), 0, te).astype(jnp.int32)
    xs = _dispatch(h, g_ffn, pos, n_tiles * te, tm=tm)
    ys = _experts(xs, tile_expert, tile_valid, w_gate, w_up, w_down, te=te)
    return _combine(h, route, pos, ys, p, g_ple, wgate_bf, wproj_bf, tm=tm)


REC_E1, REC_E2, REC_S1, REC_S2, REC_W1, REC_W2 = range(6)
PACKED = D_MODEL // 2
ROW_W = PACKED + LANES
LIST_LANES = LANES
LIST_COUNT = LIST_LANES - 1


def _slots(tm):
    need = 2 * tm + (SUBLANES - 1) * N_EXPERTS + SUBLANES
    return -(-need // LANES) * LANES


def _pieces(x):
    hi = x.astype(BF16)
    r1 = x - hi.astype(F32)
    mid = r1.astype(BF16)
    return hi, mid, (r1 - mid.astype(F32)).astype(BF16)


def _sort_kernel(h_ref, g_ref, wr_ref, before_ref, sel_ref, xs_ref, rec_ref, cnt_ref):
    m = _rms(h_ref[...], g_ref[...])
    logits = jnp.dot(m.astype(BF16), wr_ref[...], preferred_element_type=F32)
    tm = logits.shape[0]
    slots = xs_ref.shape[0]
    lane_i = lax.broadcasted_iota(jnp.int32, (tm, LANES), 1)
    lane = lane_i.astype(F32)
    big = jnp.float32(4 * LANES)

    is_g = jnp.logical_and(lane_i >= GROUP_LANE0, lane_i < GROUP_LANE0 + N_GROUPS)
    gl = jnp.where(is_g, logits, NEG)
    gmax = jnp.max(gl, axis=-1, keepdims=True)
    g_w = 1.0 / jnp.sum(jnp.where(is_g, jnp.exp(gl - gmax), 0.0), axis=-1, keepdims=True)
    g_sel = jnp.min(jnp.where(gl == gmax, lane - GROUP_LANE0, big), axis=-1, keepdims=True)

    grp_of_lane = (lane_i // EXPERTS_PER_GROUP).astype(F32)
    in_grp = jnp.logical_and(lane_i < N_EXPERTS, grp_of_lane == g_sel)
    el = jnp.where(in_grp, logits, NEG)
    t1 = jnp.max(el, axis=-1, keepdims=True)
    e1 = jnp.min(jnp.where(el == t1, lane, big), axis=-1, keepdims=True)
    el2 = jnp.where(lane == e1, NEG, el)
    t2 = jnp.max(el2, axis=-1, keepdims=True)
    e2 = jnp.min(jnp.where(el2 == t2, lane, big), axis=-1, keepdims=True)
    ex = jnp.exp(t2 - t1)
    w1 = g_w / (1.0 + ex)
    w2 = g_w * ex / (1.0 + ex)

    hit1 = lane == e1
    hit2 = lane == e2
    c = jnp.where(hit1, 1.0, jnp.where(hit2, 1.0, 0.0))
    rr = lax.broadcasted_iota(jnp.int32, (tm, tm), 0)
    cc = lax.broadcasted_iota(jnp.int32, (tm, tm), 1)
    lower = jnp.where(rr > cc, 1.0, 0.0).astype(BF16)
    rank = jnp.dot(lower, c.astype(BF16), preferred_element_type=F32)
    cnt = jnp.sum(c, axis=0, keepdims=True)
    chunks = jnp.floor((cnt + (SUBLANES - 1)) * (1.0 / SUBLANES))
    start = SUBLANES * jnp.dot(jnp.broadcast_to(chunks, (SUBLANES, LANES)).astype(BF16), before_ref[...],
                               preferred_element_type=F32)[0:1, :]
    slot_of = rank + start
    s1 = jnp.sum(jnp.where(hit1, slot_of, 0.0), axis=-1, keepdims=True)
    s2 = jnp.sum(jnp.where(hit2, slot_of, 0.0), axis=-1, keepdims=True)

    rec = jnp.zeros((tm, LANES), F32)
    for idx, val in ((REC_E1, e1), (REC_E2, e2), (REC_S1, s1), (REC_S2, s2), (REC_W1, w1), (REC_W2, w2)):
        rec = jnp.where(lane_i == idx, val, rec)
    rec_ref[...] = rec
    cnt_ref[0] = jnp.broadcast_to(cnt, (SUBLANES, LANES))

    rec_parts = _pieces(rec)
    srow = sum(lax.dot_general(sel_ref[...], part, NT, preferred_element_type=F32) for part in rec_parts)
    slot_id = lax.broadcasted_iota(jnp.int32, (slots, tm), 0).astype(F32)
    place = jnp.where(slot_id == srow[0:1, :], 1.0, jnp.where(slot_id == srow[1:2, :], 1.0, 0.0)).astype(BF16)
    payload = jnp.concatenate([m.astype(BF16), *rec_parts], axis=-1)
    moved = jnp.dot(place, payload, preferred_element_type=F32)
    info = (moved[:, D_MODEL:D_MODEL + LANES] + moved[:, D_MODEL + LANES:D_MODEL + 2 * LANES]
            + moved[:, D_MODEL + 2 * LANES:])
    xs_ref[...] = jnp.concatenate([_pack_bf16(moved[:, :D_MODEL]), pltpu.bitcast(info, jnp.uint32)], axis=-1)


def _sort(h, g_ffn, w_router, before, sel, *, tm):
    n = h.shape[0]
    slots = _slots(tm)
    tok = lambda w: pl.BlockSpec((tm, w), lambda i: (i, 0))
    return pl.pallas_call(
        _sort_kernel,
        grid=(n // tm,),
        in_specs=[tok(D_MODEL), _full((1, D_MODEL)), _full((D_MODEL, LANES)), _full((LANES, LANES)),
                  _full((SUBLANES, LANES))],
        out_specs=[pl.BlockSpec((slots, ROW_W), lambda i: (i, 0)), tok(LANES),
                   pl.BlockSpec((1, SUBLANES, LANES), lambda i: (i, 0, 0))],
        out_shape=[jax.ShapeDtypeStruct((n // tm * slots, ROW_W), jnp.uint32), jax.ShapeDtypeStruct((n, LANES), F32),
                   jax.ShapeDtypeStruct((n // tm, SUBLANES, LANES), F32)],
        compiler_params=_cparams(1),
        name="moe_sort",
    )(h, g_ffn, w_router, before, sel)


def _chunk_copy(src_hbm, row, dst, c, sem):
    if not isinstance(row, int):
        row = pl.multiple_of(row, SUBLANES)
    to = c * SUBLANES
    if not isinstance(to, int):
        to = pl.multiple_of(to, SUBLANES)
    return pltpu.make_async_copy(src_hbm.at[pl.ds(row, SUBLANES), :], dst.at[pl.ds(to, SUBLANES), :], sem)


def _expert_kernel2(te_ref, tv_ref, lst_ref, xa_ref, xb_ref, wg_ref, wu_ref, wd_ref, y_ref,
                    xbuf, sem, wg_bf, wu_bf, wd_bf, *, rows_a):
    i = pl.program_id(0)
    n = pl.num_programs(0)
    te = xbuf.shape[1]
    slot = i % 2

    def fetch(tile, to_slot):
        n_a = lst_ref[tile, LIST_COUNT]

        def from_a(c, carry):
            _chunk_copy(xa_ref, lst_ref[tile, c], xbuf.at[to_slot], c, sem.at[to_slot]).start()
            return carry

        def from_b(c, carry):
            _chunk_copy(xb_ref, lst_ref[tile, c] - rows_a, xbuf.at[to_slot], c, sem.at[to_slot]).start()
            return carry

        per_tile = te // SUBLANES

        @pl.when(n_a == per_tile)
        def _():
            for c in range(per_tile):
                from_a(c, 0)

        @pl.when(n_a != per_tile)
        def _():
            lax.fori_loop(0, n_a, from_a, 0)
            lax.fori_loop(n_a, per_tile, from_b, 0)

    @pl.when(jnp.logical_and(i == 0, tv_ref[0] > 0))
    def _():
        fetch(0, 0)

    nxt = jnp.minimum(i + 1, n - 1)

    @pl.when(jnp.logical_and(i + 1 < n, tv_ref[nxt] > 0))
    def _():
        fetch(nxt, 1 - slot)

    valid = tv_ref[i]
    changed = jnp.logical_or(i == 0, te_ref[i] != te_ref[jnp.maximum(i - 1, 0)])

    @pl.when(jnp.logical_and(changed, valid > 0))
    def _():
        wg_bf[...] = wg_ref[0].astype(BF16)
        wu_bf[...] = wu_ref[0].astype(BF16)
        wd_bf[...] = wd_ref[0].astype(BF16)

    @pl.when(valid > 0)
    def _():
        pltpu.make_async_copy(xbuf.at[slot], xbuf.at[slot], sem.at[slot]).wait()
        rows = xbuf[slot]
        x = _unpack_bf16(rows[:, :PACKED])
        info = pltpu.bitcast(rows[:, PACKED:], F32)
        mine = info[:, REC_E1:REC_E1 + 1] == te_ref[i].astype(F32)
        gate = jnp.where(mine, info[:, REC_W1:REC_W1 + 1], info[:, REC_W2:REC_W2 + 1])
        hg = jnp.dot(x, wg_bf[...], preferred_element_type=F32)
        hu = jnp.dot(x, wu_bf[...], preferred_element_type=F32)
        hid = (hg * jax.nn.sigmoid(hg)) * hu * gate
        y = jnp.dot(hid.astype(BF16), wd_bf[...], preferred_element_type=F32)
        y_ref[...] = _pack_bf16(y)

    @pl.when(valid <= 0)
    def _():
        y_ref[...] = jnp.zeros_like(y_ref)


def _pack_bf16(x):
    w = x.shape[-1] // 2
    bits = lambda a: pltpu.bitcast(a.astype(BF16).astype(F32), jnp.uint32)
    return jnp.bitwise_or(lax.shift_right_logical(bits(x[:, :w]), jnp.uint32(16)),
                          jnp.bitwise_and(bits(x[:, w:]), jnp.uint32(0xFFFF0000)))


def _unpack_bf16(words):
    as_f32 = lambda a: pltpu.bitcast(a, F32)
    return jnp.concatenate([as_f32(lax.shift_left(words, jnp.uint32(16))),
                            as_f32(jnp.bitwise_and(words, jnp.uint32(0xFFFF0000)))], axis=-1).astype(BF16)


def _experts2(xs_a, xs_b, tile_expert, tile_valid, src, w_gate, w_up, w_down, *, te, layer):
    n_tiles = tile_expert.shape[0]
    wspec = lambda a, b: pl.BlockSpec((None, 1, a, b), lambda i, e, v, s: (layer, e[i], 0, 0))
    grid_spec = pltpu.PrefetchScalarGridSpec(
        num_scalar_prefetch=3,
        grid=(n_tiles,),
        in_specs=[pl.BlockSpec(memory_space=pl.ANY), pl.BlockSpec(memory_space=pl.ANY),
                  wspec(D_MODEL, D_EXPERT), wspec(D_MODEL, D_EXPERT), wspec(D_EXPERT, D_MODEL)],
        out_specs=pl.BlockSpec((te, PACKED), lambda i, e, v, s: (i, 0)),
        scratch_shapes=[pltpu.VMEM((2, te, ROW_W), jnp.uint32), pltpu.SemaphoreType.DMA((2,)),
                        pltpu.VMEM((D_MODEL, D_EXPERT), BF16), pltpu.VMEM((D_MODEL, D_EXPERT), BF16),
                        pltpu.VMEM((D_EXPERT, D_MODEL), BF16)])
    return pl.pallas_call(
        functools.partial(_expert_kernel2, rows_a=xs_a.shape[0]),
        grid_spec=grid_spec,
        out_shape=jax.ShapeDtypeStruct((n_tiles * te, PACKED), jnp.uint32),
        compiler_params=_cparams(1),
        name="experts",
    )(tile_expert, tile_valid, src.reshape(n_tiles, LIST_LANES), xs_a, xs_b, w_gate, w_up, w_down)


def _combine_kernel2(lst_ref, h_ref, rec_ref, ys_ref, p_ref, gple_ref, wgate_ref, wproj_ref, out_ref,
                     ybuf, sem):
    i = pl.program_id(0)
    n = pl.num_programs(0)
    slot = i % 2
    tm = h_ref.shape[0]
    slots = ybuf.shape[1]

    def fetch(tile, to_slot):
        for c in range(slots // SUBLANES):
            _chunk_copy(ys_ref, lst_ref[tile, c], ybuf.at[to_slot], c, sem.at[to_slot]).start()

    @pl.when(i == 0)
    def _():
        fetch(0, 0)

    @pl.when(i + 1 < n)
    def _():
        fetch(i + 1, 1 - slot)

    pltpu.make_async_copy(ybuf.at[slot], ybuf.at[slot], sem.at[slot]).wait()

    rec = rec_ref[...]
    slot_id = lax.broadcasted_iota(jnp.int32, (tm, slots), 1).astype(F32)
    back = jnp.where(slot_id == rec[:, REC_S1:REC_S1 + 1], 1.0,
                     jnp.where(slot_id == rec[:, REC_S2:REC_S2 + 1], 1.0, 0.0)).astype(BF16)
    h2 = h_ref[...] + jnp.dot(back, _unpack_bf16(ybuf[slot]), preferred_element_type=F32)
    gate = jax.nn.sigmoid(jnp.dot(_rms(h2, gple_ref[...]).astype(BF16), wgate_ref[...], preferred_element_type=F32))
    ple = jnp.dot(p_ref[...].astype(BF16), wproj_ref[...], preferred_element_type=F32)
    out_ref[...] = h2 + ple * gate


def _combine2(h, rec, dst, ys, p, g_ple, wgate_bf, wproj_bf, *, tm, layer):
    n = h.shape[0]
    n_tok = n // tm
    tok = lambda w: pl.BlockSpec((tm, w), lambda i, s: (i, 0))
    full = lambda shape: pl.BlockSpec(shape, lambda i, s: (0,) * len(shape))
    grid_spec = pltpu.PrefetchScalarGridSpec(
        num_scalar_prefetch=1,
        grid=(n_tok,),
        in_specs=[tok(D_MODEL), tok(LANES), pl.BlockSpec(memory_space=pl.ANY),
                  pl.BlockSpec((None, tm, D_PLE), lambda i, s: (layer, i, 0)),
                  full((1, D_MODEL)), full((D_MODEL, D_MODEL)), full((D_PLE, D_MODEL))],
        out_specs=tok(D_MODEL),
        scratch_shapes=[pltpu.VMEM((2, _slots(tm), PACKED), jnp.uint32), pltpu.SemaphoreType.DMA((2,))])
    return pl.pallas_call(
        _combine_kernel2,
        grid_spec=grid_spec,
        out_shape=jax.ShapeDtypeStruct((n, D_MODEL), F32),
        compiler_params=_cparams(1),
        name="combine",
    )(dst.reshape(n_tok, LIST_LANES), h, rec, ys, p, g_ple, wgate_bf, wproj_bf)


def _excl_cumsum(x, axis):
    return jnp.cumsum(x, axis=axis) - x


def _chunk_plan(cnt, tile_row0, *, n_pairs, rows_a, zero_row, te):
    n_tok = cnt.shape[0]
    per_tile = te // SUBLANES
    chunks = (cnt + SUBLANES - 1) // SUBLANES
    run0 = _excl_cumsum(chunks, 1)
    n_run = jnp.sum(chunks, axis=1)
    seg0 = _excl_cumsum(chunks, 0)
    total = jnp.sum(chunks, axis=0)
    region = ((total + per_tile - 1) // per_tile) * per_tile
    reg_end = jnp.cumsum(region)
    reg0 = reg_end - region

    n_tiles = -(-(n_pairs + (SUBLANES - 1) * N_EXPERTS * n_tok) // te) + N_EXPERTS
    t0 = jnp.arange(n_tiles, dtype=jnp.int32) * per_tile
    tile_expert = jnp.minimum(jnp.sum((t0[:, None] >= reg_end[None, :]).astype(jnp.int32), axis=1), N_EXPERTS - 1)
    pick = tile_expert[:, None] == jnp.arange(N_EXPERTS, dtype=jnp.int32)[None, :]
    of_tile = lambda v: jnp.sum(jnp.where(pick, v[None, :], 0), axis=1)
    tile_valid = jnp.clip(of_tile(total) - (t0 - of_tile(reg0)), 0, per_tile).astype(jnp.int32)

    q = (t0 - of_tile(reg0))[:, None] + jnp.arange(per_tile, dtype=jnp.int32)[None, :]
    col_of_tile = lambda v: jnp.sum(jnp.where(pick[:, None, :], v[None, :, :], 0), axis=2)
    seg0_t, seg1_t, run0_t = col_of_tile(seg0), col_of_tile(seg0 + chunks), col_of_tile(run0)
    holds = (q[:, :, None] >= seg0_t[:, None, :]) & (q[:, :, None] < seg1_t[:, None, :])
    local = run0_t[:, None, :] + q[:, :, None] - seg0_t[:, None, :]
    row = jnp.asarray(tile_row0, jnp.int32)[None, None, :] + SUBLANES * local
    src = jnp.sum(jnp.where(holds, row, 0), axis=2)
    src = jnp.where(jnp.any(holds, axis=2), src, zero_row)
    n_first = jnp.sum((src < rows_a).astype(jnp.int32), axis=1)
    lane = jnp.arange(LIST_LANES, dtype=jnp.int32)[None, :]
    src = jnp.where(lane == LIST_COUNT, n_first[:, None], jnp.pad(src, ((0, 0), (0, LIST_LANES - per_tile))))
    src = src.reshape(n_tiles, 1, LIST_LANES).astype(jnp.int32)

    j = jnp.arange(LIST_LANES, dtype=jnp.int32)[None, :, None]
    inside = (j >= run0[:, None, :]) & (j < (run0 + chunks)[:, None, :])
    base = (reg0[None, :] + seg0 - run0)[:, None, :]
    dst = SUBLANES * jnp.sum(jnp.where(inside, base + j, 0), axis=2)
    dst = dst.reshape(n_tok, 1, LIST_LANES).astype(jnp.int32)
    return tile_expert.astype(jnp.int32), tile_valid, src, dst


def _moe_ple2(h_a, h_b, p_a, p_b, g_ffn, w_router, before, sel, w_gate, w_up, w_down, g_ple, wgate_bf, wproj_bf,
              *, tm_a, tm_b, te, layer):
    xs_a, rec_a, cnt_a = _sort(h_a, g_ffn, w_router, before, sel, tm=tm_a)
    xs_b, rec_b, cnt_b = _sort(h_b, g_ffn, w_router, before, sel, tm=tm_b)
    t_a, t_b = cnt_a.shape[0], cnt_b.shape[0]
    cnt = jnp.concatenate([cnt_a[:, 0, :N_EXPERTS], cnt_b[:, 0, :N_EXPERTS]], axis=0).astype(jnp.int32)
    rows_a = xs_a.shape[0]
    row0 = np.concatenate([np.arange(t_a) * _slots(tm_a), rows_a + np.arange(t_b) * _slots(tm_b)])
    tile_expert, tile_valid, src, dst = _chunk_plan(
        cnt, row0, n_pairs=2 * (h_a.shape[0] + h_b.shape[0]), rows_a=rows_a,
        zero_row=rows_a + _slots(tm_b) - SUBLANES, te=te)
    ys = _experts2(xs_a, xs_b, tile_expert, tile_valid, src, w_gate, w_up, w_down, te=te, layer=layer)
    out_a = _combine2(h_a, rec_a, dst[:t_a], ys, p_a, g_ple, wgate_bf, wproj_bf, tm=tm_a, layer=layer)
    out_b = _combine2(h_b, rec_b, dst[t_a:], ys, p_b, g_ple, wgate_bf, wproj_bf, tm=tm_b, layer=layer)
    return out_a, out_b


def _bucket_np(dist):
    max_exact = N_BUCKETS // 2
    d_f = np.maximum(dist, 1).astype(np.float32)
    large = max_exact + (np.log(d_f / np.float32(max_exact)) / np.float32(np.log(MAX_DISTANCE / max_exact))
                         * np.float32(N_BUCKETS - max_exact)).astype(np.int32)
    large = np.minimum(large, N_BUCKETS - 1)
    return np.where(dist < max_exact, dist, large).astype(np.int32)


def _bias_from_buckets(rel_bias, bucket, valid):
    onehot = (jnp.asarray(bucket)[..., None] == jnp.arange(N_BUCKETS, dtype=jnp.int32)).astype(F32)
    bias = jnp.einsum("...k,kh->h...", onehot, rel_bias.astype(F32), precision=HIGHEST)
    return jnp.where(jnp.asarray(valid)[None], bias, NEG)


def _block_order(dil):
    g = np.arange(Q_BLOCK) // SUBLANES
    j = np.arange(Q_BLOCK) % SUBLANES
    if dil == 1:
        return 16 * j + g
    if dil == 4:
        return 32 * (g // 4) + 4 * j + g % 4
    return SUBLANES * g + j


def _band_bias(rel_bias, dil):
    mu = _block_order(dil)
    qi = mu[:, None] + Q_BLOCK
    ki = np.concatenate([mu, mu + Q_BLOCK])[None, :]
    off = qi - ki
    valid = (off >= 0) & (off <= N_KEYS)
    bucket = _bucket_np(dil * np.clip(off, 0, N_KEYS))
    first = valid & (np.arange(2 * Q_BLOCK)[None, :] >= Q_BLOCK)
    tables = [_bias_from_buckets(rel_bias, bucket, v).reshape(N_HEADS * Q_BLOCK, 2 * Q_BLOCK) for v in (valid, first)]
    return jnp.stack(tables)


def _sample_tables(rel_bias, w_buf):
    qpos = w_buf + np.arange(SAMPLE_T)[:, None]
    pos = np.arange(w_buf + NEW_COLS)[None, :]
    dist = qpos - pos
    in_seq = (dist >= 0) & (pos < w_buf + SAMPLE_T)
    mult = np.zeros(dist.shape, np.float32)
    for (w, d) in PATTERNS:
        mult += in_seq & (dist % d == 0) & (dist <= w)
    bucket = _bucket_np(np.maximum(dist, 0))
    bias = jnp.transpose(_bias_from_buckets(rel_bias, bucket, mult > 0), (1, 0, 2))
    rows = SAMPLE_T * N_HEADS
    mult_rows = np.broadcast_to(mult[:, None, :], (SAMPLE_T, N_HEADS, mult.shape[-1]))
    return bias.reshape(rows, -1), jnp.asarray(mult_rows.reshape(rows, -1))


PROJ_TM = 256
OUT_TM = 512
PROMPT_TM = 256
PROMPT_TE = 256
ATTN_SUB = 4


def _row_perm(tm):
    a = np.arange(tm)
    src = (a // Q_BLOCK) * Q_BLOCK + 16 * (a % SUBLANES) + (a % Q_BLOCK) // SUBLANES
    perm = np.zeros((tm, tm), np.float32)
    perm[a, src] = 1.0
    return perm


def kernel(x_prompt, x_sample, cache_k, cache_v, state_conv, p_prompt, p_sample, rel_bias, g_mix, w_in, q_gain,
           k_gain, conv_w, g_out_att, g_out_conv, w_out, g_ffn, w_router_group, w_router_expert, w_gate, w_up,
           w_down, g_ple, w_ple_gate, w_ple_proj):
    depth = w_in.shape[0]
    batch, seq, _ = x_prompt.shape
    dec_b, dec_t, _ = x_sample.shape
    w_buf = cache_k.shape[2]
    n_s = dec_b * dec_t
    keep = min(w_buf, seq)
    assert dec_t == SAMPLE_T and w_buf % LANES == 0
    assert seq % (Q_BLOCK * 16 * 2) == 0 and keep % PROJ_TM == 0 and seq % PROMPT_TM == 0
    cache_kt = jnp.transpose(cache_k, (0, 1, 3, 4, 2))
    cache_vt = jnp.transpose(cache_v, (0, 1, 3, 4, 2))

    row = lambda a: a.reshape(1, -1)
    src_lane = np.arange(LANES)
    expand = jnp.asarray((src_lane[:, None] // LSE_LANES_PER_HEAD == np.arange(ATT_DIM)[None, :] // HEAD_DIM)
                         & (src_lane[:, None] % LSE_LANES_PER_HEAD == 0), BF16)
    before = jnp.asarray(np.arange(LANES)[:, None] < np.arange(LANES)[None, :], BF16)
    sel_np = np.zeros((SUBLANES, LANES), np.float32)
    sel_np[0, REC_S1] = sel_np[1, REC_S2] = 1.0
    sel = jnp.asarray(sel_np, BF16)
    perm = jnp.asarray(_row_perm(PROJ_TM), BF16)
    unperm = jnp.asarray(_row_perm(OUT_TM).T, BF16)
    band = [_band_bias(rel_bias, d) for (_, d) in PATTERNS]
    s_bias, s_mult = _sample_tables(rel_bias, w_buf)

    hp = x_prompt.reshape(batch * seq, D_MODEL)
    hs = jnp.swapaxes(x_sample, 0, 1).reshape(n_s, D_MODEL)
    pp_all = p_prompt.reshape(depth, batch * seq, D_PLE)
    ps_all = jnp.swapaxes(p_sample, 1, 2).reshape(depth, n_s, D_PLE)
    new = {k: [] for k in ("kp", "vp", "cp", "ks", "vs", "cs")}
    hist_p = jnp.zeros((batch, SUBLANES, CONV_DIM), F32)

    for l in range(depth):
        w_in_bf = w_in[l].astype(BF16)
        wa_bf = w_out[l, :ATT_DIM].astype(BF16)
        wc_bf = w_out[l, ATT_DIM:].astype(BF16)
        wgate_bf = w_ple_gate[l].astype(BF16)
        wproj_bf = w_ple_proj[l].astype(BF16)
        w_router = jnp.concatenate(
            [w_router_expert[l], w_router_group[l],
             jnp.zeros((D_MODEL, LANES - N_EXPERTS - N_GROUPS), F32)], axis=1).astype(BF16)
        qg, kg = row(jnp.tile(q_gain[l], N_HEADS)), row(jnp.tile(k_gain[l], N_HEADS))
        mix = (row(g_mix[l]), w_in_bf, qg, kg, conv_w[l], row(g_out_conv[l]))
        moe = (row(g_ffn[l]), w_router, before, sel, w_gate, w_up, w_down, row(g_ple[l]), wgate_bf, wproj_bf)

        q, k, v, k_nat, v_nat, yn, nconv = _inproj(
            hp, hist_p, *mix, perm, tm=PROJ_TM, shift=1, tiles_per_seq=seq // PROJ_TM,
            keep_tiles=keep // PROJ_TM)
        os, lses = [], []
        for bias, (_, d) in zip(band, PATTERNS):
            sub = min(ATTN_SUB, seq // (Q_BLOCK * d))
            o, lse = _attn_pattern(q, k, v, bias, batch=batch, seq=seq, dil=d, sub=sub,
                                   res=min(d, ATTN_SUB // sub))
            os.append(o)
            lses.append(lse)
        hp = _outproj(os, lses, yn, hp, row(g_out_att[l]), wa_bf, wc_bf, expand, unperm, tm=OUT_TM)
        new["kp"].append(jnp.transpose(k_nat, (0, 3, 1, 2)))
        new["vp"].append(jnp.transpose(v_nat, (0, 3, 1, 2)))
        new["cp"].append(nconv[:, SUBLANES - 2:])

        hist_s = jnp.swapaxes(state_conv[l], 0, 1).reshape(1, 2 * dec_b, CONV_DIM)
        q, k, v, yn, nconv = _inproj(hs, hist_s, *mix, tm=n_s, shift=dec_b, tiles_per_seq=1)
        bmaj = lambda a: jnp.swapaxes(a.reshape(dec_t, dec_b, N_HEADS, HEAD_DIM), 0, 1)
        qb, kb, vb = bmaj(q), bmaj(k), bmaj(v)
        rows8 = lambda a: jnp.pad(a.reshape(dec_b, dec_t, ATT_DIM), ((0, 0), (0, SUBLANES - dec_t), (0, 0)))
        att = _attn_sample(qb.reshape(dec_b, dec_t, ATT_DIM), rows8(kb), rows8(vb), cache_kt, cache_vt, l,
                           s_bias, s_mult)
        att_tm = jnp.swapaxes(att, 0, 1).reshape(n_s, ATT_DIM)
        hs = _outproj([att_tm], [], yn, hs, row(g_out_att[l]), wa_bf, wc_bf, tm=n_s)
        hp, hs = _moe_ple2(hp, hs, pp_all, ps_all, *moe, tm_a=PROMPT_TM, tm_b=n_s, te=PROMPT_TE, layer=l)
        new["ks"].append(kb)
        new["vs"].append(vb)
        new["cs"].append(jnp.swapaxes(nconv.reshape(2, dec_b, CONV_DIM), 0, 1))

    y_prompt = hp.reshape(batch, seq, D_MODEL)
    y_sample = jnp.swapaxes(hs.reshape(dec_t, dec_b, D_MODEL), 0, 1)
    st = lambda key: jnp.stack(new[key])
    return (y_prompt, y_sample, st("kp"), st("vp"), st("cp"), st("ks"), st("vs"), st("cs"))
```

```python
import functools

import jax
import jax.numpy as jnp
import numpy as np
from jax import lax
from jax.experimental import pallas as pl
from jax.experimental.pallas import tpu as pltpu

F32 = jnp.float32
BF16 = jnp.bfloat16
HIGHEST = lax.Precision.HIGHEST

D_MODEL = 1024
HEAD_DIM = 64
N_HEADS = 8
ATT_DIM = N_HEADS * HEAD_DIM
CONV_DIM = D_MODEL - ATT_DIM
MIX_IN = 3 * ATT_DIM + 3 * CONV_DIM
PATTERNS = ((128, 1), (512, 4), (2048, 16))
N_KEYS = 128
Q_BLOCK = 128
N_BUCKETS = 32
MAX_DISTANCE = 2048
N_GROUPS = 4
EXPERTS_PER_GROUP = 8
N_EXPERTS = N_GROUPS * EXPERTS_PER_GROUP
D_EXPERT = 256
D_PLE = 256
EPS = 1e-6
NEG = -1e30

LANES = 128
SUBLANES = 8
SLABS = Q_BLOCK // SUBLANES
LSE_LANES_PER_HEAD = LANES // N_HEADS
VMEM_LIMIT = 56 * 1024 * 1024
NT = (((1,), (1,)), ((), ()))


def _cparams(n_axes):
    return pltpu.CompilerParams(dimension_semantics=("arbitrary",) * n_axes,
                                vmem_limit_bytes=VMEM_LIMIT)


def _full(shape):
    n = len(shape)
    return pl.BlockSpec(shape, lambda *_: (0,) * n)


def _rms(x, gain):
    ms = jnp.mean(x * x, axis=-1, keepdims=True)
    return x * lax.rsqrt(ms + EPS) * gain


def _exact_dot(x, e_ref):
    hi = x.astype(BF16)
    r1 = x - hi.astype(F32)
    mid = r1.astype(BF16)
    lo = (r1 - mid.astype(F32)).astype(BF16)
    e = e_ref[...]
    return (jnp.dot(hi, e, preferred_element_type=F32) + jnp.dot(mid, e, preferred_element_type=F32)
            + jnp.dot(lo, e, preferred_element_type=F32))


def _inproj_kernel(*refs, shift, tiles_per_seq, permute):
    (h_ref, gmix_ref, w_ref, qg_ref, kg_ref, cw_ref, gconv_ref, hist_ref) = refs[:8]
    if permute:
        perm_ref, q_ref, k_ref, v_ref, kn_ref, vn_ref, yn_ref, nconv_ref, carry_ref = refs[8:]
    else:
        q_ref, k_ref, v_ref, yn_ref, nconv_ref, carry_ref = refs[8:]
    i = pl.program_id(0)
    a = _rms(h_ref[...], gmix_ref[...])
    proj = jnp.dot(a.astype(BF16), w_ref[...], preferred_element_type=F32)
    tm = proj.shape[0]

    lower = lax.broadcasted_iota(jnp.int32, (tm, LANES), 1) < HEAD_DIM

    def head_norm(t, g):
        out = []
        for j in range(ATT_DIM // LANES):
            blk = t[:, j * LANES:(j + 1) * LANES]
            sq = blk * blk
            ms_lo = jnp.sum(jnp.where(lower, sq, 0.0), axis=-1, keepdims=True) * (1.0 / HEAD_DIM)
            ms_hi = jnp.sum(jnp.where(lower, 0.0, sq), axis=-1, keepdims=True) * (1.0 / HEAD_DIM)
            scale = jnp.where(lower, lax.rsqrt(ms_lo + EPS), lax.rsqrt(ms_hi + EPS))
            out.append(blk * scale)
        return jnp.concatenate(out, axis=-1) * g

    q = head_norm(proj[:, 0:ATT_DIM], qg_ref[...])
    k = head_norm(proj[:, ATT_DIM:2 * ATT_DIM], kg_ref[...])
    v = proj[:, 2 * ATT_DIM:3 * ATT_DIM]
    if permute:
        kn_ref[...] = k.T.reshape(N_HEADS, HEAD_DIM, tm)
        vn_ref[...] = v.T.reshape(N_HEADS, HEAD_DIM, tm)
        qkv = jnp.concatenate([q * (HEAD_DIM ** -0.5), k, v], axis=-1).astype(BF16)
        moved = jnp.dot(perm_ref[...], qkv, preferred_element_type=F32)
        q_ref[...] = moved[:, 0:ATT_DIM]
        k_ref[...] = moved[:, ATT_DIM:2 * ATT_DIM]
        v_ref[...] = moved[:, 2 * ATT_DIM:3 * ATT_DIM]
    else:
        q_ref[...] = q
        k_ref[...] = k
        v_ref[...] = v
    c0 = 3 * ATT_DIM
    hc = proj[:, c0:c0 + CONV_DIM]
    gb = proj[:, c0 + CONV_DIM:c0 + 2 * CONV_DIM]
    gc = proj[:, c0 + 2 * CONV_DIM:c0 + 3 * CONV_DIM]
    u = gc * hc

    if shift == 1:
        @pl.when(i % tiles_per_seq == 0)
        def _():
            carry_ref[...] = hist_ref[0]
        h0 = carry_ref[SUBLANES - 2:SUBLANES - 1, :]
        h1 = carry_ref[SUBLANES - 1:SUBLANES, :]
        row = lax.broadcasted_iota(jnp.int32, (tm, 1), 0)
        u1 = jnp.where(row == 0, h1, pltpu.roll(u, 1, 0))
        u2 = jnp.where(row == 0, h0, jnp.where(row == 1, h1, pltpu.roll(u, 2, 0)))
        carry_ref[...] = u[tm - SUBLANES:tm, :]
        nconv_ref[0] = u[tm - SUBLANES:tm, :]
    else:
        hist = hist_ref[0]
        u1 = jnp.concatenate([hist[shift:2 * shift], u[0:tm - shift]], axis=0)
        u2 = jnp.concatenate([hist, u[0:tm - 2 * shift]], axis=0)
        nconv_ref[0] = u[tm - 2 * shift:tm, :]
    conv = cw_ref[0:1, :] * u2 + cw_ref[1:2, :] * u1 + cw_ref[2:3, :] * u
    yn_ref[...] = _rms(gb * conv, gconv_ref[...])


def _inproj(h, hist, g_mix, w_in_bf, q_gain, k_gain, conv_w, g_out_conv, perm=None, *,
            tm, shift, tiles_per_seq, keep_tiles=0):
    n = h.shape[0]
    hist_rows = hist.shape[1]
    nseq = hist.shape[0]
    tok = lambda w: pl.BlockSpec((tm, w), lambda i: (i, 0))
    seq3 = lambda r: pl.BlockSpec((1, r, CONV_DIM), lambda i: (i // tiles_per_seq, 0, 0))
    nconv_rows = SUBLANES if shift == 1 else 2 * shift
    att = jax.ShapeDtypeStruct((n, ATT_DIM), F32)
    in_specs = [tok(D_MODEL), _full((1, D_MODEL)), _full((D_MODEL, MIX_IN)), _full((1, ATT_DIM)),
                _full((1, ATT_DIM)), _full((3, CONV_DIM)), _full((1, CONV_DIM)), seq3(hist_rows)]
    args = [h, g_mix, w_in_bf, q_gain, k_gain, conv_w, g_out_conv, hist]
    out_specs = [tok(ATT_DIM)] * 3
    out_shape = [att] * 3
    if perm is not None:
        in_specs.append(_full((tm, tm)))
        args.append(perm)
        first = tiles_per_seq - keep_tiles
        kept = pl.BlockSpec((None, N_HEADS, HEAD_DIM, tm),
                            lambda i: (i // tiles_per_seq, 0, 0, jnp.maximum(i % tiles_per_seq - first, 0)))
        out_specs += [kept, kept]
        out_shape += [jax.ShapeDtypeStruct((nseq, N_HEADS, HEAD_DIM, keep_tiles * tm), F32)] * 2
    out_specs += [tok(CONV_DIM), seq3(nconv_rows)]
    out_shape += [jax.ShapeDtypeStruct((n, CONV_DIM), F32),
                  jax.ShapeDtypeStruct((nseq, nconv_rows, CONV_DIM), F32)]
    return pl.pallas_call(
        functools.partial(_inproj_kernel, shift=shift, tiles_per_seq=tiles_per_seq, permute=perm is not None),
        grid=(n // tm,),
        in_specs=in_specs, out_specs=out_specs, out_shape=out_shape,
        scratch_shapes=[pltpu.VMEM((SUBLANES, CONV_DIM), F32)],
        compiler_params=_cparams(1),
        name="inproj",
    )(*args)


def _attn_block(q, kk, vv, bias):
    lane = lax.broadcasted_iota(jnp.int32, (Q_BLOCK, LANES), 1)
    upper = lane >= HEAD_DIM
    scores = []
    for h in range(N_HEADS):
        j, e = divmod(h, 2)
        qp = q[:, j * LANES:(j + 1) * LANES]
        qm = (jnp.where(upper, qp, 0.0) if e else jnp.where(upper, 0.0, qp)).astype(BF16)
        scores.append(lax.dot_general(qm, kk[:, j * LANES:(j + 1) * LANES], NT, preferred_element_type=F32))
    s = jnp.concatenate(scores, axis=0) + bias
    m = jnp.max(s, axis=-1, keepdims=True)
    p = jnp.exp(s - m)
    den = jnp.sum(p, axis=-1, keepdims=True)
    pb = p.astype(BF16)
    inv = 1.0 / den
    lse = m + jnp.log(den)
    lse_grp = lane // LSE_LANES_PER_HEAD
    lse_tile = jnp.zeros((Q_BLOCK, LANES), F32)
    outs = []
    for j in range(N_HEADS // 2):
        pair = None
        for e in range(2):
            h = 2 * j + e
            rows = slice(h * Q_BLOCK, (h + 1) * Q_BLOCK)
            o = jnp.dot(pb[rows], vv[:, j * LANES:(j + 1) * LANES], preferred_element_type=F32) * inv[rows]
            pair = o if e == 0 else jnp.where(upper, o, pair)
            lse_tile = jnp.where(lse_grp == h, lse[rows], lse_tile)
        outs.append(pair)
    return jnp.concatenate(outs, axis=-1), lse_tile


def _attn_kernel(q_ref, kp_ref, kc_ref, vp_ref, vc_ref, bias_ref, o_ref, lse_ref, kbuf, vbuf, *, sub, res):
    n = pl.program_id(2)
    rows = sub * Q_BLOCK
    for r in range(res):
        kbuf[0:Q_BLOCK, :] = kp_ref[:, r].reshape(Q_BLOCK, ATT_DIM).astype(BF16)
        kbuf[Q_BLOCK:, :] = kc_ref[:, :, r].reshape(rows, ATT_DIM).astype(BF16)
        vbuf[0:Q_BLOCK, :] = vp_ref[:, r].reshape(Q_BLOCK, ATT_DIM).astype(BF16)
        vbuf[Q_BLOCK:, :] = vc_ref[:, :, r].reshape(rows, ATT_DIM).astype(BF16)
        for j in range(sub):
            q = q_ref[j, :, r].reshape(Q_BLOCK, ATT_DIM)
            r0 = j * Q_BLOCK
            first = (n == 0).astype(jnp.int32) if j == 0 else 0
            o, lse = _attn_block(q, kbuf[r0:r0 + 2 * Q_BLOCK, :], vbuf[r0:r0 + 2 * Q_BLOCK, :], bias_ref[first])
            o_ref[j, :, r] = o.reshape(SLABS, SUBLANES, ATT_DIM)
            lse_ref[j, :, r] = lse.reshape(SLABS, SUBLANES, LANES)


def _attn_pattern(q, k, v, bias, *, batch, seq, dil, sub, res):
    nblk = seq // (Q_BLOCK * dil)
    view = lambda t: t.reshape(batch, nblk, SLABS, dil, SUBLANES, t.shape[-1])
    cur = lambda c: pl.BlockSpec((None, sub, SLABS, res, SUBLANES, c), lambda b, r, n: (b, n, 0, r, 0, 0))
    prev = pl.BlockSpec((None, None, SLABS, res, SUBLANES, ATT_DIM),
                        lambda b, r, n: (b, jnp.maximum(n * sub - 1, 0), 0, r, 0, 0))
    o, lse = pl.pallas_call(
        functools.partial(_attn_kernel, sub=sub, res=res),
        grid=(batch, dil // res, nblk // sub),
        in_specs=[cur(ATT_DIM), prev, cur(ATT_DIM), prev, cur(ATT_DIM),
                  _full((2, N_HEADS * Q_BLOCK, 2 * Q_BLOCK))],
        out_specs=[cur(ATT_DIM), cur(LANES)],
        out_shape=[jax.ShapeDtypeStruct((batch, nblk, SLABS, dil, SUBLANES, ATT_DIM), F32),
                   jax.ShapeDtypeStruct((batch, nblk, SLABS, dil, SUBLANES, LANES), F32)],
        scratch_shapes=[pltpu.VMEM(((sub + 1) * Q_BLOCK, ATT_DIM), BF16),
                        pltpu.VMEM(((sub + 1) * Q_BLOCK, ATT_DIM), BF16)],
        compiler_params=_cparams(3),
        name=f"attn_d{dil}",
    )(view(q), view(k), view(k), view(v), view(v), bias)
    return o.reshape(batch * seq, ATT_DIM), lse.reshape(batch * seq, LANES)


SAMPLE_T = 4
NEW_COLS = LANES


def _attn_sample_kernel(q_ref, kt_ref, kn_ref, vt_ref, vn_ref, bias_ref, mult_ref, o_ref):
    rows = SAMPLE_T * N_HEADS
    q4 = q_ref[...] * (HEAD_DIM ** -0.5)
    qt = jnp.concatenate([jnp.broadcast_to(q4[t:t + 1, :], (N_HEADS, ATT_DIM)) for t in range(SAMPLE_T)], axis=0)
    lane_head = lax.broadcasted_iota(jnp.int32, (rows, ATT_DIM), 1) // HEAD_DIM
    row_head = lax.broadcasted_iota(jnp.int32, (rows, ATT_DIM), 0) % N_HEADS
    own = lane_head == row_head
    qbd = jnp.where(own, qt, 0.0).astype(BF16)
    flat = lambda ref: ref[...].reshape(ATT_DIM, ref.shape[-1]).astype(BF16)
    pad = jnp.zeros((NEW_COLS - SUBLANES, ATT_DIM), F32)
    new_rows = lambda ref: jnp.concatenate([ref[...], pad], axis=0).astype(BF16)
    s = jnp.concatenate([jnp.dot(qbd, flat(kt_ref), preferred_element_type=F32),
                         lax.dot_general(qbd, new_rows(kn_ref), NT, preferred_element_type=F32)],
                        axis=-1) + bias_ref[...]
    m = jnp.max(s, axis=-1, keepdims=True)
    p = jnp.exp(s - m) * mult_ref[...]
    den = jnp.sum(p, axis=-1, keepdims=True)
    pb = p.astype(BF16)
    w_buf = kt_ref.shape[-1]
    acc = (lax.dot_general(pb[:, :w_buf], flat(vt_ref), NT, preferred_element_type=F32)
           + jnp.dot(pb[:, w_buf:], new_rows(vn_ref), preferred_element_type=F32))
    acc = jnp.where(own, acc / den, 0.0)
    for t in range(SAMPLE_T):
        o_ref[t:t + 1, :] = jnp.sum(acc[t * N_HEADS:(t + 1) * N_HEADS, :], axis=0, keepdims=True)


def _attn_sample(q, k_new, v_new, cache_kt, cache_vt, layer, bias, mult):
    nb = q.shape[0]
    w_buf = cache_kt.shape[-1]
    tok = pl.BlockSpec((None, SAMPLE_T, ATT_DIM), lambda b: (b, 0, 0))
    new = pl.BlockSpec((None, SUBLANES, ATT_DIM), lambda b: (b, 0, 0))
    old = pl.BlockSpec((None, None, N_HEADS, HEAD_DIM, w_buf), lambda b: (layer, b, 0, 0, 0))
    tbl = _full((SAMPLE_T * N_HEADS, w_buf + NEW_COLS))
    return pl.pallas_call(
        _attn_sample_kernel,
        grid=(nb,),
        in_specs=[tok, old, new, old, new, tbl, tbl],
        out_specs=tok,
        out_shape=jax.ShapeDtypeStruct((nb, SAMPLE_T, ATT_DIM), F32),
        compiler_params=_cparams(1),
        name="attn_sample",
    )(q, cache_kt, k_new, cache_vt, v_new, bias, mult)


def _split_dot(x, e_ref):
    hi = x.astype(BF16)
    lo = (x - hi.astype(F32)).astype(BF16)
    return (jnp.dot(hi, e_ref[...], preferred_element_type=F32)
            + jnp.dot(lo, e_ref[...], preferred_element_type=F32))


def _outproj_kernel(*refs, n_pat):
    mix = n_pat > 1
    n_lse = n_pat if mix else 0
    o_refs = refs[0:n_pat]
    l_refs = refs[n_pat:n_pat + n_lse]
    rest = refs[n_pat + n_lse:]
    if mix:
        yn_ref, h_ref, gatt_ref, exp_ref, unperm_ref, wa_ref, wc_ref, out_ref = rest
        lses = [r[...] for r in l_refs]
        top = functools.reduce(jnp.maximum, lses)
        ws = [jnp.exp(l - top) for l in lses]
        tot = functools.reduce(lambda a, b: a + b, ws)
        att = None
        for w, o_ref in zip(ws, o_refs):
            term = _split_dot(w / tot, exp_ref) * o_ref[...]
            att = term if att is None else att + term
        att_bf = jnp.dot(unperm_ref[...], _rms(att, gatt_ref[...]).astype(BF16),
                         preferred_element_type=F32).astype(BF16)
    else:
        yn_ref, h_ref, gatt_ref, wa_ref, wc_ref, out_ref = rest
        att_bf = _rms(o_refs[0][...], gatt_ref[...]).astype(BF16)
    y = (jnp.dot(att_bf, wa_ref[...], preferred_element_type=F32)
         + jnp.dot(yn_ref[...].astype(BF16), wc_ref[...], preferred_element_type=F32))
    out_ref[...] = h_ref[...] + y


def _outproj(os, lses, yn, h, g_att, wa_bf, wc_bf, expand=None, unperm=None, *, tm):
    n = h.shape[0]
    n_pat = len(os)
    tok = lambda w: pl.BlockSpec((tm, w), lambda i: (i, 0))
    in_specs = [tok(ATT_DIM)] * n_pat + [tok(LANES)] * len(lses) + [tok(CONV_DIM), tok(D_MODEL), _full((1, ATT_DIM))]
    args = [*os, *lses, yn, h, g_att]
    if n_pat > 1:
        in_specs += [_full((LANES, ATT_DIM)), _full((tm, tm))]
        args += [expand, unperm]
    in_specs += [_full((ATT_DIM, D_MODEL)), _full((CONV_DIM, D_MODEL))]
    args += [wa_bf, wc_bf]
    return pl.pallas_call(
        functools.partial(_outproj_kernel, n_pat=n_pat),
        grid=(n // tm,),
        in_specs=in_specs,
        out_specs=tok(D_MODEL),
        out_shape=jax.ShapeDtypeStruct((n, D_MODEL), F32),
        compiler_params=_cparams(1),
        name="outproj",
    )(*args)


ROUTE_I1, ROUTE_I2, ROUTE_R1, ROUTE_R2, ROUTE_W1, ROUTE_W2 = range(6)
GROUP_LANE0 = N_EXPERTS


def _route_kernel(h_ref, g_ref, wr_ref, route_ref, cnt_ref, carry_ref):
    i = pl.program_id(0)

    @pl.when(i == 0)
    def _():
        carry_ref[...] = jnp.zeros_like(carry_ref)

    m = _rms(h_ref[...], g_ref[...])
    logits = jnp.dot(m.astype(BF16), wr_ref[...], preferred_element_type=F32)
    tm = logits.shape[0]
    lane_i = lax.broadcasted_iota(jnp.int32, (tm, LANES), 1)
    lane = lane_i.astype(F32)
    big = jnp.float32(4 * LANES)

    is_g = jnp.logical_and(lane_i >= GROUP_LANE0, lane_i < GROUP_LANE0 + N_GROUPS)
    gl = jnp.where(is_g, logits, NEG)
    gmax = jnp.max(gl, axis=-1, keepdims=True)
    g_w = 1.0 / jnp.sum(jnp.where(is_g, jnp.exp(gl - gmax), 0.0), axis=-1, keepdims=True)
    g_sel = jnp.min(jnp.where(gl == gmax, lane - GROUP_LANE0, big), axis=-1, keepdims=True)

    grp_of_lane = (lane_i // EXPERTS_PER_GROUP).astype(F32)
    in_grp = jnp.logical_and(lane_i < N_EXPERTS, grp_of_lane == g_sel)
    el = jnp.where(in_grp, logits, NEG)
    t1 = jnp.max(el, axis=-1, keepdims=True)
    i1 = jnp.min(jnp.where(el == t1, lane, big), axis=-1, keepdims=True)
    el2 = jnp.where(lane == i1, NEG, el)
    t2 = jnp.max(el2, axis=-1, keepdims=True)
    i2 = jnp.min(jnp.where(el2 == t2, lane, big), axis=-1, keepdims=True)
    e2 = jnp.exp(t2 - t1)
    w1 = g_w / (1.0 + e2)
    w2 = g_w * e2 / (1.0 + e2)

    hit1 = lane == i1
    hit2 = lane == i2
    c = jnp.where(jnp.logical_or(hit1, hit2), 1.0, 0.0)
    rr = lax.broadcasted_iota(jnp.int32, (tm, tm), 0)
    cc = lax.broadcasted_iota(jnp.int32, (tm, tm), 1)
    lower = jnp.where(rr > cc, 1.0, 0.0).astype(BF16)
    before = jnp.dot(lower, c.astype(BF16), preferred_element_type=F32) + carry_ref[0:1, :]
    r1 = jnp.sum(jnp.where(hit1, before, 0.0), axis=-1, keepdims=True)
    r2 = jnp.sum(jnp.where(hit2, before, 0.0), axis=-1, keepdims=True)
    total = carry_ref[0:1, :] + jnp.sum(c, axis=0, keepdims=True)
    carry_ref[...] = jnp.broadcast_to(total, carry_ref.shape)
    cnt_ref[...] = jnp.broadcast_to(total, cnt_ref.shape)

    rec = jnp.zeros((tm, LANES), F32)
    for idx, val in ((ROUTE_I1, i1), (ROUTE_I2, i2), (ROUTE_R1, r1), (ROUTE_R2, r2), (ROUTE_W1, w1), (ROUTE_W2, w2)):
        rec = jnp.where(lane_i == idx, val, rec)
    route_ref[...] = rec


def _route(h, g_ffn, w_router, *, tm):
    n = h.shape[0]
    tok = lambda w: pl.BlockSpec((tm, w), lambda i: (i, 0))
    return pl.pallas_call(
        _route_kernel,
        grid=(n // tm,),
        in_specs=[tok(D_MODEL), _full((1, D_MODEL)), _full((D_MODEL, LANES))],
        out_specs=[tok(LANES), _full((SUBLANES, LANES))],
        out_shape=[jax.ShapeDtypeStruct((n, LANES), F32), jax.ShapeDtypeStruct((SUBLANES, LANES), F32)],
        scratch_shapes=[pltpu.VMEM((SUBLANES, LANES), F32)],
        compiler_params=_cparams(1),
        name="route",
    )(h, g_ffn, w_router)


def _dispatch_kernel(pos_ref, h_ref, g_ref, xs_ref, mbuf, sem):
    tm = mbuf.shape[0]
    mbuf[...] = _rms(h_ref[...], g_ref[...])

    def row_copy(t, p):
        return pltpu.make_async_copy(mbuf.at[pl.ds(t, 1), :], xs_ref.at[pl.ds(p, 1), :], sem)

    def issue(t, carry):
        row_copy(t, pos_ref[0, 0, 2 * t]).start()
        row_copy(t, pos_ref[0, 0, 2 * t + 1]).start()
        return carry

    lax.fori_loop(0, tm, issue, 0)

    def drain(t, carry):
        row_copy(0, 0).wait()
        row_copy(0, 0).wait()
        return carry

    lax.fori_loop(0, tm, drain, 0)


def _dispatch(h, g_ffn, pos, n_rows, *, tm):
    n = h.shape[0]
    return pl.pallas_call(
        _dispatch_kernel,
        grid=(n // tm,),
        in_specs=[pl.BlockSpec((1, 1, 2 * tm), lambda i: (i, 0, 0), memory_space=pltpu.SMEM),
                  pl.BlockSpec((tm, D_MODEL), lambda i: (i, 0)), _full((1, D_MODEL))],
        out_specs=pl.BlockSpec(memory_space=pl.ANY),
        out_shape=jax.ShapeDtypeStruct((n_rows, D_MODEL), F32),
        scratch_shapes=[pltpu.VMEM((tm, D_MODEL), F32), pltpu.SemaphoreType.DMA(())],
        compiler_params=pltpu.CompilerParams(dimension_semantics=("arbitrary",), vmem_limit_bytes=VMEM_LIMIT,
                                             has_side_effects=True),
        name="dispatch",
    )(pos, h, g_ffn)


def _expert_kernel(te_ref, tv_ref, x_ref, wg_ref, wu_ref, wd_ref, y_ref, wg_bf, wu_bf, wd_bf):
    i = pl.program_id(0)
    valid = tv_ref[i]
    changed = jnp.logical_or(i == 0, te_ref[i] != te_ref[jnp.maximum(i - 1, 0)])

    @pl.when(jnp.logical_and(changed, valid > 0))
    def _():
        wg_bf[...] = wg_ref[0].astype(BF16)
        wu_bf[...] = wu_ref[0].astype(BF16)
        wd_bf[...] = wd_ref[0].astype(BF16)

    @pl.when(valid > 0)
    def _():
        te = x_ref.shape[0]
        row = lax.broadcasted_iota(jnp.int32, (te, 1), 0)
        x = jnp.where(row < valid, x_ref[...], 0.0).astype(BF16)
        hg = jnp.dot(x, wg_bf[...], preferred_element_type=F32)
        hu = jnp.dot(x, wu_bf[...], preferred_element_type=F32)
        hid = (hg * jax.nn.sigmoid(hg)) * hu
        y_ref[...] = jnp.dot(hid.astype(BF16), wd_bf[...], preferred_element_type=F32)

    @pl.when(valid <= 0)
    def _():
        y_ref[...] = jnp.zeros_like(y_ref)


def _experts(xs, tile_expert, tile_valid, w_gate, w_up, w_down, *, te):
    n_rows = xs.shape[0]
    wspec = lambda a, b: pl.BlockSpec((1, a, b), lambda i, e, v: (e[i], 0, 0))
    grid_spec = pltpu.PrefetchScalarGridSpec(
        num_scalar_prefetch=2,
        grid=(n_rows // te,),
        in_specs=[pl.BlockSpec((te, D_MODEL), lambda i, e, v: (i, 0)),
                  wspec(D_MODEL, D_EXPERT), wspec(D_MODEL, D_EXPERT), wspec(D_EXPERT, D_MODEL)],
        out_specs=pl.BlockSpec((te, D_MODEL), lambda i, e, v: (i, 0)),
        scratch_shapes=[pltpu.VMEM((D_MODEL, D_EXPERT), BF16), pltpu.VMEM((D_MODEL, D_EXPERT), BF16),
                        pltpu.VMEM((D_EXPERT, D_MODEL), BF16)])
    return pl.pallas_call(
        _expert_kernel,
        grid_spec=grid_spec,
        out_shape=jax.ShapeDtypeStruct((n_rows, D_MODEL), F32),
        compiler_params=_cparams(1),
        name="experts",
    )(tile_expert, tile_valid, xs, w_gate, w_up, w_down)


def _combine_kernel(pos_ref, h_ref, route_ref, ys_ref, p_ref, gple_ref, wgate_ref, wproj_ref, out_ref,
                    y0, y1, sem):
    tm = y0.shape[0]

    def row_copy(p, dst, t):
        return pltpu.make_async_copy(ys_ref.at[pl.ds(p, 1), :], dst.at[pl.ds(t, 1), :], sem)

    def issue(t, carry):
        row_copy(pos_ref[0, 0, 2 * t], y0, t).start()
        row_copy(pos_ref[0, 0, 2 * t + 1], y1, t).start()
        return carry

    lax.fori_loop(0, tm, issue, 0)

    def drain(t, carry):
        row_copy(0, y0, 0).wait()
        row_copy(0, y1, 0).wait()
        return carry

    lax.fori_loop(0, tm, drain, 0)

    rec = route_ref[...]
    w1 = rec[:, ROUTE_W1:ROUTE_W1 + 1]
    w2 = rec[:, ROUTE_W2:ROUTE_W2 + 1]
    h2 = h_ref[...] + w1 * y0[...] + w2 * y1[...]
    gate = jax.nn.sigmoid(jnp.dot(_rms(h2, gple_ref[...]).astype(BF16), wgate_ref[...], preferred_element_type=F32))
    ple = jnp.dot(p_ref[...].astype(BF16), wproj_ref[...], preferred_element_type=F32)
    out_ref[...] = h2 + ple * gate


def _combine(h, route, pos, ys, p, g_ple, wgate_bf, wproj_bf, *, tm):
    n = h.shape[0]
    tok = lambda w: pl.BlockSpec((tm, w), lambda i: (i, 0))
    return pl.pallas_call(
        _combine_kernel,
        grid=(n // tm,),
        in_specs=[pl.BlockSpec((1, 1, 2 * tm), lambda i: (i, 0, 0), memory_space=pltpu.SMEM),
                  tok(D_MODEL), tok(LANES), pl.BlockSpec(memory_space=pl.ANY), tok(D_PLE),
                  _full((1, D_MODEL)), _full((D_MODEL, D_MODEL)), _full((D_PLE, D_MODEL))],
        out_specs=tok(D_MODEL),
        out_shape=jax.ShapeDtypeStruct((n, D_MODEL), F32),
        scratch_shapes=[pltpu.VMEM((tm, D_MODEL), F32), pltpu.VMEM((tm, D_MODEL), F32),
                        pltpu.SemaphoreType.DMA(())],
        compiler_params=_cparams(1),
        name="combine",
    )(pos, h, route, ys, p, g_ple, wgate_bf, wproj_bf)


def _lookup(table, idx, size):
    hit = idx[..., None] == jnp.arange(size, dtype=jnp.int32)
    return jnp.sum(jnp.where(hit, table, 0), axis=-1)


def _moe_ple(h, p, g_ffn, w_router, w_gate, w_up, w_down, g_ple, wgate_bf, wproj_bf, *, tm, te):
    n = h.shape[0]
    route, counts = _route(h, g_ffn, w_router, tm=tm)
    counts = counts[0, :N_EXPERTS].astype(jnp.int32)
    padded = ((counts + te - 1) // te) * te
    ends = jnp.cumsum(padded)
    offs = ends - padded
    ids = route[:, ROUTE_I1:ROUTE_I2 + 1].astype(jnp.int32)
    ranks = route[:, ROUTE_R1:ROUTE_R2 + 1].astype(jnp.int32)
    pos = (_lookup(offs, ids, N_EXPERTS) + ranks).reshape(n // tm, 1, 2 * tm)
    n_tiles = (2 * n) // te + N_EXPERTS
    starts = jnp.arange(n_tiles, dtype=jnp.int32) * te
    tile_expert = jnp.minimum(jnp.sum((starts[:, None] >= ends[None, :]).astype(jnp.int32), axis=-1), N_EXPERTS - 1)
    tile_valid = jnp.clip(_lookup(counts, tile_expert, N_EXPERTS)
                          - (starts - _lookup(offs, tile_expert, N_EXPERTS)), 0, te).astype(jnp.int32)
    xs = _dispatch(h, g_ffn, pos, n_tiles * te, tm=tm)
    ys = _experts(xs, tile_expert, tile_valid, w_gate, w_up, w_down, te=te)
    return _combine(h, route, pos, ys, p, g_ple, wgate_bf, wproj_bf, tm=tm)


REC_E1, REC_E2, REC_S1, REC_S2, REC_W1, REC_W2 = range(6)
PACKED = D_MODEL // 2
ROW_W = PACKED + LANES
LIST_LANES = LANES
LIST_COUNT = LIST_LANES - 1
LIST_NEXT = LIST_LANES - 2
LIST_PARITY = LIST_LANES - 3


def _slots(tm):
    need = 2 * tm + (SUBLANES - 1) * N_EXPERTS + SUBLANES
    return -(-need // LANES) * LANES


def _pieces(x):
    hi = x.astype(BF16)
    r1 = x - hi.astype(F32)
    mid = r1.astype(BF16)
    return hi, mid, (r1 - mid.astype(F32)).astype(BF16)


def _sort_kernel(h_ref, g_ref, wr_ref, before_ref, sel_ref, xs_ref, rec_ref, cnt_ref, *, group):
    tm = h_ref.shape[0] // group
    slots = xs_ref.shape[0] // group
    for t in range(group):
        xs, rec, cnt = _sort_tile(h_ref[t * tm:(t + 1) * tm, :], g_ref, wr_ref, before_ref, sel_ref, slots)
        xs_ref[t * slots:(t + 1) * slots, :] = xs
        rec_ref[t * tm:(t + 1) * tm, :] = rec
        cnt_ref[t] = cnt


def _sort_tile(h, g_ref, wr_ref, before_ref, sel_ref, slots):
    m = _rms(h, g_ref[...])
    logits = jnp.dot(m.astype(BF16), wr_ref[...], preferred_element_type=F32)
    tm = logits.shape[0]
    lane_i = lax.broadcasted_iota(jnp.int32, (tm, LANES), 1)
    lane = lane_i.astype(F32)
    big = jnp.float32(4 * LANES)

    is_g = jnp.logical_and(lane_i >= GROUP_LANE0, lane_i < GROUP_LANE0 + N_GROUPS)
    gl = jnp.where(is_g, logits, NEG)
    gmax = jnp.max(gl, axis=-1, keepdims=True)
    g_w = 1.0 / jnp.sum(jnp.where(is_g, jnp.exp(gl - gmax), 0.0), axis=-1, keepdims=True)
    g_sel = jnp.min(jnp.where(gl == gmax, lane - GROUP_LANE0, big), axis=-1, keepdims=True)

    grp_of_lane = (lane_i // EXPERTS_PER_GROUP).astype(F32)
    in_grp = jnp.logical_and(lane_i < N_EXPERTS, grp_of_lane == g_sel)
    el = jnp.where(in_grp, logits, NEG)
    t1 = jnp.max(el, axis=-1, keepdims=True)
    e1 = jnp.min(jnp.where(el == t1, lane, big), axis=-1, keepdims=True)
    el2 = jnp.where(lane == e1, NEG, el)
    t2 = jnp.max(el2, axis=-1, keepdims=True)
    e2 = jnp.min(jnp.where(el2 == t2, lane, big), axis=-1, keepdims=True)
    ex = jnp.exp(t2 - t1)
    w1 = g_w / (1.0 + ex)
    w2 = g_w * ex / (1.0 + ex)

    hit1 = lane == e1
    hit2 = lane == e2
    c = jnp.where(hit1, 1.0, jnp.where(hit2, 1.0, 0.0))
    rr = lax.broadcasted_iota(jnp.int32, (tm, tm), 0)
    cc = lax.broadcasted_iota(jnp.int32, (tm, tm), 1)
    lower = jnp.where(rr > cc, 1.0, 0.0).astype(BF16)
    rank = jnp.dot(lower, c.astype(BF16), preferred_element_type=F32)
    cnt = jnp.sum(c, axis=0, keepdims=True)
    chunks = jnp.floor((cnt + (SUBLANES - 1)) * (1.0 / SUBLANES))
    start = SUBLANES * jnp.dot(jnp.broadcast_to(chunks, (SUBLANES, LANES)).astype(BF16), before_ref[...],
                               preferred_element_type=F32)[0:1, :]
    slot_of = rank + start
    s1 = jnp.sum(jnp.where(hit1, slot_of, 0.0), axis=-1, keepdims=True)
    s2 = jnp.sum(jnp.where(hit2, slot_of, 0.0), axis=-1, keepdims=True)

    rec = jnp.zeros((tm, LANES), F32)
    for idx, val in ((REC_E1, e1), (REC_E2, e2), (REC_S1, s1), (REC_S2, s2), (REC_W1, w1), (REC_W2, w2)):
        rec = jnp.where(lane_i == idx, val, rec)

    rec_parts = _pieces(rec)
    srow = sum(lax.dot_general(sel_ref[...], part, NT, preferred_element_type=F32) for part in rec_parts)
    slot_id = lax.broadcasted_iota(jnp.int32, (slots, tm), 0).astype(F32)
    place = jnp.where(slot_id == srow[0:1, :], 1.0, jnp.where(slot_id == srow[1:2, :], 1.0, 0.0)).astype(BF16)
    payload = jnp.concatenate([m.astype(BF16), *rec_parts], axis=-1)
    moved = jnp.dot(place, payload, preferred_element_type=F32)
    info = (moved[:, D_MODEL:D_MODEL + LANES] + moved[:, D_MODEL + LANES:D_MODEL + 2 * LANES]
            + moved[:, D_MODEL + 2 * LANES:])
    xs = jnp.concatenate([_pack_bf16(moved[:, :D_MODEL], is_bf16=True), pltpu.bitcast(info, jnp.uint32)], axis=-1)
    return xs, rec, jnp.broadcast_to(cnt, (SUBLANES, LANES))


def _sort(h, g_ffn, w_router, before, sel, *, tm):
    n = h.shape[0]
    slots = _slots(tm)
    group = 2 if (n // tm) % 2 == 0 else 1
    tok = lambda w: pl.BlockSpec((group * tm, w), lambda i: (i, 0))
    return pl.pallas_call(
        functools.partial(_sort_kernel, group=group),
        grid=(n // (group * tm),),
        in_specs=[tok(D_MODEL), _full((1, D_MODEL)), _full((D_MODEL, LANES)), _full((LANES, LANES)),
                  _full((SUBLANES, LANES))],
        out_specs=[pl.BlockSpec((group * slots, ROW_W), lambda i: (i, 0)), tok(LANES),
                   pl.BlockSpec((group, SUBLANES, LANES), lambda i: (i, 0, 0))],
        out_shape=[jax.ShapeDtypeStruct((n // tm * slots, ROW_W), jnp.uint32), jax.ShapeDtypeStruct((n, LANES), F32),
                   jax.ShapeDtypeStruct((n // tm, SUBLANES, LANES), F32)],
        compiler_params=_cparams(1),
        name="moe_sort",
    )(h, g_ffn, w_router, before, sel)


def _chunk_copy(src_hbm, row, dst, c, sem):
    if not isinstance(row, int):
        row = pl.multiple_of(row, SUBLANES)
    to = c * SUBLANES
    if not isinstance(to, int):
        to = pl.multiple_of(to, SUBLANES)
    return pltpu.make_async_copy(src_hbm.at[pl.ds(row, SUBLANES), :], dst.at[pl.ds(to, SUBLANES), :], sem)


def _expert_kernel2(te_ref, tv_ref, lst_ref, xa_ref, xb_ref, wg_ref, wu_ref, wd_ref, y_ref,
                    xbuf, sem, wg_st, wu_st, wd_st, wsem, wg_bf, wu_bf, wd_bf, *, rows_a, layer):
    i = pl.program_id(0)
    n = pl.num_programs(0)
    te = xbuf.shape[1]
    slot = i % 2

    def fetch(tile, to_slot):
        n_a = lst_ref[tile, LIST_COUNT]

        def from_a(c, carry):
            _chunk_copy(xa_ref, lst_ref[tile, c], xbuf.at[to_slot], c, sem.at[to_slot]).start()
            return carry

        def from_b(c, carry):
            _chunk_copy(xb_ref, lst_ref[tile, c] - rows_a, xbuf.at[to_slot], c, sem.at[to_slot]).start()
            return carry

        per_tile = te // SUBLANES

        @pl.when(n_a == per_tile)
        def _():
            for c in range(per_tile):
                from_a(c, 0)

        @pl.when(n_a != per_tile)
        def _():
            lax.fori_loop(0, n_a, from_a, 0)
            lax.fori_loop(n_a, per_tile, from_b, 0)

    @pl.when(jnp.logical_and(i == 0, tv_ref[0] > 0))
    def _():
        fetch(0, 0)

    nxt = jnp.minimum(i + 1, n - 1)

    @pl.when(jnp.logical_and(i + 1 < n, tv_ref[nxt] > 0))
    def _():
        fetch(nxt, 1 - slot)

    valid = tv_ref[i]
    changed = jnp.logical_or(i == 0, te_ref[i] != te_ref[jnp.maximum(i - 1, 0)])

    def weight_copies(expert, s):
        return [pltpu.make_async_copy(src.at[layer, expert], dst.at[s], wsem.at[s])
                for src, dst in ((wg_ref, wg_st), (wu_ref, wu_st), (wd_ref, wd_st))]

    @pl.when(jnp.logical_and(changed, valid > 0))
    def _():
        s = lst_ref[i, LIST_PARITY]

        @pl.when(i == 0)
        def _():
            for cp in weight_copies(te_ref[0], 0):
                cp.start()

        for cp in weight_copies(te_ref[i], s):
            cp.wait()
        wg_bf[...] = wg_st[s].astype(BF16)
        wu_bf[...] = wu_st[s].astype(BF16)
        wd_bf[...] = wd_st[s].astype(BF16)
        nxt_expert = lst_ref[i, LIST_NEXT]

        @pl.when(nxt_expert >= 0)
        def _():
            for cp in weight_copies(nxt_expert, 1 - s):
                cp.start()

    @pl.when(valid > 0)
    def _():
        pltpu.make_async_copy(xbuf.at[slot], xbuf.at[slot], sem.at[slot]).wait()
        rows = xbuf[slot]
        x = _unpack_bf16(rows[:, :PACKED])
        info = pltpu.bitcast(rows[:, PACKED:], F32)
        mine = info[:, REC_E1:REC_E1 + 1] == te_ref[i].astype(F32)
        gate = jnp.where(mine, info[:, REC_W1:REC_W1 + 1], info[:, REC_W2:REC_W2 + 1])
        hg = jnp.dot(x, wg_bf[...], preferred_element_type=F32)
        hu = jnp.dot(x, wu_bf[...], preferred_element_type=F32)
        hid = (hg * jax.nn.sigmoid(hg)) * hu * gate
        y = jnp.dot(hid.astype(BF16), wd_bf[...], preferred_element_type=F32)
        y_ref[...] = _pack_bf16(y)

    @pl.when(valid <= 0)
    def _():
        y_ref[...] = jnp.zeros_like(y_ref)


def _pack_bf16(x, is_bf16=False):
    w = x.shape[-1] // 2
    bits = lambda a: pltpu.bitcast(a if is_bf16 else a.astype(BF16).astype(F32), jnp.uint32)
    return jnp.bitwise_or(lax.shift_right_logical(bits(x[:, :w]), jnp.uint32(16)),
                          jnp.bitwise_and(bits(x[:, w:]), jnp.uint32(0xFFFF0000)))


def _unpack_bf16(words):
    as_f32 = lambda a: pltpu.bitcast(a, F32)
    return jnp.concatenate([as_f32(lax.shift_left(words, jnp.uint32(16))),
                            as_f32(jnp.bitwise_and(words, jnp.uint32(0xFFFF0000)))], axis=-1).astype(BF16)


def _experts2(xs_a, xs_b, tile_expert, tile_valid, src, w_gate, w_up, w_down, *, te, layer):
    n_tiles = tile_expert.shape[0]
    hbm = pl.BlockSpec(memory_space=pl.ANY)
    grid_spec = pltpu.PrefetchScalarGridSpec(
        num_scalar_prefetch=3,
        grid=(n_tiles,),
        in_specs=[hbm, hbm, hbm, hbm, hbm],
        out_specs=pl.BlockSpec((te, PACKED), lambda i, e, v, s: (i, 0)),
        scratch_shapes=[pltpu.VMEM((2, te, ROW_W), jnp.uint32), pltpu.SemaphoreType.DMA((2,)),
                        pltpu.VMEM((2, D_MODEL, D_EXPERT), F32), pltpu.VMEM((2, D_MODEL, D_EXPERT), F32),
                        pltpu.VMEM((2, D_EXPERT, D_MODEL), F32), pltpu.SemaphoreType.DMA((2,)),
                        pltpu.VMEM((D_MODEL, D_EXPERT), BF16), pltpu.VMEM((D_MODEL, D_EXPERT), BF16),
                        pltpu.VMEM((D_EXPERT, D_MODEL), BF16)])
    return pl.pallas_call(
        functools.partial(_expert_kernel2, rows_a=xs_a.shape[0], layer=layer),
        grid_spec=grid_spec,
        out_shape=jax.ShapeDtypeStruct((n_tiles * te, PACKED), jnp.uint32),
        compiler_params=_cparams(1),
        name="experts",
    )(tile_expert, tile_valid, src.reshape(n_tiles, LIST_LANES), xs_a, xs_b, w_gate, w_up, w_down)


def _combine_kernel2(lst_ref, h_ref, rec_ref, ys_ref, p_ref, gple_ref, wgate_ref, wproj_ref, out_ref,
                     ybuf, sem):
    i = pl.program_id(0)
    n = pl.num_programs(0)
    slot = i % 2
    tm = h_ref.shape[0]
    slots = ybuf.shape[1]

    def fetch(tile, to_slot):
        for c in range(slots // SUBLANES):
            _chunk_copy(ys_ref, lst_ref[tile, c], ybuf.at[to_slot], c, sem.at[to_slot]).start()

    @pl.when(i == 0)
    def _():
        fetch(0, 0)

    @pl.when(i + 1 < n)
    def _():
        fetch(i + 1, 1 - slot)

    pltpu.make_async_copy(ybuf.at[slot], ybuf.at[slot], sem.at[slot]).wait()

    rec = rec_ref[...]
    slot_id = lax.broadcasted_iota(jnp.int32, (tm, slots), 1).astype(F32)
    back = jnp.where(slot_id == rec[:, REC_S1:REC_S1 + 1], 1.0,
                     jnp.where(slot_id == rec[:, REC_S2:REC_S2 + 1], 1.0, 0.0)).astype(BF16)
    h2 = h_ref[...] + jnp.dot(back, _unpack_bf16(ybuf[slot]), preferred_element_type=F32)
    gate = jax.nn.sigmoid(jnp.dot(_rms(h2, gple_ref[...]).astype(BF16), wgate_ref[...], preferred_element_type=F32))
    ple = jnp.dot(p_ref[...].astype(BF16), wproj_ref[...], preferred_element_type=F32)
    out_ref[...] = h2 + ple * gate


def _combine2(h, rec, dst, ys, p, g_ple, wgate_bf, wproj_bf, *, tm, layer):
    n = h.shape[0]
    n_tok = n // tm
    tok = lambda w: pl.BlockSpec((tm, w), lambda i, s: (i, 0))
    full = lambda shape: pl.BlockSpec(shape, lambda i, s: (0,) * len(shape))
    grid_spec = pltpu.PrefetchScalarGridSpec(
        num_scalar_prefetch=1,
        grid=(n_tok,),
        in_specs=[tok(D_MODEL), tok(LANES), pl.BlockSpec(memory_space=pl.ANY),
                  pl.BlockSpec((None, tm, D_PLE), lambda i, s: (layer, i, 0)),
                  full((1, D_MODEL)), full((D_MODEL, D_MODEL)), full((D_PLE, D_MODEL))],
        out_specs=tok(D_MODEL),
        scratch_shapes=[pltpu.VMEM((2, _slots(tm), PACKED), jnp.uint32), pltpu.SemaphoreType.DMA((2,))])
    return pl.pallas_call(
        _combine_kernel2,
        grid_spec=grid_spec,
        out_shape=jax.ShapeDtypeStruct((n, D_MODEL), F32),
        compiler_params=_cparams(1),
        name="combine",
    )(dst.reshape(n_tok, LIST_LANES), h, rec, ys, p, g_ple, wgate_bf, wproj_bf)


def _excl_cumsum(x, axis):
    return jnp.cumsum(x, axis=axis) - x


def _chunk_plan(cnt, tile_row0, *, n_pairs, rows_a, zero_row, te):
    n_tok = cnt.shape[0]
    per_tile = te // SUBLANES
    chunks = (cnt + SUBLANES - 1) // SUBLANES
    run0 = _excl_cumsum(chunks, 1)
    n_run = jnp.sum(chunks, axis=1)
    seg0 = _excl_cumsum(chunks, 0)
    total = jnp.sum(chunks, axis=0)
    region = ((total + per_tile - 1) // per_tile) * per_tile
    reg_end = jnp.cumsum(region)
    reg0 = reg_end - region

    n_tiles = -(-(n_pairs + (SUBLANES - 1) * N_EXPERTS * n_tok) // te) + N_EXPERTS
    t0 = jnp.arange(n_tiles, dtype=jnp.int32) * per_tile
    tile_expert = jnp.minimum(jnp.sum((t0[:, None] >= reg_end[None, :]).astype(jnp.int32), axis=1), N_EXPERTS - 1)
    pick = tile_expert[:, None] == jnp.arange(N_EXPERTS, dtype=jnp.int32)[None, :]
    of_tile = lambda v: jnp.sum(jnp.where(pick, v[None, :], 0), axis=1)
    tile_valid = jnp.clip(of_tile(total) - (t0 - of_tile(reg0)), 0, per_tile).astype(jnp.int32)

    q = (t0 - of_tile(reg0))[:, None] + jnp.arange(per_tile, dtype=jnp.int32)[None, :]
    col_of_tile = lambda v: jnp.sum(jnp.where(pick[:, None, :], v[None, :, :], 0), axis=2)
    seg0_t, seg1_t, run0_t = col_of_tile(seg0), col_of_tile(seg0 + chunks), col_of_tile(run0)
    holds = (q[:, :, None] >= seg0_t[:, None, :]) & (q[:, :, None] < seg1_t[:, None, :])
    local = run0_t[:, None, :] + q[:, :, None] - seg0_t[:, None, :]
    row = jnp.asarray(tile_row0, jnp.int32)[None, None, :] + SUBLANES * local
    src = jnp.sum(jnp.where(holds, row, 0), axis=2)
    src = jnp.where(jnp.any(holds, axis=2), src, zero_row)
    n_first = jnp.sum((src < rows_a).astype(jnp.int32), axis=1)
    idx = jnp.arange(N_EXPERTS, dtype=jnp.int32)
    nonempty = total > 0
    ordinal = _excl_cumsum(nonempty.astype(jnp.int32), 0)
    later = jnp.where((idx[None, :] > idx[:, None]) & nonempty[None, :], idx[None, :], N_EXPERTS)
    following = jnp.min(later, axis=1)
    following = jnp.where(following == N_EXPERTS, -1, following)
    lane = jnp.arange(LIST_LANES, dtype=jnp.int32)[None, :]
    src = jnp.pad(src, ((0, 0), (0, LIST_LANES - per_tile)))
    for at, val in ((LIST_COUNT, n_first), (LIST_NEXT, of_tile(following)), (LIST_PARITY, of_tile(ordinal) % 2)):
        src = jnp.where(lane == at, val[:, None], src)
    src = src.reshape(n_tiles, 1, LIST_LANES).astype(jnp.int32)

    j = jnp.arange(LIST_LANES, dtype=jnp.int32)[None, :, None]
    inside = (j >= run0[:, None, :]) & (j < (run0 + chunks)[:, None, :])
    base = (reg0[None, :] + seg0 - run0)[:, None, :]
    dst = SUBLANES * jnp.sum(jnp.where(inside, base + j, 0), axis=2)
    dst = dst.reshape(n_tok, 1, LIST_LANES).astype(jnp.int32)
    return tile_expert.astype(jnp.int32), tile_valid, src, dst


def _moe_ple2(h_a, h_b, p_a, p_b, g_ffn, w_router, before, sel, w_gate, w_up, w_down, g_ple, wgate_bf, wproj_bf,
              *, tm_a, tm_b, te, layer):
    xs_a, rec_a, cnt_a = _sort(h_a, g_ffn, w_router, before, sel, tm=tm_a)
    xs_b, rec_b, cnt_b = _sort(h_b, g_ffn, w_router, before, sel, tm=tm_b)
    t_a, t_b = cnt_a.shape[0], cnt_b.shape[0]
    cnt = jnp.concatenate([cnt_a[:, 0, :N_EXPERTS], cnt_b[:, 0, :N_EXPERTS]], axis=0).astype(jnp.int32)
    rows_a = xs_a.shape[0]
    row0 = np.concatenate([np.arange(t_a) * _slots(tm_a), rows_a + np.arange(t_b) * _slots(tm_b)])
    tile_expert, tile_valid, src, dst = _chunk_plan(
        cnt, row0, n_pairs=2 * (h_a.shape[0] + h_b.shape[0]), rows_a=rows_a,
        zero_row=rows_a + _slots(tm_b) - SUBLANES, te=te)
    ys = _experts2(xs_a, xs_b, tile_expert, tile_valid, src, w_gate, w_up, w_down, te=te, layer=layer)
    out_a = _combine2(h_a, rec_a, dst[:t_a], ys, p_a, g_ple, wgate_bf, wproj_bf, tm=tm_a, layer=layer)
    out_b = _combine2(h_b, rec_b, dst[t_a:], ys, p_b, g_ple, wgate_bf, wproj_bf, tm=tm_b, layer=layer)
    return out_a, out_b


def _bucket_np(dist):
    max_exact = N_BUCKETS // 2
    d_f = np.maximum(dist, 1).astype(np.float32)
    large = max_exact + (np.log(d_f / np.float32(max_exact)) / np.float32(np.log(MAX_DISTANCE / max_exact))
                         * np.float32(N_BUCKETS - max_exact)).astype(np.int32)
    large = np.minimum(large, N_BUCKETS - 1)
    return np.where(dist < max_exact, dist, large).astype(np.int32)


def _bias_from_buckets(rel_bias, bucket, valid):
    onehot = (jnp.asarray(bucket)[..., None] == jnp.arange(N_BUCKETS, dtype=jnp.int32)).astype(F32)
    bias = jnp.einsum("...k,kh->h...", onehot, rel_bias.astype(F32), precision=HIGHEST)
    return jnp.where(jnp.asarray(valid)[None], bias, NEG)


def _block_order(dil):
    g = np.arange(Q_BLOCK) // SUBLANES
    j = np.arange(Q_BLOCK) % SUBLANES
    if dil == 1:
        return 16 * j + g
    if dil == 4:
        return 32 * (g // 4) + 4 * j + g % 4
    return SUBLANES * g + j


def _band_bias(rel_bias, dil):
    mu = _block_order(dil)
    qi = mu[:, None] + Q_BLOCK
    ki = np.concatenate([mu, mu + Q_BLOCK])[None, :]
    off = qi - ki
    valid = (off >= 0) & (off <= N_KEYS)
    bucket = _bucket_np(dil * np.clip(off, 0, N_KEYS))
    first = valid & (np.arange(2 * Q_BLOCK)[None, :] >= Q_BLOCK)
    tables = [_bias_from_buckets(rel_bias, bucket, v).reshape(N_HEADS * Q_BLOCK, 2 * Q_BLOCK) for v in (valid, first)]
    return jnp.stack(tables)


def _sample_tables(rel_bias, w_buf):
    qpos = w_buf + np.arange(SAMPLE_T)[:, None]
    pos = np.arange(w_buf + NEW_COLS)[None, :]
    dist = qpos - pos
    in_seq = (dist >= 0) & (pos < w_buf + SAMPLE_T)
    mult = np.zeros(dist.shape, np.float32)
    for (w, d) in PATTERNS:
        mult += in_seq & (dist % d == 0) & (dist <= w)
    bucket = _bucket_np(np.maximum(dist, 0))
    bias = jnp.transpose(_bias_from_buckets(rel_bias, bucket, mult > 0), (1, 0, 2))
    rows = SAMPLE_T * N_HEADS
    mult_rows = np.broadcast_to(mult[:, None, :], (SAMPLE_T, N_HEADS, mult.shape[-1]))
    return bias.reshape(rows, -1), jnp.asarray(mult_rows.reshape(rows, -1))


PROJ_TM = 256
OUT_TM = 512
PROMPT_TM = 256
PROMPT_TE = 256
ATTN_SUB = 4


def _row_perm(tm):
    a = np.arange(tm)
    src = (a // Q_BLOCK) * Q_BLOCK + 16 * (a % SUBLANES) + (a % Q_BLOCK) // SUBLANES
    perm = np.zeros((tm, tm), np.float32)
    perm[a, src] = 1.0
    return perm


def kernel(x_prompt, x_sample, cache_k, cache_v, state_conv, p_prompt, p_sample, rel_bias, g_mix, w_in, q_gain,
           k_gain, conv_w, g_out_att, g_out_conv, w_out, g_ffn, w_router_group, w_router_expert, w_gate, w_up,
           w_down, g_ple, w_ple_gate, w_ple_proj):
    depth = w_in.shape[0]
    batch, seq, _ = x_prompt.shape
    dec_b, dec_t, _ = x_sample.shape
    w_buf = cache_k.shape[2]
    n_s = dec_b * dec_t
    keep = min(w_buf, seq)
    assert dec_t == SAMPLE_T and w_buf % LANES == 0
    assert seq % (Q_BLOCK * 16 * 2) == 0 and keep % PROJ_TM == 0 and seq % PROMPT_TM == 0
    cache_kt = jnp.transpose(cache_k, (0, 1, 3, 4, 2))
    cache_vt = jnp.transpose(cache_v, (0, 1, 3, 4, 2))

    row = lambda a: a.reshape(1, -1)
    src_lane = np.arange(LANES)
    expand = jnp.asarray((src_lane[:, None] // LSE_LANES_PER_HEAD == np.arange(ATT_DIM)[None, :] // HEAD_DIM)
                         & (src_lane[:, None] % LSE_LANES_PER_HEAD == 0), BF16)
    before = jnp.asarray(np.arange(LANES)[:, None] < np.arange(LANES)[None, :], BF16)
    sel_np = np.zeros((SUBLANES, LANES), np.float32)
    sel_np[0, REC_S1] = sel_np[1, REC_S2] = 1.0
    sel = jnp.asarray(sel_np, BF16)
    perm = jnp.asarray(_row_perm(PROJ_TM), BF16)
    unperm = jnp.asarray(_row_perm(OUT_TM).T, BF16)
    band = [_band_bias(rel_bias, d) for (_, d) in PATTERNS]
    s_bias, s_mult = _sample_tables(rel_bias, w_buf)

    hp = x_prompt.reshape(batch * seq, D_MODEL)
    hs = jnp.swapaxes(x_sample, 0, 1).reshape(n_s, D_MODEL)
    pp_all = p_prompt.reshape(depth, batch * seq, D_PLE)
    ps_all = jnp.swapaxes(p_sample, 1, 2).reshape(depth, n_s, D_PLE)
    new = {k: [] for k in ("kp", "vp", "cp", "ks", "vs", "cs")}
    hist_p = jnp.zeros((batch, SUBLANES, CONV_DIM), F32)

    for l in range(depth):
        w_in_bf = w_in[l].astype(BF16)
        wa_bf = w_out[l, :ATT_DIM].astype(BF16)
        wc_bf = w_out[l, ATT_DIM:].astype(BF16)
        wgate_bf = w_ple_gate[l].astype(BF16)
        wproj_bf = w_ple_proj[l].astype(BF16)
        w_router = jnp.concatenate(
            [w_router_expert[l], w_router_group[l],
             jnp.zeros((D_MODEL, LANES - N_EXPERTS - N_GROUPS), F32)], axis=1).astype(BF16)
        qg, kg = row(jnp.tile(q_gain[l], N_HEADS)), row(jnp.tile(k_gain[l], N_HEADS))
        mix = (row(g_mix[l]), w_in_bf, qg, kg, conv_w[l], row(g_out_conv[l]))
        moe = (row(g_ffn[l]), w_router, before, sel, w_gate, w_up, w_down, row(g_ple[l]), wgate_bf, wproj_bf)

        q, k, v, k_nat, v_nat, yn, nconv = _inproj(
            hp, hist_p, *mix, perm, tm=PROJ_TM, shift=1, tiles_per_seq=seq // PROJ_TM,
            keep_tiles=keep // PROJ_TM)
        os, lses = [], []
        for bias, (_, d) in zip(band, PATTERNS):
            sub = min(ATTN_SUB, seq // (Q_BLOCK * d))
            o, lse = _attn_pattern(q, k, v, bias, batch=batch, seq=seq, dil=d, sub=sub,
                                   res=min(d, ATTN_SUB // sub))
            os.append(o)
            lses.append(lse)
        hp = _outproj(os, lses, yn, hp, row(g_out_att[l]), wa_bf, wc_bf, expand, unperm, tm=OUT_TM)
        new["kp"].append(jnp.transpose(k_nat, (0, 3, 1, 2)))
        new["vp"].append(jnp.transpose(v_nat, (0, 3, 1, 2)))
        new["cp"].append(nconv[:, SUBLANES - 2:])

        hist_s = jnp.swapaxes(state_conv[l], 0, 1).reshape(1, 2 * dec_b, CONV_DIM)
        q, k, v, yn, nconv = _inproj(hs, hist_s, *mix, tm=n_s, shift=dec_b, tiles_per_seq=1)
        bmaj = lambda a: jnp.swapaxes(a.reshape(dec_t, dec_b, N_HEADS, HEAD_DIM), 0, 1)
        qb, kb, vb = bmaj(q), bmaj(k), bmaj(v)
        rows8 = lambda a: jnp.pad(a.reshape(dec_b, dec_t, ATT_DIM), ((0, 0), (0, SUBLANES - dec_t), (0, 0)))
        att = _attn_sample(qb.reshape(dec_b, dec_t, ATT_DIM), rows8(kb), rows8(vb), cache_kt, cache_vt, l,
                           s_bias, s_mult)
        att_tm = jnp.swapaxes(att, 0, 1).reshape(n_s, ATT_DIM)
        hs = _outproj([att_tm], [], yn, hs, row(g_out_att[l]), wa_bf, wc_bf, tm=n_s)
        hp, hs = _moe_ple2(hp, hs, pp_all, ps_all, *moe, tm_a=PROMPT_TM, tm_b=n_s, te=PROMPT_TE, layer=l)
        new["ks"].append(kb)
        new["vs"].append(vb)
        new["cs"].append(jnp.swapaxes(nconv.reshape(2, dec_b, CONV_DIM), 0, 1))

    y_prompt = hp.reshape(batch, seq, D_MODEL)
    y_sample = jnp.swapaxes(hs.reshape(dec_t, dec_b, D_MODEL), 0, 1)
    st = lambda key: jnp.stack(new[key])
    return (y_prompt, y_sample, st("kp"), st("vp"), st("cp"), st("ks"), st("vs"), st("cs"))
```

```python
import functools

import jax
import jax.numpy as jnp
import numpy as np
from jax import lax
from jax.experimental import pallas as pl
from jax.experimental.pallas import tpu as pltpu

F32 = jnp.float32
BF16 = jnp.bfloat16
HIGHEST = lax.Precision.HIGHEST

D_MODEL = 1024
HEAD_DIM = 64
N_HEADS = 8
ATT_DIM = N_HEADS * HEAD_DIM
CONV_DIM = D_MODEL - ATT_DIM
MIX_IN = 3 * ATT_DIM + 3 * CONV_DIM
PATTERNS = ((128, 1), (512, 4), (2048, 16))
N_KEYS = 128
Q_BLOCK = 128
N_BUCKETS = 32
MAX_DISTANCE = 2048
N_GROUPS = 4
EXPERTS_PER_GROUP = 8
N_EXPERTS = N_GROUPS * EXPERTS_PER_GROUP
D_EXPERT = 256
D_PLE = 256
EPS = 1e-6
NEG = -1e30

LANES = 128
SUBLANES = 8
SLABS = Q_BLOCK // SUBLANES
LSE_LANES_PER_HEAD = LANES // N_HEADS
VMEM_LIMIT = 56 * 1024 * 1024
NT = (((1,), (1,)), ((), ()))


def _cparams(n_axes):
    return pltpu.CompilerParams(dimension_semantics=("arbitrary",) * n_axes,
                                vmem_limit_bytes=VMEM_LIMIT)


def _full(shape):
    n = len(shape)
    return pl.BlockSpec(shape, lambda *_: (0,) * n)


def _rms(x, gain):
    ms = jnp.mean(x * x, axis=-1, keepdims=True)
    return x * lax.rsqrt(ms + EPS) * gain


def _exact_dot(x, e_ref):
    hi = x.astype(BF16)
    r1 = x - hi.astype(F32)
    mid = r1.astype(BF16)
    lo = (r1 - mid.astype(F32)).astype(BF16)
    e = e_ref[...]
    return (jnp.dot(hi, e, preferred_element_type=F32) + jnp.dot(mid, e, preferred_element_type=F32)
            + jnp.dot(lo, e, preferred_element_type=F32))


def _inproj_kernel(*refs, shift, tiles_per_seq, permute):
    (h_ref, gmix_ref, w_ref, qg_ref, kg_ref, cw_ref, gconv_ref, hist_ref) = refs[:8]
    if permute:
        perm_ref, q_ref, k_ref, v_ref, kn_ref, vn_ref, yn_ref, nconv_ref, carry_ref = refs[8:]
    else:
        q_ref, k_ref, v_ref, yn_ref, nconv_ref, carry_ref = refs[8:]
    i = pl.program_id(0)
    a = _rms(h_ref[...], gmix_ref[...])
    proj = jnp.dot(a.astype(BF16), w_ref[...], preferred_element_type=F32)
    tm = proj.shape[0]

    lower = lax.broadcasted_iota(jnp.int32, (tm, LANES), 1) < HEAD_DIM

    def head_norm(t, g):
        out = []
        for j in range(ATT_DIM // LANES):
            blk = t[:, j * LANES:(j + 1) * LANES]
            sq = blk * blk
            ms_lo = jnp.sum(jnp.where(lower, sq, 0.0), axis=-1, keepdims=True) * (1.0 / HEAD_DIM)
            ms_hi = jnp.sum(jnp.where(lower, 0.0, sq), axis=-1, keepdims=True) * (1.0 / HEAD_DIM)
            scale = jnp.where(lower, lax.rsqrt(ms_lo + EPS), lax.rsqrt(ms_hi + EPS))
            out.append(blk * scale)
        return jnp.concatenate(out, axis=-1) * g

    q = head_norm(proj[:, 0:ATT_DIM], qg_ref[...])
    k = head_norm(proj[:, ATT_DIM:2 * ATT_DIM], kg_ref[...])
    v = proj[:, 2 * ATT_DIM:3 * ATT_DIM]
    if permute:
        kn_ref[...] = k.T.reshape(N_HEADS, HEAD_DIM, tm)
        vn_ref[...] = v.T.reshape(N_HEADS, HEAD_DIM, tm)
        qkv = jnp.concatenate([q * (HEAD_DIM ** -0.5), k, v], axis=-1).astype(BF16)
        pm = perm_ref.shape[0]
        moved = jnp.concatenate([jnp.dot(perm_ref[...], qkv[r:r + pm], preferred_element_type=F32)
                                 for r in range(0, tm, pm)], axis=0)
        q_ref[...] = moved[:, 0:ATT_DIM]
        k_ref[...] = moved[:, ATT_DIM:2 * ATT_DIM]
        v_ref[...] = moved[:, 2 * ATT_DIM:3 * ATT_DIM]
    else:
        q_ref[...] = q
        k_ref[...] = k
        v_ref[...] = v
    c0 = 3 * ATT_DIM
    hc = proj[:, c0:c0 + CONV_DIM]
    gb = proj[:, c0 + CONV_DIM:c0 + 2 * CONV_DIM]
    gc = proj[:, c0 + 2 * CONV_DIM:c0 + 3 * CONV_DIM]
    u = gc * hc

    if shift == 1:
        @pl.when(i % tiles_per_seq == 0)
        def _():
            carry_ref[...] = hist_ref[0]
        h0 = carry_ref[SUBLANES - 2:SUBLANES - 1, :]
        h1 = carry_ref[SUBLANES - 1:SUBLANES, :]
        row = lax.broadcasted_iota(jnp.int32, (tm, 1), 0)
        u1 = jnp.where(row == 0, h1, pltpu.roll(u, 1, 0))
        u2 = jnp.where(row == 0, h0, jnp.where(row == 1, h1, pltpu.roll(u, 2, 0)))
        carry_ref[...] = u[tm - SUBLANES:tm, :]
        nconv_ref[0] = u[tm - SUBLANES:tm, :]
    else:
        hist = hist_ref[0]
        u1 = jnp.concatenate([hist[shift:2 * shift], u[0:tm - shift]], axis=0)
        u2 = jnp.concatenate([hist, u[0:tm - 2 * shift]], axis=0)
        nconv_ref[0] = u[tm - 2 * shift:tm, :]
    conv = cw_ref[0:1, :] * u2 + cw_ref[1:2, :] * u1 + cw_ref[2:3, :] * u
    yn_ref[...] = _rms(gb * conv, gconv_ref[...])


def _inproj(h, hist, g_mix, w_in_bf, q_gain, k_gain, conv_w, g_out_conv, perm=None, *,
            tm, shift, tiles_per_seq, keep_tiles=0):
    n = h.shape[0]
    hist_rows = hist.shape[1]
    nseq = hist.shape[0]
    tok = lambda w: pl.BlockSpec((tm, w), lambda i: (i, 0))
    seq3 = lambda r: pl.BlockSpec((1, r, CONV_DIM), lambda i: (i // tiles_per_seq, 0, 0))
    nconv_rows = SUBLANES if shift == 1 else 2 * shift
    att = jax.ShapeDtypeStruct((n, ATT_DIM), F32)
    in_specs = [tok(D_MODEL), _full((1, D_MODEL)), _full((D_MODEL, MIX_IN)), _full((1, ATT_DIM)),
                _full((1, ATT_DIM)), _full((3, CONV_DIM)), _full((1, CONV_DIM)), seq3(hist_rows)]
    args = [h, g_mix, w_in_bf, q_gain, k_gain, conv_w, g_out_conv, hist]
    out_specs = [tok(ATT_DIM)] * 3
    out_shape = [att] * 3
    if perm is not None:
        in_specs.append(_full(perm.shape))
        args.append(perm)
        first = tiles_per_seq - keep_tiles
        kept = pl.BlockSpec((None, N_HEADS, HEAD_DIM, tm),
                            lambda i: (i // tiles_per_seq, 0, 0, jnp.maximum(i % tiles_per_seq - first, 0)))
        out_specs += [kept, kept]
        out_shape += [jax.ShapeDtypeStruct((nseq, N_HEADS, HEAD_DIM, keep_tiles * tm), F32)] * 2
    out_specs += [tok(CONV_DIM), seq3(nconv_rows)]
    out_shape += [jax.ShapeDtypeStruct((n, CONV_DIM), F32),
                  jax.ShapeDtypeStruct((nseq, nconv_rows, CONV_DIM), F32)]
    return pl.pallas_call(
        functools.partial(_inproj_kernel, shift=shift, tiles_per_seq=tiles_per_seq, permute=perm is not None),
        grid=(n // tm,),
        in_specs=in_specs, out_specs=out_specs, out_shape=out_shape,
        scratch_shapes=[pltpu.VMEM((SUBLANES, CONV_DIM), F32)],
        compiler_params=_cparams(1),
        name="inproj",
    )(*args)


def _attn_block(q, kk, vv, bias):
    lane = lax.broadcasted_iota(jnp.int32, (Q_BLOCK, LANES), 1)
    upper = lane >= HEAD_DIM
    scores = []
    for h in range(N_HEADS):
        j, e = divmod(h, 2)
        qp = q[:, j * LANES:(j + 1) * LANES]
        qm = (jnp.where(upper, qp, 0.0) if e else jnp.where(upper, 0.0, qp)).astype(BF16)
        scores.append(lax.dot_general(qm, kk[:, j * LANES:(j + 1) * LANES], NT, preferred_element_type=F32))
    s = jnp.concatenate(scores, axis=0) + bias
    m = jnp.max(s, axis=-1, keepdims=True)
    p = jnp.exp(s - m)
    den = jnp.sum(p, axis=-1, keepdims=True)
    pb = p.astype(BF16)
    inv = 1.0 / den
    lse = m + jnp.log(den)
    lse_grp = lane // LSE_LANES_PER_HEAD
    lse_tile = jnp.zeros((Q_BLOCK, LANES), F32)
    outs = []
    for j in range(N_HEADS // 2):
        pair = None
        for e in range(2):
            h = 2 * j + e
            rows = slice(h * Q_BLOCK, (h + 1) * Q_BLOCK)
            o = jnp.dot(pb[rows], vv[:, j * LANES:(j + 1) * LANES], preferred_element_type=F32) * inv[rows]
            pair = o if e == 0 else jnp.where(upper, o, pair)
            lse_tile = jnp.where(lse_grp == h, lse[rows], lse_tile)
        outs.append(pair)
    return jnp.concatenate(outs, axis=-1), lse_tile


def _attn_kernel(q_ref, kp_ref, kc_ref, vp_ref, vc_ref, bias_ref, o_ref, lse_ref, kbuf, vbuf, *, sub, res):
    n = pl.program_id(2)
    rows = sub * Q_BLOCK
    for r in range(res):
        kbuf[0:Q_BLOCK, :] = kp_ref[:, r].reshape(Q_BLOCK, ATT_DIM).astype(BF16)
        kbuf[Q_BLOCK:, :] = kc_ref[:, :, r].reshape(rows, ATT_DIM).astype(BF16)
        vbuf[0:Q_BLOCK, :] = vp_ref[:, r].reshape(Q_BLOCK, ATT_DIM).astype(BF16)
        vbuf[Q_BLOCK:, :] = vc_ref[:, :, r].reshape(rows, ATT_DIM).astype(BF16)
        for j in range(sub):
            q = q_ref[j, :, r].reshape(Q_BLOCK, ATT_DIM)
            r0 = j * Q_BLOCK
            first = (n == 0).astype(jnp.int32) if j == 0 else 0
            o, lse = _attn_block(q, kbuf[r0:r0 + 2 * Q_BLOCK, :], vbuf[r0:r0 + 2 * Q_BLOCK, :], bias_ref[first])
            o_ref[j, :, r] = o.reshape(SLABS, SUBLANES, ATT_DIM)
            lse_ref[j, :, r] = lse.reshape(SLABS, SUBLANES, LANES)


def _attn_pattern(q, k, v, bias, *, batch, seq, dil, sub, res):
    nblk = seq // (Q_BLOCK * dil)
    view = lambda t: t.reshape(batch, nblk, SLABS, dil, SUBLANES, t.shape[-1])
    cur = lambda c: pl.BlockSpec((None, sub, SLABS, res, SUBLANES, c), lambda b, r, n: (b, n, 0, r, 0, 0))
    prev = pl.BlockSpec((None, None, SLABS, res, SUBLANES, ATT_DIM),
                        lambda b, r, n: (b, jnp.maximum(n * sub - 1, 0), 0, r, 0, 0))
    o, lse = pl.pallas_call(
        functools.partial(_attn_kernel, sub=sub, res=res),
        grid=(batch, dil // res, nblk // sub),
        in_specs=[cur(ATT_DIM), prev, cur(ATT_DIM), prev, cur(ATT_DIM),
                  _full((2, N_HEADS * Q_BLOCK, 2 * Q_BLOCK))],
        out_specs=[cur(ATT_DIM), cur(LANES)],
        out_shape=[jax.ShapeDtypeStruct((batch, nblk, SLABS, dil, SUBLANES, ATT_DIM), F32),
                   jax.ShapeDtypeStruct((batch, nblk, SLABS, dil, SUBLANES, LANES), F32)],
        scratch_shapes=[pltpu.VMEM(((sub + 1) * Q_BLOCK, ATT_DIM), BF16),
                        pltpu.VMEM(((sub + 1) * Q_BLOCK, ATT_DIM), BF16)],
        compiler_params=_cparams(3),
        name=f"attn_d{dil}",
    )(view(q), view(k), view(k), view(v), view(v), bias)
    return o.reshape(batch * seq, ATT_DIM), lse.reshape(batch * seq, LANES)


SAMPLE_T = 4
NEW_COLS = LANES


def _attn_sample_kernel(q_ref, kt_ref, kn_ref, vt_ref, vn_ref, bias_ref, mult_ref, o_ref):
    rows = SAMPLE_T * N_HEADS
    q4 = q_ref[...] * (HEAD_DIM ** -0.5)
    qt = jnp.concatenate([jnp.broadcast_to(q4[t:t + 1, :], (N_HEADS, ATT_DIM)) for t in range(SAMPLE_T)], axis=0)
    lane_head = lax.broadcasted_iota(jnp.int32, (rows, ATT_DIM), 1) // HEAD_DIM
    row_head = lax.broadcasted_iota(jnp.int32, (rows, ATT_DIM), 0) % N_HEADS
    own = lane_head == row_head
    qbd = jnp.where(own, qt, 0.0).astype(BF16)
    flat = lambda ref: ref[...].reshape(ATT_DIM, ref.shape[-1]).astype(BF16)
    pad = jnp.zeros((NEW_COLS - SUBLANES, ATT_DIM), F32)
    new_rows = lambda ref: jnp.concatenate([ref[...], pad], axis=0).astype(BF16)
    s = jnp.concatenate([jnp.dot(qbd, flat(kt_ref), preferred_element_type=F32),
                         lax.dot_general(qbd, new_rows(kn_ref), NT, preferred_element_type=F32)],
                        axis=-1) + bias_ref[...]
    m = jnp.max(s, axis=-1, keepdims=True)
    p = jnp.exp(s - m) * mult_ref[...]
    den = jnp.sum(p, axis=-1, keepdims=True)
    pb = p.astype(BF16)
    w_buf = kt_ref.shape[-1]
    acc = (lax.dot_general(pb[:, :w_buf], flat(vt_ref), NT, preferred_element_type=F32)
           + jnp.dot(pb[:, w_buf:], new_rows(vn_ref), preferred_element_type=F32))
    acc = jnp.where(own, acc / den, 0.0)
    for t in range(SAMPLE_T):
        o_ref[t:t + 1, :] = jnp.sum(acc[t * N_HEADS:(t + 1) * N_HEADS, :], axis=0, keepdims=True)


def _attn_sample(q, k_new, v_new, cache_kt, cache_vt, layer, bias, mult):
    nb = q.shape[0]
    w_buf = cache_kt.shape[-1]
    tok = pl.BlockSpec((None, SAMPLE_T, ATT_DIM), lambda b: (b, 0, 0))
    new = pl.BlockSpec((None, SUBLANES, ATT_DIM), lambda b: (b, 0, 0))
    old = pl.BlockSpec((None, None, N_HEADS, HEAD_DIM, w_buf), lambda b: (layer, b, 0, 0, 0))
    tbl = _full((SAMPLE_T * N_HEADS, w_buf + NEW_COLS))
    return pl.pallas_call(
        _attn_sample_kernel,
        grid=(nb,),
        in_specs=[tok, old, new, old, new, tbl, tbl],
        out_specs=tok,
        out_shape=jax.ShapeDtypeStruct((nb, SAMPLE_T, ATT_DIM), F32),
        compiler_params=_cparams(1),
        name="attn_sample",
    )(q, cache_kt, k_new, cache_vt, v_new, bias, mult)


def _split_dot(x, e_ref):
    hi = x.astype(BF16)
    lo = (x - hi.astype(F32)).astype(BF16)
    return (jnp.dot(hi, e_ref[...], preferred_element_type=F32)
            + jnp.dot(lo, e_ref[...], preferred_element_type=F32))


def _outproj_kernel(*refs, n_pat):
    mix = n_pat > 1
    n_lse = n_pat if mix else 0
    o_refs = refs[0:n_pat]
    l_refs = refs[n_pat:n_pat + n_lse]
    rest = refs[n_pat + n_lse:]
    if mix:
        yn_ref, h_ref, gatt_ref, exp_ref, unperm_ref, wa_ref, wc_ref, out_ref = rest
        lses = [r[...] for r in l_refs]
        top = functools.reduce(jnp.maximum, lses)
        ws = [jnp.exp(l - top) for l in lses]
        tot = functools.reduce(lambda a, b: a + b, ws)
        att = None
        for w, o_ref in zip(ws, o_refs):
            term = _split_dot(w / tot, exp_ref) * o_ref[...]
            att = term if att is None else att + term
        att_bf = jnp.dot(unperm_ref[...], _rms(att, gatt_ref[...]).astype(BF16),
                         preferred_element_type=F32).astype(BF16)
    else:
        yn_ref, h_ref, gatt_ref, wa_ref, wc_ref, out_ref = rest
        att_bf = _rms(o_refs[0][...], gatt_ref[...]).astype(BF16)
    y = (jnp.dot(att_bf, wa_ref[...], preferred_element_type=F32)
         + jnp.dot(yn_ref[...].astype(BF16), wc_ref[...], preferred_element_type=F32))
    out_ref[...] = h_ref[...] + y


def _outproj(os, lses, yn, h, g_att, wa_bf, wc_bf, expand=None, unperm=None, *, tm):
    n = h.shape[0]
    n_pat = len(os)
    tok = lambda w: pl.BlockSpec((tm, w), lambda i: (i, 0))
    in_specs = [tok(ATT_DIM)] * n_pat + [tok(LANES)] * len(lses) + [tok(CONV_DIM), tok(D_MODEL), _full((1, ATT_DIM))]
    args = [*os, *lses, yn, h, g_att]
    if n_pat > 1:
        in_specs += [_full((LANES, ATT_DIM)), _full((tm, tm))]
        args += [expand, unperm]
    in_specs += [_full((ATT_DIM, D_MODEL)), _full((CONV_DIM, D_MODEL))]
    args += [wa_bf, wc_bf]
    return pl.pallas_call(
        functools.partial(_outproj_kernel, n_pat=n_pat),
        grid=(n // tm,),
        in_specs=in_specs,
        out_specs=tok(D_MODEL),
        out_shape=jax.ShapeDtypeStruct((n, D_MODEL), F32),
        compiler_params=_cparams(1),
        name="outproj",
    )(*args)


ROUTE_I1, ROUTE_I2, ROUTE_R1, ROUTE_R2, ROUTE_W1, ROUTE_W2 = range(6)
GROUP_LANE0 = N_EXPERTS


def _route_kernel(h_ref, g_ref, wr_ref, route_ref, cnt_ref, carry_ref):
    i = pl.program_id(0)

    @pl.when(i == 0)
    def _():
        carry_ref[...] = jnp.zeros_like(carry_ref)

    m = _rms(h_ref[...], g_ref[...])
    logits = jnp.dot(m.astype(BF16), wr_ref[...], preferred_element_type=F32)
    tm = logits.shape[0]
    lane_i = lax.broadcasted_iota(jnp.int32, (tm, LANES), 1)
    lane = lane_i.astype(F32)
    big = jnp.float32(4 * LANES)

    is_g = jnp.logical_and(lane_i >= GROUP_LANE0, lane_i < GROUP_LANE0 + N_GROUPS)
    gl = jnp.where(is_g, logits, NEG)
    gmax = jnp.max(gl, axis=-1, keepdims=True)
    g_w = 1.0 / jnp.sum(jnp.where(is_g, jnp.exp(gl - gmax), 0.0), axis=-1, keepdims=True)
    g_sel = jnp.min(jnp.where(gl == gmax, lane - GROUP_LANE0, big), axis=-1, keepdims=True)

    grp_of_lane = (lane_i // EXPERTS_PER_GROUP).astype(F32)
    in_grp = jnp.logical_and(lane_i < N_EXPERTS, grp_of_lane == g_sel)
    el = jnp.where(in_grp, logits, NEG)
    t1 = jnp.max(el, axis=-1, keepdims=True)
    i1 = jnp.min(jnp.where(el == t1, lane, big), axis=-1, keepdims=True)
    el2 = jnp.where(lane == i1, NEG, el)
    t2 = jnp.max(el2, axis=-1, keepdims=True)
    i2 = jnp.min(jnp.where(el2 == t2, lane, big), axis=-1, keepdims=True)
    e2 = jnp.exp(t2 - t1)
    w1 = g_w / (1.0 + e2)
    w2 = g_w * e2 / (1.0 + e2)

    hit1 = lane == i1
    hit2 = lane == i2
    c = jnp.where(jnp.logical_or(hit1, hit2), 1.0, 0.0)
    rr = lax.broadcasted_iota(jnp.int32, (tm, tm), 0)
    cc = lax.broadcasted_iota(jnp.int32, (tm, tm), 1)
    lower = jnp.where(rr > cc, 1.0, 0.0).astype(BF16)
    before = jnp.dot(lower, c.astype(BF16), preferred_element_type=F32) + carry_ref[0:1, :]
    r1 = jnp.sum(jnp.where(hit1, before, 0.0), axis=-1, keepdims=True)
    r2 = jnp.sum(jnp.where(hit2, before, 0.0), axis=-1, keepdims=True)
    total = carry_ref[0:1, :] + jnp.sum(c, axis=0, keepdims=True)
    carry_ref[...] = jnp.broadcast_to(total, carry_ref.shape)
    cnt_ref[...] = jnp.broadcast_to(total, cnt_ref.shape)

    rec = jnp.zeros((tm, LANES), F32)
    for idx, val in ((ROUTE_I1, i1), (ROUTE_I2, i2), (ROUTE_R1, r1), (ROUTE_R2, r2), (ROUTE_W1, w1), (ROUTE_W2, w2)):
        rec = jnp.where(lane_i == idx, val, rec)
    route_ref[...] = rec


def _route(h, g_ffn, w_router, *, tm):
    n = h.shape[0]
    tok = lambda w: pl.BlockSpec((tm, w), lambda i: (i, 0))
    return pl.pallas_call(
        _route_kernel,
        grid=(n // tm,),
        in_specs=[tok(D_MODEL), _full((1, D_MODEL)), _full((D_MODEL, LANES))],
        out_specs=[tok(LANES), _full((SUBLANES, LANES))],
        out_shape=[jax.ShapeDtypeStruct((n, LANES), F32), jax.ShapeDtypeStruct((SUBLANES, LANES), F32)],
        scratch_shapes=[pltpu.VMEM((SUBLANES, LANES), F32)],
        compiler_params=_cparams(1),
        name="route",
    )(h, g_ffn, w_router)


def _dispatch_kernel(pos_ref, h_ref, g_ref, xs_ref, mbuf, sem):
    tm = mbuf.shape[0]
    mbuf[...] = _rms(h_ref[...], g_ref[...])

    def row_copy(t, p):
        return pltpu.make_async_copy(mbuf.at[pl.ds(t, 1), :], xs_ref.at[pl.ds(p, 1), :], sem)

    def issue(t, carry):
        row_copy(t, pos_ref[0, 0, 2 * t]).start()
        row_copy(t, pos_ref[0, 0, 2 * t + 1]).start()
        return carry

    lax.fori_loop(0, tm, issue, 0)

    def drain(t, carry):
        row_copy(0, 0).wait()
        row_copy(0, 0).wait()
        return carry

    lax.fori_loop(0, tm, drain, 0)


def _dispatch(h, g_ffn, pos, n_rows, *, tm):
    n = h.shape[0]
    return pl.pallas_call(
        _dispatch_kernel,
        grid=(n // tm,),
        in_specs=[pl.BlockSpec((1, 1, 2 * tm), lambda i: (i, 0, 0), memory_space=pltpu.SMEM),
                  pl.BlockSpec((tm, D_MODEL), lambda i: (i, 0)), _full((1, D_MODEL))],
        out_specs=pl.BlockSpec(memory_space=pl.ANY),
        out_shape=jax.ShapeDtypeStruct((n_rows, D_MODEL), F32),
        scratch_shapes=[pltpu.VMEM((tm, D_MODEL), F32), pltpu.SemaphoreType.DMA(())],
        compiler_params=pltpu.CompilerParams(dimension_semantics=("arbitrary",), vmem_limit_bytes=VMEM_LIMIT,
                                             has_side_effects=True),
        name="dispatch",
    )(pos, h, g_ffn)


def _expert_kernel(te_ref, tv_ref, x_ref, wg_ref, wu_ref, wd_ref, y_ref, wg_bf, wu_bf, wd_bf):
    i = pl.program_id(0)
    valid = tv_ref[i]
    changed = jnp.logical_or(i == 0, te_ref[i] != te_ref[jnp.maximum(i - 1, 0)])

    @pl.when(jnp.logical_and(changed, valid > 0))
    def _():
        wg_bf[...] = wg_ref[0].astype(BF16)
        wu_bf[...] = wu_ref[0].astype(BF16)
        wd_bf[...] = wd_ref[0].astype(BF16)

    @pl.when(valid > 0)
    def _():
        te = x_ref.shape[0]
        row = lax.broadcasted_iota(jnp.int32, (te, 1), 0)
        x = jnp.where(row < valid, x_ref[...], 0.0).astype(BF16)
        hg = jnp.dot(x, wg_bf[...], preferred_element_type=F32)
        hu = jnp.dot(x, wu_bf[...], preferred_element_type=F32)
        hid = (hg * jax.nn.sigmoid(hg)) * hu
        y_ref[...] = jnp.dot(hid.astype(BF16), wd_bf[...], preferred_element_type=F32)

    @pl.when(valid <= 0)
    def _():
        y_ref[...] = jnp.zeros_like(y_ref)


def _experts(xs, tile_expert, tile_valid, w_gate, w_up, w_down, *, te):
    n_rows = xs.shape[0]
    wspec = lambda a, b: pl.BlockSpec((1, a, b), lambda i, e, v: (e[i], 0, 0))
    grid_spec = pltpu.PrefetchScalarGridSpec(
        num_scalar_prefetch=2,
        grid=(n_rows // te,),
        in_specs=[pl.BlockSpec((te, D_MODEL), lambda i, e, v: (i, 0)),
                  wspec(D_MODEL, D_EXPERT), wspec(D_MODEL, D_EXPERT), wspec(D_EXPERT, D_MODEL)],
        out_specs=pl.BlockSpec((te, D_MODEL), lambda i, e, v: (i, 0)),
        scratch_shapes=[pltpu.VMEM((D_MODEL, D_EXPERT), BF16), pltpu.VMEM((D_MODEL, D_EXPERT), BF16),
                        pltpu.VMEM((D_EXPERT, D_MODEL), BF16)])
    return pl.pallas_call(
        _expert_kernel,
        grid_spec=grid_spec,
        out_shape=jax.ShapeDtypeStruct((n_rows, D_MODEL), F32),
        compiler_params=_cparams(1),
        name="experts",
    )(tile_expert, tile_valid, xs, w_gate, w_up, w_down)


def _combine_kernel(pos_ref, h_ref, route_ref, ys_ref, p_ref, gple_ref, wgate_ref, wproj_ref, out_ref,
                    y0, y1, sem):
    tm = y0.shape[0]

    def row_copy(p, dst, t):
        return pltpu.make_async_copy(ys_ref.at[pl.ds(p, 1), :], dst.at[pl.ds(t, 1), :], sem)

    def issue(t, carry):
        row_copy(pos_ref[0, 0, 2 * t], y0, t).start()
        row_copy(pos_ref[0, 0, 2 * t + 1], y1, t).start()
        return carry

    lax.fori_loop(0, tm, issue, 0)

    def drain(t, carry):
        row_copy(0, y0, 0).wait()
        row_copy(0, y1, 0).wait()
        return carry

    lax.fori_loop(0, tm, drain, 0)

    rec = route_ref[...]
    w1 = rec[:, ROUTE_W1:ROUTE_W1 + 1]
    w2 = rec[:, ROUTE_W2:ROUTE_W2 + 1]
    h2 = h_ref[...] + w1 * y0[...] + w2 * y1[...]
    gate = jax.nn.sigmoid(jnp.dot(_rms(h2, gple_ref[...]).astype(BF16), wgate_ref[...], preferred_element_type=F32))
    ple = jnp.dot(p_ref[...].astype(BF16), wproj_ref[...], preferred_element_type=F32)
    out_ref[...] = h2 + ple * gate


def _combine(h, route, pos, ys, p, g_ple, wgate_bf, wproj_bf, *, tm):
    n = h.shape[0]
    tok = lambda w: pl.BlockSpec((tm, w), lambda i: (i, 0))
    return pl.pallas_call(
        _combine_kernel,
        grid=(n // tm,),
        in_specs=[pl.BlockSpec((1, 1, 2 * tm), lambda i: (i, 0, 0), memory_space=pltpu.SMEM),
                  tok(D_MODEL), tok(LANES), pl.BlockSpec(memory_space=pl.ANY), tok(D_PLE),
                  _full((1, D_MODEL)), _full((D_MODEL, D_MODEL)), _full((D_PLE, D_MODEL))],
        out_specs=tok(D_MODEL),
        out_shape=jax.ShapeDtypeStruct((n, D_MODEL), F32),
        scratch_shapes=[pltpu.VMEM((tm, D_MODEL), F32), pltpu.VMEM((tm, D_MODEL), F32),
                        pltpu.SemaphoreType.DMA(())],
        compiler_params=_cparams(1),
        name="combine",
    )(pos, h, route, ys, p, g_ple, wgate_bf, wproj_bf)


def _lookup(table, idx, size):
    hit = idx[..., None] == jnp.arange(size, dtype=jnp.int32)
    return jnp.sum(jnp.where(hit, table, 0), axis=-1)


def _moe_ple(h, p, g_ffn, w_router, w_gate, w_up, w_down, g_ple, wgate_bf, wproj_bf, *, tm, te):
    n = h.shape[0]
    route, counts = _route(h, g_ffn, w_router, tm=tm)
    counts = counts[0, :N_EXPERTS].astype(jnp.int32)
    padded = ((counts + te - 1) // te) * te
    ends = jnp.cumsum(padded)
    offs = ends - padded
    ids = route[:, ROUTE_I1:ROUTE_I2 + 1].astype(jnp.int32)
    ranks = route[:, ROUTE_R1:ROUTE_R2 + 1].astype(jnp.int32)
    pos = (_lookup(offs, ids, N_EXPERTS) + ranks).reshape(n // tm, 1, 2 * tm)
    n_tiles = (2 * n) // te + N_EXPERTS
    starts = jnp.arange(n_tiles, dtype=jnp.int32) * te
    tile_expert = jnp.minimum(jnp.sum((starts[:, None] >= ends[None, :]).astype(jnp.int32), axis=-1), N_EXPERTS - 1)
    tile_valid = jnp.clip(_lookup(counts, tile_expert, N_EXPERTS)
                          - (starts - _lookup(offs, tile_expert, N_EXPERTS)), 0, te).astype(jnp.int32)
    xs = _dispatch(h, g_ffn, pos, n_tiles * te, tm=tm)
    ys = _experts(xs, tile_expert, tile_valid, w_gate, w_up, w_down, te=te)
    return _combine(h, route, pos, ys, p, g_ple, wgate_bf, wproj_bf, tm=tm)


REC_E1, REC_E2, REC_S1, REC_S2, REC_W1, REC_W2 = range(6)
PACKED = D_MODEL // 2
ROW_W = PACKED + LANES
LIST_LANES = LANES
LIST_COUNT = LIST_LANES - 1
LIST_NEXT = LIST_LANES - 2
LIST_PARITY = LIST_LANES - 3


def _slots(tm):
    need = 2 * tm + (SUBLANES - 1) * N_EXPERTS + SUBLANES
    return -(-need // LANES) * LANES


def _pieces(x):
    hi = x.astype(BF16)
    r1 = x - hi.astype(F32)
    mid = r1.astype(BF16)
    return hi, mid, (r1 - mid.astype(F32)).astype(BF16)


def _sort_kernel(h_ref, g_ref, wr_ref, before_ref, sel_ref, xs_ref, rec_ref, cnt_ref, *, group):
    tm = h_ref.shape[0] // group
    slots = xs_ref.shape[0] // group
    for t in range(group):
        xs, rec, cnt = _sort_tile(h_ref[t * tm:(t + 1) * tm, :], g_ref, wr_ref, before_ref, sel_ref, slots)
        xs_ref[t * slots:(t + 1) * slots, :] = xs
        rec_ref[t * tm:(t + 1) * tm, :] = rec
        cnt_ref[t] = cnt


def _sort_tile(h, g_ref, wr_ref, before_ref, sel_ref, slots):
    m = _rms(h, g_ref[...])
    logits = jnp.dot(m.astype(BF16), wr_ref[...], preferred_element_type=F32)
    tm = logits.shape[0]
    lane_i = lax.broadcasted_iota(jnp.int32, (tm, LANES), 1)
    lane = lane_i.astype(F32)
    big = jnp.float32(4 * LANES)

    is_g = jnp.logical_and(lane_i >= GROUP_LANE0, lane_i < GROUP_LANE0 + N_GROUPS)
    gl = jnp.where(is_g, logits, NEG)
    gmax = jnp.max(gl, axis=-1, keepdims=True)
    g_w = 1.0 / jnp.sum(jnp.where(is_g, jnp.exp(gl - gmax), 0.0), axis=-1, keepdims=True)
    g_sel = jnp.min(jnp.where(gl == gmax, lane - GROUP_LANE0, big), axis=-1, keepdims=True)

    grp_of_lane = (lane_i // EXPERTS_PER_GROUP).astype(F32)
    in_grp = jnp.logical_and(lane_i < N_EXPERTS, grp_of_lane == g_sel)
    el = jnp.where(in_grp, logits, NEG)
    t1 = jnp.max(el, axis=-1, keepdims=True)
    e1 = jnp.min(jnp.where(el == t1, lane, big), axis=-1, keepdims=True)
    el2 = jnp.where(lane == e1, NEG, el)
    t2 = jnp.max(el2, axis=-1, keepdims=True)
    e2 = jnp.min(jnp.where(el2 == t2, lane, big), axis=-1, keepdims=True)
    ex = jnp.exp(t2 - t1)
    w1 = g_w / (1.0 + ex)
    w2 = g_w * ex / (1.0 + ex)

    hit1 = lane == e1
    hit2 = lane == e2
    c = jnp.where(hit1, 1.0, jnp.where(hit2, 1.0, 0.0))
    rr = lax.broadcasted_iota(jnp.int32, (tm, tm), 0)
    cc = lax.broadcasted_iota(jnp.int32, (tm, tm), 1)
    lower = jnp.where(rr > cc, 1.0, 0.0).astype(BF16)
    rank = jnp.dot(lower, c.astype(BF16), preferred_element_type=F32)
    cnt = jnp.sum(c, axis=0, keepdims=True)
    chunks = jnp.floor((cnt + (SUBLANES - 1)) * (1.0 / SUBLANES))
    start = SUBLANES * jnp.dot(jnp.broadcast_to(chunks, (SUBLANES, LANES)).astype(BF16), before_ref[...],
                               preferred_element_type=F32)[0:1, :]
    slot_of = rank + start
    s1 = jnp.sum(jnp.where(hit1, slot_of, 0.0), axis=-1, keepdims=True)
    s2 = jnp.sum(jnp.where(hit2, slot_of, 0.0), axis=-1, keepdims=True)

    rec = jnp.zeros((tm, LANES), F32)
    for idx, val in ((REC_E1, e1), (REC_E2, e2), (REC_S1, s1), (REC_S2, s2), (REC_W1, w1), (REC_W2, w2)):
        rec = jnp.where(lane_i == idx, val, rec)

    rec_parts = _pieces(rec)
    srow = sum(lax.dot_general(sel_ref[...], part, NT, preferred_element_type=F32) for part in rec_parts)
    slot_id = lax.broadcasted_iota(jnp.int32, (slots, tm), 0).astype(F32)
    place = jnp.where(slot_id == srow[0:1, :], 1.0, jnp.where(slot_id == srow[1:2, :], 1.0, 0.0)).astype(BF16)
    payload = jnp.concatenate([m.astype(BF16), *rec_parts], axis=-1)
    moved = jnp.dot(place, payload, preferred_element_type=F32)
    info = (moved[:, D_MODEL:D_MODEL + LANES] + moved[:, D_MODEL + LANES:D_MODEL + 2 * LANES]
            + moved[:, D_MODEL + 2 * LANES:])
    xs = jnp.concatenate([_pack_bf16(moved[:, :D_MODEL], is_bf16=True), pltpu.bitcast(info, jnp.uint32)], axis=-1)
    return xs, rec, jnp.broadcast_to(cnt, (SUBLANES, LANES))


def _sort(h, g_ffn, w_router, before, sel, *, tm):
    n = h.shape[0]
    slots = _slots(tm)
    group = 2 if (n // tm) % 2 == 0 else 1
    tok = lambda w: pl.BlockSpec((group * tm, w), lambda i: (i, 0))
    return pl.pallas_call(
        functools.partial(_sort_kernel, group=group),
        grid=(n // (group * tm),),
        in_specs=[tok(D_MODEL), _full((1, D_MODEL)), _full((D_MODEL, LANES)), _full((LANES, LANES)),
                  _full((SUBLANES, LANES))],
        out_specs=[pl.BlockSpec((group * slots, ROW_W), lambda i: (i, 0)), tok(LANES),
                   pl.BlockSpec((group, SUBLANES, LANES), lambda i: (i, 0, 0))],
        out_shape=[jax.ShapeDtypeStruct((n // tm * slots, ROW_W), jnp.uint32), jax.ShapeDtypeStruct((n, LANES), F32),
                   jax.ShapeDtypeStruct((n // tm, SUBLANES, LANES), F32)],
        compiler_params=_cparams(1),
        name="moe_sort",
    )(h, g_ffn, w_router, before, sel)


def _chunk_copy(src_hbm, row, dst, c, sem):
    if not isinstance(row, int):
        row = pl.multiple_of(row, SUBLANES)
    to = c * SUBLANES
    if not isinstance(to, int):
        to = pl.multiple_of(to, SUBLANES)
    return pltpu.make_async_copy(src_hbm.at[pl.ds(row, SUBLANES), :], dst.at[pl.ds(to, SUBLANES), :], sem)


def _expert_kernel2(te_ref, tv_ref, lst_ref, xa_ref, xb_ref, wg_ref, wu_ref, wd_ref, y_ref,
                    xbuf, sem, wg_st, wu_st, wd_st, wsem, wg_bf, wu_bf, wd_bf, *, rows_a, layer):
    i = pl.program_id(0)
    n = pl.num_programs(0)
    te = xbuf.shape[1]
    slot = i % 2

    def fetch(tile, to_slot):
        n_a = lst_ref[tile, LIST_COUNT]

        def from_a(c, carry):
            _chunk_copy(xa_ref, lst_ref[tile, c], xbuf.at[to_slot], c, sem.at[to_slot]).start()
            return carry

        def from_b(c, carry):
            _chunk_copy(xb_ref, lst_ref[tile, c] - rows_a, xbuf.at[to_slot], c, sem.at[to_slot]).start()
            return carry

        per_tile = te // SUBLANES

        @pl.when(n_a == per_tile)
        def _():
            for c in range(per_tile):
                from_a(c, 0)

        @pl.when(n_a != per_tile)
        def _():
            lax.fori_loop(0, n_a, from_a, 0)
            lax.fori_loop(n_a, per_tile, from_b, 0)

    @pl.when(jnp.logical_and(i == 0, tv_ref[0] > 0))
    def _():
        fetch(0, 0)

    nxt = jnp.minimum(i + 1, n - 1)

    @pl.when(jnp.logical_and(i + 1 < n, tv_ref[nxt] > 0))
    def _():
        fetch(nxt, 1 - slot)

    valid = tv_ref[i]
    changed = jnp.logical_or(i == 0, te_ref[i] != te_ref[jnp.maximum(i - 1, 0)])

    def weight_copies(expert, s):
        return [pltpu.make_async_copy(src.at[layer, expert], dst.at[s], wsem.at[s])
                for src, dst in ((wg_ref, wg_st), (wu_ref, wu_st), (wd_ref, wd_st))]

    @pl.when(jnp.logical_and(changed, valid > 0))
    def _():
        s = lst_ref[i, LIST_PARITY]

        @pl.when(i == 0)
        def _():
            for cp in weight_copies(te_ref[0], 0):
                cp.start()

        for cp in weight_copies(te_ref[i], s):
            cp.wait()
        wg_bf[...] = wg_st[s].astype(BF16)
        wu_bf[...] = wu_st[s].astype(BF16)
        wd_bf[...] = wd_st[s].astype(BF16)
        nxt_expert = lst_ref[i, LIST_NEXT]

        @pl.when(nxt_expert >= 0)
        def _():
            for cp in weight_copies(nxt_expert, 1 - s):
                cp.start()

    @pl.when(valid > 0)
    def _():
        pltpu.make_async_copy(xbuf.at[slot], xbuf.at[slot], sem.at[slot]).wait()
        rows = xbuf[slot]
        x = _unpack_bf16(rows[:, :PACKED])
        info = pltpu.bitcast(rows[:, PACKED:], F32)
        mine = info[:, REC_E1:REC_E1 + 1] == te_ref[i].astype(F32)
        gate = jnp.where(mine, info[:, REC_W1:REC_W1 + 1], info[:, REC_W2:REC_W2 + 1])
        hg = jnp.dot(x, wg_bf[...], preferred_element_type=F32)
        hu = jnp.dot(x, wu_bf[...], preferred_element_type=F32)
        hid = (hg * jax.nn.sigmoid(hg)) * hu * gate
        y = jnp.dot(hid.astype(BF16), wd_bf[...], preferred_element_type=F32)
        y_ref[...] = _pack_bf16(y)

    @pl.when(valid <= 0)
    def _():
        y_ref[...] = jnp.zeros_like(y_ref)


def _pack_bf16(x, is_bf16=False):
    w = x.shape[-1] // 2
    bits = lambda a: pltpu.bitcast(a if is_bf16 else a.astype(BF16).astype(F32), jnp.uint32)
    return jnp.bitwise_or(lax.shift_right_logical(bits(x[:, :w]), jnp.uint32(16)),
                          jnp.bitwise_and(bits(x[:, w:]), jnp.uint32(0xFFFF0000)))


def _unpack_bf16(words):
    as_f32 = lambda a: pltpu.bitcast(a, F32)
    return jnp.concatenate([as_f32(lax.shift_left(words, jnp.uint32(16))),
                            as_f32(jnp.bitwise_and(words, jnp.uint32(0xFFFF0000)))], axis=-1).astype(BF16)


def _experts2(xs_a, xs_b, tile_expert, tile_valid, src, w_gate, w_up, w_down, *, te, layer):
    n_tiles = tile_expert.shape[0]
    hbm = pl.BlockSpec(memory_space=pl.ANY)
    grid_spec = pltpu.PrefetchScalarGridSpec(
        num_scalar_prefetch=3,
        grid=(n_tiles,),
        in_specs=[hbm, hbm, hbm, hbm, hbm],
        out_specs=pl.BlockSpec((te, PACKED), lambda i, e, v, s: (i, 0)),
        scratch_shapes=[pltpu.VMEM((2, te, ROW_W), jnp.uint32), pltpu.SemaphoreType.DMA((2,)),
                        pltpu.VMEM((2, D_MODEL, D_EXPERT), F32), pltpu.VMEM((2, D_MODEL, D_EXPERT), F32),
                        pltpu.VMEM((2, D_EXPERT, D_MODEL), F32), pltpu.SemaphoreType.DMA((2,)),
                        pltpu.VMEM((D_MODEL, D_EXPERT), BF16), pltpu.VMEM((D_MODEL, D_EXPERT), BF16),
                        pltpu.VMEM((D_EXPERT, D_MODEL), BF16)])
    return pl.pallas_call(
        functools.partial(_expert_kernel2, rows_a=xs_a.shape[0], layer=layer),
        grid_spec=grid_spec,
        out_shape=jax.ShapeDtypeStruct((n_tiles * te, PACKED), jnp.uint32),
        compiler_params=_cparams(1),
        name="experts",
    )(tile_expert, tile_valid, src.reshape(n_tiles, LIST_LANES), xs_a, xs_b, w_gate, w_up, w_down)


def _combine_kernel2(lst_ref, h_ref, rec_ref, ys_ref, p_ref, gple_ref, wgate_ref, wproj_ref, out_ref,
                     ybuf, sem):
    i = pl.program_id(0)
    n = pl.num_programs(0)
    slot = i % 2
    tm = h_ref.shape[0]
    slots = ybuf.shape[1]

    def fetch(tile, to_slot):
        for c in range(slots // SUBLANES):
            _chunk_copy(ys_ref, lst_ref[tile, c], ybuf.at[to_slot], c, sem.at[to_slot]).start()

    @pl.when(i == 0)
    def _():
        fetch(0, 0)

    @pl.when(i + 1 < n)
    def _():
        fetch(i + 1, 1 - slot)

    pltpu.make_async_copy(ybuf.at[slot], ybuf.at[slot], sem.at[slot]).wait()

    rec = rec_ref[...]
    slot_id = lax.broadcasted_iota(jnp.int32, (tm, slots), 1).astype(F32)
    back = jnp.where(slot_id == rec[:, REC_S1:REC_S1 + 1], 1.0,
                     jnp.where(slot_id == rec[:, REC_S2:REC_S2 + 1], 1.0, 0.0)).astype(BF16)
    h2 = h_ref[...] + jnp.dot(back, _unpack_bf16(ybuf[slot]), preferred_element_type=F32)
    gate = jax.nn.sigmoid(jnp.dot(_rms(h2, gple_ref[...]).astype(BF16), wgate_ref[...], preferred_element_type=F32))
    ple = jnp.dot(p_ref[...].astype(BF16), wproj_ref[...], preferred_element_type=F32)
    out_ref[...] = h2 + ple * gate


def _combine2(h, rec, dst, ys, p, g_ple, wgate_bf, wproj_bf, *, tm, layer):
    n = h.shape[0]
    n_tok = n // tm
    tok = lambda w: pl.BlockSpec((tm, w), lambda i, s: (i, 0))
    full = lambda shape: pl.BlockSpec(shape, lambda i, s: (0,) * len(shape))
    grid_spec = pltpu.PrefetchScalarGridSpec(
        num_scalar_prefetch=1,
        grid=(n_tok,),
        in_specs=[tok(D_MODEL), tok(LANES), pl.BlockSpec(memory_space=pl.ANY),
                  pl.BlockSpec((None, tm, D_PLE), lambda i, s: (layer, i, 0)),
                  full((1, D_MODEL)), full((D_MODEL, D_MODEL)), full((D_PLE, D_MODEL))],
        out_specs=tok(D_MODEL),
        scratch_shapes=[pltpu.VMEM((2, _slots(tm), PACKED), jnp.uint32), pltpu.SemaphoreType.DMA((2,))])
    return pl.pallas_call(
        _combine_kernel2,
        grid_spec=grid_spec,
        out_shape=jax.ShapeDtypeStruct((n, D_MODEL), F32),
        compiler_params=_cparams(1),
        name="combine",
    )(dst.reshape(n_tok, LIST_LANES), h, rec, ys, p, g_ple, wgate_bf, wproj_bf)


def _excl_cumsum(x, axis):
    return jnp.cumsum(x, axis=axis) - x


def _chunk_plan(cnt, tile_row0, *, n_pairs, rows_a, zero_row, te):
    n_tok = cnt.shape[0]
    per_tile = te // SUBLANES
    chunks = (cnt + SUBLANES - 1) // SUBLANES
    run0 = _excl_cumsum(chunks, 1)
    n_run = jnp.sum(chunks, axis=1)
    seg0 = _excl_cumsum(chunks, 0)
    total = jnp.sum(chunks, axis=0)
    region = ((total + per_tile - 1) // per_tile) * per_tile
    reg_end = jnp.cumsum(region)
    reg0 = reg_end - region

    n_tiles = -(-(n_pairs + (SUBLANES - 1) * N_EXPERTS * n_tok) // te) + N_EXPERTS
    t0 = jnp.arange(n_tiles, dtype=jnp.int32) * per_tile
    tile_expert = jnp.minimum(jnp.sum((t0[:, None] >= reg_end[None, :]).astype(jnp.int32), axis=1), N_EXPERTS - 1)
    pick = tile_expert[:, None] == jnp.arange(N_EXPERTS, dtype=jnp.int32)[None, :]
    of_tile = lambda v: jnp.sum(jnp.where(pick, v[None, :], 0), axis=1)
    tile_valid = jnp.clip(of_tile(total) - (t0 - of_tile(reg0)), 0, per_tile).astype(jnp.int32)

    q = (t0 - of_tile(reg0))[:, None] + jnp.arange(per_tile, dtype=jnp.int32)[None, :]
    col_of_tile = lambda v: jnp.sum(jnp.where(pick[:, None, :], v[None, :, :], 0), axis=2)
    seg0_t, seg1_t, run0_t = col_of_tile(seg0), col_of_tile(seg0 + chunks), col_of_tile(run0)
    holds = (q[:, :, None] >= seg0_t[:, None, :]) & (q[:, :, None] < seg1_t[:, None, :])
    local = run0_t[:, None, :] + q[:, :, None] - seg0_t[:, None, :]
    row = jnp.asarray(tile_row0, jnp.int32)[None, None, :] + SUBLANES * local
    src = jnp.sum(jnp.where(holds, row, 0), axis=2)
    src = jnp.where(jnp.any(holds, axis=2), src, zero_row)
    n_first = jnp.sum((src < rows_a).astype(jnp.int32), axis=1)
    idx = jnp.arange(N_EXPERTS, dtype=jnp.int32)
    nonempty = total > 0
    ordinal = _excl_cumsum(nonempty.astype(jnp.int32), 0)
    later = jnp.where((idx[None, :] > idx[:, None]) & nonempty[None, :], idx[None, :], N_EXPERTS)
    following = jnp.min(later, axis=1)
    following = jnp.where(following == N_EXPERTS, -1, following)
    lane = jnp.arange(LIST_LANES, dtype=jnp.int32)[None, :]
    src = jnp.pad(src, ((0, 0), (0, LIST_LANES - per_tile)))
    for at, val in ((LIST_COUNT, n_first), (LIST_NEXT, of_tile(following)), (LIST_PARITY, of_tile(ordinal) % 2)):
        src = jnp.where(lane == at, val[:, None], src)
    src = src.reshape(n_tiles, 1, LIST_LANES).astype(jnp.int32)

    j = jnp.arange(LIST_LANES, dtype=jnp.int32)[None, :, None]
    inside = (j >= run0[:, None, :]) & (j < (run0 + chunks)[:, None, :])
    base = (reg0[None, :] + seg0 - run0)[:, None, :]
    dst = SUBLANES * jnp.sum(jnp.where(inside, base + j, 0), axis=2)
    dst = dst.reshape(n_tok, 1, LIST_LANES).astype(jnp.int32)
    return tile_expert.astype(jnp.int32), tile_valid, src, dst


def _moe_ple2(h_a, h_b, p_a, p_b, g_ffn, w_router, before, sel, w_gate, w_up, w_down, g_ple, wgate_bf, wproj_bf,
              *, tm_a, tm_b, te, layer):
    xs_a, rec_a, cnt_a = _sort(h_a, g_ffn, w_router, before, sel, tm=tm_a)
    xs_b, rec_b, cnt_b = _sort(h_b, g_ffn, w_router, before, sel, tm=tm_b)
    t_a, t_b = cnt_a.shape[0], cnt_b.shape[0]
    cnt = jnp.concatenate([cnt_a[:, 0, :N_EXPERTS], cnt_b[:, 0, :N_EXPERTS]], axis=0).astype(jnp.int32)
    rows_a = xs_a.shape[0]
    row0 = np.concatenate([np.arange(t_a) * _slots(tm_a), rows_a + np.arange(t_b) * _slots(tm_b)])
    tile_expert, tile_valid, src, dst = _chunk_plan(
        cnt, row0, n_pairs=2 * (h_a.shape[0] + h_b.shape[0]), rows_a=rows_a,
        zero_row=rows_a + _slots(tm_b) - SUBLANES, te=te)
    ys = _experts2(xs_a, xs_b, tile_expert, tile_valid, src, w_gate, w_up, w_down, te=te, layer=layer)
    out_a = _combine2(h_a, rec_a, dst[:t_a], ys, p_a, g_ple, wgate_bf, wproj_bf, tm=tm_a, layer=layer)
    out_b = _combine2(h_b, rec_b, dst[t_a:], ys, p_b, g_ple, wgate_bf, wproj_bf, tm=tm_b, layer=layer)
    return out_a, out_b


def _bucket_np(dist):
    max_exact = N_BUCKETS // 2
    d_f = np.maximum(dist, 1).astype(np.float32)
    large = max_exact + (np.log(d_f / np.float32(max_exact)) / np.float32(np.log(MAX_DISTANCE / max_exact))
                         * np.float32(N_BUCKETS - max_exact)).astype(np.int32)
    large = np.minimum(large, N_BUCKETS - 1)
    return np.where(dist < max_exact, dist, large).astype(np.int32)


def _bias_from_buckets(rel_bias, bucket, valid):
    onehot = (jnp.asarray(bucket)[..., None] == jnp.arange(N_BUCKETS, dtype=jnp.int32)).astype(F32)
    bias = jnp.einsum("...k,kh->h...", onehot, rel_bias.astype(F32), precision=HIGHEST)
    return jnp.where(jnp.asarray(valid)[None], bias, NEG)


def _block_order(dil):
    g = np.arange(Q_BLOCK) // SUBLANES
    j = np.arange(Q_BLOCK) % SUBLANES
    if dil == 1:
        return 16 * j + g
    if dil == 4:
        return 32 * (g // 4) + 4 * j + g % 4
    return SUBLANES * g + j


def _band_bias(rel_bias, dil):
    mu = _block_order(dil)
    qi = mu[:, None] + Q_BLOCK
    ki = np.concatenate([mu, mu + Q_BLOCK])[None, :]
    off = qi - ki
    valid = (off >= 0) & (off <= N_KEYS)
    bucket = _bucket_np(dil * np.clip(off, 0, N_KEYS))
    first = valid & (np.arange(2 * Q_BLOCK)[None, :] >= Q_BLOCK)
    tables = [_bias_from_buckets(rel_bias, bucket, v).reshape(N_HEADS * Q_BLOCK, 2 * Q_BLOCK) for v in (valid, first)]
    return jnp.stack(tables)


def _sample_tables(rel_bias, w_buf):
    qpos = w_buf + np.arange(SAMPLE_T)[:, None]
    pos = np.arange(w_buf + NEW_COLS)[None, :]
    dist = qpos - pos
    in_seq = (dist >= 0) & (pos < w_buf + SAMPLE_T)
    mult = np.zeros(dist.shape, np.float32)
    for (w, d) in PATTERNS:
        mult += in_seq & (dist % d == 0) & (dist <= w)
    bucket = _bucket_np(np.maximum(dist, 0))
    bias = jnp.transpose(_bias_from_buckets(rel_bias, bucket, mult > 0), (1, 0, 2))
    rows = SAMPLE_T * N_HEADS
    mult_rows = np.broadcast_to(mult[:, None, :], (SAMPLE_T, N_HEADS, mult.shape[-1]))
    return bias.reshape(rows, -1), jnp.asarray(mult_rows.reshape(rows, -1))


PROJ_TM = 512
PERM_ROWS = 256
OUT_TM = 512
PROMPT_TM = 256
PROMPT_TE = 512
ATTN_SUB = 4


def _row_perm(tm):
    a = np.arange(tm)
    src = (a // Q_BLOCK) * Q_BLOCK + 16 * (a % SUBLANES) + (a % Q_BLOCK) // SUBLANES
    perm = np.zeros((tm, tm), np.float32)
    perm[a, src] = 1.0
    return perm


def kernel(x_prompt, x_sample, cache_k, cache_v, state_conv, p_prompt, p_sample, rel_bias, g_mix, w_in, q_gain,
           k_gain, conv_w, g_out_att, g_out_conv, w_out, g_ffn, w_router_group, w_router_expert, w_gate, w_up,
           w_down, g_ple, w_ple_gate, w_ple_proj):
    depth = w_in.shape[0]
    batch, seq, _ = x_prompt.shape
    dec_b, dec_t, _ = x_sample.shape
    w_buf = cache_k.shape[2]
    n_s = dec_b * dec_t
    keep = min(w_buf, seq)
    assert dec_t == SAMPLE_T and w_buf % LANES == 0
    assert seq % (Q_BLOCK * 16 * 2) == 0 and keep % PROJ_TM == 0 and seq % PROMPT_TM == 0
    cache_kt = jnp.transpose(cache_k, (0, 1, 3, 4, 2))
    cache_vt = jnp.transpose(cache_v, (0, 1, 3, 4, 2))

    row = lambda a: a.reshape(1, -1)
    src_lane = np.arange(LANES)
    expand = jnp.asarray((src_lane[:, None] // LSE_LANES_PER_HEAD == np.arange(ATT_DIM)[None, :] // HEAD_DIM)
                         & (src_lane[:, None] % LSE_LANES_PER_HEAD == 0), BF16)
    before = jnp.asarray(np.arange(LANES)[:, None] < np.arange(LANES)[None, :], BF16)
    sel_np = np.zeros((SUBLANES, LANES), np.float32)
    sel_np[0, REC_S1] = sel_np[1, REC_S2] = 1.0
    sel = jnp.asarray(sel_np, BF16)
    perm = jnp.asarray(_row_perm(PERM_ROWS), BF16)
    unperm = jnp.asarray(_row_perm(OUT_TM).T, BF16)
    band = [_band_bias(rel_bias, d) for (_, d) in PATTERNS]
    s_bias, s_mult = _sample_tables(rel_bias, w_buf)

    hp = x_prompt.reshape(batch * seq, D_MODEL)
    hs = jnp.swapaxes(x_sample, 0, 1).reshape(n_s, D_MODEL)
    pp_all = p_prompt.reshape(depth, batch * seq, D_PLE)
    ps_all = jnp.swapaxes(p_sample, 1, 2).reshape(depth, n_s, D_PLE)
    new = {k: [] for k in ("kp", "vp", "cp", "ks", "vs", "cs")}
    hist_p = jnp.zeros((batch, SUBLANES, CONV_DIM), F32)

    for l in range(depth):
        w_in_bf = w_in[l].astype(BF16)
        wa_bf = w_out[l, :ATT_DIM].astype(BF16)
        wc_bf = w_out[l, ATT_DIM:].astype(BF16)
        wgate_bf = w_ple_gate[l].astype(BF16)
        wproj_bf = w_ple_proj[l].astype(BF16)
        w_router = jnp.concatenate(
            [w_router_expert[l], w_router_group[l],
             jnp.zeros((D_MODEL, LANES - N_EXPERTS - N_GROUPS), F32)], axis=1).astype(BF16)
        qg, kg = row(jnp.tile(q_gain[l], N_HEADS)), row(jnp.tile(k_gain[l], N_HEADS))
        mix = (row(g_mix[l]), w_in_bf, qg, kg, conv_w[l], row(g_out_conv[l]))
        moe = (row(g_ffn[l]), w_router, before, sel, w_gate, w_up, w_down, row(g_ple[l]), wgate_bf, wproj_bf)

        q, k, v, k_nat, v_nat, yn, nconv = _inproj(
            hp, hist_p, *mix, perm, tm=PROJ_TM, shift=1, tiles_per_seq=seq // PROJ_TM,
            keep_tiles=keep // PROJ_TM)
        os, lses = [], []
        for bias, (_, d) in zip(band, PATTERNS):
            sub = min(ATTN_SUB, seq // (Q_BLOCK * d))
            o, lse = _attn_pattern(q, k, v, bias, batch=batch, seq=seq, dil=d, sub=sub,
                                   res=min(d, ATTN_SUB // sub))
            os.append(o)
            lses.append(lse)
        hp = _outproj(os, lses, yn, hp, row(g_out_att[l]), wa_bf, wc_bf, expand, unperm, tm=OUT_TM)
        new["kp"].append(jnp.transpose(k_nat, (0, 3, 1, 2)))
        new["vp"].append(jnp.transpose(v_nat, (0, 3, 1, 2)))
        new["cp"].append(nconv[:, SUBLANES - 2:])

        hist_s = jnp.swapaxes(state_conv[l], 0, 1).reshape(1, 2 * dec_b, CONV_DIM)
        q, k, v, yn, nconv = _inproj(hs, hist_s, *mix, tm=n_s, shift=dec_b, tiles_per_seq=1)
        bmaj = lambda a: jnp.swapaxes(a.reshape(dec_t, dec_b, N_HEADS, HEAD_DIM), 0, 1)
        qb, kb, vb = bmaj(q), bmaj(k), bmaj(v)
        rows8 = lambda a: jnp.pad(a.reshape(dec_b, dec_t, ATT_DIM), ((0, 0), (0, SUBLANES - dec_t), (0, 0)))
        att = _attn_sample(qb.reshape(dec_b, dec_t, ATT_DIM), rows8(kb), rows8(vb), cache_kt, cache_vt, l,
                           s_bias, s_mult)
        att_tm = jnp.swapaxes(att, 0, 1).reshape(n_s, ATT_DIM)
        hs = _outproj([att_tm], [], yn, hs, row(g_out_att[l]), wa_bf, wc_bf, tm=n_s)
        hp, hs = _moe_ple2(hp, hs, pp_all, ps_all, *moe, tm_a=PROMPT_TM, tm_b=n_s, te=PROMPT_TE, layer=l)
        new["ks"].append(kb)
        new["vs"].append(vb)
        new["cs"].append(jnp.swapaxes(nconv.reshape(2, dec_b, CONV_DIM), 0, 1))

    y_prompt = hp.reshape(batch, seq, D_MODEL)
    y_sample = jnp.swapaxes(hs.reshape(dec_t, dec_b, D_MODEL), 0, 1)
    st = lambda key: jnp.stack(new[key])
    return (y_prompt, y_sample, st("kp"), st("vp"), st("cp"), st("ks"), st("vs"), st("cs"))
```

```python
import functools

import jax
import jax.numpy as jnp
import numpy as np
from jax import lax
from jax.experimental import pallas as pl
from jax.experimental.pallas import tpu as pltpu

F32 = jnp.float32
BF16 = jnp.bfloat16
HIGHEST = lax.Precision.HIGHEST

D_MODEL = 1024
HEAD_DIM = 64
N_HEADS = 8
ATT_DIM = N_HEADS * HEAD_DIM
CONV_DIM = D_MODEL - ATT_DIM
MIX_IN = 3 * ATT_DIM + 3 * CONV_DIM
PATTERNS = ((128, 1), (512, 4), (2048, 16))
N_KEYS = 128
Q_BLOCK = 128
N_BUCKETS = 32
MAX_DISTANCE = 2048
N_GROUPS = 4
EXPERTS_PER_GROUP = 8
N_EXPERTS = N_GROUPS * EXPERTS_PER_GROUP
D_EXPERT = 256
D_PLE = 256
EPS = 1e-6
NEG = -1e30

LANES = 128
SUBLANES = 8
SLABS = Q_BLOCK // SUBLANES
LSE_LANES_PER_HEAD = LANES // N_HEADS
VMEM_LIMIT = 56 * 1024 * 1024
NT = (((1,), (1,)), ((), ()))


def _cparams(n_axes):
    return pltpu.CompilerParams(dimension_semantics=("arbitrary",) * n_axes,
                                vmem_limit_bytes=VMEM_LIMIT)


def _full(shape):
    n = len(shape)
    return pl.BlockSpec(shape, lambda *_: (0,) * n)


def _rms(x, gain):
    ms = jnp.mean(x * x, axis=-1, keepdims=True)
    return x * lax.rsqrt(ms + EPS) * gain


def _inproj_kernel(*refs, shift, tiles_per_seq, permute):
    (h_ref, gmix_ref, w_ref, qg_ref, kg_ref, cw_ref, gconv_ref, hist_ref) = refs[:8]
    if permute:
        perm_ref, q_ref, k_ref, v_ref, kn_ref, vn_ref, yn_ref, nconv_ref, carry_ref = refs[8:]
    else:
        q_ref, k_ref, v_ref, yn_ref, nconv_ref, carry_ref = refs[8:]
    i = pl.program_id(0)
    a = _rms(h_ref[...], gmix_ref[...])
    proj = jnp.dot(a.astype(BF16), w_ref[...], preferred_element_type=F32)
    tm = proj.shape[0]

    lower = lax.broadcasted_iota(jnp.int32, (tm, LANES), 1) < HEAD_DIM

    def head_norm(t, g):
        out = []
        for j in range(ATT_DIM // LANES):
            blk = t[:, j * LANES:(j + 1) * LANES]
            sq = blk * blk
            ms_lo = jnp.sum(jnp.where(lower, sq, 0.0), axis=-1, keepdims=True) * (1.0 / HEAD_DIM)
            ms_hi = jnp.sum(jnp.where(lower, 0.0, sq), axis=-1, keepdims=True) * (1.0 / HEAD_DIM)
            scale = jnp.where(lower, lax.rsqrt(ms_lo + EPS), lax.rsqrt(ms_hi + EPS))
            out.append(blk * scale)
        return jnp.concatenate(out, axis=-1) * g

    q = head_norm(proj[:, 0:ATT_DIM], qg_ref[...])
    k = head_norm(proj[:, ATT_DIM:2 * ATT_DIM], kg_ref[...])
    v = proj[:, 2 * ATT_DIM:3 * ATT_DIM]
    if permute:
        kn_ref[...] = k.T.reshape(N_HEADS, HEAD_DIM, tm)
        vn_ref[...] = v.T.reshape(N_HEADS, HEAD_DIM, tm)
        qkv = jnp.concatenate([q * (HEAD_DIM ** -0.5), k, v], axis=-1).astype(BF16)
        pm = perm_ref.shape[0]
        moved = jnp.concatenate([jnp.dot(perm_ref[...], qkv[r:r + pm], preferred_element_type=F32)
                                 for r in range(0, tm, pm)], axis=0)
        q_ref[...] = moved[:, 0:ATT_DIM]
        k_ref[...] = moved[:, ATT_DIM:2 * ATT_DIM]
        v_ref[...] = moved[:, 2 * ATT_DIM:3 * ATT_DIM]
    else:
        q_ref[...] = q
        k_ref[...] = k
        v_ref[...] = v
    c0 = 3 * ATT_DIM
    hc = proj[:, c0:c0 + CONV_DIM]
    gb = proj[:, c0 + CONV_DIM:c0 + 2 * CONV_DIM]
    gc = proj[:, c0 + 2 * CONV_DIM:c0 + 3 * CONV_DIM]
    u = gc * hc

    if shift == 1:
        @pl.when(i % tiles_per_seq == 0)
        def _():
            carry_ref[...] = hist_ref[0]
        h0 = carry_ref[SUBLANES - 2:SUBLANES - 1, :]
        h1 = carry_ref[SUBLANES - 1:SUBLANES, :]
        row = lax.broadcasted_iota(jnp.int32, (tm, 1), 0)
        u1 = jnp.where(row == 0, h1, pltpu.roll(u, 1, 0))
        u2 = jnp.where(row == 0, h0, jnp.where(row == 1, h1, pltpu.roll(u, 2, 0)))
        carry_ref[...] = u[tm - SUBLANES:tm, :]
        nconv_ref[0] = u[tm - SUBLANES:tm, :]
    else:
        hist = hist_ref[0]
        u1 = jnp.concatenate([hist[shift:2 * shift], u[0:tm - shift]], axis=0)
        u2 = jnp.concatenate([hist, u[0:tm - 2 * shift]], axis=0)
        nconv_ref[0] = u[tm - 2 * shift:tm, :]
    conv = cw_ref[0:1, :] * u2 + cw_ref[1:2, :] * u1 + cw_ref[2:3, :] * u
    yn_ref[...] = _rms(gb * conv, gconv_ref[...])


def _inproj(h, hist, g_mix, w_in_bf, q_gain, k_gain, conv_w, g_out_conv, perm=None, *,
            tm, shift, tiles_per_seq, keep_tiles=0):
    n = h.shape[0]
    hist_rows = hist.shape[1]
    nseq = hist.shape[0]
    tok = lambda w: pl.BlockSpec((tm, w), lambda i: (i, 0))
    seq3 = lambda r: pl.BlockSpec((1, r, CONV_DIM), lambda i: (i // tiles_per_seq, 0, 0))
    nconv_rows = SUBLANES if shift == 1 else 2 * shift
    att = jax.ShapeDtypeStruct((n, ATT_DIM), F32)
    in_specs = [tok(D_MODEL), _full((1, D_MODEL)), _full((D_MODEL, MIX_IN)), _full((1, ATT_DIM)),
                _full((1, ATT_DIM)), _full((3, CONV_DIM)), _full((1, CONV_DIM)), seq3(hist_rows)]
    args = [h, g_mix, w_in_bf, q_gain, k_gain, conv_w, g_out_conv, hist]
    out_specs = [tok(ATT_DIM)] * 3
    out_shape = [att] * 3
    if perm is not None:
        in_specs.append(_full(perm.shape))
        args.append(perm)
        first = tiles_per_seq - keep_tiles
        kept = pl.BlockSpec((None, N_HEADS, HEAD_DIM, tm),
                            lambda i: (i // tiles_per_seq, 0, 0, jnp.maximum(i % tiles_per_seq - first, 0)))
        out_specs += [kept, kept]
        out_shape += [jax.ShapeDtypeStruct((nseq, N_HEADS, HEAD_DIM, keep_tiles * tm), F32)] * 2
    out_specs += [tok(CONV_DIM), seq3(nconv_rows)]
    out_shape += [jax.ShapeDtypeStruct((n, CONV_DIM), F32),
                  jax.ShapeDtypeStruct((nseq, nconv_rows, CONV_DIM), F32)]
    return pl.pallas_call(
        functools.partial(_inproj_kernel, shift=shift, tiles_per_seq=tiles_per_seq, permute=perm is not None),
        grid=(n // tm,),
        in_specs=in_specs, out_specs=out_specs, out_shape=out_shape,
        scratch_shapes=[pltpu.VMEM((SUBLANES, CONV_DIM), F32)],
        compiler_params=_cparams(1),
        name="inproj",
    )(*args)


def _attn_block(q, kk, vv, bias):
    lane = lax.broadcasted_iota(jnp.int32, (Q_BLOCK, LANES), 1)
    upper = lane >= HEAD_DIM
    scores = []
    for h in range(N_HEADS):
        j, e = divmod(h, 2)
        qp = q[:, j * LANES:(j + 1) * LANES]
        qm = (jnp.where(upper, qp, 0.0) if e else jnp.where(upper, 0.0, qp)).astype(BF16)
        scores.append(lax.dot_general(qm, kk[:, j * LANES:(j + 1) * LANES], NT, preferred_element_type=F32))
    s = jnp.concatenate(scores, axis=0) + bias
    m = jnp.max(s, axis=-1, keepdims=True)
    p = jnp.exp(s - m)
    den = jnp.sum(p, axis=-1, keepdims=True)
    pb = p.astype(BF16)
    head_rows = lambda t, h: t[h * Q_BLOCK:(h + 1) * Q_BLOCK]
    lse_grp = lane // LSE_LANES_PER_HEAD
    m_tile = jnp.zeros((Q_BLOCK, LANES), F32)
    den_tile = jnp.ones((Q_BLOCK, LANES), F32)
    outs = []
    for j in range(N_HEADS // 2):
        even, odd = 2 * j, 2 * j + 1
        vp = vv[:, j * LANES:(j + 1) * LANES]
        pair = jnp.where(upper, jnp.dot(head_rows(pb, odd), vp, preferred_element_type=F32),
                         jnp.dot(head_rows(pb, even), vp, preferred_element_type=F32))
        outs.append(pair / jnp.where(upper, head_rows(den, odd), head_rows(den, even)))
        for h in (even, odd):
            m_tile = jnp.where(lse_grp == h, head_rows(m, h), m_tile)
            den_tile = jnp.where(lse_grp == h, head_rows(den, h), den_tile)
    return jnp.concatenate(outs, axis=-1), m_tile + jnp.log(den_tile)


def _attn_kernel(q_ref, kp_ref, kc_ref, vp_ref, vc_ref, bias_ref, o_ref, lse_ref, kbuf, vbuf, *, sub, res):
    n = pl.program_id(2)
    rows = sub * Q_BLOCK
    for r in range(res):
        kbuf[0:Q_BLOCK, :] = kp_ref[:, r].reshape(Q_BLOCK, ATT_DIM).astype(BF16)
        kbuf[Q_BLOCK:, :] = kc_ref[:, :, r].reshape(rows, ATT_DIM).astype(BF16)
        vbuf[0:Q_BLOCK, :] = vp_ref[:, r].reshape(Q_BLOCK, ATT_DIM).astype(BF16)
        vbuf[Q_BLOCK:, :] = vc_ref[:, :, r].reshape(rows, ATT_DIM).astype(BF16)
        for j in range(sub):
            q = q_ref[j, :, r].reshape(Q_BLOCK, ATT_DIM)
            r0 = j * Q_BLOCK
            first = (n == 0).astype(jnp.int32) if j == 0 else 0
            o, lse = _attn_block(q, kbuf[r0:r0 + 2 * Q_BLOCK, :], vbuf[r0:r0 + 2 * Q_BLOCK, :], bias_ref[first])
            o_ref[j, :, r] = o.reshape(SLABS, SUBLANES, ATT_DIM)
            lse_ref[j, :, r] = lse.reshape(SLABS, SUBLANES, LANES)


def _attn_pattern(q, k, v, bias, *, batch, seq, dil, sub, res):
    nblk = seq // (Q_BLOCK * dil)
    view = lambda t: t.reshape(batch, nblk, SLABS, dil, SUBLANES, t.shape[-1])
    cur = lambda c: pl.BlockSpec((None, sub, SLABS, res, SUBLANES, c), lambda b, r, n: (b, n, 0, r, 0, 0))
    prev = pl.BlockSpec((None, None, SLABS, res, SUBLANES, ATT_DIM),
                        lambda b, r, n: (b, jnp.maximum(n * sub - 1, 0), 0, r, 0, 0))
    o, lse = pl.pallas_call(
        functools.partial(_attn_kernel, sub=sub, res=res),
        grid=(batch, dil // res, nblk // sub),
        in_specs=[cur(ATT_DIM), prev, cur(ATT_DIM), prev, cur(ATT_DIM),
                  _full((2, N_HEADS * Q_BLOCK, 2 * Q_BLOCK))],
        out_specs=[cur(ATT_DIM), cur(LANES)],
        out_shape=[jax.ShapeDtypeStruct((batch, nblk, SLABS, dil, SUBLANES, ATT_DIM), F32),
                   jax.ShapeDtypeStruct((batch, nblk, SLABS, dil, SUBLANES, LANES), F32)],
        scratch_shapes=[pltpu.VMEM(((sub + 1) * Q_BLOCK, ATT_DIM), BF16),
                        pltpu.VMEM(((sub + 1) * Q_BLOCK, ATT_DIM), BF16)],
        compiler_params=_cparams(3),
        name=f"attn_d{dil}",
    )(view(q), view(k), view(k), view(v), view(v), bias)
    return o.reshape(batch * seq, ATT_DIM), lse.reshape(batch * seq, LANES)


SAMPLE_T = 4
NEW_COLS = LANES


def _attn_sample_kernel(q_ref, kt_ref, kn_ref, vt_ref, vn_ref, bias_ref, mult_ref, o_ref):
    rows = SAMPLE_T * N_HEADS
    q4 = q_ref[...] * (HEAD_DIM ** -0.5)
    qt = jnp.concatenate([jnp.broadcast_to(q4[t:t + 1, :], (N_HEADS, ATT_DIM)) for t in range(SAMPLE_T)], axis=0)
    lane_head = lax.broadcasted_iota(jnp.int32, (rows, ATT_DIM), 1) // HEAD_DIM
    row_head = lax.broadcasted_iota(jnp.int32, (rows, ATT_DIM), 0) % N_HEADS
    own = lane_head == row_head
    qbd = jnp.where(own, qt, 0.0).astype(BF16)
    flat = lambda ref: ref[...].reshape(ATT_DIM, ref.shape[-1]).astype(BF16)
    pad = jnp.zeros((NEW_COLS - SUBLANES, ATT_DIM), F32)
    new_rows = lambda ref: jnp.concatenate([ref[...], pad], axis=0).astype(BF16)
    s = jnp.concatenate([jnp.dot(qbd, flat(kt_ref), preferred_element_type=F32),
                         lax.dot_general(qbd, new_rows(kn_ref), NT, preferred_element_type=F32)],
                        axis=-1) + bias_ref[...]
    m = jnp.max(s, axis=-1, keepdims=True)
    p = jnp.exp(s - m) * mult_ref[...]
    den = jnp.sum(p, axis=-1, keepdims=True)
    pb = p.astype(BF16)
    w_buf = kt_ref.shape[-1]
    acc = (lax.dot_general(pb[:, :w_buf], flat(vt_ref), NT, preferred_element_type=F32)
           + jnp.dot(pb[:, w_buf:], new_rows(vn_ref), preferred_element_type=F32))
    acc = jnp.where(own, acc / den, 0.0)
    for t in range(SAMPLE_T):
        o_ref[t:t + 1, :] = jnp.sum(acc[t * N_HEADS:(t + 1) * N_HEADS, :], axis=0, keepdims=True)


def _attn_sample(q, k_new, v_new, cache_kt, cache_vt, layer, bias, mult):
    nb = q.shape[0]
    w_buf = cache_kt.shape[-1]
    tok = pl.BlockSpec((None, SAMPLE_T, ATT_DIM), lambda b: (b, 0, 0))
    new = pl.BlockSpec((None, SUBLANES, ATT_DIM), lambda b: (b, 0, 0))
    old = pl.BlockSpec((None, None, N_HEADS, HEAD_DIM, w_buf), lambda b: (layer, b, 0, 0, 0))
    tbl = _full((SAMPLE_T * N_HEADS, w_buf + NEW_COLS))
    return pl.pallas_call(
        _attn_sample_kernel,
        grid=(nb,),
        in_specs=[tok, old, new, old, new, tbl, tbl],
        out_specs=tok,
        out_shape=jax.ShapeDtypeStruct((nb, SAMPLE_T, ATT_DIM), F32),
        compiler_params=_cparams(1),
        name="attn_sample",
    )(q, cache_kt, k_new, cache_vt, v_new, bias, mult)


def _split_dot(x, e_ref):
    hi = x.astype(BF16)
    lo = (x - hi.astype(F32)).astype(BF16)
    return (jnp.dot(hi, e_ref[...], preferred_element_type=F32)
            + jnp.dot(lo, e_ref[...], preferred_element_type=F32))


def _outproj_kernel(*refs, n_pat):
    mix = n_pat > 1
    n_lse = n_pat if mix else 0
    o_refs = refs[0:n_pat]
    l_refs = refs[n_pat:n_pat + n_lse]
    rest = refs[n_pat + n_lse:]
    if mix:
        yn_ref, h_ref, gatt_ref, exp_ref, unperm_ref, wa_ref, wc_ref, out_ref = rest
        lses = [r[...] for r in l_refs]
        top = functools.reduce(jnp.maximum, lses)
        ws = [jnp.exp(l - top) for l in lses]
        tot = functools.reduce(lambda a, b: a + b, ws)
        att = None
        for w, o_ref in zip(ws, o_refs):
            term = _split_dot(w / tot, exp_ref) * o_ref[...]
            att = term if att is None else att + term
        att_bf = jnp.dot(unperm_ref[...], _rms(att, gatt_ref[...]).astype(BF16),
                         preferred_element_type=F32).astype(BF16)
    else:
        yn_ref, h_ref, gatt_ref, wa_ref, wc_ref, out_ref = rest
        att_bf = _rms(o_refs[0][...], gatt_ref[...]).astype(BF16)
    y = (jnp.dot(att_bf, wa_ref[...], preferred_element_type=F32)
         + jnp.dot(yn_ref[...].astype(BF16), wc_ref[...], preferred_element_type=F32))
    out_ref[...] = h_ref[...] + y


def _outproj(os, lses, yn, h, g_att, wa_bf, wc_bf, expand=None, unperm=None, *, tm):
    n = h.shape[0]
    n_pat = len(os)
    tok = lambda w: pl.BlockSpec((tm, w), lambda i: (i, 0))
    in_specs = [tok(ATT_DIM)] * n_pat + [tok(LANES)] * len(lses) + [tok(CONV_DIM), tok(D_MODEL), _full((1, ATT_DIM))]
    args = [*os, *lses, yn, h, g_att]
    if n_pat > 1:
        in_specs += [_full((LANES, ATT_DIM)), _full((tm, tm))]
        args += [expand, unperm]
    in_specs += [_full((ATT_DIM, D_MODEL)), _full((CONV_DIM, D_MODEL))]
    args += [wa_bf, wc_bf]
    return pl.pallas_call(
        functools.partial(_outproj_kernel, n_pat=n_pat),
        grid=(n // tm,),
        in_specs=in_specs,
        out_specs=tok(D_MODEL),
        out_shape=jax.ShapeDtypeStruct((n, D_MODEL), F32),
        compiler_params=_cparams(1),
        name="outproj",
    )(*args)


GROUP_LANE0 = N_EXPERTS
REC_E1, REC_E2, REC_S1, REC_S2, REC_W1, REC_W2 = range(6)
PACKED = D_MODEL // 2
ROW_W = PACKED + LANES
LIST_LANES = LANES
LIST_COUNT = LIST_LANES - 1
LIST_NEXT = LIST_LANES - 2
LIST_PARITY = LIST_LANES - 3


def _slots(tm):
    need = 2 * tm + (SUBLANES - 1) * N_EXPERTS + SUBLANES
    return -(-need // LANES) * LANES


def _pieces(x):
    hi = x.astype(BF16)
    r1 = x - hi.astype(F32)
    mid = r1.astype(BF16)
    return hi, mid, (r1 - mid.astype(F32)).astype(BF16)


def _sort_kernel(h_ref, g_ref, wr_ref, before_ref, sel_ref, xs_ref, rec_ref, cnt_ref, *, group):
    tm = h_ref.shape[0] // group
    slots = xs_ref.shape[0] // group
    for t in range(group):
        xs, rec, cnt = _sort_tile(h_ref[t * tm:(t + 1) * tm, :], g_ref, wr_ref, before_ref, sel_ref, slots)
        xs_ref[t * slots:(t + 1) * slots, :] = xs
        rec_ref[t * tm:(t + 1) * tm, :] = rec
        cnt_ref[t] = cnt


def _sort_tile(h, g_ref, wr_ref, before_ref, sel_ref, slots):
    m = _rms(h, g_ref[...])
    logits = jnp.dot(m.astype(BF16), wr_ref[...], preferred_element_type=F32)
    tm = logits.shape[0]
    lane_i = lax.broadcasted_iota(jnp.int32, (tm, LANES), 1)
    lane = lane_i.astype(F32)
    big = jnp.float32(4 * LANES)

    is_g = jnp.logical_and(lane_i >= GROUP_LANE0, lane_i < GROUP_LANE0 + N_GROUPS)
    gl = jnp.where(is_g, logits, NEG)
    gmax = jnp.max(gl, axis=-1, keepdims=True)
    g_w = 1.0 / jnp.sum(jnp.where(is_g, jnp.exp(gl - gmax), 0.0), axis=-1, keepdims=True)
    g_sel = jnp.min(jnp.where(gl == gmax, lane - GROUP_LANE0, big), axis=-1, keepdims=True)

    grp_of_lane = (lane_i // EXPERTS_PER_GROUP).astype(F32)
    in_grp = jnp.logical_and(lane_i < N_EXPERTS, grp_of_lane == g_sel)
    el = jnp.where(in_grp, logits, NEG)
    t1 = jnp.max(el, axis=-1, keepdims=True)
    e1 = jnp.min(jnp.where(el == t1, lane, big), axis=-1, keepdims=True)
    el2 = jnp.where(lane == e1, NEG, el)
    t2 = jnp.max(el2, axis=-1, keepdims=True)
    e2 = jnp.min(jnp.where(el2 == t2, lane, big), axis=-1, keepdims=True)
    ex = jnp.exp(t2 - t1)
    w1 = g_w / (1.0 + ex)
    w2 = g_w * ex / (1.0 + ex)

    hit1 = lane == e1
    hit2 = lane == e2
    c = jnp.where(hit1, 1.0, jnp.where(hit2, 1.0, 0.0))
    rr = lax.broadcasted_iota(jnp.int32, (tm, tm), 0)
    cc = lax.broadcasted_iota(jnp.int32, (tm, tm), 1)
    lower = jnp.where(rr > cc, 1.0, 0.0).astype(BF16)
    rank = jnp.dot(lower, c.astype(BF16), preferred_element_type=F32)
    cnt = jnp.sum(c, axis=0, keepdims=True)
    chunks = jnp.floor((cnt + (SUBLANES - 1)) * (1.0 / SUBLANES))
    start = SUBLANES * jnp.dot(jnp.broadcast_to(chunks, (SUBLANES, LANES)).astype(BF16), before_ref[...],
                               preferred_element_type=F32)[0:1, :]
    slot_of = rank + start
    s1 = jnp.sum(jnp.where(hit1, slot_of, 0.0), axis=-1, keepdims=True)
    s2 = jnp.sum(jnp.where(hit2, slot_of, 0.0), axis=-1, keepdims=True)

    rec = jnp.zeros((tm, LANES), F32)
    for idx, val in ((REC_E1, e1), (REC_E2, e2), (REC_S1, s1), (REC_S2, s2), (REC_W1, w1), (REC_W2, w2)):
        rec = jnp.where(lane_i == idx, val, rec)

    rec_parts = _pieces(rec)[:2]
    srow = sum(lax.dot_general(sel_ref[...], part, NT, preferred_element_type=F32) for part in rec_parts)
    slot_id = lax.broadcasted_iota(jnp.int32, (slots, tm), 0).astype(F32)
    place = jnp.where(slot_id == srow[0:1, :], 1.0, jnp.where(slot_id == srow[1:2, :], 1.0, 0.0)).astype(BF16)
    payload = jnp.concatenate([m.astype(BF16), *rec_parts], axis=-1)
    moved = jnp.dot(place, payload, preferred_element_type=F32)
    info = moved[:, D_MODEL:D_MODEL + LANES] + moved[:, D_MODEL + LANES:]
    xs = jnp.concatenate([_pack_bf16(moved[:, :D_MODEL], is_bf16=True), pltpu.bitcast(info, jnp.uint32)], axis=-1)
    return xs, rec, jnp.broadcast_to(cnt, (SUBLANES, LANES))


def _sort(h, g_ffn, w_router, before, sel, *, tm):
    n = h.shape[0]
    slots = _slots(tm)
    group = 2 if (n // tm) % 2 == 0 else 1
    tok = lambda w: pl.BlockSpec((group * tm, w), lambda i: (i, 0))
    return pl.pallas_call(
        functools.partial(_sort_kernel, group=group),
        grid=(n // (group * tm),),
        in_specs=[tok(D_MODEL), _full((1, D_MODEL)), _full((D_MODEL, LANES)), _full((LANES, LANES)),
                  _full((SUBLANES, LANES))],
        out_specs=[pl.BlockSpec((group * slots, ROW_W), lambda i: (i, 0)), tok(LANES),
                   pl.BlockSpec((group, SUBLANES, LANES), lambda i: (i, 0, 0))],
        out_shape=[jax.ShapeDtypeStruct((n // tm * slots, ROW_W), jnp.uint32), jax.ShapeDtypeStruct((n, LANES), F32),
                   jax.ShapeDtypeStruct((n // tm, SUBLANES, LANES), F32)],
        compiler_params=_cparams(1),
        name="moe_sort",
    )(h, g_ffn, w_router, before, sel)


def _chunk_copy(src_hbm, row, dst, c, sem):
    if not isinstance(row, int):
        row = pl.multiple_of(row, SUBLANES)
    to = c * SUBLANES
    if not isinstance(to, int):
        to = pl.multiple_of(to, SUBLANES)
    return pltpu.make_async_copy(src_hbm.at[pl.ds(row, SUBLANES), :], dst.at[pl.ds(to, SUBLANES), :], sem)


def _expert_kernel(te_ref, tv_ref, lst_ref, xa_ref, xb_ref, wg_ref, wu_ref, wd_ref, y_ref,
                    xbuf, sem, wg_st, wu_st, wd_st, wsem, wg_bf, wu_bf, wd_bf, *, rows_a, layer):
    i = pl.program_id(0)
    n = pl.num_programs(0)
    te = xbuf.shape[1]
    slot = i % 2

    def fetch(tile, to_slot):
        n_a = lst_ref[tile, LIST_COUNT]

        def from_a(c, carry):
            _chunk_copy(xa_ref, lst_ref[tile, c], xbuf.at[to_slot], c, sem.at[to_slot]).start()
            return carry

        def from_b(c, carry):
            _chunk_copy(xb_ref, lst_ref[tile, c] - rows_a, xbuf.at[to_slot], c, sem.at[to_slot]).start()
            return carry

        per_tile = te // SUBLANES

        @pl.when(n_a == per_tile)
        def _():
            for c in range(per_tile):
                from_a(c, 0)

        @pl.when(n_a != per_tile)
        def _():
            lax.fori_loop(0, n_a, from_a, 0)
            lax.fori_loop(n_a, per_tile, from_b, 0)

    @pl.when(jnp.logical_and(i == 0, tv_ref[0] > 0))
    def _():
        fetch(0, 0)

    nxt = jnp.minimum(i + 1, n - 1)

    @pl.when(jnp.logical_and(i + 1 < n, tv_ref[nxt] > 0))
    def _():
        fetch(nxt, 1 - slot)

    valid = tv_ref[i]
    changed = jnp.logical_or(i == 0, te_ref[i] != te_ref[jnp.maximum(i - 1, 0)])

    def weight_copies(expert, s):
        return [pltpu.make_async_copy(src.at[layer, expert], dst.at[s], wsem.at[s])
                for src, dst in ((wg_ref, wg_st), (wu_ref, wu_st), (wd_ref, wd_st))]

    @pl.when(jnp.logical_and(changed, valid > 0))
    def _():
        s = lst_ref[i, LIST_PARITY]

        @pl.when(i == 0)
        def _():
            for cp in weight_copies(te_ref[0], 0):
                cp.start()

        for cp in weight_copies(te_ref[i], s):
            cp.wait()
        wg_bf[...] = wg_st[s].astype(BF16)
        wu_bf[...] = wu_st[s].astype(BF16)
        wd_bf[...] = wd_st[s].astype(BF16)
        nxt_expert = lst_ref[i, LIST_NEXT]

        @pl.when(nxt_expert >= 0)
        def _():
            for cp in weight_copies(nxt_expert, 1 - s):
                cp.start()

    @pl.when(valid > 0)
    def _():
        pltpu.make_async_copy(xbuf.at[slot], xbuf.at[slot], sem.at[slot]).wait()
        rows = xbuf[slot]
        x = _unpack_bf16(rows[:, :PACKED])
        info = pltpu.bitcast(rows[:, PACKED:], F32)
        mine = info[:, REC_E1:REC_E1 + 1] == te_ref[i].astype(F32)
        gate = jnp.where(mine, info[:, REC_W1:REC_W1 + 1], info[:, REC_W2:REC_W2 + 1])
        hg = jnp.dot(x, wg_bf[...], preferred_element_type=F32)
        hu = jnp.dot(x, wu_bf[...], preferred_element_type=F32)
        hid = (hg * jax.nn.sigmoid(hg)) * hu * gate
        y = jnp.dot(hid.astype(BF16), wd_bf[...], preferred_element_type=F32)
        y_ref[...] = _pack_bf16(y)

    @pl.when(valid <= 0)
    def _():
        y_ref[...] = jnp.zeros_like(y_ref)


def _pack_bf16(x, is_bf16=False):
    w = x.shape[-1] // 2
    bits = lambda a: pltpu.bitcast(a if is_bf16 else a.astype(BF16).astype(F32), jnp.uint32)
    return jnp.bitwise_or(lax.shift_right_logical(bits(x[:, :w]), jnp.uint32(16)),
                          jnp.bitwise_and(bits(x[:, w:]), jnp.uint32(0xFFFF0000)))


def _unpack_bf16(words):
    as_f32 = lambda a: pltpu.bitcast(a, F32)
    return jnp.concatenate([as_f32(lax.shift_left(words, jnp.uint32(16))),
                            as_f32(jnp.bitwise_and(words, jnp.uint32(0xFFFF0000)))], axis=-1).astype(BF16)


def _experts(xs_a, xs_b, tile_expert, tile_valid, src, w_gate, w_up, w_down, *, te, layer):
    n_tiles = tile_expert.shape[0]
    hbm = pl.BlockSpec(memory_space=pl.ANY)
    grid_spec = pltpu.PrefetchScalarGridSpec(
        num_scalar_prefetch=3,
        grid=(n_tiles,),
        in_specs=[hbm, hbm, hbm, hbm, hbm],
        out_specs=pl.BlockSpec((te, PACKED), lambda i, e, v, s: (i, 0)),
        scratch_shapes=[pltpu.VMEM((2, te, ROW_W), jnp.uint32), pltpu.SemaphoreType.DMA((2,)),
                        pltpu.VMEM((2, D_MODEL, D_EXPERT), F32), pltpu.VMEM((2, D_MODEL, D_EXPERT), F32),
                        pltpu.VMEM((2, D_EXPERT, D_MODEL), F32), pltpu.SemaphoreType.DMA((2,)),
                        pltpu.VMEM((D_MODEL, D_EXPERT), BF16), pltpu.VMEM((D_MODEL, D_EXPERT), BF16),
                        pltpu.VMEM((D_EXPERT, D_MODEL), BF16)])
    return pl.pallas_call(
        functools.partial(_expert_kernel, rows_a=xs_a.shape[0], layer=layer),
        grid_spec=grid_spec,
        out_shape=jax.ShapeDtypeStruct((n_tiles * te, PACKED), jnp.uint32),
        compiler_params=_cparams(1),
        name="experts",
    )(tile_expert, tile_valid, src.reshape(n_tiles, LIST_LANES), xs_a, xs_b, w_gate, w_up, w_down)


def _combine_kernel(lst_ref, h_ref, rec_ref, ys_ref, p_ref, gple_ref, wgate_ref, wproj_ref, out_ref,
                     ybuf, sem):
    i = pl.program_id(0)
    n = pl.num_programs(0)
    slot = i % 2
    tm = h_ref.shape[0]
    slots = ybuf.shape[1]

    def fetch(tile, to_slot):
        for c in range(slots // SUBLANES):
            _chunk_copy(ys_ref, lst_ref[tile, c], ybuf.at[to_slot], c, sem.at[to_slot]).start()

    @pl.when(i == 0)
    def _():
        fetch(0, 0)

    @pl.when(i + 1 < n)
    def _():
        fetch(i + 1, 1 - slot)

    pltpu.make_async_copy(ybuf.at[slot], ybuf.at[slot], sem.at[slot]).wait()

    rec = rec_ref[...]
    slot_id = lax.broadcasted_iota(jnp.int32, (tm, slots), 1).astype(F32)
    back = jnp.where(slot_id == rec[:, REC_S1:REC_S1 + 1], 1.0,
                     jnp.where(slot_id == rec[:, REC_S2:REC_S2 + 1], 1.0, 0.0)).astype(BF16)
    h2 = h_ref[...] + jnp.dot(back, _unpack_bf16(ybuf[slot]), preferred_element_type=F32)
    gate = jax.nn.sigmoid(jnp.dot(_rms(h2, gple_ref[...]).astype(BF16), wgate_ref[...], preferred_element_type=F32))
    ple = jnp.dot(p_ref[...].astype(BF16), wproj_ref[...], preferred_element_type=F32)
    out_ref[...] = h2 + ple * gate


def _combine(h, rec, dst, ys, p, g_ple, wgate_bf, wproj_bf, *, tm, layer):
    n = h.shape[0]
    n_tok = n // tm
    tok = lambda w: pl.BlockSpec((tm, w), lambda i, s: (i, 0))
    full = lambda shape: pl.BlockSpec(shape, lambda i, s: (0,) * len(shape))
    grid_spec = pltpu.PrefetchScalarGridSpec(
        num_scalar_prefetch=1,
        grid=(n_tok,),
        in_specs=[tok(D_MODEL), tok(LANES), pl.BlockSpec(memory_space=pl.ANY),
                  pl.BlockSpec((None, tm, D_PLE), lambda i, s: (layer, i, 0)),
                  full((1, D_MODEL)), full((D_MODEL, D_MODEL)), full((D_PLE, D_MODEL))],
        out_specs=tok(D_MODEL),
        scratch_shapes=[pltpu.VMEM((2, _slots(tm), PACKED), jnp.uint32), pltpu.SemaphoreType.DMA((2,))])
    return pl.pallas_call(
        _combine_kernel,
        grid_spec=grid_spec,
        out_shape=jax.ShapeDtypeStruct((n, D_MODEL), F32),
        compiler_params=_cparams(1),
        name="combine",
    )(dst.reshape(n_tok, LIST_LANES), h, rec, ys, p, g_ple, wgate_bf, wproj_bf)


def _excl_cumsum(x, axis):
    return jnp.cumsum(x, axis=axis) - x


def _chunk_plan(cnt, tile_row0, *, n_pairs, rows_a, zero_row, te):
    n_tok = cnt.shape[0]
    per_tile = te // SUBLANES
    chunks = (cnt + SUBLANES - 1) // SUBLANES
    run0 = _excl_cumsum(chunks, 1)
    seg0 = _excl_cumsum(chunks, 0)
    total = jnp.sum(chunks, axis=0)
    region = ((total + per_tile - 1) // per_tile) * per_tile
    reg_end = jnp.cumsum(region)
    reg0 = reg_end - region

    n_tiles = -(-(n_pairs + (SUBLANES - 1) * N_EXPERTS * n_tok) // te) + N_EXPERTS
    t0 = jnp.arange(n_tiles, dtype=jnp.int32) * per_tile
    tile_expert = jnp.minimum(jnp.sum((t0[:, None] >= reg_end[None, :]).astype(jnp.int32), axis=1), N_EXPERTS - 1)
    pick = tile_expert[:, None] == jnp.arange(N_EXPERTS, dtype=jnp.int32)[None, :]
    of_tile = lambda v: jnp.sum(jnp.where(pick, v[None, :], 0), axis=1)
    tile_valid = jnp.clip(of_tile(total) - (t0 - of_tile(reg0)), 0, per_tile).astype(jnp.int32)

    q = (t0 - of_tile(reg0))[:, None] + jnp.arange(per_tile, dtype=jnp.int32)[None, :]
    col_of_tile = lambda v: jnp.sum(jnp.where(pick[:, None, :], v[None, :, :], 0), axis=2)
    seg0_t, seg1_t, run0_t = col_of_tile(seg0), col_of_tile(seg0 + chunks), col_of_tile(run0)
    holds = (q[:, :, None] >= seg0_t[:, None, :]) & (q[:, :, None] < seg1_t[:, None, :])
    local = run0_t[:, None, :] + q[:, :, None] - seg0_t[:, None, :]
    row = jnp.asarray(tile_row0, jnp.int32)[None, None, :] + SUBLANES * local
    src = jnp.sum(jnp.where(holds, row, 0), axis=2)
    src = jnp.where(jnp.any(holds, axis=2), src, zero_row)
    n_first = jnp.sum((src < rows_a).astype(jnp.int32), axis=1)
    idx = jnp.arange(N_EXPERTS, dtype=jnp.int32)
    nonempty = total > 0
    ordinal = _excl_cumsum(nonempty.astype(jnp.int32), 0)
    later = jnp.where((idx[None, :] > idx[:, None]) & nonempty[None, :], idx[None, :], N_EXPERTS)
    following = jnp.min(later, axis=1)
    following = jnp.where(following == N_EXPERTS, -1, following)
    lane = jnp.arange(LIST_LANES, dtype=jnp.int32)[None, :]
    src = jnp.pad(src, ((0, 0), (0, LIST_LANES - per_tile)))
    for at, val in ((LIST_COUNT, n_first), (LIST_NEXT, of_tile(following)), (LIST_PARITY, of_tile(ordinal) % 2)):
        src = jnp.where(lane == at, val[:, None], src)
    src = src.reshape(n_tiles, 1, LIST_LANES).astype(jnp.int32)

    j = jnp.arange(LIST_LANES, dtype=jnp.int32)[None, :, None]
    inside = (j >= run0[:, None, :]) & (j < (run0 + chunks)[:, None, :])
    base = (reg0[None, :] + seg0 - run0)[:, None, :]
    dst = SUBLANES * jnp.sum(jnp.where(inside, base + j, 0), axis=2)
    dst = dst.reshape(n_tok, 1, LIST_LANES).astype(jnp.int32)
    return tile_expert.astype(jnp.int32), tile_valid, src, dst


def _moe_ple(h_a, h_b, p_a, p_b, g_ffn, w_router, before, sel, w_gate, w_up, w_down, g_ple, wgate_bf, wproj_bf,
              *, tm_a, tm_b, te, layer):
    xs_a, rec_a, cnt_a = _sort(h_a, g_ffn, w_router, before, sel, tm=tm_a)
    xs_b, rec_b, cnt_b = _sort(h_b, g_ffn, w_router, before, sel, tm=tm_b)
    t_a, t_b = cnt_a.shape[0], cnt_b.shape[0]
    cnt = jnp.concatenate([cnt_a[:, 0, :N_EXPERTS], cnt_b[:, 0, :N_EXPERTS]], axis=0).astype(jnp.int32)
    rows_a = xs_a.shape[0]
    row0 = np.concatenate([np.arange(t_a) * _slots(tm_a), rows_a + np.arange(t_b) * _slots(tm_b)])
    tile_expert, tile_valid, src, dst = _chunk_plan(
        cnt, row0, n_pairs=2 * (h_a.shape[0] + h_b.shape[0]), rows_a=rows_a,
        zero_row=rows_a + _slots(tm_b) - SUBLANES, te=te)
    ys = _experts(xs_a, xs_b, tile_expert, tile_valid, src, w_gate, w_up, w_down, te=te, layer=layer)
    out_a = _combine(h_a, rec_a, dst[:t_a], ys, p_a, g_ple, wgate_bf, wproj_bf, tm=tm_a, layer=layer)
    out_b = _combine(h_b, rec_b, dst[t_a:], ys, p_b, g_ple, wgate_bf, wproj_bf, tm=tm_b, layer=layer)
    return out_a, out_b


def _bucket_np(dist):
    max_exact = N_BUCKETS // 2
    d_f = np.maximum(dist, 1).astype(np.float32)
    large = max_exact + (np.log(d_f / np.float32(max_exact)) / np.float32(np.log(MAX_DISTANCE / max_exact))
                         * np.float32(N_BUCKETS - max_exact)).astype(np.int32)
    large = np.minimum(large, N_BUCKETS - 1)
    return np.where(dist < max_exact, dist, large).astype(np.int32)


def _bias_from_buckets(rel_bias, bucket, valid):
    onehot = (jnp.asarray(bucket)[..., None] == jnp.arange(N_BUCKETS, dtype=jnp.int32)).astype(F32)
    bias = jnp.einsum("...k,kh->h...", onehot, rel_bias.astype(F32), precision=HIGHEST)
    return jnp.where(jnp.asarray(valid)[None], bias, NEG)


def _block_order(dil):
    g = np.arange(Q_BLOCK) // SUBLANES
    j = np.arange(Q_BLOCK) % SUBLANES
    if dil == 1:
        return 16 * j + g
    if dil == 4:
        return 32 * (g // 4) + 4 * j + g % 4
    return SUBLANES * g + j


def _band_bias(rel_bias, dil):
    mu = _block_order(dil)
    qi = mu[:, None] + Q_BLOCK
    ki = np.concatenate([mu, mu + Q_BLOCK])[None, :]
    off = qi - ki
    valid = (off >= 0) & (off <= N_KEYS)
    bucket = _bucket_np(dil * np.clip(off, 0, N_KEYS))
    first = valid & (np.arange(2 * Q_BLOCK)[None, :] >= Q_BLOCK)
    tables = [_bias_from_buckets(rel_bias, bucket, v).reshape(N_HEADS * Q_BLOCK, 2 * Q_BLOCK) for v in (valid, first)]
    return jnp.stack(tables)


def _sample_tables(rel_bias, w_buf):
    qpos = w_buf + np.arange(SAMPLE_T)[:, None]
    pos = np.arange(w_buf + NEW_COLS)[None, :]
    dist = qpos - pos
    in_seq = (dist >= 0) & (pos < w_buf + SAMPLE_T)
    mult = np.zeros(dist.shape, np.float32)
    for (w, d) in PATTERNS:
        mult += in_seq & (dist % d == 0) & (dist <= w)
    bucket = _bucket_np(np.maximum(dist, 0))
    bias = jnp.transpose(_bias_from_buckets(rel_bias, bucket, mult > 0), (1, 0, 2))
    rows = SAMPLE_T * N_HEADS
    mult_rows = np.broadcast_to(mult[:, None, :], (SAMPLE_T, N_HEADS, mult.shape[-1]))
    return bias.reshape(rows, -1), jnp.asarray(mult_rows.reshape(rows, -1))


PROJ_TM = 512
PERM_ROWS = 256
OUT_TM = 512
PROMPT_TM = 256
PROMPT_TE = 512
ATTN_SUB = 4


def _row_perm(tm):
    a = np.arange(tm)
    src = (a // Q_BLOCK) * Q_BLOCK + 16 * (a % SUBLANES) + (a % Q_BLOCK) // SUBLANES
    perm = np.zeros((tm, tm), np.float32)
    perm[a, src] = 1.0
    return perm


def kernel(x_prompt, x_sample, cache_k, cache_v, state_conv, p_prompt, p_sample, rel_bias, g_mix, w_in, q_gain,
           k_gain, conv_w, g_out_att, g_out_conv, w_out, g_ffn, w_router_group, w_router_expert, w_gate, w_up,
           w_down, g_ple, w_ple_gate, w_ple_proj):
    depth = w_in.shape[0]
    batch, seq, _ = x_prompt.shape
    dec_b, dec_t, _ = x_sample.shape
    w_buf = cache_k.shape[2]
    n_s = dec_b * dec_t
    keep = min(w_buf, seq)
    assert dec_t == SAMPLE_T and w_buf % LANES == 0
    assert seq % (Q_BLOCK * 16 * 2) == 0 and keep % PROJ_TM == 0 and seq % PROMPT_TM == 0
    cache_kt = jnp.transpose(cache_k, (0, 1, 3, 4, 2))
    cache_vt = jnp.transpose(cache_v, (0, 1, 3, 4, 2))

    row = lambda a: a.reshape(1, -1)
    src_lane = np.arange(LANES)
    expand = jnp.asarray((src_lane[:, None] // LSE_LANES_PER_HEAD == np.arange(ATT_DIM)[None, :] // HEAD_DIM)
                         & (src_lane[:, None] % LSE_LANES_PER_HEAD == 0), BF16)
    before = jnp.asarray(np.arange(LANES)[:, None] < np.arange(LANES)[None, :], BF16)
    sel_np = np.zeros((SUBLANES, LANES), np.float32)
    sel_np[0, REC_S1] = sel_np[1, REC_S2] = 1.0
    sel = jnp.asarray(sel_np, BF16)
    perm = jnp.asarray(_row_perm(PERM_ROWS), BF16)
    unperm = jnp.asarray(_row_perm(OUT_TM).T, BF16)
    band = [_band_bias(rel_bias, d) for (_, d) in PATTERNS]
    s_bias, s_mult = _sample_tables(rel_bias, w_buf)

    hp = x_prompt.reshape(batch * seq, D_MODEL)
    hs = jnp.swapaxes(x_sample, 0, 1).reshape(n_s, D_MODEL)
    pp_all = p_prompt.reshape(depth, batch * seq, D_PLE)
    ps_all = jnp.swapaxes(p_sample, 1, 2).reshape(depth, n_s, D_PLE)
    new = {k: [] for k in ("kp", "vp", "cp", "ks", "vs", "cs")}
    hist_p = jnp.zeros((batch, SUBLANES, CONV_DIM), F32)

    for l in range(depth):
        w_in_bf = w_in[l].astype(BF16)
        wa_bf = w_out[l, :ATT_DIM].astype(BF16)
        wc_bf = w_out[l, ATT_DIM:].astype(BF16)
        wgate_bf = w_ple_gate[l].astype(BF16)
        wproj_bf = w_ple_proj[l].astype(BF16)
        w_router = jnp.concatenate(
            [w_router_expert[l], w_router_group[l],
             jnp.zeros((D_MODEL, LANES - N_EXPERTS - N_GROUPS), F32)], axis=1).astype(BF16)
        qg, kg = row(jnp.tile(q_gain[l], N_HEADS)), row(jnp.tile(k_gain[l], N_HEADS))
        mix = (row(g_mix[l]), w_in_bf, qg, kg, conv_w[l], row(g_out_conv[l]))
        moe = (row(g_ffn[l]), w_router, before, sel, w_gate, w_up, w_down, row(g_ple[l]), wgate_bf, wproj_bf)

        q, k, v, k_nat, v_nat, yn, nconv = _inproj(
            hp, hist_p, *mix, perm, tm=PROJ_TM, shift=1, tiles_per_seq=seq // PROJ_TM,
            keep_tiles=keep // PROJ_TM)
        os, lses = [], []
        for bias, (_, d) in zip(band, PATTERNS):
            sub = min(ATTN_SUB, seq // (Q_BLOCK * d))
            o, lse = _attn_pattern(q, k, v, bias, batch=batch, seq=seq, dil=d, sub=sub,
                                   res=min(d, ATTN_SUB // sub))
            os.append(o)
            lses.append(lse)
        hp = _outproj(os, lses, yn, hp, row(g_out_att[l]), wa_bf, wc_bf, expand, unperm, tm=OUT_TM)
        new["kp"].append(jnp.transpose(k_nat, (0, 3, 1, 2)))
        new["vp"].append(jnp.transpose(v_nat, (0, 3, 1, 2)))
        new["cp"].append(nconv[:, SUBLANES - 2:])

        hist_s = jnp.swapaxes(state_conv[l], 0, 1).reshape(1, 2 * dec_b, CONV_DIM)
        q, k, v, yn, nconv = _inproj(hs, hist_s, *mix, tm=n_s, shift=dec_b, tiles_per_seq=1)
        bmaj = lambda a: jnp.swapaxes(a.reshape(dec_t, dec_b, N_HEADS, HEAD_DIM), 0, 1)
        qb, kb, vb = bmaj(q), bmaj(k), bmaj(v)
        rows8 = lambda a: jnp.pad(a.reshape(dec_b, dec_t, ATT_DIM), ((0, 0), (0, SUBLANES - dec_t), (0, 0)))
        att = _attn_sample(qb.reshape(dec_b, dec_t, ATT_DIM), rows8(kb), rows8(vb), cache_kt, cache_vt, l,
                           s_bias, s_mult)
        att_tm = jnp.swapaxes(att, 0, 1).reshape(n_s, ATT_DIM)
        hs = _outproj([att_tm], [], yn, hs, row(g_out_att[l]), wa_bf, wc_bf, tm=n_s)
        hp, hs = _moe_ple(hp, hs, pp_all, ps_all, *moe, tm_a=PROMPT_TM, tm_b=n_s, te=PROMPT_TE, layer=l)
        new["ks"].append(kb)
        new["vs"].append(vb)
        new["cs"].append(jnp.swapaxes(nconv.reshape(2, dec_b, CONV_DIM), 0, 1))

    y_prompt = hp.reshape(batch, seq, D_MODEL)
    y_sample = jnp.swapaxes(hs.reshape(dec_t, dec_b, D_MODEL), 0, 1)
    st = lambda key: jnp.stack(new[key])
    return (y_prompt, y_sample, st("kp"), st("vp"), st("cp"), st("ks"), st("vs"), st("cs"))
```

```python
import functools

import jax
import jax.numpy as jnp
import numpy as np
from jax import lax
from jax.experimental import pallas as pl
from jax.experimental.pallas import tpu as pltpu

F32 = jnp.float32
BF16 = jnp.bfloat16
HIGHEST = lax.Precision.HIGHEST

D_MODEL = 1024
HEAD_DIM = 64
N_HEADS = 8
ATT_DIM = N_HEADS * HEAD_DIM
CONV_DIM = D_MODEL - ATT_DIM
MIX_IN = 3 * ATT_DIM + 3 * CONV_DIM
PATTERNS = ((128, 1), (512, 4), (2048, 16))
N_KEYS = 128
Q_BLOCK = 128
N_BUCKETS = 32
MAX_DISTANCE = 2048
N_GROUPS = 4
EXPERTS_PER_GROUP = 8
N_EXPERTS = N_GROUPS * EXPERTS_PER_GROUP
D_EXPERT = 256
D_PLE = 256
EPS = 1e-6
NEG = -1e30

LANES = 128
SUBLANES = 8
SLABS = Q_BLOCK // SUBLANES
LSE_LANES_PER_HEAD = LANES // N_HEADS
VMEM_LIMIT = 56 * 1024 * 1024
NT = (((1,), (1,)), ((), ()))


def _cparams(n_axes):
    return pltpu.CompilerParams(dimension_semantics=("arbitrary",) * n_axes,
                                vmem_limit_bytes=VMEM_LIMIT)


def _full(shape):
    n = len(shape)
    return pl.BlockSpec(shape, lambda *_: (0,) * n)


def _rms(x, gain):
    ms = jnp.mean(x * x, axis=-1, keepdims=True)
    return x * lax.rsqrt(ms + EPS) * gain


def _inproj_kernel(*refs, shift, tiles_per_seq, permute):
    (h_ref, gmix_ref, w_ref, qg_ref, kg_ref, cw_ref, gconv_ref, hist_ref) = refs[:8]
    if permute:
        perm_ref, q_ref, k_ref, v_ref, kn_ref, vn_ref, yn_ref, nconv_ref, carry_ref = refs[8:]
    else:
        q_ref, k_ref, v_ref, yn_ref, nconv_ref, carry_ref = refs[8:]
    i = pl.program_id(0)
    a = _rms(h_ref[...], gmix_ref[...])
    proj = jnp.dot(a.astype(BF16), w_ref[...], preferred_element_type=F32)
    tm = proj.shape[0]

    lower = lax.broadcasted_iota(jnp.int32, (tm, LANES), 1) < HEAD_DIM

    def head_norm(t, g):
        out = []
        for j in range(ATT_DIM // LANES):
            blk = t[:, j * LANES:(j + 1) * LANES]
            sq = blk * blk
            ms_lo = jnp.sum(jnp.where(lower, sq, 0.0), axis=-1, keepdims=True) * (1.0 / HEAD_DIM)
            ms_hi = jnp.sum(jnp.where(lower, 0.0, sq), axis=-1, keepdims=True) * (1.0 / HEAD_DIM)
            scale = jnp.where(lower, lax.rsqrt(ms_lo + EPS), lax.rsqrt(ms_hi + EPS))
            out.append(blk * scale)
        return jnp.concatenate(out, axis=-1) * g

    q = head_norm(proj[:, 0:ATT_DIM], qg_ref[...])
    k = head_norm(proj[:, ATT_DIM:2 * ATT_DIM], kg_ref[...])
    v = proj[:, 2 * ATT_DIM:3 * ATT_DIM]
    if permute:
        kn_ref[...] = k.T.reshape(N_HEADS, HEAD_DIM, tm)
        vn_ref[...] = v.T.reshape(N_HEADS, HEAD_DIM, tm)
        qkv = jnp.concatenate([q * (HEAD_DIM ** -0.5), k, v], axis=-1).astype(BF16)
        pm = perm_ref.shape[0]
        moved = jnp.concatenate([jnp.dot(perm_ref[...], qkv[r:r + pm], preferred_element_type=F32)
                                 for r in range(0, tm, pm)], axis=0)
        q_ref[...] = moved[:, 0:ATT_DIM]
        k_ref[...] = moved[:, ATT_DIM:2 * ATT_DIM]
        v_ref[...] = moved[:, 2 * ATT_DIM:3 * ATT_DIM]
    else:
        q_ref[...] = q
        k_ref[...] = k
        v_ref[...] = v
    c0 = 3 * ATT_DIM
    hc = proj[:, c0:c0 + CONV_DIM]
    gb = proj[:, c0 + CONV_DIM:c0 + 2 * CONV_DIM]
    gc = proj[:, c0 + 2 * CONV_DIM:c0 + 3 * CONV_DIM]
    u = gc * hc

    if shift == 1:
        @pl.when(i % tiles_per_seq == 0)
        def _():
            carry_ref[...] = hist_ref[0]
        h0 = carry_ref[SUBLANES - 2:SUBLANES - 1, :]
        h1 = carry_ref[SUBLANES - 1:SUBLANES, :]
        row = lax.broadcasted_iota(jnp.int32, (tm, 1), 0)
        u1 = jnp.where(row == 0, h1, pltpu.roll(u, 1, 0))
        u2 = jnp.where(row == 0, h0, jnp.where(row == 1, h1, pltpu.roll(u, 2, 0)))
        carry_ref[...] = u[tm - SUBLANES:tm, :]
        nconv_ref[0] = u[tm - SUBLANES:tm, :]
    else:
        hist = hist_ref[0]
        u1 = jnp.concatenate([hist[shift:2 * shift], u[0:tm - shift]], axis=0)
        u2 = jnp.concatenate([hist, u[0:tm - 2 * shift]], axis=0)
        nconv_ref[0] = u[tm - 2 * shift:tm, :]
    conv = cw_ref[0:1, :] * u2 + cw_ref[1:2, :] * u1 + cw_ref[2:3, :] * u
    yn_ref[...] = _rms(gb * conv, gconv_ref[...])


def _inproj(h, hist, g_mix, w_in_bf, q_gain, k_gain, conv_w, g_out_conv, perm=None, *,
            tm, shift, tiles_per_seq, keep_tiles=0):
    n = h.shape[0]
    hist_rows = hist.shape[1]
    nseq = hist.shape[0]
    tok = lambda w: pl.BlockSpec((tm, w), lambda i: (i, 0))
    seq3 = lambda r: pl.BlockSpec((1, r, CONV_DIM), lambda i: (i // tiles_per_seq, 0, 0))
    nconv_rows = SUBLANES if shift == 1 else 2 * shift
    att = jax.ShapeDtypeStruct((n, ATT_DIM), F32)
    in_specs = [tok(D_MODEL), _full((1, D_MODEL)), _full((D_MODEL, MIX_IN)), _full((1, ATT_DIM)),
                _full((1, ATT_DIM)), _full((3, CONV_DIM)), _full((1, CONV_DIM)), seq3(hist_rows)]
    args = [h, g_mix, w_in_bf, q_gain, k_gain, conv_w, g_out_conv, hist]
    out_specs = [tok(ATT_DIM)] * 3
    out_shape = [att] * 3
    if perm is not None:
        in_specs.append(_full(perm.shape))
        args.append(perm)
        first = tiles_per_seq - keep_tiles
        kept = pl.BlockSpec((None, N_HEADS, HEAD_DIM, tm),
                            lambda i: (i // tiles_per_seq, 0, 0, jnp.maximum(i % tiles_per_seq - first, 0)))
        out_specs += [kept, kept]
        out_shape += [jax.ShapeDtypeStruct((nseq, N_HEADS, HEAD_DIM, keep_tiles * tm), F32)] * 2
    out_specs += [tok(CONV_DIM), seq3(nconv_rows)]
    out_shape += [jax.ShapeDtypeStruct((n, CONV_DIM), F32),
                  jax.ShapeDtypeStruct((nseq, nconv_rows, CONV_DIM), F32)]
    return pl.pallas_call(
        functools.partial(_inproj_kernel, shift=shift, tiles_per_seq=tiles_per_seq, permute=perm is not None),
        grid=(n // tm,),
        in_specs=in_specs, out_specs=out_specs, out_shape=out_shape,
        scratch_shapes=[pltpu.VMEM((SUBLANES, CONV_DIM), F32)],
        compiler_params=_cparams(1),
        name="inproj",
    )(*args)


def _attn_block(q, kk, vv, bias):
    lane = lax.broadcasted_iota(jnp.int32, (Q_BLOCK, LANES), 1)
    upper = lane >= HEAD_DIM
    scores = []
    for h in range(N_HEADS):
        j, e = divmod(h, 2)
        qp = q[:, j * LANES:(j + 1) * LANES]
        qm = (jnp.where(upper, qp, 0.0) if e else jnp.where(upper, 0.0, qp)).astype(BF16)
        scores.append(lax.dot_general(qm, kk[:, j * LANES:(j + 1) * LANES], NT, preferred_element_type=F32))
    s = jnp.concatenate(scores, axis=0) + bias
    m = jnp.max(s, axis=-1, keepdims=True)
    p = jnp.exp(s - m)
    den = jnp.sum(p, axis=-1, keepdims=True)
    pb = p.astype(BF16)
    head_rows = lambda t, h: t[h * Q_BLOCK:(h + 1) * Q_BLOCK]
    lse_grp = lane // LSE_LANES_PER_HEAD
    m_tile = jnp.zeros((Q_BLOCK, LANES), F32)
    den_tile = jnp.ones((Q_BLOCK, LANES), F32)
    outs = []
    for j in range(N_HEADS // 2):
        even, odd = 2 * j, 2 * j + 1
        vp = vv[:, j * LANES:(j + 1) * LANES]
        pair = jnp.where(upper, jnp.dot(head_rows(pb, odd), vp, preferred_element_type=F32),
                         jnp.dot(head_rows(pb, even), vp, preferred_element_type=F32))
        outs.append(pair / jnp.where(upper, head_rows(den, odd), head_rows(den, even)))
        for h in (even, odd):
            m_tile = jnp.where(lse_grp == h, head_rows(m, h), m_tile)
            den_tile = jnp.where(lse_grp == h, head_rows(den, h), den_tile)
    return jnp.concatenate(outs, axis=-1), m_tile + jnp.log(den_tile)


def _attn_kernel(q_ref, kp_ref, kc_ref, vp_ref, vc_ref, bias_ref, o_ref, lse_ref, kbuf, vbuf, *, sub, res):
    n = pl.program_id(2)
    rows = sub * Q_BLOCK
    for r in range(res):
        kbuf[0:Q_BLOCK, :] = kp_ref[:, r].reshape(Q_BLOCK, ATT_DIM).astype(BF16)
        kbuf[Q_BLOCK:, :] = kc_ref[:, :, r].reshape(rows, ATT_DIM).astype(BF16)
        vbuf[0:Q_BLOCK, :] = vp_ref[:, r].reshape(Q_BLOCK, ATT_DIM).astype(BF16)
        vbuf[Q_BLOCK:, :] = vc_ref[:, :, r].reshape(rows, ATT_DIM).astype(BF16)
        for j in range(sub):
            q = q_ref[j, :, r].reshape(Q_BLOCK, ATT_DIM)
            r0 = j * Q_BLOCK
            first = (n == 0).astype(jnp.int32) if j == 0 else 0
            o, lse = _attn_block(q, kbuf[r0:r0 + 2 * Q_BLOCK, :], vbuf[r0:r0 + 2 * Q_BLOCK, :], bias_ref[first])
            o_ref[j, :, r] = o.reshape(SLABS, SUBLANES, ATT_DIM)
            lse_ref[j, :, r] = lse.reshape(SLABS, SUBLANES, LANES)


def _attn_pattern(q, k, v, bias, *, batch, seq, dil, sub, res):
    nblk = seq // (Q_BLOCK * dil)
    view = lambda t: t.reshape(batch, nblk, SLABS, dil, SUBLANES, t.shape[-1])
    cur = lambda c: pl.BlockSpec((None, sub, SLABS, res, SUBLANES, c), lambda b, r, n: (b, n, 0, r, 0, 0))
    prev = pl.BlockSpec((None, None, SLABS, res, SUBLANES, ATT_DIM),
                        lambda b, r, n: (b, jnp.maximum(n * sub - 1, 0), 0, r, 0, 0))
    o, lse = pl.pallas_call(
        functools.partial(_attn_kernel, sub=sub, res=res),
        grid=(batch, dil // res, nblk // sub),
        in_specs=[cur(ATT_DIM), prev, cur(ATT_DIM), prev, cur(ATT_DIM),
                  _full((2, N_HEADS * Q_BLOCK, 2 * Q_BLOCK))],
        out_specs=[cur(ATT_DIM), cur(LANES)],
        out_shape=[jax.ShapeDtypeStruct((batch, nblk, SLABS, dil, SUBLANES, ATT_DIM), F32),
                   jax.ShapeDtypeStruct((batch, nblk, SLABS, dil, SUBLANES, LANES), F32)],
        scratch_shapes=[pltpu.VMEM(((sub + 1) * Q_BLOCK, ATT_DIM), BF16),
                        pltpu.VMEM(((sub + 1) * Q_BLOCK, ATT_DIM), BF16)],
        compiler_params=_cparams(3),
        name=f"attn_d{dil}",
    )(view(q), view(k), view(k), view(v), view(v), bias)
    return o.reshape(batch * seq, ATT_DIM), lse.reshape(batch * seq, LANES)


SAMPLE_T = 4
NEW_COLS = LANES


def _attn_sample_kernel(q_ref, kt_ref, kn_ref, vt_ref, vn_ref, bias_ref, mult_ref, o_ref):
    rows = SAMPLE_T * N_HEADS
    q4 = q_ref[...] * (HEAD_DIM ** -0.5)
    qt = jnp.concatenate([jnp.broadcast_to(q4[t:t + 1, :], (N_HEADS, ATT_DIM)) for t in range(SAMPLE_T)], axis=0)
    lane_head = lax.broadcasted_iota(jnp.int32, (rows, ATT_DIM), 1) // HEAD_DIM
    row_head = lax.broadcasted_iota(jnp.int32, (rows, ATT_DIM), 0) % N_HEADS
    own = lane_head == row_head
    qbd = jnp.where(own, qt, 0.0).astype(BF16)
    flat = lambda ref: ref[...].reshape(ATT_DIM, ref.shape[-1]).astype(BF16)
    pad = jnp.zeros((NEW_COLS - SUBLANES, ATT_DIM), F32)
    new_rows = lambda ref: jnp.concatenate([ref[...], pad], axis=0).astype(BF16)
    s = jnp.concatenate([jnp.dot(qbd, flat(kt_ref), preferred_element_type=F32),
                         lax.dot_general(qbd, new_rows(kn_ref), NT, preferred_element_type=F32)],
                        axis=-1) + bias_ref[...]
    m = jnp.max(s, axis=-1, keepdims=True)
    p = jnp.exp(s - m) * mult_ref[...]
    den = jnp.sum(p, axis=-1, keepdims=True)
    pb = p.astype(BF16)
    w_buf = kt_ref.shape[-1]
    acc = (lax.dot_general(pb[:, :w_buf], flat(vt_ref), NT, preferred_element_type=F32)
           + jnp.dot(pb[:, w_buf:], new_rows(vn_ref), preferred_element_type=F32))
    acc = jnp.where(own, acc / den, 0.0)
    for t in range(SAMPLE_T):
        o_ref[t:t + 1, :] = jnp.sum(acc[t * N_HEADS:(t + 1) * N_HEADS, :], axis=0, keepdims=True)


def _attn_sample(q, k_new, v_new, cache_kt, cache_vt, layer, bias, mult):
    nb = q.shape[0]
    w_buf = cache_kt.shape[-1]
    tok = pl.BlockSpec((None, SAMPLE_T, ATT_DIM), lambda b: (b, 0, 0))
    new = pl.BlockSpec((None, SUBLANES, ATT_DIM), lambda b: (b, 0, 0))
    old = pl.BlockSpec((None, None, N_HEADS, HEAD_DIM, w_buf), lambda b: (layer, b, 0, 0, 0))
    tbl = _full((SAMPLE_T * N_HEADS, w_buf + NEW_COLS))
    return pl.pallas_call(
        _attn_sample_kernel,
        grid=(nb,),
        in_specs=[tok, old, new, old, new, tbl, tbl],
        out_specs=tok,
        out_shape=jax.ShapeDtypeStruct((nb, SAMPLE_T, ATT_DIM), F32),
        compiler_params=_cparams(1),
        name="attn_sample",
    )(q, cache_kt, k_new, cache_vt, v_new, bias, mult)


def _split_dot(x, e_ref):
    hi = x.astype(BF16)
    lo = (x - hi.astype(F32)).astype(BF16)
    return (jnp.dot(hi, e_ref[...], preferred_element_type=F32)
            + jnp.dot(lo, e_ref[...], preferred_element_type=F32))


def _outproj_kernel(*refs, n_pat):
    mix = n_pat > 1
    n_lse = n_pat if mix else 0
    o_refs = refs[0:n_pat]
    l_refs = refs[n_pat:n_pat + n_lse]
    rest = refs[n_pat + n_lse:]
    if mix:
        yn_ref, h_ref, gatt_ref, exp_ref, unperm_ref, wa_ref, wc_ref, out_ref = rest
        lses = [r[...] for r in l_refs]
        top = functools.reduce(jnp.maximum, lses)
        ws = [jnp.exp(l - top) for l in lses]
        tot = functools.reduce(lambda a, b: a + b, ws)
        att = None
        for w, o_ref in zip(ws, o_refs):
            term = _split_dot(w / tot, exp_ref) * o_ref[...]
            att = term if att is None else att + term
        att_bf = jnp.dot(unperm_ref[...], _rms(att, gatt_ref[...]).astype(BF16),
                         preferred_element_type=F32).astype(BF16)
    else:
        yn_ref, h_ref, gatt_ref, wa_ref, wc_ref, out_ref = rest
        att_bf = _rms(o_refs[0][...], gatt_ref[...]).astype(BF16)
    y = (jnp.dot(att_bf, wa_ref[...], preferred_element_type=F32)
         + jnp.dot(yn_ref[...].astype(BF16), wc_ref[...], preferred_element_type=F32))
    out_ref[...] = h_ref[...] + y


def _outproj(os, lses, yn, h, g_att, wa_bf, wc_bf, expand=None, unperm=None, *, tm):
    n = h.shape[0]
    n_pat = len(os)
    tok = lambda w: pl.BlockSpec((tm, w), lambda i: (i, 0))
    in_specs = [tok(ATT_DIM)] * n_pat + [tok(LANES)] * len(lses) + [tok(CONV_DIM), tok(D_MODEL), _full((1, ATT_DIM))]
    args = [*os, *lses, yn, h, g_att]
    if n_pat > 1:
        in_specs += [_full((LANES, ATT_DIM)), _full((tm, tm))]
        args += [expand, unperm]
    in_specs += [_full((ATT_DIM, D_MODEL)), _full((CONV_DIM, D_MODEL))]
    args += [wa_bf, wc_bf]
    return pl.pallas_call(
        functools.partial(_outproj_kernel, n_pat=n_pat),
        grid=(n // tm,),
        in_specs=in_specs,
        out_specs=tok(D_MODEL),
        out_shape=jax.ShapeDtypeStruct((n, D_MODEL), F32),
        compiler_params=_cparams(1),
        name="outproj",
    )(*args)


GROUP_LANE0 = N_EXPERTS
REC_E1, REC_E2, REC_S1, REC_S2, REC_W1, REC_W2 = range(6)
PACKED = D_MODEL // 2
ROW_W = PACKED + LANES
N_DMA_PRIORITIES = 2
LIST_LANES = LANES
LIST_COUNT = LIST_LANES - 1
LIST_NEXT = LIST_LANES - 2
LIST_PARITY = LIST_LANES - 3


def _slots(tm):
    need = 2 * tm + (SUBLANES - 1) * N_EXPERTS + SUBLANES
    return -(-need // LANES) * LANES


def _pieces(x):
    hi = x.astype(BF16)
    r1 = x - hi.astype(F32)
    mid = r1.astype(BF16)
    return hi, mid, (r1 - mid.astype(F32)).astype(BF16)


def _sort_kernel(h_ref, g_ref, wr_ref, before_ref, sel_ref, xs_ref, rec_ref, cnt_ref, *, group):
    tm = h_ref.shape[0] // group
    slots = xs_ref.shape[0] // group
    for t in range(group):
        xs, rec, cnt = _sort_tile(h_ref[t * tm:(t + 1) * tm, :], g_ref, wr_ref, before_ref, sel_ref, slots)
        xs_ref[t * slots:(t + 1) * slots, :] = xs
        rec_ref[t * tm:(t + 1) * tm, :] = rec
        cnt_ref[t] = cnt


def _sort_tile(h, g_ref, wr_ref, before_ref, sel_ref, slots):
    m = _rms(h, g_ref[...])
    logits = jnp.dot(m.astype(BF16), wr_ref[...], preferred_element_type=F32)
    tm = logits.shape[0]
    lane_i = lax.broadcasted_iota(jnp.int32, (tm, LANES), 1)
    lane = lane_i.astype(F32)
    big = jnp.float32(4 * LANES)

    is_g = jnp.logical_and(lane_i >= GROUP_LANE0, lane_i < GROUP_LANE0 + N_GROUPS)
    gl = jnp.where(is_g, logits, NEG)
    gmax = jnp.max(gl, axis=-1, keepdims=True)
    g_w = 1.0 / jnp.sum(jnp.where(is_g, jnp.exp(gl - gmax), 0.0), axis=-1, keepdims=True)
    g_sel = jnp.min(jnp.where(gl == gmax, lane - GROUP_LANE0, big), axis=-1, keepdims=True)

    grp_of_lane = (lane_i // EXPERTS_PER_GROUP).astype(F32)
    in_grp = jnp.logical_and(lane_i < N_EXPERTS, grp_of_lane == g_sel)
    el = jnp.where(in_grp, logits, NEG)
    t1 = jnp.max(el, axis=-1, keepdims=True)
    e1 = jnp.min(jnp.where(el == t1, lane, big), axis=-1, keepdims=True)
    el2 = jnp.where(lane == e1, NEG, el)
    t2 = jnp.max(el2, axis=-1, keepdims=True)
    e2 = jnp.min(jnp.where(el2 == t2, lane, big), axis=-1, keepdims=True)
    ex = jnp.exp(t2 - t1)
    w1 = g_w / (1.0 + ex)
    w2 = g_w * ex / (1.0 + ex)

    hit1 = lane == e1
    hit2 = lane == e2
    c = jnp.where(hit1, 1.0, jnp.where(hit2, 1.0, 0.0))
    rr = lax.broadcasted_iota(jnp.int32, (tm, tm), 0)
    cc = lax.broadcasted_iota(jnp.int32, (tm, tm), 1)
    lower = jnp.where(rr > cc, 1.0, 0.0).astype(BF16)
    rank = jnp.dot(lower, c.astype(BF16), preferred_element_type=F32)
    cnt = jnp.sum(c, axis=0, keepdims=True)
    chunks = jnp.floor((cnt + (SUBLANES - 1)) * (1.0 / SUBLANES))
    start = SUBLANES * jnp.dot(jnp.broadcast_to(chunks, (SUBLANES, LANES)).astype(BF16), before_ref[...],
                               preferred_element_type=F32)[0:1, :]
    slot_of = rank + start
    s1 = jnp.sum(jnp.where(hit1, slot_of, 0.0), axis=-1, keepdims=True)
    s2 = jnp.sum(jnp.where(hit2, slot_of, 0.0), axis=-1, keepdims=True)

    rec = jnp.zeros((tm, LANES), F32)
    for idx, val in ((REC_E1, e1), (REC_E2, e2), (REC_S1, s1), (REC_S2, s2), (REC_W1, w1), (REC_W2, w2)):
        rec = jnp.where(lane_i == idx, val, rec)

    rec_parts = _pieces(rec)[:2]
    srow = sum(lax.dot_general(sel_ref[...], part, NT, preferred_element_type=F32) for part in rec_parts)
    slot_id = lax.broadcasted_iota(jnp.int32, (slots, tm), 0).astype(F32)
    place = jnp.where(slot_id == srow[0:1, :], 1.0, jnp.where(slot_id == srow[1:2, :], 1.0, 0.0)).astype(BF16)
    payload = jnp.concatenate([m.astype(BF16), *rec_parts], axis=-1)
    moved = jnp.dot(place, payload, preferred_element_type=F32)
    info = moved[:, D_MODEL:D_MODEL + LANES] + moved[:, D_MODEL + LANES:]
    xs = jnp.concatenate([_pack_bf16(moved[:, :D_MODEL], is_bf16=True), pltpu.bitcast(info, jnp.uint32)], axis=-1)
    return xs, rec, jnp.broadcast_to(cnt, (SUBLANES, LANES))


def _sort(h, g_ffn, w_router, before, sel, *, tm):
    n = h.shape[0]
    slots = _slots(tm)
    group = 2 if (n // tm) % 2 == 0 else 1
    tok = lambda w: pl.BlockSpec((group * tm, w), lambda i: (i, 0))
    return pl.pallas_call(
        functools.partial(_sort_kernel, group=group),
        grid=(n // (group * tm),),
        in_specs=[tok(D_MODEL), _full((1, D_MODEL)), _full((D_MODEL, LANES)), _full((LANES, LANES)),
                  _full((SUBLANES, LANES))],
        out_specs=[pl.BlockSpec((group * slots, ROW_W), lambda i: (i, 0)), tok(LANES),
                   pl.BlockSpec((group, SUBLANES, LANES), lambda i: (i, 0, 0))],
        out_shape=[jax.ShapeDtypeStruct((n // tm * slots, ROW_W), jnp.uint32), jax.ShapeDtypeStruct((n, LANES), F32),
                   jax.ShapeDtypeStruct((n // tm, SUBLANES, LANES), F32)],
        compiler_params=_cparams(1),
        name="moe_sort",
    )(h, g_ffn, w_router, before, sel)


def _chunk_copy(src_hbm, row, dst, c, sem):
    if not isinstance(row, int):
        row = pl.multiple_of(row, SUBLANES)
    to = c * SUBLANES
    if not isinstance(to, int):
        to = pl.multiple_of(to, SUBLANES)
    return pltpu.make_async_copy(src_hbm.at[pl.ds(row, SUBLANES), :], dst.at[pl.ds(to, SUBLANES), :], sem)


def _expert_kernel(te_ref, tv_ref, lst_ref, xa_ref, xb_ref, wg_ref, wu_ref, wd_ref, y_ref,
                    xbuf, sem, wg_st, wu_st, wd_st, wsem, wg_bf, wu_bf, wd_bf, *, rows_a, layer):
    i = pl.program_id(0)
    n = pl.num_programs(0)
    te = xbuf.shape[1]
    slot = i % 2

    def fetch(tile, to_slot):
        n_a = lst_ref[tile, LIST_COUNT]

        def from_a(c, carry):
            priority = c % N_DMA_PRIORITIES if isinstance(c, int) else 0
            _chunk_copy(xa_ref, lst_ref[tile, c], xbuf.at[to_slot], c, sem.at[to_slot]).start(priority=priority)
            return carry

        def from_b(c, carry):
            _chunk_copy(xb_ref, lst_ref[tile, c] - rows_a, xbuf.at[to_slot], c, sem.at[to_slot]).start()
            return carry

        per_tile = te // SUBLANES

        @pl.when(n_a == per_tile)
        def _():
            for c in range(per_tile):
                from_a(c, 0)

        @pl.when(n_a != per_tile)
        def _():
            lax.fori_loop(0, n_a, from_a, 0)
            lax.fori_loop(n_a, per_tile, from_b, 0)

    @pl.when(jnp.logical_and(i == 0, tv_ref[0] > 0))
    def _():
        fetch(0, 0)

    nxt = jnp.minimum(i + 1, n - 1)

    @pl.when(jnp.logical_and(i + 1 < n, tv_ref[nxt] > 0))
    def _():
        fetch(nxt, 1 - slot)

    valid = tv_ref[i]
    changed = jnp.logical_or(i == 0, te_ref[i] != te_ref[jnp.maximum(i - 1, 0)])

    def weight_copies(expert, s):
        return [pltpu.make_async_copy(src.at[layer, expert], dst.at[s], wsem.at[s])
                for src, dst in ((wg_ref, wg_st), (wu_ref, wu_st), (wd_ref, wd_st))]

    @pl.when(jnp.logical_and(changed, valid > 0))
    def _():
        s = lst_ref[i, LIST_PARITY]

        @pl.when(i == 0)
        def _():
            for cp in weight_copies(te_ref[0], 0):
                cp.start()

        for cp in weight_copies(te_ref[i], s):
            cp.wait()
        wg_bf[...] = wg_st[s].astype(BF16)
        wu_bf[...] = wu_st[s].astype(BF16)
        wd_bf[...] = wd_st[s].astype(BF16)
        nxt_expert = lst_ref[i, LIST_NEXT]

        @pl.when(nxt_expert >= 0)
        def _():
            for cp in weight_copies(nxt_expert, 1 - s):
                cp.start()

    @pl.when(valid > 0)
    def _():
        pltpu.make_async_copy(xbuf.at[slot], xbuf.at[slot], sem.at[slot]).wait()
        rows = xbuf[slot]
        x = _unpack_bf16(rows[:, :PACKED])
        info = pltpu.bitcast(rows[:, PACKED:], F32)
        mine = info[:, REC_E1:REC_E1 + 1] == te_ref[i].astype(F32)
        gate = jnp.where(mine, info[:, REC_W1:REC_W1 + 1], info[:, REC_W2:REC_W2 + 1])
        hg = jnp.dot(x, wg_bf[...], preferred_element_type=F32)
        hu = jnp.dot(x, wu_bf[...], preferred_element_type=F32)
        hid = (hg * jax.nn.sigmoid(hg)) * hu * gate
        y = jnp.dot(hid.astype(BF16), wd_bf[...], preferred_element_type=F32)
        y_ref[...] = _pack_bf16(y)

    @pl.when(valid <= 0)
    def _():
        y_ref[...] = jnp.zeros_like(y_ref)


def _pack_bf16(x, is_bf16=False):
    w = x.shape[-1] // 2
    bits = lambda a: pltpu.bitcast(a if is_bf16 else a.astype(BF16).astype(F32), jnp.uint32)
    return jnp.bitwise_or(lax.shift_right_logical(bits(x[:, :w]), jnp.uint32(16)),
                          jnp.bitwise_and(bits(x[:, w:]), jnp.uint32(0xFFFF0000)))


def _unpack_bf16(words):
    as_f32 = lambda a: pltpu.bitcast(a, F32)
    return jnp.concatenate([as_f32(lax.shift_left(words, jnp.uint32(16))),
                            as_f32(jnp.bitwise_and(words, jnp.uint32(0xFFFF0000)))], axis=-1).astype(BF16)


def _experts(xs_a, xs_b, tile_expert, tile_valid, src, w_gate, w_up, w_down, *, te, layer):
    n_tiles = tile_expert.shape[0]
    hbm = pl.BlockSpec(memory_space=pl.ANY)
    grid_spec = pltpu.PrefetchScalarGridSpec(
        num_scalar_prefetch=3,
        grid=(n_tiles,),
        in_specs=[hbm, hbm, hbm, hbm, hbm],
        out_specs=pl.BlockSpec((te, PACKED), lambda i, e, v, s: (i, 0)),
        scratch_shapes=[pltpu.VMEM((2, te, ROW_W), jnp.uint32), pltpu.SemaphoreType.DMA((2,)),
                        pltpu.VMEM((2, D_MODEL, D_EXPERT), F32), pltpu.VMEM((2, D_MODEL, D_EXPERT), F32),
                        pltpu.VMEM((2, D_EXPERT, D_MODEL), F32), pltpu.SemaphoreType.DMA((2,)),
                        pltpu.VMEM((D_MODEL, D_EXPERT), BF16), pltpu.VMEM((D_MODEL, D_EXPERT), BF16),
                        pltpu.VMEM((D_EXPERT, D_MODEL), BF16)])
    return pl.pallas_call(
        functools.partial(_expert_kernel, rows_a=xs_a.shape[0], layer=layer),
        grid_spec=grid_spec,
        out_shape=jax.ShapeDtypeStruct((n_tiles * te, PACKED), jnp.uint32),
        compiler_params=_cparams(1),
        name="experts",
    )(tile_expert, tile_valid, src.reshape(n_tiles, LIST_LANES), xs_a, xs_b, w_gate, w_up, w_down)


def _combine_kernel(lst_ref, h_ref, rec_ref, ys_ref, p_ref, gple_ref, wgate_ref, wproj_ref, out_ref,
                     ybuf, sem):
    i = pl.program_id(0)
    n = pl.num_programs(0)
    slot = i % 2
    tm = h_ref.shape[0]
    slots = ybuf.shape[1]

    def fetch(tile, to_slot):
        for c in range(slots // SUBLANES):
            _chunk_copy(ys_ref, lst_ref[tile, c], ybuf.at[to_slot], c, sem.at[to_slot]).start(
                priority=c % N_DMA_PRIORITIES)

    @pl.when(i == 0)
    def _():
        fetch(0, 0)

    @pl.when(i + 1 < n)
    def _():
        fetch(i + 1, 1 - slot)

    pltpu.make_async_copy(ybuf.at[slot], ybuf.at[slot], sem.at[slot]).wait()

    rec = rec_ref[...]
    slot_id = lax.broadcasted_iota(jnp.int32, (tm, slots), 1).astype(F32)
    back = jnp.where(slot_id == rec[:, REC_S1:REC_S1 + 1], 1.0,
                     jnp.where(slot_id == rec[:, REC_S2:REC_S2 + 1], 1.0, 0.0)).astype(BF16)
    h2 = h_ref[...] + jnp.dot(back, _unpack_bf16(ybuf[slot]), preferred_element_type=F32)
    gate = jax.nn.sigmoid(jnp.dot(_rms(h2, gple_ref[...]).astype(BF16), wgate_ref[...], preferred_element_type=F32))
    ple = jnp.dot(p_ref[...].astype(BF16), wproj_ref[...], preferred_element_type=F32)
    out_ref[...] = h2 + ple * gate


def _combine(h, rec, dst, ys, p, g_ple, wgate_bf, wproj_bf, *, tm, layer):
    n = h.shape[0]
    n_tok = n // tm
    tok = lambda w: pl.BlockSpec((tm, w), lambda i, s: (i, 0))
    full = lambda shape: pl.BlockSpec(shape, lambda i, s: (0,) * len(shape))
    grid_spec = pltpu.PrefetchScalarGridSpec(
        num_scalar_prefetch=1,
        grid=(n_tok,),
        in_specs=[tok(D_MODEL), tok(LANES), pl.BlockSpec(memory_space=pl.ANY),
                  pl.BlockSpec((None, tm, D_PLE), lambda i, s: (layer, i, 0)),
                  full((1, D_MODEL)), full((D_MODEL, D_MODEL)), full((D_PLE, D_MODEL))],
        out_specs=tok(D_MODEL),
        scratch_shapes=[pltpu.VMEM((2, _slots(tm), PACKED), jnp.uint32), pltpu.SemaphoreType.DMA((2,))])
    return pl.pallas_call(
        _combine_kernel,
        grid_spec=grid_spec,
        out_shape=jax.ShapeDtypeStruct((n, D_MODEL), F32),
        compiler_params=_cparams(1),
        name="combine",
    )(dst.reshape(n_tok, LIST_LANES), h, rec, ys, p, g_ple, wgate_bf, wproj_bf)


def _excl_cumsum(x, axis):
    return jnp.cumsum(x, axis=axis) - x


def _chunk_plan(cnt, tile_row0, *, n_pairs, rows_a, zero_row, te):
    n_tok = cnt.shape[0]
    per_tile = te // SUBLANES
    chunks = (cnt + SUBLANES - 1) // SUBLANES
    run0 = _excl_cumsum(chunks, 1)
    seg0 = _excl_cumsum(chunks, 0)
    total = jnp.sum(chunks, axis=0)
    region = ((total + per_tile - 1) // per_tile) * per_tile
    reg_end = jnp.cumsum(region)
    reg0 = reg_end - region

    n_tiles = -(-(n_pairs + (SUBLANES - 1) * N_EXPERTS * n_tok) // te) + N_EXPERTS
    t0 = jnp.arange(n_tiles, dtype=jnp.int32) * per_tile
    tile_expert = jnp.minimum(jnp.sum((t0[:, None] >= reg_end[None, :]).astype(jnp.int32), axis=1), N_EXPERTS - 1)
    pick = tile_expert[:, None] == jnp.arange(N_EXPERTS, dtype=jnp.int32)[None, :]
    of_tile = lambda v: jnp.sum(jnp.where(pick, v[None, :], 0), axis=1)
    tile_valid = jnp.clip(of_tile(total) - (t0 - of_tile(reg0)), 0, per_tile).astype(jnp.int32)

    q = (t0 - of_tile(reg0))[:, None] + jnp.arange(per_tile, dtype=jnp.int32)[None, :]
    col_of_tile = lambda v: jnp.sum(jnp.where(pick[:, None, :], v[None, :, :], 0), axis=2)
    seg0_t, seg1_t, run0_t = col_of_tile(seg0), col_of_tile(seg0 + chunks), col_of_tile(run0)
    holds = (q[:, :, None] >= seg0_t[:, None, :]) & (q[:, :, None] < seg1_t[:, None, :])
    local = run0_t[:, None, :] + q[:, :, None] - seg0_t[:, None, :]
    row = jnp.asarray(tile_row0, jnp.int32)[None, None, :] + SUBLANES * local
    src = jnp.sum(jnp.where(holds, row, 0), axis=2)
    src = jnp.where(jnp.any(holds, axis=2), src, zero_row)
    n_first = jnp.sum((src < rows_a).astype(jnp.int32), axis=1)
    idx = jnp.arange(N_EXPERTS, dtype=jnp.int32)
    nonempty = total > 0
    ordinal = _excl_cumsum(nonempty.astype(jnp.int32), 0)
    later = jnp.where((idx[None, :] > idx[:, None]) & nonempty[None, :], idx[None, :], N_EXPERTS)
    following = jnp.min(later, axis=1)
    following = jnp.where(following == N_EXPERTS, -1, following)
    lane = jnp.arange(LIST_LANES, dtype=jnp.int32)[None, :]
    src = jnp.pad(src, ((0, 0), (0, LIST_LANES - per_tile)))
    for at, val in ((LIST_COUNT, n_first), (LIST_NEXT, of_tile(following)), (LIST_PARITY, of_tile(ordinal) % 2)):
        src = jnp.where(lane == at, val[:, None], src)
    src = src.reshape(n_tiles, 1, LIST_LANES).astype(jnp.int32)

    j = jnp.arange(LIST_LANES, dtype=jnp.int32)[None, :, None]
    inside = (j >= run0[:, None, :]) & (j < (run0 + chunks)[:, None, :])
    base = (reg0[None, :] + seg0 - run0)[:, None, :]
    dst = SUBLANES * jnp.sum(jnp.where(inside, base + j, 0), axis=2)
    dst = dst.reshape(n_tok, 1, LIST_LANES).astype(jnp.int32)
    return tile_expert.astype(jnp.int32), tile_valid, src, dst


def _moe_ple(h_a, h_b, p_a, p_b, g_ffn, w_router, before, sel, w_gate, w_up, w_down, g_ple, wgate_bf, wproj_bf,
              *, tm_a, tm_b, te, layer):
    xs_a, rec_a, cnt_a = _sort(h_a, g_ffn, w_router, before, sel, tm=tm_a)
    xs_b, rec_b, cnt_b = _sort(h_b, g_ffn, w_router, before, sel, tm=tm_b)
    t_a, t_b = cnt_a.shape[0], cnt_b.shape[0]
    cnt = jnp.concatenate([cnt_a[:, 0, :N_EXPERTS], cnt_b[:, 0, :N_EXPERTS]], axis=0).astype(jnp.int32)
    rows_a = xs_a.shape[0]
    row0 = np.concatenate([np.arange(t_a) * _slots(tm_a), rows_a + np.arange(t_b) * _slots(tm_b)])
    tile_expert, tile_valid, src, dst = _chunk_plan(
        cnt, row0, n_pairs=2 * (h_a.shape[0] + h_b.shape[0]), rows_a=rows_a,
        zero_row=rows_a + _slots(tm_b) - SUBLANES, te=te)
    ys = _experts(xs_a, xs_b, tile_expert, tile_valid, src, w_gate, w_up, w_down, te=te, layer=layer)
    out_a = _combine(h_a, rec_a, dst[:t_a], ys, p_a, g_ple, wgate_bf, wproj_bf, tm=tm_a, layer=layer)
    out_b = _combine(h_b, rec_b, dst[t_a:], ys, p_b, g_ple, wgate_bf, wproj_bf, tm=tm_b, layer=layer)
    return out_a, out_b


def _bucket_np(dist):
    max_exact = N_BUCKETS // 2
    d_f = np.maximum(dist, 1).astype(np.float32)
    large = max_exact + (np.log(d_f / np.float32(max_exact)) / np.float32(np.log(MAX_DISTANCE / max_exact))
                         * np.float32(N_BUCKETS - max_exact)).astype(np.int32)
    large = np.minimum(large, N_BUCKETS - 1)
    return np.where(dist < max_exact, dist, large).astype(np.int32)


def _bias_from_buckets(rel_bias, bucket, valid):
    onehot = (jnp.asarray(bucket)[..., None] == jnp.arange(N_BUCKETS, dtype=jnp.int32)).astype(F32)
    bias = jnp.einsum("...k,kh->h...", onehot, rel_bias.astype(F32), precision=HIGHEST)
    return jnp.where(jnp.asarray(valid)[None], bias, NEG)


def _block_order(dil):
    g = np.arange(Q_BLOCK) // SUBLANES
    j = np.arange(Q_BLOCK) % SUBLANES
    if dil == 1:
        return 16 * j + g
    if dil == 4:
        return 32 * (g // 4) + 4 * j + g % 4
    return SUBLANES * g + j


def _band_bias(rel_bias, dil):
    mu = _block_order(dil)
    qi = mu[:, None] + Q_BLOCK
    ki = np.concatenate([mu, mu + Q_BLOCK])[None, :]
    off = qi - ki
    valid = (off >= 0) & (off <= N_KEYS)
    bucket = _bucket_np(dil * np.clip(off, 0, N_KEYS))
    first = valid & (np.arange(2 * Q_BLOCK)[None, :] >= Q_BLOCK)
    tables = [_bias_from_buckets(rel_bias, bucket, v).reshape(N_HEADS * Q_BLOCK, 2 * Q_BLOCK) for v in (valid, first)]
    return jnp.stack(tables)


def _sample_tables(rel_bias, w_buf):
    qpos = w_buf + np.arange(SAMPLE_T)[:, None]
    pos = np.arange(w_buf + NEW_COLS)[None, :]
    dist = qpos - pos
    in_seq = (dist >= 0) & (pos < w_buf + SAMPLE_T)
    mult = np.zeros(dist.shape, np.float32)
    for (w, d) in PATTERNS:
        mult += in_seq & (dist % d == 0) & (dist <= w)
    bucket = _bucket_np(np.maximum(dist, 0))
    bias = jnp.transpose(_bias_from_buckets(rel_bias, bucket, mult > 0), (1, 0, 2))
    rows = SAMPLE_T * N_HEADS
    mult_rows = np.broadcast_to(mult[:, None, :], (SAMPLE_T, N_HEADS, mult.shape[-1]))
    return bias.reshape(rows, -1), jnp.asarray(mult_rows.reshape(rows, -1))


PROJ_TM = 512
PERM_ROWS = 256
OUT_TM = 512
PROMPT_TM = 256
PROMPT_TE = 512
ATTN_SUB = 4


def _row_perm(tm):
    a = np.arange(tm)
    src = (a // Q_BLOCK) * Q_BLOCK + 16 * (a % SUBLANES) + (a % Q_BLOCK) // SUBLANES
    perm = np.zeros((tm, tm), np.float32)
    perm[a, src] = 1.0
    return perm


def kernel(x_prompt, x_sample, cache_k, cache_v, state_conv, p_prompt, p_sample, rel_bias, g_mix, w_in, q_gain,
           k_gain, conv_w, g_out_att, g_out_conv, w_out, g_ffn, w_router_group, w_router_expert, w_gate, w_up,
           w_down, g_ple, w_ple_gate, w_ple_proj):
    depth = w_in.shape[0]
    batch, seq, _ = x_prompt.shape
    dec_b, dec_t, _ = x_sample.shape
    w_buf = cache_k.shape[2]
    n_s = dec_b * dec_t
    keep = min(w_buf, seq)
    assert dec_t == SAMPLE_T and w_buf % LANES == 0
    assert seq % (Q_BLOCK * 16 * 2) == 0 and keep % PROJ_TM == 0 and seq % PROMPT_TM == 0
    cache_kt = jnp.transpose(cache_k, (0, 1, 3, 4, 2))
    cache_vt = jnp.transpose(cache_v, (0, 1, 3, 4, 2))

    row = lambda a: a.reshape(1, -1)
    src_lane = np.arange(LANES)
    expand = jnp.asarray((src_lane[:, None] // LSE_LANES_PER_HEAD == np.arange(ATT_DIM)[None, :] // HEAD_DIM)
                         & (src_lane[:, None] % LSE_LANES_PER_HEAD == 0), BF16)
    before = jnp.asarray(np.arange(LANES)[:, None] < np.arange(LANES)[None, :], BF16)
    sel_np = np.zeros((SUBLANES, LANES), np.float32)
    sel_np[0, REC_S1] = sel_np[1, REC_S2] = 1.0
    sel = jnp.asarray(sel_np, BF16)
    perm = jnp.asarray(_row_perm(PERM_ROWS), BF16)
    unperm = jnp.asarray(_row_perm(OUT_TM).T, BF16)
    band = [_band_bias(rel_bias, d) for (_, d) in PATTERNS]
    s_bias, s_mult = _sample_tables(rel_bias, w_buf)

    hp = x_prompt.reshape(batch * seq, D_MODEL)
    hs = jnp.swapaxes(x_sample, 0, 1).reshape(n_s, D_MODEL)
    pp_all = p_prompt.reshape(depth, batch * seq, D_PLE)
    ps_all = jnp.swapaxes(p_sample, 1, 2).reshape(depth, n_s, D_PLE)
    new = {k: [] for k in ("kp", "vp", "cp", "ks", "vs", "cs")}
    hist_p = jnp.zeros((batch, SUBLANES, CONV_DIM), F32)

    for l in range(depth):
        w_in_bf = w_in[l].astype(BF16)
        wa_bf = w_out[l, :ATT_DIM].astype(BF16)
        wc_bf = w_out[l, ATT_DIM:].astype(BF16)
        wgate_bf = w_ple_gate[l].astype(BF16)
        wproj_bf = w_ple_proj[l].astype(BF16)
        w_router = jnp.concatenate(
            [w_router_expert[l], w_router_group[l],
             jnp.zeros((D_MODEL, LANES - N_EXPERTS - N_GROUPS), F32)], axis=1).astype(BF16)
        qg, kg = row(jnp.tile(q_gain[l], N_HEADS)), row(jnp.tile(k_gain[l], N_HEADS))
        mix = (row(g_mix[l]), w_in_bf, qg, kg, conv_w[l], row(g_out_conv[l]))
        moe = (row(g_ffn[l]), w_router, before, sel, w_gate, w_up, w_down, row(g_ple[l]), wgate_bf, wproj_bf)

        q, k, v, k_nat, v_nat, yn, nconv = _inproj(
            hp, hist_p, *mix, perm, tm=PROJ_TM, shift=1, tiles_per_seq=seq // PROJ_TM,
            keep_tiles=keep // PROJ_TM)
        os, lses = [], []
        for bias, (_, d) in zip(band, PATTERNS):
            sub = min(ATTN_SUB, seq // (Q_BLOCK * d))
            o, lse = _attn_pattern(q, k, v, bias, batch=batch, seq=seq, dil=d, sub=sub,
                                   res=min(d, ATTN_SUB // sub))
            os.append(o)
            lses.append(lse)
        hp = _outproj(os, lses, yn, hp, row(g_out_att[l]), wa_bf, wc_bf, expand, unperm, tm=OUT_TM)
        new["kp"].append(jnp.transpose(k_nat, (0, 3, 1, 2)))
        new["vp"].append(jnp.transpose(v_nat, (0, 3, 1, 2)))
        new["cp"].append(nconv[:, SUBLANES - 2:])

        hist_s = jnp.swapaxes(state_conv[l], 0, 1).reshape(1, 2 * dec_b, CONV_DIM)
        q, k, v, yn, nconv = _inproj(hs, hist_s, *mix, tm=n_s, shift=dec_b, tiles_per_seq=1)
        bmaj = lambda a: jnp.swapaxes(a.reshape(dec_t, dec_b, N_HEADS, HEAD_DIM), 0, 1)
        qb, kb, vb = bmaj(q), bmaj(k), bmaj(v)
        rows8 = lambda a: jnp.pad(a.reshape(dec_b, dec_t, ATT_DIM), ((0, 0), (0, SUBLANES - dec_t), (0, 0)))
        att = _attn_sample(qb.reshape(dec_b, dec_t, ATT_DIM), rows8(kb), rows8(vb), cache_kt, cache_vt, l,
                           s_bias, s_mult)
        att_tm = jnp.swapaxes(att, 0, 1).reshape(n_s, ATT_DIM)
        hs = _outproj([att_tm], [], yn, hs, row(g_out_att[l]), wa_bf, wc_bf, tm=n_s)
        hp, hs = _moe_ple(hp, hs, pp_all, ps_all, *moe, tm_a=PROMPT_TM, tm_b=n_s, te=PROMPT_TE, layer=l)
        new["ks"].append(kb)
        new["vs"].append(vb)
        new["cs"].append(jnp.swapaxes(nconv.reshape(2, dec_b, CONV_DIM), 0, 1))

    y_prompt = hp.reshape(batch, seq, D_MODEL)
    y_sample = jnp.swapaxes(hs.reshape(dec_t, dec_b, D_MODEL), 0, 1)
    st = lambda key: jnp.stack(new[key])
    return (y_prompt, y_sample, st("kp"), st("vp"), st("cp"), st("ks"), st("vs"), st("cs"))
```

```python
import functools

import jax
import jax.numpy as jnp
import numpy as np
from jax import lax
from jax.experimental import pallas as pl
from jax.experimental.pallas import tpu as pltpu

F32 = jnp.float32
BF16 = jnp.bfloat16
HIGHEST = lax.Precision.HIGHEST

D_MODEL = 1024
HEAD_DIM = 64
N_HEADS = 8
ATT_DIM = N_HEADS * HEAD_DIM
CONV_DIM = D_MODEL - ATT_DIM
MIX_IN = 3 * ATT_DIM + 3 * CONV_DIM
PATTERNS = ((128, 1), (512, 4), (2048, 16))
N_KEYS = 128
Q_BLOCK = 128
N_BUCKETS = 32
MAX_DISTANCE = 2048
N_GROUPS = 4
EXPERTS_PER_GROUP = 8
N_EXPERTS = N_GROUPS * EXPERTS_PER_GROUP
D_EXPERT = 256
D_PLE = 256
EPS = 1e-6
NEG = -1e30

LANES = 128
SUBLANES = 8
SLABS = Q_BLOCK // SUBLANES
LSE_LANES_PER_HEAD = LANES // N_HEADS
VMEM_LIMIT = 56 * 1024 * 1024
NT = (((1,), (1,)), ((), ()))


def _cparams(n_axes):
    return pltpu.CompilerParams(dimension_semantics=("arbitrary",) * n_axes,
                                vmem_limit_bytes=VMEM_LIMIT)


def _full(shape):
    n = len(shape)
    return pl.BlockSpec(shape, lambda *_: (0,) * n)


def _rms(x, gain):
    ms = jnp.mean(x * x, axis=-1, keepdims=True)
    return x * lax.rsqrt(ms + EPS) * gain


def _inproj_kernel(*refs, shift, tiles_per_seq, permute):
    (h_ref, gmix_ref, w_ref, qg_ref, kg_ref, cw_ref, gconv_ref, hist_ref) = refs[:8]
    if permute:
        perm_ref, q_ref, k_ref, v_ref, kn_ref, vn_ref, yn_ref, nconv_ref, carry_ref = refs[8:]
    else:
        q_ref, k_ref, v_ref, yn_ref, nconv_ref, carry_ref = refs[8:]
    i = pl.program_id(0)
    a = _rms(h_ref[...], gmix_ref[...])
    proj = jnp.dot(a.astype(BF16), w_ref[...], preferred_element_type=F32)
    tm = proj.shape[0]

    lower = lax.broadcasted_iota(jnp.int32, (tm, LANES), 1) < HEAD_DIM

    def head_norm(t, g):
        out = []
        for j in range(ATT_DIM // LANES):
            blk = t[:, j * LANES:(j + 1) * LANES]
            sq = blk * blk
            ms_lo = jnp.sum(jnp.where(lower, sq, 0.0), axis=-1, keepdims=True) * (1.0 / HEAD_DIM)
            ms_hi = jnp.sum(jnp.where(lower, 0.0, sq), axis=-1, keepdims=True) * (1.0 / HEAD_DIM)
            scale = jnp.where(lower, lax.rsqrt(ms_lo + EPS), lax.rsqrt(ms_hi + EPS))
            out.append(blk * scale)
        return jnp.concatenate(out, axis=-1) * g

    q = head_norm(proj[:, 0:ATT_DIM], qg_ref[...])
    k = head_norm(proj[:, ATT_DIM:2 * ATT_DIM], kg_ref[...])
    v = proj[:, 2 * ATT_DIM:3 * ATT_DIM]
    if permute:
        kn_ref[...] = k.T.reshape(N_HEADS, HEAD_DIM, tm)
        vn_ref[...] = v.T.reshape(N_HEADS, HEAD_DIM, tm)
        qkv = jnp.concatenate([q * (HEAD_DIM ** -0.5), k, v], axis=-1).astype(BF16)
        pm = perm_ref.shape[0]
        moved = jnp.concatenate([jnp.dot(perm_ref[...], qkv[r:r + pm], preferred_element_type=F32)
                                 for r in range(0, tm, pm)], axis=0)
        q_ref[...] = moved[:, 0:ATT_DIM]
        k_ref[...] = moved[:, ATT_DIM:2 * ATT_DIM]
        v_ref[...] = moved[:, 2 * ATT_DIM:3 * ATT_DIM]
    else:
        q_ref[...] = q
        k_ref[...] = k
        v_ref[...] = v
    c0 = 3 * ATT_DIM
    hc = proj[:, c0:c0 + CONV_DIM]
    gb = proj[:, c0 + CONV_DIM:c0 + 2 * CONV_DIM]
    gc = proj[:, c0 + 2 * CONV_DIM:c0 + 3 * CONV_DIM]
    u = gc * hc

    if shift == 1:
        @pl.when(i % tiles_per_seq == 0)
        def _():
            carry_ref[...] = hist_ref[0]
        h0 = carry_ref[SUBLANES - 2:SUBLANES - 1, :]
        h1 = carry_ref[SUBLANES - 1:SUBLANES, :]
        row = lax.broadcasted_iota(jnp.int32, (tm, 1), 0)
        u1 = jnp.where(row == 0, h1, pltpu.roll(u, 1, 0))
        u2 = jnp.where(row == 0, h0, jnp.where(row == 1, h1, pltpu.roll(u, 2, 0)))
        carry_ref[...] = u[tm - SUBLANES:tm, :]
        nconv_ref[0] = u[tm - SUBLANES:tm, :]
    else:
        hist = hist_ref[0]
        u1 = jnp.concatenate([hist[shift:2 * shift], u[0:tm - shift]], axis=0)
        u2 = jnp.concatenate([hist, u[0:tm - 2 * shift]], axis=0)
        nconv_ref[0] = u[tm - 2 * shift:tm, :]
    conv = cw_ref[0:1, :] * u2 + cw_ref[1:2, :] * u1 + cw_ref[2:3, :] * u
    yn_ref[...] = _rms(gb * conv, gconv_ref[...])


def _inproj(h, hist, g_mix, w_in_bf, q_gain, k_gain, conv_w, g_out_conv, perm=None, *,
            tm, shift, tiles_per_seq, keep_tiles=0):
    n = h.shape[0]
    hist_rows = hist.shape[1]
    nseq = hist.shape[0]
    tok = lambda w: pl.BlockSpec((tm, w), lambda i: (i, 0))
    seq3 = lambda r: pl.BlockSpec((1, r, CONV_DIM), lambda i: (i // tiles_per_seq, 0, 0))
    nconv_rows = SUBLANES if shift == 1 else 2 * shift
    att = jax.ShapeDtypeStruct((n, ATT_DIM), F32)
    in_specs = [tok(D_MODEL), _full((1, D_MODEL)), _full((D_MODEL, MIX_IN)), _full((1, ATT_DIM)),
                _full((1, ATT_DIM)), _full((3, CONV_DIM)), _full((1, CONV_DIM)), seq3(hist_rows)]
    args = [h, g_mix, w_in_bf, q_gain, k_gain, conv_w, g_out_conv, hist]
    out_specs = [tok(ATT_DIM)] * 3
    out_shape = [att] * 3
    if perm is not None:
        in_specs.append(_full(perm.shape))
        args.append(perm)
        first = tiles_per_seq - keep_tiles
        kept = pl.BlockSpec((None, N_HEADS, HEAD_DIM, tm),
                            lambda i: (i // tiles_per_seq, 0, 0, jnp.maximum(i % tiles_per_seq - first, 0)))
        out_specs += [kept, kept]
        out_shape += [jax.ShapeDtypeStruct((nseq, N_HEADS, HEAD_DIM, keep_tiles * tm), F32)] * 2
    out_specs += [tok(CONV_DIM), seq3(nconv_rows)]
    out_shape += [jax.ShapeDtypeStruct((n, CONV_DIM), F32),
                  jax.ShapeDtypeStruct((nseq, nconv_rows, CONV_DIM), F32)]
    return pl.pallas_call(
        functools.partial(_inproj_kernel, shift=shift, tiles_per_seq=tiles_per_seq, permute=perm is not None),
        grid=(n // tm,),
        in_specs=in_specs, out_specs=out_specs, out_shape=out_shape,
        scratch_shapes=[pltpu.VMEM((SUBLANES, CONV_DIM), F32)],
        compiler_params=_cparams(1),
        name="inproj",
    )(*args)


def _attn_block(q, kk, vv, bias):
    lane = lax.broadcasted_iota(jnp.int32, (Q_BLOCK, LANES), 1)
    upper = lane >= HEAD_DIM
    scores = []
    for h in range(N_HEADS):
        j, e = divmod(h, 2)
        qp = q[:, j * LANES:(j + 1) * LANES]
        qm = (jnp.where(upper, qp, 0.0) if e else jnp.where(upper, 0.0, qp)).astype(BF16)
        scores.append(lax.dot_general(qm, kk[:, j * LANES:(j + 1) * LANES], NT, preferred_element_type=F32))
    s = jnp.concatenate(scores, axis=0) + bias
    m = jnp.max(s, axis=-1, keepdims=True)
    p = jnp.exp(s - m)
    den = jnp.sum(p, axis=-1, keepdims=True)
    pb = p.astype(BF16)
    head_rows = lambda t, h: t[h * Q_BLOCK:(h + 1) * Q_BLOCK]
    lse_grp = lane // LSE_LANES_PER_HEAD
    m_tile = jnp.zeros((Q_BLOCK, LANES), F32)
    den_tile = jnp.ones((Q_BLOCK, LANES), F32)
    outs = []
    for j in range(N_HEADS // 2):
        even, odd = 2 * j, 2 * j + 1
        vp = vv[:, j * LANES:(j + 1) * LANES]
        pair = jnp.where(upper, jnp.dot(head_rows(pb, odd), vp, preferred_element_type=F32),
                         jnp.dot(head_rows(pb, even), vp, preferred_element_type=F32))
        outs.append(pair / jnp.where(upper, head_rows(den, odd), head_rows(den, even)))
        for h in (even, odd):
            m_tile = jnp.where(lse_grp == h, head_rows(m, h), m_tile)
            den_tile = jnp.where(lse_grp == h, head_rows(den, h), den_tile)
    return jnp.concatenate(outs, axis=-1), m_tile + jnp.log(den_tile)


def _attn_kernel(q_ref, kp_ref, kc_ref, vp_ref, vc_ref, bias_ref, o_ref, lse_ref, kbuf, vbuf, *, sub, res):
    n = pl.program_id(2)
    rows = sub * Q_BLOCK
    for r in range(res):
        kbuf[0:Q_BLOCK, :] = kp_ref[:, r].reshape(Q_BLOCK, ATT_DIM).astype(BF16)
        kbuf[Q_BLOCK:, :] = kc_ref[:, :, r].reshape(rows, ATT_DIM).astype(BF16)
        vbuf[0:Q_BLOCK, :] = vp_ref[:, r].reshape(Q_BLOCK, ATT_DIM).astype(BF16)
        vbuf[Q_BLOCK:, :] = vc_ref[:, :, r].reshape(rows, ATT_DIM).astype(BF16)
        for j in range(sub):
            q = q_ref[j, :, r].reshape(Q_BLOCK, ATT_DIM)
            r0 = j * Q_BLOCK
            first = (n == 0).astype(jnp.int32) if j == 0 else 0
            o, lse = _attn_block(q, kbuf[r0:r0 + 2 * Q_BLOCK, :], vbuf[r0:r0 + 2 * Q_BLOCK, :], bias_ref[first])
            o_ref[j, :, r] = o.reshape(SLABS, SUBLANES, ATT_DIM)
            lse_ref[j, :, r] = lse.reshape(SLABS, SUBLANES, LANES)


def _attn_pattern(q, k, v, bias, *, batch, seq, dil, sub, res):
    nblk = seq // (Q_BLOCK * dil)
    view = lambda t: t.reshape(batch, nblk, SLABS, dil, SUBLANES, t.shape[-1])
    cur = lambda c: pl.BlockSpec((None, sub, SLABS, res, SUBLANES, c), lambda b, r, n: (b, n, 0, r, 0, 0))
    prev = pl.BlockSpec((None, None, SLABS, res, SUBLANES, ATT_DIM),
                        lambda b, r, n: (b, jnp.maximum(n * sub - 1, 0), 0, r, 0, 0))
    o, lse = pl.pallas_call(
        functools.partial(_attn_kernel, sub=sub, res=res),
        grid=(batch, dil // res, nblk // sub),
        in_specs=[cur(ATT_DIM), prev, cur(ATT_DIM), prev, cur(ATT_DIM),
                  _full((2, N_HEADS * Q_BLOCK, 2 * Q_BLOCK))],
        out_specs=[cur(ATT_DIM), cur(LANES)],
        out_shape=[jax.ShapeDtypeStruct((batch, nblk, SLABS, dil, SUBLANES, ATT_DIM), F32),
                   jax.ShapeDtypeStruct((batch, nblk, SLABS, dil, SUBLANES, LANES), F32)],
        scratch_shapes=[pltpu.VMEM(((sub + 1) * Q_BLOCK, ATT_DIM), BF16),
                        pltpu.VMEM(((sub + 1) * Q_BLOCK, ATT_DIM), BF16)],
        compiler_params=_cparams(3),
        name=f"attn_d{dil}",
    )(view(q), view(k), view(k), view(v), view(v), bias)
    return o.reshape(batch * seq, ATT_DIM), lse.reshape(batch * seq, LANES)


SAMPLE_T = 4
NEW_COLS = LANES


def _attn_sample_kernel(q_ref, kt_ref, kn_ref, vt_ref, vn_ref, bias_ref, mult_ref, o_ref):
    rows = SAMPLE_T * N_HEADS
    q4 = q_ref[...] * (HEAD_DIM ** -0.5)
    qt = jnp.concatenate([jnp.broadcast_to(q4[t:t + 1, :], (N_HEADS, ATT_DIM)) for t in range(SAMPLE_T)], axis=0)
    lane_head = lax.broadcasted_iota(jnp.int32, (rows, ATT_DIM), 1) // HEAD_DIM
    row_head = lax.broadcasted_iota(jnp.int32, (rows, ATT_DIM), 0) % N_HEADS
    own = lane_head == row_head
    qbd = jnp.where(own, qt, 0.0).astype(BF16)
    flat = lambda ref: ref[...].reshape(ATT_DIM, ref.shape[-1]).astype(BF16)
    pad = jnp.zeros((NEW_COLS - SUBLANES, ATT_DIM), F32)
    new_rows = lambda ref: jnp.concatenate([ref[...], pad], axis=0).astype(BF16)
    s = jnp.concatenate([jnp.dot(qbd, flat(kt_ref), preferred_element_type=F32),
                         lax.dot_general(qbd, new_rows(kn_ref), NT, preferred_element_type=F32)],
                        axis=-1) + bias_ref[...]
    m = jnp.max(s, axis=-1, keepdims=True)
    p = jnp.exp(s - m) * mult_ref[...]
    den = jnp.sum(p, axis=-1, keepdims=True)
    pb = p.astype(BF16)
    w_buf = kt_ref.shape[-1]
    acc = (lax.dot_general(pb[:, :w_buf], flat(vt_ref), NT, preferred_element_type=F32)
           + jnp.dot(pb[:, w_buf:], new_rows(vn_ref), preferred_element_type=F32))
    acc = jnp.where(own, acc / den, 0.0)
    for t in range(SAMPLE_T):
        o_ref[t:t + 1, :] = jnp.sum(acc[t * N_HEADS:(t + 1) * N_HEADS, :], axis=0, keepdims=True)


def _attn_sample(q, k_new, v_new, cache_kt, cache_vt, layer, bias, mult):
    nb = q.shape[0]
    w_buf = cache_kt.shape[-1]
    tok = pl.BlockSpec((None, SAMPLE_T, ATT_DIM), lambda b: (b, 0, 0))
    new = pl.BlockSpec((None, SUBLANES, ATT_DIM), lambda b: (b, 0, 0))
    old = pl.BlockSpec((None, None, N_HEADS, HEAD_DIM, w_buf), lambda b: (layer, b, 0, 0, 0))
    tbl = _full((SAMPLE_T * N_HEADS, w_buf + NEW_COLS))
    return pl.pallas_call(
        _attn_sample_kernel,
        grid=(nb,),
        in_specs=[tok, old, new, old, new, tbl, tbl],
        out_specs=tok,
        out_shape=jax.ShapeDtypeStruct((nb, SAMPLE_T, ATT_DIM), F32),
        compiler_params=_cparams(1),
        name="attn_sample",
    )(q, cache_kt, k_new, cache_vt, v_new, bias, mult)


def _split_dot(x, e_ref):
    hi = x.astype(BF16)
    lo = (x - hi.astype(F32)).astype(BF16)
    return (jnp.dot(hi, e_ref[...], preferred_element_type=F32)
            + jnp.dot(lo, e_ref[...], preferred_element_type=F32))


def _outproj_kernel(*refs, n_pat):
    mix = n_pat > 1
    n_lse = n_pat if mix else 0
    o_refs = refs[0:n_pat]
    l_refs = refs[n_pat:n_pat + n_lse]
    rest = refs[n_pat + n_lse:]
    if mix:
        yn_ref, h_ref, gatt_ref, exp_ref, unperm_ref, wa_ref, wc_ref, out_ref = rest
        lses = [r[...] for r in l_refs]
        top = functools.reduce(jnp.maximum, lses)
        ws = [jnp.exp(l - top) for l in lses]
        tot = functools.reduce(lambda a, b: a + b, ws)
        att = None
        for w, o_ref in zip(ws, o_refs):
            term = _split_dot(w / tot, exp_ref) * o_ref[...]
            att = term if att is None else att + term
        att_bf = jnp.dot(unperm_ref[...], _rms(att, gatt_ref[...]).astype(BF16),
                         preferred_element_type=F32).astype(BF16)
    else:
        yn_ref, h_ref, gatt_ref, wa_ref, wc_ref, out_ref = rest
        att_bf = _rms(o_refs[0][...], gatt_ref[...]).astype(BF16)
    y = (jnp.dot(att_bf, wa_ref[...], preferred_element_type=F32)
         + jnp.dot(yn_ref[...].astype(BF16), wc_ref[...], preferred_element_type=F32))
    out_ref[...] = h_ref[...] + y


def _outproj(os, lses, yn, h, g_att, wa_bf, wc_bf, expand=None, unperm=None, *, tm):
    n = h.shape[0]
    n_pat = len(os)
    tok = lambda w: pl.BlockSpec((tm, w), lambda i: (i, 0))
    in_specs = [tok(ATT_DIM)] * n_pat + [tok(LANES)] * len(lses) + [tok(CONV_DIM), tok(D_MODEL), _full((1, ATT_DIM))]
    args = [*os, *lses, yn, h, g_att]
    if n_pat > 1:
        in_specs += [_full((LANES, ATT_DIM)), _full((tm, tm))]
        args += [expand, unperm]
    in_specs += [_full((ATT_DIM, D_MODEL)), _full((CONV_DIM, D_MODEL))]
    args += [wa_bf, wc_bf]
    return pl.pallas_call(
        functools.partial(_outproj_kernel, n_pat=n_pat),
        grid=(n // tm,),
        in_specs=in_specs,
        out_specs=tok(D_MODEL),
        out_shape=jax.ShapeDtypeStruct((n, D_MODEL), F32),
        compiler_params=_cparams(1),
        name="outproj",
    )(*args)


GROUP_LANE0 = N_EXPERTS
REC_E1, REC_E2, REC_S1, REC_S2, REC_W1, REC_W2 = range(6)
PACKED = D_MODEL // 2
ROW_W = PACKED + LANES
N_DMA_PRIORITIES = 2
FETCH_PARTS = 4
LIST_LANES = LANES
LIST_COUNT = LIST_LANES - 1
LIST_NEXT = LIST_LANES - 2
LIST_PARITY = LIST_LANES - 3


def _slots(tm):
    need = 2 * tm + (SUBLANES - 1) * N_EXPERTS + SUBLANES
    return -(-need // LANES) * LANES


def _pieces(x):
    hi = x.astype(BF16)
    r1 = x - hi.astype(F32)
    mid = r1.astype(BF16)
    return hi, mid, (r1 - mid.astype(F32)).astype(BF16)


def _sort_kernel(h_ref, g_ref, wr_ref, before_ref, sel_ref, xs_ref, rec_ref, cnt_ref, *, group):
    tm = h_ref.shape[0] // group
    slots = xs_ref.shape[0] // group
    for t in range(group):
        xs, rec, cnt = _sort_tile(h_ref[t * tm:(t + 1) * tm, :], g_ref, wr_ref, before_ref, sel_ref, slots)
        xs_ref[t * slots:(t + 1) * slots, :] = xs
        rec_ref[t * tm:(t + 1) * tm, :] = rec
        cnt_ref[t] = cnt


def _sort_tile(h, g_ref, wr_ref, before_ref, sel_ref, slots):
    m = _rms(h, g_ref[...])
    logits = jnp.dot(m.astype(BF16), wr_ref[...], preferred_element_type=F32)
    tm = logits.shape[0]
    lane_i = lax.broadcasted_iota(jnp.int32, (tm, LANES), 1)
    lane = lane_i.astype(F32)
    big = jnp.float32(4 * LANES)

    is_g = jnp.logical_and(lane_i >= GROUP_LANE0, lane_i < GROUP_LANE0 + N_GROUPS)
    gl = jnp.where(is_g, logits, NEG)
    gmax = jnp.max(gl, axis=-1, keepdims=True)
    g_w = 1.0 / jnp.sum(jnp.where(is_g, jnp.exp(gl - gmax), 0.0), axis=-1, keepdims=True)
    g_sel = jnp.min(jnp.where(gl == gmax, lane - GROUP_LANE0, big), axis=-1, keepdims=True)

    grp_of_lane = (lane_i // EXPERTS_PER_GROUP).astype(F32)
    in_grp = jnp.logical_and(lane_i < N_EXPERTS, grp_of_lane == g_sel)
    el = jnp.where(in_grp, logits, NEG)
    t1 = jnp.max(el, axis=-1, keepdims=True)
    e1 = jnp.min(jnp.where(el == t1, lane, big), axis=-1, keepdims=True)
    el2 = jnp.where(lane == e1, NEG, el)
    t2 = jnp.max(el2, axis=-1, keepdims=True)
    e2 = jnp.min(jnp.where(el2 == t2, lane, big), axis=-1, keepdims=True)
    ex = jnp.exp(t2 - t1)
    w1 = g_w / (1.0 + ex)
    w2 = g_w * ex / (1.0 + ex)

    hit1 = lane == e1
    hit2 = lane == e2
    c = jnp.where(hit1, 1.0, jnp.where(hit2, 1.0, 0.0))
    rr = lax.broadcasted_iota(jnp.int32, (tm, tm), 0)
    cc = lax.broadcasted_iota(jnp.int32, (tm, tm), 1)
    lower = jnp.where(rr > cc, 1.0, 0.0).astype(BF16)
    rank = jnp.dot(lower, c.astype(BF16), preferred_element_type=F32)
    cnt = jnp.sum(c, axis=0, keepdims=True)
    chunks = jnp.floor((cnt + (SUBLANES - 1)) * (1.0 / SUBLANES))
    start = SUBLANES * jnp.dot(jnp.broadcast_to(chunks, (SUBLANES, LANES)).astype(BF16), before_ref[...],
                               preferred_element_type=F32)[0:1, :]
    slot_of = rank + start
    s1 = jnp.sum(jnp.where(hit1, slot_of, 0.0), axis=-1, keepdims=True)
    s2 = jnp.sum(jnp.where(hit2, slot_of, 0.0), axis=-1, keepdims=True)

    rec = jnp.zeros((tm, LANES), F32)
    for idx, val in ((REC_E1, e1), (REC_E2, e2), (REC_S1, s1), (REC_S2, s2), (REC_W1, w1), (REC_W2, w2)):
        rec = jnp.where(lane_i == idx, val, rec)

    rec_parts = _pieces(rec)[:2]
    srow = sum(lax.dot_general(sel_ref[...], part, NT, preferred_element_type=F32) for part in rec_parts)
    slot_id = lax.broadcasted_iota(jnp.int32, (slots, tm), 0).astype(F32)
    place = jnp.where(slot_id == srow[0:1, :], 1.0, jnp.where(slot_id == srow[1:2, :], 1.0, 0.0)).astype(BF16)
    payload = jnp.concatenate([m.astype(BF16), *rec_parts], axis=-1)
    moved = jnp.dot(place, payload, preferred_element_type=F32)
    info = moved[:, D_MODEL:D_MODEL + LANES] + moved[:, D_MODEL + LANES:]
    xs = jnp.concatenate([_pack_bf16(moved[:, :D_MODEL], is_bf16=True), pltpu.bitcast(info, jnp.uint32)], axis=-1)
    return xs, rec, jnp.broadcast_to(cnt, (SUBLANES, LANES))


def _sort(h, g_ffn, w_router, before, sel, *, tm):
    n = h.shape[0]
    slots = _slots(tm)
    group = 2 if (n // tm) % 2 == 0 else 1
    tok = lambda w: pl.BlockSpec((group * tm, w), lambda i: (i, 0))
    return pl.pallas_call(
        functools.partial(_sort_kernel, group=group),
        grid=(n // (group * tm),),
        in_specs=[tok(D_MODEL), _full((1, D_MODEL)), _full((D_MODEL, LANES)), _full((LANES, LANES)),
                  _full((SUBLANES, LANES))],
        out_specs=[pl.BlockSpec((group * slots, ROW_W), lambda i: (i, 0)), tok(LANES),
                   pl.BlockSpec((group, SUBLANES, LANES), lambda i: (i, 0, 0))],
        out_shape=[jax.ShapeDtypeStruct((n // tm * slots, ROW_W), jnp.uint32), jax.ShapeDtypeStruct((n, LANES), F32),
                   jax.ShapeDtypeStruct((n // tm, SUBLANES, LANES), F32)],
        compiler_params=_cparams(1),
        name="moe_sort",
    )(h, g_ffn, w_router, before, sel)


def _chunk_copy(src_hbm, row, dst, c, sem):
    if not isinstance(row, int):
        row = pl.multiple_of(row, SUBLANES)
    to = c * SUBLANES
    if not isinstance(to, int):
        to = pl.multiple_of(to, SUBLANES)
    return pltpu.make_async_copy(src_hbm.at[pl.ds(row, SUBLANES), :], dst.at[pl.ds(to, SUBLANES), :], sem)


def _expert_kernel(te_ref, tv_ref, lst_ref, xa_ref, xb_ref, wg_ref, wu_ref, wd_ref, y_ref,
                    xbuf, sem, wg_st, wu_st, wd_st, wsem, wg_bf, wu_bf, wd_bf, *, rows_a, layer):
    i = pl.program_id(0)
    n = pl.num_programs(0)
    te = xbuf.shape[1]
    slot = i % 2

    def fetch(tile, to_slot):
        n_a = lst_ref[tile, LIST_COUNT]

        def from_a(c, carry):
            priority = c % N_DMA_PRIORITIES if isinstance(c, int) else 0
            _chunk_copy(xa_ref, lst_ref[tile, c], xbuf.at[to_slot], c, sem.at[to_slot]).start(priority=priority)
            return carry

        def from_b(c, carry):
            _chunk_copy(xb_ref, lst_ref[tile, c] - rows_a, xbuf.at[to_slot], c, sem.at[to_slot]).start()
            return carry

        per_tile = te // SUBLANES

        @pl.when(n_a == per_tile)
        def _():
            for c in range(per_tile):
                from_a(c, 0)

        @pl.when(n_a != per_tile)
        def _():
            lax.fori_loop(0, n_a, from_a, 0)
            lax.fori_loop(n_a, per_tile, from_b, 0)

    @pl.when(jnp.logical_and(i == 0, tv_ref[0] > 0))
    def _():
        fetch(0, 0)

    nxt = jnp.minimum(i + 1, n - 1)

    @pl.when(jnp.logical_and(i + 1 < n, tv_ref[nxt] > 0))
    def _():
        fetch(nxt, 1 - slot)

    valid = tv_ref[i]
    changed = jnp.logical_or(i == 0, te_ref[i] != te_ref[jnp.maximum(i - 1, 0)])

    def weight_copies(expert, s):
        return [pltpu.make_async_copy(src.at[layer, expert], dst.at[s], wsem.at[s])
                for src, dst in ((wg_ref, wg_st), (wu_ref, wu_st), (wd_ref, wd_st))]

    @pl.when(jnp.logical_and(changed, valid > 0))
    def _():
        s = lst_ref[i, LIST_PARITY]

        @pl.when(i == 0)
        def _():
            for cp in weight_copies(te_ref[0], 0):
                cp.start()

        for cp in weight_copies(te_ref[i], s):
            cp.wait()
        wg_bf[...] = wg_st[s].astype(BF16)
        wu_bf[...] = wu_st[s].astype(BF16)
        wd_bf[...] = wd_st[s].astype(BF16)
        nxt_expert = lst_ref[i, LIST_NEXT]

        @pl.when(nxt_expert >= 0)
        def _():
            for cp in weight_copies(nxt_expert, 1 - s):
                cp.start()

    @pl.when(valid > 0)
    def _():
        pltpu.make_async_copy(xbuf.at[slot], xbuf.at[slot], sem.at[slot]).wait()
        rows = xbuf[slot]
        x = _unpack_bf16(rows[:, :PACKED])
        info = pltpu.bitcast(rows[:, PACKED:], F32)
        mine = info[:, REC_E1:REC_E1 + 1] == te_ref[i].astype(F32)
        gate = jnp.where(mine, info[:, REC_W1:REC_W1 + 1], info[:, REC_W2:REC_W2 + 1])
        hg = jnp.dot(x, wg_bf[...], preferred_element_type=F32)
        hu = jnp.dot(x, wu_bf[...], preferred_element_type=F32)
        hid = (hg * jax.nn.sigmoid(hg)) * hu * gate
        y = jnp.dot(hid.astype(BF16), wd_bf[...], preferred_element_type=F32)
        y_ref[...] = _pack_bf16(y)

    @pl.when(valid <= 0)
    def _():
        y_ref[...] = jnp.zeros_like(y_ref)


def _pack_bf16(x, is_bf16=False):
    w = x.shape[-1] // 2
    bits = lambda a: pltpu.bitcast(a if is_bf16 else a.astype(BF16).astype(F32), jnp.uint32)
    return jnp.bitwise_or(lax.shift_right_logical(bits(x[:, :w]), jnp.uint32(16)),
                          jnp.bitwise_and(bits(x[:, w:]), jnp.uint32(0xFFFF0000)))


def _unpack_bf16(words):
    as_f32 = lambda a: pltpu.bitcast(a, F32)
    return jnp.concatenate([as_f32(lax.shift_left(words, jnp.uint32(16))),
                            as_f32(jnp.bitwise_and(words, jnp.uint32(0xFFFF0000)))], axis=-1).astype(BF16)


def _experts(xs_a, xs_b, tile_expert, tile_valid, src, w_gate, w_up, w_down, *, te, layer):
    n_tiles = tile_expert.shape[0]
    hbm = pl.BlockSpec(memory_space=pl.ANY)
    grid_spec = pltpu.PrefetchScalarGridSpec(
        num_scalar_prefetch=3,
        grid=(n_tiles,),
        in_specs=[hbm, hbm, hbm, hbm, hbm],
        out_specs=pl.BlockSpec((te, PACKED), lambda i, e, v, s: (i, 0)),
        scratch_shapes=[pltpu.VMEM((2, te, ROW_W), jnp.uint32), pltpu.SemaphoreType.DMA((2,)),
                        pltpu.VMEM((2, D_MODEL, D_EXPERT), F32), pltpu.VMEM((2, D_MODEL, D_EXPERT), F32),
                        pltpu.VMEM((2, D_EXPERT, D_MODEL), F32), pltpu.SemaphoreType.DMA((2,)),
                        pltpu.VMEM((D_MODEL, D_EXPERT), BF16), pltpu.VMEM((D_MODEL, D_EXPERT), BF16),
                        pltpu.VMEM((D_EXPERT, D_MODEL), BF16)])
    return pl.pallas_call(
        functools.partial(_expert_kernel, rows_a=xs_a.shape[0], layer=layer),
        grid_spec=grid_spec,
        out_shape=jax.ShapeDtypeStruct((n_tiles * te, PACKED), jnp.uint32),
        compiler_params=_cparams(1),
        name="experts",
    )(tile_expert, tile_valid, src.reshape(n_tiles, LIST_LANES), xs_a, xs_b, w_gate, w_up, w_down)


def _combine_kernel(lst_ref, h_ref, rec_ref, ys_ref, p_ref, gple_ref, wgate_ref, wproj_ref, out_ref,
                     ybuf, sem):
    i = pl.program_id(0)
    n = pl.num_programs(0)
    slot = i % 2
    tm = h_ref.shape[0]
    slots = ybuf.shape[1]

    n_chunks = slots // SUBLANES
    per_part = n_chunks // FETCH_PARTS

    def fetch(tile, to_slot, part):
        for c in range(part * per_part, (part + 1) * per_part):
            _chunk_copy(ys_ref, lst_ref[tile, c], ybuf.at[to_slot], c, sem.at[to_slot]).start(
                priority=c % N_DMA_PRIORITIES)

    def prefetch(part):
        @pl.when(i + 1 < n)
        def _():
            fetch(i + 1, 1 - slot, part)

    @pl.when(i == 0)
    def _():
        for part in range(FETCH_PARTS):
            fetch(0, 0, part)

    pltpu.make_async_copy(ybuf.at[slot], ybuf.at[slot], sem.at[slot]).wait()
    prefetch(0)
    rec = rec_ref[...]
    slot_id = lax.broadcasted_iota(jnp.int32, (tm, slots), 1).astype(F32)
    back = jnp.where(slot_id == rec[:, REC_S1:REC_S1 + 1], 1.0,
                     jnp.where(slot_id == rec[:, REC_S2:REC_S2 + 1], 1.0, 0.0)).astype(BF16)
    prefetch(1)
    h2 = h_ref[...] + jnp.dot(back, _unpack_bf16(ybuf[slot]), preferred_element_type=F32)
    prefetch(2)
    gate = jax.nn.sigmoid(jnp.dot(_rms(h2, gple_ref[...]).astype(BF16), wgate_ref[...], preferred_element_type=F32))
    prefetch(3)
    ple = jnp.dot(p_ref[...].astype(BF16), wproj_ref[...], preferred_element_type=F32)
    out_ref[...] = h2 + ple * gate


def _combine(h, rec, dst, ys, p, g_ple, wgate_bf, wproj_bf, *, tm, layer):
    n = h.shape[0]
    n_tok = n // tm
    tok = lambda w: pl.BlockSpec((tm, w), lambda i, s: (i, 0))
    full = lambda shape: pl.BlockSpec(shape, lambda i, s: (0,) * len(shape))
    grid_spec = pltpu.PrefetchScalarGridSpec(
        num_scalar_prefetch=1,
        grid=(n_tok,),
        in_specs=[tok(D_MODEL), tok(LANES), pl.BlockSpec(memory_space=pl.ANY),
                  pl.BlockSpec((None, tm, D_PLE), lambda i, s: (layer, i, 0)),
                  full((1, D_MODEL)), full((D_MODEL, D_MODEL)), full((D_PLE, D_MODEL))],
        out_specs=tok(D_MODEL),
        scratch_shapes=[pltpu.VMEM((2, _slots(tm), PACKED), jnp.uint32), pltpu.SemaphoreType.DMA((2,))])
    return pl.pallas_call(
        _combine_kernel,
        grid_spec=grid_spec,
        out_shape=jax.ShapeDtypeStruct((n, D_MODEL), F32),
        compiler_params=_cparams(1),
        name="combine",
    )(dst.reshape(n_tok, LIST_LANES), h, rec, ys, p, g_ple, wgate_bf, wproj_bf)


def _excl_cumsum(x, axis):
    return jnp.cumsum(x, axis=axis) - x


def _chunk_plan(cnt, tile_row0, *, n_pairs, rows_a, zero_row, te):
    n_tok = cnt.shape[0]
    per_tile = te // SUBLANES
    chunks = (cnt + SUBLANES - 1) // SUBLANES
    run0 = _excl_cumsum(chunks, 1)
    seg0 = _excl_cumsum(chunks, 0)
    total = jnp.sum(chunks, axis=0)
    region = ((total + per_tile - 1) // per_tile) * per_tile
    reg_end = jnp.cumsum(region)
    reg0 = reg_end - region

    n_tiles = -(-(n_pairs + (SUBLANES - 1) * N_EXPERTS * n_tok) // te) + N_EXPERTS
    t0 = jnp.arange(n_tiles, dtype=jnp.int32) * per_tile
    tile_expert = jnp.minimum(jnp.sum((t0[:, None] >= reg_end[None, :]).astype(jnp.int32), axis=1), N_EXPERTS - 1)
    pick = tile_expert[:, None] == jnp.arange(N_EXPERTS, dtype=jnp.int32)[None, :]
    of_tile = lambda v: jnp.sum(jnp.where(pick, v[None, :], 0), axis=1)
    tile_valid = jnp.clip(of_tile(total) - (t0 - of_tile(reg0)), 0, per_tile).astype(jnp.int32)

    q = (t0 - of_tile(reg0))[:, None] + jnp.arange(per_tile, dtype=jnp.int32)[None, :]
    col_of_tile = lambda v: jnp.sum(jnp.where(pick[:, None, :], v[None, :, :], 0), axis=2)
    seg0_t, seg1_t, run0_t = col_of_tile(seg0), col_of_tile(seg0 + chunks), col_of_tile(run0)
    holds = (q[:, :, None] >= seg0_t[:, None, :]) & (q[:, :, None] < seg1_t[:, None, :])
    local = run0_t[:, None, :] + q[:, :, None] - seg0_t[:, None, :]
    row = jnp.asarray(tile_row0, jnp.int32)[None, None, :] + SUBLANES * local
    src = jnp.sum(jnp.where(holds, row, 0), axis=2)
    src = jnp.where(jnp.any(holds, axis=2), src, zero_row)
    n_first = jnp.sum((src < rows_a).astype(jnp.int32), axis=1)
    idx = jnp.arange(N_EXPERTS, dtype=jnp.int32)
    nonempty = total > 0
    ordinal = _excl_cumsum(nonempty.astype(jnp.int32), 0)
    later = jnp.where((idx[None, :] > idx[:, None]) & nonempty[None, :], idx[None, :], N_EXPERTS)
    following = jnp.min(later, axis=1)
    following = jnp.where(following == N_EXPERTS, -1, following)
    lane = jnp.arange(LIST_LANES, dtype=jnp.int32)[None, :]
    src = jnp.pad(src, ((0, 0), (0, LIST_LANES - per_tile)))
    for at, val in ((LIST_COUNT, n_first), (LIST_NEXT, of_tile(following)), (LIST_PARITY, of_tile(ordinal) % 2)):
        src = jnp.where(lane == at, val[:, None], src)
    src = src.reshape(n_tiles, 1, LIST_LANES).astype(jnp.int32)

    j = jnp.arange(LIST_LANES, dtype=jnp.int32)[None, :, None]
    inside = (j >= run0[:, None, :]) & (j < (run0 + chunks)[:, None, :])
    base = (reg0[None, :] + seg0 - run0)[:, None, :]
    dst = SUBLANES * jnp.sum(jnp.where(inside, base + j, 0), axis=2)
    dst = dst.reshape(n_tok, 1, LIST_LANES).astype(jnp.int32)
    return tile_expert.astype(jnp.int32), tile_valid, src, dst


def _moe_ple(h_a, h_b, p_a, p_b, g_ffn, w_router, before, sel, w_gate, w_up, w_down, g_ple, wgate_bf, wproj_bf,
              *, tm_a, tm_b, te, layer):
    xs_a, rec_a, cnt_a = _sort(h_a, g_ffn, w_router, before, sel, tm=tm_a)
    xs_b, rec_b, cnt_b = _sort(h_b, g_ffn, w_router, before, sel, tm=tm_b)
    t_a, t_b = cnt_a.shape[0], cnt_b.shape[0]
    cnt = jnp.concatenate([cnt_a[:, 0, :N_EXPERTS], cnt_b[:, 0, :N_EXPERTS]], axis=0).astype(jnp.int32)
    rows_a = xs_a.shape[0]
    row0 = np.concatenate([np.arange(t_a) * _slots(tm_a), rows_a + np.arange(t_b) * _slots(tm_b)])
    tile_expert, tile_valid, src, dst = _chunk_plan(
        cnt, row0, n_pairs=2 * (h_a.shape[0] + h_b.shape[0]), rows_a=rows_a,
        zero_row=rows_a + _slots(tm_b) - SUBLANES, te=te)
    ys = _experts(xs_a, xs_b, tile_expert, tile_valid, src, w_gate, w_up, w_down, te=te, layer=layer)
    out_a = _combine(h_a, rec_a, dst[:t_a], ys, p_a, g_ple, wgate_bf, wproj_bf, tm=tm_a, layer=layer)
    out_b = _combine(h_b, rec_b, dst[t_a:], ys, p_b, g_ple, wgate_bf, wproj_bf, tm=tm_b, layer=layer)
    return out_a, out_b


def _bucket_np(dist):
    max_exact = N_BUCKETS // 2
    d_f = np.maximum(dist, 1).astype(np.float32)
    large = max_exact + (np.log(d_f / np.float32(max_exact)) / np.float32(np.log(MAX_DISTANCE / max_exact))
                         * np.float32(N_BUCKETS - max_exact)).astype(np.int32)
    large = np.minimum(large, N_BUCKETS - 1)
    return np.where(dist < max_exact, dist, large).astype(np.int32)


def _bias_from_buckets(rel_bias, bucket, valid):
    onehot = (jnp.asarray(bucket)[..., None] == jnp.arange(N_BUCKETS, dtype=jnp.int32)).astype(F32)
    bias = jnp.einsum("...k,kh->h...", onehot, rel_bias.astype(F32), precision=HIGHEST)
    return jnp.where(jnp.asarray(valid)[None], bias, NEG)


def _block_order(dil):
    g = np.arange(Q_BLOCK) // SUBLANES
    j = np.arange(Q_BLOCK) % SUBLANES
    if dil == 1:
        return 16 * j + g
    if dil == 4:
        return 32 * (g // 4) + 4 * j + g % 4
    return SUBLANES * g + j


def _band_bias(rel_bias, dil):
    mu = _block_order(dil)
    qi = mu[:, None] + Q_BLOCK
    ki = np.concatenate([mu, mu + Q_BLOCK])[None, :]
    off = qi - ki
    valid = (off >= 0) & (off <= N_KEYS)
    bucket = _bucket_np(dil * np.clip(off, 0, N_KEYS))
    first = valid & (np.arange(2 * Q_BLOCK)[None, :] >= Q_BLOCK)
    tables = [_bias_from_buckets(rel_bias, bucket, v).reshape(N_HEADS * Q_BLOCK, 2 * Q_BLOCK) for v in (valid, first)]
    return jnp.stack(tables)


def _sample_tables(rel_bias, w_buf):
    qpos = w_buf + np.arange(SAMPLE_T)[:, None]
    pos = np.arange(w_buf + NEW_COLS)[None, :]
    dist = qpos - pos
    in_seq = (dist >= 0) & (pos < w_buf + SAMPLE_T)
    mult = np.zeros(dist.shape, np.float32)
    for (w, d) in PATTERNS:
        mult += in_seq & (dist % d == 0) & (dist <= w)
    bucket = _bucket_np(np.maximum(dist, 0))
    bias = jnp.transpose(_bias_from_buckets(rel_bias, bucket, mult > 0), (1, 0, 2))
    rows = SAMPLE_T * N_HEADS
    mult_rows = np.broadcast_to(mult[:, None, :], (SAMPLE_T, N_HEADS, mult.shape[-1]))
    return bias.reshape(rows, -1), jnp.asarray(mult_rows.reshape(rows, -1))


PROJ_TM = 512
PERM_ROWS = 256
OUT_TM = 512
PROMPT_TM = 256
PROMPT_TE = 512
ATTN_SUB = 4


def _row_perm(tm):
    a = np.arange(tm)
    src = (a // Q_BLOCK) * Q_BLOCK + 16 * (a % SUBLANES) + (a % Q_BLOCK) // SUBLANES
    perm = np.zeros((tm, tm), np.float32)
    perm[a, src] = 1.0
    return perm


def kernel(x_prompt, x_sample, cache_k, cache_v, state_conv, p_prompt, p_sample, rel_bias, g_mix, w_in, q_gain,
           k_gain, conv_w, g_out_att, g_out_conv, w_out, g_ffn, w_router_group, w_router_expert, w_gate, w_up,
           w_down, g_ple, w_ple_gate, w_ple_proj):
    depth = w_in.shape[0]
    batch, seq, _ = x_prompt.shape
    dec_b, dec_t, _ = x_sample.shape
    w_buf = cache_k.shape[2]
    n_s = dec_b * dec_t
    keep = min(w_buf, seq)
    assert dec_t == SAMPLE_T and w_buf % LANES == 0
    assert seq % (Q_BLOCK * 16 * 2) == 0 and keep % PROJ_TM == 0 and seq % PROMPT_TM == 0
    cache_kt = jnp.transpose(cache_k, (0, 1, 3, 4, 2))
    cache_vt = jnp.transpose(cache_v, (0, 1, 3, 4, 2))

    row = lambda a: a.reshape(1, -1)
    src_lane = np.arange(LANES)
    expand = jnp.asarray((src_lane[:, None] // LSE_LANES_PER_HEAD == np.arange(ATT_DIM)[None, :] // HEAD_DIM)
                         & (src_lane[:, None] % LSE_LANES_PER_HEAD == 0), BF16)
    before = jnp.asarray(np.arange(LANES)[:, None] < np.arange(LANES)[None, :], BF16)
    sel_np = np.zeros((SUBLANES, LANES), np.float32)
    sel_np[0, REC_S1] = sel_np[1, REC_S2] = 1.0
    sel = jnp.asarray(sel_np, BF16)
    perm = jnp.asarray(_row_perm(PERM_ROWS), BF16)
    unperm = jnp.asarray(_row_perm(OUT_TM).T, BF16)
    band = [_band_bias(rel_bias, d) for (_, d) in PATTERNS]
    s_bias, s_mult = _sample_tables(rel_bias, w_buf)

    hp = x_prompt.reshape(batch * seq, D_MODEL)
    hs = jnp.swapaxes(x_sample, 0, 1).reshape(n_s, D_MODEL)
    pp_all = p_prompt.reshape(depth, batch * seq, D_PLE)
    ps_all = jnp.swapaxes(p_sample, 1, 2).reshape(depth, n_s, D_PLE)
    new = {k: [] for k in ("kp", "vp", "cp", "ks", "vs", "cs")}
    hist_p = jnp.zeros((batch, SUBLANES, CONV_DIM), F32)

    for l in range(depth):
        w_in_bf = w_in[l].astype(BF16)
        wa_bf = w_out[l, :ATT_DIM].astype(BF16)
        wc_bf = w_out[l, ATT_DIM:].astype(BF16)
        wgate_bf = w_ple_gate[l].astype(BF16)
        wproj_bf = w_ple_proj[l].astype(BF16)
        w_router = jnp.concatenate(
            [w_router_expert[l], w_router_group[l],
             jnp.zeros((D_MODEL, LANES - N_EXPERTS - N_GROUPS), F32)], axis=1).astype(BF16)
        qg, kg = row(jnp.tile(q_gain[l], N_HEADS)), row(jnp.tile(k_gain[l], N_HEADS))
        mix = (row(g_mix[l]), w_in_bf, qg, kg, conv_w[l], row(g_out_conv[l]))
        moe = (row(g_ffn[l]), w_router, before, sel, w_gate, w_up, w_down, row(g_ple[l]), wgate_bf, wproj_bf)

        q, k, v, k_nat, v_nat, yn, nconv = _inproj(
            hp, hist_p, *mix, perm, tm=PROJ_TM, shift=1, tiles_per_seq=seq // PROJ_TM,
            keep_tiles=keep // PROJ_TM)
        os, lses = [], []
        for bias, (_, d) in zip(band, PATTERNS):
            sub = min(ATTN_SUB, seq // (Q_BLOCK * d))
            o, lse = _attn_pattern(q, k, v, bias, batch=batch, seq=seq, dil=d, sub=sub,
                                   res=min(d, ATTN_SUB // sub))
            os.append(o)
            lses.append(lse)
        hp = _outproj(os, lses, yn, hp, row(g_out_att[l]), wa_bf, wc_bf, expand, unperm, tm=OUT_TM)
        new["kp"].append(jnp.transpose(k_nat, (0, 3, 1, 2)))
        new["vp"].append(jnp.transpose(v_nat, (0, 3, 1, 2)))
        new["cp"].append(nconv[:, SUBLANES - 2:])

        hist_s = jnp.swapaxes(state_conv[l], 0, 1).reshape(1, 2 * dec_b, CONV_DIM)
        q, k, v, yn, nconv = _inproj(hs, hist_s, *mix, tm=n_s, shift=dec_b, tiles_per_seq=1)
        bmaj = lambda a: jnp.swapaxes(a.reshape(dec_t, dec_b, N_HEADS, HEAD_DIM), 0, 1)
        qb, kb, vb = bmaj(q), bmaj(k), bmaj(v)
        rows8 = lambda a: jnp.pad(a.reshape(dec_b, dec_t, ATT_DIM), ((0, 0), (0, SUBLANES - dec_t), (0, 0)))
        att = _attn_sample(qb.reshape(dec_b, dec_t, ATT_DIM), rows8(kb), rows8(vb), cache_kt, cache_vt, l,
                           s_bias, s_mult)
        att_tm = jnp.swapaxes(att, 0, 1).reshape(n_s, ATT_DIM)
        hs = _outproj([att_tm], [], yn, hs, row(g_out_att[l]), wa_bf, wc_bf, tm=n_s)
        hp, hs = _moe_ple(hp, hs, pp_all, ps_all, *moe, tm_a=PROMPT_TM, tm_b=n_s, te=PROMPT_TE, layer=l)
        new["ks"].append(kb)
        new["vs"].append(vb)
        new["cs"].append(jnp.swapaxes(nconv.reshape(2, dec_b, CONV_DIM), 0, 1))

    y_prompt = hp.reshape(batch, seq, D_MODEL)
    y_sample = jnp.swapaxes(hs.reshape(dec_t, dec_b, D_MODEL), 0, 1)
    st = lambda key: jnp.stack(new[key])
    return (y_prompt, y_sample, st("kp"), st("vp"), st("cp"), st("ks"), st("vs"), st("cs"))
```

```python
import functools

import jax
import jax.numpy as jnp
import numpy as np
from jax import lax
from jax.experimental import pallas as pl
from jax.experimental.pallas import tpu as pltpu

F32 = jnp.float32
BF16 = jnp.bfloat16
HIGHEST = lax.Precision.HIGHEST

D_MODEL = 1024
HEAD_DIM = 64
N_HEADS = 8
ATT_DIM = N_HEADS * HEAD_DIM
CONV_DIM = D_MODEL - ATT_DIM
MIX_IN = 3 * ATT_DIM + 3 * CONV_DIM
PATTERNS = ((128, 1), (512, 4), (2048, 16))
N_KEYS = 128
Q_BLOCK = 128
N_BUCKETS = 32
MAX_DISTANCE = 2048
N_GROUPS = 4
EXPERTS_PER_GROUP = 8
N_EXPERTS = N_GROUPS * EXPERTS_PER_GROUP
D_EXPERT = 256
D_PLE = 256
EPS = 1e-6
NEG = -1e30

LANES = 128
SUBLANES = 8
SLABS = Q_BLOCK // SUBLANES
LSE_LANES_PER_HEAD = LANES // N_HEADS
VMEM_LIMIT = 56 * 1024 * 1024
NT = (((1,), (1,)), ((), ()))


def _cparams(n_axes):
    return pltpu.CompilerParams(dimension_semantics=("arbitrary",) * n_axes,
                                vmem_limit_bytes=VMEM_LIMIT)


def _full(shape):
    n = len(shape)
    return pl.BlockSpec(shape, lambda *_: (0,) * n)


def _rms(x, gain):
    ms = jnp.mean(x * x, axis=-1, keepdims=True)
    return x * lax.rsqrt(ms + EPS) * gain


def _inproj_kernel(*refs, shift, tiles_per_seq, permute):
    (h_ref, gmix_ref, w_ref, qg_ref, kg_ref, cw_ref, gconv_ref, hist_ref) = refs[:8]
    if permute:
        perm_ref, q_ref, k_ref, v_ref, kn_ref, vn_ref, yn_ref, nconv_ref, carry_ref = refs[8:]
    else:
        q_ref, k_ref, v_ref, yn_ref, nconv_ref, carry_ref = refs[8:]
    i = pl.program_id(0)
    a = _rms(h_ref[...], gmix_ref[...])
    proj = jnp.dot(a.astype(BF16), w_ref[...], preferred_element_type=F32)
    tm = proj.shape[0]

    lower = lax.broadcasted_iota(jnp.int32, (tm, LANES), 1) < HEAD_DIM

    def head_norm(t, g):
        out = []
        for j in range(ATT_DIM // LANES):
            blk = t[:, j * LANES:(j + 1) * LANES]
            sq = blk * blk
            ms_lo = jnp.sum(jnp.where(lower, sq, 0.0), axis=-1, keepdims=True) * (1.0 / HEAD_DIM)
            ms_hi = jnp.sum(jnp.where(lower, 0.0, sq), axis=-1, keepdims=True) * (1.0 / HEAD_DIM)
            scale = jnp.where(lower, lax.rsqrt(ms_lo + EPS), lax.rsqrt(ms_hi + EPS))
            out.append(blk * scale)
        return jnp.concatenate(out, axis=-1) * g

    q = head_norm(proj[:, 0:ATT_DIM], qg_ref[...])
    k = head_norm(proj[:, ATT_DIM:2 * ATT_DIM], kg_ref[...])
    v = proj[:, 2 * ATT_DIM:3 * ATT_DIM]
    if permute:
        kn_ref[...] = k.T.reshape(N_HEADS, HEAD_DIM, tm)
        vn_ref[...] = v.T.reshape(N_HEADS, HEAD_DIM, tm)
        qkv = jnp.concatenate([q * (HEAD_DIM ** -0.5), k, v], axis=-1).astype(BF16)
        pm = perm_ref.shape[0]
        moved = jnp.concatenate([jnp.dot(perm_ref[...], qkv[r:r + pm], preferred_element_type=F32)
                                 for r in range(0, tm, pm)], axis=0)
        q_ref[...] = _pack_bf16(moved[:, 0:ATT_DIM], is_bf16=True)
        k_ref[...] = _pack_bf16(moved[:, ATT_DIM:2 * ATT_DIM], is_bf16=True)
        v_ref[...] = _pack_bf16(moved[:, 2 * ATT_DIM:3 * ATT_DIM], is_bf16=True)
    else:
        q_ref[...] = q
        k_ref[...] = k
        v_ref[...] = v
    c0 = 3 * ATT_DIM
    hc = proj[:, c0:c0 + CONV_DIM]
    gb = proj[:, c0 + CONV_DIM:c0 + 2 * CONV_DIM]
    gc = proj[:, c0 + 2 * CONV_DIM:c0 + 3 * CONV_DIM]
    u = gc * hc

    if shift == 1:
        @pl.when(i % tiles_per_seq == 0)
        def _():
            carry_ref[...] = hist_ref[0]
        h0 = carry_ref[SUBLANES - 2:SUBLANES - 1, :]
        h1 = carry_ref[SUBLANES - 1:SUBLANES, :]
        row = lax.broadcasted_iota(jnp.int32, (tm, 1), 0)
        u1 = jnp.where(row == 0, h1, pltpu.roll(u, 1, 0))
        u2 = jnp.where(row == 0, h0, jnp.where(row == 1, h1, pltpu.roll(u, 2, 0)))
        carry_ref[...] = u[tm - SUBLANES:tm, :]
        nconv_ref[0] = u[tm - SUBLANES:tm, :]
    else:
        hist = hist_ref[0]
        u1 = jnp.concatenate([hist[shift:2 * shift], u[0:tm - shift]], axis=0)
        u2 = jnp.concatenate([hist, u[0:tm - 2 * shift]], axis=0)
        nconv_ref[0] = u[tm - 2 * shift:tm, :]
    conv = cw_ref[0:1, :] * u2 + cw_ref[1:2, :] * u1 + cw_ref[2:3, :] * u
    yn_ref[...] = _rms(gb * conv, gconv_ref[...])


def _inproj(h, hist, g_mix, w_in_bf, q_gain, k_gain, conv_w, g_out_conv, perm=None, *,
            tm, shift, tiles_per_seq, keep_tiles=0):
    n = h.shape[0]
    hist_rows = hist.shape[1]
    nseq = hist.shape[0]
    tok = lambda w: pl.BlockSpec((tm, w), lambda i: (i, 0))
    seq3 = lambda r: pl.BlockSpec((1, r, CONV_DIM), lambda i: (i // tiles_per_seq, 0, 0))
    nconv_rows = SUBLANES if shift == 1 else 2 * shift
    packed = perm is not None
    att_w = ATT_DIM // 2 if packed else ATT_DIM
    att = jax.ShapeDtypeStruct((n, att_w), jnp.uint32 if packed else F32)
    in_specs = [tok(D_MODEL), _full((1, D_MODEL)), _full((D_MODEL, MIX_IN)), _full((1, ATT_DIM)),
                _full((1, ATT_DIM)), _full((3, CONV_DIM)), _full((1, CONV_DIM)), seq3(hist_rows)]
    args = [h, g_mix, w_in_bf, q_gain, k_gain, conv_w, g_out_conv, hist]
    out_specs = [tok(att_w)] * 3
    out_shape = [att] * 3
    if perm is not None:
        in_specs.append(_full(perm.shape))
        args.append(perm)
        first = tiles_per_seq - keep_tiles
        kept = pl.BlockSpec((None, N_HEADS, HEAD_DIM, tm),
                            lambda i: (i // tiles_per_seq, 0, 0, jnp.maximum(i % tiles_per_seq - first, 0)))
        out_specs += [kept, kept]
        out_shape += [jax.ShapeDtypeStruct((nseq, N_HEADS, HEAD_DIM, keep_tiles * tm), F32)] * 2
    out_specs += [tok(CONV_DIM), seq3(nconv_rows)]
    out_shape += [jax.ShapeDtypeStruct((n, CONV_DIM), F32),
                  jax.ShapeDtypeStruct((nseq, nconv_rows, CONV_DIM), F32)]
    return pl.pallas_call(
        functools.partial(_inproj_kernel, shift=shift, tiles_per_seq=tiles_per_seq, permute=perm is not None),
        grid=(n // tm,),
        in_specs=in_specs, out_specs=out_specs, out_shape=out_shape,
        scratch_shapes=[pltpu.VMEM((SUBLANES, CONV_DIM), F32)],
        compiler_params=_cparams(1),
        name="inproj",
    )(*args)


def _attn_block(q, kk, vv, bias):
    lane = lax.broadcasted_iota(jnp.int32, (Q_BLOCK, LANES), 1)
    upper = lane >= HEAD_DIM
    scores = []
    for h in range(N_HEADS):
        j, e = divmod(h, 2)
        qp = q[:, j * LANES:(j + 1) * LANES]
        qm = (jnp.where(upper, qp, 0.0) if e else jnp.where(upper, 0.0, qp)).astype(BF16)
        scores.append(lax.dot_general(qm, kk[:, j * LANES:(j + 1) * LANES], NT, preferred_element_type=F32))
    s = jnp.concatenate(scores, axis=0) + bias
    m = jnp.max(s, axis=-1, keepdims=True)
    p = jnp.exp(s - m)
    den = jnp.sum(p, axis=-1, keepdims=True)
    pb = p.astype(BF16)
    head_rows = lambda t, h: t[h * Q_BLOCK:(h + 1) * Q_BLOCK]
    lse_grp = lane // LSE_LANES_PER_HEAD
    m_tile = jnp.zeros((Q_BLOCK, LANES), F32)
    den_tile = jnp.ones((Q_BLOCK, LANES), F32)
    outs = []
    for j in range(N_HEADS // 2):
        even, odd = 2 * j, 2 * j + 1
        vp = vv[:, j * LANES:(j + 1) * LANES]
        pair = jnp.where(upper, jnp.dot(head_rows(pb, odd), vp, preferred_element_type=F32),
                         jnp.dot(head_rows(pb, even), vp, preferred_element_type=F32))
        outs.append(pair / jnp.where(upper, head_rows(den, odd), head_rows(den, even)))
        for h in (even, odd):
            m_tile = jnp.where(lse_grp == h, head_rows(m, h), m_tile)
            den_tile = jnp.where(lse_grp == h, head_rows(den, h), den_tile)
    return jnp.concatenate(outs, axis=-1), m_tile + jnp.log(den_tile)


def _attn_kernel(q_ref, kp_ref, kc_ref, vp_ref, vc_ref, bias_ref, o_ref, lse_ref, kbuf, vbuf, *, sub, res):
    n = pl.program_id(2)
    rows = sub * Q_BLOCK
    half = ATT_DIM // 2
    for r in range(res):
        kbuf[0:Q_BLOCK, :] = _unpack_bf16(kp_ref[:, r].reshape(Q_BLOCK, half))
        kbuf[Q_BLOCK:, :] = _unpack_bf16(kc_ref[:, :, r].reshape(rows, half))
        vbuf[0:Q_BLOCK, :] = _unpack_bf16(vp_ref[:, r].reshape(Q_BLOCK, half))
        vbuf[Q_BLOCK:, :] = _unpack_bf16(vc_ref[:, :, r].reshape(rows, half))
        for j in range(sub):
            q = _unpack_bf16(q_ref[j, :, r].reshape(Q_BLOCK, half), F32)
            r0 = j * Q_BLOCK
            first = (n == 0).astype(jnp.int32) if j == 0 else 0
            o, lse = _attn_block(q, kbuf[r0:r0 + 2 * Q_BLOCK, :], vbuf[r0:r0 + 2 * Q_BLOCK, :], bias_ref[first])
            o_ref[j, :, r] = o.reshape(SLABS, SUBLANES, ATT_DIM)
            lse_ref[j, :, r] = lse.reshape(SLABS, SUBLANES, LANES)


def _attn_pattern(q, k, v, bias, *, batch, seq, dil, sub, res):
    nblk = seq // (Q_BLOCK * dil)
    view = lambda t: t.reshape(batch, nblk, SLABS, dil, SUBLANES, t.shape[-1])
    cur = lambda c: pl.BlockSpec((None, sub, SLABS, res, SUBLANES, c), lambda b, r, n: (b, n, 0, r, 0, 0))
    half = ATT_DIM // 2
    prev = pl.BlockSpec((None, None, SLABS, res, SUBLANES, half),
                        lambda b, r, n: (b, jnp.maximum(n * sub - 1, 0), 0, r, 0, 0))
    o, lse = pl.pallas_call(
        functools.partial(_attn_kernel, sub=sub, res=res),
        grid=(batch, dil // res, nblk // sub),
        in_specs=[cur(half), prev, cur(half), prev, cur(half),
                  _full((2, N_HEADS * Q_BLOCK, 2 * Q_BLOCK))],
        out_specs=[cur(ATT_DIM), cur(LANES)],
        out_shape=[jax.ShapeDtypeStruct((batch, nblk, SLABS, dil, SUBLANES, ATT_DIM), F32),
                   jax.ShapeDtypeStruct((batch, nblk, SLABS, dil, SUBLANES, LANES), F32)],
        scratch_shapes=[pltpu.VMEM(((sub + 1) * Q_BLOCK, ATT_DIM), BF16),
                        pltpu.VMEM(((sub + 1) * Q_BLOCK, ATT_DIM), BF16)],
        compiler_params=_cparams(3),
        name=f"attn_d{dil}",
    )(view(q), view(k), view(k), view(v), view(v), bias)
    return o.reshape(batch * seq, ATT_DIM), lse.reshape(batch * seq, LANES)


SAMPLE_T = 4
NEW_COLS = LANES


def _attn_sample_kernel(q_ref, kt_ref, kn_ref, vt_ref, vn_ref, bias_ref, mult_ref, o_ref):
    rows = SAMPLE_T * N_HEADS
    q4 = q_ref[...] * (HEAD_DIM ** -0.5)
    qt = jnp.concatenate([jnp.broadcast_to(q4[t:t + 1, :], (N_HEADS, ATT_DIM)) for t in range(SAMPLE_T)], axis=0)
    lane_head = lax.broadcasted_iota(jnp.int32, (rows, ATT_DIM), 1) // HEAD_DIM
    row_head = lax.broadcasted_iota(jnp.int32, (rows, ATT_DIM), 0) % N_HEADS
    own = lane_head == row_head
    qbd = jnp.where(own, qt, 0.0).astype(BF16)
    flat = lambda ref: ref[...].reshape(ATT_DIM, ref.shape[-1]).astype(BF16)
    pad = jnp.zeros((NEW_COLS - SUBLANES, ATT_DIM), F32)
    new_rows = lambda ref: jnp.concatenate([ref[...], pad], axis=0).astype(BF16)
    s = jnp.concatenate([jnp.dot(qbd, flat(kt_ref), preferred_element_type=F32),
                         lax.dot_general(qbd, new_rows(kn_ref), NT, preferred_element_type=F32)],
                        axis=-1) + bias_ref[...]
    m = jnp.max(s, axis=-1, keepdims=True)
    p = jnp.exp(s - m) * mult_ref[...]
    den = jnp.sum(p, axis=-1, keepdims=True)
    pb = p.astype(BF16)
    w_buf = kt_ref.shape[-1]
    acc = (lax.dot_general(pb[:, :w_buf], flat(vt_ref), NT, preferred_element_type=F32)
           + jnp.dot(pb[:, w_buf:], new_rows(vn_ref), preferred_element_type=F32))
    acc = jnp.where(own, acc / den, 0.0)
    for t in range(SAMPLE_T):
        o_ref[t:t + 1, :] = jnp.sum(acc[t * N_HEADS:(t + 1) * N_HEADS, :], axis=0, keepdims=True)


def _attn_sample(q, k_new, v_new, cache_kt, cache_vt, layer, bias, mult):
    nb = q.shape[0]
    w_buf = cache_kt.shape[-1]
    tok = pl.BlockSpec((None, SAMPLE_T, ATT_DIM), lambda b: (b, 0, 0))
    new = pl.BlockSpec((None, SUBLANES, ATT_DIM), lambda b: (b, 0, 0))
    old = pl.BlockSpec((None, None, N_HEADS, HEAD_DIM, w_buf), lambda b: (layer, b, 0, 0, 0))
    tbl = _full((SAMPLE_T * N_HEADS, w_buf + NEW_COLS))
    return pl.pallas_call(
        _attn_sample_kernel,
        grid=(nb,),
        in_specs=[tok, old, new, old, new, tbl, tbl],
        out_specs=tok,
        out_shape=jax.ShapeDtypeStruct((nb, SAMPLE_T, ATT_DIM), F32),
        compiler_params=_cparams(1),
        name="attn_sample",
    )(q, cache_kt, k_new, cache_vt, v_new, bias, mult)


def _split_dot(x, e_ref):
    hi = x.astype(BF16)
    lo = (x - hi.astype(F32)).astype(BF16)
    return (jnp.dot(hi, e_ref[...], preferred_element_type=F32)
            + jnp.dot(lo, e_ref[...], preferred_element_type=F32))


def _outproj_kernel(*refs, n_pat):
    mix = n_pat > 1
    n_lse = n_pat if mix else 0
    o_refs = refs[0:n_pat]
    l_refs = refs[n_pat:n_pat + n_lse]
    rest = refs[n_pat + n_lse:]
    if mix:
        yn_ref, h_ref, gatt_ref, exp_ref, unperm_ref, wa_ref, wc_ref, out_ref = rest
        lses = [r[...] for r in l_refs]
        top = functools.reduce(jnp.maximum, lses)
        ws = [jnp.exp(l - top) for l in lses]
        tot = functools.reduce(lambda a, b: a + b, ws)
        att = None
        for w, o_ref in zip(ws, o_refs):
            term = _split_dot(w / tot, exp_ref) * o_ref[...]
            att = term if att is None else att + term
        att_bf = jnp.dot(unperm_ref[...], _rms(att, gatt_ref[...]).astype(BF16),
                         preferred_element_type=F32).astype(BF16)
    else:
        yn_ref, h_ref, gatt_ref, wa_ref, wc_ref, out_ref = rest
        att_bf = _rms(o_refs[0][...], gatt_ref[...]).astype(BF16)
    y = (jnp.dot(att_bf, wa_ref[...], preferred_element_type=F32)
         + jnp.dot(yn_ref[...].astype(BF16), wc_ref[...], preferred_element_type=F32))
    out_ref[...] = h_ref[...] + y


def _outproj(os, lses, yn, h, g_att, wa_bf, wc_bf, expand=None, unperm=None, *, tm):
    n = h.shape[0]
    n_pat = len(os)
    tok = lambda w: pl.BlockSpec((tm, w), lambda i: (i, 0))
    in_specs = [tok(ATT_DIM)] * n_pat + [tok(LANES)] * len(lses) + [tok(CONV_DIM), tok(D_MODEL), _full((1, ATT_DIM))]
    args = [*os, *lses, yn, h, g_att]
    if n_pat > 1:
        in_specs += [_full((LANES, ATT_DIM)), _full((tm, tm))]
        args += [expand, unperm]
    in_specs += [_full((ATT_DIM, D_MODEL)), _full((CONV_DIM, D_MODEL))]
    args += [wa_bf, wc_bf]
    return pl.pallas_call(
        functools.partial(_outproj_kernel, n_pat=n_pat),
        grid=(n // tm,),
        in_specs=in_specs,
        out_specs=tok(D_MODEL),
        out_shape=jax.ShapeDtypeStruct((n, D_MODEL), F32),
        compiler_params=_cparams(1),
        name="outproj",
    )(*args)


GROUP_LANE0 = N_EXPERTS
REC_E1, REC_E2, REC_S1, REC_S2, REC_W1, REC_W2 = range(6)
PACKED = D_MODEL // 2
ROW_W = PACKED + LANES
N_DMA_PRIORITIES = 2
LIST_LANES = LANES
LIST_COUNT = LIST_LANES - 1
LIST_NEXT = LIST_LANES - 2
LIST_PARITY = LIST_LANES - 3


def _slots(tm):
    need = 2 * tm + (SUBLANES - 1) * N_EXPERTS + SUBLANES
    return -(-need // LANES) * LANES


def _pieces(x):
    hi = x.astype(BF16)
    r1 = x - hi.astype(F32)
    mid = r1.astype(BF16)
    return hi, mid, (r1 - mid.astype(F32)).astype(BF16)


def _sort_kernel(h_ref, g_ref, wr_ref, before_ref, sel_ref, xs_ref, rec_ref, cnt_ref, *, group):
    tm = h_ref.shape[0] // group
    slots = xs_ref.shape[0] // group
    for t in range(group):
        xs, rec, cnt = _sort_tile(h_ref[t * tm:(t + 1) * tm, :], g_ref, wr_ref, before_ref, sel_ref, slots)
        xs_ref[t * slots:(t + 1) * slots, :] = xs
        rec_ref[t * tm:(t + 1) * tm, :] = rec
        cnt_ref[t] = cnt


def _sort_tile(h, g_ref, wr_ref, before_ref, sel_ref, slots):
    m = _rms(h, g_ref[...])
    logits = jnp.dot(m.astype(BF16), wr_ref[...], preferred_element_type=F32)
    tm = logits.shape[0]
    lane_i = lax.broadcasted_iota(jnp.int32, (tm, LANES), 1)
    lane = lane_i.astype(F32)
    big = jnp.float32(4 * LANES)

    is_g = jnp.logical_and(lane_i >= GROUP_LANE0, lane_i < GROUP_LANE0 + N_GROUPS)
    gl = jnp.where(is_g, logits, NEG)
    gmax = jnp.max(gl, axis=-1, keepdims=True)
    g_w = 1.0 / jnp.sum(jnp.where(is_g, jnp.exp(gl - gmax), 0.0), axis=-1, keepdims=True)
    g_sel = jnp.min(jnp.where(gl == gmax, lane - GROUP_LANE0, big), axis=-1, keepdims=True)

    grp_of_lane = (lane_i // EXPERTS_PER_GROUP).astype(F32)
    in_grp = jnp.logical_and(lane_i < N_EXPERTS, grp_of_lane == g_sel)
    el = jnp.where(in_grp, logits, NEG)
    t1 = jnp.max(el, axis=-1, keepdims=True)
    e1 = jnp.min(jnp.where(el == t1, lane, big), axis=-1, keepdims=True)
    el2 = jnp.where(lane == e1, NEG, el)
    t2 = jnp.max(el2, axis=-1, keepdims=True)
    e2 = jnp.min(jnp.where(el2 == t2, lane, big), axis=-1, keepdims=True)
    ex = jnp.exp(t2 - t1)
    w1 = g_w / (1.0 + ex)
    w2 = g_w * ex / (1.0 + ex)

    hit1 = lane == e1
    hit2 = lane == e2
    c = jnp.where(hit1, 1.0, jnp.where(hit2, 1.0, 0.0))
    rr = lax.broadcasted_iota(jnp.int32, (tm, tm), 0)
    cc = lax.broadcasted_iota(jnp.int32, (tm, tm), 1)
    lower = jnp.where(rr > cc, 1.0, 0.0).astype(BF16)
    rank = jnp.dot(lower, c.astype(BF16), preferred_element_type=F32)
    cnt = jnp.sum(c, axis=0, keepdims=True)
    chunks = jnp.floor((cnt + (SUBLANES - 1)) * (1.0 / SUBLANES))
    start = SUBLANES * jnp.dot(jnp.broadcast_to(chunks, (SUBLANES, LANES)).astype(BF16), before_ref[...],
                               preferred_element_type=F32)[0:1, :]
    slot_of = rank + start
    s1 = jnp.sum(jnp.where(hit1, slot_of, 0.0), axis=-1, keepdims=True)
    s2 = jnp.sum(jnp.where(hit2, slot_of, 0.0), axis=-1, keepdims=True)

    rec = jnp.zeros((tm, LANES), F32)
    for idx, val in ((REC_E1, e1), (REC_E2, e2), (REC_S1, s1), (REC_S2, s2), (REC_W1, w1), (REC_W2, w2)):
        rec = jnp.where(lane_i == idx, val, rec)

    rec_parts = _pieces(rec)[:2]
    srow = sum(lax.dot_general(sel_ref[...], part, NT, preferred_element_type=F32) for part in rec_parts)
    slot_id = lax.broadcasted_iota(jnp.int32, (slots, tm), 0).astype(F32)
    place = jnp.where(slot_id == srow[0:1, :], 1.0, jnp.where(slot_id == srow[1:2, :], 1.0, 0.0)).astype(BF16)
    payload = jnp.concatenate([m.astype(BF16), *rec_parts], axis=-1)
    moved = jnp.dot(place, payload, preferred_element_type=F32)
    info = moved[:, D_MODEL:D_MODEL + LANES] + moved[:, D_MODEL + LANES:]
    xs = jnp.concatenate([_pack_bf16(moved[:, :D_MODEL], is_bf16=True), pltpu.bitcast(info, jnp.uint32)], axis=-1)
    return xs, rec, jnp.broadcast_to(cnt, (SUBLANES, LANES))


def _sort(h, g_ffn, w_router, before, sel, *, tm):
    n = h.shape[0]
    slots = _slots(tm)
    group = 2 if (n // tm) % 2 == 0 else 1
    tok = lambda w: pl.BlockSpec((group * tm, w), lambda i: (i, 0))
    return pl.pallas_call(
        functools.partial(_sort_kernel, group=group),
        grid=(n // (group * tm),),
        in_specs=[tok(D_MODEL), _full((1, D_MODEL)), _full((D_MODEL, LANES)), _full((LANES, LANES)),
                  _full((SUBLANES, LANES))],
        out_specs=[pl.BlockSpec((group * slots, ROW_W), lambda i: (i, 0)), tok(LANES),
                   pl.BlockSpec((group, SUBLANES, LANES), lambda i: (i, 0, 0))],
        out_shape=[jax.ShapeDtypeStruct((n // tm * slots, ROW_W), jnp.uint32), jax.ShapeDtypeStruct((n, LANES), F32),
                   jax.ShapeDtypeStruct((n // tm, SUBLANES, LANES), F32)],
        compiler_params=_cparams(1),
        name="moe_sort",
    )(h, g_ffn, w_router, before, sel)


def _chunk_copy(src_hbm, row, dst, c, sem):
    if not isinstance(row, int):
        row = pl.multiple_of(row, SUBLANES)
    to = c * SUBLANES
    if not isinstance(to, int):
        to = pl.multiple_of(to, SUBLANES)
    return pltpu.make_async_copy(src_hbm.at[pl.ds(row, SUBLANES), :], dst.at[pl.ds(to, SUBLANES), :], sem)


def _expert_kernel(te_ref, tv_ref, lst_ref, xa_ref, xb_ref, wg_ref, wu_ref, wd_ref, y_ref,
                    xbuf, sem, wg_st, wu_st, wd_st, wsem, wg_bf, wu_bf, wd_bf, *, rows_a, layer):
    i = pl.program_id(0)
    n = pl.num_programs(0)
    te = xbuf.shape[1]
    slot = i % 2

    def fetch(tile, to_slot):
        n_a = lst_ref[tile, LIST_COUNT]

        def from_a(c, carry):
            priority = c % N_DMA_PRIORITIES if isinstance(c, int) else 0
            _chunk_copy(xa_ref, lst_ref[tile, c], xbuf.at[to_slot], c, sem.at[to_slot]).start(priority=priority)
            return carry

        def from_b(c, carry):
            _chunk_copy(xb_ref, lst_ref[tile, c] - rows_a, xbuf.at[to_slot], c, sem.at[to_slot]).start()
            return carry

        per_tile = te // SUBLANES

        @pl.when(n_a == per_tile)
        def _():
            for c in range(per_tile):
                from_a(c, 0)

        @pl.when(n_a != per_tile)
        def _():
            lax.fori_loop(0, n_a, from_a, 0)
            lax.fori_loop(n_a, per_tile, from_b, 0)

    @pl.when(jnp.logical_and(i == 0, tv_ref[0] > 0))
    def _():
        fetch(0, 0)

    nxt = jnp.minimum(i + 1, n - 1)

    @pl.when(jnp.logical_and(i + 1 < n, tv_ref[nxt] > 0))
    def _():
        fetch(nxt, 1 - slot)

    valid = tv_ref[i]
    changed = jnp.logical_or(i == 0, te_ref[i] != te_ref[jnp.maximum(i - 1, 0)])

    def weight_copies(expert, s):
        return [pltpu.make_async_copy(src.at[layer, expert], dst.at[s], wsem.at[s])
                for src, dst in ((wg_ref, wg_st), (wu_ref, wu_st), (wd_ref, wd_st))]

    @pl.when(jnp.logical_and(changed, valid > 0))
    def _():
        s = lst_ref[i, LIST_PARITY]

        @pl.when(i == 0)
        def _():
            for cp in weight_copies(te_ref[0], 0):
                cp.start()

        for cp in weight_copies(te_ref[i], s):
            cp.wait()
        wg_bf[...] = wg_st[s].astype(BF16)
        wu_bf[...] = wu_st[s].astype(BF16)
        wd_bf[...] = wd_st[s].astype(BF16)
        nxt_expert = lst_ref[i, LIST_NEXT]

        @pl.when(nxt_expert >= 0)
        def _():
            for cp in weight_copies(nxt_expert, 1 - s):
                cp.start()

    @pl.when(valid > 0)
    def _():
        pltpu.make_async_copy(xbuf.at[slot], xbuf.at[slot], sem.at[slot]).wait()
        rows = xbuf[slot]
        x = _unpack_bf16(rows[:, :PACKED])
        info = pltpu.bitcast(rows[:, PACKED:], F32)
        mine = info[:, REC_E1:REC_E1 + 1] == te_ref[i].astype(F32)
        gate = jnp.where(mine, info[:, REC_W1:REC_W1 + 1], info[:, REC_W2:REC_W2 + 1])
        hg = jnp.dot(x, wg_bf[...], preferred_element_type=F32)
        hu = jnp.dot(x, wu_bf[...], preferred_element_type=F32)
        hid = (hg * jax.nn.sigmoid(hg)) * hu * gate
        y = jnp.dot(hid.astype(BF16), wd_bf[...], preferred_element_type=F32)
        y_ref[...] = _pack_bf16(y)

    @pl.when(valid <= 0)
    def _():
        y_ref[...] = jnp.zeros_like(y_ref)


def _pack_bf16(x, is_bf16=False):
    w = x.shape[-1] // 2
    bits = lambda a: pltpu.bitcast(a if is_bf16 else a.astype(BF16).astype(F32), jnp.uint32)
    return jnp.bitwise_or(lax.shift_right_logical(bits(x[:, :w]), jnp.uint32(16)),
                          jnp.bitwise_and(bits(x[:, w:]), jnp.uint32(0xFFFF0000)))


def _unpack_bf16(words, dtype=BF16):
    as_f32 = lambda a: pltpu.bitcast(a, F32)
    return jnp.concatenate([as_f32(lax.shift_left(words, jnp.uint32(16))),
                            as_f32(jnp.bitwise_and(words, jnp.uint32(0xFFFF0000)))], axis=-1).astype(dtype)


def _experts(xs_a, xs_b, tile_expert, tile_valid, src, w_gate, w_up, w_down, *, te, layer):
    n_tiles = tile_expert.shape[0]
    hbm = pl.BlockSpec(memory_space=pl.ANY)
    grid_spec = pltpu.PrefetchScalarGridSpec(
        num_scalar_prefetch=3,
        grid=(n_tiles,),
        in_specs=[hbm, hbm, hbm, hbm, hbm],
        out_specs=pl.BlockSpec((te, PACKED), lambda i, e, v, s: (i, 0)),
        scratch_shapes=[pltpu.VMEM((2, te, ROW_W), jnp.uint32), pltpu.SemaphoreType.DMA((2,)),
                        pltpu.VMEM((2, D_MODEL, D_EXPERT), F32), pltpu.VMEM((2, D_MODEL, D_EXPERT), F32),
                        pltpu.VMEM((2, D_EXPERT, D_MODEL), F32), pltpu.SemaphoreType.DMA((2,)),
                        pltpu.VMEM((D_MODEL, D_EXPERT), BF16), pltpu.VMEM((D_MODEL, D_EXPERT), BF16),
                        pltpu.VMEM((D_EXPERT, D_MODEL), BF16)])
    return pl.pallas_call(
        functools.partial(_expert_kernel, rows_a=xs_a.shape[0], layer=layer),
        grid_spec=grid_spec,
        out_shape=jax.ShapeDtypeStruct((n_tiles * te, PACKED), jnp.uint32),
        compiler_params=_cparams(1),
        name="experts",
    )(tile_expert, tile_valid, src.reshape(n_tiles, LIST_LANES), xs_a, xs_b, w_gate, w_up, w_down)


def _combine_kernel(lst_ref, h_ref, rec_ref, ys_ref, p_ref, gple_ref, wgate_ref, wproj_ref, out_ref,
                     ybuf, sem):
    i = pl.program_id(0)
    n = pl.num_programs(0)
    slot = i % 2
    tm = h_ref.shape[0]
    slots = ybuf.shape[1]

    def fetch(tile, to_slot):
        for c in range(slots // SUBLANES):
            _chunk_copy(ys_ref, lst_ref[tile, c], ybuf.at[to_slot], c, sem.at[to_slot]).start(
                priority=c % N_DMA_PRIORITIES)

    @pl.when(i == 0)
    def _():
        fetch(0, 0)

    @pl.when(i + 1 < n)
    def _():
        fetch(i + 1, 1 - slot)

    pltpu.make_async_copy(ybuf.at[slot], ybuf.at[slot], sem.at[slot]).wait()

    rec = rec_ref[...]
    slot_id = lax.broadcasted_iota(jnp.int32, (tm, slots), 1).astype(F32)
    back = jnp.where(slot_id == rec[:, REC_S1:REC_S1 + 1], 1.0,
                     jnp.where(slot_id == rec[:, REC_S2:REC_S2 + 1], 1.0, 0.0)).astype(BF16)
    h2 = h_ref[...] + jnp.dot(back, _unpack_bf16(ybuf[slot]), preferred_element_type=F32)
    gate = jax.nn.sigmoid(jnp.dot(_rms(h2, gple_ref[...]).astype(BF16), wgate_ref[...], preferred_element_type=F32))
    ple = jnp.dot(p_ref[...].astype(BF16), wproj_ref[...], preferred_element_type=F32)
    out_ref[...] = h2 + ple * gate


def _combine(h, rec, dst, ys, p, g_ple, wgate_bf, wproj_bf, *, tm, layer):
    n = h.shape[0]
    n_tok = n // tm
    tok = lambda w: pl.BlockSpec((tm, w), lambda i, s: (i, 0))
    full = lambda shape: pl.BlockSpec(shape, lambda i, s: (0,) * len(shape))
    grid_spec = pltpu.PrefetchScalarGridSpec(
        num_scalar_prefetch=1,
        grid=(n_tok,),
        in_specs=[tok(D_MODEL), tok(LANES), pl.BlockSpec(memory_space=pl.ANY),
                  pl.BlockSpec((None, tm, D_PLE), lambda i, s: (layer, i, 0)),
                  full((1, D_MODEL)), full((D_MODEL, D_MODEL)), full((D_PLE, D_MODEL))],
        out_specs=tok(D_MODEL),
        scratch_shapes=[pltpu.VMEM((2, _slots(tm), PACKED), jnp.uint32), pltpu.SemaphoreType.DMA((2,))])
    return pl.pallas_call(
        _combine_kernel,
        grid_spec=grid_spec,
        out_shape=jax.ShapeDtypeStruct((n, D_MODEL), F32),
        compiler_params=_cparams(1),
        name="combine",
    )(dst.reshape(n_tok, LIST_LANES), h, rec, ys, p, g_ple, wgate_bf, wproj_bf)


def _excl_cumsum(x, axis):
    return jnp.cumsum(x, axis=axis) - x


def _chunk_plan(cnt, tile_row0, *, n_pairs, rows_a, zero_row, te):
    n_tok = cnt.shape[0]
    per_tile = te // SUBLANES
    chunks = (cnt + SUBLANES - 1) // SUBLANES
    run0 = _excl_cumsum(chunks, 1)
    seg0 = _excl_cumsum(chunks, 0)
    total = jnp.sum(chunks, axis=0)
    region = ((total + per_tile - 1) // per_tile) * per_tile
    reg_end = jnp.cumsum(region)
    reg0 = reg_end - region

    n_tiles = -(-(n_pairs + (SUBLANES - 1) * N_EXPERTS * n_tok) // te) + N_EXPERTS
    t0 = jnp.arange(n_tiles, dtype=jnp.int32) * per_tile
    tile_expert = jnp.minimum(jnp.sum((t0[:, None] >= reg_end[None, :]).astype(jnp.int32), axis=1), N_EXPERTS - 1)
    pick = tile_expert[:, None] == jnp.arange(N_EXPERTS, dtype=jnp.int32)[None, :]
    of_tile = lambda v: jnp.sum(jnp.where(pick, v[None, :], 0), axis=1)
    tile_valid = jnp.clip(of_tile(total) - (t0 - of_tile(reg0)), 0, per_tile).astype(jnp.int32)

    q = (t0 - of_tile(reg0))[:, None] + jnp.arange(per_tile, dtype=jnp.int32)[None, :]
    col_of_tile = lambda v: jnp.sum(jnp.where(pick[:, None, :], v[None, :, :], 0), axis=2)
    seg0_t, seg1_t, run0_t = col_of_tile(seg0), col_of_tile(seg0 + chunks), col_of_tile(run0)
    holds = (q[:, :, None] >= seg0_t[:, None, :]) & (q[:, :, None] < seg1_t[:, None, :])
    local = run0_t[:, None, :] + q[:, :, None] - seg0_t[:, None, :]
    row = jnp.asarray(tile_row0, jnp.int32)[None, None, :] + SUBLANES * local
    src = jnp.sum(jnp.where(holds, row, 0), axis=2)
    src = jnp.where(jnp.any(holds, axis=2), src, zero_row)
    n_first = jnp.sum((src < rows_a).astype(jnp.int32), axis=1)
    idx = jnp.arange(N_EXPERTS, dtype=jnp.int32)
    nonempty = total > 0
    ordinal = _excl_cumsum(nonempty.astype(jnp.int32), 0)
    later = jnp.where((idx[None, :] > idx[:, None]) & nonempty[None, :], idx[None, :], N_EXPERTS)
    following = jnp.min(later, axis=1)
    following = jnp.where(following == N_EXPERTS, -1, following)
    lane = jnp.arange(LIST_LANES, dtype=jnp.int32)[None, :]
    src = jnp.pad(src, ((0, 0), (0, LIST_LANES - per_tile)))
    for at, val in ((LIST_COUNT, n_first), (LIST_NEXT, of_tile(following)), (LIST_PARITY, of_tile(ordinal) % 2)):
        src = jnp.where(lane == at, val[:, None], src)
    src = src.reshape(n_tiles, 1, LIST_LANES).astype(jnp.int32)

    j = jnp.arange(LIST_LANES, dtype=jnp.int32)[None, :, None]
    inside = (j >= run0[:, None, :]) & (j < (run0 + chunks)[:, None, :])
    base = (reg0[None, :] + seg0 - run0)[:, None, :]
    dst = SUBLANES * jnp.sum(jnp.where(inside, base + j, 0), axis=2)
    dst = dst.reshape(n_tok, 1, LIST_LANES).astype(jnp.int32)
    return tile_expert.astype(jnp.int32), tile_valid, src, dst


def _moe_ple(h_a, h_b, p_a, p_b, g_ffn, w_router, before, sel, w_gate, w_up, w_down, g_ple, wgate_bf, wproj_bf,
              *, tm_a, tm_b, te, layer):
    xs_a, rec_a, cnt_a = _sort(h_a, g_ffn, w_router, before, sel, tm=tm_a)
    xs_b, rec_b, cnt_b = _sort(h_b, g_ffn, w_router, before, sel, tm=tm_b)
    t_a, t_b = cnt_a.shape[0], cnt_b.shape[0]
    cnt = jnp.concatenate([cnt_a[:, 0, :N_EXPERTS], cnt_b[:, 0, :N_EXPERTS]], axis=0).astype(jnp.int32)
    rows_a = xs_a.shape[0]
    row0 = np.concatenate([np.arange(t_a) * _slots(tm_a), rows_a + np.arange(t_b) * _slots(tm_b)])
    tile_expert, tile_valid, src, dst = _chunk_plan(
        cnt, row0, n_pairs=2 * (h_a.shape[0] + h_b.shape[0]), rows_a=rows_a,
        zero_row=rows_a + _slots(tm_b) - SUBLANES, te=te)
    ys = _experts(xs_a, xs_b, tile_expert, tile_valid, src, w_gate, w_up, w_down, te=te, layer=layer)
    out_a = _combine(h_a, rec_a, dst[:t_a], ys, p_a, g_ple, wgate_bf, wproj_bf, tm=tm_a, layer=layer)
    out_b = _combine(h_b, rec_b, dst[t_a:], ys, p_b, g_ple, wgate_bf, wproj_bf, tm=tm_b, layer=layer)
    return out_a, out_b


def _bucket_np(dist):
    max_exact = N_BUCKETS // 2
    d_f = np.maximum(dist, 1).astype(np.float32)
    large = max_exact + (np.log(d_f / np.float32(max_exact)) / np.float32(np.log(MAX_DISTANCE / max_exact))
                         * np.float32(N_BUCKETS - max_exact)).astype(np.int32)
    large = np.minimum(large, N_BUCKETS - 1)
    return np.where(dist < max_exact, dist, large).astype(np.int32)


def _bias_from_buckets(rel_bias, bucket, valid):
    onehot = (jnp.asarray(bucket)[..., None] == jnp.arange(N_BUCKETS, dtype=jnp.int32)).astype(F32)
    bias = jnp.einsum("...k,kh->h...", onehot, rel_bias.astype(F32), precision=HIGHEST)
    return jnp.where(jnp.asarray(valid)[None], bias, NEG)


def _block_order(dil):
    g = np.arange(Q_BLOCK) // SUBLANES
    j = np.arange(Q_BLOCK) % SUBLANES
    if dil == 1:
        return 16 * j + g
    if dil == 4:
        return 32 * (g // 4) + 4 * j + g % 4
    return SUBLANES * g + j


def _band_bias(rel_bias, dil):
    mu = _block_order(dil)
    qi = mu[:, None] + Q_BLOCK
    ki = np.concatenate([mu, mu + Q_BLOCK])[None, :]
    off = qi - ki
    valid = (off >= 0) & (off <= N_KEYS)
    bucket = _bucket_np(dil * np.clip(off, 0, N_KEYS))
    first = valid & (np.arange(2 * Q_BLOCK)[None, :] >= Q_BLOCK)
    tables = [_bias_from_buckets(rel_bias, bucket, v).reshape(N_HEADS * Q_BLOCK, 2 * Q_BLOCK) for v in (valid, first)]
    return jnp.stack(tables)


def _sample_tables(rel_bias, w_buf):
    qpos = w_buf + np.arange(SAMPLE_T)[:, None]
    pos = np.arange(w_buf + NEW_COLS)[None, :]
    dist = qpos - pos
    in_seq = (dist >= 0) & (pos < w_buf + SAMPLE_T)
    mult = np.zeros(dist.shape, np.float32)
    for (w, d) in PATTERNS:
        mult += in_seq & (dist % d == 0) & (dist <= w)
    bucket = _bucket_np(np.maximum(dist, 0))
    bias = jnp.transpose(_bias_from_buckets(rel_bias, bucket, mult > 0), (1, 0, 2))
    rows = SAMPLE_T * N_HEADS
    mult_rows = np.broadcast_to(mult[:, None, :], (SAMPLE_T, N_HEADS, mult.shape[-1]))
    return bias.reshape(rows, -1), jnp.asarray(mult_rows.reshape(rows, -1))


PROJ_TM = 512
PERM_ROWS = 256
OUT_TM = 512
PROMPT_TM = 256
PROMPT_TE = 512
ATTN_SUB = 4


def _row_perm(tm):
    a = np.arange(tm)
    src = (a // Q_BLOCK) * Q_BLOCK + 16 * (a % SUBLANES) + (a % Q_BLOCK) // SUBLANES
    perm = np.zeros((tm, tm), np.float32)
    perm[a, src] = 1.0
    return perm


def kernel(x_prompt, x_sample, cache_k, cache_v, state_conv, p_prompt, p_sample, rel_bias, g_mix, w_in, q_gain,
           k_gain, conv_w, g_out_att, g_out_conv, w_out, g_ffn, w_router_group, w_router_expert, w_gate, w_up,
           w_down, g_ple, w_ple_gate, w_ple_proj):
    depth = w_in.shape[0]
    batch, seq, _ = x_prompt.shape
    dec_b, dec_t, _ = x_sample.shape
    w_buf = cache_k.shape[2]
    n_s = dec_b * dec_t
    keep = min(w_buf, seq)
    assert dec_t == SAMPLE_T and w_buf % LANES == 0
    assert seq % (Q_BLOCK * 16 * 2) == 0 and keep % PROJ_TM == 0 and seq % PROMPT_TM == 0
    cache_kt = jnp.transpose(cache_k, (0, 1, 3, 4, 2))
    cache_vt = jnp.transpose(cache_v, (0, 1, 3, 4, 2))

    row = lambda a: a.reshape(1, -1)
    src_lane = np.arange(LANES)
    expand = jnp.asarray((src_lane[:, None] // LSE_LANES_PER_HEAD == np.arange(ATT_DIM)[None, :] // HEAD_DIM)
                         & (src_lane[:, None] % LSE_LANES_PER_HEAD == 0), BF16)
    before = jnp.asarray(np.arange(LANES)[:, None] < np.arange(LANES)[None, :], BF16)
    sel_np = np.zeros((SUBLANES, LANES), np.float32)
    sel_np[0, REC_S1] = sel_np[1, REC_S2] = 1.0
    sel = jnp.asarray(sel_np, BF16)
    perm = jnp.asarray(_row_perm(PERM_ROWS), BF16)
    unperm = jnp.asarray(_row_perm(OUT_TM).T, BF16)
    band = [_band_bias(rel_bias, d) for (_, d) in PATTERNS]
    s_bias, s_mult = _sample_tables(rel_bias, w_buf)

    hp = x_prompt.reshape(batch * seq, D_MODEL)
    hs = jnp.swapaxes(x_sample, 0, 1).reshape(n_s, D_MODEL)
    pp_all = p_prompt.reshape(depth, batch * seq, D_PLE)
    ps_all = jnp.swapaxes(p_sample, 1, 2).reshape(depth, n_s, D_PLE)
    new = {k: [] for k in ("kp", "vp", "cp", "ks", "vs", "cs")}
    hist_p = jnp.zeros((batch, SUBLANES, CONV_DIM), F32)

    for l in range(depth):
        w_in_bf = w_in[l].astype(BF16)
        wa_bf = w_out[l, :ATT_DIM].astype(BF16)
        wc_bf = w_out[l, ATT_DIM:].astype(BF16)
        wgate_bf = w_ple_gate[l].astype(BF16)
        wproj_bf = w_ple_proj[l].astype(BF16)
        w_router = jnp.concatenate(
            [w_router_expert[l], w_router_group[l],
             jnp.zeros((D_MODEL, LANES - N_EXPERTS - N_GROUPS), F32)], axis=1).astype(BF16)
        qg, kg = row(jnp.tile(q_gain[l], N_HEADS)), row(jnp.tile(k_gain[l], N_HEADS))
        mix = (row(g_mix[l]), w_in_bf, qg, kg, conv_w[l], row(g_out_conv[l]))
        moe = (row(g_ffn[l]), w_router, before, sel, w_gate, w_up, w_down, row(g_ple[l]), wgate_bf, wproj_bf)

        q, k, v, k_nat, v_nat, yn, nconv = _inproj(
            hp, hist_p, *mix, perm, tm=PROJ_TM, shift=1, tiles_per_seq=seq // PROJ_TM,
            keep_tiles=keep // PROJ_TM)
        os, lses = [], []
        for bias, (_, d) in zip(band, PATTERNS):
            sub = min(ATTN_SUB, seq // (Q_BLOCK * d))
            o, lse = _attn_pattern(q, k, v, bias, batch=batch, seq=seq, dil=d, sub=sub,
                                   res=min(d, ATTN_SUB // sub))
            os.append(o)
            lses.append(lse)
        hp = _outproj(os, lses, yn, hp, row(g_out_att[l]), wa_bf, wc_bf, expand, unperm, tm=OUT_TM)
        new["kp"].append(jnp.transpose(k_nat, (0, 3, 1, 2)))
        new["vp"].append(jnp.transpose(v_nat, (0, 3, 1, 2)))
        new["cp"].append(nconv[:, SUBLANES - 2:])

        hist_s = jnp.swapaxes(state_conv[l], 0, 1).reshape(1, 2 * dec_b, CONV_DIM)
        q, k, v, yn, nconv = _inproj(hs, hist_s, *mix, tm=n_s, shift=dec_b, tiles_per_seq=1)
        bmaj = lambda a: jnp.swapaxes(a.reshape(dec_t, dec_b, N_HEADS, HEAD_DIM), 0, 1)
        qb, kb, vb = bmaj(q), bmaj(k), bmaj(v)
        rows8 = lambda a: jnp.pad(a.reshape(dec_b, dec_t, ATT_DIM), ((0, 0), (0, SUBLANES - dec_t), (0, 0)))
        att = _attn_sample(qb.reshape(dec_b, dec_t, ATT_DIM), rows8(kb), rows8(vb), cache_kt, cache_vt, l,
                           s_bias, s_mult)
        att_tm = jnp.swapaxes(att, 0, 1).reshape(n_s, ATT_DIM)
        hs = _outproj([att_tm], [], yn, hs, row(g_out_att[l]), wa_bf, wc_bf, tm=n_s)
        hp, hs = _moe_ple(hp, hs, pp_all, ps_all, *moe, tm_a=PROMPT_TM, tm_b=n_s, te=PROMPT_TE, layer=l)
        new["ks"].append(kb)
        new["vs"].append(vb)
        new["cs"].append(jnp.swapaxes(nconv.reshape(2, dec_b, CONV_DIM), 0, 1))

    y_prompt = hp.reshape(batch, seq, D_MODEL)
    y_sample = jnp.swapaxes(hs.reshape(dec_t, dec_b, D_MODEL), 0, 1)
    st = lambda key: jnp.stack(new[key])
    return (y_prompt, y_sample, st("kp"), st("vp"), st("cp"), st("ks"), st("vs"), st("cs"))
```

```python
import functools

import jax
import jax.numpy as jnp
import numpy as np
from jax import lax
from jax.experimental import pallas as pl
from jax.experimental.pallas import tpu as pltpu

F32 = jnp.float32
BF16 = jnp.bfloat16
HIGHEST = lax.Precision.HIGHEST

D_MODEL = 1024
HEAD_DIM = 64
N_HEADS = 8
ATT_DIM = N_HEADS * HEAD_DIM
CONV_DIM = D_MODEL - ATT_DIM
MIX_IN = 3 * ATT_DIM + 3 * CONV_DIM
PATTERNS = ((128, 1), (512, 4), (2048, 16))
N_KEYS = 128
Q_BLOCK = 128
N_BUCKETS = 32
MAX_DISTANCE = 2048
N_GROUPS = 4
EXPERTS_PER_GROUP = 8
N_EXPERTS = N_GROUPS * EXPERTS_PER_GROUP
D_EXPERT = 256
D_PLE = 256
EPS = 1e-6
NEG = -1e30

LANES = 128
SUBLANES = 8
SLABS = Q_BLOCK // SUBLANES
LSE_LANES_PER_HEAD = LANES // N_HEADS
VMEM_LIMIT = 56 * 1024 * 1024
NT = (((1,), (1,)), ((), ()))


def _cparams(n_axes):
    return pltpu.CompilerParams(dimension_semantics=("arbitrary",) * n_axes,
                                vmem_limit_bytes=VMEM_LIMIT)


def _full(shape):
    n = len(shape)
    return pl.BlockSpec(shape, lambda *_: (0,) * n)


def _rms(x, gain):
    ms = jnp.mean(x * x, axis=-1, keepdims=True)
    return x * lax.rsqrt(ms + EPS) * gain


def _inproj_kernel(*refs, shift, tiles_per_seq, permute, first_kept=0):
    (h_ref, gmix_ref, w_ref, qg_ref, kg_ref, cw_ref, gconv_ref, hist_ref) = refs[:8]
    if permute:
        perm_ref, q_ref, k_ref, v_ref, kn_ref, vn_ref, yn_ref, nconv_ref, carry_ref = refs[8:]
    else:
        q_ref, k_ref, v_ref, yn_ref, nconv_ref, carry_ref = refs[8:]
    i = pl.program_id(0)
    a = _rms(h_ref[...], gmix_ref[...])
    proj = jnp.dot(a.astype(BF16), w_ref[...], preferred_element_type=F32)
    tm = proj.shape[0]

    lower = lax.broadcasted_iota(jnp.int32, (tm, LANES), 1) < HEAD_DIM

    def head_norm(t, g):
        out = []
        for j in range(ATT_DIM // LANES):
            blk = t[:, j * LANES:(j + 1) * LANES]
            sq = blk * blk
            ms_lo = jnp.sum(jnp.where(lower, sq, 0.0), axis=-1, keepdims=True) * (1.0 / HEAD_DIM)
            ms_hi = jnp.sum(jnp.where(lower, 0.0, sq), axis=-1, keepdims=True) * (1.0 / HEAD_DIM)
            scale = jnp.where(lower, lax.rsqrt(ms_lo + EPS), lax.rsqrt(ms_hi + EPS))
            out.append(blk * scale)
        return jnp.concatenate(out, axis=-1) * g

    q = head_norm(proj[:, 0:ATT_DIM], qg_ref[...])
    k = head_norm(proj[:, ATT_DIM:2 * ATT_DIM], kg_ref[...])
    v = proj[:, 2 * ATT_DIM:3 * ATT_DIM]
    if permute:
        @pl.when(i % tiles_per_seq >= first_kept)
        def _():
            kn_ref[...] = k.T.reshape(N_HEADS, HEAD_DIM, tm)
            vn_ref[...] = v.T.reshape(N_HEADS, HEAD_DIM, tm)
        qkv = jnp.concatenate([q * (HEAD_DIM ** -0.5), k, v], axis=-1).astype(BF16)
        pm = perm_ref.shape[0]
        moved = jnp.concatenate([jnp.dot(perm_ref[...], qkv[r:r + pm], preferred_element_type=F32)
                                 for r in range(0, tm, pm)], axis=0)
        q_ref[...] = _pack_bf16(moved[:, 0:ATT_DIM], is_bf16=True)
        k_ref[...] = _pack_bf16(moved[:, ATT_DIM:2 * ATT_DIM], is_bf16=True)
        v_ref[...] = _pack_bf16(moved[:, 2 * ATT_DIM:3 * ATT_DIM], is_bf16=True)
    else:
        q_ref[...] = q
        k_ref[...] = k
        v_ref[...] = v
    c0 = 3 * ATT_DIM
    hc = proj[:, c0:c0 + CONV_DIM]
    gb = proj[:, c0 + CONV_DIM:c0 + 2 * CONV_DIM]
    gc = proj[:, c0 + 2 * CONV_DIM:c0 + 3 * CONV_DIM]
    u = gc * hc

    if shift == 1:
        @pl.when(i % tiles_per_seq == 0)
        def _():
            carry_ref[...] = hist_ref[0]
        h0 = carry_ref[SUBLANES - 2:SUBLANES - 1, :]
        h1 = carry_ref[SUBLANES - 1:SUBLANES, :]
        row = lax.broadcasted_iota(jnp.int32, (tm, 1), 0)
        u1 = jnp.where(row == 0, h1, pltpu.roll(u, 1, 0))
        u2 = jnp.where(row == 0, h0, jnp.where(row == 1, h1, pltpu.roll(u, 2, 0)))
        carry_ref[...] = u[tm - SUBLANES:tm, :]
        nconv_ref[0] = u[tm - SUBLANES:tm, :]
    else:
        hist = hist_ref[0]
        u1 = jnp.concatenate([hist[shift:2 * shift], u[0:tm - shift]], axis=0)
        u2 = jnp.concatenate([hist, u[0:tm - 2 * shift]], axis=0)
        nconv_ref[0] = u[tm - 2 * shift:tm, :]
    conv = cw_ref[0:1, :] * u2 + cw_ref[1:2, :] * u1 + cw_ref[2:3, :] * u
    yn_ref[...] = _rms(gb * conv, gconv_ref[...])


def _inproj(h, hist, g_mix, w_in_bf, q_gain, k_gain, conv_w, g_out_conv, perm=None, *,
            tm, shift, tiles_per_seq, keep_tiles=0):
    n = h.shape[0]
    hist_rows = hist.shape[1]
    nseq = hist.shape[0]
    tok = lambda w: pl.BlockSpec((tm, w), lambda i: (i, 0))
    seq3 = lambda r: pl.BlockSpec((1, r, CONV_DIM), lambda i: (i // tiles_per_seq, 0, 0))
    nconv_rows = SUBLANES if shift == 1 else 2 * shift
    packed = perm is not None
    att_w = ATT_DIM // 2 if packed else ATT_DIM
    att = jax.ShapeDtypeStruct((n, att_w), jnp.uint32 if packed else F32)
    in_specs = [tok(D_MODEL), _full((1, D_MODEL)), _full((D_MODEL, MIX_IN)), _full((1, ATT_DIM)),
                _full((1, ATT_DIM)), _full((3, CONV_DIM)), _full((1, CONV_DIM)), seq3(hist_rows)]
    args = [h, g_mix, w_in_bf, q_gain, k_gain, conv_w, g_out_conv, hist]
    out_specs = [tok(att_w)] * 3
    out_shape = [att] * 3
    if perm is not None:
        in_specs.append(_full(perm.shape))
        args.append(perm)
        first = tiles_per_seq - keep_tiles
        kept = pl.BlockSpec((None, N_HEADS, HEAD_DIM, tm),
                            lambda i: (i // tiles_per_seq, 0, 0, jnp.maximum(i % tiles_per_seq - first, 0)))
        out_specs += [kept, kept]
        out_shape += [jax.ShapeDtypeStruct((nseq, N_HEADS, HEAD_DIM, keep_tiles * tm), F32)] * 2
    out_specs += [tok(CONV_DIM), seq3(nconv_rows)]
    out_shape += [jax.ShapeDtypeStruct((n, CONV_DIM), F32),
                  jax.ShapeDtypeStruct((nseq, nconv_rows, CONV_DIM), F32)]
    return pl.pallas_call(
        functools.partial(_inproj_kernel, shift=shift, tiles_per_seq=tiles_per_seq, permute=perm is not None,
                          first_kept=tiles_per_seq - keep_tiles),
        grid=(n // tm,),
        in_specs=in_specs, out_specs=out_specs, out_shape=out_shape,
        scratch_shapes=[pltpu.VMEM((SUBLANES, CONV_DIM), F32)],
        compiler_params=_cparams(1),
        name="inproj",
    )(*args)


def _attn_block(q, kk, vv, bias):
    lane = lax.broadcasted_iota(jnp.int32, (Q_BLOCK, LANES), 1)
    upper = lane >= HEAD_DIM
    scores = []
    for h in range(N_HEADS):
        j, e = divmod(h, 2)
        qp = q[:, j * LANES:(j + 1) * LANES]
        qm = (jnp.where(upper, qp, 0.0) if e else jnp.where(upper, 0.0, qp)).astype(BF16)
        scores.append(lax.dot_general(qm, kk[:, j * LANES:(j + 1) * LANES], NT, preferred_element_type=F32))
    s = jnp.concatenate(scores, axis=0) + bias
    m = jnp.max(s, axis=-1, keepdims=True)
    p = jnp.exp(s - m)
    den = jnp.sum(p, axis=-1, keepdims=True)
    pb = p.astype(BF16)
    head_rows = lambda t, h: t[h * Q_BLOCK:(h + 1) * Q_BLOCK]
    lse_grp = lane // LSE_LANES_PER_HEAD
    m_tile = jnp.zeros((Q_BLOCK, LANES), F32)
    den_tile = jnp.ones((Q_BLOCK, LANES), F32)
    outs = []
    for j in range(N_HEADS // 2):
        even, odd = 2 * j, 2 * j + 1
        vp = vv[:, j * LANES:(j + 1) * LANES]
        pair = jnp.where(upper, jnp.dot(head_rows(pb, odd), vp, preferred_element_type=F32),
                         jnp.dot(head_rows(pb, even), vp, preferred_element_type=F32))
        outs.append(pair / jnp.where(upper, head_rows(den, odd), head_rows(den, even)))
        for h in (even, odd):
            m_tile = jnp.where(lse_grp == h, head_rows(m, h), m_tile)
            den_tile = jnp.where(lse_grp == h, head_rows(den, h), den_tile)
    return jnp.concatenate(outs, axis=-1), m_tile + jnp.log(den_tile)


def _attn_kernel(q_ref, kp_ref, kc_ref, vp_ref, vc_ref, bias_ref, o_ref, lse_ref, kbuf, vbuf, *, sub, res):
    n = pl.program_id(2)
    rows = sub * Q_BLOCK
    half = ATT_DIM // 2
    for r in range(res):
        kbuf[0:Q_BLOCK, :] = _unpack_bf16(kp_ref[:, r].reshape(Q_BLOCK, half))
        kbuf[Q_BLOCK:, :] = _unpack_bf16(kc_ref[:, :, r].reshape(rows, half))
        vbuf[0:Q_BLOCK, :] = _unpack_bf16(vp_ref[:, r].reshape(Q_BLOCK, half))
        vbuf[Q_BLOCK:, :] = _unpack_bf16(vc_ref[:, :, r].reshape(rows, half))
        for j in range(sub):
            q = _unpack_bf16(q_ref[j, :, r].reshape(Q_BLOCK, half), F32)
            r0 = j * Q_BLOCK
            first = (n == 0).astype(jnp.int32) if j == 0 else 0
            o, lse = _attn_block(q, kbuf[r0:r0 + 2 * Q_BLOCK, :], vbuf[r0:r0 + 2 * Q_BLOCK, :], bias_ref[first])
            o_ref[j, :, r] = o.reshape(SLABS, SUBLANES, ATT_DIM)
            lse_ref[j, :, r] = lse.reshape(SLABS, SUBLANES, LANES)


def _attn_pattern(q, k, v, bias, *, batch, seq, dil, sub, res):
    nblk = seq // (Q_BLOCK * dil)
    view = lambda t: t.reshape(batch, nblk, SLABS, dil, SUBLANES, t.shape[-1])
    cur = lambda c: pl.BlockSpec((None, sub, SLABS, res, SUBLANES, c), lambda b, r, n: (b, n, 0, r, 0, 0))
    half = ATT_DIM // 2
    prev = pl.BlockSpec((None, None, SLABS, res, SUBLANES, half),
                        lambda b, r, n: (b, jnp.maximum(n * sub - 1, 0), 0, r, 0, 0))
    o, lse = pl.pallas_call(
        functools.partial(_attn_kernel, sub=sub, res=res),
        grid=(batch, dil // res, nblk // sub),
        in_specs=[cur(half), prev, cur(half), prev, cur(half),
                  _full((2, N_HEADS * Q_BLOCK, 2 * Q_BLOCK))],
        out_specs=[cur(ATT_DIM), cur(LANES)],
        out_shape=[jax.ShapeDtypeStruct((batch, nblk, SLABS, dil, SUBLANES, ATT_DIM), F32),
                   jax.ShapeDtypeStruct((batch, nblk, SLABS, dil, SUBLANES, LANES), F32)],
        scratch_shapes=[pltpu.VMEM(((sub + 1) * Q_BLOCK, ATT_DIM), BF16),
                        pltpu.VMEM(((sub + 1) * Q_BLOCK, ATT_DIM), BF16)],
        compiler_params=_cparams(3),
        name=f"attn_d{dil}",
    )(view(q), view(k), view(k), view(v), view(v), bias)
    return o.reshape(batch * seq, ATT_DIM), lse.reshape(batch * seq, LANES)


SAMPLE_T = 4
NEW_COLS = LANES


def _attn_sample_kernel(q_ref, kt_ref, kn_ref, vt_ref, vn_ref, bias_ref, mult_ref, o_ref):
    rows = SAMPLE_T * N_HEADS
    q4 = q_ref[...] * (HEAD_DIM ** -0.5)
    qt = jnp.concatenate([jnp.broadcast_to(q4[t:t + 1, :], (N_HEADS, ATT_DIM)) for t in range(SAMPLE_T)], axis=0)
    lane_head = lax.broadcasted_iota(jnp.int32, (rows, ATT_DIM), 1) // HEAD_DIM
    row_head = lax.broadcasted_iota(jnp.int32, (rows, ATT_DIM), 0) % N_HEADS
    own = lane_head == row_head
    qbd = jnp.where(own, qt, 0.0).astype(BF16)
    flat = lambda ref: ref[...].reshape(ATT_DIM, ref.shape[-1]).astype(BF16)
    pad = jnp.zeros((NEW_COLS - SUBLANES, ATT_DIM), F32)
    new_rows = lambda ref: jnp.concatenate([ref[...], pad], axis=0).astype(BF16)
    s = jnp.concatenate([jnp.dot(qbd, flat(kt_ref), preferred_element_type=F32),
                         lax.dot_general(qbd, new_rows(kn_ref), NT, preferred_element_type=F32)],
                        axis=-1) + bias_ref[...]
    m = jnp.max(s, axis=-1, keepdims=True)
    p = jnp.exp(s - m) * mult_ref[...]
    den = jnp.sum(p, axis=-1, keepdims=True)
    pb = p.astype(BF16)
    w_buf = kt_ref.shape[-1]
    acc = (lax.dot_general(pb[:, :w_buf], flat(vt_ref), NT, preferred_element_type=F32)
           + jnp.dot(pb[:, w_buf:], new_rows(vn_ref), preferred_element_type=F32))
    acc = jnp.where(own, acc / den, 0.0)
    for t in range(SAMPLE_T):
        o_ref[t:t + 1, :] = jnp.sum(acc[t * N_HEADS:(t + 1) * N_HEADS, :], axis=0, keepdims=True)


def _attn_sample(q, k_new, v_new, cache_kt, cache_vt, layer, bias, mult):
    nb = q.shape[0]
    w_buf = cache_kt.shape[-1]
    tok = pl.BlockSpec((None, SAMPLE_T, ATT_DIM), lambda b: (b, 0, 0))
    new = pl.BlockSpec((None, SUBLANES, ATT_DIM), lambda b: (b, 0, 0))
    old = pl.BlockSpec((None, None, N_HEADS, HEAD_DIM, w_buf), lambda b: (layer, b, 0, 0, 0))
    tbl = _full((SAMPLE_T * N_HEADS, w_buf + NEW_COLS))
    return pl.pallas_call(
        _attn_sample_kernel,
        grid=(nb,),
        in_specs=[tok, old, new, old, new, tbl, tbl],
        out_specs=tok,
        out_shape=jax.ShapeDtypeStruct((nb, SAMPLE_T, ATT_DIM), F32),
        compiler_params=_cparams(1),
        name="attn_sample",
    )(q, cache_kt, k_new, cache_vt, v_new, bias, mult)


def _split_dot(x, e_ref):
    hi = x.astype(BF16)
    lo = (x - hi.astype(F32)).astype(BF16)
    return (jnp.dot(hi, e_ref[...], preferred_element_type=F32)
            + jnp.dot(lo, e_ref[...], preferred_element_type=F32))


def _outproj_kernel(*refs, n_pat):
    mix = n_pat > 1
    n_lse = n_pat if mix else 0
    o_refs = refs[0:n_pat]
    l_refs = refs[n_pat:n_pat + n_lse]
    rest = refs[n_pat + n_lse:]
    if mix:
        yn_ref, h_ref, gatt_ref, exp_ref, unperm_ref, wa_ref, wc_ref, out_ref = rest
        lses = [r[...] for r in l_refs]
        top = functools.reduce(jnp.maximum, lses)
        ws = [jnp.exp(l - top) for l in lses]
        tot = functools.reduce(lambda a, b: a + b, ws)
        att = None
        for w, o_ref in zip(ws, o_refs):
            term = _split_dot(w / tot, exp_ref) * o_ref[...]
            att = term if att is None else att + term
        att_bf = jnp.dot(unperm_ref[...], _rms(att, gatt_ref[...]).astype(BF16),
                         preferred_element_type=F32).astype(BF16)
    else:
        yn_ref, h_ref, gatt_ref, wa_ref, wc_ref, out_ref = rest
        att_bf = _rms(o_refs[0][...], gatt_ref[...]).astype(BF16)
    y = (jnp.dot(att_bf, wa_ref[...], preferred_element_type=F32)
         + jnp.dot(yn_ref[...].astype(BF16), wc_ref[...], preferred_element_type=F32))
    out_ref[...] = h_ref[...] + y


def _outproj(os, lses, yn, h, g_att, wa_bf, wc_bf, expand=None, unperm=None, *, tm):
    n = h.shape[0]
    n_pat = len(os)
    tok = lambda w: pl.BlockSpec((tm, w), lambda i: (i, 0))
    in_specs = [tok(ATT_DIM)] * n_pat + [tok(LANES)] * len(lses) + [tok(CONV_DIM), tok(D_MODEL), _full((1, ATT_DIM))]
    args = [*os, *lses, yn, h, g_att]
    if n_pat > 1:
        in_specs += [_full((LANES, ATT_DIM)), _full((tm, tm))]
        args += [expand, unperm]
    in_specs += [_full((ATT_DIM, D_MODEL)), _full((CONV_DIM, D_MODEL))]
    args += [wa_bf, wc_bf]
    return pl.pallas_call(
        functools.partial(_outproj_kernel, n_pat=n_pat),
        grid=(n // tm,),
        in_specs=in_specs,
        out_specs=tok(D_MODEL),
        out_shape=jax.ShapeDtypeStruct((n, D_MODEL), F32),
        compiler_params=_cparams(1),
        name="outproj",
    )(*args)


GROUP_LANE0 = N_EXPERTS
REC_E1, REC_E2, REC_S1, REC_S2, REC_W1, REC_W2 = range(6)
PACKED = D_MODEL // 2
ROW_W = PACKED + LANES
N_DMA_PRIORITIES = 2
LIST_LANES = LANES
LIST_COUNT = LIST_LANES - 1
LIST_NEXT = LIST_LANES - 2
LIST_PARITY = LIST_LANES - 3


def _slots(tm):
    need = 2 * tm + (SUBLANES - 1) * N_EXPERTS + SUBLANES
    return -(-need // LANES) * LANES


def _pieces(x):
    hi = x.astype(BF16)
    r1 = x - hi.astype(F32)
    mid = r1.astype(BF16)
    return hi, mid, (r1 - mid.astype(F32)).astype(BF16)


def _sort_kernel(h_ref, g_ref, wr_ref, before_ref, sel_ref, xs_ref, rec_ref, cnt_ref, *, group):
    tm = h_ref.shape[0] // group
    slots = xs_ref.shape[0] // group
    for t in range(group):
        xs, rec, cnt = _sort_tile(h_ref[t * tm:(t + 1) * tm, :], g_ref, wr_ref, before_ref, sel_ref, slots)
        xs_ref[t * slots:(t + 1) * slots, :] = xs
        rec_ref[t * tm:(t + 1) * tm, :] = rec
        cnt_ref[t] = cnt


def _sort_tile(h, g_ref, wr_ref, before_ref, sel_ref, slots):
    m = _rms(h, g_ref[...])
    logits = jnp.dot(m.astype(BF16), wr_ref[...], preferred_element_type=F32)
    tm = logits.shape[0]
    lane_i = lax.broadcasted_iota(jnp.int32, (tm, LANES), 1)
    lane = lane_i.astype(F32)
    big = jnp.float32(4 * LANES)

    is_g = jnp.logical_and(lane_i >= GROUP_LANE0, lane_i < GROUP_LANE0 + N_GROUPS)
    gl = jnp.where(is_g, logits, NEG)
    gmax = jnp.max(gl, axis=-1, keepdims=True)
    g_w = 1.0 / jnp.sum(jnp.where(is_g, jnp.exp(gl - gmax), 0.0), axis=-1, keepdims=True)
    g_sel = jnp.min(jnp.where(gl == gmax, lane - GROUP_LANE0, big), axis=-1, keepdims=True)

    grp_of_lane = (lane_i // EXPERTS_PER_GROUP).astype(F32)
    in_grp = jnp.logical_and(lane_i < N_EXPERTS, grp_of_lane == g_sel)
    el = jnp.where(in_grp, logits, NEG)
    t1 = jnp.max(el, axis=-1, keepdims=True)
    e1 = jnp.min(jnp.where(el == t1, lane, big), axis=-1, keepdims=True)
    el2 = jnp.where(lane == e1, NEG, el)
    t2 = jnp.max(el2, axis=-1, keepdims=True)
    e2 = jnp.min(jnp.where(el2 == t2, lane, big), axis=-1, keepdims=True)
    ex = jnp.exp(t2 - t1)
    w1 = g_w / (1.0 + ex)
    w2 = g_w * ex / (1.0 + ex)

    hit1 = lane == e1
    hit2 = lane == e2
    c = jnp.where(hit1, 1.0, jnp.where(hit2, 1.0, 0.0))
    rr = lax.broadcasted_iota(jnp.int32, (tm, tm), 0)
    cc = lax.broadcasted_iota(jnp.int32, (tm, tm), 1)
    lower = jnp.where(rr > cc, 1.0, 0.0).astype(BF16)
    rank = jnp.dot(lower, c.astype(BF16), preferred_element_type=F32)
    cnt = jnp.sum(c, axis=0, keepdims=True)
    chunks = jnp.floor((cnt + (SUBLANES - 1)) * (1.0 / SUBLANES))
    start = SUBLANES * jnp.dot(jnp.broadcast_to(chunks, (SUBLANES, LANES)).astype(BF16), before_ref[...],
                               preferred_element_type=F32)[0:1, :]
    slot_of = rank + start
    s1 = jnp.sum(jnp.where(hit1, slot_of, 0.0), axis=-1, keepdims=True)
    s2 = jnp.sum(jnp.where(hit2, slot_of, 0.0), axis=-1, keepdims=True)

    rec = jnp.zeros((tm, LANES), F32)
    for idx, val in ((REC_E1, e1), (REC_E2, e2), (REC_S1, s1), (REC_S2, s2), (REC_W1, w1), (REC_W2, w2)):
        rec = jnp.where(lane_i == idx, val, rec)

    rec_parts = _pieces(rec)[:2]
    srow = sum(lax.dot_general(sel_ref[...], part, NT, preferred_element_type=F32) for part in rec_parts)
    slot_id = lax.broadcasted_iota(jnp.int32, (slots, tm), 0).astype(F32)
    place = jnp.where(slot_id == srow[0:1, :], 1.0, jnp.where(slot_id == srow[1:2, :], 1.0, 0.0)).astype(BF16)
    payload = jnp.concatenate([m.astype(BF16), *rec_parts], axis=-1)
    moved = jnp.dot(place, payload, preferred_element_type=F32)
    info = moved[:, D_MODEL:D_MODEL + LANES] + moved[:, D_MODEL + LANES:]
    xs = jnp.concatenate([_pack_bf16(moved[:, :D_MODEL], is_bf16=True), pltpu.bitcast(info, jnp.uint32)], axis=-1)
    return xs, rec, jnp.broadcast_to(cnt, (SUBLANES, LANES))


def _sort(h, g_ffn, w_router, before, sel, *, tm):
    n = h.shape[0]
    slots = _slots(tm)
    group = 2 if (n // tm) % 2 == 0 else 1
    tok = lambda w: pl.BlockSpec((group * tm, w), lambda i: (i, 0))
    return pl.pallas_call(
        functools.partial(_sort_kernel, group=group),
        grid=(n // (group * tm),),
        in_specs=[tok(D_MODEL), _full((1, D_MODEL)), _full((D_MODEL, LANES)), _full((LANES, LANES)),
                  _full((SUBLANES, LANES))],
        out_specs=[pl.BlockSpec((group * slots, ROW_W), lambda i: (i, 0)), tok(LANES),
                   pl.BlockSpec((group, SUBLANES, LANES), lambda i: (i, 0, 0))],
        out_shape=[jax.ShapeDtypeStruct((n // tm * slots, ROW_W), jnp.uint32), jax.ShapeDtypeStruct((n, LANES), F32),
                   jax.ShapeDtypeStruct((n // tm, SUBLANES, LANES), F32)],
        compiler_params=_cparams(1),
        name="moe_sort",
    )(h, g_ffn, w_router, before, sel)


def _chunk_copy(src_hbm, row, dst, c, sem):
    if not isinstance(row, int):
        row = pl.multiple_of(row, SUBLANES)
    to = c * SUBLANES
    if not isinstance(to, int):
        to = pl.multiple_of(to, SUBLANES)
    return pltpu.make_async_copy(src_hbm.at[pl.ds(row, SUBLANES), :], dst.at[pl.ds(to, SUBLANES), :], sem)


def _expert_kernel(te_ref, tv_ref, lst_ref, xa_ref, xb_ref, wg_ref, wu_ref, wd_ref, y_ref,
                    xbuf, sem, wg_st, wu_st, wd_st, wsem, wg_bf, wu_bf, wd_bf, *, rows_a, layer):
    i = pl.program_id(0)
    n = pl.num_programs(0)
    te = xbuf.shape[1]
    slot = i % 2

    def fetch(tile, to_slot):
        n_a = lst_ref[tile, LIST_COUNT]

        def from_a(c, carry):
            priority = c % N_DMA_PRIORITIES if isinstance(c, int) else 0
            _chunk_copy(xa_ref, lst_ref[tile, c], xbuf.at[to_slot], c, sem.at[to_slot]).start(priority=priority)
            return carry

        def from_b(c, carry):
            _chunk_copy(xb_ref, lst_ref[tile, c] - rows_a, xbuf.at[to_slot], c, sem.at[to_slot]).start()
            return carry

        per_tile = te // SUBLANES

        @pl.when(n_a == per_tile)
        def _():
            for c in range(per_tile):
                from_a(c, 0)

        @pl.when(n_a != per_tile)
        def _():
            lax.fori_loop(0, n_a, from_a, 0)
            lax.fori_loop(n_a, per_tile, from_b, 0)

    @pl.when(jnp.logical_and(i == 0, tv_ref[0] > 0))
    def _():
        fetch(0, 0)

    nxt = jnp.minimum(i + 1, n - 1)

    @pl.when(jnp.logical_and(i + 1 < n, tv_ref[nxt] > 0))
    def _():
        fetch(nxt, 1 - slot)

    valid = tv_ref[i]
    changed = jnp.logical_or(i == 0, te_ref[i] != te_ref[jnp.maximum(i - 1, 0)])

    def weight_copies(expert, s):
        return [pltpu.make_async_copy(src.at[layer, expert], dst.at[s], wsem.at[s])
                for src, dst in ((wg_ref, wg_st), (wu_ref, wu_st), (wd_ref, wd_st))]

    @pl.when(jnp.logical_and(changed, valid > 0))
    def _():
        s = lst_ref[i, LIST_PARITY]

        @pl.when(i == 0)
        def _():
            for cp in weight_copies(te_ref[0], 0):
                cp.start()

        for cp in weight_copies(te_ref[i], s):
            cp.wait()
        wg_bf[...] = wg_st[s].astype(BF16)
        wu_bf[...] = wu_st[s].astype(BF16)
        wd_bf[...] = wd_st[s].astype(BF16)
        nxt_expert = lst_ref[i, LIST_NEXT]

        @pl.when(nxt_expert >= 0)
        def _():
            for cp in weight_copies(nxt_expert, 1 - s):
                cp.start()

    @pl.when(valid > 0)
    def _():
        pltpu.make_async_copy(xbuf.at[slot], xbuf.at[slot], sem.at[slot]).wait()
        rows = xbuf[slot]
        x = _unpack_bf16(rows[:, :PACKED])
        info = pltpu.bitcast(rows[:, PACKED:], F32)
        mine = info[:, REC_E1:REC_E1 + 1] == te_ref[i].astype(F32)
        gate = jnp.where(mine, info[:, REC_W1:REC_W1 + 1], info[:, REC_W2:REC_W2 + 1])
        hg = jnp.dot(x, wg_bf[...], preferred_element_type=F32)
        hu = jnp.dot(x, wu_bf[...], preferred_element_type=F32)
        hid = (hg * jax.nn.sigmoid(hg)) * hu * gate
        y = jnp.dot(hid.astype(BF16), wd_bf[...], preferred_element_type=F32)
        y_ref[...] = _pack_bf16(y)

    @pl.when(valid <= 0)
    def _():
        y_ref[...] = jnp.zeros_like(y_ref)


def _pack_bf16(x, is_bf16=False):
    w = x.shape[-1] // 2
    bits = lambda a: pltpu.bitcast(a if is_bf16 else a.astype(BF16).astype(F32), jnp.uint32)
    return jnp.bitwise_or(lax.shift_right_logical(bits(x[:, :w]), jnp.uint32(16)),
                          jnp.bitwise_and(bits(x[:, w:]), jnp.uint32(0xFFFF0000)))


def _unpack_bf16(words, dtype=BF16):
    as_f32 = lambda a: pltpu.bitcast(a, F32)
    return jnp.concatenate([as_f32(lax.shift_left(words, jnp.uint32(16))),
                            as_f32(jnp.bitwise_and(words, jnp.uint32(0xFFFF0000)))], axis=-1).astype(dtype)


def _experts(xs_a, xs_b, tile_expert, tile_valid, src, w_gate, w_up, w_down, *, te, layer):
    n_tiles = tile_expert.shape[0]
    hbm = pl.BlockSpec(memory_space=pl.ANY)
    grid_spec = pltpu.PrefetchScalarGridSpec(
        num_scalar_prefetch=3,
        grid=(n_tiles,),
        in_specs=[hbm, hbm, hbm, hbm, hbm],
        out_specs=pl.BlockSpec((te, PACKED), lambda i, e, v, s: (i, 0)),
        scratch_shapes=[pltpu.VMEM((2, te, ROW_W), jnp.uint32), pltpu.SemaphoreType.DMA((2,)),
                        pltpu.VMEM((2, D_MODEL, D_EXPERT), F32), pltpu.VMEM((2, D_MODEL, D_EXPERT), F32),
                        pltpu.VMEM((2, D_EXPERT, D_MODEL), F32), pltpu.SemaphoreType.DMA((2,)),
                        pltpu.VMEM((D_MODEL, D_EXPERT), BF16), pltpu.VMEM((D_MODEL, D_EXPERT), BF16),
                        pltpu.VMEM((D_EXPERT, D_MODEL), BF16)])
    return pl.pallas_call(
        functools.partial(_expert_kernel, rows_a=xs_a.shape[0], layer=layer),
        grid_spec=grid_spec,
        out_shape=jax.ShapeDtypeStruct((n_tiles * te, PACKED), jnp.uint32),
        compiler_params=_cparams(1),
        name="experts",
    )(tile_expert, tile_valid, src.reshape(n_tiles, LIST_LANES), xs_a, xs_b, w_gate, w_up, w_down)


def _combine_kernel(lst_ref, h_ref, rec_ref, ys_ref, p_ref, gple_ref, wgate_ref, wproj_ref, out_ref,
                     ybuf, sem):
    i = pl.program_id(0)
    n = pl.num_programs(0)
    slot = i % 2
    tm = h_ref.shape[0]
    slots = ybuf.shape[1]

    def fetch(tile, to_slot):
        for c in range(slots // SUBLANES):
            _chunk_copy(ys_ref, lst_ref[tile, c], ybuf.at[to_slot], c, sem.at[to_slot]).start(
                priority=c % N_DMA_PRIORITIES)

    @pl.when(i == 0)
    def _():
        fetch(0, 0)

    @pl.when(i + 1 < n)
    def _():
        fetch(i + 1, 1 - slot)

    pltpu.make_async_copy(ybuf.at[slot], ybuf.at[slot], sem.at[slot]).wait()

    rec = rec_ref[...]
    slot_id = lax.broadcasted_iota(jnp.int32, (tm, slots), 1).astype(F32)
    back = jnp.where(slot_id == rec[:, REC_S1:REC_S1 + 1], 1.0,
                     jnp.where(slot_id == rec[:, REC_S2:REC_S2 + 1], 1.0, 0.0)).astype(BF16)
    h2 = h_ref[...] + jnp.dot(back, _unpack_bf16(ybuf[slot]), preferred_element_type=F32)
    gate = jax.nn.sigmoid(jnp.dot(_rms(h2, gple_ref[...]).astype(BF16), wgate_ref[...], preferred_element_type=F32))
    ple = jnp.dot(p_ref[...].astype(BF16), wproj_ref[...], preferred_element_type=F32)
    out_ref[...] = h2 + ple * gate


def _combine(h, rec, dst, ys, p, g_ple, wgate_bf, wproj_bf, *, tm, layer):
    n = h.shape[0]
    n_tok = n // tm
    tok = lambda w: pl.BlockSpec((tm, w), lambda i, s: (i, 0))
    full = lambda shape: pl.BlockSpec(shape, lambda i, s: (0,) * len(shape))
    grid_spec = pltpu.PrefetchScalarGridSpec(
        num_scalar_prefetch=1,
        grid=(n_tok,),
        in_specs=[tok(D_MODEL), tok(LANES), pl.BlockSpec(memory_space=pl.ANY),
                  pl.BlockSpec((None, tm, D_PLE), lambda i, s: (layer, i, 0)),
                  full((1, D_MODEL)), full((D_MODEL, D_MODEL)), full((D_PLE, D_MODEL))],
        out_specs=tok(D_MODEL),
        scratch_shapes=[pltpu.VMEM((2, _slots(tm), PACKED), jnp.uint32), pltpu.SemaphoreType.DMA((2,))])
    return pl.pallas_call(
        _combine_kernel,
        grid_spec=grid_spec,
        out_shape=jax.ShapeDtypeStruct((n, D_MODEL), F32),
        compiler_params=_cparams(1),
        name="combine",
    )(dst.reshape(n_tok, LIST_LANES), h, rec, ys, p, g_ple, wgate_bf, wproj_bf)


def _excl_cumsum(x, axis):
    return jnp.cumsum(x, axis=axis) - x


def _chunk_plan(cnt, tile_row0, *, n_pairs, rows_a, zero_row, te):
    n_tok = cnt.shape[0]
    per_tile = te // SUBLANES
    chunks = (cnt + SUBLANES - 1) // SUBLANES
    run0 = _excl_cumsum(chunks, 1)
    seg0 = _excl_cumsum(chunks, 0)
    total = jnp.sum(chunks, axis=0)
    region = ((total + per_tile - 1) // per_tile) * per_tile
    reg_end = jnp.cumsum(region)
    reg0 = reg_end - region

    n_tiles = -(-(n_pairs + (SUBLANES - 1) * N_EXPERTS * n_tok) // te) + N_EXPERTS
    t0 = jnp.arange(n_tiles, dtype=jnp.int32) * per_tile
    tile_expert = jnp.minimum(jnp.sum((t0[:, None] >= reg_end[None, :]).astype(jnp.int32), axis=1), N_EXPERTS - 1)
    pick = tile_expert[:, None] == jnp.arange(N_EXPERTS, dtype=jnp.int32)[None, :]
    of_tile = lambda v: jnp.sum(jnp.where(pick, v[None, :], 0), axis=1)
    tile_valid = jnp.clip(of_tile(total) - (t0 - of_tile(reg0)), 0, per_tile).astype(jnp.int32)

    q = (t0 - of_tile(reg0))[:, None] + jnp.arange(per_tile, dtype=jnp.int32)[None, :]
    col_of_tile = lambda v: jnp.sum(jnp.where(pick[:, None, :], v[None, :, :], 0), axis=2)
    seg0_t, seg1_t, run0_t = col_of_tile(seg0), col_of_tile(seg0 + chunks), col_of_tile(run0)
    holds = (q[:, :, None] >= seg0_t[:, None, :]) & (q[:, :, None] < seg1_t[:, None, :])
    local = run0_t[:, None, :] + q[:, :, None] - seg0_t[:, None, :]
    row = jnp.asarray(tile_row0, jnp.int32)[None, None, :] + SUBLANES * local
    src = jnp.sum(jnp.where(holds, row, 0), axis=2)
    src = jnp.where(jnp.any(holds, axis=2), src, zero_row)
    n_first = jnp.sum((src < rows_a).astype(jnp.int32), axis=1)
    idx = jnp.arange(N_EXPERTS, dtype=jnp.int32)
    nonempty = total > 0
    ordinal = _excl_cumsum(nonempty.astype(jnp.int32), 0)
    later = jnp.where((idx[None, :] > idx[:, None]) & nonempty[None, :], idx[None, :], N_EXPERTS)
    following = jnp.min(later, axis=1)
    following = jnp.where(following == N_EXPERTS, -1, following)
    lane = jnp.arange(LIST_LANES, dtype=jnp.int32)[None, :]
    src = jnp.pad(src, ((0, 0), (0, LIST_LANES - per_tile)))
    for at, val in ((LIST_COUNT, n_first), (LIST_NEXT, of_tile(following)), (LIST_PARITY, of_tile(ordinal) % 2)):
        src = jnp.where(lane == at, val[:, None], src)
    src = src.reshape(n_tiles, 1, LIST_LANES).astype(jnp.int32)

    j = jnp.arange(LIST_LANES, dtype=jnp.int32)[None, :, None]
    inside = (j >= run0[:, None, :]) & (j < (run0 + chunks)[:, None, :])
    base = (reg0[None, :] + seg0 - run0)[:, None, :]
    dst = SUBLANES * jnp.sum(jnp.where(inside, base + j, 0), axis=2)
    dst = dst.reshape(n_tok, 1, LIST_LANES).astype(jnp.int32)
    return tile_expert.astype(jnp.int32), tile_valid, src, dst


def _moe_ple(h_a, h_b, p_a, p_b, g_ffn, w_router, before, sel, w_gate, w_up, w_down, g_ple, wgate_bf, wproj_bf,
              *, tm_a, tm_b, te, layer):
    xs_a, rec_a, cnt_a = _sort(h_a, g_ffn, w_router, before, sel, tm=tm_a)
    xs_b, rec_b, cnt_b = _sort(h_b, g_ffn, w_router, before, sel, tm=tm_b)
    t_a, t_b = cnt_a.shape[0], cnt_b.shape[0]
    cnt = jnp.concatenate([cnt_a[:, 0, :N_EXPERTS], cnt_b[:, 0, :N_EXPERTS]], axis=0).astype(jnp.int32)
    rows_a = xs_a.shape[0]
    row0 = np.concatenate([np.arange(t_a) * _slots(tm_a), rows_a + np.arange(t_b) * _slots(tm_b)])
    tile_expert, tile_valid, src, dst = _chunk_plan(
        cnt, row0, n_pairs=2 * (h_a.shape[0] + h_b.shape[0]), rows_a=rows_a,
        zero_row=rows_a + _slots(tm_b) - SUBLANES, te=te)
    ys = _experts(xs_a, xs_b, tile_expert, tile_valid, src, w_gate, w_up, w_down, te=te, layer=layer)
    out_a = _combine(h_a, rec_a, dst[:t_a], ys, p_a, g_ple, wgate_bf, wproj_bf, tm=tm_a, layer=layer)
    out_b = _combine(h_b, rec_b, dst[t_a:], ys, p_b, g_ple, wgate_bf, wproj_bf, tm=tm_b, layer=layer)
    return out_a, out_b


def _bucket_np(dist):
    max_exact = N_BUCKETS // 2
    d_f = np.maximum(dist, 1).astype(np.float32)
    large = max_exact + (np.log(d_f / np.float32(max_exact)) / np.float32(np.log(MAX_DISTANCE / max_exact))
                         * np.float32(N_BUCKETS - max_exact)).astype(np.int32)
    large = np.minimum(large, N_BUCKETS - 1)
    return np.where(dist < max_exact, dist, large).astype(np.int32)


def _bias_from_buckets(rel_bias, bucket, valid):
    onehot = (jnp.asarray(bucket)[..., None] == jnp.arange(N_BUCKETS, dtype=jnp.int32)).astype(F32)
    bias = jnp.einsum("...k,kh->h...", onehot, rel_bias.astype(F32), precision=HIGHEST)
    return jnp.where(jnp.asarray(valid)[None], bias, NEG)


def _block_order(dil):
    g = np.arange(Q_BLOCK) // SUBLANES
    j = np.arange(Q_BLOCK) % SUBLANES
    if dil == 1:
        return 16 * j + g
    if dil == 4:
        return 32 * (g // 4) + 4 * j + g % 4
    return SUBLANES * g + j


def _band_bias(rel_bias, dil):
    mu = _block_order(dil)
    qi = mu[:, None] + Q_BLOCK
    ki = np.concatenate([mu, mu + Q_BLOCK])[None, :]
    off = qi - ki
    valid = (off >= 0) & (off <= N_KEYS)
    bucket = _bucket_np(dil * np.clip(off, 0, N_KEYS))
    first = valid & (np.arange(2 * Q_BLOCK)[None, :] >= Q_BLOCK)
    tables = [_bias_from_buckets(rel_bias, bucket, v).reshape(N_HEADS * Q_BLOCK, 2 * Q_BLOCK) for v in (valid, first)]
    return jnp.stack(tables)


def _sample_tables(rel_bias, w_buf):
    qpos = w_buf + np.arange(SAMPLE_T)[:, None]
    pos = np.arange(w_buf + NEW_COLS)[None, :]
    dist = qpos - pos
    in_seq = (dist >= 0) & (pos < w_buf + SAMPLE_T)
    mult = np.zeros(dist.shape, np.float32)
    for (w, d) in PATTERNS:
        mult += in_seq & (dist % d == 0) & (dist <= w)
    bucket = _bucket_np(np.maximum(dist, 0))
    bias = jnp.transpose(_bias_from_buckets(rel_bias, bucket, mult > 0), (1, 0, 2))
    rows = SAMPLE_T * N_HEADS
    mult_rows = np.broadcast_to(mult[:, None, :], (SAMPLE_T, N_HEADS, mult.shape[-1]))
    return bias.reshape(rows, -1), jnp.asarray(mult_rows.reshape(rows, -1))


PROJ_TM = 512
PERM_ROWS = 256
OUT_TM = 512
PROMPT_TM = 256
PROMPT_TE = 512
ATTN_SUB = 8


def _row_perm(tm):
    a = np.arange(tm)
    src = (a // Q_BLOCK) * Q_BLOCK + 16 * (a % SUBLANES) + (a % Q_BLOCK) // SUBLANES
    perm = np.zeros((tm, tm), np.float32)
    perm[a, src] = 1.0
    return perm


def kernel(x_prompt, x_sample, cache_k, cache_v, state_conv, p_prompt, p_sample, rel_bias, g_mix, w_in, q_gain,
           k_gain, conv_w, g_out_att, g_out_conv, w_out, g_ffn, w_router_group, w_router_expert, w_gate, w_up,
           w_down, g_ple, w_ple_gate, w_ple_proj):
    depth = w_in.shape[0]
    batch, seq, _ = x_prompt.shape
    dec_b, dec_t, _ = x_sample.shape
    w_buf = cache_k.shape[2]
    n_s = dec_b * dec_t
    keep = min(w_buf, seq)
    assert dec_t == SAMPLE_T and w_buf % LANES == 0
    assert seq % (Q_BLOCK * 16 * 2) == 0 and keep % PROJ_TM == 0 and seq % PROMPT_TM == 0
    cache_kt = jnp.transpose(cache_k, (0, 1, 3, 4, 2))
    cache_vt = jnp.transpose(cache_v, (0, 1, 3, 4, 2))

    row = lambda a: a.reshape(1, -1)
    src_lane = np.arange(LANES)
    expand = jnp.asarray((src_lane[:, None] // LSE_LANES_PER_HEAD == np.arange(ATT_DIM)[None, :] // HEAD_DIM)
                         & (src_lane[:, None] % LSE_LANES_PER_HEAD == 0), BF16)
    before = jnp.asarray(np.arange(LANES)[:, None] < np.arange(LANES)[None, :], BF16)
    sel_np = np.zeros((SUBLANES, LANES), np.float32)
    sel_np[0, REC_S1] = sel_np[1, REC_S2] = 1.0
    sel = jnp.asarray(sel_np, BF16)
    perm = jnp.asarray(_row_perm(PERM_ROWS), BF16)
    unperm = jnp.asarray(_row_perm(OUT_TM).T, BF16)
    band = [_band_bias(rel_bias, d) for (_, d) in PATTERNS]
    s_bias, s_mult = _sample_tables(rel_bias, w_buf)

    hp = x_prompt.reshape(batch * seq, D_MODEL)
    hs = jnp.swapaxes(x_sample, 0, 1).reshape(n_s, D_MODEL)
    pp_all = p_prompt.reshape(depth, batch * seq, D_PLE)
    ps_all = jnp.swapaxes(p_sample, 1, 2).reshape(depth, n_s, D_PLE)
    new = {k: [] for k in ("kp", "vp", "cp", "ks", "vs", "cs")}
    hist_p = jnp.zeros((batch, SUBLANES, CONV_DIM), F32)

    for l in range(depth):
        w_in_bf = w_in[l].astype(BF16)
        wa_bf = w_out[l, :ATT_DIM].astype(BF16)
        wc_bf = w_out[l, ATT_DIM:].astype(BF16)
        wgate_bf = w_ple_gate[l].astype(BF16)
        wproj_bf = w_ple_proj[l].astype(BF16)
        w_router = jnp.concatenate(
            [w_router_expert[l], w_router_group[l],
             jnp.zeros((D_MODEL, LANES - N_EXPERTS - N_GROUPS), F32)], axis=1).astype(BF16)
        qg, kg = row(jnp.tile(q_gain[l], N_HEADS)), row(jnp.tile(k_gain[l], N_HEADS))
        mix = (row(g_mix[l]), w_in_bf, qg, kg, conv_w[l], row(g_out_conv[l]))
        moe = (row(g_ffn[l]), w_router, before, sel, w_gate, w_up, w_down, row(g_ple[l]), wgate_bf, wproj_bf)

        q, k, v, k_nat, v_nat, yn, nconv = _inproj(
            hp, hist_p, *mix, perm, tm=PROJ_TM, shift=1, tiles_per_seq=seq // PROJ_TM,
            keep_tiles=keep // PROJ_TM)
        os, lses = [], []
        for bias, (_, d) in zip(band, PATTERNS):
            sub = min(ATTN_SUB, seq // (Q_BLOCK * d))
            o, lse = _attn_pattern(q, k, v, bias, batch=batch, seq=seq, dil=d, sub=sub,
                                   res=min(d, ATTN_SUB // sub))
            os.append(o)
            lses.append(lse)
        hp = _outproj(os, lses, yn, hp, row(g_out_att[l]), wa_bf, wc_bf, expand, unperm, tm=OUT_TM)
        new["kp"].append(jnp.transpose(k_nat, (0, 3, 1, 2)))
        new["vp"].append(jnp.transpose(v_nat, (0, 3, 1, 2)))
        new["cp"].append(nconv[:, SUBLANES - 2:])

        hist_s = jnp.swapaxes(state_conv[l], 0, 1).reshape(1, 2 * dec_b, CONV_DIM)
        q, k, v, yn, nconv = _inproj(hs, hist_s, *mix, tm=n_s, shift=dec_b, tiles_per_seq=1)
        bmaj = lambda a: jnp.swapaxes(a.reshape(dec_t, dec_b, N_HEADS, HEAD_DIM), 0, 1)
        qb, kb, vb = bmaj(q), bmaj(k), bmaj(v)
        rows8 = lambda a: jnp.pad(a.reshape(dec_b, dec_t, ATT_DIM), ((0, 0), (0, SUBLANES - dec_t), (0, 0)))
        att = _attn_sample(qb.reshape(dec_b, dec_t, ATT_DIM), rows8(kb), rows8(vb), cache_kt, cache_vt, l,
                           s_bias, s_mult)
        att_tm = jnp.swapaxes(att, 0, 1).reshape(n_s, ATT_DIM)
        hs = _outproj([att_tm], [], yn, hs, row(g_out_att[l]), wa_bf, wc_bf, tm=n_s)
        hp, hs = _moe_ple(hp, hs, pp_all, ps_all, *moe, tm_a=PROMPT_TM, tm_b=n_s, te=PROMPT_TE, layer=l)
        new["ks"].append(kb)
        new["vs"].append(vb)
        new["cs"].append(jnp.swapaxes(nconv.reshape(2, dec_b, CONV_DIM), 0, 1))

    y_prompt = hp.reshape(batch, seq, D_MODEL)
    y_sample = jnp.swapaxes(hs.reshape(dec_t, dec_b, D_MODEL), 0, 1)
    st = lambda key: jnp.stack(new[key])
    return (y_prompt, y_sample, st("kp"), st("vp"), st("cp"), st("ks"), st("vs"), st("cs"))
```

```python
import functools

import jax
import jax.numpy as jnp
import numpy as np
from jax import lax
from jax.experimental import pallas as pl
from jax.experimental.pallas import tpu as pltpu

F32 = jnp.float32
BF16 = jnp.bfloat16
HIGHEST = lax.Precision.HIGHEST

D_MODEL = 1024
HEAD_DIM = 64
N_HEADS = 8
ATT_DIM = N_HEADS * HEAD_DIM
CONV_DIM = D_MODEL - ATT_DIM
MIX_IN = 3 * ATT_DIM + 3 * CONV_DIM
PATTERNS = ((128, 1), (512, 4), (2048, 16))
N_KEYS = 128
Q_BLOCK = 128
N_BUCKETS = 32
MAX_DISTANCE = 2048
N_GROUPS = 4
EXPERTS_PER_GROUP = 8
N_EXPERTS = N_GROUPS * EXPERTS_PER_GROUP
D_EXPERT = 256
D_PLE = 256
EPS = 1e-6
NEG = -1e30

LANES = 128
SUBLANES = 8
SLABS = Q_BLOCK // SUBLANES
LSE_LANES_PER_HEAD = LANES // N_HEADS
VMEM_LIMIT = 56 * 1024 * 1024
NT = (((1,), (1,)), ((), ()))


def _cparams(n_axes):
    return pltpu.CompilerParams(dimension_semantics=("arbitrary",) * n_axes,
                                vmem_limit_bytes=VMEM_LIMIT)


def _full(shape):
    n = len(shape)
    return pl.BlockSpec(shape, lambda *_: (0,) * n)


def _rms(x, gain):
    ms = jnp.mean(x * x, axis=-1, keepdims=True)
    return x * lax.rsqrt(ms + EPS) * gain


def _inproj_kernel(*refs, shift, tiles_per_seq, permute):
    (h_ref, gmix_ref, w_ref, qg_ref, kg_ref, cw_ref, gconv_ref, hist_ref) = refs[:8]
    if permute:
        perm_ref, q_ref, k_ref, v_ref, kn_ref, vn_ref, yn_ref, nconv_ref, carry_ref = refs[8:]
    else:
        q_ref, k_ref, v_ref, yn_ref, nconv_ref, carry_ref = refs[8:]
    i = pl.program_id(0)
    a = _rms(h_ref[...], gmix_ref[...])
    proj = jnp.dot(a.astype(BF16), w_ref[...], preferred_element_type=F32)
    tm = proj.shape[0]

    lower = lax.broadcasted_iota(jnp.int32, (tm, LANES), 1) < HEAD_DIM

    def head_norm(t, g):
        out = []
        for j in range(ATT_DIM // LANES):
            blk = t[:, j * LANES:(j + 1) * LANES]
            sq = blk * blk
            ms_lo = jnp.sum(jnp.where(lower, sq, 0.0), axis=-1, keepdims=True) * (1.0 / HEAD_DIM)
            ms_hi = jnp.sum(jnp.where(lower, 0.0, sq), axis=-1, keepdims=True) * (1.0 / HEAD_DIM)
            scale = jnp.where(lower, lax.rsqrt(ms_lo + EPS), lax.rsqrt(ms_hi + EPS))
            out.append(blk * scale)
        return jnp.concatenate(out, axis=-1) * g

    q = head_norm(proj[:, 0:ATT_DIM], qg_ref[...])
    k = head_norm(proj[:, ATT_DIM:2 * ATT_DIM], kg_ref[...])
    v = proj[:, 2 * ATT_DIM:3 * ATT_DIM]
    if permute:
        kn_ref[...] = k.T.reshape(N_HEADS, HEAD_DIM, tm)
        vn_ref[...] = v.T.reshape(N_HEADS, HEAD_DIM, tm)
        qkv = jnp.concatenate([q * (HEAD_DIM ** -0.5), k, v], axis=-1).astype(BF16)
        pm = perm_ref.shape[0]
        moved = jnp.concatenate([jnp.dot(perm_ref[...], qkv[r:r + pm], preferred_element_type=F32)
                                 for r in range(0, tm, pm)], axis=0)
        q_ref[...] = _pack_bf16(moved[:, 0:ATT_DIM], is_bf16=True)
        k_ref[...] = _pack_bf16(moved[:, ATT_DIM:2 * ATT_DIM], is_bf16=True)
        v_ref[...] = _pack_bf16(moved[:, 2 * ATT_DIM:3 * ATT_DIM], is_bf16=True)
    else:
        q_ref[...] = q
        k_ref[...] = k
        v_ref[...] = v
    c0 = 3 * ATT_DIM
    hc = proj[:, c0:c0 + CONV_DIM]
    gb = proj[:, c0 + CONV_DIM:c0 + 2 * CONV_DIM]
    gc = proj[:, c0 + 2 * CONV_DIM:c0 + 3 * CONV_DIM]
    u = gc * hc

    if shift == 1:
        @pl.when(i % tiles_per_seq == 0)
        def _():
            carry_ref[...] = hist_ref[0]
        h0 = carry_ref[SUBLANES - 2:SUBLANES - 1, :]
        h1 = carry_ref[SUBLANES - 1:SUBLANES, :]
        row = lax.broadcasted_iota(jnp.int32, (tm, 1), 0)
        u1 = jnp.where(row == 0, h1, pltpu.roll(u, 1, 0))
        u2 = jnp.where(row == 0, h0, jnp.where(row == 1, h1, pltpu.roll(u, 2, 0)))
        carry_ref[...] = u[tm - SUBLANES:tm, :]
        nconv_ref[0] = u[tm - SUBLANES:tm, :]
    else:
        hist = hist_ref[0]
        u1 = jnp.concatenate([hist[shift:2 * shift], u[0:tm - shift]], axis=0)
        u2 = jnp.concatenate([hist, u[0:tm - 2 * shift]], axis=0)
        nconv_ref[0] = u[tm - 2 * shift:tm, :]
    conv = cw_ref[0:1, :] * u2 + cw_ref[1:2, :] * u1 + cw_ref[2:3, :] * u
    yn_ref[...] = _rms(gb * conv, gconv_ref[...])


def _inproj(h, hist, g_mix, w_in_bf, q_gain, k_gain, conv_w, g_out_conv, perm=None, *,
            tm, shift, tiles_per_seq, keep_tiles=0):
    n = h.shape[0]
    hist_rows = hist.shape[1]
    nseq = hist.shape[0]
    tok = lambda w: pl.BlockSpec((tm, w), lambda i: (i, 0))
    seq3 = lambda r: pl.BlockSpec((1, r, CONV_DIM), lambda i: (i // tiles_per_seq, 0, 0))
    nconv_rows = SUBLANES if shift == 1 else 2 * shift
    packed = perm is not None
    att_w = ATT_DIM // 2 if packed else ATT_DIM
    att = jax.ShapeDtypeStruct((n, att_w), jnp.uint32 if packed else F32)
    in_specs = [tok(D_MODEL), _full((1, D_MODEL)), _full((D_MODEL, MIX_IN)), _full((1, ATT_DIM)),
                _full((1, ATT_DIM)), _full((3, CONV_DIM)), _full((1, CONV_DIM)), seq3(hist_rows)]
    args = [h, g_mix, w_in_bf, q_gain, k_gain, conv_w, g_out_conv, hist]
    out_specs = [tok(att_w)] * 3
    out_shape = [att] * 3
    if perm is not None:
        in_specs.append(_full(perm.shape))
        args.append(perm)
        first = tiles_per_seq - keep_tiles
        kept = pl.BlockSpec((None, N_HEADS, HEAD_DIM, tm),
                            lambda i: (i // tiles_per_seq, 0, 0, jnp.maximum(i % tiles_per_seq - first, 0)))
        out_specs += [kept, kept]
        out_shape += [jax.ShapeDtypeStruct((nseq, N_HEADS, HEAD_DIM, keep_tiles * tm), F32)] * 2
    out_specs += [tok(CONV_DIM), seq3(nconv_rows)]
    out_shape += [jax.ShapeDtypeStruct((n, CONV_DIM), F32),
                  jax.ShapeDtypeStruct((nseq, nconv_rows, CONV_DIM), F32)]
    return pl.pallas_call(
        functools.partial(_inproj_kernel, shift=shift, tiles_per_seq=tiles_per_seq, permute=perm is not None),
        grid=(n // tm,),
        in_specs=in_specs, out_specs=out_specs, out_shape=out_shape,
        scratch_shapes=[pltpu.VMEM((SUBLANES, CONV_DIM), F32)],
        compiler_params=_cparams(1),
        name="inproj",
    )(*args)


def _attn_block(q, kk, vv, bias):
    lane = lax.broadcasted_iota(jnp.int32, (Q_BLOCK, LANES), 1)
    upper = lane >= HEAD_DIM
    scores = []
    for h in range(N_HEADS):
        j, e = divmod(h, 2)
        qp = q[:, j * LANES:(j + 1) * LANES]
        qm = (jnp.where(upper, qp, 0.0) if e else jnp.where(upper, 0.0, qp)).astype(BF16)
        scores.append(lax.dot_general(qm, kk[:, j * LANES:(j + 1) * LANES], NT, preferred_element_type=F32))
    s = jnp.concatenate(scores, axis=0) + bias
    m = jnp.max(s, axis=-1, keepdims=True)
    p = jnp.exp(s - m)
    den = jnp.sum(p, axis=-1, keepdims=True)
    pb = p.astype(BF16)
    head_rows = lambda t, h: t[h * Q_BLOCK:(h + 1) * Q_BLOCK]
    lse_grp = lane // LSE_LANES_PER_HEAD
    m_tile = jnp.zeros((Q_BLOCK, LANES), F32)
    den_tile = jnp.ones((Q_BLOCK, LANES), F32)
    outs = []
    for j in range(N_HEADS // 2):
        even, odd = 2 * j, 2 * j + 1
        vp = vv[:, j * LANES:(j + 1) * LANES]
        pair = jnp.where(upper, jnp.dot(head_rows(pb, odd), vp, preferred_element_type=F32),
                         jnp.dot(head_rows(pb, even), vp, preferred_element_type=F32))
        outs.append(pair / jnp.where(upper, head_rows(den, odd), head_rows(den, even)))
        for h in (even, odd):
            m_tile = jnp.where(lse_grp == h, head_rows(m, h), m_tile)
            den_tile = jnp.where(lse_grp == h, head_rows(den, h), den_tile)
    return jnp.concatenate(outs, axis=-1), m_tile + jnp.log(den_tile)


def _attn_kernel(q_ref, kp_ref, kc_ref, vp_ref, vc_ref, bias_ref, o_ref, lse_ref, kbuf, vbuf, *, sub, res):
    n = pl.program_id(2)
    rows = sub * Q_BLOCK
    half = ATT_DIM // 2
    for r in range(res):
        kbuf[0:Q_BLOCK, :] = _unpack_bf16(kp_ref[:, r].reshape(Q_BLOCK, half))
        kbuf[Q_BLOCK:, :] = _unpack_bf16(kc_ref[:, :, r].reshape(rows, half))
        vbuf[0:Q_BLOCK, :] = _unpack_bf16(vp_ref[:, r].reshape(Q_BLOCK, half))
        vbuf[Q_BLOCK:, :] = _unpack_bf16(vc_ref[:, :, r].reshape(rows, half))
        for j in range(sub):
            q = _unpack_bf16(q_ref[j, :, r].reshape(Q_BLOCK, half), F32)
            r0 = j * Q_BLOCK
            first = (n == 0).astype(jnp.int32) if j == 0 else 0
            o, lse = _attn_block(q, kbuf[r0:r0 + 2 * Q_BLOCK, :], vbuf[r0:r0 + 2 * Q_BLOCK, :], bias_ref[first])
            o_ref[j, :, r] = o.reshape(SLABS, SUBLANES, ATT_DIM)
            lse_ref[j, :, r] = lse.reshape(SLABS, SUBLANES, LANES)


def _attn_pattern(q, k, v, bias, *, batch, seq, dil, sub, res):
    nblk = seq // (Q_BLOCK * dil)
    view = lambda t: t.reshape(batch, nblk, SLABS, dil, SUBLANES, t.shape[-1])
    cur = lambda c: pl.BlockSpec((None, sub, SLABS, res, SUBLANES, c), lambda b, r, n: (b, n, 0, r, 0, 0))
    half = ATT_DIM // 2
    prev = pl.BlockSpec((None, None, SLABS, res, SUBLANES, half),
                        lambda b, r, n: (b, jnp.maximum(n * sub - 1, 0), 0, r, 0, 0))
    o, lse = pl.pallas_call(
        functools.partial(_attn_kernel, sub=sub, res=res),
        grid=(batch, dil // res, nblk // sub),
        in_specs=[cur(half), prev, cur(half), prev, cur(half),
                  _full((2, N_HEADS * Q_BLOCK, 2 * Q_BLOCK))],
        out_specs=[cur(ATT_DIM), cur(LANES)],
        out_shape=[jax.ShapeDtypeStruct((batch, nblk, SLABS, dil, SUBLANES, ATT_DIM), F32),
                   jax.ShapeDtypeStruct((batch, nblk, SLABS, dil, SUBLANES, LANES), F32)],
        scratch_shapes=[pltpu.VMEM(((sub + 1) * Q_BLOCK, ATT_DIM), BF16),
                        pltpu.VMEM(((sub + 1) * Q_BLOCK, ATT_DIM), BF16)],
        compiler_params=_cparams(3),
        name=f"attn_d{dil}",
    )(view(q), view(k), view(k), view(v), view(v), bias)
    return o.reshape(batch * seq, ATT_DIM), lse.reshape(batch * seq, LANES)


SAMPLE_T = 4
NEW_COLS = LANES


def _attn_sample_kernel(q_ref, kt_ref, kn_ref, vt_ref, vn_ref, bias_ref, mult_ref, o_ref):
    rows = SAMPLE_T * N_HEADS
    q4 = q_ref[...] * (HEAD_DIM ** -0.5)
    qt = jnp.concatenate([jnp.broadcast_to(q4[t:t + 1, :], (N_HEADS, ATT_DIM)) for t in range(SAMPLE_T)], axis=0)
    lane_head = lax.broadcasted_iota(jnp.int32, (rows, ATT_DIM), 1) // HEAD_DIM
    row_head = lax.broadcasted_iota(jnp.int32, (rows, ATT_DIM), 0) % N_HEADS
    own = lane_head == row_head
    qbd = jnp.where(own, qt, 0.0).astype(BF16)
    flat = lambda ref: ref[...].reshape(ATT_DIM, ref.shape[-1]).astype(BF16)
    pad = jnp.zeros((NEW_COLS - SUBLANES, ATT_DIM), F32)
    new_rows = lambda ref: jnp.concatenate([ref[...], pad], axis=0).astype(BF16)
    s = jnp.concatenate([jnp.dot(qbd, flat(kt_ref), preferred_element_type=F32),
                         lax.dot_general(qbd, new_rows(kn_ref), NT, preferred_element_type=F32)],
                        axis=-1) + bias_ref[...]
    m = jnp.max(s, axis=-1, keepdims=True)
    p = jnp.exp(s - m) * mult_ref[...]
    den = jnp.sum(p, axis=-1, keepdims=True)
    pb = p.astype(BF16)
    w_buf = kt_ref.shape[-1]
    acc = (lax.dot_general(pb[:, :w_buf], flat(vt_ref), NT, preferred_element_type=F32)
           + jnp.dot(pb[:, w_buf:], new_rows(vn_ref), preferred_element_type=F32))
    acc = jnp.where(own, acc / den, 0.0)
    for t in range(SAMPLE_T):
        o_ref[t:t + 1, :] = jnp.sum(acc[t * N_HEADS:(t + 1) * N_HEADS, :], axis=0, keepdims=True)


def _attn_sample(q, k_new, v_new, cache_kt, cache_vt, layer, bias, mult):
    nb = q.shape[0]
    w_buf = cache_kt.shape[-1]
    tok = pl.BlockSpec((None, SAMPLE_T, ATT_DIM), lambda b: (b, 0, 0))
    new = pl.BlockSpec((None, SUBLANES, ATT_DIM), lambda b: (b, 0, 0))
    old = pl.BlockSpec((None, None, N_HEADS, HEAD_DIM, w_buf), lambda b: (layer, b, 0, 0, 0))
    tbl = _full((SAMPLE_T * N_HEADS, w_buf + NEW_COLS))
    return pl.pallas_call(
        _attn_sample_kernel,
        grid=(nb,),
        in_specs=[tok, old, new, old, new, tbl, tbl],
        out_specs=tok,
        out_shape=jax.ShapeDtypeStruct((nb, SAMPLE_T, ATT_DIM), F32),
        compiler_params=_cparams(1),
        name="attn_sample",
    )(q, cache_kt, k_new, cache_vt, v_new, bias, mult)


def _split_dot(x, e_ref):
    hi = x.astype(BF16)
    lo = (x - hi.astype(F32)).astype(BF16)
    return (jnp.dot(hi, e_ref[...], preferred_element_type=F32)
            + jnp.dot(lo, e_ref[...], preferred_element_type=F32))


def _outproj_kernel(*refs, n_pat):
    mix = n_pat > 1
    n_lse = n_pat if mix else 0
    o_refs = refs[0:n_pat]
    l_refs = refs[n_pat:n_pat + n_lse]
    rest = refs[n_pat + n_lse:]
    if mix:
        yn_ref, h_ref, gatt_ref, exp_ref, unperm_ref, wa_ref, wc_ref, out_ref = rest
        lses = [r[...] for r in l_refs]
        top = functools.reduce(jnp.maximum, lses)
        ws = [jnp.exp(l - top) for l in lses]
        tot = functools.reduce(lambda a, b: a + b, ws)
        att = None
        for w, o_ref in zip(ws, o_refs):
            term = _split_dot(w / tot, exp_ref) * o_ref[...]
            att = term if att is None else att + term
        att_bf = jnp.dot(unperm_ref[...], _rms(att, gatt_ref[...]).astype(BF16),
                         preferred_element_type=F32).astype(BF16)
    else:
        yn_ref, h_ref, gatt_ref, wa_ref, wc_ref, out_ref = rest
        att_bf = _rms(o_refs[0][...], gatt_ref[...]).astype(BF16)
    y = (jnp.dot(att_bf, wa_ref[...], preferred_element_type=F32)
         + jnp.dot(yn_ref[...].astype(BF16), wc_ref[...], preferred_element_type=F32))
    out_ref[...] = h_ref[...] + y


def _outproj(os, lses, yn, h, g_att, wa_bf, wc_bf, expand=None, unperm=None, *, tm):
    n = h.shape[0]
    n_pat = len(os)
    tok = lambda w: pl.BlockSpec((tm, w), lambda i: (i, 0))
    in_specs = [tok(ATT_DIM)] * n_pat + [tok(LANES)] * len(lses) + [tok(CONV_DIM), tok(D_MODEL), _full((1, ATT_DIM))]
    args = [*os, *lses, yn, h, g_att]
    if n_pat > 1:
        in_specs += [_full((LANES, ATT_DIM)), _full((tm, tm))]
        args += [expand, unperm]
    in_specs += [_full((ATT_DIM, D_MODEL)), _full((CONV_DIM, D_MODEL))]
    args += [wa_bf, wc_bf]
    return pl.pallas_call(
        functools.partial(_outproj_kernel, n_pat=n_pat),
        grid=(n // tm,),
        in_specs=in_specs,
        out_specs=tok(D_MODEL),
        out_shape=jax.ShapeDtypeStruct((n, D_MODEL), F32),
        compiler_params=_cparams(1),
        name="outproj",
    )(*args)


GROUP_LANE0 = N_EXPERTS
REC_E1, REC_E2, REC_S1, REC_S2, REC_W1, REC_W2 = range(6)
PACKED = D_MODEL // 2
ROW_W = PACKED + LANES
N_DMA_PRIORITIES = 2
LIST_LANES = LANES
LIST_COUNT = LIST_LANES - 1
LIST_NEXT = LIST_LANES - 2
LIST_PARITY = LIST_LANES - 3


def _slots(tm):
    need = 2 * tm + (SUBLANES - 1) * N_EXPERTS + SUBLANES
    return -(-need // LANES) * LANES


def _pieces(x):
    hi = x.astype(BF16)
    r1 = x - hi.astype(F32)
    mid = r1.astype(BF16)
    return hi, mid, (r1 - mid.astype(F32)).astype(BF16)


def _sort_kernel(h_ref, g_ref, wr_ref, before_ref, sel_ref, xs_ref, rec_ref, cnt_ref, *, group):
    tm = h_ref.shape[0] // group
    slots = xs_ref.shape[0] // group
    for t in range(group):
        xs, rec, cnt = _sort_tile(h_ref[t * tm:(t + 1) * tm, :], g_ref, wr_ref, before_ref, sel_ref, slots)
        xs_ref[t * slots:(t + 1) * slots, :] = xs
        rec_ref[t * tm:(t + 1) * tm, :] = rec
        cnt_ref[t] = cnt


def _sort_tile(h, g_ref, wr_ref, before_ref, sel_ref, slots):
    m = _rms(h, g_ref[...])
    logits = jnp.dot(m.astype(BF16), wr_ref[...], preferred_element_type=F32)
    tm = logits.shape[0]
    lane_i = lax.broadcasted_iota(jnp.int32, (tm, LANES), 1)
    lane = lane_i.astype(F32)
    big = jnp.float32(4 * LANES)

    is_g = jnp.logical_and(lane_i >= GROUP_LANE0, lane_i < GROUP_LANE0 + N_GROUPS)
    gl = jnp.where(is_g, logits, NEG)
    gmax = jnp.max(gl, axis=-1, keepdims=True)
    g_w = 1.0 / jnp.sum(jnp.where(is_g, jnp.exp(gl - gmax), 0.0), axis=-1, keepdims=True)
    g_sel = jnp.min(jnp.where(gl == gmax, lane - GROUP_LANE0, big), axis=-1, keepdims=True)

    grp_of_lane = (lane_i // EXPERTS_PER_GROUP).astype(F32)
    in_grp = jnp.logical_and(lane_i < N_EXPERTS, grp_of_lane == g_sel)
    el = jnp.where(in_grp, logits, NEG)
    t1 = jnp.max(el, axis=-1, keepdims=True)
    e1 = jnp.min(jnp.where(el == t1, lane, big), axis=-1, keepdims=True)
    el2 = jnp.where(lane == e1, NEG, el)
    t2 = jnp.max(el2, axis=-1, keepdims=True)
    e2 = jnp.min(jnp.where(el2 == t2, lane, big), axis=-1, keepdims=True)
    ex = jnp.exp(t2 - t1)
    w1 = g_w / (1.0 + ex)
    w2 = g_w * ex / (1.0 + ex)

    hit1 = lane == e1
    hit2 = lane == e2
    c = jnp.where(hit1, 1.0, jnp.where(hit2, 1.0, 0.0))
    rr = lax.broadcasted_iota(jnp.int32, (tm, tm), 0)
    cc = lax.broadcasted_iota(jnp.int32, (tm, tm), 1)
    lower = jnp.where(rr > cc, 1.0, 0.0).astype(BF16)
    rank = jnp.dot(lower, c.astype(BF16), preferred_element_type=F32)
    cnt = jnp.sum(c, axis=0, keepdims=True)
    chunks = jnp.floor((cnt + (SUBLANES - 1)) * (1.0 / SUBLANES))
    start = SUBLANES * jnp.dot(jnp.broadcast_to(chunks, (SUBLANES, LANES)).astype(BF16), before_ref[...],
                               preferred_element_type=F32)[0:1, :]
    slot_of = rank + start
    s1 = jnp.sum(jnp.where(hit1, slot_of, 0.0), axis=-1, keepdims=True)
    s2 = jnp.sum(jnp.where(hit2, slot_of, 0.0), axis=-1, keepdims=True)

    rec = jnp.zeros((tm, LANES), F32)
    for idx, val in ((REC_E1, e1), (REC_E2, e2), (REC_S1, s1), (REC_S2, s2), (REC_W1, w1), (REC_W2, w2)):
        rec = jnp.where(lane_i == idx, val, rec)

    rec_parts = _pieces(rec)[:2]
    srow = sum(lax.dot_general(sel_ref[...], part, NT, preferred_element_type=F32) for part in rec_parts)
    slot_id = lax.broadcasted_iota(jnp.int32, (slots, tm), 0).astype(F32)
    place = jnp.where(slot_id == srow[0:1, :], 1.0, jnp.where(slot_id == srow[1:2, :], 1.0, 0.0)).astype(BF16)
    payload = jnp.concatenate([m.astype(BF16), *rec_parts], axis=-1)
    moved = jnp.dot(place, payload, preferred_element_type=F32)
    info = moved[:, D_MODEL:D_MODEL + LANES] + moved[:, D_MODEL + LANES:]
    xs = jnp.concatenate([_pack_bf16(moved[:, :D_MODEL], is_bf16=True), pltpu.bitcast(info, jnp.uint32)], axis=-1)
    return xs, rec, jnp.broadcast_to(cnt, (SUBLANES, LANES))


def _sort(h, g_ffn, w_router, before, sel, *, tm):
    n = h.shape[0]
    slots = _slots(tm)
    group = 2 if (n // tm) % 2 == 0 else 1
    tok = lambda w: pl.BlockSpec((group * tm, w), lambda i: (i, 0))
    return pl.pallas_call(
        functools.partial(_sort_kernel, group=group),
        grid=(n // (group * tm),),
        in_specs=[tok(D_MODEL), _full((1, D_MODEL)), _full((D_MODEL, LANES)), _full((LANES, LANES)),
                  _full((SUBLANES, LANES))],
        out_specs=[pl.BlockSpec((group * slots, ROW_W), lambda i: (i, 0)), tok(LANES),
                   pl.BlockSpec((group, SUBLANES, LANES), lambda i: (i, 0, 0))],
        out_shape=[jax.ShapeDtypeStruct((n // tm * slots, ROW_W), jnp.uint32), jax.ShapeDtypeStruct((n, LANES), F32),
                   jax.ShapeDtypeStruct((n // tm, SUBLANES, LANES), F32)],
        compiler_params=_cparams(1),
        name="moe_sort",
    )(h, g_ffn, w_router, before, sel)


def _chunk_copy(src_hbm, row, dst, c, sem):
    if not isinstance(row, int):
        row = pl.multiple_of(row, SUBLANES)
    to = c * SUBLANES
    if not isinstance(to, int):
        to = pl.multiple_of(to, SUBLANES)
    return pltpu.make_async_copy(src_hbm.at[pl.ds(row, SUBLANES), :], dst.at[pl.ds(to, SUBLANES), :], sem)


def _expert_kernel(te_ref, tv_ref, lst_ref, xa_ref, xb_ref, wg_ref, wu_ref, wd_ref, y_ref,
                    xbuf, sem, wg_st, wu_st, wd_st, wsem, wg_bf, wu_bf, wd_bf, *, rows_a, layer):
    i = pl.program_id(0)
    n = pl.num_programs(0)
    te = xbuf.shape[1]
    slot = i % 2

    def fetch(tile, to_slot):
        n_a = lst_ref[tile, LIST_COUNT]

        def from_a(c, carry):
            priority = c % N_DMA_PRIORITIES if isinstance(c, int) else 0
            _chunk_copy(xa_ref, lst_ref[tile, c], xbuf.at[to_slot], c, sem.at[to_slot]).start(priority=priority)
            return carry

        def from_b(c, carry):
            _chunk_copy(xb_ref, lst_ref[tile, c] - rows_a, xbuf.at[to_slot], c, sem.at[to_slot]).start()
            return carry

        per_tile = te // SUBLANES

        @pl.when(n_a == per_tile)
        def _():
            for c in range(per_tile):
                from_a(c, 0)

        @pl.when(n_a != per_tile)
        def _():
            lax.fori_loop(0, n_a, from_a, 0)
            lax.fori_loop(n_a, per_tile, from_b, 0)

    @pl.when(jnp.logical_and(i == 0, tv_ref[0] > 0))
    def _():
        fetch(0, 0)

    nxt = jnp.minimum(i + 1, n - 1)

    @pl.when(jnp.logical_and(i + 1 < n, tv_ref[nxt] > 0))
    def _():
        fetch(nxt, 1 - slot)

    valid = tv_ref[i]
    changed = jnp.logical_or(i == 0, te_ref[i] != te_ref[jnp.maximum(i - 1, 0)])

    def weight_copies(expert, s):
        return [pltpu.make_async_copy(src.at[layer, expert], dst.at[s], wsem.at[s])
                for src, dst in ((wg_ref, wg_st), (wu_ref, wu_st), (wd_ref, wd_st))]

    @pl.when(jnp.logical_and(changed, valid > 0))
    def _():
        s = lst_ref[i, LIST_PARITY]

        @pl.when(i == 0)
        def _():
            for cp in weight_copies(te_ref[0], 0):
                cp.start()

        for cp in weight_copies(te_ref[i], s):
            cp.wait()
        wg_bf[...] = wg_st[s].astype(BF16)
        wu_bf[...] = wu_st[s].astype(BF16)
        wd_bf[...] = wd_st[s].astype(BF16)
        nxt_expert = lst_ref[i, LIST_NEXT]

        @pl.when(nxt_expert >= 0)
        def _():
            for cp in weight_copies(nxt_expert, 1 - s):
                cp.start()

    @pl.when(valid > 0)
    def _():
        pltpu.make_async_copy(xbuf.at[slot], xbuf.at[slot], sem.at[slot]).wait()
        rows = xbuf[slot]
        x = _unpack_bf16(rows[:, :PACKED])
        info = pltpu.bitcast(rows[:, PACKED:], F32)
        mine = info[:, REC_E1:REC_E1 + 1] == te_ref[i].astype(F32)
        gate = jnp.where(mine, info[:, REC_W1:REC_W1 + 1], info[:, REC_W2:REC_W2 + 1])
        hg = jnp.dot(x, wg_bf[...], preferred_element_type=F32)
        hu = jnp.dot(x, wu_bf[...], preferred_element_type=F32)
        hid = (hg * jax.nn.sigmoid(hg)) * hu * gate
        y = jnp.dot(hid.astype(BF16), wd_bf[...], preferred_element_type=F32)
        y_ref[...] = _pack_bf16(y)

    @pl.when(valid <= 0)
    def _():
        y_ref[...] = jnp.zeros_like(y_ref)


def _pack_bf16(x, is_bf16=False):
    w = x.shape[-1] // 2
    bits = lambda a: pltpu.bitcast(a if is_bf16 else a.astype(BF16).astype(F32), jnp.uint32)
    return jnp.bitwise_or(lax.shift_right_logical(bits(x[:, :w]), jnp.uint32(16)),
                          jnp.bitwise_and(bits(x[:, w:]), jnp.uint32(0xFFFF0000)))


def _unpack_bf16(words, dtype=BF16):
    as_f32 = lambda a: pltpu.bitcast(a, F32)
    return jnp.concatenate([as_f32(lax.shift_left(words, jnp.uint32(16))),
                            as_f32(jnp.bitwise_and(words, jnp.uint32(0xFFFF0000)))], axis=-1).astype(dtype)


def _experts(xs_a, xs_b, tile_expert, tile_valid, src, w_gate, w_up, w_down, *, te, layer):
    n_tiles = tile_expert.shape[0]
    hbm = pl.BlockSpec(memory_space=pl.ANY)
    grid_spec = pltpu.PrefetchScalarGridSpec(
        num_scalar_prefetch=3,
        grid=(n_tiles,),
        in_specs=[hbm, hbm, hbm, hbm, hbm],
        out_specs=pl.BlockSpec((te, PACKED), lambda i, e, v, s: (i, 0)),
        scratch_shapes=[pltpu.VMEM((2, te, ROW_W), jnp.uint32), pltpu.SemaphoreType.DMA((2,)),
                        pltpu.VMEM((2, D_MODEL, D_EXPERT), F32), pltpu.VMEM((2, D_MODEL, D_EXPERT), F32),
                        pltpu.VMEM((2, D_EXPERT, D_MODEL), F32), pltpu.SemaphoreType.DMA((2,)),
                        pltpu.VMEM((D_MODEL, D_EXPERT), BF16), pltpu.VMEM((D_MODEL, D_EXPERT), BF16),
                        pltpu.VMEM((D_EXPERT, D_MODEL), BF16)])
    return pl.pallas_call(
        functools.partial(_expert_kernel, rows_a=xs_a.shape[0], layer=layer),
        grid_spec=grid_spec,
        out_shape=jax.ShapeDtypeStruct((n_tiles * te, PACKED), jnp.uint32),
        compiler_params=_cparams(1),
        name="experts",
    )(tile_expert, tile_valid, src.reshape(n_tiles, LIST_LANES), xs_a, xs_b, w_gate, w_up, w_down)


def _combine_kernel(lst_ref, h_ref, rec_ref, ys_ref, p_ref, gple_ref, wgate_ref, wproj_ref, out_ref,
                     ybuf, sem):
    i = pl.program_id(0)
    n = pl.num_programs(0)
    slot = i % 2
    tm = h_ref.shape[0]
    slots = ybuf.shape[1]

    def fetch(tile, to_slot):
        for c in range(slots // SUBLANES):
            _chunk_copy(ys_ref, lst_ref[tile, c], ybuf.at[to_slot], c, sem.at[to_slot]).start(
                priority=c % N_DMA_PRIORITIES)

    @pl.when(i == 0)
    def _():
        fetch(0, 0)

    @pl.when(i + 1 < n)
    def _():
        fetch(i + 1, 1 - slot)

    pltpu.make_async_copy(ybuf.at[slot], ybuf.at[slot], sem.at[slot]).wait()

    rec = rec_ref[...]
    slot_id = lax.broadcasted_iota(jnp.int32, (tm, slots), 1).astype(F32)
    back = jnp.where(slot_id == rec[:, REC_S1:REC_S1 + 1], 1.0,
                     jnp.where(slot_id == rec[:, REC_S2:REC_S2 + 1], 1.0, 0.0)).astype(BF16)
    h2 = h_ref[...] + jnp.dot(back, _unpack_bf16(ybuf[slot]), preferred_element_type=F32)
    gate = jax.nn.sigmoid(jnp.dot(_rms(h2, gple_ref[...]).astype(BF16), wgate_ref[...], preferred_element_type=F32))
    ple = jnp.dot(p_ref[...].astype(BF16), wproj_ref[...], preferred_element_type=F32)
    out_ref[...] = h2 + ple * gate


def _combine(h, rec, dst, ys, p, g_ple, wgate_bf, wproj_bf, *, tm, layer):
    n = h.shape[0]
    n_tok = n // tm
    tok = lambda w: pl.BlockSpec((tm, w), lambda i, s: (i, 0))
    full = lambda shape: pl.BlockSpec(shape, lambda i, s: (0,) * len(shape))
    grid_spec = pltpu.PrefetchScalarGridSpec(
        num_scalar_prefetch=1,
        grid=(n_tok,),
        in_specs=[tok(D_MODEL), tok(LANES), pl.BlockSpec(memory_space=pl.ANY),
                  pl.BlockSpec((None, tm, D_PLE), lambda i, s: (layer, i, 0)),
                  full((1, D_MODEL)), full((D_MODEL, D_MODEL)), full((D_PLE, D_MODEL))],
        out_specs=tok(D_MODEL),
        scratch_shapes=[pltpu.VMEM((2, _slots(tm), PACKED), jnp.uint32), pltpu.SemaphoreType.DMA((2,))])
    return pl.pallas_call(
        _combine_kernel,
        grid_spec=grid_spec,
        out_shape=jax.ShapeDtypeStruct((n, D_MODEL), F32),
        compiler_params=_cparams(1),
        name="combine",
    )(dst.reshape(n_tok, LIST_LANES), h, rec, ys, p, g_ple, wgate_bf, wproj_bf)


def _excl_cumsum(x, axis):
    return jnp.cumsum(x, axis=axis) - x


def _chunk_plan(cnt, tile_row0, *, n_pairs, rows_a, zero_row, te):
    n_tok = cnt.shape[0]
    per_tile = te // SUBLANES
    chunks = (cnt + SUBLANES - 1) // SUBLANES
    run0 = _excl_cumsum(chunks, 1)
    seg0 = _excl_cumsum(chunks, 0)
    total = jnp.sum(chunks, axis=0)
    region = ((total + per_tile - 1) // per_tile) * per_tile
    reg_end = jnp.cumsum(region)
    reg0 = reg_end - region

    n_tiles = -(-(n_pairs + (SUBLANES - 1) * N_EXPERTS * n_tok) // te) + N_EXPERTS
    t0 = jnp.arange(n_tiles, dtype=jnp.int32) * per_tile
    tile_expert = jnp.minimum(jnp.sum((t0[:, None] >= reg_end[None, :]).astype(jnp.int32), axis=1), N_EXPERTS - 1)
    pick = tile_expert[:, None] == jnp.arange(N_EXPERTS, dtype=jnp.int32)[None, :]
    of_tile = lambda v: jnp.sum(jnp.where(pick, v[None, :], 0), axis=1)
    tile_valid = jnp.clip(of_tile(total) - (t0 - of_tile(reg0)), 0, per_tile).astype(jnp.int32)

    q = (t0 - of_tile(reg0))[:, None] + jnp.arange(per_tile, dtype=jnp.int32)[None, :]
    col_of_tile = lambda v: jnp.sum(jnp.where(pick[:, None, :], v[None, :, :], 0), axis=2)
    seg0_t, seg1_t, run0_t = col_of_tile(seg0), col_of_tile(seg0 + chunks), col_of_tile(run0)
    holds = (q[:, :, None] >= seg0_t[:, None, :]) & (q[:, :, None] < seg1_t[:, None, :])
    local = run0_t[:, None, :] + q[:, :, None] - seg0_t[:, None, :]
    row = jnp.asarray(tile_row0, jnp.int32)[None, None, :] + SUBLANES * local
    src = jnp.sum(jnp.where(holds, row, 0), axis=2)
    src = jnp.where(jnp.any(holds, axis=2), src, zero_row)
    n_first = jnp.sum((src < rows_a).astype(jnp.int32), axis=1)
    idx = jnp.arange(N_EXPERTS, dtype=jnp.int32)
    nonempty = total > 0
    ordinal = _excl_cumsum(nonempty.astype(jnp.int32), 0)
    later = jnp.where((idx[None, :] > idx[:, None]) & nonempty[None, :], idx[None, :], N_EXPERTS)
    following = jnp.min(later, axis=1)
    following = jnp.where(following == N_EXPERTS, -1, following)
    lane = jnp.arange(LIST_LANES, dtype=jnp.int32)[None, :]
    src = jnp.pad(src, ((0, 0), (0, LIST_LANES - per_tile)))
    for at, val in ((LIST_COUNT, n_first), (LIST_NEXT, of_tile(following)), (LIST_PARITY, of_tile(ordinal) % 2)):
        src = jnp.where(lane == at, val[:, None], src)
    src = src.reshape(n_tiles, 1, LIST_LANES).astype(jnp.int32)

    j = jnp.arange(LIST_LANES, dtype=jnp.int32)[None, :, None]
    inside = (j >= run0[:, None, :]) & (j < (run0 + chunks)[:, None, :])
    base = (reg0[None, :] + seg0 - run0)[:, None, :]
    dst = SUBLANES * jnp.sum(jnp.where(inside, base + j, 0), axis=2)
    dst = dst.reshape(n_tok, 1, LIST_LANES).astype(jnp.int32)
    return tile_expert.astype(jnp.int32), tile_valid, src, dst


def _moe_ple(h_a, h_b, p_a, p_b, g_ffn, w_router, before, sel, w_gate, w_up, w_down, g_ple, wgate_bf, wproj_bf,
              *, tm_a, tm_b, te, layer):
    xs_a, rec_a, cnt_a = _sort(h_a, g_ffn, w_router, before, sel, tm=tm_a)
    xs_b, rec_b, cnt_b = _sort(h_b, g_ffn, w_router, before, sel, tm=tm_b)
    t_a, t_b = cnt_a.shape[0], cnt_b.shape[0]
    cnt = jnp.concatenate([cnt_a[:, 0, :N_EXPERTS], cnt_b[:, 0, :N_EXPERTS]], axis=0).astype(jnp.int32)
    rows_a = xs_a.shape[0]
    row0 = np.concatenate([np.arange(t_a) * _slots(tm_a), rows_a + np.arange(t_b) * _slots(tm_b)])
    tile_expert, tile_valid, src, dst = _chunk_plan(
        cnt, row0, n_pairs=2 * (h_a.shape[0] + h_b.shape[0]), rows_a=rows_a,
        zero_row=rows_a + _slots(tm_b) - SUBLANES, te=te)
    ys = _experts(xs_a, xs_b, tile_expert, tile_valid, src, w_gate, w_up, w_down, te=te, layer=layer)
    out_a = _combine(h_a, rec_a, dst[:t_a], ys, p_a, g_ple, wgate_bf, wproj_bf, tm=tm_a, layer=layer)
    out_b = _combine(h_b, rec_b, dst[t_a:], ys, p_b, g_ple, wgate_bf, wproj_bf, tm=tm_b, layer=layer)
    return out_a, out_b


def _bucket_np(dist):
    max_exact = N_BUCKETS // 2
    d_f = np.maximum(dist, 1).astype(np.float32)
    large = max_exact + (np.log(d_f / np.float32(max_exact)) / np.float32(np.log(MAX_DISTANCE / max_exact))
                         * np.float32(N_BUCKETS - max_exact)).astype(np.int32)
    large = np.minimum(large, N_BUCKETS - 1)
    return np.where(dist < max_exact, dist, large).astype(np.int32)


def _bias_from_buckets(rel_bias, bucket, valid):
    onehot = (jnp.asarray(bucket)[..., None] == jnp.arange(N_BUCKETS, dtype=jnp.int32)).astype(F32)
    bias = jnp.einsum("...k,kh->h...", onehot, rel_bias.astype(F32), precision=HIGHEST)
    return jnp.where(jnp.asarray(valid)[None], bias, NEG)


def _block_order(dil):
    g = np.arange(Q_BLOCK) // SUBLANES
    j = np.arange(Q_BLOCK) % SUBLANES
    if dil == 1:
        return 16 * j + g
    if dil == 4:
        return 32 * (g // 4) + 4 * j + g % 4
    return SUBLANES * g + j


def _band_bias(rel_bias, dil):
    mu = _block_order(dil)
    qi = mu[:, None] + Q_BLOCK
    ki = np.concatenate([mu, mu + Q_BLOCK])[None, :]
    off = qi - ki
    valid = (off >= 0) & (off <= N_KEYS)
    bucket = _bucket_np(dil * np.clip(off, 0, N_KEYS))
    first = valid & (np.arange(2 * Q_BLOCK)[None, :] >= Q_BLOCK)
    tables = [_bias_from_buckets(rel_bias, bucket, v).reshape(N_HEADS * Q_BLOCK, 2 * Q_BLOCK) for v in (valid, first)]
    return jnp.stack(tables)


def _sample_tables(rel_bias, w_buf):
    qpos = w_buf + np.arange(SAMPLE_T)[:, None]
    pos = np.arange(w_buf + NEW_COLS)[None, :]
    dist = qpos - pos
    in_seq = (dist >= 0) & (pos < w_buf + SAMPLE_T)
    mult = np.zeros(dist.shape, np.float32)
    for (w, d) in PATTERNS:
        mult += in_seq & (dist % d == 0) & (dist <= w)
    bucket = _bucket_np(np.maximum(dist, 0))
    bias = jnp.transpose(_bias_from_buckets(rel_bias, bucket, mult > 0), (1, 0, 2))
    rows = SAMPLE_T * N_HEADS
    mult_rows = np.broadcast_to(mult[:, None, :], (SAMPLE_T, N_HEADS, mult.shape[-1]))
    return bias.reshape(rows, -1), jnp.asarray(mult_rows.reshape(rows, -1))


PROJ_TM = 512
PERM_ROWS = 256
OUT_TM = 512
PROMPT_TM = 256
PROMPT_TE = 512
ATTN_SUB = 8


def _row_perm(tm):
    a = np.arange(tm)
    src = (a // Q_BLOCK) * Q_BLOCK + 16 * (a % SUBLANES) + (a % Q_BLOCK) // SUBLANES
    perm = np.zeros((tm, tm), np.float32)
    perm[a, src] = 1.0
    return perm


def kernel(x_prompt, x_sample, cache_k, cache_v, state_conv, p_prompt, p_sample, rel_bias, g_mix, w_in, q_gain,
           k_gain, conv_w, g_out_att, g_out_conv, w_out, g_ffn, w_router_group, w_router_expert, w_gate, w_up,
           w_down, g_ple, w_ple_gate, w_ple_proj):
    depth = w_in.shape[0]
    batch, seq, _ = x_prompt.shape
    dec_b, dec_t, _ = x_sample.shape
    w_buf = cache_k.shape[2]
    n_s = dec_b * dec_t
    keep = min(w_buf, seq)
    assert dec_t == SAMPLE_T and w_buf % LANES == 0
    assert seq % (Q_BLOCK * 16 * 2) == 0 and keep % PROJ_TM == 0 and seq % PROMPT_TM == 0
    cache_kt = jnp.transpose(cache_k, (0, 1, 3, 4, 2))
    cache_vt = jnp.transpose(cache_v, (0, 1, 3, 4, 2))

    row = lambda a: a.reshape(1, -1)
    src_lane = np.arange(LANES)
    expand = jnp.asarray((src_lane[:, None] // LSE_LANES_PER_HEAD == np.arange(ATT_DIM)[None, :] // HEAD_DIM)
                         & (src_lane[:, None] % LSE_LANES_PER_HEAD == 0), BF16)
    before = jnp.asarray(np.arange(LANES)[:, None] < np.arange(LANES)[None, :], BF16)
    sel_np = np.zeros((SUBLANES, LANES), np.float32)
    sel_np[0, REC_S1] = sel_np[1, REC_S2] = 1.0
    sel = jnp.asarray(sel_np, BF16)
    perm = jnp.asarray(_row_perm(PERM_ROWS), BF16)
    unperm = jnp.asarray(_row_perm(OUT_TM).T, BF16)
    band = [_band_bias(rel_bias, d) for (_, d) in PATTERNS]
    s_bias, s_mult = _sample_tables(rel_bias, w_buf)

    hp = x_prompt.reshape(batch * seq, D_MODEL)
    hs = jnp.swapaxes(x_sample, 0, 1).reshape(n_s, D_MODEL)
    pp_all = p_prompt.reshape(depth, batch * seq, D_PLE)
    ps_all = jnp.swapaxes(p_sample, 1, 2).reshape(depth, n_s, D_PLE)
    new = {k: [] for k in ("kp", "vp", "cp", "ks", "vs", "cs")}
    hist_p = jnp.zeros((batch, SUBLANES, CONV_DIM), F32)

    for l in range(depth):
        w_in_bf = w_in[l].astype(BF16)
        wa_bf = w_out[l, :ATT_DIM].astype(BF16)
        wc_bf = w_out[l, ATT_DIM:].astype(BF16)
        wgate_bf = w_ple_gate[l].astype(BF16)
        wproj_bf = w_ple_proj[l].astype(BF16)
        w_router = jnp.concatenate(
            [w_router_expert[l], w_router_group[l],
             jnp.zeros((D_MODEL, LANES - N_EXPERTS - N_GROUPS), F32)], axis=1).astype(BF16)
        qg, kg = row(jnp.tile(q_gain[l], N_HEADS)), row(jnp.tile(k_gain[l], N_HEADS))
        mix = (row(g_mix[l]), w_in_bf, qg, kg, conv_w[l], row(g_out_conv[l]))
        moe = (row(g_ffn[l]), w_router, before, sel, w_gate, w_up, w_down, row(g_ple[l]), wgate_bf, wproj_bf)

        q, k, v, k_nat, v_nat, yn, nconv = _inproj(
            hp, hist_p, *mix, perm, tm=PROJ_TM, shift=1, tiles_per_seq=seq // PROJ_TM,
            keep_tiles=keep // PROJ_TM)
        os, lses = [], []
        for bias, (_, d) in zip(band, PATTERNS):
            sub = min(ATTN_SUB, seq // (Q_BLOCK * d))
            o, lse = _attn_pattern(q, k, v, bias, batch=batch, seq=seq, dil=d, sub=sub,
                                   res=min(d, ATTN_SUB // sub))
            os.append(o)
            lses.append(lse)
        hp = _outproj(os, lses, yn, hp, row(g_out_att[l]), wa_bf, wc_bf, expand, unperm, tm=OUT_TM)
        new["kp"].append(jnp.transpose(k_nat, (0, 3, 1, 2)))
        new["vp"].append(jnp.transpose(v_nat, (0, 3, 1, 2)))
        new["cp"].append(nconv[:, SUBLANES - 2:])

        hist_s = jnp.swapaxes(state_conv[l], 0, 1).reshape(1, 2 * dec_b, CONV_DIM)
        q, k, v, yn, nconv = _inproj(hs, hist_s, *mix, tm=n_s, shift=dec_b, tiles_per_seq=1)
        bmaj = lambda a: jnp.swapaxes(a.reshape(dec_t, dec_b, N_HEADS, HEAD_DIM), 0, 1)
        qb, kb, vb = bmaj(q), bmaj(k), bmaj(v)
        rows8 = lambda a: jnp.pad(a.reshape(dec_b, dec_t, ATT_DIM), ((0, 0), (0, SUBLANES - dec_t), (0, 0)))
        att = _attn_sample(qb.reshape(dec_b, dec_t, ATT_DIM), rows8(kb), rows8(vb), cache_kt, cache_vt, l,
                           s_bias, s_mult)
        att_tm = jnp.swapaxes(att, 0, 1).reshape(n_s, ATT_DIM)
        hs = _outproj([att_tm], [], yn, hs, row(g_out_att[l]), wa_bf, wc_bf, tm=n_s)
        hp, hs = _moe_ple(hp, hs, pp_all, ps_all, *moe, tm_a=PROMPT_TM, tm_b=n_s, te=PROMPT_TE, layer=l)
        new["ks"].append(kb)
        new["vs"].append(vb)
        new["cs"].append(jnp.swapaxes(nconv.reshape(2, dec_b, CONV_DIM), 0, 1))

    y_prompt = hp.reshape(batch, seq, D_MODEL)
    y_sample = jnp.swapaxes(hs.reshape(dec_t, dec_b, D_MODEL), 0, 1)
    st = lambda key: jnp.stack(new[key])
    return (y_prompt, y_sample, st("kp"), st("vp"), st("cp"), st("ks"), st("vs"), st("cs"))
```

```python
import functools

import jax
import jax.numpy as jnp
import numpy as np
from jax import lax
from jax.experimental import pallas as pl
from jax.experimental.pallas import tpu as pltpu

F32 = jnp.float32
BF16 = jnp.bfloat16
HIGHEST = lax.Precision.HIGHEST

D_MODEL = 1024
HEAD_DIM = 64
N_HEADS = 8
ATT_DIM = N_HEADS * HEAD_DIM
CONV_DIM = D_MODEL - ATT_DIM
MIX_IN = 3 * ATT_DIM + 3 * CONV_DIM
PATTERNS = ((128, 1), (512, 4), (2048, 16))
N_KEYS = 128
Q_BLOCK = 128
N_BUCKETS = 32
MAX_DISTANCE = 2048
N_GROUPS = 4
EXPERTS_PER_GROUP = 8
N_EXPERTS = N_GROUPS * EXPERTS_PER_GROUP
D_EXPERT = 256
D_PLE = 256
EPS = 1e-6
NEG = -1e30

LANES = 128
SUBLANES = 8
SLABS = Q_BLOCK // SUBLANES
LSE_LANES_PER_HEAD = LANES // N_HEADS
VMEM_LIMIT = 56 * 1024 * 1024
NT = (((1,), (1,)), ((), ()))


def _cparams(n_axes):
    return pltpu.CompilerParams(dimension_semantics=("arbitrary",) * n_axes,
                                vmem_limit_bytes=VMEM_LIMIT)


def _full(shape):
    n = len(shape)
    return pl.BlockSpec(shape, lambda *_: (0,) * n)


def _rms(x, gain):
    ms = jnp.mean(x * x, axis=-1, keepdims=True)
    return x * lax.rsqrt(ms + EPS) * gain


def _inproj_kernel(*refs, shift, tiles_per_seq, permute, n_prev=0):
    (h_ref, gmix_ref, w_ref, qg_ref, kg_ref, cw_ref, gconv_ref, hist_ref) = refs[:8]
    if permute and n_prev:
        (perm_ref, kprev_ref, vprev_ref, q_ref, k_ref, v_ref, kn_ref, vn_ref, yn_ref, nconv_ref,
         carry_ref) = refs[8:]
    elif permute:
        perm_ref, q_ref, k_ref, v_ref, kn_ref, vn_ref, yn_ref, nconv_ref, carry_ref = refs[8:]
    else:
        q_ref, k_ref, v_ref, yn_ref, nconv_ref, carry_ref = refs[8:]
    i = pl.program_id(0)
    a = _rms(h_ref[...], gmix_ref[...])
    proj = jnp.dot(a.astype(BF16), w_ref[...], preferred_element_type=F32)
    tm = proj.shape[0]

    lower = lax.broadcasted_iota(jnp.int32, (tm, LANES), 1) < HEAD_DIM

    def head_norm(t, g):
        out = []
        for j in range(ATT_DIM // LANES):
            blk = t[:, j * LANES:(j + 1) * LANES]
            sq = blk * blk
            ms_lo = jnp.sum(jnp.where(lower, sq, 0.0), axis=-1, keepdims=True) * (1.0 / HEAD_DIM)
            ms_hi = jnp.sum(jnp.where(lower, 0.0, sq), axis=-1, keepdims=True) * (1.0 / HEAD_DIM)
            scale = jnp.where(lower, lax.rsqrt(ms_lo + EPS), lax.rsqrt(ms_hi + EPS))
            out.append(blk * scale)
        return jnp.concatenate(out, axis=-1) * g

    q = head_norm(proj[:, 0:ATT_DIM], qg_ref[...])
    k = head_norm(proj[:, ATT_DIM:2 * ATT_DIM], kg_ref[...])
    v = proj[:, 2 * ATT_DIM:3 * ATT_DIM]
    if permute:
        if n_prev:
            kn_ref[0:n_prev] = kprev_ref[...]
            vn_ref[0:n_prev] = vprev_ref[...]
        kn_ref[n_prev] = k.T.reshape(N_HEADS, HEAD_DIM, tm)
        vn_ref[n_prev] = v.T.reshape(N_HEADS, HEAD_DIM, tm)
        qkv = jnp.concatenate([q * (HEAD_DIM ** -0.5), k, v], axis=-1).astype(BF16)
        pm = perm_ref.shape[0]
        moved = jnp.concatenate([jnp.dot(perm_ref[...], qkv[r:r + pm], preferred_element_type=F32)
                                 for r in range(0, tm, pm)], axis=0)
        q_ref[...] = _pack_bf16(moved[:, 0:ATT_DIM], is_bf16=True)
        k_ref[...] = _pack_bf16(moved[:, ATT_DIM:2 * ATT_DIM], is_bf16=True)
        v_ref[...] = _pack_bf16(moved[:, 2 * ATT_DIM:3 * ATT_DIM], is_bf16=True)
    else:
        q_ref[...] = q
        k_ref[...] = k
        v_ref[...] = v
    c0 = 3 * ATT_DIM
    hc = proj[:, c0:c0 + CONV_DIM]
    gb = proj[:, c0 + CONV_DIM:c0 + 2 * CONV_DIM]
    gc = proj[:, c0 + 2 * CONV_DIM:c0 + 3 * CONV_DIM]
    u = gc * hc

    if shift == 1:
        @pl.when(i % tiles_per_seq == 0)
        def _():
            carry_ref[...] = hist_ref[0]
        h0 = carry_ref[SUBLANES - 2:SUBLANES - 1, :]
        h1 = carry_ref[SUBLANES - 1:SUBLANES, :]
        row = lax.broadcasted_iota(jnp.int32, (tm, 1), 0)
        u1 = jnp.where(row == 0, h1, pltpu.roll(u, 1, 0))
        u2 = jnp.where(row == 0, h0, jnp.where(row == 1, h1, pltpu.roll(u, 2, 0)))
        carry_ref[...] = u[tm - SUBLANES:tm, :]
        nconv_ref[0] = u[tm - SUBLANES:tm, :]
    else:
        hist = hist_ref[0]
        u1 = jnp.concatenate([hist[shift:2 * shift], u[0:tm - shift]], axis=0)
        u2 = jnp.concatenate([hist, u[0:tm - 2 * shift]], axis=0)
        nconv_ref[0] = u[tm - 2 * shift:tm, :]
    conv = cw_ref[0:1, :] * u2 + cw_ref[1:2, :] * u1 + cw_ref[2:3, :] * u
    yn_ref[...] = _rms(gb * conv, gconv_ref[...])


def _inproj(h, hist, g_mix, w_in_bf, q_gain, k_gain, conv_w, g_out_conv, perm=None, kv_prev=None, *,
            tm, shift, tiles_per_seq, keep_tiles=0):
    n = h.shape[0]
    hist_rows = hist.shape[1]
    nseq = hist.shape[0]
    tok = lambda w: pl.BlockSpec((tm, w), lambda i: (i, 0))
    seq3 = lambda r: pl.BlockSpec((1, r, CONV_DIM), lambda i: (i // tiles_per_seq, 0, 0))
    nconv_rows = SUBLANES if shift == 1 else 2 * shift
    packed = perm is not None
    att_w = ATT_DIM // 2 if packed else ATT_DIM
    att = jax.ShapeDtypeStruct((n, att_w), jnp.uint32 if packed else F32)
    in_specs = [tok(D_MODEL), _full((1, D_MODEL)), _full((D_MODEL, MIX_IN)), _full((1, ATT_DIM)),
                _full((1, ATT_DIM)), _full((3, CONV_DIM)), _full((1, CONV_DIM)), seq3(hist_rows)]
    args = [h, g_mix, w_in_bf, q_gain, k_gain, conv_w, g_out_conv, hist]
    out_specs = [tok(att_w)] * 3
    out_shape = [att] * 3
    if perm is not None:
        in_specs.append(_full(perm.shape))
        args.append(perm)
        first = tiles_per_seq - keep_tiles
        kept = lambda layers: pl.BlockSpec(
            (layers, None, N_HEADS, HEAD_DIM, tm),
            lambda i: (0, i // tiles_per_seq, 0, 0, jnp.maximum(i % tiles_per_seq - first, 0)))
        n_prev = 0 if kv_prev is None else kv_prev[0].shape[0]
        if n_prev:
            in_specs += [kept(n_prev)] * 2
            args += list(kv_prev)
        out_specs += [kept(n_prev + 1)] * 2
        out_shape += [jax.ShapeDtypeStruct((n_prev + 1, nseq, N_HEADS, HEAD_DIM, keep_tiles * tm), F32)] * 2
    out_specs += [tok(CONV_DIM), seq3(nconv_rows)]
    out_shape += [jax.ShapeDtypeStruct((n, CONV_DIM), F32),
                  jax.ShapeDtypeStruct((nseq, nconv_rows, CONV_DIM), F32)]
    n_prev = 0 if kv_prev is None else kv_prev[0].shape[0]
    return pl.pallas_call(
        functools.partial(_inproj_kernel, shift=shift, tiles_per_seq=tiles_per_seq, permute=perm is not None,
                          n_prev=n_prev),
        grid=(n // tm,),
        in_specs=in_specs, out_specs=out_specs, out_shape=out_shape,
        scratch_shapes=[pltpu.VMEM((SUBLANES, CONV_DIM), F32)],
        compiler_params=_cparams(1),
        name="inproj",
    )(*args)


def _attn_block(q, kk, vv, bias):
    lane = lax.broadcasted_iota(jnp.int32, (Q_BLOCK, LANES), 1)
    upper = lane >= HEAD_DIM
    scores = []
    for h in range(N_HEADS):
        j, e = divmod(h, 2)
        qp = q[:, j * LANES:(j + 1) * LANES]
        qm = (jnp.where(upper, qp, 0.0) if e else jnp.where(upper, 0.0, qp)).astype(BF16)
        scores.append(lax.dot_general(qm, kk[:, j * LANES:(j + 1) * LANES], NT, preferred_element_type=F32))
    s = jnp.concatenate(scores, axis=0) + bias
    m = jnp.max(s, axis=-1, keepdims=True)
    p = jnp.exp(s - m)
    den = jnp.sum(p, axis=-1, keepdims=True)
    pb = p.astype(BF16)
    head_rows = lambda t, h: t[h * Q_BLOCK:(h + 1) * Q_BLOCK]
    lse_grp = lane // LSE_LANES_PER_HEAD
    m_tile = jnp.zeros((Q_BLOCK, LANES), F32)
    den_tile = jnp.ones((Q_BLOCK, LANES), F32)
    outs = []
    for j in range(N_HEADS // 2):
        even, odd = 2 * j, 2 * j + 1
        vp = vv[:, j * LANES:(j + 1) * LANES]
        pair = jnp.where(upper, jnp.dot(head_rows(pb, odd), vp, preferred_element_type=F32),
                         jnp.dot(head_rows(pb, even), vp, preferred_element_type=F32))
        outs.append(pair / jnp.where(upper, head_rows(den, odd), head_rows(den, even)))
        for h in (even, odd):
            m_tile = jnp.where(lse_grp == h, head_rows(m, h), m_tile)
            den_tile = jnp.where(lse_grp == h, head_rows(den, h), den_tile)
    return jnp.concatenate(outs, axis=-1), m_tile + jnp.log(den_tile)


def _attn_kernel(q_ref, kp_ref, kc_ref, vp_ref, vc_ref, bias_ref, o_ref, lse_ref, kbuf, vbuf, *, sub, res):
    n = pl.program_id(2)
    rows = sub * Q_BLOCK
    half = ATT_DIM // 2
    for r in range(res):
        kbuf[0:Q_BLOCK, :] = _unpack_bf16(kp_ref[:, r].reshape(Q_BLOCK, half))
        kbuf[Q_BLOCK:, :] = _unpack_bf16(kc_ref[:, :, r].reshape(rows, half))
        vbuf[0:Q_BLOCK, :] = _unpack_bf16(vp_ref[:, r].reshape(Q_BLOCK, half))
        vbuf[Q_BLOCK:, :] = _unpack_bf16(vc_ref[:, :, r].reshape(rows, half))
        for j in range(sub):
            q = _unpack_bf16(q_ref[j, :, r].reshape(Q_BLOCK, half), F32)
            r0 = j * Q_BLOCK
            first = (n == 0).astype(jnp.int32) if j == 0 else 0
            o, lse = _attn_block(q, kbuf[r0:r0 + 2 * Q_BLOCK, :], vbuf[r0:r0 + 2 * Q_BLOCK, :], bias_ref[first])
            o_ref[j, :, r] = o.reshape(SLABS, SUBLANES, ATT_DIM)
            lse_ref[j, :, r] = lse.reshape(SLABS, SUBLANES, LANES)


def _attn_pattern(q, k, v, bias, *, batch, seq, dil, sub, res):
    nblk = seq // (Q_BLOCK * dil)
    view = lambda t: t.reshape(batch, nblk, SLABS, dil, SUBLANES, t.shape[-1])
    cur = lambda c: pl.BlockSpec((None, sub, SLABS, res, SUBLANES, c), lambda b, r, n: (b, n, 0, r, 0, 0))
    half = ATT_DIM // 2
    prev = pl.BlockSpec((None, None, SLABS, res, SUBLANES, half),
                        lambda b, r, n: (b, jnp.maximum(n * sub - 1, 0), 0, r, 0, 0))
    o, lse = pl.pallas_call(
        functools.partial(_attn_kernel, sub=sub, res=res),
        grid=(batch, dil // res, nblk // sub),
        in_specs=[cur(half), prev, cur(half), prev, cur(half),
                  _full((2, N_HEADS * Q_BLOCK, 2 * Q_BLOCK))],
        out_specs=[cur(ATT_DIM), cur(LANES)],
        out_shape=[jax.ShapeDtypeStruct((batch, nblk, SLABS, dil, SUBLANES, ATT_DIM), F32),
                   jax.ShapeDtypeStruct((batch, nblk, SLABS, dil, SUBLANES, LANES), F32)],
        scratch_shapes=[pltpu.VMEM(((sub + 1) * Q_BLOCK, ATT_DIM), BF16),
                        pltpu.VMEM(((sub + 1) * Q_BLOCK, ATT_DIM), BF16)],
        compiler_params=_cparams(3),
        name=f"attn_d{dil}",
    )(view(q), view(k), view(k), view(v), view(v), bias)
    return o.reshape(batch * seq, ATT_DIM), lse.reshape(batch * seq, LANES)


SAMPLE_T = 4
NEW_COLS = LANES


def _attn_sample_kernel(q_ref, kt_ref, kn_ref, vt_ref, vn_ref, bias_ref, mult_ref, o_ref):
    rows = SAMPLE_T * N_HEADS
    q4 = q_ref[...] * (HEAD_DIM ** -0.5)
    qt = jnp.concatenate([jnp.broadcast_to(q4[t:t + 1, :], (N_HEADS, ATT_DIM)) for t in range(SAMPLE_T)], axis=0)
    lane_head = lax.broadcasted_iota(jnp.int32, (rows, ATT_DIM), 1) // HEAD_DIM
    row_head = lax.broadcasted_iota(jnp.int32, (rows, ATT_DIM), 0) % N_HEADS
    own = lane_head == row_head
    qbd = jnp.where(own, qt, 0.0).astype(BF16)
    flat = lambda ref: ref[...].reshape(ATT_DIM, ref.shape[-1]).astype(BF16)
    pad = jnp.zeros((NEW_COLS - SUBLANES, ATT_DIM), F32)
    new_rows = lambda ref: jnp.concatenate([ref[...], pad], axis=0).astype(BF16)
    s = jnp.concatenate([jnp.dot(qbd, flat(kt_ref), preferred_element_type=F32),
                         lax.dot_general(qbd, new_rows(kn_ref), NT, preferred_element_type=F32)],
                        axis=-1) + bias_ref[...]
    m = jnp.max(s, axis=-1, keepdims=True)
    p = jnp.exp(s - m) * mult_ref[...]
    den = jnp.sum(p, axis=-1, keepdims=True)
    pb = p.astype(BF16)
    w_buf = kt_ref.shape[-1]
    acc = (lax.dot_general(pb[:, :w_buf], flat(vt_ref), NT, preferred_element_type=F32)
           + jnp.dot(pb[:, w_buf:], new_rows(vn_ref), preferred_element_type=F32))
    acc = jnp.where(own, acc / den, 0.0)
    for t in range(SAMPLE_T):
        o_ref[t:t + 1, :] = jnp.sum(acc[t * N_HEADS:(t + 1) * N_HEADS, :], axis=0, keepdims=True)


def _attn_sample(q, k_new, v_new, cache_kt, cache_vt, layer, bias, mult):
    nb = q.shape[0]
    w_buf = cache_kt.shape[-1]
    tok = pl.BlockSpec((None, SAMPLE_T, ATT_DIM), lambda b: (b, 0, 0))
    new = pl.BlockSpec((None, SUBLANES, ATT_DIM), lambda b: (b, 0, 0))
    old = pl.BlockSpec((None, None, N_HEADS, HEAD_DIM, w_buf), lambda b: (layer, b, 0, 0, 0))
    tbl = _full((SAMPLE_T * N_HEADS, w_buf + NEW_COLS))
    return pl.pallas_call(
        _attn_sample_kernel,
        grid=(nb,),
        in_specs=[tok, old, new, old, new, tbl, tbl],
        out_specs=tok,
        out_shape=jax.ShapeDtypeStruct((nb, SAMPLE_T, ATT_DIM), F32),
        compiler_params=_cparams(1),
        name="attn_sample",
    )(q, cache_kt, k_new, cache_vt, v_new, bias, mult)


def _split_dot(x, e_ref):
    hi = x.astype(BF16)
    lo = (x - hi.astype(F32)).astype(BF16)
    return (jnp.dot(hi, e_ref[...], preferred_element_type=F32)
            + jnp.dot(lo, e_ref[...], preferred_element_type=F32))


def _outproj_kernel(*refs, n_pat):
    mix = n_pat > 1
    n_lse = n_pat if mix else 0
    o_refs = refs[0:n_pat]
    l_refs = refs[n_pat:n_pat + n_lse]
    rest = refs[n_pat + n_lse:]
    if mix:
        yn_ref, h_ref, gatt_ref, exp_ref, unperm_ref, wa_ref, wc_ref, out_ref = rest
        lses = [r[...] for r in l_refs]
        top = functools.reduce(jnp.maximum, lses)
        ws = [jnp.exp(l - top) for l in lses]
        tot = functools.reduce(lambda a, b: a + b, ws)
        att = None
        for w, o_ref in zip(ws, o_refs):
            term = _split_dot(w / tot, exp_ref) * o_ref[...]
            att = term if att is None else att + term
        att_bf = jnp.dot(unperm_ref[...], _rms(att, gatt_ref[...]).astype(BF16),
                         preferred_element_type=F32).astype(BF16)
    else:
        yn_ref, h_ref, gatt_ref, wa_ref, wc_ref, out_ref = rest
        att_bf = _rms(o_refs[0][...], gatt_ref[...]).astype(BF16)
    y = (jnp.dot(att_bf, wa_ref[...], preferred_element_type=F32)
         + jnp.dot(yn_ref[...].astype(BF16), wc_ref[...], preferred_element_type=F32))
    out_ref[...] = h_ref[...] + y


def _outproj(os, lses, yn, h, g_att, wa_bf, wc_bf, expand=None, unperm=None, *, tm):
    n = h.shape[0]
    n_pat = len(os)
    tok = lambda w: pl.BlockSpec((tm, w), lambda i: (i, 0))
    in_specs = [tok(ATT_DIM)] * n_pat + [tok(LANES)] * len(lses) + [tok(CONV_DIM), tok(D_MODEL), _full((1, ATT_DIM))]
    args = [*os, *lses, yn, h, g_att]
    if n_pat > 1:
        in_specs += [_full((LANES, ATT_DIM)), _full((tm, tm))]
        args += [expand, unperm]
    in_specs += [_full((ATT_DIM, D_MODEL)), _full((CONV_DIM, D_MODEL))]
    args += [wa_bf, wc_bf]
    return pl.pallas_call(
        functools.partial(_outproj_kernel, n_pat=n_pat),
        grid=(n // tm,),
        in_specs=in_specs,
        out_specs=tok(D_MODEL),
        out_shape=jax.ShapeDtypeStruct((n, D_MODEL), F32),
        compiler_params=_cparams(1),
        name="outproj",
    )(*args)


GROUP_LANE0 = N_EXPERTS
REC_E1, REC_E2, REC_S1, REC_S2, REC_W1, REC_W2 = range(6)
PACKED = D_MODEL // 2
ROW_W = PACKED + LANES
N_DMA_PRIORITIES = 2
LIST_LANES = LANES
LIST_COUNT = LIST_LANES - 1
LIST_NEXT = LIST_LANES - 2
LIST_PARITY = LIST_LANES - 3


def _slots(tm):
    need = 2 * tm + (SUBLANES - 1) * N_EXPERTS + SUBLANES
    return -(-need // LANES) * LANES


def _pieces(x):
    hi = x.astype(BF16)
    r1 = x - hi.astype(F32)
    mid = r1.astype(BF16)
    return hi, mid, (r1 - mid.astype(F32)).astype(BF16)


def _sort_kernel(h_ref, g_ref, wr_ref, before_ref, sel_ref, xs_ref, rec_ref, cnt_ref, *, group):
    tm = h_ref.shape[0] // group
    slots = xs_ref.shape[0] // group
    for t in range(group):
        xs, rec, cnt = _sort_tile(h_ref[t * tm:(t + 1) * tm, :], g_ref, wr_ref, before_ref, sel_ref, slots)
        xs_ref[t * slots:(t + 1) * slots, :] = xs
        rec_ref[t * tm:(t + 1) * tm, :] = rec
        cnt_ref[t] = cnt


def _sort_tile(h, g_ref, wr_ref, before_ref, sel_ref, slots):
    m = _rms(h, g_ref[...])
    logits = jnp.dot(m.astype(BF16), wr_ref[...], preferred_element_type=F32)
    tm = logits.shape[0]
    lane_i = lax.broadcasted_iota(jnp.int32, (tm, LANES), 1)
    lane = lane_i.astype(F32)
    big = jnp.float32(4 * LANES)

    is_g = jnp.logical_and(lane_i >= GROUP_LANE0, lane_i < GROUP_LANE0 + N_GROUPS)
    gl = jnp.where(is_g, logits, NEG)
    gmax = jnp.max(gl, axis=-1, keepdims=True)
    g_w = 1.0 / jnp.sum(jnp.where(is_g, jnp.exp(gl - gmax), 0.0), axis=-1, keepdims=True)
    g_sel = jnp.min(jnp.where(gl == gmax, lane - GROUP_LANE0, big), axis=-1, keepdims=True)

    grp_of_lane = (lane_i // EXPERTS_PER_GROUP).astype(F32)
    in_grp = jnp.logical_and(lane_i < N_EXPERTS, grp_of_lane == g_sel)
    el = jnp.where(in_grp, logits, NEG)
    t1 = jnp.max(el, axis=-1, keepdims=True)
    e1 = jnp.min(jnp.where(el == t1, lane, big), axis=-1, keepdims=True)
    el2 = jnp.where(lane == e1, NEG, el)
    t2 = jnp.max(el2, axis=-1, keepdims=True)
    e2 = jnp.min(jnp.where(el2 == t2, lane, big), axis=-1, keepdims=True)
    ex = jnp.exp(t2 - t1)
    w1 = g_w / (1.0 + ex)
    w2 = g_w * ex / (1.0 + ex)

    hit1 = lane == e1
    hit2 = lane == e2
    c = jnp.where(hit1, 1.0, jnp.where(hit2, 1.0, 0.0))
    rr = lax.broadcasted_iota(jnp.int32, (tm, tm), 0)
    cc = lax.broadcasted_iota(jnp.int32, (tm, tm), 1)
    lower = jnp.where(rr > cc, 1.0, 0.0).astype(BF16)
    rank = jnp.dot(lower, c.astype(BF16), preferred_element_type=F32)
    cnt = jnp.sum(c, axis=0, keepdims=True)
    chunks = jnp.floor((cnt + (SUBLANES - 1)) * (1.0 / SUBLANES))
    start = SUBLANES * jnp.dot(jnp.broadcast_to(chunks, (SUBLANES, LANES)).astype(BF16), before_ref[...],
                               preferred_element_type=F32)[0:1, :]
    slot_of = rank + start
    s1 = jnp.sum(jnp.where(hit1, slot_of, 0.0), axis=-1, keepdims=True)
    s2 = jnp.sum(jnp.where(hit2, slot_of, 0.0), axis=-1, keepdims=True)

    rec = jnp.zeros((tm, LANES), F32)
    for idx, val in ((REC_E1, e1), (REC_E2, e2), (REC_S1, s1), (REC_S2, s2), (REC_W1, w1), (REC_W2, w2)):
        rec = jnp.where(lane_i == idx, val, rec)

    rec_parts = _pieces(rec)[:2]
    srow = sum(lax.dot_general(sel_ref[...], part, NT, preferred_element_type=F32) for part in rec_parts)
    slot_id = lax.broadcasted_iota(jnp.int32, (slots, tm), 0).astype(F32)
    place = jnp.where(slot_id == srow[0:1, :], 1.0, jnp.where(slot_id == srow[1:2, :], 1.0, 0.0)).astype(BF16)
    payload = jnp.concatenate([m.astype(BF16), *rec_parts], axis=-1)
    moved = jnp.dot(place, payload, preferred_element_type=F32)
    info = moved[:, D_MODEL:D_MODEL + LANES] + moved[:, D_MODEL + LANES:]
    xs = jnp.concatenate([_pack_bf16(moved[:, :D_MODEL], is_bf16=True), pltpu.bitcast(info, jnp.uint32)], axis=-1)
    return xs, rec, jnp.broadcast_to(cnt, (SUBLANES, LANES))


def _sort(h, g_ffn, w_router, before, sel, *, tm):
    n = h.shape[0]
    slots = _slots(tm)
    group = 2 if (n // tm) % 2 == 0 else 1
    tok = lambda w: pl.BlockSpec((group * tm, w), lambda i: (i, 0))
    return pl.pallas_call(
        functools.partial(_sort_kernel, group=group),
        grid=(n // (group * tm),),
        in_specs=[tok(D_MODEL), _full((1, D_MODEL)), _full((D_MODEL, LANES)), _full((LANES, LANES)),
                  _full((SUBLANES, LANES))],
        out_specs=[pl.BlockSpec((group * slots, ROW_W), lambda i: (i, 0)), tok(LANES),
                   pl.BlockSpec((group, SUBLANES, LANES), lambda i: (i, 0, 0))],
        out_shape=[jax.ShapeDtypeStruct((n // tm * slots, ROW_W), jnp.uint32), jax.ShapeDtypeStruct((n, LANES), F32),
                   jax.ShapeDtypeStruct((n // tm, SUBLANES, LANES), F32)],
        compiler_params=_cparams(1),
        name="moe_sort",
    )(h, g_ffn, w_router, before, sel)


def _chunk_copy(src_hbm, row, dst, c, sem):
    if not isinstance(row, int):
        row = pl.multiple_of(row, SUBLANES)
    to = c * SUBLANES
    if not isinstance(to, int):
        to = pl.multiple_of(to, SUBLANES)
    return pltpu.make_async_copy(src_hbm.at[pl.ds(row, SUBLANES), :], dst.at[pl.ds(to, SUBLANES), :], sem)


def _expert_kernel(te_ref, tv_ref, lst_ref, xa_ref, xb_ref, wg_ref, wu_ref, wd_ref, y_ref,
                    xbuf, sem, wg_st, wu_st, wd_st, wsem, wg_bf, wu_bf, wd_bf, *, rows_a, layer):
    i = pl.program_id(0)
    n = pl.num_programs(0)
    te = xbuf.shape[1]
    slot = i % 2

    def fetch(tile, to_slot):
        n_a = lst_ref[tile, LIST_COUNT]

        def from_a(c, carry):
            priority = c % N_DMA_PRIORITIES if isinstance(c, int) else 0
            _chunk_copy(xa_ref, lst_ref[tile, c], xbuf.at[to_slot], c, sem.at[to_slot]).start(priority=priority)
            return carry

        def from_b(c, carry):
            _chunk_copy(xb_ref, lst_ref[tile, c] - rows_a, xbuf.at[to_slot], c, sem.at[to_slot]).start()
            return carry

        per_tile = te // SUBLANES

        @pl.when(n_a == per_tile)
        def _():
            for c in range(per_tile):
                from_a(c, 0)

        @pl.when(n_a != per_tile)
        def _():
            lax.fori_loop(0, n_a, from_a, 0)
            lax.fori_loop(n_a, per_tile, from_b, 0)

    @pl.when(jnp.logical_and(i == 0, tv_ref[0] > 0))
    def _():
        fetch(0, 0)

    nxt = jnp.minimum(i + 1, n - 1)

    @pl.when(jnp.logical_and(i + 1 < n, tv_ref[nxt] > 0))
    def _():
        fetch(nxt, 1 - slot)

    valid = tv_ref[i]
    changed = jnp.logical_or(i == 0, te_ref[i] != te_ref[jnp.maximum(i - 1, 0)])

    def weight_copies(expert, s):
        return [pltpu.make_async_copy(src.at[layer, expert], dst.at[s], wsem.at[s])
                for src, dst in ((wg_ref, wg_st), (wu_ref, wu_st), (wd_ref, wd_st))]

    @pl.when(jnp.logical_and(changed, valid > 0))
    def _():
        s = lst_ref[i, LIST_PARITY]

        @pl.when(i == 0)
        def _():
            for cp in weight_copies(te_ref[0], 0):
                cp.start()

        for cp in weight_copies(te_ref[i], s):
            cp.wait()
        wg_bf[...] = wg_st[s].astype(BF16)
        wu_bf[...] = wu_st[s].astype(BF16)
        wd_bf[...] = wd_st[s].astype(BF16)
        nxt_expert = lst_ref[i, LIST_NEXT]

        @pl.when(nxt_expert >= 0)
        def _():
            for cp in weight_copies(nxt_expert, 1 - s):
                cp.start()

    @pl.when(valid > 0)
    def _():
        pltpu.make_async_copy(xbuf.at[slot], xbuf.at[slot], sem.at[slot]).wait()
        rows = xbuf[slot]
        x = _unpack_bf16(rows[:, :PACKED])
        info = pltpu.bitcast(rows[:, PACKED:], F32)
        mine = info[:, REC_E1:REC_E1 + 1] == te_ref[i].astype(F32)
        gate = jnp.where(mine, info[:, REC_W1:REC_W1 + 1], info[:, REC_W2:REC_W2 + 1])
        hg = jnp.dot(x, wg_bf[...], preferred_element_type=F32)
        hu = jnp.dot(x, wu_bf[...], preferred_element_type=F32)
        hid = (hg * jax.nn.sigmoid(hg)) * hu * gate
        y = jnp.dot(hid.astype(BF16), wd_bf[...], preferred_element_type=F32)
        y_ref[...] = _pack_bf16(y)

    @pl.when(valid <= 0)
    def _():
        y_ref[...] = jnp.zeros_like(y_ref)


def _pack_bf16(x, is_bf16=False):
    w = x.shape[-1] // 2
    bits = lambda a: pltpu.bitcast(a if is_bf16 else a.astype(BF16).astype(F32), jnp.uint32)
    return jnp.bitwise_or(lax.shift_right_logical(bits(x[:, :w]), jnp.uint32(16)),
                          jnp.bitwise_and(bits(x[:, w:]), jnp.uint32(0xFFFF0000)))


def _unpack_bf16(words, dtype=BF16):
    as_f32 = lambda a: pltpu.bitcast(a, F32)
    return jnp.concatenate([as_f32(lax.shift_left(words, jnp.uint32(16))),
                            as_f32(jnp.bitwise_and(words, jnp.uint32(0xFFFF0000)))], axis=-1).astype(dtype)


def _experts(xs_a, xs_b, tile_expert, tile_valid, src, w_gate, w_up, w_down, *, te, layer):
    n_tiles = tile_expert.shape[0]
    hbm = pl.BlockSpec(memory_space=pl.ANY)
    grid_spec = pltpu.PrefetchScalarGridSpec(
        num_scalar_prefetch=3,
        grid=(n_tiles,),
        in_specs=[hbm, hbm, hbm, hbm, hbm],
        out_specs=pl.BlockSpec((te, PACKED), lambda i, e, v, s: (i, 0)),
        scratch_shapes=[pltpu.VMEM((2, te, ROW_W), jnp.uint32), pltpu.SemaphoreType.DMA((2,)),
                        pltpu.VMEM((2, D_MODEL, D_EXPERT), F32), pltpu.VMEM((2, D_MODEL, D_EXPERT), F32),
                        pltpu.VMEM((2, D_EXPERT, D_MODEL), F32), pltpu.SemaphoreType.DMA((2,)),
                        pltpu.VMEM((D_MODEL, D_EXPERT), BF16), pltpu.VMEM((D_MODEL, D_EXPERT), BF16),
                        pltpu.VMEM((D_EXPERT, D_MODEL), BF16)])
    return pl.pallas_call(
        functools.partial(_expert_kernel, rows_a=xs_a.shape[0], layer=layer),
        grid_spec=grid_spec,
        out_shape=jax.ShapeDtypeStruct((n_tiles * te, PACKED), jnp.uint32),
        compiler_params=_cparams(1),
        name="experts",
    )(tile_expert, tile_valid, src.reshape(n_tiles, LIST_LANES), xs_a, xs_b, w_gate, w_up, w_down)


def _combine_kernel(lst_ref, h_ref, rec_ref, ys_ref, p_ref, gple_ref, wgate_ref, wproj_ref, out_ref,
                     ybuf, sem):
    i = pl.program_id(0)
    n = pl.num_programs(0)
    slot = i % 2
    tm = h_ref.shape[0]
    slots = ybuf.shape[1]

    def fetch(tile, to_slot):
        for c in range(slots // SUBLANES):
            _chunk_copy(ys_ref, lst_ref[tile, c], ybuf.at[to_slot], c, sem.at[to_slot]).start(
                priority=c % N_DMA_PRIORITIES)

    @pl.when(i == 0)
    def _():
        fetch(0, 0)

    @pl.when(i + 1 < n)
    def _():
        fetch(i + 1, 1 - slot)

    pltpu.make_async_copy(ybuf.at[slot], ybuf.at[slot], sem.at[slot]).wait()

    rec = rec_ref[...]
    slot_id = lax.broadcasted_iota(jnp.int32, (tm, slots), 1).astype(F32)
    back = jnp.where(slot_id == rec[:, REC_S1:REC_S1 + 1], 1.0,
                     jnp.where(slot_id == rec[:, REC_S2:REC_S2 + 1], 1.0, 0.0)).astype(BF16)
    h2 = h_ref[...] + jnp.dot(back, _unpack_bf16(ybuf[slot]), preferred_element_type=F32)
    gate = jax.nn.sigmoid(jnp.dot(_rms(h2, gple_ref[...]).astype(BF16), wgate_ref[...], preferred_element_type=F32))
    ple = jnp.dot(p_ref[...].astype(BF16), wproj_ref[...], preferred_element_type=F32)
    out_ref[...] = h2 + ple * gate


def _combine(h, rec, dst, ys, p, g_ple, wgate_bf, wproj_bf, *, tm, layer):
    n = h.shape[0]
    n_tok = n // tm
    tok = lambda w: pl.BlockSpec((tm, w), lambda i, s: (i, 0))
    full = lambda shape: pl.BlockSpec(shape, lambda i, s: (0,) * len(shape))
    grid_spec = pltpu.PrefetchScalarGridSpec(
        num_scalar_prefetch=1,
        grid=(n_tok,),
        in_specs=[tok(D_MODEL), tok(LANES), pl.BlockSpec(memory_space=pl.ANY),
                  pl.BlockSpec((None, tm, D_PLE), lambda i, s: (layer, i, 0)),
                  full((1, D_MODEL)), full((D_MODEL, D_MODEL)), full((D_PLE, D_MODEL))],
        out_specs=tok(D_MODEL),
        scratch_shapes=[pltpu.VMEM((2, _slots(tm), PACKED), jnp.uint32), pltpu.SemaphoreType.DMA((2,))])
    return pl.pallas_call(
        _combine_kernel,
        grid_spec=grid_spec,
        out_shape=jax.ShapeDtypeStruct((n, D_MODEL), F32),
        compiler_params=_cparams(1),
        name="combine",
    )(dst.reshape(n_tok, LIST_LANES), h, rec, ys, p, g_ple, wgate_bf, wproj_bf)


def _excl_cumsum(x, axis):
    return jnp.cumsum(x, axis=axis) - x


def _chunk_plan(cnt, tile_row0, *, n_pairs, rows_a, zero_row, te):
    n_tok = cnt.shape[0]
    per_tile = te // SUBLANES
    chunks = (cnt + SUBLANES - 1) // SUBLANES
    run0 = _excl_cumsum(chunks, 1)
    seg0 = _excl_cumsum(chunks, 0)
    total = jnp.sum(chunks, axis=0)
    region = ((total + per_tile - 1) // per_tile) * per_tile
    reg_end = jnp.cumsum(region)
    reg0 = reg_end - region

    n_tiles = -(-(n_pairs + (SUBLANES - 1) * N_EXPERTS * n_tok) // te) + N_EXPERTS
    t0 = jnp.arange(n_tiles, dtype=jnp.int32) * per_tile
    tile_expert = jnp.minimum(jnp.sum((t0[:, None] >= reg_end[None, :]).astype(jnp.int32), axis=1), N_EXPERTS - 1)
    pick = tile_expert[:, None] == jnp.arange(N_EXPERTS, dtype=jnp.int32)[None, :]
    of_tile = lambda v: jnp.sum(jnp.where(pick, v[None, :], 0), axis=1)
    tile_valid = jnp.clip(of_tile(total) - (t0 - of_tile(reg0)), 0, per_tile).astype(jnp.int32)

    q = (t0 - of_tile(reg0))[:, None] + jnp.arange(per_tile, dtype=jnp.int32)[None, :]
    col_of_tile = lambda v: jnp.sum(jnp.where(pick[:, None, :], v[None, :, :], 0), axis=2)
    seg0_t, seg1_t, run0_t = col_of_tile(seg0), col_of_tile(seg0 + chunks), col_of_tile(run0)
    holds = (q[:, :, None] >= seg0_t[:, None, :]) & (q[:, :, None] < seg1_t[:, None, :])
    local = run0_t[:, None, :] + q[:, :, None] - seg0_t[:, None, :]
    row = jnp.asarray(tile_row0, jnp.int32)[None, None, :] + SUBLANES * local
    src = jnp.sum(jnp.where(holds, row, 0), axis=2)
    src = jnp.where(jnp.any(holds, axis=2), src, zero_row)
    n_first = jnp.sum((src < rows_a).astype(jnp.int32), axis=1)
    idx = jnp.arange(N_EXPERTS, dtype=jnp.int32)
    nonempty = total > 0
    ordinal = _excl_cumsum(nonempty.astype(jnp.int32), 0)
    later = jnp.where((idx[None, :] > idx[:, None]) & nonempty[None, :], idx[None, :], N_EXPERTS)
    following = jnp.min(later, axis=1)
    following = jnp.where(following == N_EXPERTS, -1, following)
    lane = jnp.arange(LIST_LANES, dtype=jnp.int32)[None, :]
    src = jnp.pad(src, ((0, 0), (0, LIST_LANES - per_tile)))
    for at, val in ((LIST_COUNT, n_first), (LIST_NEXT, of_tile(following)), (LIST_PARITY, of_tile(ordinal) % 2)):
        src = jnp.where(lane == at, val[:, None], src)
    src = src.reshape(n_tiles, 1, LIST_LANES).astype(jnp.int32)

    j = jnp.arange(LIST_LANES, dtype=jnp.int32)[None, :, None]
    inside = (j >= run0[:, None, :]) & (j < (run0 + chunks)[:, None, :])
    base = (reg0[None, :] + seg0 - run0)[:, None, :]
    dst = SUBLANES * jnp.sum(jnp.where(inside, base + j, 0), axis=2)
    dst = dst.reshape(n_tok, 1, LIST_LANES).astype(jnp.int32)
    return tile_expert.astype(jnp.int32), tile_valid, src, dst


def _moe_ple(h_a, h_b, p_a, p_b, g_ffn, w_router, before, sel, w_gate, w_up, w_down, g_ple, wgate_bf, wproj_bf,
              *, tm_a, tm_b, te, layer):
    xs_a, rec_a, cnt_a = _sort(h_a, g_ffn, w_router, before, sel, tm=tm_a)
    xs_b, rec_b, cnt_b = _sort(h_b, g_ffn, w_router, before, sel, tm=tm_b)
    t_a, t_b = cnt_a.shape[0], cnt_b.shape[0]
    cnt = jnp.concatenate([cnt_a[:, 0, :N_EXPERTS], cnt_b[:, 0, :N_EXPERTS]], axis=0).astype(jnp.int32)
    rows_a = xs_a.shape[0]
    row0 = np.concatenate([np.arange(t_a) * _slots(tm_a), rows_a + np.arange(t_b) * _slots(tm_b)])
    tile_expert, tile_valid, src, dst = _chunk_plan(
        cnt, row0, n_pairs=2 * (h_a.shape[0] + h_b.shape[0]), rows_a=rows_a,
        zero_row=rows_a + _slots(tm_b) - SUBLANES, te=te)
    ys = _experts(xs_a, xs_b, tile_expert, tile_valid, src, w_gate, w_up, w_down, te=te, layer=layer)
    out_a = _combine(h_a, rec_a, dst[:t_a], ys, p_a, g_ple, wgate_bf, wproj_bf, tm=tm_a, layer=layer)
    out_b = _combine(h_b, rec_b, dst[t_a:], ys, p_b, g_ple, wgate_bf, wproj_bf, tm=tm_b, layer=layer)
    return out_a, out_b


def _bucket_np(dist):
    max_exact = N_BUCKETS // 2
    d_f = np.maximum(dist, 1).astype(np.float32)
    large = max_exact + (np.log(d_f / np.float32(max_exact)) / np.float32(np.log(MAX_DISTANCE / max_exact))
                         * np.float32(N_BUCKETS - max_exact)).astype(np.int32)
    large = np.minimum(large, N_BUCKETS - 1)
    return np.where(dist < max_exact, dist, large).astype(np.int32)


def _bias_from_buckets(rel_bias, bucket, valid):
    onehot = (jnp.asarray(bucket)[..., None] == jnp.arange(N_BUCKETS, dtype=jnp.int32)).astype(F32)
    bias = jnp.einsum("...k,kh->h...", onehot, rel_bias.astype(F32), precision=HIGHEST)
    return jnp.where(jnp.asarray(valid)[None], bias, NEG)


def _block_order(dil):
    g = np.arange(Q_BLOCK) // SUBLANES
    j = np.arange(Q_BLOCK) % SUBLANES
    if dil == 1:
        return 16 * j + g
    if dil == 4:
        return 32 * (g // 4) + 4 * j + g % 4
    return SUBLANES * g + j


def _band_bias(rel_bias, dil):
    mu = _block_order(dil)
    qi = mu[:, None] + Q_BLOCK
    ki = np.concatenate([mu, mu + Q_BLOCK])[None, :]
    off = qi - ki
    valid = (off >= 0) & (off <= N_KEYS)
    bucket = _bucket_np(dil * np.clip(off, 0, N_KEYS))
    first = valid & (np.arange(2 * Q_BLOCK)[None, :] >= Q_BLOCK)
    tables = [_bias_from_buckets(rel_bias, bucket, v).reshape(N_HEADS * Q_BLOCK, 2 * Q_BLOCK) for v in (valid, first)]
    return jnp.stack(tables)


def _sample_tables(rel_bias, w_buf):
    qpos = w_buf + np.arange(SAMPLE_T)[:, None]
    pos = np.arange(w_buf + NEW_COLS)[None, :]
    dist = qpos - pos
    in_seq = (dist >= 0) & (pos < w_buf + SAMPLE_T)
    mult = np.zeros(dist.shape, np.float32)
    for (w, d) in PATTERNS:
        mult += in_seq & (dist % d == 0) & (dist <= w)
    bucket = _bucket_np(np.maximum(dist, 0))
    bias = jnp.transpose(_bias_from_buckets(rel_bias, bucket, mult > 0), (1, 0, 2))
    rows = SAMPLE_T * N_HEADS
    mult_rows = np.broadcast_to(mult[:, None, :], (SAMPLE_T, N_HEADS, mult.shape[-1]))
    return bias.reshape(rows, -1), jnp.asarray(mult_rows.reshape(rows, -1))


PROJ_TM = 512
PERM_ROWS = 256
OUT_TM = 512
PROMPT_TM = 256
PROMPT_TE = 512
ATTN_SUB = 8


def _row_perm(tm):
    a = np.arange(tm)
    src = (a // Q_BLOCK) * Q_BLOCK + 16 * (a % SUBLANES) + (a % Q_BLOCK) // SUBLANES
    perm = np.zeros((tm, tm), np.float32)
    perm[a, src] = 1.0
    return perm


def kernel(x_prompt, x_sample, cache_k, cache_v, state_conv, p_prompt, p_sample, rel_bias, g_mix, w_in, q_gain,
           k_gain, conv_w, g_out_att, g_out_conv, w_out, g_ffn, w_router_group, w_router_expert, w_gate, w_up,
           w_down, g_ple, w_ple_gate, w_ple_proj):
    depth = w_in.shape[0]
    batch, seq, _ = x_prompt.shape
    dec_b, dec_t, _ = x_sample.shape
    w_buf = cache_k.shape[2]
    n_s = dec_b * dec_t
    keep = min(w_buf, seq)
    assert dec_t == SAMPLE_T and w_buf % LANES == 0
    assert seq % (Q_BLOCK * 16 * 2) == 0 and keep % PROJ_TM == 0 and seq % PROMPT_TM == 0
    cache_kt = jnp.transpose(cache_k, (0, 1, 3, 4, 2))
    cache_vt = jnp.transpose(cache_v, (0, 1, 3, 4, 2))

    row = lambda a: a.reshape(1, -1)
    src_lane = np.arange(LANES)
    expand = jnp.asarray((src_lane[:, None] // LSE_LANES_PER_HEAD == np.arange(ATT_DIM)[None, :] // HEAD_DIM)
                         & (src_lane[:, None] % LSE_LANES_PER_HEAD == 0), BF16)
    before = jnp.asarray(np.arange(LANES)[:, None] < np.arange(LANES)[None, :], BF16)
    sel_np = np.zeros((SUBLANES, LANES), np.float32)
    sel_np[0, REC_S1] = sel_np[1, REC_S2] = 1.0
    sel = jnp.asarray(sel_np, BF16)
    perm = jnp.asarray(_row_perm(PERM_ROWS), BF16)
    unperm = jnp.asarray(_row_perm(OUT_TM).T, BF16)
    band = [_band_bias(rel_bias, d) for (_, d) in PATTERNS]
    s_bias, s_mult = _sample_tables(rel_bias, w_buf)

    hp = x_prompt.reshape(batch * seq, D_MODEL)
    hs = jnp.swapaxes(x_sample, 0, 1).reshape(n_s, D_MODEL)
    pp_all = p_prompt.reshape(depth, batch * seq, D_PLE)
    ps_all = jnp.swapaxes(p_sample, 1, 2).reshape(depth, n_s, D_PLE)
    new = {k: [] for k in ("cp", "ks", "vs", "cs")}
    kv_rows = None
    hist_p = jnp.zeros((batch, SUBLANES, CONV_DIM), F32)

    for l in range(depth):
        w_in_bf = w_in[l].astype(BF16)
        wa_bf = w_out[l, :ATT_DIM].astype(BF16)
        wc_bf = w_out[l, ATT_DIM:].astype(BF16)
        wgate_bf = w_ple_gate[l].astype(BF16)
        wproj_bf = w_ple_proj[l].astype(BF16)
        w_router = jnp.concatenate(
            [w_router_expert[l], w_router_group[l],
             jnp.zeros((D_MODEL, LANES - N_EXPERTS - N_GROUPS), F32)], axis=1).astype(BF16)
        qg, kg = row(jnp.tile(q_gain[l], N_HEADS)), row(jnp.tile(k_gain[l], N_HEADS))
        mix = (row(g_mix[l]), w_in_bf, qg, kg, conv_w[l], row(g_out_conv[l]))
        moe = (row(g_ffn[l]), w_router, before, sel, w_gate, w_up, w_down, row(g_ple[l]), wgate_bf, wproj_bf)

        q, k, v, k_rows, v_rows, yn, nconv = _inproj(
            hp, hist_p, *mix, perm, kv_rows, tm=PROJ_TM, shift=1, tiles_per_seq=seq // PROJ_TM,
            keep_tiles=keep // PROJ_TM)
        kv_rows = (k_rows, v_rows)
        os, lses = [], []
        for bias, (_, d) in zip(band, PATTERNS):
            sub = min(ATTN_SUB, seq // (Q_BLOCK * d))
            o, lse = _attn_pattern(q, k, v, bias, batch=batch, seq=seq, dil=d, sub=sub,
                                   res=min(d, ATTN_SUB // sub))
            os.append(o)
            lses.append(lse)
        hp = _outproj(os, lses, yn, hp, row(g_out_att[l]), wa_bf, wc_bf, expand, unperm, tm=OUT_TM)
        new["cp"].append(nconv[:, SUBLANES - 2:])

        hist_s = jnp.swapaxes(state_conv[l], 0, 1).reshape(1, 2 * dec_b, CONV_DIM)
        q, k, v, yn, nconv = _inproj(hs, hist_s, *mix, tm=n_s, shift=dec_b, tiles_per_seq=1)
        bmaj = lambda a: jnp.swapaxes(a.reshape(dec_t, dec_b, N_HEADS, HEAD_DIM), 0, 1)
        qb, kb, vb = bmaj(q), bmaj(k), bmaj(v)
        rows8 = lambda a: jnp.pad(a.reshape(dec_b, dec_t, ATT_DIM), ((0, 0), (0, SUBLANES - dec_t), (0, 0)))
        att = _attn_sample(qb.reshape(dec_b, dec_t, ATT_DIM), rows8(kb), rows8(vb), cache_kt, cache_vt, l,
                           s_bias, s_mult)
        att_tm = jnp.swapaxes(att, 0, 1).reshape(n_s, ATT_DIM)
        hs = _outproj([att_tm], [], yn, hs, row(g_out_att[l]), wa_bf, wc_bf, tm=n_s)
        hp, hs = _moe_ple(hp, hs, pp_all, ps_all, *moe, tm_a=PROMPT_TM, tm_b=n_s, te=PROMPT_TE, layer=l)
        new["ks"].append(kb)
        new["vs"].append(vb)
        new["cs"].append(jnp.swapaxes(nconv.reshape(2, dec_b, CONV_DIM), 0, 1))

    y_prompt = hp.reshape(batch, seq, D_MODEL)
    y_sample = jnp.swapaxes(hs.reshape(dec_t, dec_b, D_MODEL), 0, 1)
    st = lambda key: jnp.stack(new[key])
    natural = lambda rows: jnp.transpose(rows, (0, 1, 4, 2, 3))
    return (y_prompt, y_sample, natural(kv_rows[0]), natural(kv_rows[1]), st("cp"), st("ks"), st("vs"), st("cs"))
```

```python
import functools

import jax
import jax.numpy as jnp
import numpy as np
from jax import lax
from jax.experimental import pallas as pl
from jax.experimental.pallas import tpu as pltpu

F32 = jnp.float32
BF16 = jnp.bfloat16
HIGHEST = lax.Precision.HIGHEST

D_MODEL = 1024
HEAD_DIM = 64
N_HEADS = 8
ATT_DIM = N_HEADS * HEAD_DIM
CONV_DIM = D_MODEL - ATT_DIM
MIX_IN = 3 * ATT_DIM + 3 * CONV_DIM
PATTERNS = ((128, 1), (512, 4), (2048, 16))
N_KEYS = 128
Q_BLOCK = 128
N_BUCKETS = 32
MAX_DISTANCE = 2048
N_GROUPS = 4
EXPERTS_PER_GROUP = 8
N_EXPERTS = N_GROUPS * EXPERTS_PER_GROUP
D_EXPERT = 256
D_PLE = 256
EPS = 1e-6
NEG = -1e30

LANES = 128
SUBLANES = 8
SLABS = Q_BLOCK // SUBLANES
LSE_LANES_PER_HEAD = LANES // N_HEADS
VMEM_LIMIT = 56 * 1024 * 1024
NT = (((1,), (1,)), ((), ()))


def _cparams(n_axes):
    return pltpu.CompilerParams(dimension_semantics=("arbitrary",) * n_axes,
                                vmem_limit_bytes=VMEM_LIMIT)


def _full(shape):
    n = len(shape)
    return pl.BlockSpec(shape, lambda *_: (0,) * n)


def _rms(x, gain):
    ms = jnp.mean(x * x, axis=-1, keepdims=True)
    return x * lax.rsqrt(ms + EPS) * gain


def _inproj_kernel(*refs, shift, tiles_per_seq, permute, n_prev=0):
    (h_ref, gmix_ref, w_ref, qg_ref, kg_ref, cw_ref, gconv_ref, hist_ref) = refs[:8]
    if permute and n_prev:
        (perm_ref, kprev_ref, vprev_ref, q_ref, k_ref, v_ref, kn_ref, vn_ref, yn_ref, nconv_ref,
         carry_ref) = refs[8:]
    elif permute:
        perm_ref, q_ref, k_ref, v_ref, kn_ref, vn_ref, yn_ref, nconv_ref, carry_ref = refs[8:]
    else:
        q_ref, k_ref, v_ref, yn_ref, nconv_ref, carry_ref = refs[8:]
    i = pl.program_id(0)
    a = _rms(h_ref[...], gmix_ref[...])
    proj = jnp.dot(a.astype(BF16), w_ref[...], preferred_element_type=F32)
    tm = proj.shape[0]

    lower = lax.broadcasted_iota(jnp.int32, (tm, LANES), 1) < HEAD_DIM

    def head_norm(t, g):
        out = []
        for j in range(ATT_DIM // LANES):
            blk = t[:, j * LANES:(j + 1) * LANES]
            sq = blk * blk
            ms_lo = jnp.sum(jnp.where(lower, sq, 0.0), axis=-1, keepdims=True) * (1.0 / HEAD_DIM)
            ms_hi = jnp.sum(jnp.where(lower, 0.0, sq), axis=-1, keepdims=True) * (1.0 / HEAD_DIM)
            scale = jnp.where(lower, lax.rsqrt(ms_lo + EPS), lax.rsqrt(ms_hi + EPS))
            out.append(blk * scale)
        return jnp.concatenate(out, axis=-1) * g

    q = head_norm(proj[:, 0:ATT_DIM], qg_ref[...])
    k = head_norm(proj[:, ATT_DIM:2 * ATT_DIM], kg_ref[...])
    v = proj[:, 2 * ATT_DIM:3 * ATT_DIM]
    if permute:
        if n_prev:
            kn_ref[0:n_prev] = kprev_ref[...]
            vn_ref[0:n_prev] = vprev_ref[...]
        kn_ref[n_prev] = k.T.reshape(N_HEADS, HEAD_DIM, tm)
        vn_ref[n_prev] = v.T.reshape(N_HEADS, HEAD_DIM, tm)
        qkv = jnp.concatenate([q * (HEAD_DIM ** -0.5), k, v], axis=-1).astype(BF16)
        pm = perm_ref.shape[0]
        moved = jnp.concatenate([jnp.dot(perm_ref[...], qkv[r:r + pm], preferred_element_type=F32)
                                 for r in range(0, tm, pm)], axis=0)
        q_ref[...] = _pack_bf16(moved[:, 0:ATT_DIM], is_bf16=True)
        k_ref[...] = _pack_bf16(moved[:, ATT_DIM:2 * ATT_DIM], is_bf16=True)
        v_ref[...] = _pack_bf16(moved[:, 2 * ATT_DIM:3 * ATT_DIM], is_bf16=True)
    else:
        q_ref[...] = q
        k_ref[...] = k
        v_ref[...] = v
    c0 = 3 * ATT_DIM
    hc = proj[:, c0:c0 + CONV_DIM]
    gb = proj[:, c0 + CONV_DIM:c0 + 2 * CONV_DIM]
    gc = proj[:, c0 + 2 * CONV_DIM:c0 + 3 * CONV_DIM]
    u = gc * hc

    if shift == 1:
        @pl.when(i % tiles_per_seq == 0)
        def _():
            carry_ref[...] = hist_ref[0]
        h0 = carry_ref[SUBLANES - 2:SUBLANES - 1, :]
        h1 = carry_ref[SUBLANES - 1:SUBLANES, :]
        row = lax.broadcasted_iota(jnp.int32, (tm, 1), 0)
        u1 = jnp.where(row == 0, h1, pltpu.roll(u, 1, 0))
        u2 = jnp.where(row == 0, h0, jnp.where(row == 1, h1, pltpu.roll(u, 2, 0)))
        carry_ref[...] = u[tm - SUBLANES:tm, :]
        nconv_ref[0] = u[tm - SUBLANES:tm, :]
    else:
        hist = hist_ref[0]
        u1 = jnp.concatenate([hist[shift:2 * shift], u[0:tm - shift]], axis=0)
        u2 = jnp.concatenate([hist, u[0:tm - 2 * shift]], axis=0)
        nconv_ref[0] = u[tm - 2 * shift:tm, :]
    conv = cw_ref[0:1, :] * u2 + cw_ref[1:2, :] * u1 + cw_ref[2:3, :] * u
    yn_ref[...] = _rms(gb * conv, gconv_ref[...])


def _inproj(h, hist, g_mix, w_in_bf, q_gain, k_gain, conv_w, g_out_conv, perm=None, kv_prev=None, *,
            tm, shift, tiles_per_seq, keep_tiles=0):
    n = h.shape[0]
    hist_rows = hist.shape[1]
    nseq = hist.shape[0]
    tok = lambda w: pl.BlockSpec((tm, w), lambda i: (i, 0))
    seq3 = lambda r: pl.BlockSpec((1, r, CONV_DIM), lambda i: (i // tiles_per_seq, 0, 0))
    nconv_rows = SUBLANES if shift == 1 else 2 * shift
    packed = perm is not None
    att_w = ATT_DIM // 2 if packed else ATT_DIM
    att = jax.ShapeDtypeStruct((n, att_w), jnp.uint32 if packed else F32)
    in_specs = [tok(D_MODEL), _full((1, D_MODEL)), _full((D_MODEL, MIX_IN)), _full((1, ATT_DIM)),
                _full((1, ATT_DIM)), _full((3, CONV_DIM)), _full((1, CONV_DIM)), seq3(hist_rows)]
    args = [h, g_mix, w_in_bf, q_gain, k_gain, conv_w, g_out_conv, hist]
    out_specs = [tok(att_w)] * 3
    out_shape = [att] * 3
    if perm is not None:
        in_specs.append(_full(perm.shape))
        args.append(perm)
        first = tiles_per_seq - keep_tiles
        kept = lambda layers: pl.BlockSpec(
            (layers, None, N_HEADS, HEAD_DIM, tm),
            lambda i: (0, i // tiles_per_seq, 0, 0, jnp.maximum(i % tiles_per_seq - first, 0)))
        n_prev = 0 if kv_prev is None else kv_prev[0].shape[0]
        if n_prev:
            in_specs += [kept(n_prev)] * 2
            args += list(kv_prev)
        out_specs += [kept(n_prev + 1)] * 2
        out_shape += [jax.ShapeDtypeStruct((n_prev + 1, nseq, N_HEADS, HEAD_DIM, keep_tiles * tm), F32)] * 2
    out_specs += [tok(CONV_DIM), seq3(nconv_rows)]
    out_shape += [jax.ShapeDtypeStruct((n, CONV_DIM), F32),
                  jax.ShapeDtypeStruct((nseq, nconv_rows, CONV_DIM), F32)]
    n_prev = 0 if kv_prev is None else kv_prev[0].shape[0]
    return pl.pallas_call(
        functools.partial(_inproj_kernel, shift=shift, tiles_per_seq=tiles_per_seq, permute=perm is not None,
                          n_prev=n_prev),
        grid=(n // tm,),
        in_specs=in_specs, out_specs=out_specs, out_shape=out_shape,
        scratch_shapes=[pltpu.VMEM((SUBLANES, CONV_DIM), F32)],
        compiler_params=_cparams(1),
        name="inproj",
    )(*args)


def _attn_block(q, kk, vv, bias):
    lane = lax.broadcasted_iota(jnp.int32, (Q_BLOCK, LANES), 1)
    upper = lane >= HEAD_DIM
    scores = []
    for h in range(N_HEADS):
        j, e = divmod(h, 2)
        qp = q[:, j * LANES:(j + 1) * LANES]
        qm = (jnp.where(upper, qp, 0.0) if e else jnp.where(upper, 0.0, qp)).astype(BF16)
        scores.append(lax.dot_general(qm, kk[:, j * LANES:(j + 1) * LANES], NT, preferred_element_type=F32))
    s = jnp.concatenate(scores, axis=0) + bias
    m = jnp.max(s, axis=-1, keepdims=True)
    p = jnp.exp(s - m)
    den = jnp.sum(p, axis=-1, keepdims=True)
    pb = p.astype(BF16)
    head_rows = lambda t, h: t[h * Q_BLOCK:(h + 1) * Q_BLOCK]
    lse_grp = lane // LSE_LANES_PER_HEAD
    m_tile = jnp.zeros((Q_BLOCK, LANES), F32)
    den_tile = jnp.ones((Q_BLOCK, LANES), F32)
    outs = []
    for j in range(N_HEADS // 2):
        even, odd = 2 * j, 2 * j + 1
        vp = vv[:, j * LANES:(j + 1) * LANES]
        pair = jnp.where(upper, jnp.dot(head_rows(pb, odd), vp, preferred_element_type=F32),
                         jnp.dot(head_rows(pb, even), vp, preferred_element_type=F32))
        outs.append(pair / jnp.where(upper, head_rows(den, odd), head_rows(den, even)))
        for h in (even, odd):
            m_tile = jnp.where(lse_grp == h, head_rows(m, h), m_tile)
            den_tile = jnp.where(lse_grp == h, head_rows(den, h), den_tile)
    return jnp.concatenate(outs, axis=-1), m_tile + jnp.log(den_tile)


def _attn_kernel(q_ref, kp_ref, kc_ref, vp_ref, vc_ref, bias_ref, o_ref, lse_ref, kbuf, vbuf, *, sub, res):
    n = pl.program_id(2)
    rows = sub * Q_BLOCK
    half = ATT_DIM // 2
    for r in range(res):
        kbuf[0:Q_BLOCK, :] = _unpack_bf16(kp_ref[:, r].reshape(Q_BLOCK, half))
        kbuf[Q_BLOCK:, :] = _unpack_bf16(kc_ref[:, :, r].reshape(rows, half))
        vbuf[0:Q_BLOCK, :] = _unpack_bf16(vp_ref[:, r].reshape(Q_BLOCK, half))
        vbuf[Q_BLOCK:, :] = _unpack_bf16(vc_ref[:, :, r].reshape(rows, half))
        for j in range(sub):
            q = _unpack_bf16(q_ref[j, :, r].reshape(Q_BLOCK, half), F32)
            r0 = j * Q_BLOCK
            first = (n == 0).astype(jnp.int32) if j == 0 else 0
            o, lse = _attn_block(q, kbuf[r0:r0 + 2 * Q_BLOCK, :], vbuf[r0:r0 + 2 * Q_BLOCK, :], bias_ref[first])
            o_ref[j, :, r] = o.reshape(SLABS, SUBLANES, ATT_DIM)
            lse_ref[j, :, r] = lse.reshape(SLABS, SUBLANES, LANES)


def _attn_pattern(q, k, v, bias, *, batch, seq, dil, sub, res):
    nblk = seq // (Q_BLOCK * dil)
    view = lambda t: t.reshape(batch, nblk, SLABS, dil, SUBLANES, t.shape[-1])
    cur = lambda c: pl.BlockSpec((None, sub, SLABS, res, SUBLANES, c), lambda b, r, n: (b, n, 0, r, 0, 0))
    half = ATT_DIM // 2
    prev = pl.BlockSpec((None, None, SLABS, res, SUBLANES, half),
                        lambda b, r, n: (b, jnp.maximum(n * sub - 1, 0), 0, r, 0, 0))
    o, lse = pl.pallas_call(
        functools.partial(_attn_kernel, sub=sub, res=res),
        grid=(batch, dil // res, nblk // sub),
        in_specs=[cur(half), prev, cur(half), prev, cur(half),
                  _full((2, N_HEADS * Q_BLOCK, 2 * Q_BLOCK))],
        out_specs=[cur(ATT_DIM), cur(LANES)],
        out_shape=[jax.ShapeDtypeStruct((batch, nblk, SLABS, dil, SUBLANES, ATT_DIM), F32),
                   jax.ShapeDtypeStruct((batch, nblk, SLABS, dil, SUBLANES, LANES), F32)],
        scratch_shapes=[pltpu.VMEM(((sub + 1) * Q_BLOCK, ATT_DIM), BF16),
                        pltpu.VMEM(((sub + 1) * Q_BLOCK, ATT_DIM), BF16)],
        compiler_params=_cparams(3),
        name=f"attn_d{dil}",
    )(view(q), view(k), view(k), view(v), view(v), bias)
    return o.reshape(batch * seq, ATT_DIM), lse.reshape(batch * seq, LANES)


SAMPLE_T = 4
NEW_COLS = LANES


def _attn_sample_kernel(q_ref, kt_ref, kn_ref, vt_ref, vn_ref, bias_ref, mult_ref, o_ref):
    rows = SAMPLE_T * N_HEADS
    q4 = q_ref[...] * (HEAD_DIM ** -0.5)
    qt = jnp.concatenate([jnp.broadcast_to(q4[t:t + 1, :], (N_HEADS, ATT_DIM)) for t in range(SAMPLE_T)], axis=0)
    lane_head = lax.broadcasted_iota(jnp.int32, (rows, ATT_DIM), 1) // HEAD_DIM
    row_head = lax.broadcasted_iota(jnp.int32, (rows, ATT_DIM), 0) % N_HEADS
    own = lane_head == row_head
    qbd = jnp.where(own, qt, 0.0).astype(BF16)
    flat = lambda ref: ref[...].reshape(ATT_DIM, ref.shape[-1]).astype(BF16)
    pad = jnp.zeros((NEW_COLS - SUBLANES, ATT_DIM), F32)
    new_rows = lambda ref: jnp.concatenate([ref[...], pad], axis=0).astype(BF16)
    s = jnp.concatenate([jnp.dot(qbd, flat(kt_ref), preferred_element_type=F32),
                         lax.dot_general(qbd, new_rows(kn_ref), NT, preferred_element_type=F32)],
                        axis=-1) + bias_ref[...]
    m = jnp.max(s, axis=-1, keepdims=True)
    p = jnp.exp(s - m) * mult_ref[...]
    den = jnp.sum(p, axis=-1, keepdims=True)
    pb = p.astype(BF16)
    w_buf = kt_ref.shape[-1]
    acc = (lax.dot_general(pb[:, :w_buf], flat(vt_ref), NT, preferred_element_type=F32)
           + jnp.dot(pb[:, w_buf:], new_rows(vn_ref), preferred_element_type=F32))
    acc = jnp.where(own, acc / den, 0.0)
    for t in range(SAMPLE_T):
        o_ref[t:t + 1, :] = jnp.sum(acc[t * N_HEADS:(t + 1) * N_HEADS, :], axis=0, keepdims=True)


def _attn_sample(q, k_new, v_new, cache_kt, cache_vt, layer, bias, mult):
    nb = q.shape[0]
    w_buf = cache_kt.shape[-1]
    tok = pl.BlockSpec((None, SAMPLE_T, ATT_DIM), lambda b: (b, 0, 0))
    new = pl.BlockSpec((None, SUBLANES, ATT_DIM), lambda b: (b, 0, 0))
    old = pl.BlockSpec((None, None, N_HEADS, HEAD_DIM, w_buf), lambda b: (layer, b, 0, 0, 0))
    tbl = _full((SAMPLE_T * N_HEADS, w_buf + NEW_COLS))
    return pl.pallas_call(
        _attn_sample_kernel,
        grid=(nb,),
        in_specs=[tok, old, new, old, new, tbl, tbl],
        out_specs=tok,
        out_shape=jax.ShapeDtypeStruct((nb, SAMPLE_T, ATT_DIM), F32),
        compiler_params=_cparams(1),
        name="attn_sample",
    )(q, cache_kt, k_new, cache_vt, v_new, bias, mult)


def _split_dot(x, e_ref):
    hi = x.astype(BF16)
    lo = (x - hi.astype(F32)).astype(BF16)
    return (jnp.dot(hi, e_ref[...], preferred_element_type=F32)
            + jnp.dot(lo, e_ref[...], preferred_element_type=F32))


def _outproj_kernel(*refs, n_pat):
    mix = n_pat > 1
    n_lse = n_pat if mix else 0
    o_refs = refs[0:n_pat]
    l_refs = refs[n_pat:n_pat + n_lse]
    rest = refs[n_pat + n_lse:]
    if mix:
        yn_ref, h_ref, gatt_ref, exp_ref, unperm_ref, wa_ref, wc_ref, out_ref = rest
        lses = [r[...] for r in l_refs]
        top = functools.reduce(jnp.maximum, lses)
        ws = [jnp.exp(l - top) for l in lses]
        tot = functools.reduce(lambda a, b: a + b, ws)
        att = None
        for w, o_ref in zip(ws, o_refs):
            term = _split_dot(w / tot, exp_ref) * o_ref[...]
            att = term if att is None else att + term
        att_bf = jnp.dot(unperm_ref[...], _rms(att, gatt_ref[...]).astype(BF16),
                         preferred_element_type=F32).astype(BF16)
    else:
        yn_ref, h_ref, gatt_ref, wa_ref, wc_ref, out_ref = rest
        att_bf = _rms(o_refs[0][...], gatt_ref[...]).astype(BF16)
    y = (jnp.dot(att_bf, wa_ref[...], preferred_element_type=F32)
         + jnp.dot(yn_ref[...].astype(BF16), wc_ref[...], preferred_element_type=F32))
    out_ref[...] = h_ref[...] + y


def _outproj(os, lses, yn, h, g_att, wa_bf, wc_bf, expand=None, unperm=None, *, tm):
    n = h.shape[0]
    n_pat = len(os)
    tok = lambda w: pl.BlockSpec((tm, w), lambda i: (i, 0))
    in_specs = [tok(ATT_DIM)] * n_pat + [tok(LANES)] * len(lses) + [tok(CONV_DIM), tok(D_MODEL), _full((1, ATT_DIM))]
    args = [*os, *lses, yn, h, g_att]
    if n_pat > 1:
        in_specs += [_full((LANES, ATT_DIM)), _full((tm, tm))]
        args += [expand, unperm]
    in_specs += [_full((ATT_DIM, D_MODEL)), _full((CONV_DIM, D_MODEL))]
    args += [wa_bf, wc_bf]
    return pl.pallas_call(
        functools.partial(_outproj_kernel, n_pat=n_pat),
        grid=(n // tm,),
        in_specs=in_specs,
        out_specs=tok(D_MODEL),
        out_shape=jax.ShapeDtypeStruct((n, D_MODEL), F32),
        compiler_params=_cparams(1),
        name="outproj",
    )(*args)


GROUP_LANE0 = N_EXPERTS
REC_E1, REC_E2, REC_S1, REC_S2, REC_W1, REC_W2 = range(6)
PACKED = D_MODEL // 2
ROW_W = PACKED + LANES
N_DMA_PRIORITIES = 2
LIST_LANES = LANES
LIST_COUNT = LIST_LANES - 1
LIST_NEXT = LIST_LANES - 2
LIST_PARITY = LIST_LANES - 3


def _slots(tm):
    need = 2 * tm + (SUBLANES - 1) * N_EXPERTS + SUBLANES
    return -(-need // LANES) * LANES


def _pieces(x):
    hi = x.astype(BF16)
    r1 = x - hi.astype(F32)
    mid = r1.astype(BF16)
    return hi, mid, (r1 - mid.astype(F32)).astype(BF16)


def _sort_kernel(h_ref, g_ref, wr_ref, before_ref, sel_ref, xs_ref, rec_ref, cnt_ref, *, group):
    tm = h_ref.shape[0] // group
    slots = xs_ref.shape[0] // group
    for t in range(group):
        xs, rec, cnt = _sort_tile(h_ref[t * tm:(t + 1) * tm, :], g_ref, wr_ref, before_ref, sel_ref, slots)
        xs_ref[t * slots:(t + 1) * slots, :] = xs
        rec_ref[t * tm:(t + 1) * tm, :] = rec
        cnt_ref[t] = cnt


def _sort_tile(h, g_ref, wr_ref, before_ref, sel_ref, slots):
    m = _rms(h, g_ref[...])
    logits = jnp.dot(m.astype(BF16), wr_ref[...], preferred_element_type=F32)
    tm = logits.shape[0]
    lane_i = lax.broadcasted_iota(jnp.int32, (tm, LANES), 1)
    lane = lane_i.astype(F32)
    big = jnp.float32(4 * LANES)

    is_g = jnp.logical_and(lane_i >= GROUP_LANE0, lane_i < GROUP_LANE0 + N_GROUPS)
    gl = jnp.where(is_g, logits, NEG)
    gmax = jnp.max(gl, axis=-1, keepdims=True)
    g_w = 1.0 / jnp.sum(jnp.where(is_g, jnp.exp(gl - gmax), 0.0), axis=-1, keepdims=True)
    g_sel = jnp.min(jnp.where(gl == gmax, lane - GROUP_LANE0, big), axis=-1, keepdims=True)

    grp_of_lane = (lane_i // EXPERTS_PER_GROUP).astype(F32)
    in_grp = jnp.logical_and(lane_i < N_EXPERTS, grp_of_lane == g_sel)
    el = jnp.where(in_grp, logits, NEG)
    t1 = jnp.max(el, axis=-1, keepdims=True)
    e1 = jnp.min(jnp.where(el == t1, lane, big), axis=-1, keepdims=True)
    el2 = jnp.where(lane == e1, NEG, el)
    t2 = jnp.max(el2, axis=-1, keepdims=True)
    e2 = jnp.min(jnp.where(el2 == t2, lane, big), axis=-1, keepdims=True)
    ex = jnp.exp(t2 - t1)
    w1 = g_w / (1.0 + ex)
    w2 = g_w * ex / (1.0 + ex)

    hit1 = lane == e1
    hit2 = lane == e2
    c = jnp.where(hit1, 1.0, jnp.where(hit2, 1.0, 0.0))
    rr = lax.broadcasted_iota(jnp.int32, (tm, tm), 0)
    cc = lax.broadcasted_iota(jnp.int32, (tm, tm), 1)
    lower = jnp.where(rr > cc, 1.0, 0.0).astype(BF16)
    rank = jnp.dot(lower, c.astype(BF16), preferred_element_type=F32)
    cnt = jnp.sum(c, axis=0, keepdims=True)
    chunks = jnp.floor((cnt + (SUBLANES - 1)) * (1.0 / SUBLANES))
    start = SUBLANES * jnp.dot(jnp.broadcast_to(chunks, (SUBLANES, LANES)).astype(BF16), before_ref[...],
                               preferred_element_type=F32)[0:1, :]
    slot_of = rank + start
    s1 = jnp.sum(jnp.where(hit1, slot_of, 0.0), axis=-1, keepdims=True)
    s2 = jnp.sum(jnp.where(hit2, slot_of, 0.0), axis=-1, keepdims=True)

    rec = jnp.zeros((tm, LANES), F32)
    for idx, val in ((REC_E1, e1), (REC_E2, e2), (REC_S1, s1), (REC_S2, s2), (REC_W1, w1), (REC_W2, w2)):
        rec = jnp.where(lane_i == idx, val, rec)

    rec_parts = _pieces(rec)[:2]
    srow = sum(lax.dot_general(sel_ref[...], part, NT, preferred_element_type=F32) for part in rec_parts)
    slot_id = lax.broadcasted_iota(jnp.int32, (slots, tm), 0).astype(F32)
    place = jnp.where(slot_id == srow[0:1, :], 1.0, jnp.where(slot_id == srow[1:2, :], 1.0, 0.0)).astype(BF16)
    payload = jnp.concatenate([m.astype(BF16), *rec_parts], axis=-1)
    moved = jnp.dot(place, payload, preferred_element_type=F32)
    info = moved[:, D_MODEL:D_MODEL + LANES] + moved[:, D_MODEL + LANES:]
    xs = jnp.concatenate([_pack_bf16(moved[:, :D_MODEL], is_bf16=True), pltpu.bitcast(info, jnp.uint32)], axis=-1)
    return xs, rec, jnp.broadcast_to(cnt, (SUBLANES, LANES))


def _sort(h, g_ffn, w_router, before, sel, *, tm):
    n = h.shape[0]
    slots = _slots(tm)
    group = 2 if (n // tm) % 2 == 0 else 1
    tok = lambda w: pl.BlockSpec((group * tm, w), lambda i: (i, 0))
    return pl.pallas_call(
        functools.partial(_sort_kernel, group=group),
        grid=(n // (group * tm),),
        in_specs=[tok(D_MODEL), _full((1, D_MODEL)), _full((D_MODEL, LANES)), _full((LANES, LANES)),
                  _full((SUBLANES, LANES))],
        out_specs=[pl.BlockSpec((group * slots, ROW_W), lambda i: (i, 0)), tok(LANES),
                   pl.BlockSpec((group, SUBLANES, LANES), lambda i: (i, 0, 0))],
        out_shape=[jax.ShapeDtypeStruct((n // tm * slots, ROW_W), jnp.uint32), jax.ShapeDtypeStruct((n, LANES), F32),
                   jax.ShapeDtypeStruct((n // tm, SUBLANES, LANES), F32)],
        compiler_params=_cparams(1),
        name="moe_sort",
    )(h, g_ffn, w_router, before, sel)


def _chunk_copy(src_hbm, row, dst, c, sem):
    if not isinstance(row, int):
        row = pl.multiple_of(row, SUBLANES)
    to = c * SUBLANES
    if not isinstance(to, int):
        to = pl.multiple_of(to, SUBLANES)
    return pltpu.make_async_copy(src_hbm.at[pl.ds(row, SUBLANES), :], dst.at[pl.ds(to, SUBLANES), :], sem)


def _expert_kernel(te_ref, tv_ref, lst_ref, xa_ref, xb_ref, wg_ref, wu_ref, wd_ref, y_ref,
                    xbuf, sem, wg_st, wu_st, wd_st, wsem, wg_bf, wu_bf, wd_bf, *, rows_a, layer):
    i = pl.program_id(0)
    n = pl.num_programs(0)
    te = xbuf.shape[1]
    slot = i % 2

    def fetch(tile, to_slot):
        n_a = lst_ref[tile, LIST_COUNT]

        def from_a(c, carry):
            priority = c % N_DMA_PRIORITIES if isinstance(c, int) else 0
            _chunk_copy(xa_ref, lst_ref[tile, c], xbuf.at[to_slot], c, sem.at[to_slot]).start(priority=priority)
            return carry

        def from_b(c, carry):
            _chunk_copy(xb_ref, lst_ref[tile, c] - rows_a, xbuf.at[to_slot], c, sem.at[to_slot]).start()
            return carry

        per_tile = te // SUBLANES

        @pl.when(n_a == per_tile)
        def _():
            for c in range(per_tile):
                from_a(c, 0)

        @pl.when(n_a != per_tile)
        def _():
            lax.fori_loop(0, n_a, from_a, 0)
            lax.fori_loop(n_a, per_tile, from_b, 0)

    @pl.when(jnp.logical_and(i == 0, tv_ref[0] > 0))
    def _():
        fetch(0, 0)

    nxt = jnp.minimum(i + 1, n - 1)

    @pl.when(jnp.logical_and(i + 1 < n, tv_ref[nxt] > 0))
    def _():
        fetch(nxt, 1 - slot)

    valid = tv_ref[i]
    changed = jnp.logical_or(i == 0, te_ref[i] != te_ref[jnp.maximum(i - 1, 0)])

    def weight_copies(expert, s):
        return [pltpu.make_async_copy(src.at[layer, expert], dst.at[s], wsem.at[s])
                for src, dst in ((wg_ref, wg_st), (wu_ref, wu_st), (wd_ref, wd_st))]

    @pl.when(jnp.logical_and(changed, valid > 0))
    def _():
        s = lst_ref[i, LIST_PARITY]

        @pl.when(i == 0)
        def _():
            for cp in weight_copies(te_ref[0], 0):
                cp.start()

        for cp in weight_copies(te_ref[i], s):
            cp.wait()
        wg_bf[...] = wg_st[s].astype(BF16)
        wu_bf[...] = wu_st[s].astype(BF16)
        wd_bf[...] = wd_st[s].astype(BF16)
        nxt_expert = lst_ref[i, LIST_NEXT]

        @pl.when(nxt_expert >= 0)
        def _():
            for cp in weight_copies(nxt_expert, 1 - s):
                cp.start()

    @pl.when(valid > 0)
    def _():
        pltpu.make_async_copy(xbuf.at[slot], xbuf.at[slot], sem.at[slot]).wait()
        rows = xbuf[slot]
        x = _unpack_bf16(rows[:, :PACKED])
        info = pltpu.bitcast(rows[:, PACKED:], F32)
        mine = info[:, REC_E1:REC_E1 + 1] == te_ref[i].astype(F32)
        gate = jnp.where(mine, info[:, REC_W1:REC_W1 + 1], info[:, REC_W2:REC_W2 + 1])
        hg = jnp.dot(x, wg_bf[...], preferred_element_type=F32)
        hu = jnp.dot(x, wu_bf[...], preferred_element_type=F32)
        hid = (hg * jax.nn.sigmoid(hg)) * hu * gate
        y = jnp.dot(hid.astype(BF16), wd_bf[...], preferred_element_type=F32)
        y_ref[...] = _pack_bf16(y)

    @pl.when(valid <= 0)
    def _():
        y_ref[...] = jnp.zeros_like(y_ref)


def _pack_bf16(x, is_bf16=False):
    w = x.shape[-1] // 2
    bits = lambda a: pltpu.bitcast(a if is_bf16 else a.astype(BF16).astype(F32), jnp.uint32)
    return jnp.bitwise_or(lax.shift_right_logical(bits(x[:, :w]), jnp.uint32(16)),
                          jnp.bitwise_and(bits(x[:, w:]), jnp.uint32(0xFFFF0000)))


def _unpack_bf16(words, dtype=BF16):
    as_f32 = lambda a: pltpu.bitcast(a, F32)
    return jnp.concatenate([as_f32(lax.shift_left(words, jnp.uint32(16))),
                            as_f32(jnp.bitwise_and(words, jnp.uint32(0xFFFF0000)))], axis=-1).astype(dtype)


def _experts(xs_a, xs_b, tile_expert, tile_valid, src, w_gate, w_up, w_down, *, te, layer):
    n_tiles = tile_expert.shape[0]
    hbm = pl.BlockSpec(memory_space=pl.ANY)
    grid_spec = pltpu.PrefetchScalarGridSpec(
        num_scalar_prefetch=3,
        grid=(n_tiles,),
        in_specs=[hbm, hbm, hbm, hbm, hbm],
        out_specs=pl.BlockSpec((te, PACKED), lambda i, e, v, s: (i, 0)),
        scratch_shapes=[pltpu.VMEM((2, te, ROW_W), jnp.uint32), pltpu.SemaphoreType.DMA((2,)),
                        pltpu.VMEM((2, D_MODEL, D_EXPERT), F32), pltpu.VMEM((2, D_MODEL, D_EXPERT), F32),
                        pltpu.VMEM((2, D_EXPERT, D_MODEL), F32), pltpu.SemaphoreType.DMA((2,)),
                        pltpu.VMEM((D_MODEL, D_EXPERT), BF16), pltpu.VMEM((D_MODEL, D_EXPERT), BF16),
                        pltpu.VMEM((D_EXPERT, D_MODEL), BF16)])
    return pl.pallas_call(
        functools.partial(_expert_kernel, rows_a=xs_a.shape[0], layer=layer),
        grid_spec=grid_spec,
        out_shape=jax.ShapeDtypeStruct((n_tiles * te, PACKED), jnp.uint32),
        compiler_params=_cparams(1),
        name="experts",
    )(tile_expert, tile_valid, src.reshape(n_tiles, LIST_LANES), xs_a, xs_b, w_gate, w_up, w_down)


def _combine_kernel(lst_ref, h_ref, rec_ref, ys_ref, p_ref, gple_ref, wgate_ref, wproj_ref, out_ref,
                     ybuf, sem):
    i = pl.program_id(0)
    n = pl.num_programs(0)
    slot = i % 2
    tm = h_ref.shape[0]
    slots = ybuf.shape[1]

    def fetch(tile, to_slot):
        for c in range(slots // SUBLANES):
            _chunk_copy(ys_ref, lst_ref[tile, c], ybuf.at[to_slot], c, sem.at[to_slot]).start(
                priority=c % N_DMA_PRIORITIES)

    @pl.when(i == 0)
    def _():
        fetch(0, 0)

    @pl.when(i + 1 < n)
    def _():
        fetch(i + 1, 1 - slot)

    pltpu.make_async_copy(ybuf.at[slot], ybuf.at[slot], sem.at[slot]).wait()

    rec = rec_ref[...]
    slot_id = lax.broadcasted_iota(jnp.int32, (tm, slots), 1).astype(F32)
    back = jnp.where(slot_id == rec[:, REC_S1:REC_S1 + 1], 1.0,
                     jnp.where(slot_id == rec[:, REC_S2:REC_S2 + 1], 1.0, 0.0)).astype(BF16)
    h2 = h_ref[...] + jnp.dot(back, _unpack_bf16(ybuf[slot]), preferred_element_type=F32)
    gate = jax.nn.sigmoid(jnp.dot(_rms(h2, gple_ref[...]).astype(BF16), wgate_ref[...], preferred_element_type=F32))
    ple = jnp.dot(p_ref[...].astype(BF16), wproj_ref[...], preferred_element_type=F32)
    out_ref[...] = h2 + ple * gate


def _combine(h, rec, dst, ys, p, g_ple, wgate_bf, wproj_bf, *, tm, layer):
    n = h.shape[0]
    n_tok = n // tm
    tok = lambda w: pl.BlockSpec((tm, w), lambda i, s: (i, 0))
    full = lambda shape: pl.BlockSpec(shape, lambda i, s: (0,) * len(shape))
    grid_spec = pltpu.PrefetchScalarGridSpec(
        num_scalar_prefetch=1,
        grid=(n_tok,),
        in_specs=[tok(D_MODEL), tok(LANES), pl.BlockSpec(memory_space=pl.ANY),
                  pl.BlockSpec((None, tm, D_PLE), lambda i, s: (layer, i, 0)),
                  full((1, D_MODEL)), full((D_MODEL, D_MODEL)), full((D_PLE, D_MODEL))],
        out_specs=tok(D_MODEL),
        scratch_shapes=[pltpu.VMEM((2, _slots(tm), PACKED), jnp.uint32), pltpu.SemaphoreType.DMA((2,))])
    return pl.pallas_call(
        _combine_kernel,
        grid_spec=grid_spec,
        out_shape=jax.ShapeDtypeStruct((n, D_MODEL), F32),
        compiler_params=_cparams(1),
        name="combine",
    )(dst.reshape(n_tok, LIST_LANES), h, rec, ys, p, g_ple, wgate_bf, wproj_bf)


def _excl_cumsum(x, axis):
    return jnp.cumsum(x, axis=axis) - x


def _chunk_plan(cnt, tile_row0, *, n_pairs, rows_a, zero_row, te):
    n_tok = cnt.shape[0]
    per_tile = te // SUBLANES
    chunks = (cnt + SUBLANES - 1) // SUBLANES
    run0 = _excl_cumsum(chunks, 1)
    seg0 = _excl_cumsum(chunks, 0)
    total = jnp.sum(chunks, axis=0)
    region = ((total + per_tile - 1) // per_tile) * per_tile
    reg_end = jnp.cumsum(region)
    reg0 = reg_end - region

    n_tiles = -(-(n_pairs + (SUBLANES - 1) * N_EXPERTS * n_tok) // te) + N_EXPERTS
    t0 = jnp.arange(n_tiles, dtype=jnp.int32) * per_tile
    tile_expert = jnp.minimum(jnp.sum((t0[:, None] >= reg_end[None, :]).astype(jnp.int32), axis=1), N_EXPERTS - 1)
    pick = tile_expert[:, None] == jnp.arange(N_EXPERTS, dtype=jnp.int32)[None, :]
    of_tile = lambda v: jnp.sum(jnp.where(pick, v[None, :], 0), axis=1)
    tile_valid = jnp.clip(of_tile(total) - (t0 - of_tile(reg0)), 0, per_tile).astype(jnp.int32)

    q = (t0 - of_tile(reg0))[:, None] + jnp.arange(per_tile, dtype=jnp.int32)[None, :]
    col_of_tile = lambda v: jnp.sum(jnp.where(pick[:, None, :], v[None, :, :], 0), axis=2)
    seg0_t, seg1_t, run0_t = col_of_tile(seg0), col_of_tile(seg0 + chunks), col_of_tile(run0)
    holds = (q[:, :, None] >= seg0_t[:, None, :]) & (q[:, :, None] < seg1_t[:, None, :])
    local = run0_t[:, None, :] + q[:, :, None] - seg0_t[:, None, :]
    row = jnp.asarray(tile_row0, jnp.int32)[None, None, :] + SUBLANES * local
    src = jnp.sum(jnp.where(holds, row, 0), axis=2)
    src = jnp.where(jnp.any(holds, axis=2), src, zero_row)
    n_first = jnp.sum((src < rows_a).astype(jnp.int32), axis=1)
    idx = jnp.arange(N_EXPERTS, dtype=jnp.int32)
    nonempty = total > 0
    ordinal = _excl_cumsum(nonempty.astype(jnp.int32), 0)
    later = jnp.where((idx[None, :] > idx[:, None]) & nonempty[None, :], idx[None, :], N_EXPERTS)
    following = jnp.min(later, axis=1)
    following = jnp.where(following == N_EXPERTS, -1, following)
    lane = jnp.arange(LIST_LANES, dtype=jnp.int32)[None, :]
    src = jnp.pad(src, ((0, 0), (0, LIST_LANES - per_tile)))
    for at, val in ((LIST_COUNT, n_first), (LIST_NEXT, of_tile(following)), (LIST_PARITY, of_tile(ordinal) % 2)):
        src = jnp.where(lane == at, val[:, None], src)
    src = src.reshape(n_tiles, 1, LIST_LANES).astype(jnp.int32)

    j = jnp.arange(LIST_LANES, dtype=jnp.int32)[None, :, None]
    inside = (j >= run0[:, None, :]) & (j < (run0 + chunks)[:, None, :])
    base = (reg0[None, :] + seg0 - run0)[:, None, :]
    dst = SUBLANES * jnp.sum(jnp.where(inside, base + j, 0), axis=2)
    dst = dst.reshape(n_tok, 1, LIST_LANES).astype(jnp.int32)
    return tile_expert.astype(jnp.int32), tile_valid, src, dst


def _moe_ple(h_a, h_b, p_a, p_b, g_ffn, w_router, before, sel, w_gate, w_up, w_down, g_ple, wgate_bf, wproj_bf,
              *, tm_a, tm_b, te, layer):
    xs_a, rec_a, cnt_a = _sort(h_a, g_ffn, w_router, before, sel, tm=tm_a)
    xs_b, rec_b, cnt_b = _sort(h_b, g_ffn, w_router, before, sel, tm=tm_b)
    t_a, t_b = cnt_a.shape[0], cnt_b.shape[0]
    cnt = jnp.concatenate([cnt_a[:, 0, :N_EXPERTS], cnt_b[:, 0, :N_EXPERTS]], axis=0).astype(jnp.int32)
    rows_a = xs_a.shape[0]
    row0 = np.concatenate([np.arange(t_a) * _slots(tm_a), rows_a + np.arange(t_b) * _slots(tm_b)])
    tile_expert, tile_valid, src, dst = _chunk_plan(
        cnt, row0, n_pairs=2 * (h_a.shape[0] + h_b.shape[0]), rows_a=rows_a,
        zero_row=rows_a + _slots(tm_b) - SUBLANES, te=te)
    ys = _experts(xs_a, xs_b, tile_expert, tile_valid, src, w_gate, w_up, w_down, te=te, layer=layer)
    out_a = _combine(h_a, rec_a, dst[:t_a], ys, p_a, g_ple, wgate_bf, wproj_bf, tm=tm_a, layer=layer)
    out_b = _combine(h_b, rec_b, dst[t_a:], ys, p_b, g_ple, wgate_bf, wproj_bf, tm=tm_b, layer=layer)
    return out_a, out_b


def _bucket_np(dist):
    max_exact = N_BUCKETS // 2
    d_f = np.maximum(dist, 1).astype(np.float32)
    large = max_exact + (np.log(d_f / np.float32(max_exact)) / np.float32(np.log(MAX_DISTANCE / max_exact))
                         * np.float32(N_BUCKETS - max_exact)).astype(np.int32)
    large = np.minimum(large, N_BUCKETS - 1)
    return np.where(dist < max_exact, dist, large).astype(np.int32)


def _bias_from_buckets(rel_bias, bucket, valid):
    onehot = (jnp.asarray(bucket)[..., None] == jnp.arange(N_BUCKETS, dtype=jnp.int32)).astype(F32)
    bias = jnp.einsum("...k,kh->h...", onehot, rel_bias.astype(F32), precision=HIGHEST)
    return jnp.where(jnp.asarray(valid)[None], bias, NEG)


def _block_order(dil):
    g = np.arange(Q_BLOCK) // SUBLANES
    j = np.arange(Q_BLOCK) % SUBLANES
    if dil == 1:
        return SLABS * j + g
    if dil == 4:
        return 32 * (g // 4) + 4 * j + g % 4
    return SUBLANES * g + j


def _band_bias(rel_bias, dil):
    mu = _block_order(dil)
    qi = mu[:, None] + Q_BLOCK
    ki = np.concatenate([mu, mu + Q_BLOCK])[None, :]
    off = qi - ki
    valid = (off >= 0) & (off <= N_KEYS)
    bucket = _bucket_np(dil * np.clip(off, 0, N_KEYS))
    first = valid & (np.arange(2 * Q_BLOCK)[None, :] >= Q_BLOCK)
    tables = [_bias_from_buckets(rel_bias, bucket, v).reshape(N_HEADS * Q_BLOCK, 2 * Q_BLOCK) for v in (valid, first)]
    return jnp.stack(tables)


def _sample_tables(rel_bias, w_buf):
    qpos = w_buf + np.arange(SAMPLE_T)[:, None]
    pos = np.arange(w_buf + NEW_COLS)[None, :]
    dist = qpos - pos
    in_seq = (dist >= 0) & (pos < w_buf + SAMPLE_T)
    mult = np.zeros(dist.shape, np.float32)
    for (w, d) in PATTERNS:
        mult += in_seq & (dist % d == 0) & (dist <= w)
    bucket = _bucket_np(np.maximum(dist, 0))
    bias = jnp.transpose(_bias_from_buckets(rel_bias, bucket, mult > 0), (1, 0, 2))
    rows = SAMPLE_T * N_HEADS
    mult_rows = np.broadcast_to(mult[:, None, :], (SAMPLE_T, N_HEADS, mult.shape[-1]))
    return bias.reshape(rows, -1), jnp.asarray(mult_rows.reshape(rows, -1))


PROJ_TM = 512
PERM_ROWS = 256
OUT_TM = 512
PROMPT_TM = 256
PROMPT_TE = 512
ATTN_SUB = 8


def _row_perm(tm):
    a = np.arange(tm)
    src = (a // Q_BLOCK) * Q_BLOCK + SLABS * (a % SUBLANES) + (a % Q_BLOCK) // SUBLANES
    perm = np.zeros((tm, tm), np.float32)
    perm[a, src] = 1.0
    return perm


def kernel(x_prompt, x_sample, cache_k, cache_v, state_conv, p_prompt, p_sample, rel_bias, g_mix, w_in, q_gain,
           k_gain, conv_w, g_out_att, g_out_conv, w_out, g_ffn, w_router_group, w_router_expert, w_gate, w_up,
           w_down, g_ple, w_ple_gate, w_ple_proj):
    depth = w_in.shape[0]
    batch, seq, _ = x_prompt.shape
    dec_b, dec_t, _ = x_sample.shape
    w_buf = cache_k.shape[2]
    n_s = dec_b * dec_t
    keep = min(w_buf, seq)
    assert dec_t == SAMPLE_T and w_buf % LANES == 0
    max_dil = max(d for _, d in PATTERNS)
    assert max_dil == SLABS and all(SLABS % d == 0 and w // d == N_KEYS for w, d in PATTERNS)
    assert seq % (Q_BLOCK * max_dil * 2) == 0 and keep % PROJ_TM == 0 and seq % PROMPT_TM == 0
    cache_kt = jnp.transpose(cache_k, (0, 1, 3, 4, 2))
    cache_vt = jnp.transpose(cache_v, (0, 1, 3, 4, 2))

    row = lambda a: a.reshape(1, -1)
    src_lane = np.arange(LANES)
    expand = jnp.asarray((src_lane[:, None] // LSE_LANES_PER_HEAD == np.arange(ATT_DIM)[None, :] // HEAD_DIM)
                         & (src_lane[:, None] % LSE_LANES_PER_HEAD == 0), BF16)
    before = jnp.asarray(np.arange(LANES)[:, None] < np.arange(LANES)[None, :], BF16)
    sel_np = np.zeros((SUBLANES, LANES), np.float32)
    sel_np[0, REC_S1] = sel_np[1, REC_S2] = 1.0
    sel = jnp.asarray(sel_np, BF16)
    perm = jnp.asarray(_row_perm(PERM_ROWS), BF16)
    unperm = jnp.asarray(_row_perm(OUT_TM).T, BF16)
    band = [_band_bias(rel_bias, d) for (_, d) in PATTERNS]
    s_bias, s_mult = _sample_tables(rel_bias, w_buf)

    hp = x_prompt.reshape(batch * seq, D_MODEL)
    hs = jnp.swapaxes(x_sample, 0, 1).reshape(n_s, D_MODEL)
    pp_all = p_prompt.reshape(depth, batch * seq, D_PLE)
    ps_all = jnp.swapaxes(p_sample, 1, 2).reshape(depth, n_s, D_PLE)
    new = {k: [] for k in ("cp", "ks", "vs", "cs")}
    kv_rows = None
    hist_p = jnp.zeros((batch, SUBLANES, CONV_DIM), F32)

    for l in range(depth):
        w_in_bf = w_in[l].astype(BF16)
        wa_bf = w_out[l, :ATT_DIM].astype(BF16)
        wc_bf = w_out[l, ATT_DIM:].astype(BF16)
        wgate_bf = w_ple_gate[l].astype(BF16)
        wproj_bf = w_ple_proj[l].astype(BF16)
        w_router = jnp.concatenate(
            [w_router_expert[l], w_router_group[l],
             jnp.zeros((D_MODEL, LANES - N_EXPERTS - N_GROUPS), F32)], axis=1).astype(BF16)
        qg, kg = row(jnp.tile(q_gain[l], N_HEADS)), row(jnp.tile(k_gain[l], N_HEADS))
        mix = (row(g_mix[l]), w_in_bf, qg, kg, conv_w[l], row(g_out_conv[l]))
        moe = (row(g_ffn[l]), w_router, before, sel, w_gate, w_up, w_down, row(g_ple[l]), wgate_bf, wproj_bf)

        q, k, v, k_rows, v_rows, yn, nconv = _inproj(
            hp, hist_p, *mix, perm, kv_rows, tm=PROJ_TM, shift=1, tiles_per_seq=seq // PROJ_TM,
            keep_tiles=keep // PROJ_TM)
        kv_rows = (k_rows, v_rows)
        os, lses = [], []
        for bias, (_, d) in zip(band, PATTERNS):
            sub = min(ATTN_SUB, seq // (Q_BLOCK * d))
            o, lse = _attn_pattern(q, k, v, bias, batch=batch, seq=seq, dil=d, sub=sub,
                                   res=min(d, ATTN_SUB // sub))
            os.append(o)
            lses.append(lse)
        hp = _outproj(os, lses, yn, hp, row(g_out_att[l]), wa_bf, wc_bf, expand, unperm, tm=OUT_TM)
        new["cp"].append(nconv[:, SUBLANES - 2:])

        hist_s = jnp.swapaxes(state_conv[l], 0, 1).reshape(1, 2 * dec_b, CONV_DIM)
        q, k, v, yn, nconv = _inproj(hs, hist_s, *mix, tm=n_s, shift=dec_b, tiles_per_seq=1)
        bmaj = lambda a: jnp.swapaxes(a.reshape(dec_t, dec_b, N_HEADS, HEAD_DIM), 0, 1)
        qb, kb, vb = bmaj(q), bmaj(k), bmaj(v)
        rows8 = lambda a: jnp.pad(a.reshape(dec_b, dec_t, ATT_DIM), ((0, 0), (0, SUBLANES - dec_t), (0, 0)))
        att = _attn_sample(qb.reshape(dec_b, dec_t, ATT_DIM), rows8(kb), rows8(vb), cache_kt, cache_vt, l,
                           s_bias, s_mult)
        att_tm = jnp.swapaxes(att, 0, 1).reshape(n_s, ATT_DIM)
        hs = _outproj([att_tm], [], yn, hs, row(g_out_att[l]), wa_bf, wc_bf, tm=n_s)
        hp, hs = _moe_ple(hp, hs, pp_all, ps_all, *moe, tm_a=PROMPT_TM, tm_b=n_s, te=PROMPT_TE, layer=l)
        new["ks"].append(kb)
        new["vs"].append(vb)
        new["cs"].append(jnp.swapaxes(nconv.reshape(2, dec_b, CONV_DIM), 0, 1))

    y_prompt = hp.reshape(batch, seq, D_MODEL)
    y_sample = jnp.swapaxes(hs.reshape(dec_t, dec_b, D_MODEL), 0, 1)
    st = lambda key: jnp.stack(new[key])
    natural = lambda rows: jnp.transpose(rows, (0, 1, 4, 2, 3))
    return (y_prompt, y_sample, natural(kv_rows[0]), natural(kv_rows[1]), st("cp"), st("ks"), st("vs"), st("cs"))
```
